```python
import math, functools
import jax, jax.numpy as jnp
from jax import lax
import numpy as np

D_MODEL = 1024
BATCH = 32
SEQ = 256
DEPTH = 2
DEC_BATCH = 2
DEC_SEQ = 4096
PAST_LEN = 256

GRID_W = 64
N_AB = (DEPTH + 1) // 2
N_NA = DEPTH // 2
A_DH = 64
A_HEADS = D_MODEL // (2 * A_DH)
A_WIDTH = A_HEADS * A_DH
CONV_W = 3
CHUNK = 64
B_WIDTH = D_MODEL - A_WIDTH
B_GROUPS = 4
B_GROUP_CH = B_WIDTH // B_GROUPS
AB_IN = 4 * A_WIDTH + 4 * A_HEADS + B_WIDTH
NA_DH = 64
NA_HEADS = D_MODEL // NA_DH
WIN_R = 8
WIN_C = 16
QBLK = 128
N_GROUPS = 4
EXP_PER_GROUP = 8
N_EXPERTS = N_GROUPS * EXP_PER_GROUP
TOP_K = 2
D_EXPERT = D_MODEL // 2
MOE_BLK = 128
EPS = 1e-6

kernel_name = 'hybrid_deltanet_fnet_natten_hmoe_step'


def rmsnorm(x, gain):
    x32 = x.astype(jnp.float32)
    y = x32 * lax.rsqrt(jnp.mean(x32 * x32, axis=-1, keepdims=True) + EPS)
    return (y * gain.astype(jnp.float32)).astype(x.dtype)


def ada_params(cond, w_mod, b_mod):
    m = jax.nn.silu(cond) @ w_mod + b_mod
    return jnp.split(m[..., None, :], 6, axis=-1)


def modulate(x, gain, shift, scale):
    return rmsnorm(x, gain) * (1.0 + scale) + shift


def l2norm(x):
    x32 = x.astype(jnp.float32)
    return x32 * lax.rsqrt(jnp.sum(x32 * x32, axis=-1, keepdims=True) + EPS)


def short_conv(x, w):
    return lax.conv_general_dilated(
        x, w[:, None, :].astype(x.dtype), window_strides=(1,),
        padding=[(CONV_W // 2, CONV_W // 2)],
        dimension_numbers=('NWC', 'WIO', 'NWC'),
        feature_group_count=x.shape[-1])


def gated_delta_chunked(q, k, v, g, beta, s0):
    f32 = jnp.float32
    B, T, H, dk = q.shape
    nc = T // CHUNK

    def chunks(t):
        t = t.astype(f32).reshape((B, nc, CHUNK, H) + t.shape[3:])
        return jnp.moveaxis(t, 3, 1)

    q = chunks(q) * dk ** -0.5
    k, v, g, beta = chunks(k), chunks(v), chunks(g), chunks(beta)
    gc = jnp.cumsum(g, axis=-1)
    idx = jnp.arange(CHUNK)
    lower = idx[:, None] >= idx[None, :]
    strict = idx[:, None] > idx[None, :]
    diff = jnp.where(lower, gc[..., :, None] - gc[..., None, :], 0.0)
    decay = jnp.where(lower, jnp.exp(diff), 0.0)
    k_beta = k * beta[..., None]
    a_mat = jnp.where(strict, jnp.einsum('bhnid,bhnjd->bhnij', k_beta, k) * decay, 0.0)
    t_mat = a_mat + jnp.eye(CHUNK, dtype=f32)
    solve = functools.partial(lax.linalg.triangular_solve, left_side=True, lower=True, unit_diagonal=True)
    u = solve(t_mat, v * beta[..., None])
    w = solve(t_mat, k_beta * jnp.exp(gc)[..., None])
    attn = jnp.einsum('bhnid,bhnjd->bhnij', q, k) * decay
    q_dec = q * jnp.exp(gc)[..., None]
    k_dec = k * jnp.exp(gc[..., -1:] - gc)[..., None]
    g_end = jnp.exp(gc[..., -1])

    def step(s, xs):
        q_i, k_i, u_i, w_i, a_i, ge_i = xs
        v_new = u_i - jnp.einsum('bhcd,bhde->bhce', w_i, s)
        o_i = jnp.einsum('bhcd,bhde->bhce', q_i, s) + jnp.einsum('bhij,bhje->bhie', a_i, v_new)
        s = s * ge_i[..., None, None] + jnp.einsum('bhcd,bhce->bhde', k_i, v_new)
        return s, o_i

    xs = tuple(jnp.moveaxis(t, 2, 0) for t in (q_dec, k_dec, u, w, attn, g_end))
    s_end, o = lax.scan(step, s0.astype(f32), xs)
    o = jnp.moveaxis(o, 0, 2).reshape(B, H, T, -1).transpose(0, 2, 1, 3)
    return o, s_end


def deltanet_fnet_mix(h, w_in, conv_w, a_log, dt_bias, o_gain, w_out, s_fwd0, s_bwd0):
    f32 = jnp.float32
    B, T, _ = h.shape
    proj = h @ w_in
    qkv, z, a, b, u = jnp.split(
        proj, [3 * A_WIDTH, 4 * A_WIDTH, 4 * A_WIDTH + 2 * A_HEADS, 4 * A_WIDTH + 4 * A_HEADS], axis=-1)
    qkv = jax.nn.silu(short_conv(qkv, conv_w))
    q, k, v = [t.reshape(B, T, A_HEADS, A_DH) for t in jnp.split(qkv, 3, axis=-1)]
    q, k = l2norm(q), l2norm(k)
    g = -jnp.exp(a_log.astype(f32)) * jax.nn.softplus(a.reshape(B, T, 2, A_HEADS).astype(f32) + dt_bias.astype(f32))
    beta = jax.nn.sigmoid(b.reshape(B, T, 2, A_HEADS).astype(f32))
    o_f, s_f = gated_delta_chunked(q, k, v, g[:, :, 0], beta[:, :, 0], s_fwd0)
    flip = lambda t: t[:, ::-1]
    o_b, s_b = gated_delta_chunked(flip(q), flip(k), flip(v), flip(g[:, :, 1]), flip(beta[:, :, 1]), s_bwd0)
    o = o_f + flip(o_b)
    o = rmsnorm(o, o_gain) * jax.nn.silu(z.reshape(B, T, A_HEADS, A_DH).astype(f32))
    mix_a = o.reshape(B, T, A_WIDTH)
    ug = u.reshape(B, T, B_GROUPS, B_GROUP_CH).astype(f32)
    mix_b = jnp.fft.fft2(ug, axes=(1, 3), norm='ortho').real.reshape(B, T, B_WIDTH)
    out = jnp.concatenate([mix_a, mix_b], axis=-1).astype(h.dtype) @ w_out
    return out, s_f, s_b


def na_context(h, w_qkv, w_out):
    B, L, _ = h.shape
    q, k, v = [t.reshape(B, L, NA_HEADS, NA_DH) for t in jnp.split(h @ w_qkv, 3, axis=-1)]
    scale = NA_DH ** -0.5
    q_blocks = jnp.moveaxis(q.reshape(B, L // QBLK, QBLK, NA_HEADS, NA_DH), 1, 0)

    def block(q_b):
        s = jnp.einsum('bqhd,bkhd->bhqk', q_b, k).astype(jnp.float32) * scale
        p = jax.nn.softmax(s, axis=-1).astype(v.dtype)
        return jnp.einsum('bhqk,bkhd->bqhd', p, v)

    o = jnp.moveaxis(lax.map(block, q_blocks), 0, 1).reshape(B, L, D_MODEL)
    return o @ w_out, k, v


def na_latent(h, w_qkv, rpb, w_out, k_ctx, v_ctx):
    B, T, _ = h.shape
    rows = T // GRID_W
    kr = min(WIN_R, rows)
    q, k, v = [t.reshape(B, rows, GRID_W, NA_HEADS, NA_DH) for t in jnp.split(h @ w_qkv, 3, axis=-1)]
    k_ctx = k_ctx.astype(k.dtype)
    v_ctx = v_ctx.astype(v.dtype)
    scale = NA_DH ** -0.5
    col = jnp.arange(GRID_W)
    col_start = jnp.clip(col - WIN_C // 2, 0, GRID_W - WIN_C)
    col_idx = col_start[:, None] + jnp.arange(WIN_C)[None, :]
    dc_idx = col_idx - col[:, None] + (WIN_C - 1)
    n_loc = kr * WIN_C

    def row_block(r):
        r0 = jnp.clip(r - kr // 2, 0, rows - kr)
        q_r = lax.dynamic_index_in_dim(q, r, axis=1, keepdims=False)
        k_rows = lax.dynamic_slice_in_dim(k, r0, kr, axis=1)
        v_rows = lax.dynamic_slice_in_dim(v, r0, kr, axis=1)
        k_win = k_rows[:, :, col_idx]
        v_win = v_rows[:, :, col_idx]
        dr_idx = r0 + jnp.arange(kr) - r + (WIN_R - 1)
        bias = rpb[:, dr_idx[None, :, None], dc_idx[:, None, :]]
        s_loc = jnp.einsum('bqhd,brqkhd->bhqrk', q_r, k_win).astype(jnp.float32) * scale \
            + bias.astype(jnp.float32)[None]
        s_ctx = jnp.einsum('bqhd,bchd->bhqc', q_r, k_ctx).astype(jnp.float32) * scale
        s = jnp.concatenate([s_loc.reshape(B, NA_HEADS, GRID_W, n_loc), s_ctx], axis=-1)
        p = jax.nn.softmax(s, axis=-1).astype(v.dtype)
        p_loc = p[..., :n_loc].reshape(B, NA_HEADS, GRID_W, kr, WIN_C)
        p_ctx = p[..., n_loc:]
        return jnp.einsum('bhqrk,brqkhd->bqhd', p_loc, v_win) + jnp.einsum('bhqc,bchd->bqhd', p_ctx, v_ctx)

    o = lax.map(row_block, jnp.arange(rows))
    o = jnp.moveaxis(o, 0, 1).reshape(B, T, D_MODEL)
    return o @ w_out


def routed_experts(xf, e_idx, e_w, w_gate, w_up, w_down):
    n, d = xf.shape
    n_assign = n * TOP_K
    flat_e = e_idx.reshape(-1)
    order = jnp.argsort(flat_e)
    se = flat_e[order]
    stok = (order // TOP_K).astype(jnp.int32)
    sw = e_w.reshape(-1)[order]
    counts = jnp.bincount(flat_e, length=N_EXPERTS)
    padded = (counts + MOE_BLK - 1) // MOE_BLK * MOE_BLK
    end_pad = jnp.cumsum(padded)
    start_pad = end_pad - padded
    start_sorted = jnp.cumsum(counts) - counts
    dest = start_pad[se] + jnp.arange(n_assign) - start_sorted[se]
    n_blocks = -(-n_assign // MOE_BLK) + N_EXPERTS
    slot_tok = jnp.zeros((n_blocks * MOE_BLK,), jnp.int32).at[dest].set(stok)
    slot_w = jnp.zeros((n_blocks * MOE_BLK,), xf.dtype).at[dest].set(sw.astype(xf.dtype))
    block_e = jnp.minimum(jnp.searchsorted(end_pad, jnp.arange(n_blocks) * MOE_BLK, side='right'),
                          N_EXPERTS - 1)

    def expert_block(args):
        tok, e = args
        xb = xf[tok]
        hb = jax.nn.silu(xb @ w_gate[e]) * (xb @ w_up[e])
        return hb @ w_down[e]

    out = lax.map(expert_block, (slot_tok.reshape(n_blocks, MOE_BLK), block_e))
    return jnp.zeros_like(xf).at[slot_tok].add(out.reshape(-1, d) * slot_w[:, None])


def hier_moe(h, w_rg, b_rg, w_re, b_re, w_gate, w_up, w_down):
    B, T, D = h.shape
    xf = h.reshape(-1, D)
    n = xf.shape[0]
    pg = jax.nn.softmax((xf @ w_rg).astype(jnp.float32) + b_rg.astype(jnp.float32), axis=-1)
    p_grp, grp = lax.top_k(pg, 1)
    le = jnp.einsum('nd,gde->nge', xf, w_re).astype(jnp.float32) + b_re.astype(jnp.float32)
    le = le[jnp.arange(n), grp[:, 0]]
    w_top, i_top = lax.top_k(jax.nn.softmax(le, axis=-1), TOP_K)
    w_top = p_grp * w_top / jnp.sum(w_top, axis=-1, keepdims=True)
    e_idx = grp * EXP_PER_GROUP + i_top
    y = routed_experts(xf, e_idx, w_top, w_gate, w_up, w_down)
    return y.reshape(B, T, D)


def setup_inputs(seed: int = 0) -> dict:
    key = jax.random.key(seed)
    keys = iter(jax.random.split(key, 40))

    def nrm(shape, scale):
        return jax.random.normal(next(keys), shape, jnp.float32) * scale

    def gain(shape):
        return 1.0 + nrm(shape, 0.05)

    D = D_MODEL
    dt = jnp.exp(jax.random.uniform(next(keys), (N_AB, 2, A_HEADS), jnp.float32,
                                    math.log(1e-3), math.log(1e-1)))
    a_log = jnp.log(jax.random.uniform(next(keys), (N_AB, 2, A_HEADS), jnp.float32, 1.0, 16.0))
    return {
        'x_prompt': nrm((BATCH, SEQ, D), 1.0),
        'x_sample': nrm((DEC_BATCH, DEC_SEQ, D), 1.0),
        'state_A_fwd': nrm((DEC_BATCH, N_AB, A_HEADS, A_DH, A_DH), 0.3),
        'state_A_bwd': nrm((DEC_BATCH, N_AB, A_HEADS, A_DH, A_DH), 0.3),
        'cache_k': nrm((DEC_BATCH, N_NA, PAST_LEN, NA_HEADS, NA_DH), 1.0),
        'cache_v': nrm((DEC_BATCH, N_NA, PAST_LEN, NA_HEADS, NA_DH), 1.0),
        'c': nrm((DEC_BATCH, D), 1.0),
        'c_ctx': nrm((D,), 1.0),
        'mod_w': nrm((DEPTH, D, 6 * D), 0.5 * D ** -0.5),
        'mod_b': nrm((DEPTH, 6 * D), 0.02),
        'norm_mix': gain((DEPTH, D)),
        'norm_ffn': gain((DEPTH, D)),
        'ab_w_in': nrm((N_AB, D, AB_IN), D ** -0.5),
        'ab_conv': nrm((N_AB, CONV_W, 3 * A_WIDTH), CONV_W ** -0.5),
        'ab_a_log': a_log,
        'ab_dt_bias': dt + jnp.log(-jnp.expm1(-dt)),
        'ab_o_gain': gain((N_AB, A_DH)),
        'ab_w_out': nrm((N_AB, D, D), D ** -0.5),
        'na_w_qkv': nrm((N_NA, D, 3 * D), D ** -0.5),
        'na_rpb': nrm((N_NA, NA_HEADS, 2 * WIN_R - 1, 2 * WIN_C - 1), 0.1),
        'na_w_out': nrm((N_NA, D, D), D ** -0.5),
        'moe_w_rg': nrm((DEPTH, D, N_GROUPS), D ** -0.5),
        'moe_b_rg': nrm((DEPTH, N_GROUPS), 0.01),
        'moe_w_re': nrm((DEPTH, N_GROUPS, D, EXP_PER_GROUP), D ** -0.5),
        'moe_b_re': nrm((DEPTH, N_GROUPS, EXP_PER_GROUP), 0.01),
        'moe_w_gate': nrm((DEPTH, N_EXPERTS, D, D_EXPERT), D ** -0.5),
        'moe_w_up': nrm((DEPTH, N_EXPERTS, D, D_EXPERT), D ** -0.5),
        'moe_w_down': nrm((DEPTH, N_EXPERTS, D_EXPERT, D), D_EXPERT ** -0.5),
        'final_norm': gain((D,)),
    }


def reference(x_prompt, x_sample, state_A_fwd, state_A_bwd, cache_k, cache_v, c, c_ctx,
              mod_w, mod_b, norm_mix, norm_ffn,
              ab_w_in, ab_conv, ab_a_log, ab_dt_bias, ab_o_gain, ab_w_out,
              na_w_qkv, na_rpb, na_w_out,
              moe_w_rg, moe_b_rg, moe_w_re, moe_b_re, moe_w_gate, moe_w_up, moe_w_down,
              final_norm):
    xc, xl = x_prompt, x_sample
    cond_ctx = c_ctx[None, :]
    new_fwd, new_bwd, new_k, new_v = [], [], [], []
    for i in range(DEPTH):
        j = i // 2
        sh1c, sc1c, g1c, sh2c, sc2c, g2c = ada_params(cond_ctx, mod_w[i], mod_b[i])
        sh1l, sc1l, g1l, sh2l, sc2l, g2l = ada_params(c, mod_w[i], mod_b[i])
        hc = modulate(xc, norm_mix[i], sh1c, sc1c)
        hl = modulate(xl, norm_mix[i], sh1l, sc1l)
        if i % 2 == 0:
            s_zero = jnp.zeros((xc.shape[0], A_HEADS, A_DH, A_DH), jnp.float32)
            ab = (ab_w_in[j], ab_conv[j], ab_a_log[j], ab_dt_bias[j], ab_o_gain[j], ab_w_out[j])
            oc, s_f, s_b = deltanet_fnet_mix(hc, *ab, s_zero, s_zero)
            ol, _, _ = deltanet_fnet_mix(hl, *ab, state_A_fwd[:, j], state_A_bwd[:, j])
            new_fwd.append(s_f)
            new_bwd.append(s_b)
        else:
            oc, k_c, v_c = na_context(hc, na_w_qkv[j], na_w_out[j])
            ol = na_latent(hl, na_w_qkv[j], na_rpb[j], na_w_out[j], cache_k[:, j], cache_v[:, j])
            new_k.append(k_c)
            new_v.append(v_c)
        xc = xc + g1c * oc
        xl = xl + g1l * ol
        moe = (moe_w_rg[i], moe_b_rg[i], moe_w_re[i], moe_b_re[i], moe_w_gate[i], moe_w_up[i], moe_w_down[i])
        xc = xc + g2c * hier_moe(modulate(xc, norm_ffn[i], sh2c, sc2c), *moe)
        xl = xl + g2l * hier_moe(modulate(xl, norm_ffn[i], sh2l, sc2l), *moe)
    y_prompt = rmsnorm(xc, final_norm)
    y_sample = rmsnorm(xl, final_norm)
    return (y_prompt, y_sample, jnp.stack(new_fwd, axis=1), jnp.stack(new_bwd, axis=1),
            jnp.stack(new_k, axis=1), jnp.stack(new_v, axis=1))
```

```python
import functools
import math

import numpy as np
import jax
import jax.numpy as jnp
from jax import lax
from jax.experimental import pallas as pl
from jax.experimental.pallas import tpu as pltpu

F32 = jnp.float32
BF16 = jnp.bfloat16
HIGHEST = lax.Precision.HIGHEST

D = 1024
BATCH, SEQ = 32, 256
DEC_BATCH, DEC_SEQ = 2, 4096
NC = BATCH * SEQ
NL = DEC_BATCH * DEC_SEQ
N = NC + NL
DEPTH = 2
GRID_W = 64
A_DH = 64
A_HEADS = 8
A_WIDTH = 512
CHUNK = 64
B_WIDTH = 512
B_GROUPS = 4
NA_DH = 64
NA_HEADS = 16
WIN_R, WIN_C = 8, 16
N_GROUPS, EXP_PER_GROUP, N_EXPERTS = 4, 8, 32
D_EXPERT = 512
EPS = 1e-6

LANES = 128
TM = 256
MOE_ROWS = 256
N_SLOT_BLOCKS = (2 * N) // MOE_ROWS + N_EXPERTS
AB_COLS = 2688
VMEM_LIMIT = 56 * 1024 * 1024
NEG = -1e30


def _cparams(sem):
    return pltpu.CompilerParams(dimension_semantics=sem, vmem_limit_bytes=VMEM_LIMIT)


def _mm(a, b):
    return jnp.dot(a.astype(BF16), b.astype(BF16), preferred_element_type=F32)


def _mm_nt(a, b):
    return lax.dot_general(a.astype(BF16), b.astype(BF16), (((1,), (1,)), ((), ())),
                           preferred_element_type=F32)


SOLVE_BLK = 16


def _unit_lower_solve(a_mat, rhs, same_blk, eye):
    dg = jnp.where(same_blk, a_mat, 0.0)
    p = -dg
    dinv = eye + p
    for _ in range(int(math.log2(SOLVE_BLK)) - 1):
        p = _mm(p, p)
        dinv = dinv + _mm(p, dinv)
    mp = -_mm(dinv, a_mat - dg)
    y = _mm(dinv, rhs)
    y = y + _mm(mp, y)
    for _ in range(int(math.log2(a_mat.shape[0] // SOLVE_BLK)) - 1):
        mp = _mm(mp, mp)
        y = y + _mm(mp, y)
    return y


def _mm_hi(a, b):
    return jnp.dot(a, b, preferred_element_type=F32, precision=HIGHEST)


def _silu(x):
    return x * jax.nn.sigmoid(x)


def _cond_row(i):
    return jnp.where(i < NC // TM, 0, 1 + (i - NC // TM) // (DEC_SEQ // TM))


def _modulated_norm(x, m_ref, g_ref, shift_idx, scale_idx):
    ms = jnp.mean(x * x, axis=-1, keepdims=True)
    y = x * lax.rsqrt(ms + EPS) * g_ref[...]
    return y * (1.0 + m_ref[scale_idx:scale_idx + 1, :]) + m_ref[shift_idx:shift_idx + 1, :]


def _ada_kernel(cond_ref, w_ref, b_ref, o_ref):
    o_ref[...] = _mm_hi(_silu(cond_ref[...]), w_ref[...]) + b_ref[...]


def _ada_params(cond8, mod_w, mod_b):
    tn = 1536
    return pl.pallas_call(
        _ada_kernel,
        grid=(DEPTH, 6 * D // tn),
        in_specs=[pl.BlockSpec((8, D), lambda l, j: (0, 0)),
                  pl.BlockSpec((None, D, tn), lambda l, j: (l, 0, j)),
                  pl.BlockSpec((None, 1, tn), lambda l, j: (l, 0, j))],
        out_specs=pl.BlockSpec((None, 8, tn), lambda l, j: (l, 0, j)),
        out_shape=jax.ShapeDtypeStruct((DEPTH, 8, 6 * D), F32),
        compiler_params=_cparams(("arbitrary", "arbitrary")),
        name="ada_params",
    )(cond8, mod_w, mod_b.reshape(DEPTH, 1, 6 * D))


def _modproj_kernel(x_ref, m_ref, g_ref, w_ref, o_ref, *, n_chunk):
    hb = _modulated_norm(x_ref[...], m_ref, g_ref, 0, 1).astype(BF16)
    for j in range(o_ref.shape[1] // n_chunk):
        sl = slice(j * n_chunk, (j + 1) * n_chunk)
        o_ref[:, sl] = jnp.dot(hb, w_ref[:, sl], preferred_element_type=F32)


def _modproj(x, m3, gain, w_bf16, n_chunk):
    nout = w_bf16.shape[1]
    return pl.pallas_call(
        functools.partial(_modproj_kernel, n_chunk=n_chunk),
        grid=(N // TM,),
        in_specs=[pl.BlockSpec((TM, D), lambda i: (i, 0)),
                  pl.BlockSpec((None, 6, D), lambda i: (_cond_row(i), 0, 0)),
                  pl.BlockSpec((1, D), lambda i: (0, 0)),
                  pl.BlockSpec((D, nout), lambda i: (0, 0))],
        out_specs=pl.BlockSpec((TM, nout), lambda i: (i, 0)),
        out_shape=jax.ShapeDtypeStruct((N, nout), F32),
        compiler_params=_cparams(("arbitrary",)),
        name="modproj",
    )(x, m3, gain, w_bf16)


def _outproj_kernel(*refs, n_in):
    a_refs = refs[:2 * n_in]
    w_refs = refs[2 * n_in:3 * n_in]
    x_ref, m_ref, o_ref = refs[3 * n_in:]
    is_ctx = pl.program_id(0) < NC // TM
    acc = None
    for j, w_ref in enumerate(w_refs):
        a = jnp.where(is_ctx, a_refs[2 * j][...], a_refs[2 * j + 1][...])
        part = _mm(a, w_ref[...])
        acc = part if acc is None else acc + part
    o_ref[...] = x_ref[...] + m_ref[2:3, :] * acc


def _outproj(a_pairs, w_list, x, m3):
    n_in = len(a_pairs)
    nct = NC // TM
    in_specs, args = [], []
    for a_ctx, a_lat in a_pairs:
        in_specs.append(pl.BlockSpec((TM, a_ctx.shape[1]), lambda i: (jnp.minimum(i, nct - 1), 0)))
        in_specs.append(pl.BlockSpec((TM, a_lat.shape[1]), lambda i: (jnp.maximum(i - nct, 0), 0)))
        args += [a_ctx, a_lat]
    in_specs += ([pl.BlockSpec(w.shape, lambda i: (0, 0)) for w in w_list]
                 + [pl.BlockSpec((TM, D), lambda i: (i, 0)),
                    pl.BlockSpec((None, 6, D), lambda i: (_cond_row(i), 0, 0))])
    return pl.pallas_call(
        functools.partial(_outproj_kernel, n_in=n_in),
        grid=(N // TM,),
        in_specs=in_specs,
        out_specs=pl.BlockSpec((TM, D), lambda i: (i, 0)),
        out_shape=jax.ShapeDtypeStruct((N, D), F32),
        compiler_params=_cparams(("arbitrary",)),
        name="outproj",
    )(*args, *w_list, x, m3)


def _deltanet_kernel(q_ref, k_ref, v_ref, z_ref, ab_ref, cq_ref, ck_ref, cv_ref, gp_ref, og_ref,
                     s0f_ref, s0b_ref, tri_ref, jbd_ref, o_ref, sf_ref, sb_ref, qs, ks, vs, ob, *, seq_len):
    hp = pl.program_id(1)
    nc = seq_len // CHUNK
    C = CHUNK
    lane = lax.broadcasted_iota(jnp.int32, (C, LANES), 1)
    row = lax.broadcasted_iota(jnp.int32, (C, LANES), 0)
    first_head = lane < A_DH
    ri = lax.broadcasted_iota(jnp.int32, (C, C), 0)
    ci = lax.broadcasted_iota(jnp.int32, (C, C), 1)
    same_blk = (ri // SOLVE_BLK) == (ci // SOLVE_BLK)
    eye = jnp.where(ri == ci, 1.0, 0.0)
    jbd = jbd_ref[...]
    lincl = tri_ref[...]

    def conv_silu(ref, w_ref, c):
        base = pl.multiple_of(c * C, C)
        xc = ref[pl.ds(base, C), :]
        pbase = pl.multiple_of(jnp.maximum(base - 8, 0), 8)
        nbase = pl.multiple_of(jnp.minimum(base + C, seq_len - 8), 8)
        prev_row = ref[pl.ds(pbase, 8), :][7:8, :] * jnp.where(c > 0, 1.0, 0.0)
        next_row = ref[pl.ds(nbase, 8), :][0:1, :] * jnp.where(c < nc - 1, 1.0, 0.0)
        x_prev = jnp.where(row == 0, prev_row, pltpu.roll(xc, 1, 0))
        x_next = jnp.where(row == C - 1, next_row, pltpu.roll(xc, C - 1, 0))
        y = w_ref[0:1, :] * x_prev + w_ref[1:2, :] * xc + w_ref[2:3, :] * x_next
        return _silu(y)

    def prep(c, carry):
        base = pl.multiple_of(c * C, C)
        q = conv_silu(q_ref, cq_ref, c)
        k = conv_silu(k_ref, ck_ref, c)
        v = conv_silu(v_ref, cv_ref, c)
        q = q * lax.rsqrt(_mm_hi(q * q, jbd) + EPS) * (A_DH ** -0.5)
        k = k * lax.rsqrt(_mm_hi(k * k, jbd) + EPS)
        qs[pl.ds(base, C), :] = q
        ks[pl.ds(base, C), :] = k
        vs[pl.ds(base, C), :] = v
        return carry

    lax.fori_loop(0, nc, prep, 0)

    neg_a = -jnp.exp(gp_ref[0:1, :])
    dt_b = gp_ref[1:2, :]

    def block_diag(s_ref):
        z = jnp.zeros((A_DH, A_DH), F32)
        return jnp.concatenate([jnp.concatenate([s_ref[0], z], axis=1),
                                jnp.concatenate([z, s_ref[1]], axis=1)], axis=0)

    bd_mask = (lax.broadcasted_iota(jnp.int32, (LANES, LANES), 0) < A_DH) == \
              (lax.broadcasted_iota(jnp.int32, (LANES, LANES), 1) < A_DH)

    def lane_col(arr, idx):
        return jnp.sum(jnp.where(lane == idx, arr, 0.0), axis=-1, keepdims=True)

    def chunk_dir(c, s_bd, d):
        base = pl.multiple_of(c * C, C)
        q = qs[pl.ds(base, C), :]
        k = ks[pl.ds(base, C), :]
        v = vs[pl.ds(base, C), :]
        ab = ab_ref[pl.ds(base, C), :]
        g_all = neg_a * jax.nn.softplus(ab + dt_b)
        beta_all = jax.nn.sigmoid(ab)
        tri = lincl if d == 0 else lincl.T
        gc_all = _mm_hi(tri, g_all)
        gc_all_t = gc_all.T
        sub = lax.broadcasted_iota(jnp.int32, (LANES, C), 0)
        incl = (ri >= ci) if d == 0 else (ri <= ci)
        strict = (ri > ci) if d == 0 else (ri < ci)
        u_parts, w_parts, attn_parts, egc, ekd, gend = [], [], [], [], [], []
        for a in range(2):
            hd = 2 * hp + a
            hmask = first_head if a == 0 else jnp.logical_not(first_head)
            g_lane = d * A_HEADS + hd
            gc_col = lane_col(gc_all, g_lane)
            gc_row = jnp.sum(jnp.where(sub == g_lane, gc_all_t, 0.0), axis=0, keepdims=True)
            beta_col = lane_col(beta_all, 2 * A_HEADS + g_lane)
            g_tot = jnp.sum(lane_col(g_all, g_lane), axis=0, keepdims=True)
            decay = jnp.where(incl, jnp.exp(jnp.where(incl, gc_col - gc_row, 0.0)), 0.0)
            km = jnp.where(hmask, k, 0.0)
            qm = jnp.where(hmask, q, 0.0)
            kk = _mm_nt(km, k)
            qk = _mm_nt(qm, k)
            e_gc = jnp.exp(gc_col)
            a_mat = jnp.where(strict, beta_col * kk * decay, 0.0)
            r = jnp.concatenate([v * beta_col, k * (beta_col * e_gc)], axis=1)
            r = _unit_lower_solve(a_mat, r, same_blk, eye)
            u_parts.append(r[:, :LANES])
            w_parts.append(r[:, LANES:])
            attn_parts.append(jnp.where(incl, qk * decay, 0.0))
            egc.append(e_gc)
            ekd.append(jnp.exp(g_tot - gc_col))
            gend.append(jnp.exp(g_tot))
        pick = lambda x0, x1: jnp.where(first_head, x0, x1)
        u = pick(u_parts[0], u_parts[1])
        w = pick(w_parts[0], w_parts[1])
        q_dec = q * pick(egc[0], egc[1])
        k_dec = k * pick(ekd[0], ekd[1])
        ws = _mm(jnp.concatenate([w, q_dec], axis=0), s_bd)
        v_new = u - ws[:C]
        o = ws[C:] + pick(_mm(attn_parts[0], v_new), _mm(attn_parts[1], v_new))
        g_end = jnp.where(lax.broadcasted_iota(jnp.int32, (1, LANES), 1) < A_DH, gend[0], gend[1])
        s_new = s_bd * g_end + jnp.where(bd_mask, _mm(k_dec.T, v_new), 0.0)
        return o, s_new

    def step(i, carry):
        s_f, s_b = carry
        cf = i
        cb = nc - 1 - i
        o_f, s_f = chunk_dir(cf, s_f, 0)
        o_b, s_b = chunk_dir(cb, s_b, 1)
        o_ref[pl.ds(pl.multiple_of(cf * C, C), C), :] = o_f
        ob[pl.ds(pl.multiple_of(cb * C, C), C), :] = o_b
        return s_f, s_b

    s_f, s_b = lax.fori_loop(0, nc, step, (block_diag(s0f_ref), block_diag(s0b_ref)))
    sf_ref[0] = s_f[:A_DH, :A_DH]
    sf_ref[1] = s_f[A_DH:, A_DH:]
    sb_ref[0] = s_b[:A_DH, :A_DH]
    sb_ref[1] = s_b[A_DH:, A_DH:]

    def finish(c, carry):
        base = pl.multiple_of(c * C, C)
        o = o_ref[pl.ds(base, C), :] + ob[pl.ds(base, C), :]
        ms = _mm_hi(o * o, jbd) * (1.0 / A_DH)
        y = o * lax.rsqrt(ms + EPS) * og_ref[...]
        o_ref[pl.ds(base, C), :] = y * _silu(z_ref[pl.ds(base, C), :])
        return carry

    lax.fori_loop(0, nc, finish, 0)


def _deltanet(proj, conv_w, gate_p, o_gain2, s0f, s0b, tri, jbd, *, seq_len, n_seq, row_blk0):
    rb = lambda b: row_blk0 + b
    col = lambda off: (lambda b, hp: (rb(b), off + hp))
    in_specs = [pl.BlockSpec((seq_len, LANES), col(0)),
                pl.BlockSpec((seq_len, LANES), col(4)),
                pl.BlockSpec((seq_len, LANES), col(8)),
                pl.BlockSpec((seq_len, LANES), col(12)),
                pl.BlockSpec((seq_len, LANES), lambda b, hp: (rb(b), 20)),
                pl.BlockSpec((3, LANES), lambda b, hp: (0, hp)),
                pl.BlockSpec((3, LANES), lambda b, hp: (0, 4 + hp)),
                pl.BlockSpec((3, LANES), lambda b, hp: (0, 8 + hp)),
                pl.BlockSpec((8, LANES), lambda b, hp: (0, 0)),
                pl.BlockSpec((1, LANES), lambda b, hp: (0, 0)),
                pl.BlockSpec((None, 2, A_DH, A_DH), lambda b, hp: (b, hp, 0, 0)),
                pl.BlockSpec((None, 2, A_DH, A_DH), lambda b, hp: (b, hp, 0, 0)),
                pl.BlockSpec((CHUNK, CHUNK), lambda b, hp: (0, 0)),
                pl.BlockSpec((LANES, LANES), lambda b, hp: (0, 0))]
    args = [proj, proj, proj, proj, proj, conv_w, conv_w, conv_w, gate_p, o_gain2, s0f, s0b, tri, jbd]
    st_shape = jax.ShapeDtypeStruct((n_seq, A_HEADS, A_DH, A_DH), F32)
    return pl.pallas_call(
        functools.partial(_deltanet_kernel, seq_len=seq_len),
        grid=(n_seq, A_HEADS // 2),
        in_specs=in_specs,
        out_specs=[pl.BlockSpec((seq_len, LANES), lambda b, hp: (b, hp)),
                   pl.BlockSpec((None, 2, A_DH, A_DH), lambda b, hp: (b, hp, 0, 0)),
                   pl.BlockSpec((None, 2, A_DH, A_DH), lambda b, hp: (b, hp, 0, 0))],
        out_shape=[jax.ShapeDtypeStruct((n_seq * seq_len, A_WIDTH), F32), st_shape, st_shape],
        scratch_shapes=[pltpu.VMEM((seq_len, LANES), F32)] * 4,
        compiler_params=_cparams(("arbitrary", "arbitrary")),
        name="deltanet",
    )(*args)


def _dft_mats(n):
    idx = np.arange(n)
    ang = 2.0 * np.pi * ((idx[:, None] * idx[None, :]) % n) / n
    return np.cos(ang), np.sin(ang)


def _fnet_ctx_kernel(u_ref, cs_ref, dft_ref, o_ref):
    norm = 1.0 / math.sqrt(SEQ * LANES)
    for g in range(B_GROUPS):
        sl = slice(g * LANES, (g + 1) * LANES)
        p = _mm(u_ref[:, sl], cs_ref[...])
        stack = jnp.concatenate([p[:, :LANES], p[:, LANES:]], axis=0)
        o_ref[:, sl] = _mm(dft_ref[...], stack) * norm


def _fnet_ctx(proj, cs, dft):
    return pl.pallas_call(
        _fnet_ctx_kernel,
        grid=(BATCH,),
        in_specs=[pl.BlockSpec((SEQ, B_WIDTH), lambda b: (b, 4)),
                  pl.BlockSpec(cs.shape, lambda b: (0, 0)),
                  pl.BlockSpec(dft.shape, lambda b: (0, 0))],
        out_specs=pl.BlockSpec((SEQ, B_WIDTH), lambda b: (b, 0)),
        out_shape=jax.ShapeDtypeStruct((NC, B_WIDTH), F32),
        compiler_params=_cparams(("arbitrary",)),
        name="fnet_ctx",
    )(proj, cs, dft)


def _fnet_lat1_kernel(u_ref, ca_ref, m1_ref, twc_ref, tws_ref, o_ref):
    c = twc_ref[...]
    s = tws_ref[...]
    for g in range(B_GROUPS):
        pa = _mm(u_ref[:, g * LANES:(g + 1) * LANES], ca_ref[...])
        rhs = jnp.concatenate([pa[:, :2 * LANES], pa[:, 2 * LANES:]], axis=0)
        zz = _mm(m1_ref[...], rhs)
        zr, zi = zz[:, :LANES], zz[:, LANES:]
        o_ref[:, 2 * g * LANES:(2 * g + 1) * LANES] = zr * c + zi * s
        o_ref[:, (2 * g + 1) * LANES:(2 * g + 2) * LANES] = zi * c - zr * s


def _fnet_lat2_kernel(z_ref, m1_ref, o_ref):
    norm = 1.0 / math.sqrt(DEC_SEQ * LANES)
    for g in range(B_GROUPS):
        rhs = jnp.concatenate([z_ref[:, 2 * g * LANES:(2 * g + 1) * LANES],
                               z_ref[:, (2 * g + 1) * LANES:(2 * g + 2) * LANES]], axis=0)
        o_ref[:, g * LANES:(g + 1) * LANES] = _mm(m1_ref[...], rhs) * norm


def _fnet_latent(proj, ca, m1, twc, tws):
    r = 64
    u = proj[NC:, 2048:2560].reshape(DEC_BATCH, r, r, B_WIDTH)
    u = u.transpose(0, 2, 1, 3).reshape(DEC_BATCH * r * r, B_WIDTH)
    steps = DEC_BATCH * r
    z = pl.pallas_call(
        _fnet_lat1_kernel,
        grid=(steps,),
        in_specs=[pl.BlockSpec((r, B_WIDTH), lambda s: (s, 0)),
                  pl.BlockSpec(ca.shape, lambda s: (0, 0)),
                  pl.BlockSpec(m1.shape, lambda s: (0, 0)),
                  pl.BlockSpec((None, r, LANES), lambda s: (s % r, 0, 0)),
                  pl.BlockSpec((None, r, LANES), lambda s: (s % r, 0, 0))],
        out_specs=pl.BlockSpec((r, 2 * B_WIDTH), lambda s: (s, 0)),
        out_shape=jax.ShapeDtypeStruct((NL, 2 * B_WIDTH), F32),
        compiler_params=_cparams(("arbitrary",)),
        name="fnet_lat1",
    )(u, ca, m1, twc, tws)
    z = z.reshape(DEC_BATCH, r, r, 2 * B_WIDTH).transpose(0, 2, 1, 3).reshape(NL, 2 * B_WIDTH)
    y = pl.pallas_call(
        _fnet_lat2_kernel,
        grid=(steps,),
        in_specs=[pl.BlockSpec((r, 2 * B_WIDTH), lambda s: (s, 0)),
                  pl.BlockSpec(m1.shape, lambda s: (0, 0))],
        out_specs=pl.BlockSpec((r, B_WIDTH), lambda s: (s, 0)),
        out_shape=jax.ShapeDtypeStruct((NL, B_WIDTH), F32),
        compiler_params=_cparams(("arbitrary",)),
        name="fnet_lat2",
    )(z, m1)
    return y.reshape(DEC_BATCH, r, r, B_WIDTH).transpose(0, 2, 1, 3).reshape(NL, B_WIDTH)


def _head_masks():
    lane = lax.broadcasted_iota(jnp.int32, (1, LANES), 1)
    return lane < NA_DH


def _na_ctx_kernel(q_ref, k_ref, v_ref, o_ref):
    first = _head_masks()
    q = q_ref[...]
    k = k_ref[...].astype(BF16)
    v = v_ref[...].astype(BF16)
    outs = []
    for a in range(2):
        hm = first if a == 0 else jnp.logical_not(first)
        s = _mm_nt(jnp.where(hm, q, 0.0), k) * (NA_DH ** -0.5)
        m = jnp.max(s, axis=-1, keepdims=True)
        p = jnp.exp(s - m)
        l = jnp.sum(p, axis=-1, keepdims=True)
        outs.append(_mm(p, v) / l)
    o_ref[...] = jnp.where(first, outs[0], outs[1])


def _na_ctx(qkv):
    return pl.pallas_call(
        _na_ctx_kernel,
        grid=(BATCH, NA_HEADS // 2),
        in_specs=[pl.BlockSpec((SEQ, LANES), lambda b, hp: (b, hp)),
                  pl.BlockSpec((SEQ, LANES), lambda b, hp: (b, 8 + hp)),
                  pl.BlockSpec((SEQ, LANES), lambda b, hp: (b, 16 + hp))],
        out_specs=pl.BlockSpec((SEQ, LANES), lambda b, hp: (b, hp)),
        out_shape=jax.ShapeDtypeStruct((NC, D), F32),
        compiler_params=_cparams(("arbitrary", "arbitrary")),
        name="na_ctx",
    )(qkv, qkv, qkv)


def _na_lat_kernel(q_ref, k_ref, v_ref, kc_ref, vc_ref, tt_ref, o_ref):
    first = _head_masks()
    rows = DEC_SEQ // GRID_W
    kctx = kc_ref[...].astype(BF16)
    vctx = vc_ref[...].astype(BF16)
    scale = NA_DH ** -0.5
    nkeys = WIN_R * GRID_W

    def row_body(r, carry):
        r0 = jnp.clip(r - WIN_R // 2, 0, rows - WIN_R)
        dr0 = r0 - r + (WIN_R - 1)
        q = q_ref[pl.ds(pl.multiple_of(r * GRID_W, GRID_W), GRID_W), :]
        kbase = pl.multiple_of(r0 * GRID_W, GRID_W)
        kl = k_ref[pl.ds(kbase, nkeys), :].astype(BF16)
        vl = v_ref[pl.ds(kbase, nkeys), :].astype(BF16)
        outs = []
        for a in range(2):
            hm = first if a == 0 else jnp.logical_not(first)
            qm = jnp.where(hm, q, 0.0)
            bias = jnp.concatenate([tt_ref[a, dr0 + 2 * j] for j in range(WIN_R // 2)], axis=1)
            s_loc = _mm_nt(qm, kl) * scale + bias
            s_ctx = _mm_nt(qm, kctx) * scale
            m = jnp.maximum(jnp.max(s_loc, axis=-1, keepdims=True), jnp.max(s_ctx, axis=-1, keepdims=True))
            p_loc = jnp.exp(s_loc - m)
            p_ctx = jnp.exp(s_ctx - m)
            l = jnp.sum(p_loc, axis=-1, keepdims=True) + jnp.sum(p_ctx, axis=-1, keepdims=True)
            outs.append((_mm(p_loc, vl) + _mm(p_ctx, vctx)) / l)
        o_ref[pl.ds(pl.multiple_of(r * GRID_W, GRID_W), GRID_W), :] = jnp.where(first, outs[0], outs[1])
        return carry

    lax.fori_loop(0, rows, row_body, 0)


def _na_latent(qkv, cache_k2, cache_v2, tt2):
    rb0 = NC // DEC_SEQ
    return pl.pallas_call(
        _na_lat_kernel,
        grid=(DEC_BATCH, NA_HEADS // 2),
        in_specs=[pl.BlockSpec((DEC_SEQ, LANES), lambda b, hp: (rb0 + b, hp)),
                  pl.BlockSpec((DEC_SEQ, LANES), lambda b, hp: (rb0 + b, 8 + hp)),
                  pl.BlockSpec((DEC_SEQ, LANES), lambda b, hp: (rb0 + b, 16 + hp)),
                  pl.BlockSpec((None, 256, LANES), lambda b, hp: (b, 0, hp)),
                  pl.BlockSpec((None, 256, LANES), lambda b, hp: (b, 0, hp)),
                  pl.BlockSpec((2, 2 * WIN_R - 2, GRID_W, LANES), lambda b, hp: (hp, 0, 0, 0))],
        out_specs=pl.BlockSpec((DEC_SEQ, LANES), lambda b, hp: (b, hp)),
        out_shape=jax.ShapeDtypeStruct((NL, D), F32),
        compiler_params=_cparams(("arbitrary", "arbitrary")),
        name="na_latent",
    )(qkv, qkv, qkv, cache_k2, cache_v2, tt2)


def _rpb_tables(rpb):
    col = np.arange(GRID_W)
    start = np.clip(col - WIN_C // 2, 0, GRID_W - WIN_C)
    inside = (col[None, :] >= start[:, None]) & (col[None, :] < start[:, None] + WIN_C)
    dc = np.clip(col[None, :] - col[:, None] + (WIN_C - 1), 0, 2 * WIN_C - 2)
    t = jnp.where(inside[None, None], rpb[:, :, dc], NEG)
    return jnp.concatenate([t[:, :-1], t[:, 1:]], axis=-1)


def _router_kernel(x_ref, m_ref, g_ref, wr_ref, br_ref, tri_ref, xf_ref, ri_ref, rw_ref, cnt_ref, base_scr):
    i = pl.program_id(0)

    @pl.when(i == 0)
    def _():
        base_scr[...] = jnp.zeros_like(base_scr)

    h = _modulated_norm(x_ref[...], m_ref, g_ref, 3, 4)
    xf_ref[...] = h
    logits = _mm_hi(h, wr_ref[...]) + br_ref[...]
    lane = lax.broadcasted_iota(jnp.int32, logits.shape, 1)
    rmax = lambda x: jnp.max(x, axis=-1, keepdims=True)
    rmin = lambda x: jnp.min(x, axis=-1, keepdims=True)
    rsum = lambda x: jnp.sum(x, axis=-1, keepdims=True)

    gmask = lane < N_GROUPS
    mg = rmax(jnp.where(gmask, logits, NEG))
    eg = jnp.where(gmask, jnp.exp(jnp.where(gmask, logits - mg, NEG)), 0.0)
    pg = eg / rsum(eg)
    p_grp = rmax(pg)
    grp = rmin(jnp.where(jnp.logical_and(gmask, pg == p_grp), lane, LANES))
    lo = N_GROUPS + grp * EXP_PER_GROUP
    emask = jnp.logical_and(lane >= lo, lane < lo + EXP_PER_GROUP)
    me = rmax(jnp.where(emask, logits, NEG))
    ee = jnp.where(emask, jnp.exp(jnp.where(emask, logits - me, NEG)), 0.0)
    pe = ee / rsum(ee)
    p1 = rmax(pe)
    i1 = rmin(jnp.where(jnp.logical_and(emask, pe == p1), lane, LANES))
    m2 = jnp.logical_and(emask, lane != i1)
    p2 = rmax(jnp.where(m2, pe, -1.0))
    i2 = rmin(jnp.where(jnp.logical_and(m2, pe == p2), lane, LANES))
    den = p1 + p2
    w1 = p_grp * p1 / den
    w2 = p_grp * p2 / den
    e1 = i1 - N_GROUPS
    e2 = i2 - N_GROUPS

    oh1 = jnp.where(lane == e1, 1.0, 0.0)
    oh2 = jnp.where(lane == e2, 1.0, 0.0)
    c1 = _mm(tri_ref[...], oh1)
    c2 = _mm(tri_ref[...], oh2)
    tot1 = jnp.sum(oh1, axis=0, keepdims=True)
    tot2 = jnp.sum(oh2, axis=0, keepdims=True)
    base = base_scr[0:1, :]
    rank1 = rsum(jnp.where(lane == e1, base + c1, 0.0))
    rank2 = rsum(jnp.where(lane == e2, base + tot1 + c2, 0.0))
    new_base = base + tot1 + tot2
    base_scr[...] = jnp.broadcast_to(new_base, base_scr.shape)
    cnt_ref[...] = jnp.broadcast_to(new_base, cnt_ref.shape)
    ri = jnp.where(lane == 0, e1, jnp.where(lane == 1, e2,
         jnp.where(lane == 2, rank1.astype(jnp.int32), jnp.where(lane == 3, rank2.astype(jnp.int32), 0))))
    ri_ref[...] = ri
    rw_ref[...] = jnp.where(lane == 0, w1, jnp.where(lane == 1, w2, 0.0))


def _router(x, m3, gain, wr, br, tri):
    return pl.pallas_call(
        _router_kernel,
        grid=(N // TM,),
        in_specs=[pl.BlockSpec((TM, D), lambda i: (i, 0)),
                  pl.BlockSpec((None, 6, D), lambda i: (_cond_row(i), 0, 0)),
                  pl.BlockSpec((1, D), lambda i: (0, 0)),
                  pl.BlockSpec((D, LANES), lambda i: (0, 0)),
                  pl.BlockSpec((1, LANES), lambda i: (0, 0)),
                  pl.BlockSpec((TM, TM), lambda i: (0, 0))],
        out_specs=[pl.BlockSpec((TM, D), lambda i: (i, 0)),
                   pl.BlockSpec((TM, LANES), lambda i: (i, 0)),
                   pl.BlockSpec((TM, LANES), lambda i: (i, 0)),
                   pl.BlockSpec((8, LANES), lambda i: (0, 0))],
        out_shape=[jax.ShapeDtypeStruct((N, D), F32),
                   jax.ShapeDtypeStruct((N, LANES), jnp.int32),
                   jax.ShapeDtypeStruct((N, LANES), F32),
                   jax.ShapeDtypeStruct((8, LANES), F32)],
        scratch_shapes=[pltpu.VMEM((8, LANES), F32)],
        compiler_params=_cparams(("arbitrary",)),
        name="router",
    )(x, m3, gain, wr, br, tri)


def _row_copy(src_hbm, row, dst, r, sem):
    return pltpu.make_async_copy(src_hbm.at[pl.ds(row, 1)], dst.at[pl.ds(r, 1)], sem)


def _expert_kernel(be_ref, nu_ref, st_ref, xf_hbm, wg_ref, wu_ref, wd_ref, ys_ref, xbuf, sem):
    del be_ref
    b = pl.program_id(0)

    @pl.when(b < nu_ref[0])
    def _():
        base = b * MOE_ROWS

        def issue(r, c):
            _row_copy(xf_hbm, st_ref[base + r], xbuf, r, sem.at[0]).start()
            return c

        lax.fori_loop(0, MOE_ROWS, issue, 0)

        def wait(r, c):
            _row_copy(xf_hbm, 0, xbuf, r, sem.at[0]).wait()
            return c

        lax.fori_loop(0, MOE_ROWS, wait, 0)
        xb = xbuf[...].astype(BF16)
        g = _mm(xb, wg_ref[...])
        u = _mm(xb, wu_ref[...])
        ys_ref[...] = _mm(_silu(g) * u, wd_ref[...])

    @pl.when(b >= nu_ref[0])
    def _():
        ys_ref[...] = jnp.zeros_like(ys_ref)


def _experts(block_e, n_used, slot_tok, xf, w_gate, w_up, w_down, layer):
    grid_spec = pltpu.PrefetchScalarGridSpec(
        num_scalar_prefetch=3,
        grid=(N_SLOT_BLOCKS,),
        in_specs=[pl.BlockSpec(memory_space=pl.ANY),
                  pl.BlockSpec((None, None, D, D_EXPERT), lambda b, be, nu, st: (layer, be[b], 0, 0)),
                  pl.BlockSpec((None, None, D, D_EXPERT), lambda b, be, nu, st: (layer, be[b], 0, 0)),
                  pl.BlockSpec((None, None, D_EXPERT, D), lambda b, be, nu, st: (layer, be[b], 0, 0))],
        out_specs=pl.BlockSpec((MOE_ROWS, D), lambda b, be, nu, st: (b, 0)),
        scratch_shapes=[pltpu.VMEM((MOE_ROWS, D), F32), pltpu.SemaphoreType.DMA((1,))])
    return pl.pallas_call(
        _expert_kernel,
        grid_spec=grid_spec,
        out_shape=jax.ShapeDtypeStruct((N_SLOT_BLOCKS * MOE_ROWS, D), F32),
        compiler_params=_cparams(("arbitrary",)),
        name="experts",
    )(block_e, n_used, slot_tok, xf, w_gate, w_up, w_down)


def _combine_kernel(d_ref, ys_hbm, x_ref, m_ref, rw_ref, fn_ref, o_ref, buf, sem, *, final):
    i = pl.program_id(0)
    base = i * TM

    def issue(r, c):
        for kk in range(2):
            _row_copy(ys_hbm, d_ref[(base + r) * 2 + kk], buf.at[kk], r, sem.at[0]).start()
        return c

    lax.fori_loop(0, TM, issue, 0)

    def wait(r, c):
        for kk in range(2):
            _row_copy(ys_hbm, 0, buf.at[kk], r, sem.at[0]).wait()
        return c

    lax.fori_loop(0, TM, wait, 0)
    w = rw_ref[...]
    y = w[:, 0:1] * buf[0] + w[:, 1:2] * buf[1]
    out = x_ref[...] + m_ref[5:6, :] * y
    if final:
        ms = jnp.mean(out * out, axis=-1, keepdims=True)
        out = out * lax.rsqrt(ms + EPS) * fn_ref[...]
    o_ref[...] = out


def _combine(dest_flat, ys, x, m3, rw, final_norm, final):
    grid_spec = pltpu.PrefetchScalarGridSpec(
        num_scalar_prefetch=1,
        grid=(N // TM,),
        in_specs=[pl.BlockSpec(memory_space=pl.ANY),
                  pl.BlockSpec((TM, D), lambda i, d: (i, 0)),
                  pl.BlockSpec((None, 6, D), lambda i, d: (_cond_row(i), 0, 0)),
                  pl.BlockSpec((TM, LANES), lambda i, d: (i, 0)),
                  pl.BlockSpec((1, D), lambda i, d: (0, 0))],
        out_specs=pl.BlockSpec((TM, D), lambda i, d: (i, 0)),
        scratch_shapes=[pltpu.VMEM((2, TM, D), F32), pltpu.SemaphoreType.DMA((1,))])
    return pl.pallas_call(
        functools.partial(_combine_kernel, final=final),
        grid_spec=grid_spec,
        out_shape=jax.ShapeDtypeStruct((N, D), F32),
        compiler_params=_cparams(("arbitrary",)),
        name="combine",
    )(dest_flat, ys, x, m3, rw, final_norm)


def _hier_moe(x, m3, gain, layer, w_rg, b_rg, w_re, b_re, w_gate, w_up, w_down, tri_tm, final_norm, final):
    wr = jnp.concatenate([w_rg, w_re.transpose(1, 0, 2).reshape(D, N_EXPERTS),
                          jnp.zeros((D, LANES - N_GROUPS - N_EXPERTS), F32)], axis=1)
    br = jnp.concatenate([b_rg, b_re.reshape(N_EXPERTS),
                          jnp.zeros((LANES - N_GROUPS - N_EXPERTS,), F32)])[None, :]
    xf, ri, rw, cnt = _router(x, m3, gain, wr, br, tri_tm)
    e_idx = ri[:, 0:2]
    rank = ri[:, 2:4]
    counts = cnt[0, :N_EXPERTS].astype(jnp.int32)
    padded = (counts + MOE_ROWS - 1) // MOE_ROWS * MOE_ROWS
    end_pad = jnp.cumsum(padded)
    start_pad = end_pad - padded
    dest = (start_pad[e_idx] + rank).reshape(-1).astype(jnp.int32)
    block_e = jnp.minimum(jnp.searchsorted(end_pad, jnp.arange(N_SLOT_BLOCKS, dtype=jnp.int32) * MOE_ROWS,
                                           side='right'), N_EXPERTS - 1).astype(jnp.int32)
    n_used = (end_pad[-1:] // MOE_ROWS).astype(jnp.int32)
    tok = jnp.repeat(jnp.arange(N, dtype=jnp.int32), 2)
    slot_tok = jnp.zeros((N_SLOT_BLOCKS * MOE_ROWS,), jnp.int32).at[dest].set(tok)
    ys = _experts(block_e, n_used, slot_tok, xf, w_gate, w_up, w_down, layer)
    return _combine(dest, ys, x, m3, rw, final_norm, final)


def kernel(x_prompt, x_sample, state_A_fwd, state_A_bwd, cache_k, cache_v, c, c_ctx, mod_w, mod_b, norm_mix, norm_ffn, ab_w_in, ab_conv, ab_a_log, ab_dt_bias, ab_o_gain, ab_w_out, na_w_qkv, na_rpb, na_w_out, moe_w_rg, moe_b_rg, moe_w_re, moe_b_re, moe_w_gate, moe_w_up, moe_w_down, final_norm):
    x = jnp.concatenate([x_prompt.reshape(NC, D), x_sample.reshape(NL, D)], axis=0)
    cond8 = jnp.concatenate([c_ctx[None, :], c, jnp.zeros((8 - 1 - DEC_BATCH, D), F32)], axis=0)
    mods = _ada_params(cond8, mod_w, mod_b).reshape(DEPTH, 8, 6, D)

    tri_tm = jnp.asarray(np.tril(np.ones((TM, TM)), -1), BF16)
    tri_c = jnp.asarray(np.tril(np.ones((CHUNK, CHUNK))), F32)
    half = np.arange(LANES) < A_DH
    jbd = jnp.asarray((half[:, None] == half[None, :]).astype(np.float32))
    fn = final_norm[None, :]

    m3 = mods[0]
    w_in = ab_w_in[0]
    w_in = jnp.concatenate([w_in[:, :2048], w_in[:, 2080:2592], w_in[:, 2048:2080],
                            jnp.zeros((D, AB_COLS - 2592), F32)], axis=1).astype(BF16)
    proj = _modproj(x, m3, norm_mix[0][None, :], w_in, 896)
    gate_p = jnp.zeros((8, LANES), F32)
    gate_p = gate_p.at[0, :2 * A_HEADS].set(ab_a_log[0].reshape(-1))
    gate_p = gate_p.at[1, :2 * A_HEADS].set(ab_dt_bias[0].reshape(-1))
    o_gain2 = jnp.tile(ab_o_gain[0], 2)[None, :]
    zeros_state = jnp.zeros((BATCH, A_HEADS, A_DH, A_DH), F32)
    mix_a_c, s_f, s_b = _deltanet(proj, ab_conv[0], gate_p, o_gain2, zeros_state, zeros_state, tri_c, jbd,
                                  seq_len=SEQ, n_seq=BATCH, row_blk0=0)
    mix_a_l, _, _ = _deltanet(proj, ab_conv[0], gate_p, o_gain2, state_A_fwd[:, 0], state_A_bwd[:, 0], tri_c, jbd,
                              seq_len=DEC_SEQ, n_seq=DEC_BATCH, row_blk0=NC // DEC_SEQ)

    cc, sc = _dft_mats(LANES)
    ct, st = _dft_mats(SEQ)
    c64, s64 = _dft_mats(64)
    cs = jnp.asarray(np.concatenate([cc, sc], axis=1), BF16)
    dft = jnp.asarray(np.concatenate([ct, -st], axis=1), BF16)
    ca = jnp.asarray(np.concatenate([cc, -sc, -sc, -cc], axis=1), BF16)
    m1 = jnp.asarray(np.concatenate([c64, s64], axis=1), BF16)
    tw_idx = np.arange(64)
    tw_ang = 2.0 * np.pi * (tw_idx[:, None] * tw_idx[None, :]) / DEC_SEQ
    twc = jnp.broadcast_to(jnp.asarray(np.cos(tw_ang), F32)[:, :, None], (64, 64, LANES))
    tws = jnp.broadcast_to(jnp.asarray(np.sin(tw_ang), F32)[:, :, None], (64, 64, LANES))
    mix_b_c = _fnet_ctx(proj, cs, dft)
    mix_b_l = _fnet_latent(proj, ca, m1, twc, tws)

    w_out = ab_w_out[0].astype(BF16)
    x = _outproj([(mix_a_c, mix_a_l), (mix_b_c, mix_b_l)], [w_out[:A_WIDTH], w_out[A_WIDTH:]], x, m3)
    x = _hier_moe(x, m3, norm_ffn[0][None, :], 0, moe_w_rg[0], moe_b_rg[0], moe_w_re[0], moe_b_re[0],
                  moe_w_gate, moe_w_up, moe_w_down, tri_tm, fn, False)

    m3 = mods[1]
    qkv = _modproj(x, m3, norm_mix[1][None, :], na_w_qkv[0].astype(BF16), 512)
    attn_c = _na_ctx(qkv)
    attn_l = _na_latent(qkv, cache_k[:, 0].reshape(DEC_BATCH, 256, D), cache_v[:, 0].reshape(DEC_BATCH, 256, D),
                        _rpb_tables(na_rpb[0]))
    x = _outproj([(attn_c, attn_l)], [na_w_out[0].astype(BF16)], x, m3)
    x = _hier_moe(x, m3, norm_ffn[1][None, :], 1, moe_w_rg[1], moe_b_rg[1], moe_w_re[1], moe_b_re[1],
                  moe_w_gate, moe_w_up, moe_w_down, tri_tm, fn, True)

    new_k = qkv[:NC, D:2 * D].reshape(BATCH, 1, SEQ, NA_HEADS, NA_DH)
    new_v = qkv[:NC, 2 * D:].reshape(BATCH, 1, SEQ, NA_HEADS, NA_DH)
    return (x[:NC].reshape(BATCH, SEQ, D), x[NC:].reshape(DEC_BATCH, DEC_SEQ, D),
            s_f[:, None], s_b[:, None], new_k, new_v)
```

```python
import functools
import math

import numpy as np
import jax
import jax.numpy as jnp
from jax import lax
from jax.experimental import pallas as pl
from jax.experimental.pallas import tpu as pltpu

F32 = jnp.float32
BF16 = jnp.bfloat16
HIGHEST = lax.Precision.HIGHEST

D = 1024
BATCH, SEQ = 32, 256
DEC_BATCH, DEC_SEQ = 2, 4096
NC = BATCH * SEQ
NL = DEC_BATCH * DEC_SEQ
N = NC + NL
DEPTH = 2
GRID_W = 64
A_DH = 64
A_HEADS = 8
A_WIDTH = 512
CHUNK = 64
B_WIDTH = 512
B_GROUPS = 4
NA_DH = 64
NA_HEADS = 16
WIN_R, WIN_C = 8, 16
N_GROUPS, EXP_PER_GROUP, N_EXPERTS = 4, 8, 32
D_EXPERT = 512
EPS = 1e-6

LANES = 128
TM = 256
MOE_ROWS = 256
N_SLOT_BLOCKS = (2 * N) // MOE_ROWS + N_EXPERTS
AB_COLS = 2688
VMEM_LIMIT = 56 * 1024 * 1024
NEG = -1e30


def _cparams(sem):
    return pltpu.CompilerParams(dimension_semantics=sem, vmem_limit_bytes=VMEM_LIMIT)


def _mm(a, b):
    return jnp.dot(a.astype(BF16), b.astype(BF16), preferred_element_type=F32)


def _mm_nt(a, b):
    return lax.dot_general(a.astype(BF16), b.astype(BF16), (((1,), (1,)), ((), ())),
                           preferred_element_type=F32)


SOLVE_BLK = 16
TERM_UNROLL = 4


def _unit_lower_solve(a_mat, rhs, same_blk, eye):
    dg = jnp.where(same_blk, a_mat, 0.0)
    p = -dg
    dinv = eye + p
    for _ in range(int(math.log2(SOLVE_BLK)) - 1):
        p = _mm(p, p)
        dinv = dinv + _mm(p, dinv)
    mp = -_mm(dinv, a_mat - dg)
    y = _mm(dinv, rhs)
    y = y + _mm(mp, y)
    for _ in range(int(math.log2(a_mat.shape[0] // SOLVE_BLK)) - 1):
        mp = _mm(mp, mp)
        y = y + _mm(mp, y)
    return y


def _unit_lower_solve_many(a_mats, rhs, same_blk, eye):
    dg = [jnp.where(same_blk, a, 0.0) for a in a_mats]
    off = [a - g for a, g in zip(a_mats, dg)]
    p = [-g for g in dg]
    dinv = [eye + x for x in p]
    for _ in range(int(math.log2(SOLVE_BLK)) - 1):
        p = [_mm(x, x) for x in p]
        dinv = [di + _mm(x, di) for x, di in zip(p, dinv)]
    mp = [-_mm(di, o) for di, o in zip(dinv, off)]
    y = [_mm(di, r) for di, r in zip(dinv, rhs)]
    y = [yi + _mm(m, yi) for m, yi in zip(mp, y)]
    for _ in range(int(math.log2(a_mats[0].shape[0] // SOLVE_BLK)) - 1):
        mp = [_mm(m, m) for m in mp]
        y = [yi + _mm(m, yi) for m, yi in zip(mp, y)]
    return y


def _mm_split(a, b, parts, split_rhs=False):
    x = b if split_rhs else a
    acc = None
    for _ in range(parts):
        piece = x.astype(BF16)
        x = x - piece.astype(F32)
        term = (jnp.dot(a.astype(BF16), piece, preferred_element_type=F32) if split_rhs
                else jnp.dot(piece, b.astype(BF16), preferred_element_type=F32))
        acc = term if acc is None else acc + term
    return acc


def _mm_hi(a, b):
    return jnp.dot(a, b, preferred_element_type=F32, precision=HIGHEST)


def _silu(x):
    return x * jax.nn.sigmoid(x)


def _cond_row(i):
    return jnp.where(i < NC // TM, 0, 1 + (i - NC // TM) // (DEC_SEQ // TM))


def _modulated_norm(x, m_ref, g_ref, shift_idx, scale_idx):
    ms = jnp.mean(x * x, axis=-1, keepdims=True)
    y = x * lax.rsqrt(ms + EPS) * g_ref[...]
    return y * (1.0 + m_ref[scale_idx:scale_idx + 1, :]) + m_ref[shift_idx:shift_idx + 1, :]


def _ada_kernel(cond_ref, w_ref, b_ref, o_ref):
    o_ref[...] = _mm_hi(_silu(cond_ref[...]), w_ref[...]) + b_ref[...]


def _ada_params(cond8, mod_w, mod_b):
    tn = 1536
    return pl.pallas_call(
        _ada_kernel,
        grid=(DEPTH, 6 * D // tn),
        in_specs=[pl.BlockSpec((8, D), lambda l, j: (0, 0)),
                  pl.BlockSpec((None, D, tn), lambda l, j: (l, 0, j)),
                  pl.BlockSpec((None, 1, tn), lambda l, j: (l, 0, j))],
        out_specs=pl.BlockSpec((None, 8, tn), lambda l, j: (l, 0, j)),
        out_shape=jax.ShapeDtypeStruct((DEPTH, 8, 6 * D), F32),
        compiler_params=_cparams(("arbitrary", "arbitrary")),
        name="ada_params",
    )(cond8, mod_w, mod_b.reshape(DEPTH, 1, 6 * D))


def _modproj_kernel(x_ref, m_ref, g_ref, w_ref, o_ref, *, n_chunk):
    hb = _modulated_norm(x_ref[...], m_ref, g_ref, 0, 1).astype(BF16)
    for j in range(o_ref.shape[1] // n_chunk):
        sl = slice(j * n_chunk, (j + 1) * n_chunk)
        o_ref[:, sl] = jnp.dot(hb, w_ref[:, sl], preferred_element_type=F32)


def _modproj(x, m3, gain, w_bf16, n_chunk):
    nout = w_bf16.shape[1]
    return pl.pallas_call(
        functools.partial(_modproj_kernel, n_chunk=n_chunk),
        grid=(N // TM,),
        in_specs=[pl.BlockSpec((TM, D), lambda i: (i, 0)),
                  pl.BlockSpec((None, 6, D), lambda i: (_cond_row(i), 0, 0)),
                  pl.BlockSpec((1, D), lambda i: (0, 0)),
                  pl.BlockSpec((D, nout), lambda i: (0, 0))],
        out_specs=pl.BlockSpec((TM, nout), lambda i: (i, 0)),
        out_shape=jax.ShapeDtypeStruct((N, nout), F32),
        compiler_params=_cparams(("arbitrary",)),
        name="modproj",
    )(x, m3, gain, w_bf16)


def _outproj_kernel(*refs, n_in):
    a_refs = refs[:2 * n_in]
    w_refs = refs[2 * n_in:3 * n_in]
    x_ref, m_ref, o_ref = refs[3 * n_in:]
    is_ctx = pl.program_id(0) < NC // TM
    acc = None
    for j, w_ref in enumerate(w_refs):
        a = jnp.where(is_ctx, a_refs[2 * j][...], a_refs[2 * j + 1][...])
        part = _mm(a, w_ref[...])
        acc = part if acc is None else acc + part
    o_ref[...] = x_ref[...] + m_ref[2:3, :] * acc


def _outproj(a_pairs, w_list, x, m3):
    n_in = len(a_pairs)
    nct = NC // TM
    in_specs, args = [], []
    for a_ctx, a_lat in a_pairs:
        in_specs.append(pl.BlockSpec((TM, a_ctx.shape[1]), lambda i: (jnp.minimum(i, nct - 1), 0)))
        in_specs.append(pl.BlockSpec((TM, a_lat.shape[1]), lambda i: (jnp.maximum(i - nct, 0), 0)))
        args += [a_ctx, a_lat]
    in_specs += ([pl.BlockSpec(w.shape, lambda i: (0, 0)) for w in w_list]
                 + [pl.BlockSpec((TM, D), lambda i: (i, 0)),
                    pl.BlockSpec((None, 6, D), lambda i: (_cond_row(i), 0, 0))])
    return pl.pallas_call(
        functools.partial(_outproj_kernel, n_in=n_in),
        grid=(N // TM,),
        in_specs=in_specs,
        out_specs=pl.BlockSpec((TM, D), lambda i: (i, 0)),
        out_shape=jax.ShapeDtypeStruct((N, D), F32),
        compiler_params=_cparams(("arbitrary",)),
        name="outproj",
    )(*args, *w_list, x, m3)


def _deltanet_kernel(q_ref, k_ref, v_ref, z_ref, ab_ref, cq_ref, ck_ref, cv_ref, gp_ref, og_ref,
                     s0f_ref, s0b_ref, tri_ref, jbd_ref, o_ref, sf_ref, sb_ref,
                     u_s, w_s, qd_s, at_s, kt_s, ge_s, st_s, ob, *, seq_len, n_sub):
    hp = pl.program_id(1)
    C = CHUNK
    nc = seq_len // C
    nct = n_sub * nc
    P = LANES
    lane = lax.broadcasted_iota(jnp.int32, (C, P), 1)
    row = lax.broadcasted_iota(jnp.int32, (C, P), 0)
    first_head = lane < A_DH
    ri = lax.broadcasted_iota(jnp.int32, (P, P), 0)
    ci = lax.broadcasted_iota(jnp.int32, (P, P), 1)
    same_head = (ri < C) == (ci < C)
    same_blk = (ri // SOLVE_BLK) == (ci // SOLVE_BLK)
    eye = jnp.where(ri == ci, 1.0, 0.0)
    jbd = jbd_ref[...]
    lincl = tri_ref[...]
    tri_b = [lincl.astype(BF16), lincl.T.astype(BF16)]
    incl_m = [jnp.logical_and(same_head, ri >= ci), jnp.logical_and(same_head, ri <= ci)]
    strict_m = [jnp.logical_and(same_head, ri > ci), jnp.logical_and(same_head, ri < ci)]
    neg_a = -jnp.exp(gp_ref[0:1, :])
    dt_b = gp_ref[1:2, :]

    def conv_silu(ref, w_ref, c):
        base = pl.multiple_of(c * C, C)
        cs = c % nc
        xc = ref[pl.ds(base, C), :]
        pbase = pl.multiple_of(jnp.maximum(base - 8, 0), 8)
        nbase = pl.multiple_of(jnp.minimum(base + C, n_sub * seq_len - 8), 8)
        prev_row = ref[pl.ds(pbase, 8), :][7:8, :] * jnp.where(cs > 0, 1.0, 0.0)
        next_row = ref[pl.ds(nbase, 8), :][0:1, :] * jnp.where(cs < nc - 1, 1.0, 0.0)
        x_prev = jnp.where(row == 0, prev_row, pltpu.roll(xc, 1, 0))
        x_next = jnp.where(row == C - 1, next_row, pltpu.roll(xc, C - 1, 0))
        y = w_ref[0:1, :] * x_prev + w_ref[1:2, :] * xc + w_ref[2:3, :] * x_next
        return _silu(y)

    def stack(x):
        return jnp.concatenate([jnp.where(first_head, x, 0.0), jnp.where(first_head, 0.0, x)], axis=0)

    def lane_col(arr, idx):
        return jnp.sum(jnp.where(lane == idx, arr, 0.0), axis=-1, keepdims=True)

    def chunk_inputs(c):
        base = pl.multiple_of(c * C, C)
        q = conv_silu(q_ref, cq_ref, c)
        k = conv_silu(k_ref, ck_ref, c)
        v = conv_silu(v_ref, cv_ref, c)
        ab = ab_ref[pl.ds(base, C), :]
        g_all = neg_a * jax.nn.softplus(ab + dt_b)
        beta_all = jax.nn.sigmoid(ab)
        return q, k, v, g_all, beta_all

    def chain_gates(g_all, beta_all, gc_all, d):
        cols = []
        for a in range(2):
            g_lane = d * A_HEADS + 2 * hp + a
            g_col = lane_col(g_all, g_lane)
            tot = jnp.broadcast_to(jnp.sum(g_col, axis=0, keepdims=True), (C, 1))
            cols.append((lane_col(gc_all, g_lane), lane_col(beta_all, 2 * A_HEADS + g_lane), tot))
        return [jnp.concatenate([cols[0][j], cols[1][j]], axis=0) for j in range(3)]

    def terms_body(j, carry):
        cs = [TERM_UNROLL * j + t for t in range(TERM_UNROLL)]
        ins = [chunk_inputs(c) for c in cs]
        qsq = [_mm_split(x[0] * x[0], jbd, 2) for x in ins]
        ksq = [_mm_split(x[1] * x[1], jbd, 2) for x in ins]
        qs = [x[0] * lax.rsqrt(s + EPS) * (A_DH ** -0.5) for x, s in zip(ins, qsq)]
        ks = [x[1] * lax.rsqrt(s + EPS) for x, s in zip(ins, ksq)]
        qst = [stack(x) for x in qs]
        kst = [stack(x) for x in ks]
        vst = [stack(x[2]) for x in ins]
        kk = [_mm_nt(x, x) for x in kst]
        qk = [_mm_nt(x, y) for x, y in zip(qst, kst)]
        chains = [(t, d) for t in range(TERM_UNROLL) for d in range(2)]
        gc_all = [_mm_split(tri_b[d], ins[t][3], 3, split_rhs=True) for t, d in chains]
        gates = [chain_gates(ins[t][3], ins[t][4], gc, d) for (t, d), gc in zip(chains, gc_all)]
        decay, e_gc = [], []
        for (t, d), (gc_col, beta_col, tot_col) in zip(chains, gates):
            gcb = jnp.broadcast_to(gc_col, (P, P))
            decay.append(jnp.where(incl_m[d], jnp.exp(jnp.where(incl_m[d], gcb - gcb.T, 0.0)), 0.0))
            e_gc.append(jnp.exp(gc_col))
        a_mats = [jnp.where(strict_m[d], g[1] * kk[t] * dc, 0.0) for (t, d), g, dc in zip(chains, gates, decay)]
        rhs = [jnp.concatenate([vst[t] * g[1], kst[t] * (g[1] * e)], axis=1)
               for (t, d), g, e in zip(chains, gates, e_gc)]
        xs = _unit_lower_solve_many(a_mats, rhs, same_blk, eye)
        for (t, d), x, g, e, dc in zip(chains, xs, gates, e_gc, decay):
            c = cs[t]
            u_s[d, c] = x[:, :P].astype(BF16)
            w_s[d, c] = x[:, P:].astype(BF16)
            qd_s[d, c] = (qst[t] * e).astype(BF16)
            at_s[d, c] = jnp.where(incl_m[d], qk[t] * dc, 0.0).astype(BF16)
            kt_s[d, c] = (kst[t] * jnp.exp(g[2] - g[0])).T.astype(BF16)
            e_tot = jnp.exp(g[2])
            ge_s[d, c] = jnp.where(lax.broadcasted_iota(jnp.int32, (8, P), 1) < A_DH,
                                   e_tot[0:1, :], e_tot[C:C + 1, :])
        return carry

    lax.fori_loop(0, nct // TERM_UNROLL, terms_body, 0)

    def block_diag(s2):
        z = jnp.zeros((A_DH, A_DH), F32)
        return jnp.concatenate([jnp.concatenate([s2[0], z], axis=1),
                                jnp.concatenate([z, s2[1]], axis=1)], axis=0)

    for s in range(n_sub):
        st_s[2 * s] = block_diag(s0f_ref[s])
        st_s[2 * s + 1] = block_diag(s0b_ref[s])

    def scan_body(i, carry):
        chains = [(s, d, s * nc + (i if d == 0 else nc - 1 - i)) for s in range(n_sub) for d in range(2)]
        dot = functools.partial(jnp.dot, preferred_element_type=F32)
        s_bd = [st_s[2 * s + d] for s, d, c in chains]
        sb16 = [x.astype(BF16) for x in s_bd]
        ws = [dot(w_s[d, c], sb) for (s, d, c), sb in zip(chains, sb16)]
        qs_ = [dot(qd_s[d, c], sb) for (s, d, c), sb in zip(chains, sb16)]
        vb = [(u_s[d, c].astype(F32) - x).astype(BF16) for (s, d, c), x in zip(chains, ws)]
        av = [dot(at_s[d, c], x) for (s, d, c), x in zip(chains, vb)]
        kv = [dot(kt_s[d, c], x) for (s, d, c), x in zip(chains, vb)]
        for (s, d, c), sb, q_, a_, k_ in zip(chains, s_bd, qs_, av, kv):
            st_s[2 * s + d] = sb * ge_s[d, c][0:1, :] + k_
            o_st = q_ + a_
            dst = o_ref if d == 0 else ob
            dst[pl.ds(pl.multiple_of(c * C, C), C), :] = o_st[:C] + o_st[C:]
        return carry

    lax.fori_loop(0, nc, scan_body, 0)

    for s in range(n_sub):
        for d, ref in ((0, sf_ref), (1, sb_ref)):
            s_bd = st_s[2 * s + d]
            ref[s, 0] = s_bd[:A_DH, :A_DH]
            ref[s, 1] = s_bd[A_DH:, A_DH:]

    def finish(c, carry):
        base = pl.multiple_of(c * C, C)
        o = o_ref[pl.ds(base, C), :] + ob[pl.ds(base, C), :]
        ms = _mm_split(o * o, jbd, 2) * (1.0 / A_DH)
        y = o * lax.rsqrt(ms + EPS) * og_ref[...]
        o_ref[pl.ds(base, C), :] = y * _silu(z_ref[pl.ds(base, C), :])
        return carry

    lax.fori_loop(0, nct, finish, 0)


def _deltanet(proj, conv_w, gate_p, o_gain2, s0f, s0b, tri, jbd, *, seq_len, n_seq, n_sub, row_blk0):
    rows = n_sub * seq_len
    nct = rows // CHUNK
    rb = lambda b: row_blk0 + b
    col = lambda off: (lambda b, hp: (rb(b), off + hp))
    st_spec = pl.BlockSpec((n_sub, 2, A_DH, A_DH), lambda b, hp: (b, hp, 0, 0))
    in_specs = [pl.BlockSpec((rows, LANES), col(0)),
                pl.BlockSpec((rows, LANES), col(4)),
                pl.BlockSpec((rows, LANES), col(8)),
                pl.BlockSpec((rows, LANES), col(12)),
                pl.BlockSpec((rows, LANES), lambda b, hp: (rb(b), 20)),
                pl.BlockSpec((3, LANES), lambda b, hp: (0, hp)),
                pl.BlockSpec((3, LANES), lambda b, hp: (0, 4 + hp)),
                pl.BlockSpec((3, LANES), lambda b, hp: (0, 8 + hp)),
                pl.BlockSpec((8, LANES), lambda b, hp: (0, 0)),
                pl.BlockSpec((1, LANES), lambda b, hp: (0, 0)),
                st_spec, st_spec,
                pl.BlockSpec((CHUNK, CHUNK), lambda b, hp: (0, 0)),
                pl.BlockSpec((LANES, LANES), lambda b, hp: (0, 0))]
    args = [proj, proj, proj, proj, proj, conv_w, conv_w, conv_w, gate_p, o_gain2, s0f, s0b, tri, jbd]
    st_shape = jax.ShapeDtypeStruct((n_seq, A_HEADS, A_DH, A_DH), F32)
    tile = lambda dt: pltpu.VMEM((2, nct, LANES, LANES), dt)
    return pl.pallas_call(
        functools.partial(_deltanet_kernel, seq_len=seq_len, n_sub=n_sub),
        grid=(n_seq // n_sub, A_HEADS // 2),
        in_specs=in_specs,
        out_specs=[pl.BlockSpec((rows, LANES), lambda b, hp: (b, hp)), st_spec, st_spec],
        out_shape=[jax.ShapeDtypeStruct((n_seq * seq_len, A_WIDTH), F32), st_shape, st_shape],
        scratch_shapes=[tile(BF16), tile(BF16), tile(BF16), tile(BF16), tile(BF16),
                        pltpu.VMEM((2, nct, 8, LANES), F32),
                        pltpu.VMEM((2 * n_sub, LANES, LANES), F32),
                        pltpu.VMEM((rows, LANES), F32)],
        compiler_params=_cparams(("arbitrary", "arbitrary")),
        name="deltanet",
    )(*args)


def _dft_mats(n):
    idx = np.arange(n)
    ang = 2.0 * np.pi * ((idx[:, None] * idx[None, :]) % n) / n
    return np.cos(ang), np.sin(ang)


def _fnet_ctx_kernel(u_ref, cs_ref, dft_ref, o_ref):
    norm = 1.0 / math.sqrt(SEQ * LANES)
    for g in range(B_GROUPS):
        sl = slice(g * LANES, (g + 1) * LANES)
        p = _mm(u_ref[:, sl], cs_ref[...])
        stack = jnp.concatenate([p[:, :LANES], p[:, LANES:]], axis=0)
        o_ref[:, sl] = _mm(dft_ref[...], stack) * norm


def _fnet_ctx(proj, cs, dft):
    return pl.pallas_call(
        _fnet_ctx_kernel,
        grid=(BATCH,),
        in_specs=[pl.BlockSpec((SEQ, B_WIDTH), lambda b: (b, 4)),
                  pl.BlockSpec(cs.shape, lambda b: (0, 0)),
                  pl.BlockSpec(dft.shape, lambda b: (0, 0))],
        out_specs=pl.BlockSpec((SEQ, B_WIDTH), lambda b: (b, 0)),
        out_shape=jax.ShapeDtypeStruct((NC, B_WIDTH), F32),
        compiler_params=_cparams(("arbitrary",)),
        name="fnet_ctx",
    )(proj, cs, dft)


def _fnet_lat1_kernel(u_ref, ca_ref, m1_ref, twc_ref, tws_ref, o_ref):
    c = twc_ref[...]
    s = tws_ref[...]
    for g in range(B_GROUPS):
        pa = _mm(u_ref[:, g * LANES:(g + 1) * LANES], ca_ref[...])
        rhs = jnp.concatenate([pa[:, :2 * LANES], pa[:, 2 * LANES:]], axis=0)
        zz = _mm(m1_ref[...], rhs)
        zr, zi = zz[:, :LANES], zz[:, LANES:]
        o_ref[:, 2 * g * LANES:(2 * g + 1) * LANES] = zr * c + zi * s
        o_ref[:, (2 * g + 1) * LANES:(2 * g + 2) * LANES] = zi * c - zr * s


def _fnet_lat2_kernel(z_ref, m1_ref, o_ref):
    norm = 1.0 / math.sqrt(DEC_SEQ * LANES)
    for g in range(B_GROUPS):
        rhs = jnp.concatenate([z_ref[:, 2 * g * LANES:(2 * g + 1) * LANES],
                               z_ref[:, (2 * g + 1) * LANES:(2 * g + 2) * LANES]], axis=0)
        o_ref[:, g * LANES:(g + 1) * LANES] = _mm(m1_ref[...], rhs) * norm


def _fnet_latent(proj, ca, m1, twc, tws):
    r = 64
    u = proj[NC:, 2048:2560].reshape(DEC_BATCH, r, r, B_WIDTH)
    u = u.transpose(0, 2, 1, 3).reshape(DEC_BATCH * r * r, B_WIDTH)
    steps = DEC_BATCH * r
    z = pl.pallas_call(
        _fnet_lat1_kernel,
        grid=(steps,),
        in_specs=[pl.BlockSpec((r, B_WIDTH), lambda s: (s, 0)),
                  pl.BlockSpec(ca.shape, lambda s: (0, 0)),
                  pl.BlockSpec(m1.shape, lambda s: (0, 0)),
                  pl.BlockSpec((None, r, LANES), lambda s: (s % r, 0, 0)),
                  pl.BlockSpec((None, r, LANES), lambda s: (s % r, 0, 0))],
        out_specs=pl.BlockSpec((r, 2 * B_WIDTH), lambda s: (s, 0)),
        out_shape=jax.ShapeDtypeStruct((NL, 2 * B_WIDTH), F32),
        compiler_params=_cparams(("arbitrary",)),
        name="fnet_lat1",
    )(u, ca, m1, twc, tws)
    z = z.reshape(DEC_BATCH, r, r, 2 * B_WIDTH).transpose(0, 2, 1, 3).reshape(NL, 2 * B_WIDTH)
    y = pl.pallas_call(
        _fnet_lat2_kernel,
        grid=(steps,),
        in_specs=[pl.BlockSpec((r, 2 * B_WIDTH), lambda s: (s, 0)),
                  pl.BlockSpec(m1.shape, lambda s: (0, 0))],
        out_specs=pl.BlockSpec((r, B_WIDTH), lambda s: (s, 0)),
        out_shape=jax.ShapeDtypeStruct((NL, B_WIDTH), F32),
        compiler_params=_cparams(("arbitrary",)),
        name="fnet_lat2",
    )(z, m1)
    return y.reshape(DEC_BATCH, r, r, B_WIDTH).transpose(0, 2, 1, 3).reshape(NL, B_WIDTH)


def _head_masks():
    lane = lax.broadcasted_iota(jnp.int32, (1, LANES), 1)
    return lane < NA_DH


def _na_ctx_kernel(q_ref, k_ref, v_ref, o_ref):
    first = _head_masks()
    q = q_ref[...]
    k = k_ref[...].astype(BF16)
    v = v_ref[...].astype(BF16)
    outs = []
    for a in range(2):
        hm = first if a == 0 else jnp.logical_not(first)
        s = _mm_nt(jnp.where(hm, q, 0.0), k) * (NA_DH ** -0.5)
        m = jnp.max(s, axis=-1, keepdims=True)
        p = jnp.exp(s - m)
        l = jnp.sum(p, axis=-1, keepdims=True)
        outs.append(_mm(p, v) / l)
    o_ref[...] = jnp.where(first, outs[0], outs[1])


def _na_ctx(qkv):
    return pl.pallas_call(
        _na_ctx_kernel,
        grid=(BATCH, NA_HEADS // 2),
        in_specs=[pl.BlockSpec((SEQ, LANES), lambda b, hp: (b, hp)),
                  pl.BlockSpec((SEQ, LANES), lambda b, hp: (b, 8 + hp)),
                  pl.BlockSpec((SEQ, LANES), lambda b, hp: (b, 16 + hp))],
        out_specs=pl.BlockSpec((SEQ, LANES), lambda b, hp: (b, hp)),
        out_shape=jax.ShapeDtypeStruct((NC, D), F32),
        compiler_params=_cparams(("arbitrary", "arbitrary")),
        name="na_ctx",
    )(qkv, qkv, qkv)


def _na_lat_kernel(q_ref, k_ref, v_ref, kc_ref, vc_ref, tt_ref, o_ref):
    first = _head_masks()
    rows = DEC_SEQ // GRID_W
    kctx = kc_ref[...].astype(BF16)
    vctx = vc_ref[...].astype(BF16)
    scale = NA_DH ** -0.5
    nkeys = WIN_R * GRID_W

    def row_body(r, carry):
        r0 = jnp.clip(r - WIN_R // 2, 0, rows - WIN_R)
        dr0 = r0 - r + (WIN_R - 1)
        q = q_ref[pl.ds(pl.multiple_of(r * GRID_W, GRID_W), GRID_W), :]
        kbase = pl.multiple_of(r0 * GRID_W, GRID_W)
        kl = k_ref[pl.ds(kbase, nkeys), :].astype(BF16)
        vl = v_ref[pl.ds(kbase, nkeys), :].astype(BF16)
        outs = []
        for a in range(2):
            hm = first if a == 0 else jnp.logical_not(first)
            qm = jnp.where(hm, q, 0.0)
            bias = jnp.concatenate([tt_ref[a, dr0 + 2 * j] for j in range(WIN_R // 2)], axis=1)
            s_loc = _mm_nt(qm, kl) * scale + bias
            s_ctx = _mm_nt(qm, kctx) * scale
            m = jnp.maximum(jnp.max(s_loc, axis=-1, keepdims=True), jnp.max(s_ctx, axis=-1, keepdims=True))
            p_loc = jnp.exp(s_loc - m)
            p_ctx = jnp.exp(s_ctx - m)
            l = jnp.sum(p_loc, axis=-1, keepdims=True) + jnp.sum(p_ctx, axis=-1, keepdims=True)
            outs.append((_mm(p_loc, vl) + _mm(p_ctx, vctx)) / l)
        o_ref[pl.ds(pl.multiple_of(r * GRID_W, GRID_W), GRID_W), :] = jnp.where(first, outs[0], outs[1])
        return carry

    lax.fori_loop(0, rows, row_body, 0)


def _na_latent(qkv, cache_k2, cache_v2, tt2):
    rb0 = NC // DEC_SEQ
    return pl.pallas_call(
        _na_lat_kernel,
        grid=(DEC_BATCH, NA_HEADS // 2),
        in_specs=[pl.BlockSpec((DEC_SEQ, LANES), lambda b, hp: (rb0 + b, hp)),
                  pl.BlockSpec((DEC_SEQ, LANES), lambda b, hp: (rb0 + b, 8 + hp)),
                  pl.BlockSpec((DEC_SEQ, LANES), lambda b, hp: (rb0 + b, 16 + hp)),
                  pl.BlockSpec((None, 256, LANES), lambda b, hp: (b, 0, hp)),
                  pl.BlockSpec((None, 256, LANES), lambda b, hp: (b, 0, hp)),
                  pl.BlockSpec((2, 2 * WIN_R - 2, GRID_W, LANES), lambda b, hp: (hp, 0, 0, 0))],
        out_specs=pl.BlockSpec((DEC_SEQ, LANES), lambda b, hp: (b, hp)),
        out_shape=jax.ShapeDtypeStruct((NL, D), F32),
        compiler_params=_cparams(("arbitrary", "arbitrary")),
        name="na_latent",
    )(qkv, qkv, qkv, cache_k2, cache_v2, tt2)


def _rpb_tables(rpb):
    col = np.arange(GRID_W)
    start = np.clip(col - WIN_C // 2, 0, GRID_W - WIN_C)
    inside = (col[None, :] >= start[:, None]) & (col[None, :] < start[:, None] + WIN_C)
    dc = np.clip(col[None, :] - col[:, None] + (WIN_C - 1), 0, 2 * WIN_C - 2)
    t = jnp.where(inside[None, None], rpb[:, :, dc], NEG)
    return jnp.concatenate([t[:, :-1], t[:, 1:]], axis=-1)


def _router_kernel(x_ref, m_ref, g_ref, wr_ref, br_ref, tri_ref, xf_ref, ri_ref, rw_ref, cnt_ref, base_scr):
    i = pl.program_id(0)

    @pl.when(i == 0)
    def _():
        base_scr[...] = jnp.zeros_like(base_scr)

    h = _modulated_norm(x_ref[...], m_ref, g_ref, 3, 4)
    xf_ref[...] = h
    logits = _mm_hi(h, wr_ref[...]) + br_ref[...]
    lane = lax.broadcasted_iota(jnp.int32, logits.shape, 1)
    rmax = lambda x: jnp.max(x, axis=-1, keepdims=True)
    rmin = lambda x: jnp.min(x, axis=-1, keepdims=True)
    rsum = lambda x: jnp.sum(x, axis=-1, keepdims=True)

    gmask = lane < N_GROUPS
    mg = rmax(jnp.where(gmask, logits, NEG))
    eg = jnp.where(gmask, jnp.exp(jnp.where(gmask, logits - mg, NEG)), 0.0)
    pg = eg / rsum(eg)
    p_grp = rmax(pg)
    grp = rmin(jnp.where(jnp.logical_and(gmask, pg == p_grp), lane, LANES))
    lo = N_GROUPS + grp * EXP_PER_GROUP
    emask = jnp.logical_and(lane >= lo, lane < lo + EXP_PER_GROUP)
    me = rmax(jnp.where(emask, logits, NEG))
    ee = jnp.where(emask, jnp.exp(jnp.where(emask, logits - me, NEG)), 0.0)
    pe = ee / rsum(ee)
    p1 = rmax(pe)
    i1 = rmin(jnp.where(jnp.logical_and(emask, pe == p1), lane, LANES))
    m2 = jnp.logical_and(emask, lane != i1)
    p2 = rmax(jnp.where(m2, pe, -1.0))
    i2 = rmin(jnp.where(jnp.logical_and(m2, pe == p2), lane, LANES))
    den = p1 + p2
    w1 = p_grp * p1 / den
    w2 = p_grp * p2 / den
    e1 = i1 - N_GROUPS
    e2 = i2 - N_GROUPS

    oh1 = jnp.where(lane == e1, 1.0, 0.0)
    oh2 = jnp.where(lane == e2, 1.0, 0.0)
    c1 = _mm(tri_ref[...], oh1)
    c2 = _mm(tri_ref[...], oh2)
    tot1 = jnp.sum(oh1, axis=0, keepdims=True)
    tot2 = jnp.sum(oh2, axis=0, keepdims=True)
    base = base_scr[0:1, :]
    rank1 = rsum(jnp.where(lane == e1, base + c1, 0.0))
    rank2 = rsum(jnp.where(lane == e2, base + tot1 + c2, 0.0))
    new_base = base + tot1 + tot2
    base_scr[...] = jnp.broadcast_to(new_base, base_scr.shape)
    cnt_ref[...] = jnp.broadcast_to(new_base, cnt_ref.shape)
    ri = jnp.where(lane == 0, e1, jnp.where(lane == 1, e2,
         jnp.where(lane == 2, rank1.astype(jnp.int32), jnp.where(lane == 3, rank2.astype(jnp.int32), 0))))
    ri_ref[...] = ri
    rw_ref[...] = jnp.where(lane == 0, w1, jnp.where(lane == 1, w2, 0.0))


def _router(x, m3, gain, wr, br, tri):
    return pl.pallas_call(
        _router_kernel,
        grid=(N // TM,),
        in_specs=[pl.BlockSpec((TM, D), lambda i: (i, 0)),
                  pl.BlockSpec((None, 6, D), lambda i: (_cond_row(i), 0, 0)),
                  pl.BlockSpec((1, D), lambda i: (0, 0)),
                  pl.BlockSpec((D, LANES), lambda i: (0, 0)),
                  pl.BlockSpec((1, LANES), lambda i: (0, 0)),
                  pl.BlockSpec((TM, TM), lambda i: (0, 0))],
        out_specs=[pl.BlockSpec((TM, D), lambda i: (i, 0)),
                   pl.BlockSpec((TM, LANES), lambda i: (i, 0)),
                   pl.BlockSpec((TM, LANES), lambda i: (i, 0)),
                   pl.BlockSpec((8, LANES), lambda i: (0, 0))],
        out_shape=[jax.ShapeDtypeStruct((N, D), F32),
                   jax.ShapeDtypeStruct((N, LANES), jnp.int32),
                   jax.ShapeDtypeStruct((N, LANES), F32),
                   jax.ShapeDtypeStruct((8, LANES), F32)],
        scratch_shapes=[pltpu.VMEM((8, LANES), F32)],
        compiler_params=_cparams(("arbitrary",)),
        name="router",
    )(x, m3, gain, wr, br, tri)


def _row_copy(src_hbm, row, dst, r, sem):
    return pltpu.make_async_copy(src_hbm.at[pl.ds(row, 1)], dst.at[pl.ds(r, 1)], sem)


def _expert_kernel(be_ref, nu_ref, st_ref, xf_hbm, wg_ref, wu_ref, wd_ref, ys_ref, xbuf, sem):
    del be_ref
    b = pl.program_id(0)

    @pl.when(b < nu_ref[0])
    def _():
        base = b * MOE_ROWS

        def issue(r, c):
            _row_copy(xf_hbm, st_ref[base + r], xbuf, r, sem.at[0]).start()
            return c

        lax.fori_loop(0, MOE_ROWS, issue, 0)

        def wait(r, c):
            _row_copy(xf_hbm, 0, xbuf, r, sem.at[0]).wait()
            return c

        lax.fori_loop(0, MOE_ROWS, wait, 0)
        xb = xbuf[...].astype(BF16)
        g = _mm(xb, wg_ref[...])
        u = _mm(xb, wu_ref[...])
        ys_ref[...] = _mm(_silu(g) * u, wd_ref[...])

    @pl.when(b >= nu_ref[0])
    def _():
        ys_ref[...] = jnp.zeros_like(ys_ref)


def _experts(block_e, n_used, slot_tok, xf, w_gate, w_up, w_down, layer):
    grid_spec = pltpu.PrefetchScalarGridSpec(
        num_scalar_prefetch=3,
        grid=(N_SLOT_BLOCKS,),
        in_specs=[pl.BlockSpec(memory_space=pl.ANY),
                  pl.BlockSpec((None, None, D, D_EXPERT), lambda b, be, nu, st: (layer, be[b], 0, 0)),
                  pl.BlockSpec((None, None, D, D_EXPERT), lambda b, be, nu, st: (layer, be[b], 0, 0)),
                  pl.BlockSpec((None, None, D_EXPERT, D), lambda b, be, nu, st: (layer, be[b], 0, 0))],
        out_specs=pl.BlockSpec((MOE_ROWS, D), lambda b, be, nu, st: (b, 0)),
        scratch_shapes=[pltpu.VMEM((MOE_ROWS, D), F32), pltpu.SemaphoreType.DMA((1,))])
    return pl.pallas_call(
        _expert_kernel,
        grid_spec=grid_spec,
        out_shape=jax.ShapeDtypeStruct((N_SLOT_BLOCKS * MOE_ROWS, D), F32),
        compiler_params=_cparams(("arbitrary",)),
        name="experts",
    )(block_e, n_used, slot_tok, xf, w_gate, w_up, w_down)


def _combine_kernel(d_ref, ys_hbm, x_ref, m_ref, rw_ref, fn_ref, o_ref, buf, sem, *, final):
    i = pl.program_id(0)
    base = i * TM

    def issue(r, c):
        for kk in range(2):
            _row_copy(ys_hbm, d_ref[(base + r) * 2 + kk], buf.at[kk], r, sem.at[0]).start()
        return c

    lax.fori_loop(0, TM, issue, 0)

    def wait(r, c):
        for kk in range(2):
            _row_copy(ys_hbm, 0, buf.at[kk], r, sem.at[0]).wait()
        return c

    lax.fori_loop(0, TM, wait, 0)
    w = rw_ref[...]
    y = w[:, 0:1] * buf[0] + w[:, 1:2] * buf[1]
    out = x_ref[...] + m_ref[5:6, :] * y
    if final:
        ms = jnp.mean(out * out, axis=-1, keepdims=True)
        out = out * lax.rsqrt(ms + EPS) * fn_ref[...]
    o_ref[...] = out


def _combine(dest_flat, ys, x, m3, rw, final_norm, final):
    grid_spec = pltpu.PrefetchScalarGridSpec(
        num_scalar_prefetch=1,
        grid=(N // TM,),
        in_specs=[pl.BlockSpec(memory_space=pl.ANY),
                  pl.BlockSpec((TM, D), lambda i, d: (i, 0)),
                  pl.BlockSpec((None, 6, D), lambda i, d: (_cond_row(i), 0, 0)),
                  pl.BlockSpec((TM, LANES), lambda i, d: (i, 0)),
                  pl.BlockSpec((1, D), lambda i, d: (0, 0))],
        out_specs=pl.BlockSpec((TM, D), lambda i, d: (i, 0)),
        scratch_shapes=[pltpu.VMEM((2, TM, D), F32), pltpu.SemaphoreType.DMA((1,))])
    return pl.pallas_call(
        functools.partial(_combine_kernel, final=final),
        grid_spec=grid_spec,
        out_shape=jax.ShapeDtypeStruct((N, D), F32),
        compiler_params=_cparams(("arbitrary",)),
        name="combine",
    )(dest_flat, ys, x, m3, rw, final_norm)


def _hier_moe(x, m3, gain, layer, w_rg, b_rg, w_re, b_re, w_gate, w_up, w_down, tri_tm, final_norm, final):
    wr = jnp.concatenate([w_rg, w_re.transpose(1, 0, 2).reshape(D, N_EXPERTS),
                          jnp.zeros((D, LANES - N_GROUPS - N_EXPERTS), F32)], axis=1)
    br = jnp.concatenate([b_rg, b_re.reshape(N_EXPERTS),
                          jnp.zeros((LANES - N_GROUPS - N_EXPERTS,), F32)])[None, :]
    xf, ri, rw, cnt = _router(x, m3, gain, wr, br, tri_tm)
    e_idx = ri[:, 0:2]
    rank = ri[:, 2:4]
    counts = cnt[0, :N_EXPERTS].astype(jnp.int32)
    padded = (counts + MOE_ROWS - 1) // MOE_ROWS * MOE_ROWS
    end_pad = jnp.cumsum(padded)
    start_pad = end_pad - padded
    dest = (start_pad[e_idx] + rank).reshape(-1).astype(jnp.int32)
    block_e = jnp.minimum(jnp.searchsorted(end_pad, jnp.arange(N_SLOT_BLOCKS, dtype=jnp.int32) * MOE_ROWS,
                                           side='right'), N_EXPERTS - 1).astype(jnp.int32)
    n_used = (end_pad[-1:] // MOE_ROWS).astype(jnp.int32)
    tok = jnp.repeat(jnp.arange(N, dtype=jnp.int32), 2)
    slot_tok = jnp.zeros((N_SLOT_BLOCKS * MOE_ROWS,), jnp.int32).at[dest].set(tok)
    ys = _experts(block_e, n_used, slot_tok, xf, w_gate, w_up, w_down, layer)
    return _combine(dest, ys, x, m3, rw, final_norm, final)


def kernel(x_prompt, x_sample, state_A_fwd, state_A_bwd, cache_k, cache_v, c, c_ctx, mod_w, mod_b, norm_mix, norm_ffn, ab_w_in, ab_conv, ab_a_log, ab_dt_bias, ab_o_gain, ab_w_out, na_w_qkv, na_rpb, na_w_out, moe_w_rg, moe_b_rg, moe_w_re, moe_b_re, moe_w_gate, moe_w_up, moe_w_down, final_norm):
    x = jnp.concatenate([x_prompt.reshape(NC, D), x_sample.reshape(NL, D)], axis=0)
    cond8 = jnp.concatenate([c_ctx[None, :], c, jnp.zeros((8 - 1 - DEC_BATCH, D), F32)], axis=0)
    mods = _ada_params(cond8, mod_w, mod_b).reshape(DEPTH, 8, 6, D)

    tri_tm = jnp.asarray(np.tril(np.ones((TM, TM)), -1), BF16)
    tri_c = jnp.asarray(np.tril(np.ones((CHUNK, CHUNK))), F32)
    half = np.arange(LANES) < A_DH
    jbd = jnp.asarray((half[:, None] == half[None, :]).astype(np.float32))
    fn = final_norm[None, :]

    m3 = mods[0]
    w_in = ab_w_in[0]
    w_in = jnp.concatenate([w_in[:, :2048], w_in[:, 2080:2592], w_in[:, 2048:2080],
                            jnp.zeros((D, AB_COLS - 2592), F32)], axis=1).astype(BF16)
    proj = _modproj(x, m3, norm_mix[0][None, :], w_in, 896)
    gate_p = jnp.zeros((8, LANES), F32)
    gate_p = gate_p.at[0, :2 * A_HEADS].set(ab_a_log[0].reshape(-1))
    gate_p = gate_p.at[1, :2 * A_HEADS].set(ab_dt_bias[0].reshape(-1))
    o_gain2 = jnp.tile(ab_o_gain[0], 2)[None, :]
    zeros_state = jnp.zeros((BATCH, A_HEADS, A_DH, A_DH), F32)
    mix_a_c, s_f, s_b = _deltanet(proj, ab_conv[0], gate_p, o_gain2, zeros_state, zeros_state, tri_c, jbd,
                                  seq_len=SEQ, n_seq=BATCH, n_sub=4, row_blk0=0)
    mix_a_l, _, _ = _deltanet(proj, ab_conv[0], gate_p, o_gain2, state_A_fwd[:, 0], state_A_bwd[:, 0], tri_c, jbd,
                              seq_len=DEC_SEQ, n_seq=DEC_BATCH, n_sub=1, row_blk0=NC // DEC_SEQ)

    cc, sc = _dft_mats(LANES)
    ct, st = _dft_mats(SEQ)
    c64, s64 = _dft_mats(64)
    cs = jnp.asarray(np.concatenate([cc, sc], axis=1), BF16)
    dft = jnp.asarray(np.concatenate([ct, -st], axis=1), BF16)
    ca = jnp.asarray(np.concatenate([cc, -sc, -sc, -cc], axis=1), BF16)
    m1 = jnp.asarray(np.concatenate([c64, s64], axis=1), BF16)
    tw_idx = np.arange(64)
    tw_ang = 2.0 * np.pi * (tw_idx[:, None] * tw_idx[None, :]) / DEC_SEQ
    twc = jnp.broadcast_to(jnp.asarray(np.cos(tw_ang), F32)[:, :, None], (64, 64, LANES))
    tws = jnp.broadcast_to(jnp.asarray(np.sin(tw_ang), F32)[:, :, None], (64, 64, LANES))
    mix_b_c = _fnet_ctx(proj, cs, dft)
    mix_b_l = _fnet_latent(proj, ca, m1, twc, tws)

    w_out = ab_w_out[0].astype(BF16)
    x = _outproj([(mix_a_c, mix_a_l), (mix_b_c, mix_b_l)], [w_out[:A_WIDTH], w_out[A_WIDTH:]], x, m3)
    x = _hier_moe(x, m3, norm_ffn[0][None, :], 0, moe_w_rg[0], moe_b_rg[0], moe_w_re[0], moe_b_re[0],
                  moe_w_gate, moe_w_up, moe_w_down, tri_tm, fn, False)

    m3 = mods[1]
    qkv = _modproj(x, m3, norm_mix[1][None, :], na_w_qkv[0].astype(BF16), 512)
    attn_c = _na_ctx(qkv)
    attn_l = _na_latent(qkv, cache_k[:, 0].reshape(DEC_BATCH, 256, D), cache_v[:, 0].reshape(DEC_BATCH, 256, D),
                        _rpb_tables(na_rpb[0]))
    x = _outproj([(attn_c, attn_l)], [na_w_out[0].astype(BF16)], x, m3)
    x = _hier_moe(x, m3, norm_ffn[1][None, :], 1, moe_w_rg[1], moe_b_rg[1], moe_w_re[1], moe_b_re[1],
                  moe_w_gate, moe_w_up, moe_w_down, tri_tm, fn, True)

    new_k = qkv[:NC, D:2 * D].reshape(BATCH, 1, SEQ, NA_HEADS, NA_DH)
    new_v = qkv[:NC, 2 * D:].reshape(BATCH, 1, SEQ, NA_HEADS, NA_DH)
    return (x[:NC].reshape(BATCH, SEQ, D), x[NC:].reshape(DEC_BATCH, DEC_SEQ, D),
            s_f[:, None], s_b[:, None], new_k, new_v)
```

```python
import functools
import math

import numpy as np
import jax
import jax.numpy as jnp
from jax import lax
from jax.experimental import pallas as pl
from jax.experimental.pallas import tpu as pltpu

F32 = jnp.float32
BF16 = jnp.bfloat16
HIGHEST = lax.Precision.HIGHEST

D = 1024
BATCH, SEQ = 32, 256
DEC_BATCH, DEC_SEQ = 2, 4096
NC = BATCH * SEQ
NL = DEC_BATCH * DEC_SEQ
N = NC + NL
DEPTH = 2
GRID_W = 64
A_DH = 64
A_HEADS = 8
A_WIDTH = 512
CHUNK = 64
B_WIDTH = 512
B_GROUPS = 4
NA_DH = 64
NA_HEADS = 16
WIN_R, WIN_C = 8, 16
N_GROUPS, EXP_PER_GROUP, N_EXPERTS = 4, 8, 32
D_EXPERT = 512
EPS = 1e-6

LANES = 128
TM = 256
MOE_ROWS = 256
N_SLOT_BLOCKS = (2 * N) // MOE_ROWS + N_EXPERTS
AB_COLS = 2688
VMEM_LIMIT = 56 * 1024 * 1024
EXPERT_VMEM_LIMIT = 60 * 1024 * 1024
NEG = -1e30


def _cparams(sem):
    return pltpu.CompilerParams(dimension_semantics=sem, vmem_limit_bytes=VMEM_LIMIT)


def _mm(a, b):
    return jnp.dot(a.astype(BF16), b.astype(BF16), preferred_element_type=F32)


def _mm_nt(a, b):
    return lax.dot_general(a.astype(BF16), b.astype(BF16), (((1,), (1,)), ((), ())),
                           preferred_element_type=F32)


SOLVE_BLK = 16
TERM_UNROLL = 4


def _unit_lower_solve(a_mat, rhs, same_blk, eye):
    dg = jnp.where(same_blk, a_mat, 0.0)
    p = -dg
    dinv = eye + p
    for _ in range(int(math.log2(SOLVE_BLK)) - 1):
        p = _mm(p, p)
        dinv = dinv + _mm(p, dinv)
    mp = -_mm(dinv, a_mat - dg)
    y = _mm(dinv, rhs)
    y = y + _mm(mp, y)
    for _ in range(int(math.log2(a_mat.shape[0] // SOLVE_BLK)) - 1):
        mp = _mm(mp, mp)
        y = y + _mm(mp, y)
    return y


def _unit_lower_solve_many(a_mats, rhs, same_blk, eye):
    dg = [jnp.where(same_blk, a, 0.0) for a in a_mats]
    off = [a - g for a, g in zip(a_mats, dg)]
    p = [-g for g in dg]
    dinv = [eye + x for x in p]
    for _ in range(int(math.log2(SOLVE_BLK)) - 1):
        p = [_mm(x, x) for x in p]
        dinv = [di + _mm(x, di) for x, di in zip(p, dinv)]
    mp = [-_mm(di, o) for di, o in zip(dinv, off)]
    y = [_mm(di, r) for di, r in zip(dinv, rhs)]
    y = [yi + _mm(m, yi) for m, yi in zip(mp, y)]
    for _ in range(int(math.log2(a_mats[0].shape[0] // SOLVE_BLK)) - 1):
        mp = [_mm(m, m) for m in mp]
        y = [yi + _mm(m, yi) for m, yi in zip(mp, y)]
    return y


def _mm_split(a, b, parts, split_rhs=False):
    x = b if split_rhs else a
    acc = None
    for _ in range(parts):
        piece = x.astype(BF16)
        x = x - piece.astype(F32)
        term = (jnp.dot(a.astype(BF16), piece, preferred_element_type=F32) if split_rhs
                else jnp.dot(piece, b.astype(BF16), preferred_element_type=F32))
        acc = term if acc is None else acc + term
    return acc


def _mm_hi(a, b):
    return jnp.dot(a, b, preferred_element_type=F32, precision=HIGHEST)


def _silu(x):
    return x * jax.nn.sigmoid(x)


def _bf16_bits(x):
    b = lax.bitcast_convert_type(x, jnp.int32)
    return b + 0x7FFF + (lax.shift_right_logical(b, jnp.int32(16)) & 1)


_HIGH16 = -65536


def _pack_bf16_pairs(a, b):
    return lax.shift_right_logical(_bf16_bits(a), jnp.int32(16)) | (_bf16_bits(b) & _HIGH16)


def _unpack_bf16_pairs(p):
    a = lax.bitcast_convert_type(lax.shift_left(p, jnp.int32(16)), F32)
    b = lax.bitcast_convert_type(p & _HIGH16, F32)
    return a.astype(BF16), b.astype(BF16)


def _cond_row(i):
    return jnp.where(i < NC // TM, 0, 1 + (i - NC // TM) // (DEC_SEQ // TM))


def _modulated_norm(x, m_ref, g_ref, shift_idx, scale_idx):
    ms = jnp.mean(x * x, axis=-1, keepdims=True)
    y = x * lax.rsqrt(ms + EPS) * g_ref[...]
    return y * (1.0 + m_ref[scale_idx:scale_idx + 1, :]) + m_ref[shift_idx:shift_idx + 1, :]


def _ada_kernel(cond_ref, w_ref, b_ref, o_ref):
    o_ref[...] = _mm_hi(_silu(cond_ref[...]), w_ref[...]) + b_ref[...]


def _ada_params(cond8, mod_w, mod_b):
    tn = 1536
    return pl.pallas_call(
        _ada_kernel,
        grid=(DEPTH, 6 * D // tn),
        in_specs=[pl.BlockSpec((8, D), lambda l, j: (0, 0)),
                  pl.BlockSpec((None, D, tn), lambda l, j: (l, 0, j)),
                  pl.BlockSpec((None, 1, tn), lambda l, j: (l, 0, j))],
        out_specs=pl.BlockSpec((None, 8, tn), lambda l, j: (l, 0, j)),
        out_shape=jax.ShapeDtypeStruct((DEPTH, 8, 6 * D), F32),
        compiler_params=_cparams(("arbitrary", "arbitrary")),
        name="ada_params",
    )(cond8, mod_w, mod_b.reshape(DEPTH, 1, 6 * D))


def _modproj_kernel(x_ref, m_ref, g_ref, w_ref, o_ref, *, n_chunk):
    hb = _modulated_norm(x_ref[...], m_ref, g_ref, 0, 1).astype(BF16)
    for j in range(o_ref.shape[1] // n_chunk):
        sl = slice(j * n_chunk, (j + 1) * n_chunk)
        o_ref[:, sl] = jnp.dot(hb, w_ref[:, sl], preferred_element_type=F32)


def _modproj(x, m3, gain, w_bf16, n_chunk):
    nout = w_bf16.shape[1]
    return pl.pallas_call(
        functools.partial(_modproj_kernel, n_chunk=n_chunk),
        grid=(N // TM,),
        in_specs=[pl.BlockSpec((TM, D), lambda i: (i, 0)),
                  pl.BlockSpec((None, 6, D), lambda i: (_cond_row(i), 0, 0)),
                  pl.BlockSpec((1, D), lambda i: (0, 0)),
                  pl.BlockSpec((D, nout), lambda i: (0, 0))],
        out_specs=pl.BlockSpec((TM, nout), lambda i: (i, 0)),
        out_shape=jax.ShapeDtypeStruct((N, nout), F32),
        compiler_params=_cparams(("arbitrary",)),
        name="modproj",
    )(x, m3, gain, w_bf16)


def _outproj_kernel(*refs, n_in):
    a_refs = refs[:2 * n_in]
    w_refs = refs[2 * n_in:3 * n_in]
    x_ref, m_ref, o_ref = refs[3 * n_in:]
    is_ctx = pl.program_id(0) < NC // TM
    acc = None
    for j, w_ref in enumerate(w_refs):
        a = jnp.where(is_ctx, a_refs[2 * j][...], a_refs[2 * j + 1][...])
        part = _mm(a, w_ref[...])
        acc = part if acc is None else acc + part
    o_ref[...] = x_ref[...] + m_ref[2:3, :] * acc


def _outproj(a_pairs, w_list, x, m3):
    n_in = len(a_pairs)
    nct = NC // TM
    in_specs, args = [], []
    for a_ctx, a_lat in a_pairs:
        in_specs.append(pl.BlockSpec((TM, a_ctx.shape[1]), lambda i: (jnp.minimum(i, nct - 1), 0)))
        in_specs.append(pl.BlockSpec((TM, a_lat.shape[1]), lambda i: (jnp.maximum(i - nct, 0), 0)))
        args += [a_ctx, a_lat]
    in_specs += ([pl.BlockSpec(w.shape, lambda i: (0, 0)) for w in w_list]
                 + [pl.BlockSpec((TM, D), lambda i: (i, 0)),
                    pl.BlockSpec((None, 6, D), lambda i: (_cond_row(i), 0, 0))])
    return pl.pallas_call(
        functools.partial(_outproj_kernel, n_in=n_in),
        grid=(N // TM,),
        in_specs=in_specs,
        out_specs=pl.BlockSpec((TM, D), lambda i: (i, 0)),
        out_shape=jax.ShapeDtypeStruct((N, D), F32),
        compiler_params=_cparams(("arbitrary",)),
        name="outproj",
    )(*args, *w_list, x, m3)


def _deltanet_kernel(q_ref, k_ref, v_ref, z_ref, ab_ref, cq_ref, ck_ref, cv_ref, gp_ref, og_ref,
                     s0f_ref, s0b_ref, tri_ref, jbd_ref, o_ref, sf_ref, sb_ref,
                     u_s, w_s, qd_s, at_s, kt_s, ge_s, st_s, ob, *, seq_len, n_sub):
    hp = pl.program_id(1)
    C = CHUNK
    nc = seq_len // C
    nct = n_sub * nc
    P = LANES
    lane = lax.broadcasted_iota(jnp.int32, (C, P), 1)
    row = lax.broadcasted_iota(jnp.int32, (C, P), 0)
    first_head = lane < A_DH
    ri = lax.broadcasted_iota(jnp.int32, (P, P), 0)
    ci = lax.broadcasted_iota(jnp.int32, (P, P), 1)
    same_head = (ri < C) == (ci < C)
    same_blk = (ri // SOLVE_BLK) == (ci // SOLVE_BLK)
    eye = jnp.where(ri == ci, 1.0, 0.0)
    jbd = jbd_ref[...]
    lincl = tri_ref[...]
    tri_b = [lincl.astype(BF16), lincl.T.astype(BF16)]
    incl_m = [jnp.logical_and(same_head, ri >= ci), jnp.logical_and(same_head, ri <= ci)]
    strict_m = [jnp.logical_and(same_head, ri > ci), jnp.logical_and(same_head, ri < ci)]
    neg_a = -jnp.exp(gp_ref[0:1, :])
    dt_b = gp_ref[1:2, :]

    def conv_silu(ref, w_ref, c):
        base = pl.multiple_of(c * C, C)
        cs = c % nc
        xc = ref[pl.ds(base, C), :]
        pbase = pl.multiple_of(jnp.maximum(base - 8, 0), 8)
        nbase = pl.multiple_of(jnp.minimum(base + C, n_sub * seq_len - 8), 8)
        prev_row = ref[pl.ds(pbase, 8), :][7:8, :] * jnp.where(cs > 0, 1.0, 0.0)
        next_row = ref[pl.ds(nbase, 8), :][0:1, :] * jnp.where(cs < nc - 1, 1.0, 0.0)
        x_prev = jnp.where(row == 0, prev_row, pltpu.roll(xc, 1, 0))
        x_next = jnp.where(row == C - 1, next_row, pltpu.roll(xc, C - 1, 0))
        y = w_ref[0:1, :] * x_prev + w_ref[1:2, :] * xc + w_ref[2:3, :] * x_next
        return _silu(y)

    def stack(x):
        return jnp.concatenate([jnp.where(first_head, x, 0.0), jnp.where(first_head, 0.0, x)], axis=0)

    def lane_col(arr, idx):
        return jnp.sum(jnp.where(lane == idx, arr, 0.0), axis=-1, keepdims=True)

    def chunk_inputs(c):
        base = pl.multiple_of(c * C, C)
        q = conv_silu(q_ref, cq_ref, c)
        k = conv_silu(k_ref, ck_ref, c)
        v = conv_silu(v_ref, cv_ref, c)
        ab = ab_ref[pl.ds(base, C), :]
        g_all = neg_a * jax.nn.softplus(ab + dt_b)
        beta_all = jax.nn.sigmoid(ab)
        return q, k, v, g_all, beta_all

    def chain_gates(g_all, beta_all, gc_all, d):
        cols = []
        for a in range(2):
            g_lane = d * A_HEADS + 2 * hp + a
            g_col = lane_col(g_all, g_lane)
            tot = jnp.broadcast_to(jnp.sum(g_col, axis=0, keepdims=True), (C, 1))
            cols.append((lane_col(gc_all, g_lane), lane_col(beta_all, 2 * A_HEADS + g_lane), tot))
        return [jnp.concatenate([cols[0][j], cols[1][j]], axis=0) for j in range(3)]

    def terms_body(j, carry):
        cs = [TERM_UNROLL * j + t for t in range(TERM_UNROLL)]
        ins = [chunk_inputs(c) for c in cs]
        qsq = [_mm_split(x[0] * x[0], jbd, 2) for x in ins]
        ksq = [_mm_split(x[1] * x[1], jbd, 2) for x in ins]
        qs = [x[0] * lax.rsqrt(s + EPS) * (A_DH ** -0.5) for x, s in zip(ins, qsq)]
        ks = [x[1] * lax.rsqrt(s + EPS) for x, s in zip(ins, ksq)]
        qst = [stack(x) for x in qs]
        kst = [stack(x) for x in ks]
        vst = [stack(x[2]) for x in ins]
        kk = [_mm_nt(x, x) for x in kst]
        qk = [_mm_nt(x, y) for x, y in zip(qst, kst)]
        chains = [(t, d) for t in range(TERM_UNROLL) for d in range(2)]
        gc_all = [_mm_split(tri_b[d], ins[t][3], 3, split_rhs=True) for t, d in chains]
        gates = [chain_gates(ins[t][3], ins[t][4], gc, d) for (t, d), gc in zip(chains, gc_all)]
        decay, e_gc = [], []
        for (t, d), (gc_col, beta_col, tot_col) in zip(chains, gates):
            gcb = jnp.broadcast_to(gc_col, (P, P))
            decay.append(jnp.where(incl_m[d], jnp.exp(jnp.where(incl_m[d], gcb - gcb.T, 0.0)), 0.0))
            e_gc.append(jnp.exp(gc_col))
        a_mats = [jnp.where(strict_m[d], g[1] * kk[t] * dc, 0.0) for (t, d), g, dc in zip(chains, gates, decay)]
        rhs = [jnp.concatenate([vst[t] * g[1], kst[t] * (g[1] * e)], axis=1)
               for (t, d), g, e in zip(chains, gates, e_gc)]
        xs = _unit_lower_solve_many(a_mats, rhs, same_blk, eye)
        for (t, d), x, g, e, dc in zip(chains, xs, gates, e_gc, decay):
            c = cs[t]
            u_s[d, c] = x[:, :P].astype(BF16)
            w_s[d, c] = x[:, P:].astype(BF16)
            qd_s[d, c] = (qst[t] * e).astype(BF16)
            at_s[d, c] = jnp.where(incl_m[d], qk[t] * dc, 0.0).astype(BF16)
            kt_s[d, c] = (kst[t] * jnp.exp(g[2] - g[0])).T.astype(BF16)
            e_tot = jnp.exp(g[2])
            ge_s[d, c] = jnp.where(lax.broadcasted_iota(jnp.int32, (8, P), 1) < A_DH,
                                   e_tot[0:1, :], e_tot[C:C + 1, :])
        return carry

    lax.fori_loop(0, nct // TERM_UNROLL, terms_body, 0)

    def block_diag(s2):
        z = jnp.zeros((A_DH, A_DH), F32)
        return jnp.concatenate([jnp.concatenate([s2[0], z], axis=1),
                                jnp.concatenate([z, s2[1]], axis=1)], axis=0)

    for s in range(n_sub):
        st_s[2 * s] = block_diag(s0f_ref[s])
        st_s[2 * s + 1] = block_diag(s0b_ref[s])

    def scan_body(i, carry):
        chains = [(s, d, s * nc + (i if d == 0 else nc - 1 - i)) for s in range(n_sub) for d in range(2)]
        dot = functools.partial(jnp.dot, preferred_element_type=F32)
        s_bd = [st_s[2 * s + d] for s, d, c in chains]
        sb16 = [x.astype(BF16) for x in s_bd]
        ws = [dot(w_s[d, c], sb) for (s, d, c), sb in zip(chains, sb16)]
        qs_ = [dot(qd_s[d, c], sb) for (s, d, c), sb in zip(chains, sb16)]
        vb = [(u_s[d, c].astype(F32) - x).astype(BF16) for (s, d, c), x in zip(chains, ws)]
        av = [dot(at_s[d, c], x) for (s, d, c), x in zip(chains, vb)]
        kv = [dot(kt_s[d, c], x) for (s, d, c), x in zip(chains, vb)]
        for (s, d, c), sb, q_, a_, k_ in zip(chains, s_bd, qs_, av, kv):
            st_s[2 * s + d] = sb * ge_s[d, c][0:1, :] + k_
            o_st = q_ + a_
            dst = o_ref if d == 0 else ob
            dst[pl.ds(pl.multiple_of(c * C, C), C), :] = o_st[:C] + o_st[C:]
        return carry

    lax.fori_loop(0, nc, scan_body, 0)

    for s in range(n_sub):
        for d, ref in ((0, sf_ref), (1, sb_ref)):
            s_bd = st_s[2 * s + d]
            ref[s, 0] = s_bd[:A_DH, :A_DH]
            ref[s, 1] = s_bd[A_DH:, A_DH:]

    def finish(c, carry):
        base = pl.multiple_of(c * C, C)
        o = o_ref[pl.ds(base, C), :] + ob[pl.ds(base, C), :]
        ms = _mm_split(o * o, jbd, 2) * (1.0 / A_DH)
        y = o * lax.rsqrt(ms + EPS) * og_ref[...]
        o_ref[pl.ds(base, C), :] = y * _silu(z_ref[pl.ds(base, C), :])
        return carry

    lax.fori_loop(0, nct, finish, 0)


def _deltanet(proj, conv_w, gate_p, o_gain2, s0f, s0b, tri, jbd, *, seq_len, n_seq, n_sub, row_blk0):
    rows = n_sub * seq_len
    nct = rows // CHUNK
    rb = lambda b: row_blk0 + b
    col = lambda off: (lambda b, hp: (rb(b), off + hp))
    st_spec = pl.BlockSpec((n_sub, 2, A_DH, A_DH), lambda b, hp: (b, hp, 0, 0))
    in_specs = [pl.BlockSpec((rows, LANES), col(0)),
                pl.BlockSpec((rows, LANES), col(4)),
                pl.BlockSpec((rows, LANES), col(8)),
                pl.BlockSpec((rows, LANES), col(12)),
                pl.BlockSpec((rows, LANES), lambda b, hp: (rb(b), 20)),
                pl.BlockSpec((3, LANES), lambda b, hp: (0, hp)),
                pl.BlockSpec((3, LANES), lambda b, hp: (0, 4 + hp)),
                pl.BlockSpec((3, LANES), lambda b, hp: (0, 8 + hp)),
                pl.BlockSpec((8, LANES), lambda b, hp: (0, 0)),
                pl.BlockSpec((1, LANES), lambda b, hp: (0, 0)),
                st_spec, st_spec,
                pl.BlockSpec((CHUNK, CHUNK), lambda b, hp: (0, 0)),
                pl.BlockSpec((LANES, LANES), lambda b, hp: (0, 0))]
    args = [proj, proj, proj, proj, proj, conv_w, conv_w, conv_w, gate_p, o_gain2, s0f, s0b, tri, jbd]
    st_shape = jax.ShapeDtypeStruct((n_seq, A_HEADS, A_DH, A_DH), F32)
    tile = lambda dt: pltpu.VMEM((2, nct, LANES, LANES), dt)
    return pl.pallas_call(
        functools.partial(_deltanet_kernel, seq_len=seq_len, n_sub=n_sub),
        grid=(n_seq // n_sub, A_HEADS // 2),
        in_specs=in_specs,
        out_specs=[pl.BlockSpec((rows, LANES), lambda b, hp: (b, hp)), st_spec, st_spec],
        out_shape=[jax.ShapeDtypeStruct((n_seq * seq_len, A_WIDTH), F32), st_shape, st_shape],
        scratch_shapes=[tile(BF16), tile(BF16), tile(BF16), tile(BF16), tile(BF16),
                        pltpu.VMEM((2, nct, 8, LANES), F32),
                        pltpu.VMEM((2 * n_sub, LANES, LANES), F32),
                        pltpu.VMEM((rows, LANES), F32)],
        compiler_params=_cparams(("arbitrary", "arbitrary")),
        name="deltanet",
    )(*args)


def _dft_mats(n):
    idx = np.arange(n)
    ang = 2.0 * np.pi * ((idx[:, None] * idx[None, :]) % n) / n
    return np.cos(ang), np.sin(ang)


def _fnet_ctx_kernel(u_ref, cs_ref, dft_ref, o_ref):
    norm = 1.0 / math.sqrt(SEQ * LANES)
    for g in range(B_GROUPS):
        sl = slice(g * LANES, (g + 1) * LANES)
        p = _mm(u_ref[:, sl], cs_ref[...])
        stack = jnp.concatenate([p[:, :LANES], p[:, LANES:]], axis=0)
        o_ref[:, sl] = _mm(dft_ref[...], stack) * norm


def _fnet_ctx(proj, cs, dft):
    return pl.pallas_call(
        _fnet_ctx_kernel,
        grid=(BATCH,),
        in_specs=[pl.BlockSpec((SEQ, B_WIDTH), lambda b: (b, 4)),
                  pl.BlockSpec(cs.shape, lambda b: (0, 0)),
                  pl.BlockSpec(dft.shape, lambda b: (0, 0))],
        out_specs=pl.BlockSpec((SEQ, B_WIDTH), lambda b: (b, 0)),
        out_shape=jax.ShapeDtypeStruct((NC, B_WIDTH), F32),
        compiler_params=_cparams(("arbitrary",)),
        name="fnet_ctx",
    )(proj, cs, dft)


def _fnet_lat1_kernel(u_ref, ca_ref, m1_ref, twc_ref, tws_ref, o_ref):
    c = twc_ref[...]
    s = tws_ref[...]
    for g in range(B_GROUPS):
        pa = _mm(u_ref[:, g * LANES:(g + 1) * LANES], ca_ref[...])
        rhs = jnp.concatenate([pa[:, :2 * LANES], pa[:, 2 * LANES:]], axis=0)
        zz = _mm(m1_ref[...], rhs)
        zr, zi = zz[:, :LANES], zz[:, LANES:]
        o_ref[:, 2 * g * LANES:(2 * g + 1) * LANES] = zr * c + zi * s
        o_ref[:, (2 * g + 1) * LANES:(2 * g + 2) * LANES] = zi * c - zr * s


def _fnet_lat2_kernel(z_ref, m1_ref, o_ref):
    norm = 1.0 / math.sqrt(DEC_SEQ * LANES)
    for g in range(B_GROUPS):
        rhs = jnp.concatenate([z_ref[:, 2 * g * LANES:(2 * g + 1) * LANES],
                               z_ref[:, (2 * g + 1) * LANES:(2 * g + 2) * LANES]], axis=0)
        o_ref[:, g * LANES:(g + 1) * LANES] = _mm(m1_ref[...], rhs) * norm


def _fnet_latent(proj, ca, m1, twc, tws):
    r = 64
    u = proj[NC:, 2048:2560].reshape(DEC_BATCH, r, r, B_WIDTH)
    u = u.transpose(0, 2, 1, 3).reshape(DEC_BATCH * r * r, B_WIDTH)
    steps = DEC_BATCH * r
    z = pl.pallas_call(
        _fnet_lat1_kernel,
        grid=(steps,),
        in_specs=[pl.BlockSpec((r, B_WIDTH), lambda s: (s, 0)),
                  pl.BlockSpec(ca.shape, lambda s: (0, 0)),
                  pl.BlockSpec(m1.shape, lambda s: (0, 0)),
                  pl.BlockSpec((None, r, LANES), lambda s: (s % r, 0, 0)),
                  pl.BlockSpec((None, r, LANES), lambda s: (s % r, 0, 0))],
        out_specs=pl.BlockSpec((r, 2 * B_WIDTH), lambda s: (s, 0)),
        out_shape=jax.ShapeDtypeStruct((NL, 2 * B_WIDTH), F32),
        compiler_params=_cparams(("arbitrary",)),
        name="fnet_lat1",
    )(u, ca, m1, twc, tws)
    z = z.reshape(DEC_BATCH, r, r, 2 * B_WIDTH).transpose(0, 2, 1, 3).reshape(NL, 2 * B_WIDTH)
    y = pl.pallas_call(
        _fnet_lat2_kernel,
        grid=(steps,),
        in_specs=[pl.BlockSpec((r, 2 * B_WIDTH), lambda s: (s, 0)),
                  pl.BlockSpec(m1.shape, lambda s: (0, 0))],
        out_specs=pl.BlockSpec((r, B_WIDTH), lambda s: (s, 0)),
        out_shape=jax.ShapeDtypeStruct((NL, B_WIDTH), F32),
        compiler_params=_cparams(("arbitrary",)),
        name="fnet_lat2",
    )(z, m1)
    return y.reshape(DEC_BATCH, r, r, B_WIDTH).transpose(0, 2, 1, 3).reshape(NL, B_WIDTH)


def _head_masks():
    lane = lax.broadcasted_iota(jnp.int32, (1, LANES), 1)
    return lane < NA_DH


def _na_ctx_kernel(q_ref, k_ref, v_ref, o_ref):
    first = _head_masks()
    q = q_ref[...]
    k = k_ref[...].astype(BF16)
    v = v_ref[...].astype(BF16)
    outs = []
    for a in range(2):
        hm = first if a == 0 else jnp.logical_not(first)
        s = _mm_nt(jnp.where(hm, q, 0.0), k) * (NA_DH ** -0.5)
        m = jnp.max(s, axis=-1, keepdims=True)
        p = jnp.exp(s - m)
        l = jnp.sum(p, axis=-1, keepdims=True)
        outs.append(_mm(p, v) / l)
    o_ref[...] = jnp.where(first, outs[0], outs[1])


def _na_ctx(qkv):
    return pl.pallas_call(
        _na_ctx_kernel,
        grid=(BATCH, NA_HEADS // 2),
        in_specs=[pl.BlockSpec((SEQ, LANES), lambda b, hp: (b, hp)),
                  pl.BlockSpec((SEQ, LANES), lambda b, hp: (b, 8 + hp)),
                  pl.BlockSpec((SEQ, LANES), lambda b, hp: (b, 16 + hp))],
        out_specs=pl.BlockSpec((SEQ, LANES), lambda b, hp: (b, hp)),
        out_shape=jax.ShapeDtypeStruct((NC, D), F32),
        compiler_params=_cparams(("arbitrary", "arbitrary")),
        name="na_ctx",
    )(qkv, qkv, qkv)


def _na_lat_kernel(q_ref, k_ref, v_ref, kc_ref, vc_ref, tt_ref, o_ref):
    first = _head_masks()
    rows = DEC_SEQ // GRID_W
    kctx = kc_ref[...].astype(BF16)
    vctx = vc_ref[...].astype(BF16)
    scale = NA_DH ** -0.5
    nkeys = WIN_R * GRID_W

    def row_body(r, carry):
        r0 = jnp.clip(r - WIN_R // 2, 0, rows - WIN_R)
        dr0 = r0 - r + (WIN_R - 1)
        q = q_ref[pl.ds(pl.multiple_of(r * GRID_W, GRID_W), GRID_W), :]
        kbase = pl.multiple_of(r0 * GRID_W, GRID_W)
        kl = k_ref[pl.ds(kbase, nkeys), :].astype(BF16)
        vl = v_ref[pl.ds(kbase, nkeys), :].astype(BF16)
        outs = []
        for a in range(2):
            hm = first if a == 0 else jnp.logical_not(first)
            qm = jnp.where(hm, q, 0.0)
            bias = jnp.concatenate([tt_ref[a, dr0 + 2 * j] for j in range(WIN_R // 2)], axis=1)
            s_loc = _mm_nt(qm, kl) * scale + bias
            s_ctx = _mm_nt(qm, kctx) * scale
            m = jnp.maximum(jnp.max(s_loc, axis=-1, keepdims=True), jnp.max(s_ctx, axis=-1, keepdims=True))
            p_loc = jnp.exp(s_loc - m)
            p_ctx = jnp.exp(s_ctx - m)
            l = jnp.sum(p_loc, axis=-1, keepdims=True) + jnp.sum(p_ctx, axis=-1, keepdims=True)
            outs.append((_mm(p_loc, vl) + _mm(p_ctx, vctx)) / l)
        o_ref[pl.ds(pl.multiple_of(r * GRID_W, GRID_W), GRID_W), :] = jnp.where(first, outs[0], outs[1])
        return carry

    lax.fori_loop(0, rows, row_body, 0)


def _na_latent(qkv, cache_k2, cache_v2, tt2):
    rb0 = NC // DEC_SEQ
    return pl.pallas_call(
        _na_lat_kernel,
        grid=(DEC_BATCH, NA_HEADS // 2),
        in_specs=[pl.BlockSpec((DEC_SEQ, LANES), lambda b, hp: (rb0 + b, hp)),
                  pl.BlockSpec((DEC_SEQ, LANES), lambda b, hp: (rb0 + b, 8 + hp)),
                  pl.BlockSpec((DEC_SEQ, LANES), lambda b, hp: (rb0 + b, 16 + hp)),
                  pl.BlockSpec((None, 256, LANES), lambda b, hp: (b, 0, hp)),
                  pl.BlockSpec((None, 256, LANES), lambda b, hp: (b, 0, hp)),
                  pl.BlockSpec((2, 2 * WIN_R - 2, GRID_W, LANES), lambda b, hp: (hp, 0, 0, 0))],
        out_specs=pl.BlockSpec((DEC_SEQ, LANES), lambda b, hp: (b, hp)),
        out_shape=jax.ShapeDtypeStruct((NL, D), F32),
        compiler_params=_cparams(("arbitrary", "arbitrary")),
        name="na_latent",
    )(qkv, qkv, qkv, cache_k2, cache_v2, tt2)


def _rpb_tables(rpb):
    col = np.arange(GRID_W)
    start = np.clip(col - WIN_C // 2, 0, GRID_W - WIN_C)
    inside = (col[None, :] >= start[:, None]) & (col[None, :] < start[:, None] + WIN_C)
    dc = np.clip(col[None, :] - col[:, None] + (WIN_C - 1), 0, 2 * WIN_C - 2)
    t = jnp.where(inside[None, None], rpb[:, :, dc], NEG)
    return jnp.concatenate([t[:, :-1], t[:, 1:]], axis=-1)


def _router_kernel(x_ref, m_ref, g_ref, wr_ref, br_ref, tri_ref, xf_ref, ri_ref, rw_ref, cnt_ref, base_scr):
    i = pl.program_id(0)

    @pl.when(i == 0)
    def _():
        base_scr[...] = jnp.zeros_like(base_scr)

    h = _modulated_norm(x_ref[...], m_ref, g_ref, 3, 4)
    xf_ref[...] = _pack_bf16_pairs(h[:, :D // 2], h[:, D // 2:])
    logits = _mm_hi(h, wr_ref[...]) + br_ref[...]
    lane = lax.broadcasted_iota(jnp.int32, logits.shape, 1)
    rmax = lambda x: jnp.max(x, axis=-1, keepdims=True)
    rmin = lambda x: jnp.min(x, axis=-1, keepdims=True)
    rsum = lambda x: jnp.sum(x, axis=-1, keepdims=True)

    gmask = lane < N_GROUPS
    mg = rmax(jnp.where(gmask, logits, NEG))
    eg = jnp.where(gmask, jnp.exp(jnp.where(gmask, logits - mg, NEG)), 0.0)
    pg = eg / rsum(eg)
    p_grp = rmax(pg)
    grp = rmin(jnp.where(jnp.logical_and(gmask, pg == p_grp), lane, LANES))
    lo = N_GROUPS + grp * EXP_PER_GROUP
    emask = jnp.logical_and(lane >= lo, lane < lo + EXP_PER_GROUP)
    me = rmax(jnp.where(emask, logits, NEG))
    ee = jnp.where(emask, jnp.exp(jnp.where(emask, logits - me, NEG)), 0.0)
    pe = ee / rsum(ee)
    p1 = rmax(pe)
    i1 = rmin(jnp.where(jnp.logical_and(emask, pe == p1), lane, LANES))
    m2 = jnp.logical_and(emask, lane != i1)
    p2 = rmax(jnp.where(m2, pe, -1.0))
    i2 = rmin(jnp.where(jnp.logical_and(m2, pe == p2), lane, LANES))
    den = p1 + p2
    w1 = p_grp * p1 / den
    w2 = p_grp * p2 / den
    e1 = i1 - N_GROUPS
    e2 = i2 - N_GROUPS

    oh1 = jnp.where(lane == e1, 1.0, 0.0)
    oh2 = jnp.where(lane == e2, 1.0, 0.0)
    c1 = _mm(tri_ref[...], oh1)
    c2 = _mm(tri_ref[...], oh2)
    tot1 = jnp.sum(oh1, axis=0, keepdims=True)
    tot2 = jnp.sum(oh2, axis=0, keepdims=True)
    base = base_scr[0:1, :]
    rank1 = rsum(jnp.where(lane == e1, base + c1, 0.0))
    rank2 = rsum(jnp.where(lane == e2, base + tot1 + c2, 0.0))
    new_base = base + tot1 + tot2
    base_scr[...] = jnp.broadcast_to(new_base, base_scr.shape)
    cnt_ref[...] = jnp.broadcast_to(new_base, cnt_ref.shape)
    ri = jnp.where(lane == 0, e1, jnp.where(lane == 1, e2,
         jnp.where(lane == 2, rank1.astype(jnp.int32), jnp.where(lane == 3, rank2.astype(jnp.int32), 0))))
    ri_ref[...] = ri
    rw_ref[...] = jnp.where(lane == 0, w1, jnp.where(lane == 1, w2, 0.0))


def _router(x, m3, gain, wr, br, tri):
    return pl.pallas_call(
        _router_kernel,
        grid=(N // TM,),
        in_specs=[pl.BlockSpec((TM, D), lambda i: (i, 0)),
                  pl.BlockSpec((None, 6, D), lambda i: (_cond_row(i), 0, 0)),
                  pl.BlockSpec((1, D), lambda i: (0, 0)),
                  pl.BlockSpec((D, LANES), lambda i: (0, 0)),
                  pl.BlockSpec((1, LANES), lambda i: (0, 0)),
                  pl.BlockSpec((TM, TM), lambda i: (0, 0))],
        out_specs=[pl.BlockSpec((TM, D // 2), lambda i: (i, 0)),
                   pl.BlockSpec((TM, LANES), lambda i: (i, 0)),
                   pl.BlockSpec((TM, LANES), lambda i: (i, 0)),
                   pl.BlockSpec((8, LANES), lambda i: (0, 0))],
        out_shape=[jax.ShapeDtypeStruct((N, D // 2), jnp.int32),
                   jax.ShapeDtypeStruct((N, LANES), jnp.int32),
                   jax.ShapeDtypeStruct((N, LANES), F32),
                   jax.ShapeDtypeStruct((8, LANES), F32)],
        scratch_shapes=[pltpu.VMEM((8, LANES), F32)],
        compiler_params=_cparams(("arbitrary",)),
        name="router",
    )(x, m3, gain, wr, br, tri)


def _row_copy(src_hbm, row, dst, r, sem):
    return pltpu.make_async_copy(src_hbm.at[pl.ds(row, 1)], dst.at[pl.ds(r, 1)], sem)


def _expert_kernel(be_ref, nu_ref, st_ref, xp_hbm, wg_ref, wu_ref, wd_ref, ys_ref, xres, xbuf, sem):
    del be_ref
    b = pl.program_id(0)
    half = D // 2

    def gather(blk, slot):
        base = blk * MOE_ROWS
        for r in range(MOE_ROWS):
            xbuf[slot, pl.ds(r, 1), :] = xres[pl.ds(st_ref[base + r], 1), :]

    @pl.when(b == 0)
    def _():
        cp = pltpu.make_async_copy(xp_hbm, xres, sem.at[0])
        cp.start()
        cp.wait()
        gather(0, 0)

    @pl.when(b < nu_ref[0])
    def _():
        gather(jnp.minimum(b + 1, N_SLOT_BLOCKS - 1), (b + 1) % 2)
        x_lo, x_hi = _unpack_bf16_pairs(xbuf[b % 2])
        dot = functools.partial(jnp.dot, preferred_element_type=F32)
        g = dot(x_lo, wg_ref[:half, :].astype(BF16)) + dot(x_hi, wg_ref[half:, :].astype(BF16))
        u = dot(x_lo, wu_ref[:half, :].astype(BF16)) + dot(x_hi, wu_ref[half:, :].astype(BF16))
        ys_ref[...] = _mm(_silu(g) * u, wd_ref[...])

    @pl.when(b >= nu_ref[0])
    def _():
        ys_ref[...] = jnp.zeros_like(ys_ref)


def _experts(block_e, n_used, slot_tok, xp, w_gate, w_up, w_down, layer):
    grid_spec = pltpu.PrefetchScalarGridSpec(
        num_scalar_prefetch=3,
        grid=(N_SLOT_BLOCKS,),
        in_specs=[pl.BlockSpec(memory_space=pl.ANY),
                  pl.BlockSpec((None, None, D, D_EXPERT), lambda b, be, nu, st: (layer, be[b], 0, 0)),
                  pl.BlockSpec((None, None, D, D_EXPERT), lambda b, be, nu, st: (layer, be[b], 0, 0)),
                  pl.BlockSpec((None, None, D_EXPERT, D), lambda b, be, nu, st: (layer, be[b], 0, 0))],
        out_specs=pl.BlockSpec((MOE_ROWS, D), lambda b, be, nu, st: (b, 0)),
        scratch_shapes=[pltpu.VMEM((N, D // 2), jnp.int32),
                        pltpu.VMEM((2, MOE_ROWS, D // 2), jnp.int32),
                        pltpu.SemaphoreType.DMA((1,))])
    return pl.pallas_call(
        _expert_kernel,
        grid_spec=grid_spec,
        out_shape=jax.ShapeDtypeStruct((N_SLOT_BLOCKS * MOE_ROWS, D), F32),
        compiler_params=pltpu.CompilerParams(dimension_semantics=("arbitrary",), vmem_limit_bytes=EXPERT_VMEM_LIMIT),
        name="experts",
    )(block_e, n_used, slot_tok, xp, w_gate, w_up, w_down)


def _combine_kernel(d_ref, ys_hbm, x_ref, m_ref, rw_ref, fn_ref, o_ref, buf, sem, *, final):
    i = pl.program_id(0)
    n_tiles = N // TM

    def issue(tile, slot):
        base = tile * (2 * TM)
        for r in range(TM):
            for kk in range(2):
                _row_copy(ys_hbm, d_ref[base + 2 * r + kk], buf.at[slot, kk], r, sem.at[slot]).start()

    @pl.when(i == 0)
    def _():
        issue(0, 0)

    @pl.when(i + 1 < n_tiles)
    def _():
        issue(i + 1, (i + 1) % 2)

    slot = i % 2
    for kk in range(2):
        pltpu.make_async_copy(ys_hbm.at[pl.ds(0, TM)], buf.at[slot, kk], sem.at[slot]).wait()
    w = rw_ref[...]
    y = w[:, 0:1] * buf[slot, 0] + w[:, 1:2] * buf[slot, 1]
    out = x_ref[...] + m_ref[5:6, :] * y
    if final:
        ms = jnp.mean(out * out, axis=-1, keepdims=True)
        out = out * lax.rsqrt(ms + EPS) * fn_ref[...]
    o_ref[...] = out


def _combine(dest_flat, ys, x, m3, rw, final_norm, final):
    grid_spec = pltpu.PrefetchScalarGridSpec(
        num_scalar_prefetch=1,
        grid=(N // TM,),
        in_specs=[pl.BlockSpec(memory_space=pl.ANY),
                  pl.BlockSpec((TM, D), lambda i, d: (i, 0)),
                  pl.BlockSpec((None, 6, D), lambda i, d: (_cond_row(i), 0, 0)),
                  pl.BlockSpec((TM, LANES), lambda i, d: (i, 0)),
                  pl.BlockSpec((1, D), lambda i, d: (0, 0))],
        out_specs=pl.BlockSpec((TM, D), lambda i, d: (i, 0)),
        scratch_shapes=[pltpu.VMEM((2, 2, TM, D), F32), pltpu.SemaphoreType.DMA((2,))])
    return pl.pallas_call(
        functools.partial(_combine_kernel, final=final),
        grid_spec=grid_spec,
        out_shape=jax.ShapeDtypeStruct((N, D), F32),
        compiler_params=_cparams(("arbitrary",)),
        name="combine",
    )(dest_flat, ys, x, m3, rw, final_norm)


def _hier_moe(x, m3, gain, layer, w_rg, b_rg, w_re, b_re, w_gate, w_up, w_down, tri_tm, final_norm, final):
    wr = jnp.concatenate([w_rg, w_re.transpose(1, 0, 2).reshape(D, N_EXPERTS),
                          jnp.zeros((D, LANES - N_GROUPS - N_EXPERTS), F32)], axis=1)
    br = jnp.concatenate([b_rg, b_re.reshape(N_EXPERTS),
                          jnp.zeros((LANES - N_GROUPS - N_EXPERTS,), F32)])[None, :]
    xf, ri, rw, cnt = _router(x, m3, gain, wr, br, tri_tm)
    e_idx = ri[:, 0:2]
    rank = ri[:, 2:4]
    counts = cnt[0, :N_EXPERTS].astype(jnp.int32)
    padded = (counts + MOE_ROWS - 1) // MOE_ROWS * MOE_ROWS
    end_pad = jnp.cumsum(padded)
    start_pad = end_pad - padded
    experts = jnp.arange(N_EXPERTS, dtype=jnp.int32)
    start_of = jnp.sum(jnp.where(e_idx[:, :, None] == experts, start_pad, 0), axis=-1)
    dest = (start_of + rank).reshape(-1).astype(jnp.int32)
    block_start = jnp.arange(N_SLOT_BLOCKS, dtype=jnp.int32) * MOE_ROWS
    block_e = jnp.minimum(jnp.sum((end_pad[None, :] <= block_start[:, None]).astype(jnp.int32), axis=1),
                          N_EXPERTS - 1)
    n_used = (end_pad[-1:] // MOE_ROWS).astype(jnp.int32)
    tok = jnp.repeat(jnp.arange(N, dtype=jnp.int32), 2)
    slot_tok = jnp.zeros((N_SLOT_BLOCKS * MOE_ROWS,), jnp.int32).at[dest].set(tok, unique_indices=True)
    ys = _experts(block_e, n_used, slot_tok, xf, w_gate, w_up, w_down, layer)
    return _combine(dest, ys, x, m3, rw, final_norm, final)


def kernel(x_prompt, x_sample, state_A_fwd, state_A_bwd, cache_k, cache_v, c, c_ctx, mod_w, mod_b, norm_mix, norm_ffn, ab_w_in, ab_conv, ab_a_log, ab_dt_bias, ab_o_gain, ab_w_out, na_w_qkv, na_rpb, na_w_out, moe_w_rg, moe_b_rg, moe_w_re, moe_b_re, moe_w_gate, moe_w_up, moe_w_down, final_norm):
    x = jnp.concatenate([x_prompt.reshape(NC, D), x_sample.reshape(NL, D)], axis=0)
    cond8 = jnp.concatenate([c_ctx[None, :], c, jnp.zeros((8 - 1 - DEC_BATCH, D), F32)], axis=0)
    mods = _ada_params(cond8, mod_w, mod_b).reshape(DEPTH, 8, 6, D)

    tri_tm = jnp.asarray(np.tril(np.ones((TM, TM)), -1), BF16)
    tri_c = jnp.asarray(np.tril(np.ones((CHUNK, CHUNK))), F32)
    half = np.arange(LANES) < A_DH
    jbd = jnp.asarray((half[:, None] == half[None, :]).astype(np.float32))
    fn = final_norm[None, :]

    m3 = mods[0]
    w_in = ab_w_in[0]
    w_in = jnp.concatenate([w_in[:, :2048], w_in[:, 2080:2592], w_in[:, 2048:2080],
                            jnp.zeros((D, AB_COLS - 2592), F32)], axis=1).astype(BF16)
    proj = _modproj(x, m3, norm_mix[0][None, :], w_in, 896)
    gate_p = jnp.zeros((8, LANES), F32)
    gate_p = gate_p.at[0, :2 * A_HEADS].set(ab_a_log[0].reshape(-1))
    gate_p = gate_p.at[1, :2 * A_HEADS].set(ab_dt_bias[0].reshape(-1))
    o_gain2 = jnp.tile(ab_o_gain[0], 2)[None, :]
    zeros_state = jnp.zeros((BATCH, A_HEADS, A_DH, A_DH), F32)
    mix_a_c, s_f, s_b = _deltanet(proj, ab_conv[0], gate_p, o_gain2, zeros_state, zeros_state, tri_c, jbd,
                                  seq_len=SEQ, n_seq=BATCH, n_sub=4, row_blk0=0)
    mix_a_l, _, _ = _deltanet(proj, ab_conv[0], gate_p, o_gain2, state_A_fwd[:, 0], state_A_bwd[:, 0], tri_c, jbd,
                              seq_len=DEC_SEQ, n_seq=DEC_BATCH, n_sub=1, row_blk0=NC // DEC_SEQ)

    cc, sc = _dft_mats(LANES)
    ct, st = _dft_mats(SEQ)
    c64, s64 = _dft_mats(64)
    cs = jnp.asarray(np.concatenate([cc, sc], axis=1), BF16)
    dft = jnp.asarray(np.concatenate([ct, -st], axis=1), BF16)
    ca = jnp.asarray(np.concatenate([cc, -sc, -sc, -cc], axis=1), BF16)
    m1 = jnp.asarray(np.concatenate([c64, s64], axis=1), BF16)
    tw_idx = np.arange(64)
    tw_ang = 2.0 * np.pi * (tw_idx[:, None] * tw_idx[None, :]) / DEC_SEQ
    twc = jnp.broadcast_to(jnp.asarray(np.cos(tw_ang), F32)[:, :, None], (64, 64, LANES))
    tws = jnp.broadcast_to(jnp.asarray(np.sin(tw_ang), F32)[:, :, None], (64, 64, LANES))
    mix_b_c = _fnet_ctx(proj, cs, dft)
    mix_b_l = _fnet_latent(proj, ca, m1, twc, tws)

    w_out = ab_w_out[0].astype(BF16)
    x = _outproj([(mix_a_c, mix_a_l), (mix_b_c, mix_b_l)], [w_out[:A_WIDTH], w_out[A_WIDTH:]], x, m3)
    x = _hier_moe(x, m3, norm_ffn[0][None, :], 0, moe_w_rg[0], moe_b_rg[0], moe_w_re[0], moe_b_re[0],
                  moe_w_gate, moe_w_up, moe_w_down, tri_tm, fn, False)

    m3 = mods[1]
    qkv = _modproj(x, m3, norm_mix[1][None, :], na_w_qkv[0].astype(BF16), 512)
    attn_c = _na_ctx(qkv)
    attn_l = _na_latent(qkv, cache_k[:, 0].reshape(DEC_BATCH, 256, D), cache_v[:, 0].reshape(DEC_BATCH, 256, D),
                        _rpb_tables(na_rpb[0]))
    x = _outproj([(attn_c, attn_l)], [na_w_out[0].astype(BF16)], x, m3)
    x = _hier_moe(x, m3, norm_ffn[1][None, :], 1, moe_w_rg[1], moe_b_rg[1], moe_w_re[1], moe_b_re[1],
                  moe_w_gate, moe_w_up, moe_w_down, tri_tm, fn, True)

    new_k = qkv[:NC, D:2 * D].reshape(BATCH, 1, SEQ, NA_HEADS, NA_DH)
    new_v = qkv[:NC, 2 * D:].reshape(BATCH, 1, SEQ, NA_HEADS, NA_DH)
    return (x[:NC].reshape(BATCH, SEQ, D), x[NC:].reshape(DEC_BATCH, DEC_SEQ, D),
            s_f[:, None], s_b[:, None], new_k, new_v)
```

```python
import functools
import math

import numpy as np
import jax
import jax.numpy as jnp
from jax import lax
from jax.experimental import pallas as pl
from jax.experimental.pallas import tpu as pltpu

F32 = jnp.float32
BF16 = jnp.bfloat16
HIGHEST = lax.Precision.HIGHEST

D = 1024
BATCH, SEQ = 32, 256
DEC_BATCH, DEC_SEQ = 2, 4096
NC = BATCH * SEQ
NL = DEC_BATCH * DEC_SEQ
N = NC + NL
DEPTH = 2
GRID_W = 64
A_DH = 64
A_HEADS = 8
A_WIDTH = 512
CHUNK = 64
B_WIDTH = 512
B_GROUPS = 4
NA_DH = 64
NA_HEADS = 16
WIN_R, WIN_C = 8, 16
N_GROUPS, EXP_PER_GROUP, N_EXPERTS = 4, 8, 32
D_EXPERT = 512
EPS = 1e-6

LANES = 128
TM = 256
MOE_ROWS = 256
N_SLOT_BLOCKS = (2 * N) // MOE_ROWS + N_EXPERTS
AB_COLS = 2688
VMEM_LIMIT = 56 * 1024 * 1024
EXPERT_VMEM_LIMIT = 60 * 1024 * 1024
NEG = -1e30


def _cparams(sem):
    return pltpu.CompilerParams(dimension_semantics=sem, vmem_limit_bytes=VMEM_LIMIT)


def _mm(a, b):
    return jnp.dot(a.astype(BF16), b.astype(BF16), preferred_element_type=F32)


def _mm_nt(a, b):
    return lax.dot_general(a.astype(BF16), b.astype(BF16), (((1,), (1,)), ((), ())),
                           preferred_element_type=F32)


SOLVE_BLK = 16
NA_ROW_UNROLL = 4
TERM_UNROLL = 4


def _unit_lower_solve(a_mat, rhs, same_blk, eye):
    dg = jnp.where(same_blk, a_mat, 0.0)
    p = -dg
    dinv = eye + p
    for _ in range(int(math.log2(SOLVE_BLK)) - 1):
        p = _mm(p, p)
        dinv = dinv + _mm(p, dinv)
    mp = -_mm(dinv, a_mat - dg)
    y = _mm(dinv, rhs)
    y = y + _mm(mp, y)
    for _ in range(int(math.log2(a_mat.shape[0] // SOLVE_BLK)) - 1):
        mp = _mm(mp, mp)
        y = y + _mm(mp, y)
    return y


def _unit_lower_solve_many(a_mats, rhs, same_blk, eye):
    dg = [jnp.where(same_blk, a, 0.0) for a in a_mats]
    off = [a - g for a, g in zip(a_mats, dg)]
    p = [-g for g in dg]
    dinv = [eye + x for x in p]
    for _ in range(int(math.log2(SOLVE_BLK)) - 1):
        p = [_mm(x, x) for x in p]
        dinv = [di + _mm(x, di) for x, di in zip(p, dinv)]
    mp = [-_mm(di, o) for di, o in zip(dinv, off)]
    y = [_mm(di, r) for di, r in zip(dinv, rhs)]
    y = [yi + _mm(m, yi) for m, yi in zip(mp, y)]
    for _ in range(int(math.log2(a_mats[0].shape[0] // SOLVE_BLK)) - 1):
        mp = [_mm(m, m) for m in mp]
        y = [yi + _mm(m, yi) for m, yi in zip(mp, y)]
    return y


def _mm_split(a, b, parts, split_rhs=False):
    x = b if split_rhs else a
    acc = None
    for _ in range(parts):
        piece = x.astype(BF16)
        x = x - piece.astype(F32)
        term = (jnp.dot(a.astype(BF16), piece, preferred_element_type=F32) if split_rhs
                else jnp.dot(piece, b.astype(BF16), preferred_element_type=F32))
        acc = term if acc is None else acc + term
    return acc


def _mm_hi(a, b):
    return jnp.dot(a, b, preferred_element_type=F32, precision=HIGHEST)


def _silu(x):
    return x * jax.nn.sigmoid(x)


def _bf16_bits(x):
    b = lax.bitcast_convert_type(x, jnp.int32)
    return b + 0x7FFF + (lax.shift_right_logical(b, jnp.int32(16)) & 1)


_HIGH16 = -65536


def _pack_bf16_pairs(a, b):
    return lax.shift_right_logical(_bf16_bits(a), jnp.int32(16)) | (_bf16_bits(b) & _HIGH16)


def _unpack_bf16_pairs(p):
    a = lax.bitcast_convert_type(lax.shift_left(p, jnp.int32(16)), F32)
    b = lax.bitcast_convert_type(p & _HIGH16, F32)
    return a.astype(BF16), b.astype(BF16)


def _cond_row(i):
    return jnp.where(i < NC // TM, 0, 1 + (i - NC // TM) // (DEC_SEQ // TM))


def _modulated_norm(x, m_ref, g_ref, shift_idx, scale_idx):
    ms = jnp.mean(x * x, axis=-1, keepdims=True)
    y = x * lax.rsqrt(ms + EPS) * g_ref[...]
    return y * (1.0 + m_ref[scale_idx:scale_idx + 1, :]) + m_ref[shift_idx:shift_idx + 1, :]


def _ada_kernel(cond_ref, w_ref, b_ref, o_ref):
    o_ref[...] = _mm_hi(_silu(cond_ref[...]), w_ref[...]) + b_ref[...]


def _ada_params(cond8, mod_w, mod_b):
    tn = 1536
    return pl.pallas_call(
        _ada_kernel,
        grid=(DEPTH, 6 * D // tn),
        in_specs=[pl.BlockSpec((8, D), lambda l, j: (0, 0)),
                  pl.BlockSpec((None, D, tn), lambda l, j: (l, 0, j)),
                  pl.BlockSpec((None, 1, tn), lambda l, j: (l, 0, j))],
        out_specs=pl.BlockSpec((None, 8, tn), lambda l, j: (l, 0, j)),
        out_shape=jax.ShapeDtypeStruct((DEPTH, 8, 6 * D), F32),
        compiler_params=_cparams(("arbitrary", "arbitrary")),
        name="ada_params",
    )(cond8, mod_w, mod_b.reshape(DEPTH, 1, 6 * D))


def _token_specs(x):
    nct = NC // TM
    if isinstance(x, tuple):
        return ([pl.BlockSpec((TM, x[0].shape[1]), lambda i: (jnp.minimum(i, nct - 1), 0)),
                 pl.BlockSpec((TM, x[1].shape[1]), lambda i: (jnp.maximum(i - nct, 0), 0))], list(x))
    return [pl.BlockSpec((TM, x.shape[1]), lambda i: (i, 0))], [x]


def _token_rows(refs):
    if len(refs) == 1:
        return refs[0][...]
    return jnp.where(pl.program_id(0) < NC // TM, refs[0][...], refs[1][...])


def _modproj_kernel(*refs, n_chunk):
    m_ref, g_ref, w_ref, o_ref = refs[-4:]
    hb = _modulated_norm(_token_rows(refs[:-4]), m_ref, g_ref, 0, 1).astype(BF16)
    for j in range(o_ref.shape[1] // n_chunk):
        sl = slice(j * n_chunk, (j + 1) * n_chunk)
        o_ref[:, sl] = jnp.dot(hb, w_ref[:, sl], preferred_element_type=F32)


def _modproj(x, m3, gain, w_bf16, n_chunk):
    nout = w_bf16.shape[1]
    x_specs, x_args = _token_specs(x)
    return pl.pallas_call(
        functools.partial(_modproj_kernel, n_chunk=n_chunk),
        grid=(N // TM,),
        in_specs=x_specs + [pl.BlockSpec((None, 6, D), lambda i: (_cond_row(i), 0, 0)),
                            pl.BlockSpec((1, D), lambda i: (0, 0)),
                            pl.BlockSpec((D, nout), lambda i: (0, 0))],
        out_specs=pl.BlockSpec((TM, nout), lambda i: (i, 0)),
        out_shape=jax.ShapeDtypeStruct((N, nout), F32),
        compiler_params=_cparams(("arbitrary",)),
        name="modproj",
    )(*x_args, m3, gain, w_bf16)


def _outproj_kernel(*refs, n_in):
    a_refs = refs[:2 * n_in]
    w_refs = refs[2 * n_in:3 * n_in]
    x_refs = refs[3 * n_in:-2]
    m_ref, o_ref = refs[-2:]
    acc = None
    for j, w_ref in enumerate(w_refs):
        part = _mm(_token_rows(a_refs[2 * j:2 * j + 2]), w_ref[...])
        acc = part if acc is None else acc + part
    o_ref[...] = _token_rows(x_refs) + m_ref[2:3, :] * acc


def _outproj(a_pairs, w_list, x, m3):
    n_in = len(a_pairs)
    in_specs, args = [], []
    for pair in a_pairs:
        specs, ops = _token_specs(pair)
        in_specs += specs
        args += ops
    x_specs, x_args = _token_specs(x)
    in_specs += ([pl.BlockSpec(w.shape, lambda i: (0, 0)) for w in w_list] + x_specs
                 + [pl.BlockSpec((None, 6, D), lambda i: (_cond_row(i), 0, 0))])
    return pl.pallas_call(
        functools.partial(_outproj_kernel, n_in=n_in),
        grid=(N // TM,),
        in_specs=in_specs,
        out_specs=pl.BlockSpec((TM, D), lambda i: (i, 0)),
        out_shape=jax.ShapeDtypeStruct((N, D), F32),
        compiler_params=_cparams(("arbitrary",)),
        name="outproj",
    )(*args, *w_list, *x_args, m3)


def _deltanet_kernel(q_ref, k_ref, v_ref, z_ref, ab_ref, cq_ref, ck_ref, cv_ref, gp_ref, og_ref,
                     s0f_ref, s0b_ref, tri_ref, jbd_ref, o_ref, sf_ref, sb_ref,
                     u_s, w_s, qd_s, at_s, kt_s, ge_s, st_s, ob, *, seq_len, n_sub):
    hp = pl.program_id(1)
    C = CHUNK
    nc = seq_len // C
    nct = n_sub * nc
    P = LANES
    lane = lax.broadcasted_iota(jnp.int32, (C, P), 1)
    row = lax.broadcasted_iota(jnp.int32, (C, P), 0)
    first_head = lane < A_DH
    ri = lax.broadcasted_iota(jnp.int32, (P, P), 0)
    ci = lax.broadcasted_iota(jnp.int32, (P, P), 1)
    same_head = (ri < C) == (ci < C)
    same_blk = (ri // SOLVE_BLK) == (ci // SOLVE_BLK)
    eye = jnp.where(ri == ci, 1.0, 0.0)
    jbd = jbd_ref[...]
    lincl = tri_ref[...]
    tri_b = [lincl.astype(BF16), lincl.T.astype(BF16)]
    incl_m = [jnp.logical_and(same_head, ri >= ci), jnp.logical_and(same_head, ri <= ci)]
    strict_m = [jnp.logical_and(same_head, ri > ci), jnp.logical_and(same_head, ri < ci)]
    neg_a = -jnp.exp(gp_ref[0:1, :])
    dt_b = gp_ref[1:2, :]

    def conv_silu(ref, w_ref, c):
        base = pl.multiple_of(c * C, C)
        cs = c % nc
        xc = ref[pl.ds(base, C), :]
        pbase = pl.multiple_of(jnp.maximum(base - 8, 0), 8)
        nbase = pl.multiple_of(jnp.minimum(base + C, n_sub * seq_len - 8), 8)
        prev_row = ref[pl.ds(pbase, 8), :][7:8, :] * jnp.where(cs > 0, 1.0, 0.0)
        next_row = ref[pl.ds(nbase, 8), :][0:1, :] * jnp.where(cs < nc - 1, 1.0, 0.0)
        x_prev = jnp.where(row == 0, prev_row, pltpu.roll(xc, 1, 0))
        x_next = jnp.where(row == C - 1, next_row, pltpu.roll(xc, C - 1, 0))
        y = w_ref[0:1, :] * x_prev + w_ref[1:2, :] * xc + w_ref[2:3, :] * x_next
        return _silu(y)

    def stack(x):
        return jnp.concatenate([jnp.where(first_head, x, 0.0), jnp.where(first_head, 0.0, x)], axis=0)

    def lane_col(arr, idx):
        return jnp.sum(jnp.where(lane == idx, arr, 0.0), axis=-1, keepdims=True)

    def chunk_inputs(c):
        base = pl.multiple_of(c * C, C)
        q = conv_silu(q_ref, cq_ref, c)
        k = conv_silu(k_ref, ck_ref, c)
        v = conv_silu(v_ref, cv_ref, c)
        ab = ab_ref[pl.ds(base, C), :]
        g_all = neg_a * jax.nn.softplus(ab + dt_b)
        beta_all = jax.nn.sigmoid(ab)
        return q, k, v, g_all, beta_all

    def chain_gates(g_all, beta_all, gc_all, d):
        cols = []
        for a in range(2):
            g_lane = d * A_HEADS + 2 * hp + a
            g_col = lane_col(g_all, g_lane)
            tot = jnp.broadcast_to(jnp.sum(g_col, axis=0, keepdims=True), (C, 1))
            cols.append((lane_col(gc_all, g_lane), lane_col(beta_all, 2 * A_HEADS + g_lane), tot))
        return [jnp.concatenate([cols[0][j], cols[1][j]], axis=0) for j in range(3)]

    def terms_body(j, carry):
        cs = [TERM_UNROLL * j + t for t in range(TERM_UNROLL)]
        ins = [chunk_inputs(c) for c in cs]
        qsq = [_mm_split(x[0] * x[0], jbd, 2) for x in ins]
        ksq = [_mm_split(x[1] * x[1], jbd, 2) for x in ins]
        qs = [x[0] * lax.rsqrt(s + EPS) * (A_DH ** -0.5) for x, s in zip(ins, qsq)]
        ks = [x[1] * lax.rsqrt(s + EPS) for x, s in zip(ins, ksq)]
        qst = [stack(x) for x in qs]
        kst = [stack(x) for x in ks]
        vst = [stack(x[2]) for x in ins]
        kk = [_mm_nt(x, x) for x in kst]
        qk = [_mm_nt(x, y) for x, y in zip(qst, kst)]
        chains = [(t, d) for t in range(TERM_UNROLL) for d in range(2)]
        gc_all = [_mm_split(tri_b[d], ins[t][3], 3, split_rhs=True) for t, d in chains]
        gates = [chain_gates(ins[t][3], ins[t][4], gc, d) for (t, d), gc in zip(chains, gc_all)]
        decay, e_gc = [], []
        for (t, d), (gc_col, beta_col, tot_col) in zip(chains, gates):
            gcb = jnp.broadcast_to(gc_col, (P, P))
            decay.append(jnp.where(incl_m[d], jnp.exp(jnp.where(incl_m[d], gcb - gcb.T, 0.0)), 0.0))
            e_gc.append(jnp.exp(gc_col))
        a_mats = [jnp.where(strict_m[d], g[1] * kk[t] * dc, 0.0) for (t, d), g, dc in zip(chains, gates, decay)]
        rhs = [jnp.concatenate([vst[t] * g[1], kst[t] * (g[1] * e)], axis=1)
               for (t, d), g, e in zip(chains, gates, e_gc)]
        xs = _unit_lower_solve_many(a_mats, rhs, same_blk, eye)
        for (t, d), x, g, e, dc in zip(chains, xs, gates, e_gc, decay):
            c = cs[t]
            u_s[d, c] = x[:, :P].astype(BF16)
            w_s[d, c] = x[:, P:].astype(BF16)
            qd_s[d, c] = (qst[t] * e).astype(BF16)
            at_s[d, c] = jnp.where(incl_m[d], qk[t] * dc, 0.0).astype(BF16)
            kt_s[d, c] = (kst[t] * jnp.exp(g[2] - g[0])).T.astype(BF16)
            e_tot = jnp.exp(g[2])
            ge_s[d, c] = jnp.where(lax.broadcasted_iota(jnp.int32, (8, P), 1) < A_DH,
                                   e_tot[0:1, :], e_tot[C:C + 1, :])
        return carry

    lax.fori_loop(0, nct // TERM_UNROLL, terms_body, 0)

    def block_diag(s2):
        z = jnp.zeros((A_DH, A_DH), F32)
        return jnp.concatenate([jnp.concatenate([s2[0], z], axis=1),
                                jnp.concatenate([z, s2[1]], axis=1)], axis=0)

    for s in range(n_sub):
        st_s[2 * s] = block_diag(s0f_ref[s])
        st_s[2 * s + 1] = block_diag(s0b_ref[s])

    def scan_body(i, carry):
        chains = [(s, d, s * nc + (i if d == 0 else nc - 1 - i)) for s in range(n_sub) for d in range(2)]
        dot = functools.partial(jnp.dot, preferred_element_type=F32)
        s_bd = [st_s[2 * s + d] for s, d, c in chains]
        sb16 = [x.astype(BF16) for x in s_bd]
        ws = [dot(w_s[d, c], sb) for (s, d, c), sb in zip(chains, sb16)]
        qs_ = [dot(qd_s[d, c], sb) for (s, d, c), sb in zip(chains, sb16)]
        vb = [(u_s[d, c].astype(F32) - x).astype(BF16) for (s, d, c), x in zip(chains, ws)]
        av = [dot(at_s[d, c], x) for (s, d, c), x in zip(chains, vb)]
        kv = [dot(kt_s[d, c], x) for (s, d, c), x in zip(chains, vb)]
        for (s, d, c), sb, q_, a_, k_ in zip(chains, s_bd, qs_, av, kv):
            st_s[2 * s + d] = sb * ge_s[d, c][0:1, :] + k_
            o_st = q_ + a_
            dst = o_ref if d == 0 else ob
            dst[pl.ds(pl.multiple_of(c * C, C), C), :] = o_st[:C] + o_st[C:]
        return carry

    lax.fori_loop(0, nc, scan_body, 0)

    for s in range(n_sub):
        for d, ref in ((0, sf_ref), (1, sb_ref)):
            s_bd = st_s[2 * s + d]
            ref[s, 0] = s_bd[:A_DH, :A_DH]
            ref[s, 1] = s_bd[A_DH:, A_DH:]

    def finish(c, carry):
        base = pl.multiple_of(c * C, C)
        o = o_ref[pl.ds(base, C), :] + ob[pl.ds(base, C), :]
        ms = _mm_split(o * o, jbd, 2) * (1.0 / A_DH)
        y = o * lax.rsqrt(ms + EPS) * og_ref[...]
        o_ref[pl.ds(base, C), :] = y * _silu(z_ref[pl.ds(base, C), :])
        return carry

    lax.fori_loop(0, nct, finish, 0)


def _deltanet(proj, conv_w, gate_p, o_gain2, s0f, s0b, tri, jbd, *, seq_len, n_seq, n_sub, row_blk0):
    rows = n_sub * seq_len
    nct = rows // CHUNK
    rb = lambda b: row_blk0 + b
    col = lambda off: (lambda b, hp: (rb(b), off + hp))
    st_spec = pl.BlockSpec((n_sub, 2, A_DH, A_DH), lambda b, hp: (b, hp, 0, 0))
    in_specs = [pl.BlockSpec((rows, LANES), col(0)),
                pl.BlockSpec((rows, LANES), col(4)),
                pl.BlockSpec((rows, LANES), col(8)),
                pl.BlockSpec((rows, LANES), col(12)),
                pl.BlockSpec((rows, LANES), lambda b, hp: (rb(b), 20)),
                pl.BlockSpec((3, LANES), lambda b, hp: (0, hp)),
                pl.BlockSpec((3, LANES), lambda b, hp: (0, 4 + hp)),
                pl.BlockSpec((3, LANES), lambda b, hp: (0, 8 + hp)),
                pl.BlockSpec((8, LANES), lambda b, hp: (0, 0)),
                pl.BlockSpec((1, LANES), lambda b, hp: (0, 0)),
                st_spec, st_spec,
                pl.BlockSpec((CHUNK, CHUNK), lambda b, hp: (0, 0)),
                pl.BlockSpec((LANES, LANES), lambda b, hp: (0, 0))]
    args = [proj, proj, proj, proj, proj, conv_w, conv_w, conv_w, gate_p, o_gain2, s0f, s0b, tri, jbd]
    st_shape = jax.ShapeDtypeStruct((n_seq, A_HEADS, A_DH, A_DH), F32)
    tile = lambda dt: pltpu.VMEM((2, nct, LANES, LANES), dt)
    return pl.pallas_call(
        functools.partial(_deltanet_kernel, seq_len=seq_len, n_sub=n_sub),
        grid=(n_seq // n_sub, A_HEADS // 2),
        in_specs=in_specs,
        out_specs=[pl.BlockSpec((rows, LANES), lambda b, hp: (b, hp)), st_spec, st_spec],
        out_shape=[jax.ShapeDtypeStruct((n_seq * seq_len, A_WIDTH), F32), st_shape, st_shape],
        scratch_shapes=[tile(BF16), tile(BF16), tile(BF16), tile(BF16), tile(BF16),
                        pltpu.VMEM((2, nct, 8, LANES), F32),
                        pltpu.VMEM((2 * n_sub, LANES, LANES), F32),
                        pltpu.VMEM((rows, LANES), F32)],
        compiler_params=_cparams(("arbitrary", "arbitrary")),
        name="deltanet",
    )(*args)


def _dft_mats(n):
    idx = np.arange(n)
    ang = 2.0 * np.pi * ((idx[:, None] * idx[None, :]) % n) / n
    return np.cos(ang), np.sin(ang)


def _fnet_ctx_kernel(u_ref, cs_ref, dft_ref, o_ref):
    norm = 1.0 / math.sqrt(SEQ * LANES)
    for g in range(B_GROUPS):
        sl = slice(g * LANES, (g + 1) * LANES)
        p = _mm(u_ref[:, sl], cs_ref[...])
        stack = jnp.concatenate([p[:, :LANES], p[:, LANES:]], axis=0)
        o_ref[:, sl] = _mm(dft_ref[...], stack) * norm


def _fnet_ctx(proj, cs, dft):
    return pl.pallas_call(
        _fnet_ctx_kernel,
        grid=(BATCH,),
        in_specs=[pl.BlockSpec((SEQ, B_WIDTH), lambda b: (b, 4)),
                  pl.BlockSpec(cs.shape, lambda b: (0, 0)),
                  pl.BlockSpec(dft.shape, lambda b: (0, 0))],
        out_specs=pl.BlockSpec((SEQ, B_WIDTH), lambda b: (b, 0)),
        out_shape=jax.ShapeDtypeStruct((NC, B_WIDTH), F32),
        compiler_params=_cparams(("arbitrary",)),
        name="fnet_ctx",
    )(proj, cs, dft)


FN_SUB = 4


def _fnet_lat1_kernel(u_ref, ca_ref, m1_ref, twc_ref, tws_ref, o_ref):
    r = 64
    for j in range(FN_SUB):
        c = twc_ref[j]
        s = tws_ref[j]
        rows = slice(j * r, (j + 1) * r)
        for g in range(B_GROUPS):
            pa = _mm(u_ref[rows, g * LANES:(g + 1) * LANES], ca_ref[...])
            rhs = jnp.concatenate([pa[:, :2 * LANES], pa[:, 2 * LANES:]], axis=0)
            zz = _mm(m1_ref[...], rhs)
            zr, zi = zz[:, :LANES], zz[:, LANES:]
            o_ref[rows, 2 * g * LANES:(2 * g + 1) * LANES] = (zr * c + zi * s).astype(BF16)
            o_ref[rows, (2 * g + 1) * LANES:(2 * g + 2) * LANES] = (zi * c - zr * s).astype(BF16)


def _fnet_lat2_kernel(z_ref, m1_ref, o_ref):
    r = 64
    norm = 1.0 / math.sqrt(DEC_SEQ * LANES)
    for j in range(FN_SUB):
        rows = slice(j * r, (j + 1) * r)
        for g in range(B_GROUPS):
            rhs = jnp.concatenate([z_ref[rows, 2 * g * LANES:(2 * g + 1) * LANES],
                                   z_ref[rows, (2 * g + 1) * LANES:(2 * g + 2) * LANES]], axis=0)
            o_ref[rows, g * LANES:(g + 1) * LANES] = _mm(m1_ref[...], rhs) * norm


def _fnet_latent(proj, ca, m1, twc, tws):
    r = 64
    u = proj[NC:, 2048:2560].astype(BF16).reshape(DEC_BATCH, r, r, B_WIDTH)
    u = u.transpose(0, 2, 1, 3).reshape(DEC_BATCH * r * r, B_WIDTH)
    steps = DEC_BATCH * r // FN_SUB
    blk = FN_SUB * r
    z = pl.pallas_call(
        _fnet_lat1_kernel,
        grid=(steps,),
        in_specs=[pl.BlockSpec((blk, B_WIDTH), lambda s: (s, 0)),
                  pl.BlockSpec(ca.shape, lambda s: (0, 0)),
                  pl.BlockSpec(m1.shape, lambda s: (0, 0)),
                  pl.BlockSpec((FN_SUB, r, LANES), lambda s: (s % (r // FN_SUB), 0, 0)),
                  pl.BlockSpec((FN_SUB, r, LANES), lambda s: (s % (r // FN_SUB), 0, 0))],
        out_specs=pl.BlockSpec((blk, 2 * B_WIDTH), lambda s: (s, 0)),
        out_shape=jax.ShapeDtypeStruct((NL, 2 * B_WIDTH), BF16),
        compiler_params=_cparams(("arbitrary",)),
        name="fnet_lat1",
    )(u, ca, m1, twc, tws)
    z = z.reshape(DEC_BATCH, r, r, 2 * B_WIDTH).transpose(0, 2, 1, 3).reshape(NL, 2 * B_WIDTH)
    y = pl.pallas_call(
        _fnet_lat2_kernel,
        grid=(steps,),
        in_specs=[pl.BlockSpec((blk, 2 * B_WIDTH), lambda s: (s, 0)),
                  pl.BlockSpec(m1.shape, lambda s: (0, 0))],
        out_specs=pl.BlockSpec((blk, B_WIDTH), lambda s: (s, 0)),
        out_shape=jax.ShapeDtypeStruct((NL, B_WIDTH), F32),
        compiler_params=_cparams(("arbitrary",)),
        name="fnet_lat2",
    )(z, m1)
    return y.reshape(DEC_BATCH, r, r, B_WIDTH).transpose(0, 2, 1, 3).reshape(NL, B_WIDTH)


def _head_masks():
    lane = lax.broadcasted_iota(jnp.int32, (1, LANES), 1)
    return lane < NA_DH


def _attend_many(chains):
    scale = NA_DH ** -0.5
    s = [[_mm_nt(q, k) * scale if b is None else _mm_nt(q, k) * scale + b for k, v, b in kv] for q, kv in chains]
    m = [functools.reduce(jnp.maximum, [jnp.max(x, axis=-1, keepdims=True) for x in xs]) for xs in s]
    p = [[jnp.exp(x - mi) for x in xs] for xs, mi in zip(s, m)]
    l = [sum(jnp.sum(x, axis=-1, keepdims=True) for x in xs) for xs in p]
    o = [sum(_mm(x, v) for x, (k, v, b) in zip(xs, kv)) for xs, (q, kv) in zip(p, chains)]
    return [oi / li for oi, li in zip(o, l)]


def _na_ctx_kernel(q_ref, k_ref, v_ref, o_ref):
    first = _head_masks()
    k = k_ref[...].astype(BF16)
    v = v_ref[...].astype(BF16)
    n_split = 2
    rows = SEQ // n_split
    chains = []
    for j in range(n_split):
        q = q_ref[j * rows:(j + 1) * rows, :]
        for a in range(2):
            chains.append((jnp.where(first if a == 0 else jnp.logical_not(first), q, 0.0), [(k, v, None)]))
    outs = _attend_many(chains)
    for j in range(n_split):
        o_ref[j * rows:(j + 1) * rows, :] = jnp.where(first, outs[2 * j], outs[2 * j + 1])


def _na_ctx(qkv):
    return pl.pallas_call(
        _na_ctx_kernel,
        grid=(BATCH, NA_HEADS // 2),
        in_specs=[pl.BlockSpec((SEQ, LANES), lambda b, hp: (b, hp)),
                  pl.BlockSpec((SEQ, LANES), lambda b, hp: (b, 8 + hp)),
                  pl.BlockSpec((SEQ, LANES), lambda b, hp: (b, 16 + hp))],
        out_specs=pl.BlockSpec((SEQ, LANES), lambda b, hp: (b, hp)),
        out_shape=jax.ShapeDtypeStruct((NC, D), F32),
        compiler_params=_cparams(("arbitrary", "arbitrary")),
        name="na_ctx",
    )(qkv, qkv, qkv)


def _na_lat_kernel(q_ref, k_ref, v_ref, kc_ref, vc_ref, tt_ref, o_ref):
    first = _head_masks()
    rows = DEC_SEQ // GRID_W
    kctx = kc_ref[...].astype(BF16)
    vctx = vc_ref[...].astype(BF16)
    nkeys = WIN_R * GRID_W

    def rows_body(j, carry):
        chains = []
        for t in range(NA_ROW_UNROLL):
            r = NA_ROW_UNROLL * j + t
            r0 = jnp.clip(r - WIN_R // 2, 0, rows - WIN_R)
            dr0 = r0 - r + (WIN_R - 1)
            q = q_ref[pl.ds(pl.multiple_of(r * GRID_W, GRID_W), GRID_W), :]
            kbase = pl.multiple_of(r0 * GRID_W, GRID_W)
            kl = k_ref[pl.ds(kbase, nkeys), :].astype(BF16)
            vl = v_ref[pl.ds(kbase, nkeys), :].astype(BF16)
            for a in range(2):
                qm = jnp.where(first if a == 0 else jnp.logical_not(first), q, 0.0)
                bias = jnp.concatenate([tt_ref[a, dr0 + 2 * i] for i in range(WIN_R // 2)], axis=1)
                chains.append((qm, [(kl, vl, bias), (kctx, vctx, None)]))
        outs = _attend_many(chains)
        for t in range(NA_ROW_UNROLL):
            r = NA_ROW_UNROLL * j + t
            o_ref[pl.ds(pl.multiple_of(r * GRID_W, GRID_W), GRID_W), :] = jnp.where(first, outs[2 * t], outs[2 * t + 1])
        return carry

    lax.fori_loop(0, rows // NA_ROW_UNROLL, rows_body, 0)


def _na_latent(qkv, cache_k2, cache_v2, tt2):
    rb0 = NC // DEC_SEQ
    return pl.pallas_call(
        _na_lat_kernel,
        grid=(DEC_BATCH, NA_HEADS // 2),
        in_specs=[pl.BlockSpec((DEC_SEQ, LANES), lambda b, hp: (rb0 + b, hp)),
                  pl.BlockSpec((DEC_SEQ, LANES), lambda b, hp: (rb0 + b, 8 + hp)),
                  pl.BlockSpec((DEC_SEQ, LANES), lambda b, hp: (rb0 + b, 16 + hp)),
                  pl.BlockSpec((None, 256, LANES), lambda b, hp: (b, 0, hp)),
                  pl.BlockSpec((None, 256, LANES), lambda b, hp: (b, 0, hp)),
                  pl.BlockSpec((2, 2 * WIN_R - 2, GRID_W, LANES), lambda b, hp: (hp, 0, 0, 0))],
        out_specs=pl.BlockSpec((DEC_SEQ, LANES), lambda b, hp: (b, hp)),
        out_shape=jax.ShapeDtypeStruct((NL, D), F32),
        compiler_params=_cparams(("arbitrary", "arbitrary")),
        name="na_latent",
    )(qkv, qkv, qkv, cache_k2, cache_v2, tt2)


def _rpb_tables(rpb):
    col = np.arange(GRID_W)
    start = np.clip(col - WIN_C // 2, 0, GRID_W - WIN_C)
    inside = (col[None, :] >= start[:, None]) & (col[None, :] < start[:, None] + WIN_C)
    dc = np.clip(col[None, :] - col[:, None] + (WIN_C - 1), 0, 2 * WIN_C - 2)
    t = jnp.where(inside[None, None], rpb[:, :, dc], NEG)
    return jnp.concatenate([t[:, :-1], t[:, 1:]], axis=-1)


def _router_kernel(x_ref, m_ref, g_ref, wr_ref, br_ref, tri_ref, xf_ref, ri_ref, rw_ref, cnt_ref, base_scr):
    i = pl.program_id(0)

    @pl.when(i == 0)
    def _():
        base_scr[...] = jnp.zeros_like(base_scr)

    h = _modulated_norm(x_ref[...], m_ref, g_ref, 3, 4)
    xf_ref[...] = _pack_bf16_pairs(h[:, :D // 2], h[:, D // 2:])
    logits = _mm_hi(h, wr_ref[...]) + br_ref[...]
    lane = lax.broadcasted_iota(jnp.int32, logits.shape, 1)
    rmax = lambda x: jnp.max(x, axis=-1, keepdims=True)
    rmin = lambda x: jnp.min(x, axis=-1, keepdims=True)
    rsum = lambda x: jnp.sum(x, axis=-1, keepdims=True)

    gmask = lane < N_GROUPS
    mg = rmax(jnp.where(gmask, logits, NEG))
    eg = jnp.where(gmask, jnp.exp(jnp.where(gmask, logits - mg, NEG)), 0.0)
    pg = eg / rsum(eg)
    p_grp = rmax(pg)
    grp = rmin(jnp.where(jnp.logical_and(gmask, pg == p_grp), lane, LANES))
    lo = N_GROUPS + grp * EXP_PER_GROUP
    emask = jnp.logical_and(lane >= lo, lane < lo + EXP_PER_GROUP)
    me = rmax(jnp.where(emask, logits, NEG))
    ee = jnp.where(emask, jnp.exp(jnp.where(emask, logits - me, NEG)), 0.0)
    pe = ee / rsum(ee)
    p1 = rmax(pe)
    i1 = rmin(jnp.where(jnp.logical_and(emask, pe == p1), lane, LANES))
    m2 = jnp.logical_and(emask, lane != i1)
    p2 = rmax(jnp.where(m2, pe, -1.0))
    i2 = rmin(jnp.where(jnp.logical_and(m2, pe == p2), lane, LANES))
    den = p1 + p2
    w1 = p_grp * p1 / den
    w2 = p_grp * p2 / den
    e1 = i1 - N_GROUPS
    e2 = i2 - N_GROUPS

    oh1 = jnp.where(lane == e1, 1.0, 0.0)
    oh2 = jnp.where(lane == e2, 1.0, 0.0)
    c1 = _mm(tri_ref[...], oh1)
    c2 = _mm(tri_ref[...], oh2)
    tot1 = jnp.sum(oh1, axis=0, keepdims=True)
    tot2 = jnp.sum(oh2, axis=0, keepdims=True)
    base = base_scr[0:1, :]
    rank1 = rsum(jnp.where(lane == e1, base + c1, 0.0))
    rank2 = rsum(jnp.where(lane == e2, base + tot1 + c2, 0.0))
    new_base = base + tot1 + tot2
    base_scr[...] = jnp.broadcast_to(new_base, base_scr.shape)
    cnt_ref[...] = jnp.broadcast_to(new_base, cnt_ref.shape)
    ri = jnp.where(lane == 0, e1, jnp.where(lane == 1, e2,
         jnp.where(lane == 2, rank1.astype(jnp.int32), jnp.where(lane == 3, rank2.astype(jnp.int32), 0))))
    ri_ref[...] = ri
    rw_ref[...] = jnp.where(lane == 0, w1, jnp.where(lane == 1, w2, 0.0))


def _router(x, m3, gain, wr, br, tri):
    return pl.pallas_call(
        _router_kernel,
        grid=(N // TM,),
        in_specs=[pl.BlockSpec((TM, D), lambda i: (i, 0)),
                  pl.BlockSpec((None, 6, D), lambda i: (_cond_row(i), 0, 0)),
                  pl.BlockSpec((1, D), lambda i: (0, 0)),
                  pl.BlockSpec((D, LANES), lambda i: (0, 0)),
                  pl.BlockSpec((1, LANES), lambda i: (0, 0)),
                  pl.BlockSpec((TM, TM), lambda i: (0, 0))],
        out_specs=[pl.BlockSpec((TM, D // 2), lambda i: (i, 0)),
                   pl.BlockSpec((TM, LANES), lambda i: (i, 0)),
                   pl.BlockSpec((TM, LANES), lambda i: (i, 0)),
                   pl.BlockSpec((8, LANES), lambda i: (0, 0))],
        out_shape=[jax.ShapeDtypeStruct((N, D // 2), jnp.int32),
                   jax.ShapeDtypeStruct((N, LANES), jnp.int32),
                   jax.ShapeDtypeStruct((N, LANES), F32),
                   jax.ShapeDtypeStruct((8, LANES), F32)],
        scratch_shapes=[pltpu.VMEM((8, LANES), F32)],
        compiler_params=_cparams(("arbitrary",)),
        name="router",
    )(x, m3, gain, wr, br, tri)


def _row_copy(src_hbm, row, dst, r, sem):
    return pltpu.make_async_copy(src_hbm.at[pl.ds(row, 1)], dst.at[pl.ds(r, 1)], sem)


def _expert_kernel(be_ref, nu_ref, st_ref, xp_hbm, wg_ref, wu_ref, wd_ref, ys_ref, xres, xbuf, sem):
    del be_ref
    b = pl.program_id(0)
    half = D // 2

    def gather(blk, slot):
        base = blk * MOE_ROWS
        for r in range(MOE_ROWS):
            xbuf[slot, pl.ds(r, 1), :] = xres[pl.ds(st_ref[base + r], 1), :]

    @pl.when(b == 0)
    def _():
        cp = pltpu.make_async_copy(xp_hbm, xres, sem.at[0])
        cp.start()
        cp.wait()
        gather(0, 0)

    @pl.when(b < nu_ref[0])
    def _():
        gather(jnp.minimum(b + 1, N_SLOT_BLOCKS - 1), (b + 1) % 2)
        x_lo, x_hi = _unpack_bf16_pairs(xbuf[b % 2])
        dot = functools.partial(jnp.dot, preferred_element_type=F32)
        g = dot(x_lo, wg_ref[:half, :].astype(BF16)) + dot(x_hi, wg_ref[half:, :].astype(BF16))
        u = dot(x_lo, wu_ref[:half, :].astype(BF16)) + dot(x_hi, wu_ref[half:, :].astype(BF16))
        ys_ref[...] = _mm(_silu(g) * u, wd_ref[...])

    @pl.when(b >= nu_ref[0])
    def _():
        ys_ref[...] = jnp.zeros_like(ys_ref)


def _experts(block_e, n_used, slot_tok, xp, w_gate, w_up, w_down, layer):
    grid_spec = pltpu.PrefetchScalarGridSpec(
        num_scalar_prefetch=3,
        grid=(N_SLOT_BLOCKS,),
        in_specs=[pl.BlockSpec(memory_space=pl.ANY),
                  pl.BlockSpec((None, None, D, D_EXPERT), lambda b, be, nu, st: (layer, be[b], 0, 0)),
                  pl.BlockSpec((None, None, D, D_EXPERT), lambda b, be, nu, st: (layer, be[b], 0, 0)),
                  pl.BlockSpec((None, None, D_EXPERT, D), lambda b, be, nu, st: (layer, be[b], 0, 0))],
        out_specs=pl.BlockSpec((MOE_ROWS, D), lambda b, be, nu, st: (b, 0)),
        scratch_shapes=[pltpu.VMEM((N, D // 2), jnp.int32),
                        pltpu.VMEM((2, MOE_ROWS, D // 2), jnp.int32),
                        pltpu.SemaphoreType.DMA((1,))])
    return pl.pallas_call(
        _expert_kernel,
        grid_spec=grid_spec,
        out_shape=jax.ShapeDtypeStruct((N_SLOT_BLOCKS * MOE_ROWS, D), F32),
        compiler_params=pltpu.CompilerParams(dimension_semantics=("arbitrary",), vmem_limit_bytes=EXPERT_VMEM_LIMIT),
        name="experts",
    )(block_e, n_used, slot_tok, xp, w_gate, w_up, w_down)


def _combine_kernel(d_ref, ys_hbm, x_ref, m_ref, rw_ref, fn_ref, o_ref, buf, sem, *, final, tile0, n_tiles):
    i = pl.program_id(0)

    def issue(tile, slot):
        base = (tile0 + tile) * (2 * TM)
        for r in range(TM):
            for kk in range(2):
                _row_copy(ys_hbm, d_ref[base + 2 * r + kk], buf.at[slot, kk], r, sem.at[slot]).start()

    @pl.when(i == 0)
    def _():
        issue(0, 0)

    @pl.when(i + 1 < n_tiles)
    def _():
        issue(i + 1, (i + 1) % 2)

    slot = i % 2
    for kk in range(2):
        pltpu.make_async_copy(ys_hbm.at[pl.ds(0, TM)], buf.at[slot, kk], sem.at[slot]).wait()
    w = rw_ref[...]
    y = w[:, 0:1] * buf[slot, 0] + w[:, 1:2] * buf[slot, 1]
    out = x_ref[...] + m_ref[5:6, :] * y
    if final:
        ms = jnp.mean(out * out, axis=-1, keepdims=True)
        out = out * lax.rsqrt(ms + EPS) * fn_ref[...]
    o_ref[...] = out


def _combine(dest_flat, ys, x, m3, rw, final_norm, final, tile0=0, n_tiles=N // TM):
    grid_spec = pltpu.PrefetchScalarGridSpec(
        num_scalar_prefetch=1,
        grid=(n_tiles,),
        in_specs=[pl.BlockSpec(memory_space=pl.ANY),
                  pl.BlockSpec((TM, D), lambda i, d: (tile0 + i, 0)),
                  pl.BlockSpec((None, 6, D), lambda i, d: (_cond_row(tile0 + i), 0, 0)),
                  pl.BlockSpec((TM, LANES), lambda i, d: (tile0 + i, 0)),
                  pl.BlockSpec((1, D), lambda i, d: (0, 0))],
        out_specs=pl.BlockSpec((TM, D), lambda i, d: (i, 0)),
        scratch_shapes=[pltpu.VMEM((2, 2, TM, D), F32), pltpu.SemaphoreType.DMA((2,))])
    return pl.pallas_call(
        functools.partial(_combine_kernel, final=final, tile0=tile0, n_tiles=n_tiles),
        grid_spec=grid_spec,
        out_shape=jax.ShapeDtypeStruct((n_tiles * TM, D), F32),
        compiler_params=_cparams(("arbitrary",)),
        name="combine",
    )(dest_flat, ys, x, m3, rw, final_norm)


def _hier_moe(x, m3, gain, layer, w_rg, b_rg, w_re, b_re, w_gate, w_up, w_down, tri_tm, final_norm, final):
    wr = jnp.concatenate([w_rg, w_re.transpose(1, 0, 2).reshape(D, N_EXPERTS),
                          jnp.zeros((D, LANES - N_GROUPS - N_EXPERTS), F32)], axis=1)
    br = jnp.concatenate([b_rg, b_re.reshape(N_EXPERTS),
                          jnp.zeros((LANES - N_GROUPS - N_EXPERTS,), F32)])[None, :]
    xf, ri, rw, cnt = _router(x, m3, gain, wr, br, tri_tm)
    e_idx = ri[:, 0:2]
    rank = ri[:, 2:4]
    counts = cnt[0, :N_EXPERTS].astype(jnp.int32)
    padded = (counts + MOE_ROWS - 1) // MOE_ROWS * MOE_ROWS
    end_pad = jnp.cumsum(padded)
    start_pad = end_pad - padded
    experts = jnp.arange(N_EXPERTS, dtype=jnp.int32)
    start_of = jnp.sum(jnp.where(e_idx[:, :, None] == experts, start_pad, 0), axis=-1)
    dest = (start_of + rank).reshape(-1).astype(jnp.int32)
    block_start = jnp.arange(N_SLOT_BLOCKS, dtype=jnp.int32) * MOE_ROWS
    block_e = jnp.minimum(jnp.sum((end_pad[None, :] <= block_start[:, None]).astype(jnp.int32), axis=1),
                          N_EXPERTS - 1)
    n_used = (end_pad[-1:] // MOE_ROWS).astype(jnp.int32)
    tok = jnp.repeat(jnp.arange(N, dtype=jnp.int32), 2)
    slot_tok = jnp.zeros((N_SLOT_BLOCKS * MOE_ROWS,), jnp.int32).at[dest].set(tok, unique_indices=True)
    ys = _experts(block_e, n_used, slot_tok, xf, w_gate, w_up, w_down, layer)
    if not final:
        return _combine(dest, ys, x, m3, rw, final_norm, False)
    nct = NC // TM
    return (_combine(dest, ys, x, m3, rw, final_norm, True, 0, nct),
            _combine(dest, ys, x, m3, rw, final_norm, True, nct, N // TM - nct))


def kernel(x_prompt, x_sample, state_A_fwd, state_A_bwd, cache_k, cache_v, c, c_ctx, mod_w, mod_b, norm_mix, norm_ffn, ab_w_in, ab_conv, ab_a_log, ab_dt_bias, ab_o_gain, ab_w_out, na_w_qkv, na_rpb, na_w_out, moe_w_rg, moe_b_rg, moe_w_re, moe_b_re, moe_w_gate, moe_w_up, moe_w_down, final_norm):
    x = (x_prompt.reshape(NC, D), x_sample.reshape(NL, D))
    cond8 = jnp.concatenate([c_ctx[None, :], c, jnp.zeros((8 - 1 - DEC_BATCH, D), F32)], axis=0)
    mods = _ada_params(cond8, mod_w, mod_b).reshape(DEPTH, 8, 6, D)

    tri_tm = jnp.asarray(np.tril(np.ones((TM, TM)), -1), BF16)
    tri_c = jnp.asarray(np.tril(np.ones((CHUNK, CHUNK))), F32)
    half = np.arange(LANES) < A_DH
    jbd = jnp.asarray((half[:, None] == half[None, :]).astype(np.float32))
    fn = final_norm[None, :]

    m3 = mods[0]
    w_in = ab_w_in[0]
    w_in = jnp.concatenate([w_in[:, :2048], w_in[:, 2080:2592], w_in[:, 2048:2080],
                            jnp.zeros((D, AB_COLS - 2592), F32)], axis=1).astype(BF16)
    proj = _modproj(x, m3, norm_mix[0][None, :], w_in, 896)
    gate_p = jnp.zeros((8, LANES), F32)
    gate_p = gate_p.at[0, :2 * A_HEADS].set(ab_a_log[0].reshape(-1))
    gate_p = gate_p.at[1, :2 * A_HEADS].set(ab_dt_bias[0].reshape(-1))
    o_gain2 = jnp.tile(ab_o_gain[0], 2)[None, :]
    zeros_state = jnp.zeros((BATCH, A_HEADS, A_DH, A_DH), F32)
    mix_a_c, s_f, s_b = _deltanet(proj, ab_conv[0], gate_p, o_gain2, zeros_state, zeros_state, tri_c, jbd,
                                  seq_len=SEQ, n_seq=BATCH, n_sub=4, row_blk0=0)
    mix_a_l, _, _ = _deltanet(proj, ab_conv[0], gate_p, o_gain2, state_A_fwd[:, 0], state_A_bwd[:, 0], tri_c, jbd,
                              seq_len=DEC_SEQ, n_seq=DEC_BATCH, n_sub=1, row_blk0=NC // DEC_SEQ)

    cc, sc = _dft_mats(LANES)
    ct, st = _dft_mats(SEQ)
    c64, s64 = _dft_mats(64)
    cs = jnp.asarray(np.concatenate([cc, sc], axis=1), BF16)
    dft = jnp.asarray(np.concatenate([ct, -st], axis=1), BF16)
    ca = jnp.asarray(np.concatenate([cc, -sc, -sc, -cc], axis=1), BF16)
    m1 = jnp.asarray(np.concatenate([c64, s64], axis=1), BF16)
    tw_idx = np.arange(64)
    tw_ang = 2.0 * np.pi * (tw_idx[:, None] * tw_idx[None, :]) / DEC_SEQ
    twc = jnp.broadcast_to(jnp.asarray(np.cos(tw_ang), F32)[:, :, None], (64, 64, LANES))
    tws = jnp.broadcast_to(jnp.asarray(np.sin(tw_ang), F32)[:, :, None], (64, 64, LANES))
    mix_b_c = _fnet_ctx(proj, cs, dft)
    mix_b_l = _fnet_latent(proj, ca, m1, twc, tws)

    w_out = ab_w_out[0].astype(BF16)
    x = _outproj([(mix_a_c, mix_a_l), (mix_b_c, mix_b_l)], [w_out[:A_WIDTH], w_out[A_WIDTH:]], x, m3)
    x = _hier_moe(x, m3, norm_ffn[0][None, :], 0, moe_w_rg[0], moe_b_rg[0], moe_w_re[0], moe_b_re[0],
                  moe_w_gate, moe_w_up, moe_w_down, tri_tm, fn, False)

    m3 = mods[1]
    qkv = _modproj(x, m3, norm_mix[1][None, :], na_w_qkv[0].astype(BF16), 512)
    attn_c = _na_ctx(qkv)
    attn_l = _na_latent(qkv, cache_k[:, 0].reshape(DEC_BATCH, 256, D), cache_v[:, 0].reshape(DEC_BATCH, 256, D),
                        _rpb_tables(na_rpb[0]))
    x = _outproj([(attn_c, attn_l)], [na_w_out[0].astype(BF16)], x, m3)
    y_c, y_l = _hier_moe(x, m3, norm_ffn[1][None, :], 1, moe_w_rg[1], moe_b_rg[1], moe_w_re[1], moe_b_re[1],
                         moe_w_gate, moe_w_up, moe_w_down, tri_tm, fn, True)

    new_k = qkv[:NC, D:2 * D].reshape(BATCH, 1, SEQ, NA_HEADS, NA_DH)
    new_v = qkv[:NC, 2 * D:].reshape(BATCH, 1, SEQ, NA_HEADS, NA_DH)
    return (y_c.reshape(BATCH, SEQ, D), y_l.reshape(DEC_BATCH, DEC_SEQ, D),
            s_f[:, None], s_b[:, None], new_k, new_v)
```

```python
import functools
import math

import numpy as np
import jax
import jax.numpy as jnp
from jax import lax
from jax.experimental import pallas as pl
from jax.experimental.pallas import tpu as pltpu

F32 = jnp.float32
BF16 = jnp.bfloat16
HIGHEST = lax.Precision.HIGHEST

D = 1024
BATCH, SEQ = 32, 256
DEC_BATCH, DEC_SEQ = 2, 4096
NC = BATCH * SEQ
NL = DEC_BATCH * DEC_SEQ
N = NC + NL
DEPTH = 2
GRID_W = 64
A_DH = 64
A_HEADS = 8
A_WIDTH = 512
CHUNK = 64
B_WIDTH = 512
B_GROUPS = 4
NA_DH = 64
NA_HEADS = 16
WIN_R, WIN_C = 8, 16
N_GROUPS, EXP_PER_GROUP, N_EXPERTS = 4, 8, 32
D_EXPERT = 512
EPS = 1e-6

LANES = 128
TM = 256
MOE_ROWS = 256
N_SLOT_BLOCKS = (2 * N) // MOE_ROWS + N_EXPERTS
AB_COLS = 2688
VMEM_LIMIT = 56 * 1024 * 1024
EXPERT_VMEM_LIMIT = 60 * 1024 * 1024
NEG = -1e30


def _cparams(sem):
    return pltpu.CompilerParams(dimension_semantics=sem, vmem_limit_bytes=VMEM_LIMIT)


def _mm(a, b):
    return jnp.dot(a.astype(BF16), b.astype(BF16), preferred_element_type=F32)


def _mm_nt(a, b):
    return lax.dot_general(a.astype(BF16), b.astype(BF16), (((1,), (1,)), ((), ())),
                           preferred_element_type=F32)


SOLVE_BLK = 16
NA_CTX_PAIRS = 2
NA_ROW_UNROLL = 4
TERM_UNROLL = 4


def _unit_lower_solve(a_mat, rhs, same_blk, eye):
    dg = jnp.where(same_blk, a_mat, 0.0)
    p = -dg
    dinv = eye + p
    for _ in range(int(math.log2(SOLVE_BLK)) - 1):
        p = _mm(p, p)
        dinv = dinv + _mm(p, dinv)
    mp = -_mm(dinv, a_mat - dg)
    y = _mm(dinv, rhs)
    y = y + _mm(mp, y)
    for _ in range(int(math.log2(a_mat.shape[0] // SOLVE_BLK)) - 1):
        mp = _mm(mp, mp)
        y = y + _mm(mp, y)
    return y


def _unit_lower_solve_many(a_mats, rhs, same_blk, eye):
    dg = [jnp.where(same_blk, a, 0.0) for a in a_mats]
    off = [a - g for a, g in zip(a_mats, dg)]
    p = [-g for g in dg]
    dinv = [eye + x for x in p]
    for _ in range(int(math.log2(SOLVE_BLK)) - 1):
        p = [_mm(x, x) for x in p]
        dinv = [di + _mm(x, di) for x, di in zip(p, dinv)]
    mp = [-_mm(di, o) for di, o in zip(dinv, off)]
    y = [_mm(di, r) for di, r in zip(dinv, rhs)]
    y = [yi + _mm(m, yi) for m, yi in zip(mp, y)]
    for _ in range(int(math.log2(a_mats[0].shape[0] // SOLVE_BLK)) - 1):
        mp = [_mm(m, m) for m in mp]
        y = [yi + _mm(m, yi) for m, yi in zip(mp, y)]
    return y


def _mm_split(a, b, parts, split_rhs=False):
    x = b if split_rhs else a
    acc = None
    for _ in range(parts):
        piece = x.astype(BF16)
        x = x - piece.astype(F32)
        term = (jnp.dot(a.astype(BF16), piece, preferred_element_type=F32) if split_rhs
                else jnp.dot(piece, b.astype(BF16), preferred_element_type=F32))
        acc = term if acc is None else acc + term
    return acc


def _mm_hi(a, b):
    return jnp.dot(a, b, preferred_element_type=F32, precision=HIGHEST)


def _silu(x):
    return x * jax.nn.sigmoid(x)


def _bf16_bits(x):
    b = lax.bitcast_convert_type(x, jnp.int32)
    return b + 0x7FFF + (lax.shift_right_logical(b, jnp.int32(16)) & 1)


_HIGH16 = -65536


def _pack_bf16_pairs(a, b):
    return lax.shift_right_logical(_bf16_bits(a), jnp.int32(16)) | (_bf16_bits(b) & _HIGH16)


def _unpack_bf16_pairs(p):
    a = lax.bitcast_convert_type(lax.shift_left(p, jnp.int32(16)), F32)
    b = lax.bitcast_convert_type(p & _HIGH16, F32)
    return a.astype(BF16), b.astype(BF16)


def _cond_row(i):
    return jnp.where(i < NC // TM, 0, 1 + (i - NC // TM) // (DEC_SEQ // TM))


def _modulated_norm(x, m_ref, g_ref, shift_idx, scale_idx):
    ms = jnp.mean(x * x, axis=-1, keepdims=True)
    y = x * lax.rsqrt(ms + EPS) * g_ref[...]
    return y * (1.0 + m_ref[scale_idx:scale_idx + 1, :]) + m_ref[shift_idx:shift_idx + 1, :]


def _ada_kernel(cond_ref, w_ref, b_ref, o_ref):
    o_ref[...] = _mm_hi(_silu(cond_ref[...]), w_ref[...]) + b_ref[...]


def _ada_params(cond8, mod_w, mod_b):
    tn = 1536
    return pl.pallas_call(
        _ada_kernel,
        grid=(DEPTH, 6 * D // tn),
        in_specs=[pl.BlockSpec((8, D), lambda l, j: (0, 0)),
                  pl.BlockSpec((None, D, tn), lambda l, j: (l, 0, j)),
                  pl.BlockSpec((None, 1, tn), lambda l, j: (l, 0, j))],
        out_specs=pl.BlockSpec((None, 8, tn), lambda l, j: (l, 0, j)),
        out_shape=jax.ShapeDtypeStruct((DEPTH, 8, 6 * D), F32),
        compiler_params=_cparams(("arbitrary", "arbitrary")),
        name="ada_params",
    )(cond8, mod_w, mod_b.reshape(DEPTH, 1, 6 * D))


def _token_specs(x):
    nct = NC // TM
    if isinstance(x, tuple):
        return ([pl.BlockSpec((TM, x[0].shape[1]), lambda i: (jnp.minimum(i, nct - 1), 0)),
                 pl.BlockSpec((TM, x[1].shape[1]), lambda i: (jnp.maximum(i - nct, 0), 0))], list(x))
    return [pl.BlockSpec((TM, x.shape[1]), lambda i: (i, 0))], [x]


def _token_rows(refs):
    if len(refs) == 1:
        return refs[0][...]
    return jnp.where(pl.program_id(0) < NC // TM, refs[0][...], refs[1][...])


def _modproj_kernel(*refs, n_chunk):
    m_ref, g_ref, w_ref, o_ref = refs[-4:]
    hb = _modulated_norm(_token_rows(refs[:-4]), m_ref, g_ref, 0, 1).astype(BF16)
    for j in range(o_ref.shape[1] // n_chunk):
        sl = slice(j * n_chunk, (j + 1) * n_chunk)
        o_ref[:, sl] = jnp.dot(hb, w_ref[:, sl], preferred_element_type=F32)


def _modproj(x, m3, gain, w_bf16, n_chunk):
    nout = w_bf16.shape[1]
    x_specs, x_args = _token_specs(x)
    return pl.pallas_call(
        functools.partial(_modproj_kernel, n_chunk=n_chunk),
        grid=(N // TM,),
        in_specs=x_specs + [pl.BlockSpec((None, 6, D), lambda i: (_cond_row(i), 0, 0)),
                            pl.BlockSpec((1, D), lambda i: (0, 0)),
                            pl.BlockSpec((D, nout), lambda i: (0, 0))],
        out_specs=pl.BlockSpec((TM, nout), lambda i: (i, 0)),
        out_shape=jax.ShapeDtypeStruct((N, nout), F32),
        compiler_params=_cparams(("arbitrary",)),
        name="modproj",
    )(*x_args, m3, gain, w_bf16)


def _outproj_kernel(*refs, n_in):
    a_refs = refs[:2 * n_in]
    w_refs = refs[2 * n_in:3 * n_in]
    x_refs = refs[3 * n_in:-2]
    m_ref, o_ref = refs[-2:]
    acc = None
    for j, w_ref in enumerate(w_refs):
        part = _mm(_token_rows(a_refs[2 * j:2 * j + 2]), w_ref[...])
        acc = part if acc is None else acc + part
    o_ref[...] = _token_rows(x_refs) + m_ref[2:3, :] * acc


def _outproj(a_pairs, w_list, x, m3):
    n_in = len(a_pairs)
    in_specs, args = [], []
    for pair in a_pairs:
        specs, ops = _token_specs(pair)
        in_specs += specs
        args += ops
    x_specs, x_args = _token_specs(x)
    in_specs += ([pl.BlockSpec(w.shape, lambda i: (0, 0)) for w in w_list] + x_specs
                 + [pl.BlockSpec((None, 6, D), lambda i: (_cond_row(i), 0, 0))])
    return pl.pallas_call(
        functools.partial(_outproj_kernel, n_in=n_in),
        grid=(N // TM,),
        in_specs=in_specs,
        out_specs=pl.BlockSpec((TM, D), lambda i: (i, 0)),
        out_shape=jax.ShapeDtypeStruct((N, D), F32),
        compiler_params=_cparams(("arbitrary",)),
        name="outproj",
    )(*args, *w_list, *x_args, m3)


def _deltanet_kernel(q_ref, k_ref, v_ref, z_ref, ab_ref, cq_ref, ck_ref, cv_ref, gp_ref, og_ref,
                     s0f_ref, s0b_ref, tri_ref, jbd_ref, o_ref, sf_ref, sb_ref,
                     u_s, w_s, qd_s, at_s, kt_s, ge_s, st_s, ob, *, seq_len, n_sub):
    hp = pl.program_id(1)
    C = CHUNK
    nc = seq_len // C
    nct = n_sub * nc
    P = LANES
    lane = lax.broadcasted_iota(jnp.int32, (C, P), 1)
    row = lax.broadcasted_iota(jnp.int32, (C, P), 0)
    first_head = lane < A_DH
    ri = lax.broadcasted_iota(jnp.int32, (P, P), 0)
    ci = lax.broadcasted_iota(jnp.int32, (P, P), 1)
    same_head = (ri < C) == (ci < C)
    same_blk = (ri // SOLVE_BLK) == (ci // SOLVE_BLK)
    eye = jnp.where(ri == ci, 1.0, 0.0)
    jbd = jbd_ref[...]
    lincl = tri_ref[...]
    tri_b = [lincl.astype(BF16), lincl.T.astype(BF16)]
    incl_m = [jnp.logical_and(same_head, ri >= ci), jnp.logical_and(same_head, ri <= ci)]
    strict_m = [jnp.logical_and(same_head, ri > ci), jnp.logical_and(same_head, ri < ci)]
    neg_a = -jnp.exp(gp_ref[0:1, :])
    dt_b = gp_ref[1:2, :]

    def conv_silu(ref, w_ref, c):
        base = pl.multiple_of(c * C, C)
        cs = c % nc
        xc = ref[pl.ds(base, C), :]
        pbase = pl.multiple_of(jnp.maximum(base - 8, 0), 8)
        nbase = pl.multiple_of(jnp.minimum(base + C, n_sub * seq_len - 8), 8)
        prev_row = ref[pl.ds(pbase, 8), :][7:8, :] * jnp.where(cs > 0, 1.0, 0.0)
        next_row = ref[pl.ds(nbase, 8), :][0:1, :] * jnp.where(cs < nc - 1, 1.0, 0.0)
        x_prev = jnp.where(row == 0, prev_row, pltpu.roll(xc, 1, 0))
        x_next = jnp.where(row == C - 1, next_row, pltpu.roll(xc, C - 1, 0))
        y = w_ref[0:1, :] * x_prev + w_ref[1:2, :] * xc + w_ref[2:3, :] * x_next
        return _silu(y)

    def stack(x):
        return jnp.concatenate([jnp.where(first_head, x, 0.0), jnp.where(first_head, 0.0, x)], axis=0)

    def lane_col(arr, idx):
        return jnp.sum(jnp.where(lane == idx, arr, 0.0), axis=-1, keepdims=True)

    def chunk_inputs(c):
        base = pl.multiple_of(c * C, C)
        q = conv_silu(q_ref, cq_ref, c)
        k = conv_silu(k_ref, ck_ref, c)
        v = conv_silu(v_ref, cv_ref, c)
        ab = ab_ref[pl.ds(base, C), :]
        g_all = neg_a * jax.nn.softplus(ab + dt_b)
        beta_all = jax.nn.sigmoid(ab)
        return q, k, v, g_all, beta_all

    def chain_gates(g_all, beta_all, gc_all, d):
        cols = []
        for a in range(2):
            g_lane = d * A_HEADS + 2 * hp + a
            g_col = lane_col(g_all, g_lane)
            tot = jnp.broadcast_to(jnp.sum(g_col, axis=0, keepdims=True), (C, 1))
            cols.append((lane_col(gc_all, g_lane), lane_col(beta_all, 2 * A_HEADS + g_lane), tot))
        return [jnp.concatenate([cols[0][j], cols[1][j]], axis=0) for j in range(3)]

    def terms_body(j, carry):
        cs = [TERM_UNROLL * j + t for t in range(TERM_UNROLL)]
        ins = [chunk_inputs(c) for c in cs]
        qsq = [_mm_split(x[0] * x[0], jbd, 2) for x in ins]
        ksq = [_mm_split(x[1] * x[1], jbd, 2) for x in ins]
        qs = [x[0] * lax.rsqrt(s + EPS) * (A_DH ** -0.5) for x, s in zip(ins, qsq)]
        ks = [x[1] * lax.rsqrt(s + EPS) for x, s in zip(ins, ksq)]
        qst = [stack(x) for x in qs]
        kst = [stack(x) for x in ks]
        vst = [stack(x[2]) for x in ins]
        kk = [_mm_nt(x, x) for x in kst]
        qk = [_mm_nt(x, y) for x, y in zip(qst, kst)]
        chains = [(t, d) for t in range(TERM_UNROLL) for d in range(2)]
        gc_all = [_mm_split(tri_b[d], ins[t][3], 3, split_rhs=True) for t, d in chains]
        gates = [chain_gates(ins[t][3], ins[t][4], gc, d) for (t, d), gc in zip(chains, gc_all)]
        decay, e_gc = [], []
        for (t, d), (gc_col, beta_col, tot_col) in zip(chains, gates):
            gcb = jnp.broadcast_to(gc_col, (P, P))
            decay.append(jnp.where(incl_m[d], jnp.exp(jnp.where(incl_m[d], gcb - gcb.T, 0.0)), 0.0))
            e_gc.append(jnp.exp(gc_col))
        a_mats = [jnp.where(strict_m[d], g[1] * kk[t] * dc, 0.0) for (t, d), g, dc in zip(chains, gates, decay)]
        rhs = [jnp.concatenate([vst[t] * g[1], kst[t] * (g[1] * e)], axis=1)
               for (t, d), g, e in zip(chains, gates, e_gc)]
        xs = _unit_lower_solve_many(a_mats, rhs, same_blk, eye)
        for (t, d), x, g, e, dc in zip(chains, xs, gates, e_gc, decay):
            c = cs[t]
            u_s[d, c] = x[:, :P].astype(BF16)
            w_s[d, c] = x[:, P:].astype(BF16)
            qd_s[d, c] = (qst[t] * e).astype(BF16)
            at_s[d, c] = jnp.where(incl_m[d], qk[t] * dc, 0.0).astype(BF16)
            kt_s[d, c] = (kst[t] * jnp.exp(g[2] - g[0])).T.astype(BF16)
            e_tot = jnp.exp(g[2])
            ge_s[d, c] = jnp.where(lax.broadcasted_iota(jnp.int32, (8, P), 1) < A_DH,
                                   e_tot[0:1, :], e_tot[C:C + 1, :])
        return carry

    lax.fori_loop(0, nct // TERM_UNROLL, terms_body, 0)

    def block_diag(s2):
        z = jnp.zeros((A_DH, A_DH), F32)
        return jnp.concatenate([jnp.concatenate([s2[0], z], axis=1),
                                jnp.concatenate([z, s2[1]], axis=1)], axis=0)

    for s in range(n_sub):
        st_s[2 * s] = block_diag(s0f_ref[s])
        st_s[2 * s + 1] = block_diag(s0b_ref[s])

    def scan_body(i, carry):
        chains = [(s, d, s * nc + (i if d == 0 else nc - 1 - i)) for s in range(n_sub) for d in range(2)]
        dot = functools.partial(jnp.dot, preferred_element_type=F32)
        s_bd = [st_s[2 * s + d] for s, d, c in chains]
        sb16 = [x.astype(BF16) for x in s_bd]
        ws = [dot(w_s[d, c], sb) for (s, d, c), sb in zip(chains, sb16)]
        qs_ = [dot(qd_s[d, c], sb) for (s, d, c), sb in zip(chains, sb16)]
        vb = [(u_s[d, c].astype(F32) - x).astype(BF16) for (s, d, c), x in zip(chains, ws)]
        av = [dot(at_s[d, c], x) for (s, d, c), x in zip(chains, vb)]
        kv = [dot(kt_s[d, c], x) for (s, d, c), x in zip(chains, vb)]
        for (s, d, c), sb, q_, a_, k_ in zip(chains, s_bd, qs_, av, kv):
            st_s[2 * s + d] = sb * ge_s[d, c][0:1, :] + k_
            o_st = q_ + a_
            dst = o_ref if d == 0 else ob
            dst[pl.ds(pl.multiple_of(c * C, C), C), :] = o_st[:C] + o_st[C:]
        return carry

    lax.fori_loop(0, nc, scan_body, 0)

    for s in range(n_sub):
        for d, ref in ((0, sf_ref), (1, sb_ref)):
            s_bd = st_s[2 * s + d]
            ref[s, 0] = s_bd[:A_DH, :A_DH]
            ref[s, 1] = s_bd[A_DH:, A_DH:]

    def finish(j, carry):
        bases = [pl.multiple_of((TERM_UNROLL * j + t) * C, C) for t in range(TERM_UNROLL)]
        o = [o_ref[pl.ds(b, C), :] + ob[pl.ds(b, C), :] for b in bases]
        ms = [_mm_split(x * x, jbd, 2) * (1.0 / A_DH) for x in o]
        for b, x, m in zip(bases, o, ms):
            o_ref[pl.ds(b, C), :] = x * lax.rsqrt(m + EPS) * og_ref[...] * _silu(z_ref[pl.ds(b, C), :])
        return carry

    lax.fori_loop(0, nct // TERM_UNROLL, finish, 0)


def _deltanet(proj, conv_w, gate_p, o_gain2, s0f, s0b, tri, jbd, *, seq_len, n_seq, n_sub, row_blk0):
    rows = n_sub * seq_len
    nct = rows // CHUNK
    rb = lambda b: row_blk0 + b
    col = lambda off: (lambda b, hp: (rb(b), off + hp))
    st_spec = pl.BlockSpec((n_sub, 2, A_DH, A_DH), lambda b, hp: (b, hp, 0, 0))
    in_specs = [pl.BlockSpec((rows, LANES), col(0)),
                pl.BlockSpec((rows, LANES), col(4)),
                pl.BlockSpec((rows, LANES), col(8)),
                pl.BlockSpec((rows, LANES), col(12)),
                pl.BlockSpec((rows, LANES), lambda b, hp: (rb(b), 20)),
                pl.BlockSpec((3, LANES), lambda b, hp: (0, hp)),
                pl.BlockSpec((3, LANES), lambda b, hp: (0, 4 + hp)),
                pl.BlockSpec((3, LANES), lambda b, hp: (0, 8 + hp)),
                pl.BlockSpec((8, LANES), lambda b, hp: (0, 0)),
                pl.BlockSpec((1, LANES), lambda b, hp: (0, 0)),
                st_spec, st_spec,
                pl.BlockSpec((CHUNK, CHUNK), lambda b, hp: (0, 0)),
                pl.BlockSpec((LANES, LANES), lambda b, hp: (0, 0))]
    args = [proj, proj, proj, proj, proj, conv_w, conv_w, conv_w, gate_p, o_gain2, s0f, s0b, tri, jbd]
    st_shape = jax.ShapeDtypeStruct((n_seq, A_HEADS, A_DH, A_DH), F32)
    tile = lambda dt: pltpu.VMEM((2, nct, LANES, LANES), dt)
    return pl.pallas_call(
        functools.partial(_deltanet_kernel, seq_len=seq_len, n_sub=n_sub),
        grid=(n_seq // n_sub, A_HEADS // 2),
        in_specs=in_specs,
        out_specs=[pl.BlockSpec((rows, LANES), lambda b, hp: (b, hp)), st_spec, st_spec],
        out_shape=[jax.ShapeDtypeStruct((n_seq * seq_len, A_WIDTH), F32), st_shape, st_shape],
        scratch_shapes=[tile(BF16), tile(BF16), tile(BF16), tile(BF16), tile(BF16),
                        pltpu.VMEM((2, nct, 8, LANES), F32),
                        pltpu.VMEM((2 * n_sub, LANES, LANES), F32),
                        pltpu.VMEM((rows, LANES), F32)],
        compiler_params=_cparams(("arbitrary", "arbitrary")),
        name="deltanet",
    )(*args)


def _dft_mats(n):
    idx = np.arange(n)
    ang = 2.0 * np.pi * ((idx[:, None] * idx[None, :]) % n) / n
    return np.cos(ang), np.sin(ang)


def _fnet_ctx_kernel(u_ref, cs_ref, dft_ref, o_ref):
    norm = 1.0 / math.sqrt(SEQ * LANES)
    for g in range(B_GROUPS):
        sl = slice(g * LANES, (g + 1) * LANES)
        p = _mm(u_ref[:, sl], cs_ref[...])
        stack = jnp.concatenate([p[:, :LANES], p[:, LANES:]], axis=0)
        o_ref[:, sl] = _mm(dft_ref[...], stack) * norm


def _fnet_ctx(proj, cs, dft):
    return pl.pallas_call(
        _fnet_ctx_kernel,
        grid=(BATCH,),
        in_specs=[pl.BlockSpec((SEQ, B_WIDTH), lambda b: (b, 4)),
                  pl.BlockSpec(cs.shape, lambda b: (0, 0)),
                  pl.BlockSpec(dft.shape, lambda b: (0, 0))],
        out_specs=pl.BlockSpec((SEQ, B_WIDTH), lambda b: (b, 0)),
        out_shape=jax.ShapeDtypeStruct((NC, B_WIDTH), F32),
        compiler_params=_cparams(("arbitrary",)),
        name="fnet_ctx",
    )(proj, cs, dft)


FN_SUB = 4


def _fnet_lat1_kernel(u_ref, ca_ref, m1_ref, twc_ref, tws_ref, o_ref):
    r = 64
    for j in range(FN_SUB):
        c = twc_ref[j]
        s = tws_ref[j]
        rows = slice(j * r, (j + 1) * r)
        for g in range(B_GROUPS):
            pa = _mm(u_ref[rows, g * LANES:(g + 1) * LANES], ca_ref[...])
            rhs = jnp.concatenate([pa[:, :2 * LANES], pa[:, 2 * LANES:]], axis=0)
            zz = _mm(m1_ref[...], rhs)
            zr, zi = zz[:, :LANES], zz[:, LANES:]
            o_ref[rows, 2 * g * LANES:(2 * g + 1) * LANES] = (zr * c + zi * s).astype(BF16)
            o_ref[rows, (2 * g + 1) * LANES:(2 * g + 2) * LANES] = (zi * c - zr * s).astype(BF16)


def _fnet_lat2_kernel(z_ref, m1_ref, o_ref):
    r = 64
    norm = 1.0 / math.sqrt(DEC_SEQ * LANES)
    for j in range(FN_SUB):
        rows = slice(j * r, (j + 1) * r)
        for g in range(B_GROUPS):
            rhs = jnp.concatenate([z_ref[rows, 2 * g * LANES:(2 * g + 1) * LANES],
                                   z_ref[rows, (2 * g + 1) * LANES:(2 * g + 2) * LANES]], axis=0)
            o_ref[rows, g * LANES:(g + 1) * LANES] = _mm(m1_ref[...], rhs) * norm


def _fnet_latent(proj, ca, m1, twc, tws):
    r = 64
    u = proj[NC:, 2048:2560].astype(BF16).reshape(DEC_BATCH, r, r, B_WIDTH)
    u = u.transpose(0, 2, 1, 3).reshape(DEC_BATCH * r * r, B_WIDTH)
    steps = DEC_BATCH * r // FN_SUB
    blk = FN_SUB * r
    z = pl.pallas_call(
        _fnet_lat1_kernel,
        grid=(steps,),
        in_specs=[pl.BlockSpec((blk, B_WIDTH), lambda s: (s, 0)),
                  pl.BlockSpec(ca.shape, lambda s: (0, 0)),
                  pl.BlockSpec(m1.shape, lambda s: (0, 0)),
                  pl.BlockSpec((FN_SUB, r, LANES), lambda s: (s % (r // FN_SUB), 0, 0)),
                  pl.BlockSpec((FN_SUB, r, LANES), lambda s: (s % (r // FN_SUB), 0, 0))],
        out_specs=pl.BlockSpec((blk, 2 * B_WIDTH), lambda s: (s, 0)),
        out_shape=jax.ShapeDtypeStruct((NL, 2 * B_WIDTH), BF16),
        compiler_params=_cparams(("arbitrary",)),
        name="fnet_lat1",
    )(u, ca, m1, twc, tws)
    z = z.reshape(DEC_BATCH, r, r, 2 * B_WIDTH).transpose(0, 2, 1, 3).reshape(NL, 2 * B_WIDTH)
    y = pl.pallas_call(
        _fnet_lat2_kernel,
        grid=(steps,),
        in_specs=[pl.BlockSpec((blk, 2 * B_WIDTH), lambda s: (s, 0)),
                  pl.BlockSpec(m1.shape, lambda s: (0, 0))],
        out_specs=pl.BlockSpec((blk, B_WIDTH), lambda s: (s, 0)),
        out_shape=jax.ShapeDtypeStruct((NL, B_WIDTH), F32),
        compiler_params=_cparams(("arbitrary",)),
        name="fnet_lat2",
    )(z, m1)
    return y.reshape(DEC_BATCH, r, r, B_WIDTH).transpose(0, 2, 1, 3).reshape(NL, B_WIDTH)


def _head_masks():
    lane = lax.broadcasted_iota(jnp.int32, (1, LANES), 1)
    return lane < NA_DH


def _attend_many(chains):
    scale = NA_DH ** -0.5
    s = [[_mm_nt(q, k) * scale if b is None else _mm_nt(q, k) * scale + b for k, v, b in kv] for q, kv in chains]
    m = [functools.reduce(jnp.maximum, [jnp.max(x, axis=-1, keepdims=True) for x in xs]) for xs in s]
    p = [[jnp.exp(x - mi) for x in xs] for xs, mi in zip(s, m)]
    l = [sum(jnp.sum(x, axis=-1, keepdims=True) for x in xs) for xs in p]
    o = [sum(_mm(x, v) for x, (k, v, b) in zip(xs, kv)) for xs, (q, kv) in zip(p, chains)]
    return [oi / li for oi, li in zip(o, l)]


def _na_ctx_kernel(q_ref, k_ref, v_ref, o_ref):
    first = _head_masks()
    n_split = 2
    rows = SEQ // n_split
    chains = []
    for p in range(NA_CTX_PAIRS):
        lanes = slice(p * LANES, (p + 1) * LANES)
        k = k_ref[:, lanes].astype(BF16)
        v = v_ref[:, lanes].astype(BF16)
        for j in range(n_split):
            q = q_ref[j * rows:(j + 1) * rows, lanes]
            for a in range(2):
                chains.append((jnp.where(first if a == 0 else jnp.logical_not(first), q, 0.0), [(k, v, None)]))
    outs = _attend_many(chains)
    for p in range(NA_CTX_PAIRS):
        for j in range(n_split):
            i = 2 * (p * n_split + j)
            o_ref[j * rows:(j + 1) * rows, p * LANES:(p + 1) * LANES] = jnp.where(first, outs[i], outs[i + 1])


def _na_ctx(qkv):
    w = NA_CTX_PAIRS * LANES
    nblk = D // w
    return pl.pallas_call(
        _na_ctx_kernel,
        grid=(BATCH, nblk),
        in_specs=[pl.BlockSpec((SEQ, w), lambda b, hp: (b, hp)),
                  pl.BlockSpec((SEQ, w), lambda b, hp: (b, nblk + hp)),
                  pl.BlockSpec((SEQ, w), lambda b, hp: (b, 2 * nblk + hp))],
        out_specs=pl.BlockSpec((SEQ, w), lambda b, hp: (b, hp)),
        out_shape=jax.ShapeDtypeStruct((NC, D), F32),
        compiler_params=_cparams(("arbitrary", "arbitrary")),
        name="na_ctx",
    )(qkv, qkv, qkv)


def _na_lat_kernel(q_ref, k_ref, v_ref, kc_ref, vc_ref, tt_ref, o_ref):
    first = _head_masks()
    rows = DEC_SEQ // GRID_W
    kctx = kc_ref[...].astype(BF16)
    vctx = vc_ref[...].astype(BF16)
    nkeys = WIN_R * GRID_W

    def rows_body(j, carry):
        chains = []
        for t in range(NA_ROW_UNROLL):
            r = NA_ROW_UNROLL * j + t
            r0 = jnp.clip(r - WIN_R // 2, 0, rows - WIN_R)
            dr0 = r0 - r + (WIN_R - 1)
            q = q_ref[pl.ds(pl.multiple_of(r * GRID_W, GRID_W), GRID_W), :]
            kbase = pl.multiple_of(r0 * GRID_W, GRID_W)
            kl = k_ref[pl.ds(kbase, nkeys), :].astype(BF16)
            vl = v_ref[pl.ds(kbase, nkeys), :].astype(BF16)
            for a in range(2):
                qm = jnp.where(first if a == 0 else jnp.logical_not(first), q, 0.0)
                bias = jnp.concatenate([tt_ref[a, dr0 + 2 * i] for i in range(WIN_R // 2)], axis=1)
                chains.append((qm, [(kl, vl, bias), (kctx, vctx, None)]))
        outs = _attend_many(chains)
        for t in range(NA_ROW_UNROLL):
            r = NA_ROW_UNROLL * j + t
            o_ref[pl.ds(pl.multiple_of(r * GRID_W, GRID_W), GRID_W), :] = jnp.where(first, outs[2 * t], outs[2 * t + 1])
        return carry

    lax.fori_loop(0, rows // NA_ROW_UNROLL, rows_body, 0)


def _na_latent(qkv, cache_k2, cache_v2, tt2):
    rb0 = NC // DEC_SEQ
    return pl.pallas_call(
        _na_lat_kernel,
        grid=(DEC_BATCH, NA_HEADS // 2),
        in_specs=[pl.BlockSpec((DEC_SEQ, LANES), lambda b, hp: (rb0 + b, hp)),
                  pl.BlockSpec((DEC_SEQ, LANES), lambda b, hp: (rb0 + b, 8 + hp)),
                  pl.BlockSpec((DEC_SEQ, LANES), lambda b, hp: (rb0 + b, 16 + hp)),
                  pl.BlockSpec((None, 256, LANES), lambda b, hp: (b, 0, hp)),
                  pl.BlockSpec((None, 256, LANES), lambda b, hp: (b, 0, hp)),
                  pl.BlockSpec((2, 2 * WIN_R - 2, GRID_W, LANES), lambda b, hp: (hp, 0, 0, 0))],
        out_specs=pl.BlockSpec((DEC_SEQ, LANES), lambda b, hp: (b, hp)),
        out_shape=jax.ShapeDtypeStruct((NL, D), F32),
        compiler_params=_cparams(("arbitrary", "arbitrary")),
        name="na_latent",
    )(qkv, qkv, qkv, cache_k2, cache_v2, tt2)


def _rpb_tables(rpb):
    col = np.arange(GRID_W)
    start = np.clip(col - WIN_C // 2, 0, GRID_W - WIN_C)
    inside = (col[None, :] >= start[:, None]) & (col[None, :] < start[:, None] + WIN_C)
    dc = col[None, :] - col[:, None] + (WIN_C - 1)
    t = jnp.full((NA_HEADS, 2 * WIN_R - 1, GRID_W, GRID_W), NEG, F32)
    for j in range(2 * WIN_C - 1):
        t = jnp.where((inside & (dc == j))[None, None], rpb[:, :, j, None, None], t)
    return jnp.concatenate([t[:, :-1], t[:, 1:]], axis=-1)


def _router_kernel(x_ref, m_ref, g_ref, wr_ref, br_ref, tri_ref, xf_ref, ri_ref, rw_ref, cnt_ref, base_scr):
    i = pl.program_id(0)

    @pl.when(i == 0)
    def _():
        base_scr[...] = jnp.zeros_like(base_scr)

    h = _modulated_norm(x_ref[...], m_ref, g_ref, 3, 4)
    xf_ref[...] = _pack_bf16_pairs(h[:, :D // 2], h[:, D // 2:])
    logits = _mm_hi(h, wr_ref[...]) + br_ref[...]
    lane = lax.broadcasted_iota(jnp.int32, logits.shape, 1)
    rmax = lambda x: jnp.max(x, axis=-1, keepdims=True)
    rmin = lambda x: jnp.min(x, axis=-1, keepdims=True)
    rsum = lambda x: jnp.sum(x, axis=-1, keepdims=True)

    gmask = lane < N_GROUPS
    mg = rmax(jnp.where(gmask, logits, NEG))
    eg = jnp.where(gmask, jnp.exp(jnp.where(gmask, logits - mg, NEG)), 0.0)
    pg = eg / rsum(eg)
    p_grp = rmax(pg)
    grp = rmin(jnp.where(jnp.logical_and(gmask, pg == p_grp), lane, LANES))
    lo = N_GROUPS + grp * EXP_PER_GROUP
    emask = jnp.logical_and(lane >= lo, lane < lo + EXP_PER_GROUP)
    me = rmax(jnp.where(emask, logits, NEG))
    ee = jnp.where(emask, jnp.exp(jnp.where(emask, logits - me, NEG)), 0.0)
    pe = ee / rsum(ee)
    p1 = rmax(pe)
    i1 = rmin(jnp.where(jnp.logical_and(emask, pe == p1), lane, LANES))
    m2 = jnp.logical_and(emask, lane != i1)
    p2 = rmax(jnp.where(m2, pe, -1.0))
    i2 = rmin(jnp.where(jnp.logical_and(m2, pe == p2), lane, LANES))
    den = p1 + p2
    w1 = p_grp * p1 / den
    w2 = p_grp * p2 / den
    e1 = i1 - N_GROUPS
    e2 = i2 - N_GROUPS

    oh1 = jnp.where(lane == e1, 1.0, 0.0)
    oh2 = jnp.where(lane == e2, 1.0, 0.0)
    c1 = _mm(tri_ref[...], oh1)
    c2 = _mm(tri_ref[...], oh2)
    tot1 = jnp.sum(oh1, axis=0, keepdims=True)
    tot2 = jnp.sum(oh2, axis=0, keepdims=True)
    base = base_scr[0:1, :]
    rank1 = rsum(jnp.where(lane == e1, base + c1, 0.0))
    rank2 = rsum(jnp.where(lane == e2, base + tot1 + c2, 0.0))
    new_base = base + tot1 + tot2
    base_scr[...] = jnp.broadcast_to(new_base, base_scr.shape)
    cnt_ref[...] = jnp.broadcast_to(new_base, cnt_ref.shape)
    ri = jnp.where(lane == 0, e1, jnp.where(lane == 1, e2,
         jnp.where(lane == 2, rank1.astype(jnp.int32), jnp.where(lane == 3, rank2.astype(jnp.int32), 0))))
    ri_ref[...] = ri
    rw_ref[...] = jnp.where(lane == 0, w1, jnp.where(lane == 1, w2, 0.0))


def _router(x, m3, gain, wr, br, tri):
    return pl.pallas_call(
        _router_kernel,
        grid=(N // TM,),
        in_specs=[pl.BlockSpec((TM, D), lambda i: (i, 0)),
                  pl.BlockSpec((None, 6, D), lambda i: (_cond_row(i), 0, 0)),
                  pl.BlockSpec((1, D), lambda i: (0, 0)),
                  pl.BlockSpec((D, LANES), lambda i: (0, 0)),
                  pl.BlockSpec((1, LANES), lambda i: (0, 0)),
                  pl.BlockSpec((TM, TM), lambda i: (0, 0))],
        out_specs=[pl.BlockSpec((TM, D // 2), lambda i: (i, 0)),
                   pl.BlockSpec((TM, LANES), lambda i: (i, 0)),
                   pl.BlockSpec((TM, LANES), lambda i: (i, 0)),
                   pl.BlockSpec((8, LANES), lambda i: (0, 0))],
        out_shape=[jax.ShapeDtypeStruct((N, D // 2), jnp.int32),
                   jax.ShapeDtypeStruct((N, LANES), jnp.int32),
                   jax.ShapeDtypeStruct((N, LANES), F32),
                   jax.ShapeDtypeStruct((8, LANES), F32)],
        scratch_shapes=[pltpu.VMEM((8, LANES), F32)],
        compiler_params=_cparams(("arbitrary",)),
        name="router",
    )(x, m3, gain, wr, br, tri)


def _row_copy(src_hbm, row, dst, r, sem):
    return pltpu.make_async_copy(src_hbm.at[pl.ds(row, 1)], dst.at[pl.ds(r, 1)], sem)


def _expert_kernel(be_ref, nu_ref, d_ref, xp_hbm, wg_ref, wu_ref, wd_ref, ys_ref, xres, xbuf, st_ref, sem):
    del be_ref
    b = pl.program_id(0)
    half = D // 2

    def gather(blk, slot):
        base = blk * MOE_ROWS
        for r in range(MOE_ROWS):
            xbuf[slot, pl.ds(r, 1), :] = xres[pl.ds(st_ref[base + r], 1), :]

    @pl.when(b == 0)
    def _():
        cp = pltpu.make_async_copy(xp_hbm, xres, sem.at[0])
        cp.start()

        def clear(j, c):
            for t in range(8):
                st_ref[8 * j + t] = 0
            return c

        lax.fori_loop(0, N_SLOT_BLOCKS * MOE_ROWS // 8, clear, 0)

        def invert(j, c):
            for t in range(8):
                st_ref[d_ref[8 * j + t]] = (8 * j + t) // 2
            return c

        lax.fori_loop(0, 2 * N // 8, invert, 0)
        cp.wait()
        gather(0, 0)

    @pl.when(b < nu_ref[0])
    def _():
        gather(jnp.minimum(b + 1, N_SLOT_BLOCKS - 1), (b + 1) % 2)
        x_lo, x_hi = _unpack_bf16_pairs(xbuf[b % 2])
        dot = functools.partial(jnp.dot, preferred_element_type=F32)
        g = dot(x_lo, wg_ref[:half, :].astype(BF16)) + dot(x_hi, wg_ref[half:, :].astype(BF16))
        u = dot(x_lo, wu_ref[:half, :].astype(BF16)) + dot(x_hi, wu_ref[half:, :].astype(BF16))
        ys_ref[...] = _mm(_silu(g) * u, wd_ref[...])

    @pl.when(b >= nu_ref[0])
    def _():
        ys_ref[...] = jnp.zeros_like(ys_ref)


def _experts(block_e, n_used, dest, xp, w_gate, w_up, w_down, layer):
    grid_spec = pltpu.PrefetchScalarGridSpec(
        num_scalar_prefetch=3,
        grid=(N_SLOT_BLOCKS,),
        in_specs=[pl.BlockSpec(memory_space=pl.ANY),
                  pl.BlockSpec((None, None, D, D_EXPERT), lambda b, be, nu, st: (layer, be[b], 0, 0)),
                  pl.BlockSpec((None, None, D, D_EXPERT), lambda b, be, nu, st: (layer, be[b], 0, 0)),
                  pl.BlockSpec((None, None, D_EXPERT, D), lambda b, be, nu, st: (layer, be[b], 0, 0))],
        out_specs=pl.BlockSpec((MOE_ROWS, D), lambda b, be, nu, st: (b, 0)),
        scratch_shapes=[pltpu.VMEM((N, D // 2), jnp.int32),
                        pltpu.VMEM((2, MOE_ROWS, D // 2), jnp.int32),
                        pltpu.SMEM((N_SLOT_BLOCKS * MOE_ROWS,), jnp.int32),
                        pltpu.SemaphoreType.DMA((1,))])
    return pl.pallas_call(
        _expert_kernel,
        grid_spec=grid_spec,
        out_shape=jax.ShapeDtypeStruct((N_SLOT_BLOCKS * MOE_ROWS, D), F32),
        compiler_params=pltpu.CompilerParams(dimension_semantics=("arbitrary",), vmem_limit_bytes=EXPERT_VMEM_LIMIT),
        name="experts",
    )(block_e, n_used, dest, xp, w_gate, w_up, w_down)


def _combine_kernel(d_ref, ys_hbm, x_ref, m_ref, rw_ref, fn_ref, o_ref, buf, sem, *, final, tile0, n_tiles):
    i = pl.program_id(0)

    def issue(tile, slot):
        base = (tile0 + tile) * (2 * TM)
        for r in range(TM):
            for kk in range(2):
                _row_copy(ys_hbm, d_ref[base + 2 * r + kk], buf.at[slot, kk], r, sem.at[slot]).start()

    @pl.when(i == 0)
    def _():
        issue(0, 0)

    @pl.when(i + 1 < n_tiles)
    def _():
        issue(i + 1, (i + 1) % 2)

    slot = i % 2
    for kk in range(2):
        pltpu.make_async_copy(ys_hbm.at[pl.ds(0, TM)], buf.at[slot, kk], sem.at[slot]).wait()
    w = rw_ref[...]
    y = w[:, 0:1] * buf[slot, 0] + w[:, 1:2] * buf[slot, 1]
    out = x_ref[...] + m_ref[5:6, :] * y
    if final:
        ms = jnp.mean(out * out, axis=-1, keepdims=True)
        out = out * lax.rsqrt(ms + EPS) * fn_ref[...]
    o_ref[...] = out


def _combine(dest_flat, ys, x, m3, rw, final_norm, final, tile0=0, n_tiles=N // TM):
    grid_spec = pltpu.PrefetchScalarGridSpec(
        num_scalar_prefetch=1,
        grid=(n_tiles,),
        in_specs=[pl.BlockSpec(memory_space=pl.ANY),
                  pl.BlockSpec((TM, D), lambda i, d: (tile0 + i, 0)),
                  pl.BlockSpec((None, 6, D), lambda i, d: (_cond_row(tile0 + i), 0, 0)),
                  pl.BlockSpec((TM, LANES), lambda i, d: (tile0 + i, 0)),
                  pl.BlockSpec((1, D), lambda i, d: (0, 0))],
        out_specs=pl.BlockSpec((TM, D), lambda i, d: (i, 0)),
        scratch_shapes=[pltpu.VMEM((2, 2, TM, D), F32), pltpu.SemaphoreType.DMA((2,))])
    return pl.pallas_call(
        functools.partial(_combine_kernel, final=final, tile0=tile0, n_tiles=n_tiles),
        grid_spec=grid_spec,
        out_shape=jax.ShapeDtypeStruct((n_tiles * TM, D), F32),
        compiler_params=_cparams(("arbitrary",)),
        name="combine",
    )(dest_flat, ys, x, m3, rw, final_norm)


def _hier_moe(x, m3, gain, layer, w_rg, b_rg, w_re, b_re, w_gate, w_up, w_down, tri_tm, final_norm, final):
    wr = jnp.concatenate([w_rg, w_re.transpose(1, 0, 2).reshape(D, N_EXPERTS),
                          jnp.zeros((D, LANES - N_GROUPS - N_EXPERTS), F32)], axis=1)
    br = jnp.concatenate([b_rg, b_re.reshape(N_EXPERTS),
                          jnp.zeros((LANES - N_GROUPS - N_EXPERTS,), F32)])[None, :]
    xf, ri, rw, cnt = _router(x, m3, gain, wr, br, tri_tm)
    e_idx = ri[:, 0:2]
    rank = ri[:, 2:4]
    counts = cnt[0, :N_EXPERTS].astype(jnp.int32)
    padded = (counts + MOE_ROWS - 1) // MOE_ROWS * MOE_ROWS
    end_pad = jnp.cumsum(padded)
    start_pad = end_pad - padded
    experts = jnp.arange(N_EXPERTS, dtype=jnp.int32)
    start_of = jnp.sum(jnp.where(e_idx[:, :, None] == experts, start_pad, 0), axis=-1)
    dest = (start_of + rank).reshape(-1).astype(jnp.int32)
    block_start = jnp.arange(N_SLOT_BLOCKS, dtype=jnp.int32) * MOE_ROWS
    block_e = jnp.minimum(jnp.sum((end_pad[None, :] <= block_start[:, None]).astype(jnp.int32), axis=1),
                          N_EXPERTS - 1)
    n_used = (end_pad[-1:] // MOE_ROWS).astype(jnp.int32)
    ys = _experts(block_e, n_used, dest, xf, w_gate, w_up, w_down, layer)
    if not final:
        return _combine(dest, ys, x, m3, rw, final_norm, False)
    nct = NC // TM
    return (_combine(dest, ys, x, m3, rw, final_norm, True, 0, nct),
            _combine(dest, ys, x, m3, rw, final_norm, True, nct, N // TM - nct))


def kernel(x_prompt, x_sample, state_A_fwd, state_A_bwd, cache_k, cache_v, c, c_ctx, mod_w, mod_b, norm_mix, norm_ffn, ab_w_in, ab_conv, ab_a_log, ab_dt_bias, ab_o_gain, ab_w_out, na_w_qkv, na_rpb, na_w_out, moe_w_rg, moe_b_rg, moe_w_re, moe_b_re, moe_w_gate, moe_w_up, moe_w_down, final_norm):
    x = (x_prompt.reshape(NC, D), x_sample.reshape(NL, D))
    cond8 = jnp.concatenate([c_ctx[None, :], c, jnp.zeros((8 - 1 - DEC_BATCH, D), F32)], axis=0)
    mods = _ada_params(cond8, mod_w, mod_b).reshape(DEPTH, 8, 6, D)

    tri_tm = jnp.asarray(np.tril(np.ones((TM, TM)), -1), BF16)
    tri_c = jnp.asarray(np.tril(np.ones((CHUNK, CHUNK))), F32)
    half = np.arange(LANES) < A_DH
    jbd = jnp.asarray((half[:, None] == half[None, :]).astype(np.float32))
    fn = final_norm[None, :]

    m3 = mods[0]
    w_in = ab_w_in[0]
    w_in = jnp.concatenate([w_in[:, :2048], w_in[:, 2080:2592], w_in[:, 2048:2080],
                            jnp.zeros((D, AB_COLS - 2592), F32)], axis=1).astype(BF16)
    proj = _modproj(x, m3, norm_mix[0][None, :], w_in, 896)
    gate_p = jnp.zeros((8, LANES), F32)
    gate_p = gate_p.at[0, :2 * A_HEADS].set(ab_a_log[0].reshape(-1))
    gate_p = gate_p.at[1, :2 * A_HEADS].set(ab_dt_bias[0].reshape(-1))
    o_gain2 = jnp.tile(ab_o_gain[0], 2)[None, :]
    zeros_state = jnp.zeros((BATCH, A_HEADS, A_DH, A_DH), F32)
    mix_a_c, s_f, s_b = _deltanet(proj, ab_conv[0], gate_p, o_gain2, zeros_state, zeros_state, tri_c, jbd,
                                  seq_len=SEQ, n_seq=BATCH, n_sub=4, row_blk0=0)
    mix_a_l, _, _ = _deltanet(proj, ab_conv[0], gate_p, o_gain2, state_A_fwd[:, 0], state_A_bwd[:, 0], tri_c, jbd,
                              seq_len=DEC_SEQ, n_seq=DEC_BATCH, n_sub=1, row_blk0=NC // DEC_SEQ)

    cc, sc = _dft_mats(LANES)
    ct, st = _dft_mats(SEQ)
    c64, s64 = _dft_mats(64)
    cs = jnp.asarray(np.concatenate([cc, sc], axis=1), BF16)
    dft = jnp.asarray(np.concatenate([ct, -st], axis=1), BF16)
    ca = jnp.asarray(np.concatenate([cc, -sc, -sc, -cc], axis=1), BF16)
    m1 = jnp.asarray(np.concatenate([c64, s64], axis=1), BF16)
    tw_idx = np.arange(64)
    tw_ang = 2.0 * np.pi * (tw_idx[:, None] * tw_idx[None, :]) / DEC_SEQ
    twc = jnp.broadcast_to(jnp.asarray(np.cos(tw_ang), F32)[:, :, None], (64, 64, LANES))
    tws = jnp.broadcast_to(jnp.asarray(np.sin(tw_ang), F32)[:, :, None], (64, 64, LANES))
    mix_b_c = _fnet_ctx(proj, cs, dft)
    mix_b_l = _fnet_latent(proj, ca, m1, twc, tws)

    w_out = ab_w_out[0].astype(BF16)
    x = _outproj([(mix_a_c, mix_a_l), (mix_b_c, mix_b_l)], [w_out[:A_WIDTH], w_out[A_WIDTH:]], x, m3)
    x = _hier_moe(x, m3, norm_ffn[0][None, :], 0, moe_w_rg[0], moe_b_rg[0], moe_w_re[0], moe_b_re[0],
                  moe_w_gate, moe_w_up, moe_w_down, tri_tm, fn, False)

    m3 = mods[1]
    qkv = _modproj(x, m3, norm_mix[1][None, :], na_w_qkv[0].astype(BF16), 512)
    attn_c = _na_ctx(qkv)
    attn_l = _na_latent(qkv, cache_k[:, 0].reshape(DEC_BATCH, 256, D), cache_v[:, 0].reshape(DEC_BATCH, 256, D),
                        _rpb_tables(na_rpb[0]))
    x = _outproj([(attn_c, attn_l)], [na_w_out[0].astype(BF16)], x, m3)
    y_c, y_l = _hier_moe(x, m3, norm_ffn[1][None, :], 1, moe_w_rg[1], moe_b_rg[1], moe_w_re[1], moe_b_re[1],
                         moe_w_gate, moe_w_up, moe_w_down, tri_tm, fn, True)

    new_k = qkv[:NC, D:2 * D].reshape(BATCH, 1, SEQ, NA_HEADS, NA_DH)
    new_v = qkv[:NC, 2 * D:].reshape(BATCH, 1, SEQ, NA_HEADS, NA_DH)
    return (y_c.reshape(BATCH, SEQ, D), y_l.reshape(DEC_BATCH, DEC_SEQ, D),
            s_f[:, None], s_b[:, None], new_k, new_v)
```

```python
import functools
import math

import numpy as np
import jax
import jax.numpy as jnp
from jax import lax
from jax.experimental import pallas as pl
from jax.experimental.pallas import tpu as pltpu

F32 = jnp.float32
BF16 = jnp.bfloat16
HIGHEST = lax.Precision.HIGHEST

D = 1024
BATCH, SEQ = 32, 256
DEC_BATCH, DEC_SEQ = 2, 4096
NC = BATCH * SEQ
NL = DEC_BATCH * DEC_SEQ
N = NC + NL
DEPTH = 2
GRID_W = 64
A_DH = 64
A_HEADS = 8
A_WIDTH = 512
CHUNK = 64
B_WIDTH = 512
B_GROUPS = 4
NA_DH = 64
NA_HEADS = 16
WIN_R, WIN_C = 8, 16
N_GROUPS, EXP_PER_GROUP, N_EXPERTS = 4, 8, 32
D_EXPERT = 512
EPS = 1e-6

LANES = 128
TM = 256
MOE_ROWS = 256
N_SLOT_BLOCKS = (2 * N) // MOE_ROWS + N_EXPERTS
AB_COLS = 2688
VMEM_LIMIT = 56 * 1024 * 1024
EXPERT_VMEM_LIMIT = 60 * 1024 * 1024
NEG = -1e30


def _cparams(sem):
    return pltpu.CompilerParams(dimension_semantics=sem, vmem_limit_bytes=VMEM_LIMIT)


def _mm(a, b):
    return jnp.dot(a.astype(BF16), b.astype(BF16), preferred_element_type=F32)


def _mm_nt(a, b):
    return lax.dot_general(a.astype(BF16), b.astype(BF16), (((1,), (1,)), ((), ())),
                           preferred_element_type=F32)


SOLVE_BLK = 16
NA_CTX_PAIRS = 2
NA_ROW_UNROLL = 4
TERM_UNROLL = 4


def _unit_lower_solve(a_mat, rhs, same_blk, eye):
    dg = jnp.where(same_blk, a_mat, 0.0)
    p = -dg
    dinv = eye + p
    for _ in range(int(math.log2(SOLVE_BLK)) - 1):
        p = _mm(p, p)
        dinv = dinv + _mm(p, dinv)
    mp = -_mm(dinv, a_mat - dg)
    y = _mm(dinv, rhs)
    y = y + _mm(mp, y)
    for _ in range(int(math.log2(a_mat.shape[0] // SOLVE_BLK)) - 1):
        mp = _mm(mp, mp)
        y = y + _mm(mp, y)
    return y


def _unit_lower_solve_many(a_mats, rhs, same_blk, eye):
    dg = [jnp.where(same_blk, a, 0.0) for a in a_mats]
    off = [a - g for a, g in zip(a_mats, dg)]
    p = [-g for g in dg]
    dinv = [eye + x for x in p]
    for _ in range(int(math.log2(SOLVE_BLK)) - 1):
        p = [_mm(x, x) for x in p]
        dinv = [di + _mm(x, di) for x, di in zip(p, dinv)]
    mp = [-_mm(di, o) for di, o in zip(dinv, off)]
    y = [_mm(di, r) for di, r in zip(dinv, rhs)]
    y = [yi + _mm(m, yi) for m, yi in zip(mp, y)]
    for _ in range(int(math.log2(a_mats[0].shape[0] // SOLVE_BLK)) - 1):
        mp = [_mm(m, m) for m in mp]
        y = [yi + _mm(m, yi) for m, yi in zip(mp, y)]
    return y


def _mm_split(a, b, parts, split_rhs=False):
    x = b if split_rhs else a
    acc = None
    for _ in range(parts):
        piece = x.astype(BF16)
        x = x - piece.astype(F32)
        term = (jnp.dot(a.astype(BF16), piece, preferred_element_type=F32) if split_rhs
                else jnp.dot(piece, b.astype(BF16), preferred_element_type=F32))
        acc = term if acc is None else acc + term
    return acc


def _mm_hi(a, b):
    return jnp.dot(a, b, preferred_element_type=F32, precision=HIGHEST)


def _silu(x):
    return x * jax.nn.sigmoid(x)


def _bf16_bits(x):
    b = lax.bitcast_convert_type(x, jnp.int32)
    return b + 0x7FFF + (lax.shift_right_logical(b, jnp.int32(16)) & 1)


_HIGH16 = -65536


def _pack_bf16_pairs(a, b):
    return lax.shift_right_logical(_bf16_bits(a), jnp.int32(16)) | (_bf16_bits(b) & _HIGH16)


def _unpack_bf16_pairs(p):
    a = lax.bitcast_convert_type(lax.shift_left(p, jnp.int32(16)), F32)
    b = lax.bitcast_convert_type(p & _HIGH16, F32)
    return a.astype(BF16), b.astype(BF16)


def _cond_row(i):
    return jnp.where(i < NC // TM, 0, 1 + (i - NC // TM) // (DEC_SEQ // TM))


def _modulated_norm(x, m_ref, g_ref, shift_idx, scale_idx):
    ms = jnp.mean(x * x, axis=-1, keepdims=True)
    y = x * lax.rsqrt(ms + EPS) * g_ref[...]
    return y * (1.0 + m_ref[scale_idx:scale_idx + 1, :]) + m_ref[shift_idx:shift_idx + 1, :]


def _ada_kernel(cond_ref, w_ref, b_ref, o_ref):
    o_ref[...] = _mm_hi(_silu(cond_ref[...]), w_ref[...]) + b_ref[...]


def _ada_params(cond8, mod_w, mod_b):
    tn = 1536
    return pl.pallas_call(
        _ada_kernel,
        grid=(DEPTH, 6 * D // tn),
        in_specs=[pl.BlockSpec((8, D), lambda l, j: (0, 0)),
                  pl.BlockSpec((None, D, tn), lambda l, j: (l, 0, j)),
                  pl.BlockSpec((None, 1, tn), lambda l, j: (l, 0, j))],
        out_specs=pl.BlockSpec((None, 8, tn), lambda l, j: (l, 0, j)),
        out_shape=jax.ShapeDtypeStruct((DEPTH, 8, 6 * D), F32),
        compiler_params=_cparams(("arbitrary", "arbitrary")),
        name="ada_params",
    )(cond8, mod_w, mod_b.reshape(DEPTH, 1, 6 * D))


def _token_specs(x):
    nct = NC // TM
    if isinstance(x, tuple):
        return ([pl.BlockSpec((TM, x[0].shape[1]), lambda i: (jnp.minimum(i, nct - 1), 0)),
                 pl.BlockSpec((TM, x[1].shape[1]), lambda i: (jnp.maximum(i - nct, 0), 0))], list(x))
    return [pl.BlockSpec((TM, x.shape[1]), lambda i: (i, 0))], [x]


def _token_rows(refs):
    if len(refs) == 1:
        return refs[0][...]
    return jnp.where(pl.program_id(0) < NC // TM, refs[0][...], refs[1][...])


def _modproj_kernel(*refs, n_chunk):
    m_ref, g_ref, w_ref, o_ref = refs[-4:]
    hb = _modulated_norm(_token_rows(refs[:-4]), m_ref, g_ref, 0, 1).astype(BF16)
    for j in range(o_ref.shape[1] // n_chunk):
        sl = slice(j * n_chunk, (j + 1) * n_chunk)
        o_ref[:, sl] = jnp.dot(hb, w_ref[:, sl], preferred_element_type=F32)


def _modproj(x, m3, gain, w_bf16, n_chunk):
    nout = w_bf16.shape[1]
    x_specs, x_args = _token_specs(x)
    return pl.pallas_call(
        functools.partial(_modproj_kernel, n_chunk=n_chunk),
        grid=(N // TM,),
        in_specs=x_specs + [pl.BlockSpec((None, 6, D), lambda i: (_cond_row(i), 0, 0)),
                            pl.BlockSpec((1, D), lambda i: (0, 0)),
                            pl.BlockSpec((D, nout), lambda i: (0, 0))],
        out_specs=pl.BlockSpec((TM, nout), lambda i: (i, 0)),
        out_shape=jax.ShapeDtypeStruct((N, nout), F32),
        compiler_params=_cparams(("arbitrary",)),
        name="modproj",
    )(*x_args, m3, gain, w_bf16)


def _outproj_kernel(*refs, n_in):
    a_refs = refs[:2 * n_in]
    w_refs = refs[2 * n_in:3 * n_in]
    x_refs = refs[3 * n_in:-2]
    m_ref, o_ref = refs[-2:]
    acc = None
    for j, w_ref in enumerate(w_refs):
        part = _mm(_token_rows(a_refs[2 * j:2 * j + 2]), w_ref[...])
        acc = part if acc is None else acc + part
    o_ref[...] = _token_rows(x_refs) + m_ref[2:3, :] * acc


def _outproj(a_pairs, w_list, x, m3):
    n_in = len(a_pairs)
    in_specs, args = [], []
    for pair in a_pairs:
        specs, ops = _token_specs(pair)
        in_specs += specs
        args += ops
    x_specs, x_args = _token_specs(x)
    in_specs += ([pl.BlockSpec(w.shape, lambda i: (0, 0)) for w in w_list] + x_specs
                 + [pl.BlockSpec((None, 6, D), lambda i: (_cond_row(i), 0, 0))])
    return pl.pallas_call(
        functools.partial(_outproj_kernel, n_in=n_in),
        grid=(N // TM,),
        in_specs=in_specs,
        out_specs=pl.BlockSpec((TM, D), lambda i: (i, 0)),
        out_shape=jax.ShapeDtypeStruct((N, D), F32),
        compiler_params=_cparams(("arbitrary",)),
        name="outproj",
    )(*args, *w_list, *x_args, m3)


def _deltanet_kernel(q_ref, k_ref, v_ref, z_ref, ab_ref, cq_ref, ck_ref, cv_ref, gp_ref, og_ref,
                     s0f_ref, s0b_ref, tri_ref, jbd_ref, o_ref, sf_ref, sb_ref,
                     u_s, w_s, qd_s, at_s, kt_s, ge_s, st_s, ob, *, seq_len, n_sub):
    hp = pl.program_id(1)
    C = CHUNK
    nc = seq_len // C
    nct = n_sub * nc
    P = LANES
    lane = lax.broadcasted_iota(jnp.int32, (C, P), 1)
    row = lax.broadcasted_iota(jnp.int32, (C, P), 0)
    first_head = lane < A_DH
    ri = lax.broadcasted_iota(jnp.int32, (P, P), 0)
    ci = lax.broadcasted_iota(jnp.int32, (P, P), 1)
    same_head = (ri < C) == (ci < C)
    same_blk = (ri // SOLVE_BLK) == (ci // SOLVE_BLK)
    eye = jnp.where(ri == ci, 1.0, 0.0)
    jbd = jbd_ref[...]
    lincl = tri_ref[...]
    tri_b = [lincl.astype(BF16), lincl.T.astype(BF16)]
    incl_m = [jnp.logical_and(same_head, ri >= ci), jnp.logical_and(same_head, ri <= ci)]
    strict_m = [jnp.logical_and(same_head, ri > ci), jnp.logical_and(same_head, ri < ci)]
    neg_a = -jnp.exp(gp_ref[0:1, :])
    dt_b = gp_ref[1:2, :]

    def conv_silu(ref, w_ref, c):
        base = pl.multiple_of(c * C, C)
        cs = c % nc
        xc = ref[pl.ds(base, C), :]
        pbase = pl.multiple_of(jnp.maximum(base - 8, 0), 8)
        nbase = pl.multiple_of(jnp.minimum(base + C, n_sub * seq_len - 8), 8)
        prev_row = ref[pl.ds(pbase, 8), :][7:8, :] * jnp.where(cs > 0, 1.0, 0.0)
        next_row = ref[pl.ds(nbase, 8), :][0:1, :] * jnp.where(cs < nc - 1, 1.0, 0.0)
        x_prev = jnp.where(row == 0, prev_row, pltpu.roll(xc, 1, 0))
        x_next = jnp.where(row == C - 1, next_row, pltpu.roll(xc, C - 1, 0))
        y = w_ref[0:1, :] * x_prev + w_ref[1:2, :] * xc + w_ref[2:3, :] * x_next
        return _silu(y)

    def stack(x):
        return jnp.concatenate([jnp.where(first_head, x, 0.0), jnp.where(first_head, 0.0, x)], axis=0)

    def lane_col(arr, idx):
        return jnp.sum(jnp.where(lane == idx, arr, 0.0), axis=-1, keepdims=True)

    def chunk_inputs(c):
        base = pl.multiple_of(c * C, C)
        q = conv_silu(q_ref, cq_ref, c)
        k = conv_silu(k_ref, ck_ref, c)
        v = conv_silu(v_ref, cv_ref, c)
        ab = ab_ref[pl.ds(base, C), :]
        g_all = neg_a * jax.nn.softplus(ab + dt_b)
        beta_all = jax.nn.sigmoid(ab)
        return q, k, v, g_all, beta_all

    def chain_gates(g_all, beta_all, gc_all, d):
        cols = []
        for a in range(2):
            g_lane = d * A_HEADS + 2 * hp + a
            g_col = lane_col(g_all, g_lane)
            tot = jnp.broadcast_to(jnp.sum(g_col, axis=0, keepdims=True), (C, 1))
            cols.append((lane_col(gc_all, g_lane), lane_col(beta_all, 2 * A_HEADS + g_lane), tot))
        return [jnp.concatenate([cols[0][j], cols[1][j]], axis=0) for j in range(3)]

    def terms_body(j, carry):
        cs = [TERM_UNROLL * j + t for t in range(TERM_UNROLL)]
        ins = [chunk_inputs(c) for c in cs]
        qsq = [_mm_split(x[0] * x[0], jbd, 2) for x in ins]
        ksq = [_mm_split(x[1] * x[1], jbd, 2) for x in ins]
        qs = [x[0] * lax.rsqrt(s + EPS) * (A_DH ** -0.5) for x, s in zip(ins, qsq)]
        ks = [x[1] * lax.rsqrt(s + EPS) for x, s in zip(ins, ksq)]
        qst = [stack(x) for x in qs]
        kst = [stack(x) for x in ks]
        vst = [stack(x[2]) for x in ins]
        kk = [_mm_nt(x, x) for x in kst]
        qk = [_mm_nt(x, y) for x, y in zip(qst, kst)]
        chains = [(t, d) for t in range(TERM_UNROLL) for d in range(2)]
        gc_all = [_mm_split(tri_b[d], ins[t][3], 3, split_rhs=True) for t, d in chains]
        gates = [chain_gates(ins[t][3], ins[t][4], gc, d) for (t, d), gc in zip(chains, gc_all)]
        decay, e_gc = [], []
        for (t, d), (gc_col, beta_col, tot_col) in zip(chains, gates):
            gcb = jnp.broadcast_to(gc_col, (P, P))
            decay.append(jnp.where(incl_m[d], jnp.exp(jnp.where(incl_m[d], gcb - gcb.T, 0.0)), 0.0))
            e_gc.append(jnp.exp(gc_col))
        a_mats = [jnp.where(strict_m[d], g[1] * kk[t] * dc, 0.0) for (t, d), g, dc in zip(chains, gates, decay)]
        rhs = [vst[t] * g[1] + pltpu.roll(kst[t] * (g[1] * e), A_DH, 1)
               for (t, d), g, e in zip(chains, gates, e_gc)]
        xs = _unit_lower_solve_many(a_mats, rhs, same_blk, eye)
        for (t, d), x, g, e, dc in zip(chains, xs, gates, e_gc, decay):
            c = cs[t]
            u_s[d, c] = jnp.where(same_head, x, 0.0).astype(BF16)
            w_s[d, c] = pltpu.roll(jnp.where(same_head, 0.0, x), A_DH, 1).astype(BF16)
            qd_s[d, c] = (qst[t] * e).astype(BF16)
            at_s[d, c] = jnp.where(incl_m[d], qk[t] * dc, 0.0).astype(BF16)
            kt_s[d, c] = (kst[t] * jnp.exp(g[2] - g[0])).T.astype(BF16)
            e_tot = jnp.exp(g[2])
            ge_s[d, c] = jnp.where(lax.broadcasted_iota(jnp.int32, (8, P), 1) < A_DH,
                                   e_tot[0:1, :], e_tot[C:C + 1, :])
        return carry

    lax.fori_loop(0, nct // TERM_UNROLL, terms_body, 0)

    def block_diag(s2):
        z = jnp.zeros((A_DH, A_DH), F32)
        return jnp.concatenate([jnp.concatenate([s2[0], z], axis=1),
                                jnp.concatenate([z, s2[1]], axis=1)], axis=0)

    for s in range(n_sub):
        st_s[2 * s] = block_diag(s0f_ref[s])
        st_s[2 * s + 1] = block_diag(s0b_ref[s])

    def scan_body(i, carry):
        chains = [(s, d, s * nc + (i if d == 0 else nc - 1 - i)) for s in range(n_sub) for d in range(2)]
        dot = functools.partial(jnp.dot, preferred_element_type=F32)
        s_bd = [st_s[2 * s + d] for s, d, c in chains]
        sb16 = [x.astype(BF16) for x in s_bd]
        ws = [dot(w_s[d, c], sb) for (s, d, c), sb in zip(chains, sb16)]
        qs_ = [dot(qd_s[d, c], sb) for (s, d, c), sb in zip(chains, sb16)]
        vb = [(u_s[d, c].astype(F32) - x).astype(BF16) for (s, d, c), x in zip(chains, ws)]
        av = [dot(at_s[d, c], x) for (s, d, c), x in zip(chains, vb)]
        kv = [dot(kt_s[d, c], x) for (s, d, c), x in zip(chains, vb)]
        for (s, d, c), sb, q_, a_, k_ in zip(chains, s_bd, qs_, av, kv):
            st_s[2 * s + d] = sb * ge_s[d, c][0:1, :] + k_
            o_st = q_ + a_
            dst = o_ref if d == 0 else ob
            dst[pl.ds(pl.multiple_of(c * C, C), C), :] = o_st[:C] + o_st[C:]
        return carry

    lax.fori_loop(0, nc, scan_body, 0)

    for s in range(n_sub):
        for d, ref in ((0, sf_ref), (1, sb_ref)):
            s_bd = st_s[2 * s + d]
            ref[s, 0] = s_bd[:A_DH, :A_DH]
            ref[s, 1] = s_bd[A_DH:, A_DH:]

    def finish(j, carry):
        bases = [pl.multiple_of((TERM_UNROLL * j + t) * C, C) for t in range(TERM_UNROLL)]
        o = [o_ref[pl.ds(b, C), :] + ob[pl.ds(b, C), :] for b in bases]
        ms = [_mm_split(x * x, jbd, 2) * (1.0 / A_DH) for x in o]
        for b, x, m in zip(bases, o, ms):
            o_ref[pl.ds(b, C), :] = x * lax.rsqrt(m + EPS) * og_ref[...] * _silu(z_ref[pl.ds(b, C), :])
        return carry

    lax.fori_loop(0, nct // TERM_UNROLL, finish, 0)


def _deltanet(proj, conv_w, gate_p, o_gain2, s0f, s0b, tri, jbd, *, seq_len, n_seq, n_sub, row_blk0):
    rows = n_sub * seq_len
    nct = rows // CHUNK
    rb = lambda b: row_blk0 + b
    col = lambda off: (lambda b, hp: (rb(b), off + hp))
    st_spec = pl.BlockSpec((n_sub, 2, A_DH, A_DH), lambda b, hp: (b, hp, 0, 0))
    in_specs = [pl.BlockSpec((rows, LANES), col(0)),
                pl.BlockSpec((rows, LANES), col(4)),
                pl.BlockSpec((rows, LANES), col(8)),
                pl.BlockSpec((rows, LANES), col(12)),
                pl.BlockSpec((rows, LANES), lambda b, hp: (rb(b), 20)),
                pl.BlockSpec((3, LANES), lambda b, hp: (0, hp)),
                pl.BlockSpec((3, LANES), lambda b, hp: (0, 4 + hp)),
                pl.BlockSpec((3, LANES), lambda b, hp: (0, 8 + hp)),
                pl.BlockSpec((8, LANES), lambda b, hp: (0, 0)),
                pl.BlockSpec((1, LANES), lambda b, hp: (0, 0)),
                st_spec, st_spec,
                pl.BlockSpec((CHUNK, CHUNK), lambda b, hp: (0, 0)),
                pl.BlockSpec((LANES, LANES), lambda b, hp: (0, 0))]
    args = [proj, proj, proj, proj, proj, conv_w, conv_w, conv_w, gate_p, o_gain2, s0f, s0b, tri, jbd]
    st_shape = jax.ShapeDtypeStruct((n_seq, A_HEADS, A_DH, A_DH), F32)
    tile = lambda dt: pltpu.VMEM((2, nct, LANES, LANES), dt)
    return pl.pallas_call(
        functools.partial(_deltanet_kernel, seq_len=seq_len, n_sub=n_sub),
        grid=(n_seq // n_sub, A_HEADS // 2),
        in_specs=in_specs,
        out_specs=[pl.BlockSpec((rows, LANES), lambda b, hp: (b, hp)), st_spec, st_spec],
        out_shape=[jax.ShapeDtypeStruct((n_seq * seq_len, A_WIDTH), F32), st_shape, st_shape],
        scratch_shapes=[tile(BF16), tile(BF16), tile(BF16), tile(BF16), tile(BF16),
                        pltpu.VMEM((2, nct, 8, LANES), F32),
                        pltpu.VMEM((2 * n_sub, LANES, LANES), F32),
                        pltpu.VMEM((rows, LANES), F32)],
        compiler_params=_cparams(("arbitrary", "arbitrary")),
        name="deltanet",
    )(*args)


def _dft_mats(n):
    idx = np.arange(n)
    ang = 2.0 * np.pi * ((idx[:, None] * idx[None, :]) % n) / n
    return np.cos(ang), np.sin(ang)


def _fnet_ctx_kernel(u_ref, cs_ref, dft_ref, o_ref):
    norm = 1.0 / math.sqrt(SEQ * LANES)
    for g in range(B_GROUPS):
        sl = slice(g * LANES, (g + 1) * LANES)
        p = _mm(u_ref[:, sl], cs_ref[...])
        stack = jnp.concatenate([p[:, :LANES], p[:, LANES:]], axis=0)
        o_ref[:, sl] = _mm(dft_ref[...], stack) * norm


def _fnet_ctx(proj, cs, dft):
    return pl.pallas_call(
        _fnet_ctx_kernel,
        grid=(BATCH,),
        in_specs=[pl.BlockSpec((SEQ, B_WIDTH), lambda b: (b, 4)),
                  pl.BlockSpec(cs.shape, lambda b: (0, 0)),
                  pl.BlockSpec(dft.shape, lambda b: (0, 0))],
        out_specs=pl.BlockSpec((SEQ, B_WIDTH), lambda b: (b, 0)),
        out_shape=jax.ShapeDtypeStruct((NC, B_WIDTH), F32),
        compiler_params=_cparams(("arbitrary",)),
        name="fnet_ctx",
    )(proj, cs, dft)


FN_SUB = 4


def _fnet_lat1_kernel(u_ref, ca_ref, m1_ref, twc_ref, tws_ref, o_ref):
    r = 64
    for j in range(FN_SUB):
        c = twc_ref[j]
        s = tws_ref[j]
        rows = slice(j * r, (j + 1) * r)
        for g in range(B_GROUPS):
            pa = _mm(u_ref[rows, g * LANES:(g + 1) * LANES], ca_ref[...])
            rhs = jnp.concatenate([pa[:, :2 * LANES], pa[:, 2 * LANES:]], axis=0)
            zz = _mm(m1_ref[...], rhs)
            zr, zi = zz[:, :LANES], zz[:, LANES:]
            o_ref[rows, 2 * g * LANES:(2 * g + 1) * LANES] = (zr * c + zi * s).astype(BF16)
            o_ref[rows, (2 * g + 1) * LANES:(2 * g + 2) * LANES] = (zi * c - zr * s).astype(BF16)


def _fnet_lat2_kernel(z_ref, m1_ref, o_ref):
    r = 64
    norm = 1.0 / math.sqrt(DEC_SEQ * LANES)
    for j in range(FN_SUB):
        rows = slice(j * r, (j + 1) * r)
        for g in range(B_GROUPS):
            rhs = jnp.concatenate([z_ref[rows, 2 * g * LANES:(2 * g + 1) * LANES],
                                   z_ref[rows, (2 * g + 1) * LANES:(2 * g + 2) * LANES]], axis=0)
            o_ref[rows, g * LANES:(g + 1) * LANES] = _mm(m1_ref[...], rhs) * norm


def _fnet_latent(proj, ca, m1, twc, tws):
    r = 64
    u = proj[NC:, 2048:2560].astype(BF16).reshape(DEC_BATCH, r, r, B_WIDTH)
    u = u.transpose(0, 2, 1, 3).reshape(DEC_BATCH * r * r, B_WIDTH)
    steps = DEC_BATCH * r // FN_SUB
    blk = FN_SUB * r
    z = pl.pallas_call(
        _fnet_lat1_kernel,
        grid=(steps,),
        in_specs=[pl.BlockSpec((blk, B_WIDTH), lambda s: (s, 0)),
                  pl.BlockSpec(ca.shape, lambda s: (0, 0)),
                  pl.BlockSpec(m1.shape, lambda s: (0, 0)),
                  pl.BlockSpec((FN_SUB, r, LANES), lambda s: (s % (r // FN_SUB), 0, 0)),
                  pl.BlockSpec((FN_SUB, r, LANES), lambda s: (s % (r // FN_SUB), 0, 0))],
        out_specs=pl.BlockSpec((blk, 2 * B_WIDTH), lambda s: (s, 0)),
        out_shape=jax.ShapeDtypeStruct((NL, 2 * B_WIDTH), BF16),
        compiler_params=_cparams(("arbitrary",)),
        name="fnet_lat1",
    )(u, ca, m1, twc, tws)
    z = z.reshape(DEC_BATCH, r, r, 2 * B_WIDTH).transpose(0, 2, 1, 3).reshape(NL, 2 * B_WIDTH)
    y = pl.pallas_call(
        _fnet_lat2_kernel,
        grid=(steps,),
        in_specs=[pl.BlockSpec((blk, 2 * B_WIDTH), lambda s: (s, 0)),
                  pl.BlockSpec(m1.shape, lambda s: (0, 0))],
        out_specs=pl.BlockSpec((blk, B_WIDTH), lambda s: (s, 0)),
        out_shape=jax.ShapeDtypeStruct((NL, B_WIDTH), F32),
        compiler_params=_cparams(("arbitrary",)),
        name="fnet_lat2",
    )(z, m1)
    return y.reshape(DEC_BATCH, r, r, B_WIDTH).transpose(0, 2, 1, 3).reshape(NL, B_WIDTH)


def _head_masks():
    lane = lax.broadcasted_iota(jnp.int32, (1, LANES), 1)
    return lane < NA_DH


def _attend_many(chains):
    scale = NA_DH ** -0.5
    s = [[_mm_nt(q, k) * scale if b is None else _mm_nt(q, k) * scale + b for k, v, b in kv] for q, kv in chains]
    m = [functools.reduce(jnp.maximum, [jnp.max(x, axis=-1, keepdims=True) for x in xs]) for xs in s]
    p = [[jnp.exp(x - mi) for x in xs] for xs, mi in zip(s, m)]
    l = [sum(jnp.sum(x, axis=-1, keepdims=True) for x in xs) for xs in p]
    o = [sum(_mm(x, v) for x, (k, v, b) in zip(xs, kv)) for xs, (q, kv) in zip(p, chains)]
    return [oi / li for oi, li in zip(o, l)]


def _na_ctx_kernel(q_ref, k_ref, v_ref, o_ref):
    first = _head_masks()
    n_split = 2
    rows = SEQ // n_split
    chains = []
    for p in range(NA_CTX_PAIRS):
        lanes = slice(p * LANES, (p + 1) * LANES)
        k = k_ref[:, lanes].astype(BF16)
        v = v_ref[:, lanes].astype(BF16)
        for j in range(n_split):
            q = q_ref[j * rows:(j + 1) * rows, lanes]
            for a in range(2):
                chains.append((jnp.where(first if a == 0 else jnp.logical_not(first), q, 0.0), [(k, v, None)]))
    outs = _attend_many(chains)
    for p in range(NA_CTX_PAIRS):
        for j in range(n_split):
            i = 2 * (p * n_split + j)
            o_ref[j * rows:(j + 1) * rows, p * LANES:(p + 1) * LANES] = jnp.where(first, outs[i], outs[i + 1])


def _na_ctx(qkv):
    w = NA_CTX_PAIRS * LANES
    nblk = D // w
    return pl.pallas_call(
        _na_ctx_kernel,
        grid=(BATCH, nblk),
        in_specs=[pl.BlockSpec((SEQ, w), lambda b, hp: (b, hp)),
                  pl.BlockSpec((SEQ, w), lambda b, hp: (b, nblk + hp)),
                  pl.BlockSpec((SEQ, w), lambda b, hp: (b, 2 * nblk + hp))],
        out_specs=pl.BlockSpec((SEQ, w), lambda b, hp: (b, hp)),
        out_shape=jax.ShapeDtypeStruct((NC, D), F32),
        compiler_params=_cparams(("arbitrary", "arbitrary")),
        name="na_ctx",
    )(qkv, qkv, qkv)


def _na_lat_kernel(q_ref, k_ref, v_ref, kc_ref, vc_ref, tt_ref, o_ref):
    first = _head_masks()
    rows = DEC_SEQ // GRID_W
    kctx = kc_ref[...].astype(BF16)
    vctx = vc_ref[...].astype(BF16)
    nkeys = WIN_R * GRID_W

    def rows_body(j, carry):
        chains = []
        for t in range(NA_ROW_UNROLL):
            r = NA_ROW_UNROLL * j + t
            r0 = jnp.clip(r - WIN_R // 2, 0, rows - WIN_R)
            dr0 = r0 - r + (WIN_R - 1)
            q = q_ref[pl.ds(pl.multiple_of(r * GRID_W, GRID_W), GRID_W), :]
            kbase = pl.multiple_of(r0 * GRID_W, GRID_W)
            kl = k_ref[pl.ds(kbase, nkeys), :].astype(BF16)
            vl = v_ref[pl.ds(kbase, nkeys), :].astype(BF16)
            for a in range(2):
                qm = jnp.where(first if a == 0 else jnp.logical_not(first), q, 0.0)
                bias = jnp.concatenate([tt_ref[a, dr0 + 2 * i] for i in range(WIN_R // 2)], axis=1)
                chains.append((qm, [(kl, vl, bias), (kctx, vctx, None)]))
        outs = _attend_many(chains)
        for t in range(NA_ROW_UNROLL):
            r = NA_ROW_UNROLL * j + t
            o_ref[pl.ds(pl.multiple_of(r * GRID_W, GRID_W), GRID_W), :] = jnp.where(first, outs[2 * t], outs[2 * t + 1])
        return carry

    lax.fori_loop(0, rows // NA_ROW_UNROLL, rows_body, 0)


def _na_latent(qkv, cache_k2, cache_v2, tt2):
    rb0 = NC // DEC_SEQ
    return pl.pallas_call(
        _na_lat_kernel,
        grid=(DEC_BATCH, NA_HEADS // 2),
        in_specs=[pl.BlockSpec((DEC_SEQ, LANES), lambda b, hp: (rb0 + b, hp)),
                  pl.BlockSpec((DEC_SEQ, LANES), lambda b, hp: (rb0 + b, 8 + hp)),
                  pl.BlockSpec((DEC_SEQ, LANES), lambda b, hp: (rb0 + b, 16 + hp)),
                  pl.BlockSpec((None, 256, LANES), lambda b, hp: (b, 0, hp)),
                  pl.BlockSpec((None, 256, LANES), lambda b, hp: (b, 0, hp)),
                  pl.BlockSpec((2, 2 * WIN_R - 2, GRID_W, LANES), lambda b, hp: (hp, 0, 0, 0))],
        out_specs=pl.BlockSpec((DEC_SEQ, LANES), lambda b, hp: (b, hp)),
        out_shape=jax.ShapeDtypeStruct((NL, D), F32),
        compiler_params=_cparams(("arbitrary", "arbitrary")),
        name="na_latent",
    )(qkv, qkv, qkv, cache_k2, cache_v2, tt2)


def _rpb_tables(rpb):
    col = np.arange(GRID_W)
    start = np.clip(col - WIN_C // 2, 0, GRID_W - WIN_C)
    inside = (col[None, :] >= start[:, None]) & (col[None, :] < start[:, None] + WIN_C)
    dc = col[None, :] - col[:, None] + (WIN_C - 1)
    t = jnp.full((NA_HEADS, 2 * WIN_R - 1, GRID_W, GRID_W), NEG, F32)
    for j in range(2 * WIN_C - 1):
        t = jnp.where((inside & (dc == j))[None, None], rpb[:, :, j, None, None], t)
    return jnp.concatenate([t[:, :-1], t[:, 1:]], axis=-1)


def _router_kernel(x_ref, m_ref, g_ref, wr_ref, br_ref, tri_ref, xf_ref, ri_ref, rw_ref, cnt_ref, base_scr):
    i = pl.program_id(0)

    @pl.when(i == 0)
    def _():
        base_scr[...] = jnp.zeros_like(base_scr)

    h = _modulated_norm(x_ref[...], m_ref, g_ref, 3, 4)
    xf_ref[...] = _pack_bf16_pairs(h[:, :D // 2], h[:, D // 2:])
    logits = _mm_hi(h, wr_ref[...]) + br_ref[...]
    lane = lax.broadcasted_iota(jnp.int32, logits.shape, 1)
    rmax = lambda x: jnp.max(x, axis=-1, keepdims=True)
    rmin = lambda x: jnp.min(x, axis=-1, keepdims=True)
    rsum = lambda x: jnp.sum(x, axis=-1, keepdims=True)

    gmask = lane < N_GROUPS
    mg = rmax(jnp.where(gmask, logits, NEG))
    eg = jnp.where(gmask, jnp.exp(jnp.where(gmask, logits - mg, NEG)), 0.0)
    pg = eg / rsum(eg)
    p_grp = rmax(pg)
    grp = rmin(jnp.where(jnp.logical_and(gmask, pg == p_grp), lane, LANES))
    lo = N_GROUPS + grp * EXP_PER_GROUP
    emask = jnp.logical_and(lane >= lo, lane < lo + EXP_PER_GROUP)
    me = rmax(jnp.where(emask, logits, NEG))
    ee = jnp.where(emask, jnp.exp(jnp.where(emask, logits - me, NEG)), 0.0)
    pe = ee / rsum(ee)
    p1 = rmax(pe)
    i1 = rmin(jnp.where(jnp.logical_and(emask, pe == p1), lane, LANES))
    m2 = jnp.logical_and(emask, lane != i1)
    p2 = rmax(jnp.where(m2, pe, -1.0))
    i2 = rmin(jnp.where(jnp.logical_and(m2, pe == p2), lane, LANES))
    den = p1 + p2
    w1 = p_grp * p1 / den
    w2 = p_grp * p2 / den
    e1 = i1 - N_GROUPS
    e2 = i2 - N_GROUPS

    oh1 = jnp.where(lane == e1, 1.0, 0.0)
    oh2 = jnp.where(lane == e2, 1.0, 0.0)
    c1 = _mm(tri_ref[...], oh1)
    c2 = _mm(tri_ref[...], oh2)
    tot1 = jnp.sum(oh1, axis=0, keepdims=True)
    tot2 = jnp.sum(oh2, axis=0, keepdims=True)
    base = base_scr[0:1, :]
    rank1 = rsum(jnp.where(lane == e1, base + c1, 0.0))
    rank2 = rsum(jnp.where(lane == e2, base + tot1 + c2, 0.0))
    new_base = base + tot1 + tot2
    base_scr[...] = jnp.broadcast_to(new_base, base_scr.shape)
    cnt_ref[...] = jnp.broadcast_to(new_base, cnt_ref.shape)
    ri = jnp.where(lane == 0, e1, jnp.where(lane == 1, e2,
         jnp.where(lane == 2, rank1.astype(jnp.int32), jnp.where(lane == 3, rank2.astype(jnp.int32), 0))))
    ri_ref[...] = ri
    rw_ref[...] = jnp.where(lane == 0, w1, jnp.where(lane == 1, w2, 0.0))


def _router(x, m3, gain, wr, br, tri):
    return pl.pallas_call(
        _router_kernel,
        grid=(N // TM,),
        in_specs=[pl.BlockSpec((TM, D), lambda i: (i, 0)),
                  pl.BlockSpec((None, 6, D), lambda i: (_cond_row(i), 0, 0)),
                  pl.BlockSpec((1, D), lambda i: (0, 0)),
                  pl.BlockSpec((D, LANES), lambda i: (0, 0)),
                  pl.BlockSpec((1, LANES), lambda i: (0, 0)),
                  pl.BlockSpec((TM, TM), lambda i: (0, 0))],
        out_specs=[pl.BlockSpec((TM, D // 2), lambda i: (i, 0)),
                   pl.BlockSpec((TM, LANES), lambda i: (i, 0)),
                   pl.BlockSpec((TM, LANES), lambda i: (i, 0)),
                   pl.BlockSpec((8, LANES), lambda i: (0, 0))],
        out_shape=[jax.ShapeDtypeStruct((N, D // 2), jnp.int32),
                   jax.ShapeDtypeStruct((N, LANES), jnp.int32),
                   jax.ShapeDtypeStruct((N, LANES), F32),
                   jax.ShapeDtypeStruct((8, LANES), F32)],
        scratch_shapes=[pltpu.VMEM((8, LANES), F32)],
        compiler_params=_cparams(("arbitrary",)),
        name="router",
    )(x, m3, gain, wr, br, tri)


def _row_copy(src_hbm, row, dst, r, sem):
    return pltpu.make_async_copy(src_hbm.at[pl.ds(row, 1)], dst.at[pl.ds(r, 1)], sem)


def _expert_kernel(be_ref, nu_ref, d_ref, xp_hbm, wg_ref, wu_ref, wd_ref, ys_ref, xres, xbuf, st_ref, sem):
    del be_ref
    b = pl.program_id(0)
    half = D // 2

    def gather(blk, slot):
        base = blk * MOE_ROWS
        for r in range(MOE_ROWS):
            xbuf[slot, pl.ds(r, 1), :] = xres[pl.ds(st_ref[base + r], 1), :]

    @pl.when(b == 0)
    def _():
        cp = pltpu.make_async_copy(xp_hbm, xres, sem.at[0])
        cp.start()

        def clear(j, c):
            for t in range(32):
                st_ref[32 * j + t] = 0
            return c

        lax.fori_loop(0, N_SLOT_BLOCKS * MOE_ROWS // 32, clear, 0)

        def invert(j, c):
            slots = [d_ref[16 * j + t] for t in range(16)]
            for t in range(16):
                st_ref[slots[t]] = 8 * j + t // 2
            return c

        lax.fori_loop(0, 2 * N // 16, invert, 0)
        cp.wait()
        gather(0, 0)

    @pl.when(b < nu_ref[0])
    def _():
        gather(jnp.minimum(b + 1, N_SLOT_BLOCKS - 1), (b + 1) % 2)
        x_lo, x_hi = _unpack_bf16_pairs(xbuf[b % 2])
        dot = functools.partial(jnp.dot, preferred_element_type=F32)
        g = dot(x_lo, wg_ref[:half, :].astype(BF16)) + dot(x_hi, wg_ref[half:, :].astype(BF16))
        u = dot(x_lo, wu_ref[:half, :].astype(BF16)) + dot(x_hi, wu_ref[half:, :].astype(BF16))
        ys_ref[...] = _mm(_silu(g) * u, wd_ref[...])

    @pl.when(b >= nu_ref[0])
    def _():
        ys_ref[...] = jnp.zeros_like(ys_ref)


def _experts(block_e, n_used, dest, xp, w_gate, w_up, w_down, layer):
    grid_spec = pltpu.PrefetchScalarGridSpec(
        num_scalar_prefetch=3,
        grid=(N_SLOT_BLOCKS,),
        in_specs=[pl.BlockSpec(memory_space=pl.ANY),
                  pl.BlockSpec((None, None, D, D_EXPERT), lambda b, be, nu, st: (layer, be[b], 0, 0)),
                  pl.BlockSpec((None, None, D, D_EXPERT), lambda b, be, nu, st: (layer, be[b], 0, 0)),
                  pl.BlockSpec((None, None, D_EXPERT, D), lambda b, be, nu, st: (layer, be[b], 0, 0))],
        out_specs=pl.BlockSpec((MOE_ROWS, D), lambda b, be, nu, st: (b, 0)),
        scratch_shapes=[pltpu.VMEM((N, D // 2), jnp.int32),
                        pltpu.VMEM((2, MOE_ROWS, D // 2), jnp.int32),
                        pltpu.SMEM((N_SLOT_BLOCKS * MOE_ROWS,), jnp.int32),
                        pltpu.SemaphoreType.DMA((1,))])
    return pl.pallas_call(
        _expert_kernel,
        grid_spec=grid_spec,
        out_shape=jax.ShapeDtypeStruct((N_SLOT_BLOCKS * MOE_ROWS, D), F32),
        compiler_params=pltpu.CompilerParams(dimension_semantics=("arbitrary",), vmem_limit_bytes=EXPERT_VMEM_LIMIT),
        name="experts",
    )(block_e, n_used, dest, xp, w_gate, w_up, w_down)


def _combine_kernel(d_ref, ys_hbm, x_ref, m_ref, rw_ref, fn_ref, o_ref, buf, sem, *, final, tile0, n_tiles):
    i = pl.program_id(0)

    def issue(tile, slot):
        base = (tile0 + tile) * (2 * TM)
        for r in range(TM):
            for kk in range(2):
                _row_copy(ys_hbm, d_ref[base + 2 * r + kk], buf.at[slot, kk], r, sem.at[slot]).start()

    @pl.when(i == 0)
    def _():
        issue(0, 0)

    @pl.when(i + 1 < n_tiles)
    def _():
        issue(i + 1, (i + 1) % 2)

    slot = i % 2
    for kk in range(2):
        pltpu.make_async_copy(ys_hbm.at[pl.ds(0, TM)], buf.at[slot, kk], sem.at[slot]).wait()
    w = rw_ref[...]
    y = w[:, 0:1] * buf[slot, 0] + w[:, 1:2] * buf[slot, 1]
    out = x_ref[...] + m_ref[5:6, :] * y
    if final:
        ms = jnp.mean(out * out, axis=-1, keepdims=True)
        out = out * lax.rsqrt(ms + EPS) * fn_ref[...]
    o_ref[...] = out


def _combine(dest_flat, ys, x, m3, rw, final_norm, final, tile0=0, n_tiles=N // TM):
    grid_spec = pltpu.PrefetchScalarGridSpec(
        num_scalar_prefetch=1,
        grid=(n_tiles,),
        in_specs=[pl.BlockSpec(memory_space=pl.ANY),
                  pl.BlockSpec((TM, D), lambda i, d: (tile0 + i, 0)),
                  pl.BlockSpec((None, 6, D), lambda i, d: (_cond_row(tile0 + i), 0, 0)),
                  pl.BlockSpec((TM, LANES), lambda i, d: (tile0 + i, 0)),
                  pl.BlockSpec((1, D), lambda i, d: (0, 0))],
        out_specs=pl.BlockSpec((TM, D), lambda i, d: (i, 0)),
        scratch_shapes=[pltpu.VMEM((2, 2, TM, D), F32), pltpu.SemaphoreType.DMA((2,))])
    return pl.pallas_call(
        functools.partial(_combine_kernel, final=final, tile0=tile0, n_tiles=n_tiles),
        grid_spec=grid_spec,
        out_shape=jax.ShapeDtypeStruct((n_tiles * TM, D), F32),
        compiler_params=_cparams(("arbitrary",)),
        name="combine",
    )(dest_flat, ys, x, m3, rw, final_norm)


def _hier_moe(x, m3, gain, layer, w_rg, b_rg, w_re, b_re, w_gate, w_up, w_down, tri_tm, final_norm, final):
    wr = jnp.concatenate([w_rg, w_re.transpose(1, 0, 2).reshape(D, N_EXPERTS),
                          jnp.zeros((D, LANES - N_GROUPS - N_EXPERTS), F32)], axis=1)
    br = jnp.concatenate([b_rg, b_re.reshape(N_EXPERTS),
                          jnp.zeros((LANES - N_GROUPS - N_EXPERTS,), F32)])[None, :]
    xf, ri, rw, cnt = _router(x, m3, gain, wr, br, tri_tm)
    e_idx = ri[:, 0:2]
    rank = ri[:, 2:4]
    counts = cnt[0, :N_EXPERTS].astype(jnp.int32)
    padded = (counts + MOE_ROWS - 1) // MOE_ROWS * MOE_ROWS
    end_pad = jnp.cumsum(padded)
    start_pad = end_pad - padded
    experts = jnp.arange(N_EXPERTS, dtype=jnp.int32)
    start_of = jnp.sum(jnp.where(e_idx[:, :, None] == experts, start_pad, 0), axis=-1)
    dest = (start_of + rank).reshape(-1).astype(jnp.int32)
    block_start = jnp.arange(N_SLOT_BLOCKS, dtype=jnp.int32) * MOE_ROWS
    block_e = jnp.minimum(jnp.sum((end_pad[None, :] <= block_start[:, None]).astype(jnp.int32), axis=1),
                          N_EXPERTS - 1)
    n_used = (end_pad[-1:] // MOE_ROWS).astype(jnp.int32)
    ys = _experts(block_e, n_used, dest, xf, w_gate, w_up, w_down, layer)
    if not final:
        return _combine(dest, ys, x, m3, rw, final_norm, False)
    nct = NC // TM
    return (_combine(dest, ys, x, m3, rw, final_norm, True, 0, nct),
            _combine(dest, ys, x, m3, rw, final_norm, True, nct, N // TM - nct))


def kernel(x_prompt, x_sample, state_A_fwd, state_A_bwd, cache_k, cache_v, c, c_ctx, mod_w, mod_b, norm_mix, norm_ffn, ab_w_in, ab_conv, ab_a_log, ab_dt_bias, ab_o_gain, ab_w_out, na_w_qkv, na_rpb, na_w_out, moe_w_rg, moe_b_rg, moe_w_re, moe_b_re, moe_w_gate, moe_w_up, moe_w_down, final_norm):
    x = (x_prompt.reshape(NC, D), x_sample.reshape(NL, D))
    cond8 = jnp.concatenate([c_ctx[None, :], c, jnp.zeros((8 - 1 - DEC_BATCH, D), F32)], axis=0)
    mods = _ada_params(cond8, mod_w, mod_b).reshape(DEPTH, 8, 6, D)

    tri_tm = jnp.asarray(np.tril(np.ones((TM, TM)), -1), BF16)
    tri_c = jnp.asarray(np.tril(np.ones((CHUNK, CHUNK))), F32)
    half = np.arange(LANES) < A_DH
    jbd = jnp.asarray((half[:, None] == half[None, :]).astype(np.float32))
    fn = final_norm[None, :]

    m3 = mods[0]
    w_in = ab_w_in[0]
    w_in = jnp.concatenate([w_in[:, :2048], w_in[:, 2080:2592], w_in[:, 2048:2080],
                            jnp.zeros((D, AB_COLS - 2592), F32)], axis=1).astype(BF16)
    proj = _modproj(x, m3, norm_mix[0][None, :], w_in, 896)
    gate_p = jnp.zeros((8, LANES), F32)
    gate_p = gate_p.at[0, :2 * A_HEADS].set(ab_a_log[0].reshape(-1))
    gate_p = gate_p.at[1, :2 * A_HEADS].set(ab_dt_bias[0].reshape(-1))
    o_gain2 = jnp.tile(ab_o_gain[0], 2)[None, :]
    zeros_state = jnp.zeros((BATCH, A_HEADS, A_DH, A_DH), F32)
    mix_a_c, s_f, s_b = _deltanet(proj, ab_conv[0], gate_p, o_gain2, zeros_state, zeros_state, tri_c, jbd,
                                  seq_len=SEQ, n_seq=BATCH, n_sub=4, row_blk0=0)
    mix_a_l, _, _ = _deltanet(proj, ab_conv[0], gate_p, o_gain2, state_A_fwd[:, 0], state_A_bwd[:, 0], tri_c, jbd,
                              seq_len=DEC_SEQ, n_seq=DEC_BATCH, n_sub=1, row_blk0=NC // DEC_SEQ)

    cc, sc = _dft_mats(LANES)
    ct, st = _dft_mats(SEQ)
    c64, s64 = _dft_mats(64)
    cs = jnp.asarray(np.concatenate([cc, sc], axis=1), BF16)
    dft = jnp.asarray(np.concatenate([ct, -st], axis=1), BF16)
    ca = jnp.asarray(np.concatenate([cc, -sc, -sc, -cc], axis=1), BF16)
    m1 = jnp.asarray(np.concatenate([c64, s64], axis=1), BF16)
    tw_idx = np.arange(64)
    tw_ang = 2.0 * np.pi * (tw_idx[:, None] * tw_idx[None, :]) / DEC_SEQ
    twc = jnp.broadcast_to(jnp.asarray(np.cos(tw_ang), F32)[:, :, None], (64, 64, LANES))
    tws = jnp.broadcast_to(jnp.asarray(np.sin(tw_ang), F32)[:, :, None], (64, 64, LANES))
    mix_b_c = _fnet_ctx(proj, cs, dft)
    mix_b_l = _fnet_latent(proj, ca, m1, twc, tws)

    w_out = ab_w_out[0].astype(BF16)
    x = _outproj([(mix_a_c, mix_a_l), (mix_b_c, mix_b_l)], [w_out[:A_WIDTH], w_out[A_WIDTH:]], x, m3)
    x = _hier_moe(x, m3, norm_ffn[0][None, :], 0, moe_w_rg[0], moe_b_rg[0], moe_w_re[0], moe_b_re[0],
                  moe_w_gate, moe_w_up, moe_w_down, tri_tm, fn, False)

    m3 = mods[1]
    qkv = _modproj(x, m3, norm_mix[1][None, :], na_w_qkv[0].astype(BF16), 512)
    attn_c = _na_ctx(qkv)
    attn_l = _na_latent(qkv, cache_k[:, 0].reshape(DEC_BATCH, 256, D), cache_v[:, 0].reshape(DEC_BATCH, 256, D),
                        _rpb_tables(na_rpb[0]))
    x = _outproj([(attn_c, attn_l)], [na_w_out[0].astype(BF16)], x, m3)
    y_c, y_l = _hier_moe(x, m3, norm_ffn[1][None, :], 1, moe_w_rg[1], moe_b_rg[1], moe_w_re[1], moe_b_re[1],
                         moe_w_gate, moe_w_up, moe_w_down, tri_tm, fn, True)

    new_k = qkv[:NC, D:2 * D].reshape(BATCH, 1, SEQ, NA_HEADS, NA_DH)
    new_v = qkv[:NC, 2 * D:].reshape(BATCH, 1, SEQ, NA_HEADS, NA_DH)
    return (y_c.reshape(BATCH, SEQ, D), y_l.reshape(DEC_BATCH, DEC_SEQ, D),
            s_f[:, None], s_b[:, None], new_k, new_v)
```

```python
import functools
import math

import numpy as np
import jax
import jax.numpy as jnp
from jax import lax
from jax.experimental import pallas as pl
from jax.experimental.pallas import tpu as pltpu

F32 = jnp.float32
BF16 = jnp.bfloat16
HIGHEST = lax.Precision.HIGHEST

D = 1024
BATCH, SEQ = 32, 256
DEC_BATCH, DEC_SEQ = 2, 4096
NC = BATCH * SEQ
NL = DEC_BATCH * DEC_SEQ
N = NC + NL
DEPTH = 2
GRID_W = 64
A_DH = 64
A_HEADS = 8
A_WIDTH = 512
CHUNK = 64
B_WIDTH = 512
B_GROUPS = 4
NA_DH = 64
NA_HEADS = 16
WIN_R, WIN_C = 8, 16
N_GROUPS, EXP_PER_GROUP, N_EXPERTS = 4, 8, 32
D_EXPERT = 512
EPS = 1e-6

LANES = 128
TM = 256
MOE_ROWS = 256
N_SLOT_BLOCKS = (2 * N) // MOE_ROWS + N_EXPERTS
AB_COLS = 2688
VMEM_LIMIT = 56 * 1024 * 1024
EXPERT_VMEM_LIMIT = 60 * 1024 * 1024
NEG = -1e30


def _cparams(sem):
    return pltpu.CompilerParams(dimension_semantics=sem, vmem_limit_bytes=VMEM_LIMIT)


def _mm(a, b):
    return jnp.dot(a.astype(BF16), b.astype(BF16), preferred_element_type=F32)


def _mm_nt(a, b):
    return lax.dot_general(a.astype(BF16), b.astype(BF16), (((1,), (1,)), ((), ())),
                           preferred_element_type=F32)


SOLVE_BLK = 16
NA_CTX_PAIRS = 2
NA_ROW_UNROLL = 4
TERM_UNROLL = 4


def _unit_lower_solve(a_mat, rhs, same_blk, eye):
    dg = jnp.where(same_blk, a_mat, 0.0)
    p = -dg
    dinv = eye + p
    for _ in range(int(math.log2(SOLVE_BLK)) - 1):
        p = _mm(p, p)
        dinv = dinv + _mm(p, dinv)
    mp = -_mm(dinv, a_mat - dg)
    y = _mm(dinv, rhs)
    y = y + _mm(mp, y)
    for _ in range(int(math.log2(a_mat.shape[0] // SOLVE_BLK)) - 1):
        mp = _mm(mp, mp)
        y = y + _mm(mp, y)
    return y


def _unit_lower_solve_many(a_mats, rhs, same_blk, eye):
    dg = [jnp.where(same_blk, a, 0.0) for a in a_mats]
    off = [a - g for a, g in zip(a_mats, dg)]
    p = [-g for g in dg]
    dinv = [eye + x for x in p]
    for _ in range(int(math.log2(SOLVE_BLK)) - 1):
        p = [_mm(x, x) for x in p]
        dinv = [di + _mm(x, di) for x, di in zip(p, dinv)]
    mp = [-_mm(di, o) for di, o in zip(dinv, off)]
    y = [_mm(di, r) for di, r in zip(dinv, rhs)]
    y = [yi + _mm(m, yi) for m, yi in zip(mp, y)]
    for _ in range(int(math.log2(a_mats[0].shape[0] // SOLVE_BLK)) - 1):
        mp = [_mm(m, m) for m in mp]
        y = [yi + _mm(m, yi) for m, yi in zip(mp, y)]
    return y


def _mm_split(a, b, parts, split_rhs=False):
    x = b if split_rhs else a
    acc = None
    for _ in range(parts):
        piece = x.astype(BF16)
        x = x - piece.astype(F32)
        term = (jnp.dot(a.astype(BF16), piece, preferred_element_type=F32) if split_rhs
                else jnp.dot(piece, b.astype(BF16), preferred_element_type=F32))
        acc = term if acc is None else acc + term
    return acc


def _mm_hi(a, b):
    return jnp.dot(a, b, preferred_element_type=F32, precision=HIGHEST)


def _silu(x):
    return x * jax.nn.sigmoid(x)


def _bf16_bits(x):
    b = lax.bitcast_convert_type(x, jnp.int32)
    return b + 0x7FFF + (lax.shift_right_logical(b, jnp.int32(16)) & 1)


_HIGH16 = -65536


def _pack_bf16_pairs(a, b):
    return lax.shift_right_logical(_bf16_bits(a), jnp.int32(16)) | (_bf16_bits(b) & _HIGH16)


def _unpack_bf16_pairs(p):
    a = lax.bitcast_convert_type(lax.shift_left(p, jnp.int32(16)), F32)
    b = lax.bitcast_convert_type(p & _HIGH16, F32)
    return a.astype(BF16), b.astype(BF16)


def _cond_row(i):
    return jnp.where(i < NC // TM, 0, 1 + (i - NC // TM) // (DEC_SEQ // TM))


def _modulated_norm(x, m_ref, g_ref, shift_idx, scale_idx):
    ms = jnp.mean(x * x, axis=-1, keepdims=True)
    y = x * lax.rsqrt(ms + EPS) * g_ref[...]
    return y * (1.0 + m_ref[scale_idx:scale_idx + 1, :]) + m_ref[shift_idx:shift_idx + 1, :]


def _ada_kernel(cond_ref, w_ref, b_ref, o_ref):
    o_ref[...] = _mm_hi(_silu(cond_ref[...]), w_ref[...]) + b_ref[...]


def _ada_params(cond8, mod_w, mod_b):
    tn = 1536
    return pl.pallas_call(
        _ada_kernel,
        grid=(DEPTH, 6 * D // tn),
        in_specs=[pl.BlockSpec((8, D), lambda l, j: (0, 0)),
                  pl.BlockSpec((None, D, tn), lambda l, j: (l, 0, j)),
                  pl.BlockSpec((None, 1, tn), lambda l, j: (l, 0, j))],
        out_specs=pl.BlockSpec((None, 8, tn), lambda l, j: (l, 0, j)),
        out_shape=jax.ShapeDtypeStruct((DEPTH, 8, 6 * D), F32),
        compiler_params=_cparams(("arbitrary", "arbitrary")),
        name="ada_params",
    )(cond8, mod_w, mod_b.reshape(DEPTH, 1, 6 * D))


def _token_specs(x):
    nct = NC // TM
    if isinstance(x, tuple):
        return ([pl.BlockSpec((TM, x[0].shape[1]), lambda i: (jnp.minimum(i, nct - 1), 0)),
                 pl.BlockSpec((TM, x[1].shape[1]), lambda i: (jnp.maximum(i - nct, 0), 0))], list(x))
    return [pl.BlockSpec((TM, x.shape[1]), lambda i: (i, 0))], [x]


def _token_rows(refs):
    if len(refs) == 1:
        return refs[0][...]
    return jnp.where(pl.program_id(0) < NC // TM, refs[0][...], refs[1][...])


def _modproj_kernel(*refs, n_chunk):
    m_ref, g_ref, w_ref, o_ref = refs[-4:]
    hb = _modulated_norm(_token_rows(refs[:-4]), m_ref, g_ref, 0, 1).astype(BF16)
    for j in range(o_ref.shape[1] // n_chunk):
        sl = slice(j * n_chunk, (j + 1) * n_chunk)
        o_ref[:, sl] = jnp.dot(hb, w_ref[:, sl], preferred_element_type=F32)


def _modproj(x, m3, gain, w_bf16, n_chunk):
    nout = w_bf16.shape[1]
    x_specs, x_args = _token_specs(x)
    return pl.pallas_call(
        functools.partial(_modproj_kernel, n_chunk=n_chunk),
        grid=(N // TM,),
        in_specs=x_specs + [pl.BlockSpec((None, 6, D), lambda i: (_cond_row(i), 0, 0)),
                            pl.BlockSpec((1, D), lambda i: (0, 0)),
                            pl.BlockSpec((D, nout), lambda i: (0, 0))],
        out_specs=pl.BlockSpec((TM, nout), lambda i: (i, 0)),
        out_shape=jax.ShapeDtypeStruct((N, nout), F32),
        compiler_params=_cparams(("arbitrary",)),
        name="modproj",
    )(*x_args, m3, gain, w_bf16)


def _outproj_kernel(*refs, n_in):
    a_refs = refs[:2 * n_in]
    w_refs = refs[2 * n_in:3 * n_in]
    x_refs = refs[3 * n_in:-2]
    m_ref, o_ref = refs[-2:]
    acc = None
    for j, w_ref in enumerate(w_refs):
        part = _mm(_token_rows(a_refs[2 * j:2 * j + 2]), w_ref[...])
        acc = part if acc is None else acc + part
    o_ref[...] = _token_rows(x_refs) + m_ref[2:3, :] * acc


def _outproj(a_pairs, w_list, x, m3):
    n_in = len(a_pairs)
    in_specs, args = [], []
    for pair in a_pairs:
        specs, ops = _token_specs(pair)
        in_specs += specs
        args += ops
    x_specs, x_args = _token_specs(x)
    in_specs += ([pl.BlockSpec(w.shape, lambda i: (0, 0)) for w in w_list] + x_specs
                 + [pl.BlockSpec((None, 6, D), lambda i: (_cond_row(i), 0, 0))])
    return pl.pallas_call(
        functools.partial(_outproj_kernel, n_in=n_in),
        grid=(N // TM,),
        in_specs=in_specs,
        out_specs=pl.BlockSpec((TM, D), lambda i: (i, 0)),
        out_shape=jax.ShapeDtypeStruct((N, D), F32),
        compiler_params=_cparams(("arbitrary",)),
        name="outproj",
    )(*args, *w_list, *x_args, m3)


def _deltanet_kernel(q_ref, k_ref, v_ref, z_ref, ab_ref, cq_ref, ck_ref, cv_ref, gp_ref, og_ref,
                     s0f_ref, s0b_ref, tri_ref, jbd_ref, o_ref, sf_ref, sb_ref,
                     u_s, w_s, qd_s, at_s, kt_s, ge_s, st_s, ob, *, seq_len, n_sub):
    hp = pl.program_id(1)
    C = CHUNK
    nc = seq_len // C
    nct = n_sub * nc
    P = LANES
    lane = lax.broadcasted_iota(jnp.int32, (C, P), 1)
    row = lax.broadcasted_iota(jnp.int32, (C, P), 0)
    first_head = lane < A_DH
    ri = lax.broadcasted_iota(jnp.int32, (P, P), 0)
    ci = lax.broadcasted_iota(jnp.int32, (P, P), 1)
    same_head = (ri < C) == (ci < C)
    same_blk = (ri // SOLVE_BLK) == (ci // SOLVE_BLK)
    eye = jnp.where(ri == ci, 1.0, 0.0)
    jbd = jbd_ref[...]
    lincl = tri_ref[...]
    cum_b = [lincl.T.astype(BF16), lincl.astype(BF16)]
    incl_m = [jnp.logical_and(same_head, ri >= ci), jnp.logical_and(same_head, ri <= ci)]
    strict_m = [jnp.logical_and(same_head, ri > ci), jnp.logical_and(same_head, ri < ci)]
    neg_a = -jnp.exp(gp_ref[0])
    dt_b = gp_ref[1]

    def conv_silu(ref, w_ref, c):
        base = pl.multiple_of(c * C, C)
        cs = c % nc
        xc = ref[pl.ds(base, C), :]
        pbase = pl.multiple_of(jnp.maximum(base - 8, 0), 8)
        nbase = pl.multiple_of(jnp.minimum(base + C, n_sub * seq_len - 8), 8)
        prev_row = ref[pl.ds(pbase, 8), :][7:8, :] * jnp.where(cs > 0, 1.0, 0.0)
        next_row = ref[pl.ds(nbase, 8), :][0:1, :] * jnp.where(cs < nc - 1, 1.0, 0.0)
        x_prev = jnp.where(row == 0, prev_row, pltpu.roll(xc, 1, 0))
        x_next = jnp.where(row == C - 1, next_row, pltpu.roll(xc, C - 1, 0))
        y = w_ref[0:1, :] * x_prev + w_ref[1:2, :] * xc + w_ref[2:3, :] * x_next
        return _silu(y)

    def stack(x):
        return jnp.concatenate([jnp.where(first_head, x, 0.0), jnp.where(first_head, 0.0, x)], axis=0)

    def chunk_inputs(c):
        base = pl.multiple_of(c * C, C)
        q = conv_silu(q_ref, cq_ref, c)
        k = conv_silu(k_ref, ck_ref, c)
        v = conv_silu(v_ref, cv_ref, c)
        return q, k, v, ab_ref[pl.ds(base, C), :].T

    sub8 = lax.broadcasted_iota(jnp.int32, (A_HEADS, C), 0)

    def pair_row(x8):
        r0 = jnp.sum(jnp.where(sub8 == 2 * hp, x8, 0.0), axis=0, keepdims=True)
        r1 = jnp.sum(jnp.where(sub8 == 2 * hp + 1, x8, 0.0), axis=0, keepdims=True)
        return jnp.concatenate([r0, r1], axis=1)

    def chain_gates(ab_t, d):
        a8 = ab_t[d * A_HEADS:(d + 1) * A_HEADS, :]
        b8 = ab_t[(2 + d) * A_HEADS:(3 + d) * A_HEADS, :]
        g8 = neg_a[d * A_HEADS:(d + 1) * A_HEADS, :C] * jax.nn.softplus(a8 + dt_b[d * A_HEADS:(d + 1) * A_HEADS, :C])
        gc8 = _mm_split(g8, cum_b[d], 3)
        tot8 = jnp.broadcast_to(jnp.sum(g8, axis=-1, keepdims=True), (A_HEADS, C))
        gc_row, beta_row, tot_row = pair_row(gc8), pair_row(jax.nn.sigmoid(b8)), pair_row(tot8)
        cols = jnp.concatenate([gc_row, beta_row, tot_row, jnp.zeros((5, P), F32)], axis=0).T
        return gc_row, tot_row, cols[:, 0:1], cols[:, 1:2], cols[:, 2:3]

    def terms_body(j, carry):
        cs = [TERM_UNROLL * j + t for t in range(TERM_UNROLL)]
        ins = [chunk_inputs(c) for c in cs]
        qsq = [_mm_split(x[0] * x[0], jbd, 2) for x in ins]
        ksq = [_mm_split(x[1] * x[1], jbd, 2) for x in ins]
        qs = [x[0] * lax.rsqrt(s + EPS) * (A_DH ** -0.5) for x, s in zip(ins, qsq)]
        ks = [x[1] * lax.rsqrt(s + EPS) for x, s in zip(ins, ksq)]
        qst = [stack(x) for x in qs]
        kst = [stack(x) for x in ks]
        vst = [stack(x[2]) for x in ins]
        kst_t = [x.T for x in kst]
        kk = [_mm_nt(x, x) for x in kst]
        qk = [_mm_nt(x, y) for x, y in zip(qst, kst)]
        chains = [(t, d) for t in range(TERM_UNROLL) for d in range(2)]
        gates = [chain_gates(ins[t][3], d) for t, d in chains]
        decay, e_gc = [], []
        for (t, d), (gc_row, tot_row, gc_col, beta_col, tot_col) in zip(chains, gates):
            diff = jnp.broadcast_to(gc_col, (P, P)) - jnp.broadcast_to(gc_row, (P, P))
            decay.append(jnp.where(incl_m[d], jnp.exp(jnp.where(incl_m[d], diff, 0.0)), 0.0))
            e_gc.append(jnp.exp(gc_col))
        a_mats = [jnp.where(strict_m[d], g[3] * kk[t] * dc, 0.0) for (t, d), g, dc in zip(chains, gates, decay)]
        rhs = [vst[t] * g[3] + pltpu.roll(kst[t] * (g[3] * e), A_DH, 1)
               for (t, d), g, e in zip(chains, gates, e_gc)]
        xs = _unit_lower_solve_many(a_mats, rhs, same_blk, eye)
        for (t, d), x, g, e, dc in zip(chains, xs, gates, e_gc, decay):
            c = cs[t]
            gc_row, tot_row = g[0], g[1]
            u_s[d, c] = jnp.where(same_head, x, 0.0).astype(BF16)
            w_s[d, c] = pltpu.roll(jnp.where(same_head, 0.0, x), A_DH, 1).astype(BF16)
            qd_s[d, c] = (qst[t] * e).astype(BF16)
            at_s[d, c] = jnp.where(incl_m[d], qk[t] * dc, 0.0).astype(BF16)
            kt_s[d, c] = (kst_t[t] * jnp.exp(tot_row - gc_row)).astype(BF16)
            ge_s[d, c] = jnp.broadcast_to(jnp.exp(tot_row), (8, P))
        return carry

    lax.fori_loop(0, nct // TERM_UNROLL, terms_body, 0)

    def block_diag(s2):
        z = jnp.zeros((A_DH, A_DH), F32)
        return jnp.concatenate([jnp.concatenate([s2[0], z], axis=1),
                                jnp.concatenate([z, s2[1]], axis=1)], axis=0)

    for s in range(n_sub):
        st_s[2 * s] = block_diag(s0f_ref[s])
        st_s[2 * s + 1] = block_diag(s0b_ref[s])

    def scan_body(i, carry):
        chains = [(s, d, s * nc + (i if d == 0 else nc - 1 - i)) for s in range(n_sub) for d in range(2)]
        dot = functools.partial(jnp.dot, preferred_element_type=F32)
        s_bd = [st_s[2 * s + d] for s, d, c in chains]
        sb16 = [x.astype(BF16) for x in s_bd]
        ws = [dot(w_s[d, c], sb) for (s, d, c), sb in zip(chains, sb16)]
        qs_ = [dot(qd_s[d, c], sb) for (s, d, c), sb in zip(chains, sb16)]
        vb = [(u_s[d, c].astype(F32) - x).astype(BF16) for (s, d, c), x in zip(chains, ws)]
        av = [dot(at_s[d, c], x) for (s, d, c), x in zip(chains, vb)]
        kv = [dot(kt_s[d, c], x) for (s, d, c), x in zip(chains, vb)]
        for (s, d, c), sb, q_, a_, k_ in zip(chains, s_bd, qs_, av, kv):
            st_s[2 * s + d] = sb * ge_s[d, c][0:1, :] + k_
            o_st = q_ + a_
            dst = o_ref if d == 0 else ob
            dst[pl.ds(pl.multiple_of(c * C, C), C), :] = o_st[:C] + o_st[C:]
        return carry

    lax.fori_loop(0, nc, scan_body, 0)

    for s in range(n_sub):
        for d, ref in ((0, sf_ref), (1, sb_ref)):
            s_bd = st_s[2 * s + d]
            ref[s, 0] = s_bd[:A_DH, :A_DH]
            ref[s, 1] = s_bd[A_DH:, A_DH:]

    def finish(j, carry):
        bases = [pl.multiple_of((TERM_UNROLL * j + t) * C, C) for t in range(TERM_UNROLL)]
        o = [o_ref[pl.ds(b, C), :] + ob[pl.ds(b, C), :] for b in bases]
        ms = [_mm_split(x * x, jbd, 2) * (1.0 / A_DH) for x in o]
        for b, x, m in zip(bases, o, ms):
            o_ref[pl.ds(b, C), :] = x * lax.rsqrt(m + EPS) * og_ref[...] * _silu(z_ref[pl.ds(b, C), :])
        return carry

    lax.fori_loop(0, nct // TERM_UNROLL, finish, 0)


def _deltanet(proj, conv_w, gate_p, o_gain2, s0f, s0b, tri, jbd, *, seq_len, n_seq, n_sub, row_blk0):
    rows = n_sub * seq_len
    nct = rows // CHUNK
    rb = lambda b: row_blk0 + b
    col = lambda off: (lambda b, hp: (rb(b), off + hp))
    st_spec = pl.BlockSpec((n_sub, 2, A_DH, A_DH), lambda b, hp: (b, hp, 0, 0))
    in_specs = [pl.BlockSpec((rows, LANES), col(0)),
                pl.BlockSpec((rows, LANES), col(4)),
                pl.BlockSpec((rows, LANES), col(8)),
                pl.BlockSpec((rows, LANES), col(12)),
                pl.BlockSpec((rows, LANES), lambda b, hp: (rb(b), 20)),
                pl.BlockSpec((3, LANES), lambda b, hp: (0, hp)),
                pl.BlockSpec((3, LANES), lambda b, hp: (0, 4 + hp)),
                pl.BlockSpec((3, LANES), lambda b, hp: (0, 8 + hp)),
                pl.BlockSpec((2, 2 * A_HEADS, LANES), lambda b, hp: (0, 0, 0)),
                pl.BlockSpec((1, LANES), lambda b, hp: (0, 0)),
                st_spec, st_spec,
                pl.BlockSpec((CHUNK, CHUNK), lambda b, hp: (0, 0)),
                pl.BlockSpec((LANES, LANES), lambda b, hp: (0, 0))]
    args = [proj, proj, proj, proj, proj, conv_w, conv_w, conv_w, gate_p, o_gain2, s0f, s0b, tri, jbd]
    st_shape = jax.ShapeDtypeStruct((n_seq, A_HEADS, A_DH, A_DH), F32)
    tile = lambda dt: pltpu.VMEM((2, nct, LANES, LANES), dt)
    return pl.pallas_call(
        functools.partial(_deltanet_kernel, seq_len=seq_len, n_sub=n_sub),
        grid=(n_seq // n_sub, A_HEADS // 2),
        in_specs=in_specs,
        out_specs=[pl.BlockSpec((rows, LANES), lambda b, hp: (b, hp)), st_spec, st_spec],
        out_shape=[jax.ShapeDtypeStruct((n_seq * seq_len, A_WIDTH), F32), st_shape, st_shape],
        scratch_shapes=[tile(BF16), tile(BF16), tile(BF16), tile(BF16), tile(BF16),
                        pltpu.VMEM((2, nct, 8, LANES), F32),
                        pltpu.VMEM((2 * n_sub, LANES, LANES), F32),
                        pltpu.VMEM((rows, LANES), F32)],
        compiler_params=_cparams(("arbitrary", "arbitrary")),
        name="deltanet",
    )(*args)


def _dft_mats(n):
    idx = np.arange(n)
    ang = 2.0 * np.pi * ((idx[:, None] * idx[None, :]) % n) / n
    return np.cos(ang), np.sin(ang)


def _fnet_ctx_kernel(u_ref, cs_ref, dft_ref, o_ref):
    norm = 1.0 / math.sqrt(SEQ * LANES)
    for g in range(B_GROUPS):
        sl = slice(g * LANES, (g + 1) * LANES)
        p = _mm(u_ref[:, sl], cs_ref[...])
        stack = jnp.concatenate([p[:, :LANES], p[:, LANES:]], axis=0)
        o_ref[:, sl] = _mm(dft_ref[...], stack) * norm


def _fnet_ctx(proj, cs, dft):
    return pl.pallas_call(
        _fnet_ctx_kernel,
        grid=(BATCH,),
        in_specs=[pl.BlockSpec((SEQ, B_WIDTH), lambda b: (b, 4)),
                  pl.BlockSpec(cs.shape, lambda b: (0, 0)),
                  pl.BlockSpec(dft.shape, lambda b: (0, 0))],
        out_specs=pl.BlockSpec((SEQ, B_WIDTH), lambda b: (b, 0)),
        out_shape=jax.ShapeDtypeStruct((NC, B_WIDTH), F32),
        compiler_params=_cparams(("arbitrary",)),
        name="fnet_ctx",
    )(proj, cs, dft)


FN_SUB = 4


def _fnet_lat1_kernel(u_ref, ca_ref, m1_ref, twc_ref, tws_ref, o_ref):
    r = 64
    for j in range(FN_SUB):
        c = twc_ref[j]
        s = tws_ref[j]
        rows = slice(j * r, (j + 1) * r)
        for g in range(B_GROUPS):
            pa = _mm(u_ref[rows, g * LANES:(g + 1) * LANES], ca_ref[...])
            rhs = jnp.concatenate([pa[:, :2 * LANES], pa[:, 2 * LANES:]], axis=0)
            zz = _mm(m1_ref[...], rhs)
            zr, zi = zz[:, :LANES], zz[:, LANES:]
            o_ref[rows, 2 * g * LANES:(2 * g + 1) * LANES] = (zr * c + zi * s).astype(BF16)
            o_ref[rows, (2 * g + 1) * LANES:(2 * g + 2) * LANES] = (zi * c - zr * s).astype(BF16)


def _fnet_lat2_kernel(z_ref, m1_ref, o_ref):
    r = 64
    norm = 1.0 / math.sqrt(DEC_SEQ * LANES)
    for j in range(FN_SUB):
        rows = slice(j * r, (j + 1) * r)
        for g in range(B_GROUPS):
            rhs = jnp.concatenate([z_ref[rows, 2 * g * LANES:(2 * g + 1) * LANES],
                                   z_ref[rows, (2 * g + 1) * LANES:(2 * g + 2) * LANES]], axis=0)
            o_ref[rows, g * LANES:(g + 1) * LANES] = _mm(m1_ref[...], rhs) * norm


def _fnet_latent(proj, ca, m1, twc, tws):
    r = 64
    u = proj[NC:, 2048:2560].astype(BF16).reshape(DEC_BATCH, r, r, B_WIDTH)
    u = u.transpose(0, 2, 1, 3).reshape(DEC_BATCH * r * r, B_WIDTH)
    steps = DEC_BATCH * r // FN_SUB
    blk = FN_SUB * r
    z = pl.pallas_call(
        _fnet_lat1_kernel,
        grid=(steps,),
        in_specs=[pl.BlockSpec((blk, B_WIDTH), lambda s: (s, 0)),
                  pl.BlockSpec(ca.shape, lambda s: (0, 0)),
                  pl.BlockSpec(m1.shape, lambda s: (0, 0)),
                  pl.BlockSpec((FN_SUB, r, LANES), lambda s: (s % (r // FN_SUB), 0, 0)),
                  pl.BlockSpec((FN_SUB, r, LANES), lambda s: (s % (r // FN_SUB), 0, 0))],
        out_specs=pl.BlockSpec((blk, 2 * B_WIDTH), lambda s: (s, 0)),
        out_shape=jax.ShapeDtypeStruct((NL, 2 * B_WIDTH), BF16),
        compiler_params=_cparams(("arbitrary",)),
        name="fnet_lat1",
    )(u, ca, m1, twc, tws)
    z = z.reshape(DEC_BATCH, r, r, 2 * B_WIDTH).transpose(0, 2, 1, 3).reshape(NL, 2 * B_WIDTH)
    y = pl.pallas_call(
        _fnet_lat2_kernel,
        grid=(steps,),
        in_specs=[pl.BlockSpec((blk, 2 * B_WIDTH), lambda s: (s, 0)),
                  pl.BlockSpec(m1.shape, lambda s: (0, 0))],
        out_specs=pl.BlockSpec((blk, B_WIDTH), lambda s: (s, 0)),
        out_shape=jax.ShapeDtypeStruct((NL, B_WIDTH), F32),
        compiler_params=_cparams(("arbitrary",)),
        name="fnet_lat2",
    )(z, m1)
    return y.reshape(DEC_BATCH, r, r, B_WIDTH).transpose(0, 2, 1, 3).reshape(NL, B_WIDTH)


def _head_masks():
    lane = lax.broadcasted_iota(jnp.int32, (1, LANES), 1)
    return lane < NA_DH


def _attend_many(chains):
    scale = NA_DH ** -0.5
    s = [[_mm_nt(q, k) * scale if b is None else _mm_nt(q, k) * scale + b for k, v, b in kv] for q, kv in chains]
    m = [functools.reduce(jnp.maximum, [jnp.max(x, axis=-1, keepdims=True) for x in xs]) for xs in s]
    p = [[jnp.exp(x - mi) for x in xs] for xs, mi in zip(s, m)]
    l = [sum(jnp.sum(x, axis=-1, keepdims=True) for x in xs) for xs in p]
    o = [sum(_mm(x, v) for x, (k, v, b) in zip(xs, kv)) for xs, (q, kv) in zip(p, chains)]
    return [oi / li for oi, li in zip(o, l)]


def _na_ctx_kernel(q_ref, k_ref, v_ref, o_ref):
    first = _head_masks()
    n_split = 2
    rows = SEQ // n_split
    chains = []
    for p in range(NA_CTX_PAIRS):
        lanes = slice(p * LANES, (p + 1) * LANES)
        k = k_ref[:, lanes].astype(BF16)
        v = v_ref[:, lanes].astype(BF16)
        for j in range(n_split):
            q = q_ref[j * rows:(j + 1) * rows, lanes]
            for a in range(2):
                chains.append((jnp.where(first if a == 0 else jnp.logical_not(first), q, 0.0), [(k, v, None)]))
    outs = _attend_many(chains)
    for p in range(NA_CTX_PAIRS):
        for j in range(n_split):
            i = 2 * (p * n_split + j)
            o_ref[j * rows:(j + 1) * rows, p * LANES:(p + 1) * LANES] = jnp.where(first, outs[i], outs[i + 1])


def _na_ctx(qkv):
    w = NA_CTX_PAIRS * LANES
    nblk = D // w
    return pl.pallas_call(
        _na_ctx_kernel,
        grid=(BATCH, nblk),
        in_specs=[pl.BlockSpec((SEQ, w), lambda b, hp: (b, hp)),
                  pl.BlockSpec((SEQ, w), lambda b, hp: (b, nblk + hp)),
                  pl.BlockSpec((SEQ, w), lambda b, hp: (b, 2 * nblk + hp))],
        out_specs=pl.BlockSpec((SEQ, w), lambda b, hp: (b, hp)),
        out_shape=jax.ShapeDtypeStruct((NC, D), F32),
        compiler_params=_cparams(("arbitrary", "arbitrary")),
        name="na_ctx",
    )(qkv, qkv, qkv)


def _na_lat_kernel(q_ref, k_ref, v_ref, kc_ref, vc_ref, tt_ref, o_ref):
    first = _head_masks()
    rows = DEC_SEQ // GRID_W
    kctx = kc_ref[...].astype(BF16)
    vctx = vc_ref[...].astype(BF16)
    nkeys = WIN_R * GRID_W

    def rows_body(j, carry):
        chains = []
        for t in range(NA_ROW_UNROLL):
            r = NA_ROW_UNROLL * j + t
            r0 = jnp.clip(r - WIN_R // 2, 0, rows - WIN_R)
            dr0 = r0 - r + (WIN_R - 1)
            q = q_ref[pl.ds(pl.multiple_of(r * GRID_W, GRID_W), GRID_W), :]
            kbase = pl.multiple_of(r0 * GRID_W, GRID_W)
            kl = k_ref[pl.ds(kbase, nkeys), :].astype(BF16)
            vl = v_ref[pl.ds(kbase, nkeys), :].astype(BF16)
            for a in range(2):
                qm = jnp.where(first if a == 0 else jnp.logical_not(first), q, 0.0)
                bias = jnp.concatenate([tt_ref[a, dr0 + 2 * i] for i in range(WIN_R // 2)], axis=1)
                chains.append((qm, [(kl, vl, bias), (kctx, vctx, None)]))
        outs = _attend_many(chains)
        for t in range(NA_ROW_UNROLL):
            r = NA_ROW_UNROLL * j + t
            o_ref[pl.ds(pl.multiple_of(r * GRID_W, GRID_W), GRID_W), :] = jnp.where(first, outs[2 * t], outs[2 * t + 1])
        return carry

    lax.fori_loop(0, rows // NA_ROW_UNROLL, rows_body, 0)


def _na_latent(qkv, cache_k2, cache_v2, tt2):
    rb0 = NC // DEC_SEQ
    return pl.pallas_call(
        _na_lat_kernel,
        grid=(DEC_BATCH, NA_HEADS // 2),
        in_specs=[pl.BlockSpec((DEC_SEQ, LANES), lambda b, hp: (rb0 + b, hp)),
                  pl.BlockSpec((DEC_SEQ, LANES), lambda b, hp: (rb0 + b, 8 + hp)),
                  pl.BlockSpec((DEC_SEQ, LANES), lambda b, hp: (rb0 + b, 16 + hp)),
                  pl.BlockSpec((None, 256, LANES), lambda b, hp: (b, 0, hp)),
                  pl.BlockSpec((None, 256, LANES), lambda b, hp: (b, 0, hp)),
                  pl.BlockSpec((2, 2 * WIN_R - 2, GRID_W, LANES), lambda b, hp: (hp, 0, 0, 0))],
        out_specs=pl.BlockSpec((DEC_SEQ, LANES), lambda b, hp: (b, hp)),
        out_shape=jax.ShapeDtypeStruct((NL, D), F32),
        compiler_params=_cparams(("arbitrary", "arbitrary")),
        name="na_latent",
    )(qkv, qkv, qkv, cache_k2, cache_v2, tt2)


def _rpb_tables(rpb):
    col = np.arange(GRID_W)
    start = np.clip(col - WIN_C // 2, 0, GRID_W - WIN_C)
    inside = (col[None, :] >= start[:, None]) & (col[None, :] < start[:, None] + WIN_C)
    dc = col[None, :] - col[:, None] + (WIN_C - 1)
    t = jnp.full((NA_HEADS, 2 * WIN_R - 1, GRID_W, GRID_W), NEG, F32)
    for j in range(2 * WIN_C - 1):
        t = jnp.where((inside & (dc == j))[None, None], rpb[:, :, j, None, None], t)
    return jnp.concatenate([t[:, :-1], t[:, 1:]], axis=-1)


ROUTE_ROWS = 40


def _router_kernel(x_ref, m_ref, g_ref, wr_ref, br_ref, tri_ref, xf_ref, ri_ref, rw_ref, cnt_ref, base_scr):
    i = pl.program_id(0)

    @pl.when(i == 0)
    def _():
        base_scr[...] = jnp.zeros_like(base_scr)
        cnt_ref[...] = jnp.zeros_like(cnt_ref)

    h = _modulated_norm(x_ref[...], m_ref, g_ref, 3, 4)
    xf_ref[...] = _pack_bf16_pairs(h[:, :D // 2], h[:, D // 2:])
    logits = lax.dot_general(wr_ref[...], h, (((1,), (1,)), ((), ())), preferred_element_type=F32,
                             precision=HIGHEST) + br_ref[:, 0:1]
    row = lax.broadcasted_iota(jnp.int32, logits.shape, 0)
    cmax = lambda x: jnp.max(x, axis=0, keepdims=True)
    cmin = lambda x: jnp.min(x, axis=0, keepdims=True)
    csum = lambda x: jnp.sum(x, axis=0, keepdims=True)

    gmask = row < N_GROUPS
    mg = cmax(jnp.where(gmask, logits, NEG))
    eg = jnp.where(gmask, jnp.exp(jnp.where(gmask, logits - mg, NEG)), 0.0)
    pg = eg / csum(eg)
    p_grp = cmax(pg)
    grp = cmin(jnp.where(jnp.logical_and(gmask, pg == p_grp), row, ROUTE_ROWS))
    lo = N_GROUPS + grp * EXP_PER_GROUP
    emask = jnp.logical_and(row >= lo, row < lo + EXP_PER_GROUP)
    me = cmax(jnp.where(emask, logits, NEG))
    ee = jnp.where(emask, jnp.exp(jnp.where(emask, logits - me, NEG)), 0.0)
    pe = ee / csum(ee)
    p1 = cmax(pe)
    i1 = cmin(jnp.where(jnp.logical_and(emask, pe == p1), row, ROUTE_ROWS))
    m2 = jnp.logical_and(emask, row != i1)
    p2 = cmax(jnp.where(m2, pe, -1.0))
    i2 = cmin(jnp.where(jnp.logical_and(m2, pe == p2), row, ROUTE_ROWS))
    den = p1 + p2
    w1 = p_grp * p1 / den
    w2 = p_grp * p2 / den

    sel1 = row == i1
    sel2 = row == i2
    oh = jnp.where(jnp.logical_or(sel1, sel2), 1.0, 0.0).astype(BF16)
    before = jnp.dot(oh, tri_ref[...], preferred_element_type=F32) + base_scr[:, 0:1]
    rank1 = csum(jnp.where(sel1, before, 0.0))
    rank2 = csum(jnp.where(sel2, before, 0.0))
    base_scr[...] = base_scr[...] + jnp.sum(oh.astype(F32), axis=1, keepdims=True)
    cnt_ref[...] = cnt_ref[...] + lax.dot_general(jnp.ones((8, TM), BF16), oh, (((1,), (1,)), ((), ())),
                                                  preferred_element_type=F32)
    sub = lax.broadcasted_iota(jnp.int32, (8, TM), 0)
    ri_ref[...] = jnp.where(sub == 0, i1 - N_GROUPS, jnp.where(sub == 1, i2 - N_GROUPS,
                  jnp.where(sub == 2, rank1.astype(jnp.int32), jnp.where(sub == 3, rank2.astype(jnp.int32), 0))))
    rw_ref[...] = jnp.where(sub == 0, w1, jnp.where(sub == 1, w2, 0.0))


def _router(x, m3, gain, wr_t, br_t, tri_upper):
    return pl.pallas_call(
        _router_kernel,
        grid=(N // TM,),
        in_specs=[pl.BlockSpec((TM, D), lambda i: (i, 0)),
                  pl.BlockSpec((None, 6, D), lambda i: (_cond_row(i), 0, 0)),
                  pl.BlockSpec((1, D), lambda i: (0, 0)),
                  pl.BlockSpec((ROUTE_ROWS, D), lambda i: (0, 0)),
                  pl.BlockSpec((ROUTE_ROWS, LANES), lambda i: (0, 0)),
                  pl.BlockSpec((TM, TM), lambda i: (0, 0))],
        out_specs=[pl.BlockSpec((TM, D // 2), lambda i: (i, 0)),
                   pl.BlockSpec((8, TM), lambda i: (0, i)),
                   pl.BlockSpec((8, TM), lambda i: (0, i)),
                   pl.BlockSpec((8, ROUTE_ROWS), lambda i: (0, 0))],
        out_shape=[jax.ShapeDtypeStruct((N, D // 2), jnp.int32),
                   jax.ShapeDtypeStruct((8, N), jnp.int32),
                   jax.ShapeDtypeStruct((8, N), F32),
                   jax.ShapeDtypeStruct((8, ROUTE_ROWS), F32)],
        scratch_shapes=[pltpu.VMEM((ROUTE_ROWS, LANES), F32)],
        compiler_params=_cparams(("arbitrary",)),
        name="router",
    )(x, m3, gain, wr_t, br_t, tri_upper)


def _row_copy(src_hbm, row, dst, r, sem):
    return pltpu.make_async_copy(src_hbm.at[pl.ds(row, 1)], dst.at[pl.ds(r, 1)], sem)


def _expert_kernel(be_ref, nu_ref, d_ref, xp_hbm, wg_ref, wu_ref, wd_ref, ys_ref, xres, xbuf, st_ref, sem):
    del be_ref
    b = pl.program_id(0)
    half = D // 2

    def gather(blk, slot):
        base = blk * MOE_ROWS
        for r in range(MOE_ROWS):
            xbuf[slot, pl.ds(r, 1), :] = xres[pl.ds(st_ref[base + r], 1), :]

    @pl.when(b == 0)
    def _():
        cp = pltpu.make_async_copy(xp_hbm, xres, sem.at[0])
        cp.start()

        def clear(j, c):
            for t in range(32):
                st_ref[32 * j + t] = 0
            return c

        lax.fori_loop(0, N_SLOT_BLOCKS * MOE_ROWS // 32, clear, 0)

        def invert(j, c):
            slots = [d_ref[16 * j + t] for t in range(16)]
            for t in range(16):
                st_ref[slots[t]] = 8 * j + t // 2
            return c

        lax.fori_loop(0, 2 * N // 16, invert, 0)
        cp.wait()
        gather(0, 0)

    @pl.when(b < nu_ref[0])
    def _():
        gather(jnp.minimum(b + 1, N_SLOT_BLOCKS - 1), (b + 1) % 2)
        x_lo, x_hi = _unpack_bf16_pairs(xbuf[b % 2])
        dot = functools.partial(jnp.dot, preferred_element_type=F32)
        g = dot(x_lo, wg_ref[:half, :].astype(BF16)) + dot(x_hi, wg_ref[half:, :].astype(BF16))
        u = dot(x_lo, wu_ref[:half, :].astype(BF16)) + dot(x_hi, wu_ref[half:, :].astype(BF16))
        ys_ref[...] = _mm(_silu(g) * u, wd_ref[...])

    @pl.when(b >= nu_ref[0])
    def _():
        ys_ref[...] = jnp.zeros_like(ys_ref)


def _experts(block_e, n_used, dest, xp, w_gate, w_up, w_down, layer):
    grid_spec = pltpu.PrefetchScalarGridSpec(
        num_scalar_prefetch=3,
        grid=(N_SLOT_BLOCKS,),
        in_specs=[pl.BlockSpec(memory_space=pl.ANY),
                  pl.BlockSpec((None, None, D, D_EXPERT), lambda b, be, nu, st: (layer, be[b], 0, 0)),
                  pl.BlockSpec((None, None, D, D_EXPERT), lambda b, be, nu, st: (layer, be[b], 0, 0)),
                  pl.BlockSpec((None, None, D_EXPERT, D), lambda b, be, nu, st: (layer, be[b], 0, 0))],
        out_specs=pl.BlockSpec((MOE_ROWS, D), lambda b, be, nu, st: (b, 0)),
        scratch_shapes=[pltpu.VMEM((N, D // 2), jnp.int32),
                        pltpu.VMEM((2, MOE_ROWS, D // 2), jnp.int32),
                        pltpu.SMEM((N_SLOT_BLOCKS * MOE_ROWS,), jnp.int32),
                        pltpu.SemaphoreType.DMA((1,))])
    return pl.pallas_call(
        _expert_kernel,
        grid_spec=grid_spec,
        out_shape=jax.ShapeDtypeStruct((N_SLOT_BLOCKS * MOE_ROWS, D), F32),
        compiler_params=pltpu.CompilerParams(dimension_semantics=("arbitrary",), vmem_limit_bytes=EXPERT_VMEM_LIMIT),
        name="experts",
    )(block_e, n_used, dest, xp, w_gate, w_up, w_down)


def _combine_kernel(d_ref, ys_hbm, x_ref, m_ref, rw_ref, fn_ref, o_ref, buf, sem, *, final, tile0, n_tiles):
    i = pl.program_id(0)

    def issue(tile, slot):
        base = (tile0 + tile) * (2 * TM)
        for r in range(TM):
            for kk in range(2):
                _row_copy(ys_hbm, d_ref[base + 2 * r + kk], buf.at[slot, kk], r, sem.at[slot]).start()

    @pl.when(i == 0)
    def _():
        issue(0, 0)

    @pl.when(i + 1 < n_tiles)
    def _():
        issue(i + 1, (i + 1) % 2)

    slot = i % 2
    for kk in range(2):
        pltpu.make_async_copy(ys_hbm.at[pl.ds(0, TM)], buf.at[slot, kk], sem.at[slot]).wait()
    w = rw_ref[...]
    y = w[:, 0:1] * buf[slot, 0] + w[:, 1:2] * buf[slot, 1]
    out = x_ref[...] + m_ref[5:6, :] * y
    if final:
        ms = jnp.mean(out * out, axis=-1, keepdims=True)
        out = out * lax.rsqrt(ms + EPS) * fn_ref[...]
    o_ref[...] = out


def _combine(dest_flat, ys, x, m3, rw, final_norm, final, tile0=0, n_tiles=N // TM):
    grid_spec = pltpu.PrefetchScalarGridSpec(
        num_scalar_prefetch=1,
        grid=(n_tiles,),
        in_specs=[pl.BlockSpec(memory_space=pl.ANY),
                  pl.BlockSpec((TM, D), lambda i, d: (tile0 + i, 0)),
                  pl.BlockSpec((None, 6, D), lambda i, d: (_cond_row(tile0 + i), 0, 0)),
                  pl.BlockSpec((TM, 2), lambda i, d: (tile0 + i, 0)),
                  pl.BlockSpec((1, D), lambda i, d: (0, 0))],
        out_specs=pl.BlockSpec((TM, D), lambda i, d: (i, 0)),
        scratch_shapes=[pltpu.VMEM((2, 2, TM, D), F32), pltpu.SemaphoreType.DMA((2,))])
    return pl.pallas_call(
        functools.partial(_combine_kernel, final=final, tile0=tile0, n_tiles=n_tiles),
        grid_spec=grid_spec,
        out_shape=jax.ShapeDtypeStruct((n_tiles * TM, D), F32),
        compiler_params=_cparams(("arbitrary",)),
        name="combine",
    )(dest_flat, ys, x, m3, rw, final_norm)


def _hier_moe(x, m3, gain, layer, w_rg, b_rg, w_re, b_re, w_gate, w_up, w_down, tri_tm, final_norm, final):
    pad = ROUTE_ROWS - N_GROUPS - N_EXPERTS
    wr_t = jnp.concatenate([w_rg.T, w_re.transpose(0, 2, 1).reshape(N_EXPERTS, D), jnp.zeros((pad, D), F32)], axis=0)
    br_t = jnp.broadcast_to(jnp.concatenate([b_rg, b_re.reshape(N_EXPERTS), jnp.zeros((pad,), F32)])[:, None],
                            (ROUTE_ROWS, LANES))
    xf, ri_t, rw_t, cnt = _router(x, m3, gain, wr_t, br_t, tri_tm)
    e_idx = ri_t[0:2].T
    rank = ri_t[2:4].T
    rw = rw_t[0:2].T
    counts = cnt[0, N_GROUPS:N_GROUPS + N_EXPERTS].astype(jnp.int32)
    padded = (counts + MOE_ROWS - 1) // MOE_ROWS * MOE_ROWS
    end_pad = jnp.cumsum(padded)
    start_pad = end_pad - padded
    experts = jnp.arange(N_EXPERTS, dtype=jnp.int32)
    start_of = jnp.sum(jnp.where(e_idx[:, :, None] == experts, start_pad, 0), axis=-1)
    dest = (start_of + rank).reshape(-1).astype(jnp.int32)
    block_start = jnp.arange(N_SLOT_BLOCKS, dtype=jnp.int32) * MOE_ROWS
    block_e = jnp.minimum(jnp.sum((end_pad[None, :] <= block_start[:, None]).astype(jnp.int32), axis=1),
                          N_EXPERTS - 1)
    n_used = (end_pad[-1:] // MOE_ROWS).astype(jnp.int32)
    ys = _experts(block_e, n_used, dest, xf, w_gate, w_up, w_down, layer)
    if not final:
        return _combine(dest, ys, x, m3, rw, final_norm, False)
    nct = NC // TM
    return (_combine(dest, ys, x, m3, rw, final_norm, True, 0, nct),
            _combine(dest, ys, x, m3, rw, final_norm, True, nct, N // TM - nct))


def kernel(x_prompt, x_sample, state_A_fwd, state_A_bwd, cache_k, cache_v, c, c_ctx, mod_w, mod_b, norm_mix, norm_ffn, ab_w_in, ab_conv, ab_a_log, ab_dt_bias, ab_o_gain, ab_w_out, na_w_qkv, na_rpb, na_w_out, moe_w_rg, moe_b_rg, moe_w_re, moe_b_re, moe_w_gate, moe_w_up, moe_w_down, final_norm):
    x = (x_prompt.reshape(NC, D), x_sample.reshape(NL, D))
    cond8 = jnp.concatenate([c_ctx[None, :], c, jnp.zeros((8 - 1 - DEC_BATCH, D), F32)], axis=0)
    mods = _ada_params(cond8, mod_w, mod_b).reshape(DEPTH, 8, 6, D)

    tri_tm = jnp.asarray(np.triu(np.ones((TM, TM)), 1), BF16)
    tri_c = jnp.asarray(np.tril(np.ones((CHUNK, CHUNK))), F32)
    half = np.arange(LANES) < A_DH
    jbd = jnp.asarray((half[:, None] == half[None, :]).astype(np.float32))
    fn = final_norm[None, :]

    m3 = mods[0]
    w_in = ab_w_in[0]
    w_in = jnp.concatenate([w_in[:, :2048], w_in[:, 2080:2592], w_in[:, 2048:2080],
                            jnp.zeros((D, AB_COLS - 2592), F32)], axis=1).astype(BF16)
    proj = _modproj(x, m3, norm_mix[0][None, :], w_in, 896)
    gate_p = jnp.broadcast_to(jnp.stack([ab_a_log[0].reshape(-1), ab_dt_bias[0].reshape(-1)])[:, :, None],
                              (2, 2 * A_HEADS, LANES))
    o_gain2 = jnp.tile(ab_o_gain[0], 2)[None, :]
    zeros_state = jnp.zeros((BATCH, A_HEADS, A_DH, A_DH), F32)
    mix_a_c, s_f, s_b = _deltanet(proj, ab_conv[0], gate_p, o_gain2, zeros_state, zeros_state, tri_c, jbd,
                                  seq_len=SEQ, n_seq=BATCH, n_sub=4, row_blk0=0)
    mix_a_l, _, _ = _deltanet(proj, ab_conv[0], gate_p, o_gain2, state_A_fwd[:, 0], state_A_bwd[:, 0], tri_c, jbd,
                              seq_len=DEC_SEQ, n_seq=DEC_BATCH, n_sub=1, row_blk0=NC // DEC_SEQ)

    cc, sc = _dft_mats(LANES)
    ct, st = _dft_mats(SEQ)
    c64, s64 = _dft_mats(64)
    cs = jnp.asarray(np.concatenate([cc, sc], axis=1), BF16)
    dft = jnp.asarray(np.concatenate([ct, -st], axis=1), BF16)
    ca = jnp.asarray(np.concatenate([cc, -sc, -sc, -cc], axis=1), BF16)
    m1 = jnp.asarray(np.concatenate([c64, s64], axis=1), BF16)
    tw_idx = np.arange(64)
    tw_ang = 2.0 * np.pi * (tw_idx[:, None] * tw_idx[None, :]) / DEC_SEQ
    twc = jnp.broadcast_to(jnp.asarray(np.cos(tw_ang), F32)[:, :, None], (64, 64, LANES))
    tws = jnp.broadcast_to(jnp.asarray(np.sin(tw_ang), F32)[:, :, None], (64, 64, LANES))
    mix_b_c = _fnet_ctx(proj, cs, dft)
    mix_b_l = _fnet_latent(proj, ca, m1, twc, tws)

    w_out = ab_w_out[0].astype(BF16)
    x = _outproj([(mix_a_c, mix_a_l), (mix_b_c, mix_b_l)], [w_out[:A_WIDTH], w_out[A_WIDTH:]], x, m3)
    x = _hier_moe(x, m3, norm_ffn[0][None, :], 0, moe_w_rg[0], moe_b_rg[0], moe_w_re[0], moe_b_re[0],
                  moe_w_gate, moe_w_up, moe_w_down, tri_tm, fn, False)

    m3 = mods[1]
    qkv = _modproj(x, m3, norm_mix[1][None, :], na_w_qkv[0].astype(BF16), 512)
    attn_c = _na_ctx(qkv)
    attn_l = _na_latent(qkv, cache_k[:, 0].reshape(DEC_BATCH, 256, D), cache_v[:, 0].reshape(DEC_BATCH, 256, D),
                        _rpb_tables(na_rpb[0]))
    x = _outproj([(attn_c, attn_l)], [na_w_out[0].astype(BF16)], x, m3)
    y_c, y_l = _hier_moe(x, m3, norm_ffn[1][None, :], 1, moe_w_rg[1], moe_b_rg[1], moe_w_re[1], moe_b_re[1],
                         moe_w_gate, moe_w_up, moe_w_down, tri_tm, fn, True)

    new_k = qkv[:NC, D:2 * D].reshape(BATCH, 1, SEQ, NA_HEADS, NA_DH)
    new_v = qkv[:NC, 2 * D:].reshape(BATCH, 1, SEQ, NA_HEADS, NA_DH)
    return (y_c.reshape(BATCH, SEQ, D), y_l.reshape(DEC_BATCH, DEC_SEQ, D),
            s_f[:, None], s_b[:, None], new_k, new_v)
```

```python
import functools
import math

import numpy as np
import jax
import jax.numpy as jnp
from jax import lax
from jax.experimental import pallas as pl
from jax.experimental.pallas import tpu as pltpu

F32 = jnp.float32
BF16 = jnp.bfloat16
HIGHEST = lax.Precision.HIGHEST

D = 1024
BATCH, SEQ = 32, 256
DEC_BATCH, DEC_SEQ = 2, 4096
NC = BATCH * SEQ
NL = DEC_BATCH * DEC_SEQ
N = NC + NL
DEPTH = 2
GRID_W = 64
A_DH = 64
A_HEADS = 8
A_WIDTH = 512
CHUNK = 64
B_WIDTH = 512
B_GROUPS = 4
NA_DH = 64
NA_HEADS = 16
WIN_R, WIN_C = 8, 16
N_GROUPS, EXP_PER_GROUP, N_EXPERTS = 4, 8, 32
D_EXPERT = 512
EPS = 1e-6

LANES = 128
TM = 256
MOE_ROWS = 256
N_SLOT_BLOCKS = (2 * N) // MOE_ROWS + N_EXPERTS
AB_COLS = 2688
VMEM_LIMIT = 56 * 1024 * 1024
EXPERT_VMEM_LIMIT = 60 * 1024 * 1024
NEG = -1e30


def _cparams(sem):
    return pltpu.CompilerParams(dimension_semantics=sem, vmem_limit_bytes=VMEM_LIMIT)


def _mm(a, b):
    return jnp.dot(a.astype(BF16), b.astype(BF16), preferred_element_type=F32)


def _mm_nt(a, b):
    return lax.dot_general(a.astype(BF16), b.astype(BF16), (((1,), (1,)), ((), ())),
                           preferred_element_type=F32)


SOLVE_BLK = 16
NA_CTX_PAIRS = 2
NA_ROW_UNROLL = 4
TERM_UNROLL = 4


def _unit_lower_solve(a_mat, rhs, same_blk, eye):
    dg = jnp.where(same_blk, a_mat, 0.0)
    p = -dg
    dinv = eye + p
    for _ in range(int(math.log2(SOLVE_BLK)) - 1):
        p = _mm(p, p)
        dinv = dinv + _mm(p, dinv)
    mp = -_mm(dinv, a_mat - dg)
    y = _mm(dinv, rhs)
    y = y + _mm(mp, y)
    for _ in range(int(math.log2(a_mat.shape[0] // SOLVE_BLK)) - 1):
        mp = _mm(mp, mp)
        y = y + _mm(mp, y)
    return y


def _unit_lower_solve_many(a_mats, rhs, same_blk, eye):
    off = [jnp.where(same_blk, 0.0, a).astype(BF16) for a in a_mats]
    p = [jnp.where(same_blk, -a, 0.0).astype(BF16) for a in a_mats]
    dinv = [(eye + x.astype(F32)).astype(BF16) for x in p]
    for _ in range(int(math.log2(SOLVE_BLK)) - 1):
        p = [_mm(x, x).astype(BF16) for x in p]
        dinv = [(di.astype(F32) + _mm(x, di)).astype(BF16) for x, di in zip(p, dinv)]
    mp = [(-_mm(di, o)).astype(BF16) for di, o in zip(dinv, off)]
    y = [_mm(di, r) for di, r in zip(dinv, rhs)]
    y = [yi + _mm(m, yi) for m, yi in zip(mp, y)]
    for _ in range(int(math.log2(a_mats[0].shape[0] // SOLVE_BLK)) - 1):
        mp = [_mm(m, m).astype(BF16) for m in mp]
        y = [yi + _mm(m, yi) for m, yi in zip(mp, y)]
    return y


def _mm_split(a, b, parts, split_rhs=False):
    x = b if split_rhs else a
    acc = None
    for _ in range(parts):
        piece = x.astype(BF16)
        x = x - piece.astype(F32)
        term = (jnp.dot(a.astype(BF16), piece, preferred_element_type=F32) if split_rhs
                else jnp.dot(piece, b.astype(BF16), preferred_element_type=F32))
        acc = term if acc is None else acc + term
    return acc


def _mm_hi(a, b):
    return jnp.dot(a, b, preferred_element_type=F32, precision=HIGHEST)


def _silu(x):
    return x * jax.nn.sigmoid(x)


def _bf16_bits(x):
    b = lax.bitcast_convert_type(x, jnp.int32)
    return b + 0x7FFF + (lax.shift_right_logical(b, jnp.int32(16)) & 1)


_HIGH16 = -65536


def _pack_bf16_pairs(a, b):
    return lax.shift_right_logical(_bf16_bits(a), jnp.int32(16)) | (_bf16_bits(b) & _HIGH16)


def _unpack_bf16_pairs(p):
    a = lax.bitcast_convert_type(lax.shift_left(p, jnp.int32(16)), F32)
    b = lax.bitcast_convert_type(p & _HIGH16, F32)
    return a.astype(BF16), b.astype(BF16)


def _cond_row(i):
    return jnp.where(i < NC // TM, 0, 1 + (i - NC // TM) // (DEC_SEQ // TM))


def _modulated_norm(x, m_ref, g_ref, shift_idx, scale_idx):
    ms = jnp.mean(x * x, axis=-1, keepdims=True)
    y = x * lax.rsqrt(ms + EPS) * g_ref[...]
    return y * (1.0 + m_ref[scale_idx:scale_idx + 1, :]) + m_ref[shift_idx:shift_idx + 1, :]


def _ada_kernel(cond_ref, w_ref, b_ref, o_ref):
    o_ref[...] = _mm_hi(_silu(cond_ref[...]), w_ref[...]) + b_ref[...]


def _ada_params(cond8, mod_w, mod_b):
    tn = 1536
    return pl.pallas_call(
        _ada_kernel,
        grid=(DEPTH, 6 * D // tn),
        in_specs=[pl.BlockSpec((8, D), lambda l, j: (0, 0)),
                  pl.BlockSpec((None, D, tn), lambda l, j: (l, 0, j)),
                  pl.BlockSpec((None, 1, tn), lambda l, j: (l, 0, j))],
        out_specs=pl.BlockSpec((None, 8, tn), lambda l, j: (l, 0, j)),
        out_shape=jax.ShapeDtypeStruct((DEPTH, 8, 6 * D), F32),
        compiler_params=_cparams(("arbitrary", "arbitrary")),
        name="ada_params",
    )(cond8, mod_w, mod_b.reshape(DEPTH, 1, 6 * D))


def _token_specs(x):
    nct = NC // TM
    if isinstance(x, tuple):
        return ([pl.BlockSpec((TM, x[0].shape[1]), lambda i: (jnp.minimum(i, nct - 1), 0)),
                 pl.BlockSpec((TM, x[1].shape[1]), lambda i: (jnp.maximum(i - nct, 0), 0))], list(x))
    return [pl.BlockSpec((TM, x.shape[1]), lambda i: (i, 0))], [x]


def _token_rows(refs):
    if len(refs) == 1:
        return refs[0][...]
    return jnp.where(pl.program_id(0) < NC // TM, refs[0][...], refs[1][...])


def _modproj_kernel(*refs, n_chunk):
    m_ref, g_ref, w_ref, o_ref = refs[-4:]
    hb = _modulated_norm(_token_rows(refs[:-4]), m_ref, g_ref, 0, 1).astype(BF16)
    for j in range(o_ref.shape[1] // n_chunk):
        sl = slice(j * n_chunk, (j + 1) * n_chunk)
        o_ref[:, sl] = jnp.dot(hb, w_ref[:, sl], preferred_element_type=F32)


def _modproj(x, m3, gain, w_bf16, n_chunk):
    nout = w_bf16.shape[1]
    x_specs, x_args = _token_specs(x)
    return pl.pallas_call(
        functools.partial(_modproj_kernel, n_chunk=n_chunk),
        grid=(N // TM,),
        in_specs=x_specs + [pl.BlockSpec((None, 6, D), lambda i: (_cond_row(i), 0, 0)),
                            pl.BlockSpec((1, D), lambda i: (0, 0)),
                            pl.BlockSpec((D, nout), lambda i: (0, 0))],
        out_specs=pl.BlockSpec((TM, nout), lambda i: (i, 0)),
        out_shape=jax.ShapeDtypeStruct((N, nout), F32),
        compiler_params=_cparams(("arbitrary",)),
        name="modproj",
    )(*x_args, m3, gain, w_bf16)


def _deltanet_kernel(q_ref, k_ref, v_ref, z_ref, ab_ref, cq_ref, ck_ref, cv_ref, gp_ref, og_ref,
                     s0f_ref, s0b_ref, tri_ref, jbd_ref, o_ref, sf_ref, sb_ref,
                     u_s, w_s, qd_s, at_s, kt_s, ge_s, st_s, ob, *, seq_len, n_sub):
    hp = pl.program_id(1)
    C = CHUNK
    nc = seq_len // C
    nct = n_sub * nc
    P = LANES
    lane = lax.broadcasted_iota(jnp.int32, (C, P), 1)
    row = lax.broadcasted_iota(jnp.int32, (C, P), 0)
    first_head = lane < A_DH
    ri = lax.broadcasted_iota(jnp.int32, (P, P), 0)
    ci = lax.broadcasted_iota(jnp.int32, (P, P), 1)
    same_head = (ri < C) == (ci < C)
    same_blk = (ri // SOLVE_BLK) == (ci // SOLVE_BLK)
    eye = jnp.where(ri == ci, 1.0, 0.0)
    jbd = jbd_ref[...]
    lincl = tri_ref[...]
    cum_b = [lincl.T.astype(BF16), lincl.astype(BF16)]
    incl_m = [jnp.logical_and(same_head, ri >= ci), jnp.logical_and(same_head, ri <= ci)]
    strict_m = [jnp.logical_and(same_head, ri > ci), jnp.logical_and(same_head, ri < ci)]
    neg_a = -jnp.exp(gp_ref[0])
    dt_b = gp_ref[1]

    def conv_silu(ref, w_ref, c):
        base = pl.multiple_of(c * C, C)
        cs = c % nc
        xc = ref[pl.ds(base, C), :]
        pbase = pl.multiple_of(jnp.maximum(base - 8, 0), 8)
        nbase = pl.multiple_of(jnp.minimum(base + C, n_sub * seq_len - 8), 8)
        prev_row = ref[pl.ds(pbase, 8), :][7:8, :] * jnp.where(cs > 0, 1.0, 0.0)
        next_row = ref[pl.ds(nbase, 8), :][0:1, :] * jnp.where(cs < nc - 1, 1.0, 0.0)
        x_prev = jnp.where(row == 0, prev_row, pltpu.roll(xc, 1, 0))
        x_next = jnp.where(row == C - 1, next_row, pltpu.roll(xc, C - 1, 0))
        y = w_ref[0:1, :] * x_prev + w_ref[1:2, :] * xc + w_ref[2:3, :] * x_next
        return _silu(y)

    def stack(x):
        return jnp.concatenate([jnp.where(first_head, x, 0.0), jnp.where(first_head, 0.0, x)], axis=0)

    def chunk_inputs(c):
        base = pl.multiple_of(c * C, C)
        q = conv_silu(q_ref, cq_ref, c)
        k = conv_silu(k_ref, ck_ref, c)
        v = conv_silu(v_ref, cv_ref, c)
        return q, k, v, ab_ref[pl.ds(base, C), :].T

    sub8 = lax.broadcasted_iota(jnp.int32, (A_HEADS, C), 0)

    def pair_row(x8):
        r0 = jnp.sum(jnp.where(sub8 == 2 * hp, x8, 0.0), axis=0, keepdims=True)
        r1 = jnp.sum(jnp.where(sub8 == 2 * hp + 1, x8, 0.0), axis=0, keepdims=True)
        return jnp.concatenate([r0, r1], axis=1)

    def chain_gates(ab_t, d):
        a8 = ab_t[d * A_HEADS:(d + 1) * A_HEADS, :]
        b8 = ab_t[(2 + d) * A_HEADS:(3 + d) * A_HEADS, :]
        g8 = neg_a[d * A_HEADS:(d + 1) * A_HEADS, :C] * jax.nn.softplus(a8 + dt_b[d * A_HEADS:(d + 1) * A_HEADS, :C])
        gc8 = _mm_split(g8, cum_b[d], 3)
        tot8 = jnp.broadcast_to(jnp.sum(g8, axis=-1, keepdims=True), (A_HEADS, C))
        gc_row, beta_row, tot_row = pair_row(gc8), pair_row(jax.nn.sigmoid(b8)), pair_row(tot8)
        cols = jnp.concatenate([gc_row, beta_row, tot_row, jnp.zeros((5, P), F32)], axis=0).T
        return gc_row, tot_row, cols[:, 0:1], cols[:, 1:2], cols[:, 2:3]

    def terms_body(j, carry):
        cs = [TERM_UNROLL * j + t for t in range(TERM_UNROLL)]
        ins = [chunk_inputs(c) for c in cs]
        qsq = [_mm_split(x[0] * x[0], jbd, 2) for x in ins]
        ksq = [_mm_split(x[1] * x[1], jbd, 2) for x in ins]
        qs = [x[0] * lax.rsqrt(s + EPS) * (A_DH ** -0.5) for x, s in zip(ins, qsq)]
        ks = [x[1] * lax.rsqrt(s + EPS) for x, s in zip(ins, ksq)]
        qst = [stack(x) for x in qs]
        kst = [stack(x) for x in ks]
        vst = [stack(x[2]) for x in ins]
        kst_t = [x.T for x in kst]
        kk = [_mm_nt(x, x) for x in kst]
        qk = [_mm_nt(x, y) for x, y in zip(qst, kst)]
        chains = [(t, d) for t in range(TERM_UNROLL) for d in range(2)]
        gates = [chain_gates(ins[t][3], d) for t, d in chains]
        decay, e_gc = [], []
        for (t, d), (gc_row, tot_row, gc_col, beta_col, tot_col) in zip(chains, gates):
            diff = jnp.broadcast_to(gc_col, (P, P)) - jnp.broadcast_to(gc_row, (P, P))
            decay.append(jnp.where(incl_m[d], jnp.exp(jnp.where(incl_m[d], diff, 0.0)), 0.0))
            e_gc.append(jnp.exp(gc_col))
        a_mats = [jnp.where(strict_m[d], g[3] * kk[t] * dc, 0.0) for (t, d), g, dc in zip(chains, gates, decay)]
        rhs = [vst[t] * g[3] + pltpu.roll(kst[t] * (g[3] * e), A_DH, 1)
               for (t, d), g, e in zip(chains, gates, e_gc)]
        xs = _unit_lower_solve_many(a_mats, rhs, same_blk, eye)
        for (t, d), x, g, e, dc in zip(chains, xs, gates, e_gc, decay):
            c = cs[t]
            gc_row, tot_row = g[0], g[1]
            u_s[d, c] = jnp.where(same_head, x, 0.0).astype(BF16)
            w_s[d, c] = pltpu.roll(jnp.where(same_head, 0.0, x), A_DH, 1).astype(BF16)
            qd_s[d, c] = (qst[t] * e).astype(BF16)
            at_s[d, c] = jnp.where(incl_m[d], qk[t] * dc, 0.0).astype(BF16)
            kt_s[d, c] = (kst_t[t] * jnp.exp(tot_row - gc_row)).astype(BF16)
            ge_s[d, c] = jnp.broadcast_to(jnp.exp(tot_row), (8, P))
        return carry

    lax.fori_loop(0, nct // TERM_UNROLL, terms_body, 0)

    def block_diag(s2):
        z = jnp.zeros((A_DH, A_DH), F32)
        return jnp.concatenate([jnp.concatenate([s2[0], z], axis=1),
                                jnp.concatenate([z, s2[1]], axis=1)], axis=0)

    for s in range(n_sub):
        st_s[2 * s] = block_diag(s0f_ref[s])
        st_s[2 * s + 1] = block_diag(s0b_ref[s])

    def scan_body(i, carry):
        chains = [(s, d, s * nc + (i if d == 0 else nc - 1 - i)) for s in range(n_sub) for d in range(2)]
        dot = functools.partial(jnp.dot, preferred_element_type=F32)
        s_bd = [st_s[2 * s + d] for s, d, c in chains]
        sb16 = [x.astype(BF16) for x in s_bd]
        ws = [dot(w_s[d, c], sb) for (s, d, c), sb in zip(chains, sb16)]
        qs_ = [dot(qd_s[d, c], sb) for (s, d, c), sb in zip(chains, sb16)]
        vb = [(u_s[d, c].astype(F32) - x).astype(BF16) for (s, d, c), x in zip(chains, ws)]
        av = [dot(at_s[d, c], x) for (s, d, c), x in zip(chains, vb)]
        kv = [dot(kt_s[d, c], x) for (s, d, c), x in zip(chains, vb)]
        for (s, d, c), sb, q_, a_, k_ in zip(chains, s_bd, qs_, av, kv):
            st_s[2 * s + d] = sb * ge_s[d, c][0:1, :] + k_
            o_st = q_ + a_
            dst = o_ref if d == 0 else ob
            dst[pl.ds(pl.multiple_of(c * C, C), C), :] = o_st[:C] + o_st[C:]
        return carry

    lax.fori_loop(0, nc, scan_body, 0)

    for s in range(n_sub):
        for d, ref in ((0, sf_ref), (1, sb_ref)):
            s_bd = st_s[2 * s + d]
            ref[s, 0] = s_bd[:A_DH, :A_DH]
            ref[s, 1] = s_bd[A_DH:, A_DH:]

    def finish(j, carry):
        bases = [pl.multiple_of((TERM_UNROLL * j + t) * C, C) for t in range(TERM_UNROLL)]
        o = [o_ref[pl.ds(b, C), :] + ob[pl.ds(b, C), :] for b in bases]
        ms = [_mm_split(x * x, jbd, 2) * (1.0 / A_DH) for x in o]
        for b, x, m in zip(bases, o, ms):
            o_ref[pl.ds(b, C), :] = x * lax.rsqrt(m + EPS) * og_ref[...] * _silu(z_ref[pl.ds(b, C), :])
        return carry

    lax.fori_loop(0, nct // TERM_UNROLL, finish, 0)


def _deltanet(proj, conv_w, gate_p, o_gain2, s0f, s0b, tri, jbd, *, seq_len, n_seq, n_sub, row_blk0):
    rows = n_sub * seq_len
    nct = rows // CHUNK
    rb = lambda b: row_blk0 + b
    col = lambda off: (lambda b, hp: (rb(b), off + hp))
    st_spec = pl.BlockSpec((n_sub, 2, A_DH, A_DH), lambda b, hp: (b, hp, 0, 0))
    in_specs = [pl.BlockSpec((rows, LANES), col(0)),
                pl.BlockSpec((rows, LANES), col(4)),
                pl.BlockSpec((rows, LANES), col(8)),
                pl.BlockSpec((rows, LANES), col(12)),
                pl.BlockSpec((rows, LANES), lambda b, hp: (rb(b), 20)),
                pl.BlockSpec((3, LANES), lambda b, hp: (0, hp)),
                pl.BlockSpec((3, LANES), lambda b, hp: (0, 4 + hp)),
                pl.BlockSpec((3, LANES), lambda b, hp: (0, 8 + hp)),
                pl.BlockSpec((2, 2 * A_HEADS, LANES), lambda b, hp: (0, 0, 0)),
                pl.BlockSpec((1, LANES), lambda b, hp: (0, 0)),
                st_spec, st_spec,
                pl.BlockSpec((CHUNK, CHUNK), lambda b, hp: (0, 0)),
                pl.BlockSpec((LANES, LANES), lambda b, hp: (0, 0))]
    args = [proj, proj, proj, proj, proj, conv_w, conv_w, conv_w, gate_p, o_gain2, s0f, s0b, tri, jbd]
    st_shape = jax.ShapeDtypeStruct((n_seq, A_HEADS, A_DH, A_DH), F32)
    tile = lambda dt: pltpu.VMEM((2, nct, LANES, LANES), dt)
    return pl.pallas_call(
        functools.partial(_deltanet_kernel, seq_len=seq_len, n_sub=n_sub),
        grid=(n_seq // n_sub, A_HEADS // 2),
        in_specs=in_specs,
        out_specs=[pl.BlockSpec((rows, LANES), lambda b, hp: (b, hp)), st_spec, st_spec],
        out_shape=[jax.ShapeDtypeStruct((n_seq * seq_len, A_WIDTH), F32), st_shape, st_shape],
        scratch_shapes=[tile(BF16), tile(BF16), tile(BF16), tile(BF16), tile(BF16),
                        pltpu.VMEM((2, nct, 8, LANES), F32),
                        pltpu.VMEM((2 * n_sub, LANES, LANES), F32),
                        pltpu.VMEM((rows, LANES), F32)],
        compiler_params=_cparams(("arbitrary", "arbitrary")),
        name="deltanet",
    )(*args)


def _dft_mats(n):
    idx = np.arange(n)
    ang = 2.0 * np.pi * ((idx[:, None] * idx[None, :]) % n) / n
    return np.cos(ang), np.sin(ang)


def _fnet_ctx_kernel(u_ref, cs_ref, dft_ref, o_ref):
    norm = 1.0 / math.sqrt(SEQ * LANES)
    for g in range(B_GROUPS):
        sl = slice(g * LANES, (g + 1) * LANES)
        p = _mm(u_ref[:, sl], cs_ref[...])
        stack = jnp.concatenate([p[:, :LANES], p[:, LANES:]], axis=0)
        o_ref[:, sl] = _mm(dft_ref[...], stack) * norm


def _fnet_ctx(proj, cs, dft):
    return pl.pallas_call(
        _fnet_ctx_kernel,
        grid=(BATCH,),
        in_specs=[pl.BlockSpec((SEQ, B_WIDTH), lambda b: (b, 4)),
                  pl.BlockSpec(cs.shape, lambda b: (0, 0)),
                  pl.BlockSpec(dft.shape, lambda b: (0, 0))],
        out_specs=pl.BlockSpec((SEQ, B_WIDTH), lambda b: (b, 0)),
        out_shape=jax.ShapeDtypeStruct((NC, B_WIDTH), F32),
        compiler_params=_cparams(("arbitrary",)),
        name="fnet_ctx",
    )(proj, cs, dft)


FN_SUB = 4


def _fnet_lat1_kernel(u_ref, ca_ref, m1_ref, twc_ref, tws_ref, o_ref):
    r = 64
    for j in range(FN_SUB):
        c = twc_ref[j]
        s = tws_ref[j]
        rows = slice(j * r, (j + 1) * r)
        for g in range(B_GROUPS):
            pa = _mm(u_ref[rows, g * LANES:(g + 1) * LANES], ca_ref[...])
            rhs = jnp.concatenate([pa[:, :2 * LANES], pa[:, 2 * LANES:]], axis=0)
            zz = _mm(m1_ref[...], rhs)
            zr, zi = zz[:, :LANES], zz[:, LANES:]
            o_ref[rows, 2 * g * LANES:(2 * g + 1) * LANES] = (zr * c + zi * s).astype(BF16)
            o_ref[rows, (2 * g + 1) * LANES:(2 * g + 2) * LANES] = (zi * c - zr * s).astype(BF16)


def _fnet_lat2_kernel(z_ref, m1_ref, o_ref):
    r = 64
    norm = 1.0 / math.sqrt(DEC_SEQ * LANES)
    for j in range(FN_SUB):
        rows = slice(j * r, (j + 1) * r)
        for g in range(B_GROUPS):
            rhs = jnp.concatenate([z_ref[rows, 2 * g * LANES:(2 * g + 1) * LANES],
                                   z_ref[rows, (2 * g + 1) * LANES:(2 * g + 2) * LANES]], axis=0)
            o_ref[rows, g * LANES:(g + 1) * LANES] = _mm(m1_ref[...], rhs) * norm


def _fnet_latent(proj, ca, m1, twc, tws):
    r = 64
    u = proj[NC:, 2048:2560].astype(BF16).reshape(DEC_BATCH, r, r, B_WIDTH)
    u = u.transpose(0, 2, 1, 3).reshape(DEC_BATCH * r * r, B_WIDTH)
    steps = DEC_BATCH * r // FN_SUB
    blk = FN_SUB * r
    z = pl.pallas_call(
        _fnet_lat1_kernel,
        grid=(steps,),
        in_specs=[pl.BlockSpec((blk, B_WIDTH), lambda s: (s, 0)),
                  pl.BlockSpec(ca.shape, lambda s: (0, 0)),
                  pl.BlockSpec(m1.shape, lambda s: (0, 0)),
                  pl.BlockSpec((FN_SUB, r, LANES), lambda s: (s % (r // FN_SUB), 0, 0)),
                  pl.BlockSpec((FN_SUB, r, LANES), lambda s: (s % (r // FN_SUB), 0, 0))],
        out_specs=pl.BlockSpec((blk, 2 * B_WIDTH), lambda s: (s, 0)),
        out_shape=jax.ShapeDtypeStruct((NL, 2 * B_WIDTH), BF16),
        compiler_params=_cparams(("arbitrary",)),
        name="fnet_lat1",
    )(u, ca, m1, twc, tws)
    z = z.reshape(DEC_BATCH, r, r, 2 * B_WIDTH).transpose(0, 2, 1, 3).reshape(NL, 2 * B_WIDTH)
    y = pl.pallas_call(
        _fnet_lat2_kernel,
        grid=(steps,),
        in_specs=[pl.BlockSpec((blk, 2 * B_WIDTH), lambda s: (s, 0)),
                  pl.BlockSpec(m1.shape, lambda s: (0, 0))],
        out_specs=pl.BlockSpec((blk, B_WIDTH), lambda s: (s, 0)),
        out_shape=jax.ShapeDtypeStruct((NL, B_WIDTH), F32),
        compiler_params=_cparams(("arbitrary",)),
        name="fnet_lat2",
    )(z, m1)
    return y.reshape(DEC_BATCH, r, r, B_WIDTH).transpose(0, 2, 1, 3).reshape(NL, B_WIDTH)


def _head_masks():
    lane = lax.broadcasted_iota(jnp.int32, (1, LANES), 1)
    return lane < NA_DH


def _attend_many(chains):
    scale = NA_DH ** -0.5
    s = [[_mm_nt(q, k) * scale if b is None else _mm_nt(q, k) * scale + b for k, v, b in kv] for q, kv in chains]
    m = [functools.reduce(jnp.maximum, [jnp.max(x, axis=-1, keepdims=True) for x in xs]) for xs in s]
    p = [[jnp.exp(x - mi) for x in xs] for xs, mi in zip(s, m)]
    l = [sum(jnp.sum(x, axis=-1, keepdims=True) for x in xs) for xs in p]
    o = [sum(_mm(x, v) for x, (k, v, b) in zip(xs, kv)) for xs, (q, kv) in zip(p, chains)]
    return [oi / li for oi, li in zip(o, l)]


def _na_ctx_kernel(q_ref, k_ref, v_ref, o_ref):
    first = _head_masks()
    n_split = 2
    rows = SEQ // n_split
    chains = []
    for p in range(NA_CTX_PAIRS):
        lanes = slice(p * LANES, (p + 1) * LANES)
        k = k_ref[:, lanes].astype(BF16)
        v = v_ref[:, lanes].astype(BF16)
        for j in range(n_split):
            q = q_ref[j * rows:(j + 1) * rows, lanes]
            for a in range(2):
                chains.append((jnp.where(first if a == 0 else jnp.logical_not(first), q, 0.0), [(k, v, None)]))
    outs = _attend_many(chains)
    for p in range(NA_CTX_PAIRS):
        for j in range(n_split):
            i = 2 * (p * n_split + j)
            o_ref[j * rows:(j + 1) * rows, p * LANES:(p + 1) * LANES] = jnp.where(first, outs[i], outs[i + 1])


def _na_ctx(qkv):
    w = NA_CTX_PAIRS * LANES
    nblk = D // w
    return pl.pallas_call(
        _na_ctx_kernel,
        grid=(BATCH, nblk),
        in_specs=[pl.BlockSpec((SEQ, w), lambda b, hp: (b, hp)),
                  pl.BlockSpec((SEQ, w), lambda b, hp: (b, nblk + hp)),
                  pl.BlockSpec((SEQ, w), lambda b, hp: (b, 2 * nblk + hp))],
        out_specs=pl.BlockSpec((SEQ, w), lambda b, hp: (b, hp)),
        out_shape=jax.ShapeDtypeStruct((NC, D), F32),
        compiler_params=_cparams(("arbitrary", "arbitrary")),
        name="na_ctx",
    )(qkv, qkv, qkv)


def _na_lat_kernel(q_ref, k_ref, v_ref, kc_ref, vc_ref, tt_ref, o_ref):
    first = _head_masks()
    rows = DEC_SEQ // GRID_W
    kctx = kc_ref[...].astype(BF16)
    vctx = vc_ref[...].astype(BF16)
    nkeys = WIN_R * GRID_W

    def rows_body(j, carry):
        chains = []
        for t in range(NA_ROW_UNROLL):
            r = NA_ROW_UNROLL * j + t
            r0 = jnp.clip(r - WIN_R // 2, 0, rows - WIN_R)
            dr0 = r0 - r + (WIN_R - 1)
            q = q_ref[pl.ds(pl.multiple_of(r * GRID_W, GRID_W), GRID_W), :]
            kbase = pl.multiple_of(r0 * GRID_W, GRID_W)
            kl = k_ref[pl.ds(kbase, nkeys), :].astype(BF16)
            vl = v_ref[pl.ds(kbase, nkeys), :].astype(BF16)
            for a in range(2):
                qm = jnp.where(first if a == 0 else jnp.logical_not(first), q, 0.0)
                bias = jnp.concatenate([tt_ref[a, dr0 + 2 * i] for i in range(WIN_R // 2)], axis=1)
                chains.append((qm, [(kl, vl, bias), (kctx, vctx, None)]))
        outs = _attend_many(chains)
        for t in range(NA_ROW_UNROLL):
            r = NA_ROW_UNROLL * j + t
            o_ref[pl.ds(pl.multiple_of(r * GRID_W, GRID_W), GRID_W), :] = jnp.where(first, outs[2 * t], outs[2 * t + 1])
        return carry

    lax.fori_loop(0, rows // NA_ROW_UNROLL, rows_body, 0)


def _na_latent(qkv, cache_k2, cache_v2, tt2):
    rb0 = NC // DEC_SEQ
    return pl.pallas_call(
        _na_lat_kernel,
        grid=(DEC_BATCH, NA_HEADS // 2),
        in_specs=[pl.BlockSpec((DEC_SEQ, LANES), lambda b, hp: (rb0 + b, hp)),
                  pl.BlockSpec((DEC_SEQ, LANES), lambda b, hp: (rb0 + b, 8 + hp)),
                  pl.BlockSpec((DEC_SEQ, LANES), lambda b, hp: (rb0 + b, 16 + hp)),
                  pl.BlockSpec((None, 256, LANES), lambda b, hp: (b, 0, hp)),
                  pl.BlockSpec((None, 256, LANES), lambda b, hp: (b, 0, hp)),
                  pl.BlockSpec((2, 2 * WIN_R - 2, GRID_W, LANES), lambda b, hp: (hp, 0, 0, 0))],
        out_specs=pl.BlockSpec((DEC_SEQ, LANES), lambda b, hp: (b, hp)),
        out_shape=jax.ShapeDtypeStruct((NL, D), F32),
        compiler_params=_cparams(("arbitrary", "arbitrary")),
        name="na_latent",
    )(qkv, qkv, qkv, cache_k2, cache_v2, tt2)


def _rpb_tables(rpb):
    col = np.arange(GRID_W)
    start = np.clip(col - WIN_C // 2, 0, GRID_W - WIN_C)
    inside = (col[None, :] >= start[:, None]) & (col[None, :] < start[:, None] + WIN_C)
    dc = col[None, :] - col[:, None] + (WIN_C - 1)
    t = jnp.full((NA_HEADS, 2 * WIN_R - 1, GRID_W, GRID_W), NEG, F32)
    for j in range(2 * WIN_C - 1):
        t = jnp.where((inside & (dc == j))[None, None], rpb[:, :, j, None, None], t)
    return jnp.concatenate([t[:, :-1], t[:, 1:]], axis=-1)


ROUTE_ROWS = 40


def _router_kernel(*refs, n_in):
    a_refs = refs[:2 * n_in]
    w_refs = refs[2 * n_in:3 * n_in]
    x_refs = refs[3 * n_in:-11]
    m_ref, g_ref, wr_ref, br_ref, tri_ref, xn_ref, xf_ref, ri_ref, rw_ref, cnt_ref, base_scr = refs[-11:]
    i = pl.program_id(0)

    @pl.when(i == 0)
    def _():
        base_scr[...] = jnp.zeros_like(base_scr)
        cnt_ref[...] = jnp.zeros_like(cnt_ref)

    acc = None
    for j, w_ref in enumerate(w_refs):
        part = _mm(_token_rows(a_refs[2 * j:2 * j + 2]), w_ref[...])
        acc = part if acc is None else acc + part
    x_new = _token_rows(x_refs) + m_ref[2:3, :] * acc
    xn_ref[...] = x_new
    h = _modulated_norm(x_new, m_ref, g_ref, 3, 4)
    xf_ref[...] = _pack_bf16_pairs(h[:, :D // 2], h[:, D // 2:])
    logits = lax.dot_general(wr_ref[...], h, (((1,), (1,)), ((), ())), preferred_element_type=F32,
                             precision=HIGHEST) + br_ref[:, 0:1]
    row = lax.broadcasted_iota(jnp.int32, logits.shape, 0)
    cmax = lambda x: jnp.max(x, axis=0, keepdims=True)
    cmin = lambda x: jnp.min(x, axis=0, keepdims=True)
    csum = lambda x: jnp.sum(x, axis=0, keepdims=True)

    gmask = row < N_GROUPS
    mg = cmax(jnp.where(gmask, logits, NEG))
    eg = jnp.where(gmask, jnp.exp(jnp.where(gmask, logits - mg, NEG)), 0.0)
    pg = eg / csum(eg)
    p_grp = cmax(pg)
    grp = cmin(jnp.where(jnp.logical_and(gmask, pg == p_grp), row, ROUTE_ROWS))
    lo = N_GROUPS + grp * EXP_PER_GROUP
    emask = jnp.logical_and(row >= lo, row < lo + EXP_PER_GROUP)
    me = cmax(jnp.where(emask, logits, NEG))
    ee = jnp.where(emask, jnp.exp(jnp.where(emask, logits - me, NEG)), 0.0)
    pe = ee / csum(ee)
    p1 = cmax(pe)
    i1 = cmin(jnp.where(jnp.logical_and(emask, pe == p1), row, ROUTE_ROWS))
    m2 = jnp.logical_and(emask, row != i1)
    p2 = cmax(jnp.where(m2, pe, -1.0))
    i2 = cmin(jnp.where(jnp.logical_and(m2, pe == p2), row, ROUTE_ROWS))
    den = p1 + p2
    w1 = p_grp * p1 / den
    w2 = p_grp * p2 / den

    sel1 = row == i1
    sel2 = row == i2
    oh = jnp.where(jnp.logical_or(sel1, sel2), 1.0, 0.0).astype(BF16)
    before = jnp.dot(oh, tri_ref[...], preferred_element_type=F32) + base_scr[:, 0:1]
    rank1 = csum(jnp.where(sel1, before, 0.0))
    rank2 = csum(jnp.where(sel2, before, 0.0))
    base_scr[...] = base_scr[...] + jnp.sum(oh.astype(F32), axis=1, keepdims=True)
    cnt_ref[...] = cnt_ref[...] + lax.dot_general(jnp.ones((8, TM), BF16), oh, (((1,), (1,)), ((), ())),
                                                  preferred_element_type=F32)
    sub = lax.broadcasted_iota(jnp.int32, (8, TM), 0)
    ri_ref[...] = jnp.where(sub == 0, i1 - N_GROUPS, jnp.where(sub == 1, i2 - N_GROUPS,
                  jnp.where(sub == 2, rank1.astype(jnp.int32), jnp.where(sub == 3, rank2.astype(jnp.int32), 0))))
    rw_ref[...] = jnp.where(sub == 0, w1, jnp.where(sub == 1, w2, 0.0))


def _router(a_pairs, w_list, x, m3, gain, wr_t, br_t, tri_upper):
    in_specs, args = [], []
    for pair in a_pairs:
        specs, ops = _token_specs(pair)
        in_specs += specs
        args += ops
    x_specs, x_args = _token_specs(x)
    in_specs += ([pl.BlockSpec(w.shape, lambda i: (0, 0)) for w in w_list] + x_specs
                 + [pl.BlockSpec((None, 6, D), lambda i: (_cond_row(i), 0, 0)),
                    pl.BlockSpec((1, D), lambda i: (0, 0)),
                    pl.BlockSpec((ROUTE_ROWS, D), lambda i: (0, 0)),
                    pl.BlockSpec((ROUTE_ROWS, LANES), lambda i: (0, 0)),
                    pl.BlockSpec((TM, TM), lambda i: (0, 0))])
    return pl.pallas_call(
        functools.partial(_router_kernel, n_in=len(a_pairs)),
        grid=(N // TM,),
        in_specs=in_specs,
        out_specs=[pl.BlockSpec((TM, D), lambda i: (i, 0)),
                   pl.BlockSpec((TM, D // 2), lambda i: (i, 0)),
                   pl.BlockSpec((8, TM), lambda i: (0, i)),
                   pl.BlockSpec((8, TM), lambda i: (0, i)),
                   pl.BlockSpec((8, ROUTE_ROWS), lambda i: (0, 0))],
        out_shape=[jax.ShapeDtypeStruct((N, D), F32),
                   jax.ShapeDtypeStruct((N, D // 2), jnp.int32),
                   jax.ShapeDtypeStruct((8, N), jnp.int32),
                   jax.ShapeDtypeStruct((8, N), F32),
                   jax.ShapeDtypeStruct((8, ROUTE_ROWS), F32)],
        scratch_shapes=[pltpu.VMEM((ROUTE_ROWS, LANES), F32)],
        compiler_params=_cparams(("arbitrary",)),
        name="router",
    )(*args, *w_list, *x_args, m3, gain, wr_t, br_t, tri_upper)


def _row_copy(src_hbm, row, dst, r, sem):
    return pltpu.make_async_copy(src_hbm.at[pl.ds(row, 1)], dst.at[pl.ds(r, 1)], sem)


def _expert_kernel(be_ref, nu_ref, d_ref, xp_hbm, wg_ref, wu_ref, wd_ref, ys_ref, xres, xbuf, st_ref, sem):
    del be_ref
    b = pl.program_id(0)
    half = D // 2

    def gather(blk, slot):
        base = blk * MOE_ROWS
        for r in range(MOE_ROWS):
            xbuf[slot, pl.ds(r, 1), :] = xres[pl.ds(st_ref[base + r], 1), :]

    @pl.when(b == 0)
    def _():
        cp = pltpu.make_async_copy(xp_hbm, xres, sem.at[0])
        cp.start()

        def clear(j, c):
            for t in range(32):
                st_ref[32 * j + t] = 0
            return c

        lax.fori_loop(0, N_SLOT_BLOCKS * MOE_ROWS // 32, clear, 0)

        def invert(j, c):
            slots = [d_ref[16 * j + t] for t in range(16)]
            for t in range(16):
                st_ref[slots[t]] = 8 * j + t // 2
            return c

        lax.fori_loop(0, 2 * N // 16, invert, 0)
        cp.wait()
        gather(0, 0)

    @pl.when(b < nu_ref[0])
    def _():
        gather(jnp.minimum(b + 1, N_SLOT_BLOCKS - 1), (b + 1) % 2)
        x_lo, x_hi = _unpack_bf16_pairs(xbuf[b % 2])
        dot = functools.partial(jnp.dot, preferred_element_type=F32)
        g = dot(x_lo, wg_ref[:half, :].astype(BF16)) + dot(x_hi, wg_ref[half:, :].astype(BF16))
        u = dot(x_lo, wu_ref[:half, :].astype(BF16)) + dot(x_hi, wu_ref[half:, :].astype(BF16))
        ys_ref[...] = _mm(_silu(g) * u, wd_ref[...])

    @pl.when(b >= nu_ref[0])
    def _():
        ys_ref[...] = jnp.zeros_like(ys_ref)


def _experts(block_e, n_used, dest, xp, w_gate, w_up, w_down, layer):
    grid_spec = pltpu.PrefetchScalarGridSpec(
        num_scalar_prefetch=3,
        grid=(N_SLOT_BLOCKS,),
        in_specs=[pl.BlockSpec(memory_space=pl.ANY),
                  pl.BlockSpec((None, None, D, D_EXPERT), lambda b, be, nu, st: (layer, be[b], 0, 0)),
                  pl.BlockSpec((None, None, D, D_EXPERT), lambda b, be, nu, st: (layer, be[b], 0, 0)),
                  pl.BlockSpec((None, None, D_EXPERT, D), lambda b, be, nu, st: (layer, be[b], 0, 0))],
        out_specs=pl.BlockSpec((MOE_ROWS, D), lambda b, be, nu, st: (b, 0)),
        scratch_shapes=[pltpu.VMEM((N, D // 2), jnp.int32),
                        pltpu.VMEM((2, MOE_ROWS, D // 2), jnp.int32),
                        pltpu.SMEM((N_SLOT_BLOCKS * MOE_ROWS,), jnp.int32),
                        pltpu.SemaphoreType.DMA((1,))])
    return pl.pallas_call(
        _expert_kernel,
        grid_spec=grid_spec,
        out_shape=jax.ShapeDtypeStruct((N_SLOT_BLOCKS * MOE_ROWS, D), F32),
        compiler_params=pltpu.CompilerParams(dimension_semantics=("arbitrary",), vmem_limit_bytes=EXPERT_VMEM_LIMIT),
        name="experts",
    )(block_e, n_used, dest, xp, w_gate, w_up, w_down)


def _combine_kernel(d_ref, ys_hbm, x_ref, m_ref, rw_ref, fn_ref, o_ref, buf, sem, *, final, tile0, n_tiles):
    i = pl.program_id(0)

    def issue(tile, slot):
        base = (tile0 + tile) * (2 * TM)
        for r in range(TM):
            for kk in range(2):
                _row_copy(ys_hbm, d_ref[base + 2 * r + kk], buf.at[slot, kk], r, sem.at[slot]).start()

    @pl.when(i == 0)
    def _():
        issue(0, 0)

    @pl.when(i + 1 < n_tiles)
    def _():
        issue(i + 1, (i + 1) % 2)

    slot = i % 2
    for kk in range(2):
        pltpu.make_async_copy(ys_hbm.at[pl.ds(0, TM)], buf.at[slot, kk], sem.at[slot]).wait()
    w = rw_ref[...]
    y = w[:, 0:1] * buf[slot, 0] + w[:, 1:2] * buf[slot, 1]
    out = x_ref[...] + m_ref[5:6, :] * y
    if final:
        ms = jnp.mean(out * out, axis=-1, keepdims=True)
        out = out * lax.rsqrt(ms + EPS) * fn_ref[...]
    o_ref[...] = out


def _combine(dest_flat, ys, x, m3, rw, final_norm, final, tile0=0, n_tiles=N // TM):
    grid_spec = pltpu.PrefetchScalarGridSpec(
        num_scalar_prefetch=1,
        grid=(n_tiles,),
        in_specs=[pl.BlockSpec(memory_space=pl.ANY),
                  pl.BlockSpec((TM, D), lambda i, d: (tile0 + i, 0)),
                  pl.BlockSpec((None, 6, D), lambda i, d: (_cond_row(tile0 + i), 0, 0)),
                  pl.BlockSpec((TM, 2), lambda i, d: (tile0 + i, 0)),
                  pl.BlockSpec((1, D), lambda i, d: (0, 0))],
        out_specs=pl.BlockSpec((TM, D), lambda i, d: (i, 0)),
        scratch_shapes=[pltpu.VMEM((2, 2, TM, D), F32), pltpu.SemaphoreType.DMA((2,))])
    return pl.pallas_call(
        functools.partial(_combine_kernel, final=final, tile0=tile0, n_tiles=n_tiles),
        grid_spec=grid_spec,
        out_shape=jax.ShapeDtypeStruct((n_tiles * TM, D), F32),
        compiler_params=_cparams(("arbitrary",)),
        name="combine",
    )(dest_flat, ys, x, m3, rw, final_norm)


def _hier_moe(a_pairs, w_list, x, m3, gain, layer, w_rg, b_rg, w_re, b_re, w_gate, w_up, w_down, tri_tm,
              final_norm, final):
    pad = ROUTE_ROWS - N_GROUPS - N_EXPERTS
    wr_t = jnp.concatenate([w_rg.T, w_re.transpose(0, 2, 1).reshape(N_EXPERTS, D), jnp.zeros((pad, D), F32)], axis=0)
    br_t = jnp.broadcast_to(jnp.concatenate([b_rg, b_re.reshape(N_EXPERTS), jnp.zeros((pad,), F32)])[:, None],
                            (ROUTE_ROWS, LANES))
    x, xf, ri_t, rw_t, cnt = _router(a_pairs, w_list, x, m3, gain, wr_t, br_t, tri_tm)
    e_idx = ri_t[0:2].T
    rank = ri_t[2:4].T
    rw = rw_t[0:2].T
    counts = cnt[0, N_GROUPS:N_GROUPS + N_EXPERTS].astype(jnp.int32)
    padded = (counts + MOE_ROWS - 1) // MOE_ROWS * MOE_ROWS
    end_pad = jnp.cumsum(padded)
    start_pad = end_pad - padded
    experts = jnp.arange(N_EXPERTS, dtype=jnp.int32)
    start_of = jnp.sum(jnp.where(e_idx[:, :, None] == experts, start_pad, 0), axis=-1)
    dest = (start_of + rank).reshape(-1).astype(jnp.int32)
    block_start = jnp.arange(N_SLOT_BLOCKS, dtype=jnp.int32) * MOE_ROWS
    block_e = jnp.minimum(jnp.sum((end_pad[None, :] <= block_start[:, None]).astype(jnp.int32), axis=1),
                          N_EXPERTS - 1)
    n_used = (end_pad[-1:] // MOE_ROWS).astype(jnp.int32)
    ys = _experts(block_e, n_used, dest, xf, w_gate, w_up, w_down, layer)
    if not final:
        return _combine(dest, ys, x, m3, rw, final_norm, False)
    nct = NC // TM
    return (_combine(dest, ys, x, m3, rw, final_norm, True, 0, nct),
            _combine(dest, ys, x, m3, rw, final_norm, True, nct, N // TM - nct))


def kernel(x_prompt, x_sample, state_A_fwd, state_A_bwd, cache_k, cache_v, c, c_ctx, mod_w, mod_b, norm_mix, norm_ffn, ab_w_in, ab_conv, ab_a_log, ab_dt_bias, ab_o_gain, ab_w_out, na_w_qkv, na_rpb, na_w_out, moe_w_rg, moe_b_rg, moe_w_re, moe_b_re, moe_w_gate, moe_w_up, moe_w_down, final_norm):
    x = (x_prompt.reshape(NC, D), x_sample.reshape(NL, D))
    cond8 = jnp.concatenate([c_ctx[None, :], c, jnp.zeros((8 - 1 - DEC_BATCH, D), F32)], axis=0)
    mods = _ada_params(cond8, mod_w, mod_b).reshape(DEPTH, 8, 6, D)

    tri_tm = jnp.asarray(np.triu(np.ones((TM, TM)), 1), BF16)
    tri_c = jnp.asarray(np.tril(np.ones((CHUNK, CHUNK))), F32)
    half = np.arange(LANES) < A_DH
    jbd = jnp.asarray((half[:, None] == half[None, :]).astype(np.float32))
    fn = final_norm[None, :]

    m3 = mods[0]
    w_in = ab_w_in[0]
    w_in = jnp.concatenate([w_in[:, :2048], w_in[:, 2080:2592], w_in[:, 2048:2080],
                            jnp.zeros((D, AB_COLS - 2592), F32)], axis=1).astype(BF16)
    proj = _modproj(x, m3, norm_mix[0][None, :], w_in, 896)
    gate_p = jnp.broadcast_to(jnp.stack([ab_a_log[0].reshape(-1), ab_dt_bias[0].reshape(-1)])[:, :, None],
                              (2, 2 * A_HEADS, LANES))
    o_gain2 = jnp.tile(ab_o_gain[0], 2)[None, :]
    zeros_state = jnp.zeros((BATCH, A_HEADS, A_DH, A_DH), F32)
    mix_a_c, s_f, s_b = _deltanet(proj, ab_conv[0], gate_p, o_gain2, zeros_state, zeros_state, tri_c, jbd,
                                  seq_len=SEQ, n_seq=BATCH, n_sub=4, row_blk0=0)
    mix_a_l, _, _ = _deltanet(proj, ab_conv[0], gate_p, o_gain2, state_A_fwd[:, 0], state_A_bwd[:, 0], tri_c, jbd,
                              seq_len=DEC_SEQ, n_seq=DEC_BATCH, n_sub=1, row_blk0=NC // DEC_SEQ)

    cc, sc = _dft_mats(LANES)
    ct, st = _dft_mats(SEQ)
    c64, s64 = _dft_mats(64)
    cs = jnp.asarray(np.concatenate([cc, sc], axis=1), BF16)
    dft = jnp.asarray(np.concatenate([ct, -st], axis=1), BF16)
    ca = jnp.asarray(np.concatenate([cc, -sc, -sc, -cc], axis=1), BF16)
    m1 = jnp.asarray(np.concatenate([c64, s64], axis=1), BF16)
    tw_idx = np.arange(64)
    tw_ang = 2.0 * np.pi * (tw_idx[:, None] * tw_idx[None, :]) / DEC_SEQ
    twc = jnp.broadcast_to(jnp.asarray(np.cos(tw_ang), F32)[:, :, None], (64, 64, LANES))
    tws = jnp.broadcast_to(jnp.asarray(np.sin(tw_ang), F32)[:, :, None], (64, 64, LANES))
    mix_b_c = _fnet_ctx(proj, cs, dft)
    mix_b_l = _fnet_latent(proj, ca, m1, twc, tws)

    w_out = ab_w_out[0].astype(BF16)
    x = _hier_moe([(mix_a_c, mix_a_l), (mix_b_c, mix_b_l)], [w_out[:A_WIDTH], w_out[A_WIDTH:]], x, m3,
                  norm_ffn[0][None, :], 0, moe_w_rg[0], moe_b_rg[0], moe_w_re[0], moe_b_re[0],
                  moe_w_gate, moe_w_up, moe_w_down, tri_tm, fn, False)

    m3 = mods[1]
    qkv = _modproj(x, m3, norm_mix[1][None, :], na_w_qkv[0].astype(BF16), 512)
    attn_c = _na_ctx(qkv)
    attn_l = _na_latent(qkv, cache_k[:, 0].reshape(DEC_BATCH, 256, D), cache_v[:, 0].reshape(DEC_BATCH, 256, D),
                        _rpb_tables(na_rpb[0]))
    y_c, y_l = _hier_moe([(attn_c, attn_l)], [na_w_out[0].astype(BF16)], x, m3, norm_ffn[1][None, :], 1, moe_w_rg[1], moe_b_rg[1], moe_w_re[1], moe_b_re[1],
                         moe_w_gate, moe_w_up, moe_w_down, tri_tm, fn, True)

    new_k = qkv[:NC, D:2 * D].reshape(BATCH, 1, SEQ, NA_HEADS, NA_DH)
    new_v = qkv[:NC, 2 * D:].reshape(BATCH, 1, SEQ, NA_HEADS, NA_DH)
    return (y_c.reshape(BATCH, SEQ, D), y_l.reshape(DEC_BATCH, DEC_SEQ, D),
            s_f[:, None], s_b[:, None], new_k, new_v)
```

```python
import functools
import math

import numpy as np
import jax
import jax.numpy as jnp
from jax import lax
from jax.experimental import pallas as pl
from jax.experimental.pallas import tpu as pltpu

F32 = jnp.float32
BF16 = jnp.bfloat16
HIGHEST = lax.Precision.HIGHEST

D = 1024
BATCH, SEQ = 32, 256
DEC_BATCH, DEC_SEQ = 2, 4096
NC = BATCH * SEQ
NL = DEC_BATCH * DEC_SEQ
N = NC + NL
DEPTH = 2
GRID_W = 64
A_DH = 64
A_HEADS = 8
A_WIDTH = 512
CHUNK = 64
B_WIDTH = 512
B_GROUPS = 4
NA_DH = 64
NA_HEADS = 16
WIN_R, WIN_C = 8, 16
N_GROUPS, EXP_PER_GROUP, N_EXPERTS = 4, 8, 32
D_EXPERT = 512
EPS = 1e-6

LANES = 128
TM = 256
MOE_ROWS = 256
N_SLOT_BLOCKS = (2 * N) // MOE_ROWS + N_EXPERTS
AB_COLS = 2688
VMEM_LIMIT = 56 * 1024 * 1024
EXPERT_VMEM_LIMIT = 60 * 1024 * 1024
NEG = -1e30


def _cparams(sem):
    return pltpu.CompilerParams(dimension_semantics=sem, vmem_limit_bytes=VMEM_LIMIT)


def _mm(a, b):
    return jnp.dot(a.astype(BF16), b.astype(BF16), preferred_element_type=F32)


def _mm_nt(a, b):
    return lax.dot_general(a.astype(BF16), b.astype(BF16), (((1,), (1,)), ((), ())),
                           preferred_element_type=F32)


SOLVE_BLK = 16
NA_CTX_PAIRS = 2
NA_ROW_UNROLL = 4
TERM_UNROLL = 4


def _unit_lower_solve(a_mat, rhs, same_blk, eye):
    dg = jnp.where(same_blk, a_mat, 0.0)
    p = -dg
    dinv = eye + p
    for _ in range(int(math.log2(SOLVE_BLK)) - 1):
        p = _mm(p, p)
        dinv = dinv + _mm(p, dinv)
    mp = -_mm(dinv, a_mat - dg)
    y = _mm(dinv, rhs)
    y = y + _mm(mp, y)
    for _ in range(int(math.log2(a_mat.shape[0] // SOLVE_BLK)) - 1):
        mp = _mm(mp, mp)
        y = y + _mm(mp, y)
    return y


def _unit_lower_solve_many(a_mats, rhs, same_blk, eye):
    off = [jnp.where(same_blk, 0.0, a).astype(BF16) for a in a_mats]
    p = [jnp.where(same_blk, -a, 0.0).astype(BF16) for a in a_mats]
    dinv = [(eye + x.astype(F32)).astype(BF16) for x in p]
    for _ in range(int(math.log2(SOLVE_BLK)) - 1):
        p = [_mm(x, x).astype(BF16) for x in p]
        dinv = [(di.astype(F32) + _mm(x, di)).astype(BF16) for x, di in zip(p, dinv)]
    mp = [(-_mm(di, o)).astype(BF16) for di, o in zip(dinv, off)]
    y = [_mm(di, r) for di, r in zip(dinv, rhs)]
    y = [yi + _mm(m, yi) for m, yi in zip(mp, y)]
    for _ in range(int(math.log2(a_mats[0].shape[0] // SOLVE_BLK)) - 1):
        mp = [_mm(m, m).astype(BF16) for m in mp]
        y = [yi + _mm(m, yi) for m, yi in zip(mp, y)]
    return y


def _mm_split(a, b, parts, split_rhs=False):
    x = b if split_rhs else a
    acc = None
    for _ in range(parts):
        piece = x.astype(BF16)
        x = x - piece.astype(F32)
        term = (jnp.dot(a.astype(BF16), piece, preferred_element_type=F32) if split_rhs
                else jnp.dot(piece, b.astype(BF16), preferred_element_type=F32))
        acc = term if acc is None else acc + term
    return acc


def _mm_hi(a, b):
    return jnp.dot(a, b, preferred_element_type=F32, precision=HIGHEST)


def _silu(x):
    return x * jax.nn.sigmoid(x)


def _bf16_bits(x):
    b = lax.bitcast_convert_type(x, jnp.int32)
    return b + 0x7FFF + (lax.shift_right_logical(b, jnp.int32(16)) & 1)


_HIGH16 = -65536


def _pack_bf16_pairs(a, b):
    return lax.shift_right_logical(_bf16_bits(a), jnp.int32(16)) | (_bf16_bits(b) & _HIGH16)


def _unpack_bf16_pairs(p):
    a = lax.bitcast_convert_type(lax.shift_left(p, jnp.int32(16)), F32)
    b = lax.bitcast_convert_type(p & _HIGH16, F32)
    return a.astype(BF16), b.astype(BF16)


def _cond_row(i):
    return jnp.where(i < NC // TM, 0, 1 + (i - NC // TM) // (DEC_SEQ // TM))


def _modulated_norm(x, m_ref, g_ref, shift_idx, scale_idx):
    ms = jnp.mean(x * x, axis=-1, keepdims=True)
    y = x * lax.rsqrt(ms + EPS) * g_ref[...]
    return y * (1.0 + m_ref[scale_idx:scale_idx + 1, :]) + m_ref[shift_idx:shift_idx + 1, :]


def _ada_kernel(cond_ref, w_ref, b_ref, o_ref):
    o_ref[...] = _mm_hi(_silu(cond_ref[...]), w_ref[...]) + b_ref[...]


def _ada_params(cond8, mod_w, mod_b):
    tn = 1536
    return pl.pallas_call(
        _ada_kernel,
        grid=(DEPTH, 6 * D // tn),
        in_specs=[pl.BlockSpec((8, D), lambda l, j: (0, 0)),
                  pl.BlockSpec((None, D, tn), lambda l, j: (l, 0, j)),
                  pl.BlockSpec((None, 1, tn), lambda l, j: (l, 0, j))],
        out_specs=pl.BlockSpec((None, 8, tn), lambda l, j: (l, 0, j)),
        out_shape=jax.ShapeDtypeStruct((DEPTH, 8, 6 * D), F32),
        compiler_params=_cparams(("arbitrary", "arbitrary")),
        name="ada_params",
    )(cond8, mod_w, mod_b.reshape(DEPTH, 1, 6 * D))


def _token_specs(x):
    nct = NC // TM
    if isinstance(x, tuple):
        return ([pl.BlockSpec((TM, x[0].shape[1]), lambda i: (jnp.minimum(i, nct - 1), 0)),
                 pl.BlockSpec((TM, x[1].shape[1]), lambda i: (jnp.maximum(i - nct, 0), 0))], list(x))
    return [pl.BlockSpec((TM, x.shape[1]), lambda i: (i, 0))], [x]


def _token_rows(refs):
    if len(refs) == 1:
        return refs[0][...]
    return jnp.where(pl.program_id(0) < NC // TM, refs[0][...], refs[1][...])


def _modproj_kernel(*refs, n_chunk):
    m_ref, g_ref, w_ref, o_ref = refs[-4:]
    hb = _modulated_norm(_token_rows(refs[:-4]), m_ref, g_ref, 0, 1).astype(BF16)
    for j in range(o_ref.shape[1] // n_chunk):
        sl = slice(j * n_chunk, (j + 1) * n_chunk)
        o_ref[:, sl] = jnp.dot(hb, w_ref[:, sl], preferred_element_type=F32)


def _modproj(x, m3, gain, w_bf16, n_chunk):
    nout = w_bf16.shape[1]
    x_specs, x_args = _token_specs(x)
    return pl.pallas_call(
        functools.partial(_modproj_kernel, n_chunk=n_chunk),
        grid=(N // TM,),
        in_specs=x_specs + [pl.BlockSpec((None, 6, D), lambda i: (_cond_row(i), 0, 0)),
                            pl.BlockSpec((1, D), lambda i: (0, 0)),
                            pl.BlockSpec((D, nout), lambda i: (0, 0))],
        out_specs=pl.BlockSpec((TM, nout), lambda i: (i, 0)),
        out_shape=jax.ShapeDtypeStruct((N, nout), F32),
        compiler_params=_cparams(("arbitrary",)),
        name="modproj",
    )(*x_args, m3, gain, w_bf16)


def _deltanet_kernel(q_ref, k_ref, v_ref, z_ref, ab_ref, cq_ref, ck_ref, cv_ref, gp_ref, og_ref,
                     s0f_ref, s0b_ref, tri_ref, jbd_ref, o_ref, sf_ref, sb_ref,
                     u_s, w_s, qd_s, at_s, kt_s, ge_s, st_s, ob, *, seq_len, n_sub):
    hp = pl.program_id(1)
    C = CHUNK
    nc = seq_len // C
    nct = n_sub * nc
    P = LANES
    lane = lax.broadcasted_iota(jnp.int32, (C, P), 1)
    row = lax.broadcasted_iota(jnp.int32, (C, P), 0)
    first_head = lane < A_DH
    ri = lax.broadcasted_iota(jnp.int32, (P, P), 0)
    ci = lax.broadcasted_iota(jnp.int32, (P, P), 1)
    same_head = (ri < C) == (ci < C)
    same_blk = (ri // SOLVE_BLK) == (ci // SOLVE_BLK)
    eye = jnp.where(ri == ci, 1.0, 0.0)
    jbd = jbd_ref[...]
    lincl = tri_ref[...]
    cum_b = [lincl.T.astype(BF16), lincl.astype(BF16)]
    incl_m = [jnp.logical_and(same_head, ri >= ci), jnp.logical_and(same_head, ri <= ci)]
    strict_m = [jnp.logical_and(same_head, ri > ci), jnp.logical_and(same_head, ri < ci)]
    neg_a = -jnp.exp(gp_ref[0])
    dt_b = gp_ref[1]

    def conv_silu(ref, w_ref, c):
        base = pl.multiple_of(c * C, C)
        cs = c % nc
        xc = ref[pl.ds(base, C), :]
        pbase = pl.multiple_of(jnp.maximum(base - 8, 0), 8)
        nbase = pl.multiple_of(jnp.minimum(base + C, n_sub * seq_len - 8), 8)
        prev_row = ref[pl.ds(pbase, 8), :][7:8, :] * jnp.where(cs > 0, 1.0, 0.0)
        next_row = ref[pl.ds(nbase, 8), :][0:1, :] * jnp.where(cs < nc - 1, 1.0, 0.0)
        x_prev = jnp.where(row == 0, prev_row, pltpu.roll(xc, 1, 0))
        x_next = jnp.where(row == C - 1, next_row, pltpu.roll(xc, C - 1, 0))
        y = w_ref[0:1, :] * x_prev + w_ref[1:2, :] * xc + w_ref[2:3, :] * x_next
        return _silu(y)

    def stack(x):
        return jnp.concatenate([jnp.where(first_head, x, 0.0), jnp.where(first_head, 0.0, x)], axis=0)

    def chunk_inputs(c):
        base = pl.multiple_of(c * C, C)
        q = conv_silu(q_ref, cq_ref, c)
        k = conv_silu(k_ref, ck_ref, c)
        v = conv_silu(v_ref, cv_ref, c)
        return q, k, v, ab_ref[pl.ds(base, C), :].T

    sub8 = lax.broadcasted_iota(jnp.int32, (A_HEADS, C), 0)

    def pair_row(x8):
        r0 = jnp.sum(jnp.where(sub8 == 2 * hp, x8, 0.0), axis=0, keepdims=True)
        r1 = jnp.sum(jnp.where(sub8 == 2 * hp + 1, x8, 0.0), axis=0, keepdims=True)
        return jnp.concatenate([r0, r1], axis=1)

    def chain_gates(ab_t, d):
        a8 = ab_t[d * A_HEADS:(d + 1) * A_HEADS, :]
        b8 = ab_t[(2 + d) * A_HEADS:(3 + d) * A_HEADS, :]
        g8 = neg_a[d * A_HEADS:(d + 1) * A_HEADS, :C] * jax.nn.softplus(a8 + dt_b[d * A_HEADS:(d + 1) * A_HEADS, :C])
        gc8 = _mm_split(g8, cum_b[d], 3)
        tot8 = jnp.broadcast_to(jnp.sum(g8, axis=-1, keepdims=True), (A_HEADS, C))
        gc_row, beta_row, tot_row = pair_row(gc8), pair_row(jax.nn.sigmoid(b8)), pair_row(tot8)
        cols = jnp.concatenate([gc_row, beta_row, tot_row, jnp.zeros((5, P), F32)], axis=0).T
        return gc_row, tot_row, cols[:, 0:1], cols[:, 1:2], cols[:, 2:3]

    def terms_body(j, carry):
        cs = [TERM_UNROLL * j + t for t in range(TERM_UNROLL)]
        ins = [chunk_inputs(c) for c in cs]
        qsq = [_mm_split(x[0] * x[0], jbd, 2) for x in ins]
        ksq = [_mm_split(x[1] * x[1], jbd, 2) for x in ins]
        qs = [x[0] * lax.rsqrt(s + EPS) * (A_DH ** -0.5) for x, s in zip(ins, qsq)]
        ks = [x[1] * lax.rsqrt(s + EPS) for x, s in zip(ins, ksq)]
        qst = [stack(x) for x in qs]
        kst = [stack(x) for x in ks]
        vst = [stack(x[2]) for x in ins]
        kst_t = [x.T for x in kst]
        kk = [_mm_nt(x, x) for x in kst]
        qk = [_mm_nt(x, y) for x, y in zip(qst, kst)]
        chains = [(t, d) for t in range(TERM_UNROLL) for d in range(2)]
        gates = [chain_gates(ins[t][3], d) for t, d in chains]
        decay, e_gc = [], []
        for (t, d), (gc_row, tot_row, gc_col, beta_col, tot_col) in zip(chains, gates):
            diff = jnp.broadcast_to(gc_col, (P, P)) - jnp.broadcast_to(gc_row, (P, P))
            decay.append(jnp.where(incl_m[d], jnp.exp(jnp.where(incl_m[d], diff, 0.0)), 0.0))
            e_gc.append(jnp.exp(gc_col))
        a_mats = [jnp.where(strict_m[d], g[3] * kk[t] * dc, 0.0) for (t, d), g, dc in zip(chains, gates, decay)]
        rhs = [vst[t] * g[3] + pltpu.roll(kst[t] * (g[3] * e), A_DH, 1)
               for (t, d), g, e in zip(chains, gates, e_gc)]
        xs = _unit_lower_solve_many(a_mats, rhs, same_blk, eye)
        for (t, d), x, g, e, dc in zip(chains, xs, gates, e_gc, decay):
            c = cs[t]
            gc_row, tot_row = g[0], g[1]
            u_s[d, c] = jnp.where(same_head, x, 0.0).astype(BF16)
            w_s[d, c] = pltpu.roll(jnp.where(same_head, 0.0, x), A_DH, 1).astype(BF16)
            qd_s[d, c] = (qst[t] * e).astype(BF16)
            at_s[d, c] = jnp.where(incl_m[d], qk[t] * dc, 0.0).astype(BF16)
            kt_s[d, c] = (kst_t[t] * jnp.exp(tot_row - gc_row)).astype(BF16)
            ge_s[d, c] = jnp.broadcast_to(jnp.exp(tot_row), (8, P))
        return carry

    lax.fori_loop(0, nct // TERM_UNROLL, terms_body, 0)

    def block_diag(s2):
        z = jnp.zeros((A_DH, A_DH), F32)
        return jnp.concatenate([jnp.concatenate([s2[0], z], axis=1),
                                jnp.concatenate([z, s2[1]], axis=1)], axis=0)

    for s in range(n_sub):
        st_s[2 * s] = block_diag(s0f_ref[s])
        st_s[2 * s + 1] = block_diag(s0b_ref[s])

    def scan_body(i, carry):
        chains = [(s, d, s * nc + (i if d == 0 else nc - 1 - i)) for s in range(n_sub) for d in range(2)]
        dot = functools.partial(jnp.dot, preferred_element_type=F32)
        s_bd = [st_s[2 * s + d] for s, d, c in chains]
        sb16 = [x.astype(BF16) for x in s_bd]
        ws = [dot(w_s[d, c], sb) for (s, d, c), sb in zip(chains, sb16)]
        qs_ = [dot(qd_s[d, c], sb) for (s, d, c), sb in zip(chains, sb16)]
        vb = [(u_s[d, c].astype(F32) - x).astype(BF16) for (s, d, c), x in zip(chains, ws)]
        av = [dot(at_s[d, c], x) for (s, d, c), x in zip(chains, vb)]
        kv = [dot(kt_s[d, c], x) for (s, d, c), x in zip(chains, vb)]
        for (s, d, c), sb, q_, a_, k_ in zip(chains, s_bd, qs_, av, kv):
            st_s[2 * s + d] = sb * ge_s[d, c][0:1, :] + k_
            o_st = q_ + a_
            dst = o_ref if d == 0 else ob
            dst[pl.ds(pl.multiple_of(c * C, C), C), :] = o_st[:C] + o_st[C:]
        return carry

    lax.fori_loop(0, nc, scan_body, 0)

    for s in range(n_sub):
        for d, ref in ((0, sf_ref), (1, sb_ref)):
            s_bd = st_s[2 * s + d]
            ref[s, 0] = s_bd[:A_DH, :A_DH]
            ref[s, 1] = s_bd[A_DH:, A_DH:]

    def finish(j, carry):
        bases = [pl.multiple_of((TERM_UNROLL * j + t) * C, C) for t in range(TERM_UNROLL)]
        o = [o_ref[pl.ds(b, C), :] + ob[pl.ds(b, C), :] for b in bases]
        ms = [_mm_split(x * x, jbd, 2) * (1.0 / A_DH) for x in o]
        for b, x, m in zip(bases, o, ms):
            o_ref[pl.ds(b, C), :] = x * lax.rsqrt(m + EPS) * og_ref[...] * _silu(z_ref[pl.ds(b, C), :])
        return carry

    lax.fori_loop(0, nct // TERM_UNROLL, finish, 0)


def _deltanet(proj, conv_w, gate_p, o_gain2, s0f, s0b, tri, jbd, *, seq_len, n_seq, n_sub, row_blk0):
    rows = n_sub * seq_len
    nct = rows // CHUNK
    rb = lambda b: row_blk0 + b
    col = lambda off: (lambda b, hp: (rb(b), off + hp))
    st_spec = pl.BlockSpec((n_sub, 2, A_DH, A_DH), lambda b, hp: (b, hp, 0, 0))
    in_specs = [pl.BlockSpec((rows, LANES), col(0)),
                pl.BlockSpec((rows, LANES), col(4)),
                pl.BlockSpec((rows, LANES), col(8)),
                pl.BlockSpec((rows, LANES), col(12)),
                pl.BlockSpec((rows, LANES), lambda b, hp: (rb(b), 20)),
                pl.BlockSpec((3, LANES), lambda b, hp: (0, hp)),
                pl.BlockSpec((3, LANES), lambda b, hp: (0, 4 + hp)),
                pl.BlockSpec((3, LANES), lambda b, hp: (0, 8 + hp)),
                pl.BlockSpec((2, 2 * A_HEADS, LANES), lambda b, hp: (0, 0, 0)),
                pl.BlockSpec((1, LANES), lambda b, hp: (0, 0)),
                st_spec, st_spec,
                pl.BlockSpec((CHUNK, CHUNK), lambda b, hp: (0, 0)),
                pl.BlockSpec((LANES, LANES), lambda b, hp: (0, 0))]
    args = [proj, proj, proj, proj, proj, conv_w, conv_w, conv_w, gate_p, o_gain2, s0f, s0b, tri, jbd]
    st_shape = jax.ShapeDtypeStruct((n_seq, A_HEADS, A_DH, A_DH), F32)
    tile = lambda dt: pltpu.VMEM((2, nct, LANES, LANES), dt)
    return pl.pallas_call(
        functools.partial(_deltanet_kernel, seq_len=seq_len, n_sub=n_sub),
        grid=(n_seq // n_sub, A_HEADS // 2),
        in_specs=in_specs,
        out_specs=[pl.BlockSpec((rows, LANES), lambda b, hp: (b, hp)), st_spec, st_spec],
        out_shape=[jax.ShapeDtypeStruct((n_seq * seq_len, A_WIDTH), F32), st_shape, st_shape],
        scratch_shapes=[tile(BF16), tile(BF16), tile(BF16), tile(BF16), tile(BF16),
                        pltpu.VMEM((2, nct, 8, LANES), F32),
                        pltpu.VMEM((2 * n_sub, LANES, LANES), F32),
                        pltpu.VMEM((rows, LANES), F32)],
        compiler_params=_cparams(("arbitrary", "arbitrary")),
        name="deltanet",
    )(*args)


def _dft_mats(n):
    idx = np.arange(n)
    ang = 2.0 * np.pi * ((idx[:, None] * idx[None, :]) % n) / n
    return np.cos(ang), np.sin(ang)


def _fnet_ctx_kernel(u_ref, cs_ref, dft_ref, o_ref):
    norm = 1.0 / math.sqrt(SEQ * LANES)
    for g in range(B_GROUPS):
        sl = slice(g * LANES, (g + 1) * LANES)
        p = _mm(u_ref[:, sl], cs_ref[...])
        stack = jnp.concatenate([p[:, :LANES], p[:, LANES:]], axis=0)
        o_ref[:, sl] = _mm(dft_ref[...], stack) * norm


def _fnet_ctx(proj, cs, dft):
    return pl.pallas_call(
        _fnet_ctx_kernel,
        grid=(BATCH,),
        in_specs=[pl.BlockSpec((SEQ, B_WIDTH), lambda b: (b, 4)),
                  pl.BlockSpec(cs.shape, lambda b: (0, 0)),
                  pl.BlockSpec(dft.shape, lambda b: (0, 0))],
        out_specs=pl.BlockSpec((SEQ, B_WIDTH), lambda b: (b, 0)),
        out_shape=jax.ShapeDtypeStruct((NC, B_WIDTH), F32),
        compiler_params=_cparams(("arbitrary",)),
        name="fnet_ctx",
    )(proj, cs, dft)


FN_SUB = 4


def _fnet_lat1_kernel(u_ref, ca_ref, m1_ref, twc_ref, tws_ref, o_ref):
    r = 64
    for j in range(FN_SUB):
        c = twc_ref[j]
        s = tws_ref[j]
        rows = slice(j * r, (j + 1) * r)
        for g in range(B_GROUPS):
            pa = _mm(u_ref[rows, g * LANES:(g + 1) * LANES], ca_ref[...])
            rhs = jnp.concatenate([pa[:, :2 * LANES], pa[:, 2 * LANES:]], axis=0)
            zz = _mm(m1_ref[...], rhs)
            zr, zi = zz[:, :LANES], zz[:, LANES:]
            o_ref[rows, 2 * g * LANES:(2 * g + 1) * LANES] = (zr * c + zi * s).astype(BF16)
            o_ref[rows, (2 * g + 1) * LANES:(2 * g + 2) * LANES] = (zi * c - zr * s).astype(BF16)


def _fnet_lat2_kernel(z_ref, m1_ref, o_ref):
    r = 64
    norm = 1.0 / math.sqrt(DEC_SEQ * LANES)
    for j in range(FN_SUB):
        rows = slice(j * r, (j + 1) * r)
        for g in range(B_GROUPS):
            rhs = jnp.concatenate([z_ref[rows, 2 * g * LANES:(2 * g + 1) * LANES],
                                   z_ref[rows, (2 * g + 1) * LANES:(2 * g + 2) * LANES]], axis=0)
            o_ref[rows, g * LANES:(g + 1) * LANES] = _mm(m1_ref[...], rhs) * norm


def _fnet_latent(proj, ca, m1, twc, tws):
    r = 64
    u = proj[NC:, 2048:2560].astype(BF16).reshape(DEC_BATCH, r, r, B_WIDTH)
    u = u.transpose(0, 2, 1, 3).reshape(DEC_BATCH * r * r, B_WIDTH)
    steps = DEC_BATCH * r // FN_SUB
    blk = FN_SUB * r
    z = pl.pallas_call(
        _fnet_lat1_kernel,
        grid=(steps,),
        in_specs=[pl.BlockSpec((blk, B_WIDTH), lambda s: (s, 0)),
                  pl.BlockSpec(ca.shape, lambda s: (0, 0)),
                  pl.BlockSpec(m1.shape, lambda s: (0, 0)),
                  pl.BlockSpec((FN_SUB, r, LANES), lambda s: (s % (r // FN_SUB), 0, 0)),
                  pl.BlockSpec((FN_SUB, r, LANES), lambda s: (s % (r // FN_SUB), 0, 0))],
        out_specs=pl.BlockSpec((blk, 2 * B_WIDTH), lambda s: (s, 0)),
        out_shape=jax.ShapeDtypeStruct((NL, 2 * B_WIDTH), BF16),
        compiler_params=_cparams(("arbitrary",)),
        name="fnet_lat1",
    )(u, ca, m1, twc, tws)
    z = z.reshape(DEC_BATCH, r, r, 2 * B_WIDTH).transpose(0, 2, 1, 3).reshape(NL, 2 * B_WIDTH)
    y = pl.pallas_call(
        _fnet_lat2_kernel,
        grid=(steps,),
        in_specs=[pl.BlockSpec((blk, 2 * B_WIDTH), lambda s: (s, 0)),
                  pl.BlockSpec(m1.shape, lambda s: (0, 0))],
        out_specs=pl.BlockSpec((blk, B_WIDTH), lambda s: (s, 0)),
        out_shape=jax.ShapeDtypeStruct((NL, B_WIDTH), F32),
        compiler_params=_cparams(("arbitrary",)),
        name="fnet_lat2",
    )(z, m1)
    return y.reshape(DEC_BATCH, r, r, B_WIDTH).transpose(0, 2, 1, 3).reshape(NL, B_WIDTH)


def _head_masks():
    lane = lax.broadcasted_iota(jnp.int32, (1, LANES), 1)
    return lane < NA_DH


def _attend_many(chains):
    scale = NA_DH ** -0.5
    s = [[_mm_nt(q, k) * scale if b is None else _mm_nt(q, k) * scale + b for k, v, b in kv] for q, kv in chains]
    m = [functools.reduce(jnp.maximum, [jnp.max(x, axis=-1, keepdims=True) for x in xs]) for xs in s]
    p = [[jnp.exp(x - mi) for x in xs] for xs, mi in zip(s, m)]
    l = [sum(jnp.sum(x, axis=-1, keepdims=True) for x in xs) for xs in p]
    o = [sum(_mm(x, v) for x, (k, v, b) in zip(xs, kv)) for xs, (q, kv) in zip(p, chains)]
    return [oi / li for oi, li in zip(o, l)]


def _na_ctx_kernel(q_ref, k_ref, v_ref, o_ref):
    first = _head_masks()
    n_split = 2
    rows = SEQ // n_split
    chains = []
    for p in range(NA_CTX_PAIRS):
        lanes = slice(p * LANES, (p + 1) * LANES)
        k = k_ref[:, lanes].astype(BF16)
        v = v_ref[:, lanes].astype(BF16)
        for j in range(n_split):
            q = q_ref[j * rows:(j + 1) * rows, lanes]
            for a in range(2):
                chains.append((jnp.where(first if a == 0 else jnp.logical_not(first), q, 0.0), [(k, v, None)]))
    outs = _attend_many(chains)
    for p in range(NA_CTX_PAIRS):
        for j in range(n_split):
            i = 2 * (p * n_split + j)
            o_ref[j * rows:(j + 1) * rows, p * LANES:(p + 1) * LANES] = jnp.where(first, outs[i], outs[i + 1])


def _na_ctx(qkv):
    w = NA_CTX_PAIRS * LANES
    nblk = D // w
    return pl.pallas_call(
        _na_ctx_kernel,
        grid=(BATCH, nblk),
        in_specs=[pl.BlockSpec((SEQ, w), lambda b, hp: (b, hp)),
                  pl.BlockSpec((SEQ, w), lambda b, hp: (b, nblk + hp)),
                  pl.BlockSpec((SEQ, w), lambda b, hp: (b, 2 * nblk + hp))],
        out_specs=pl.BlockSpec((SEQ, w), lambda b, hp: (b, hp)),
        out_shape=jax.ShapeDtypeStruct((NC, D), F32),
        compiler_params=_cparams(("arbitrary", "arbitrary")),
        name="na_ctx",
    )(qkv, qkv, qkv)


def _na_lat_kernel(q_ref, k_ref, v_ref, kc_ref, vc_ref, tt_ref, o_ref):
    first = _head_masks()
    rows = DEC_SEQ // GRID_W
    kctx = kc_ref[...].astype(BF16)
    vctx = vc_ref[...].astype(BF16)
    nkeys = WIN_R * GRID_W

    def rows_body(j, carry):
        chains = []
        for t in range(NA_ROW_UNROLL):
            r = NA_ROW_UNROLL * j + t
            r0 = jnp.clip(r - WIN_R // 2, 0, rows - WIN_R)
            dr0 = r0 - r + (WIN_R - 1)
            q = q_ref[pl.ds(pl.multiple_of(r * GRID_W, GRID_W), GRID_W), :]
            kbase = pl.multiple_of(r0 * GRID_W, GRID_W)
            kl = k_ref[pl.ds(kbase, nkeys), :].astype(BF16)
            vl = v_ref[pl.ds(kbase, nkeys), :].astype(BF16)
            for a in range(2):
                qm = jnp.where(first if a == 0 else jnp.logical_not(first), q, 0.0)
                bias = jnp.concatenate([tt_ref[a, dr0 + 2 * i] for i in range(WIN_R // 2)], axis=1)
                chains.append((qm, [(kl, vl, bias), (kctx, vctx, None)]))
        outs = _attend_many(chains)
        for t in range(NA_ROW_UNROLL):
            r = NA_ROW_UNROLL * j + t
            o_ref[pl.ds(pl.multiple_of(r * GRID_W, GRID_W), GRID_W), :] = jnp.where(first, outs[2 * t], outs[2 * t + 1])
        return carry

    lax.fori_loop(0, rows // NA_ROW_UNROLL, rows_body, 0)


def _na_latent(qkv, cache_k2, cache_v2, tt2):
    rb0 = NC // DEC_SEQ
    return pl.pallas_call(
        _na_lat_kernel,
        grid=(DEC_BATCH, NA_HEADS // 2),
        in_specs=[pl.BlockSpec((DEC_SEQ, LANES), lambda b, hp: (rb0 + b, hp)),
                  pl.BlockSpec((DEC_SEQ, LANES), lambda b, hp: (rb0 + b, 8 + hp)),
                  pl.BlockSpec((DEC_SEQ, LANES), lambda b, hp: (rb0 + b, 16 + hp)),
                  pl.BlockSpec((None, 256, LANES), lambda b, hp: (b, 0, hp)),
                  pl.BlockSpec((None, 256, LANES), lambda b, hp: (b, 0, hp)),
                  pl.BlockSpec((2, 2 * WIN_R - 2, GRID_W, LANES), lambda b, hp: (hp, 0, 0, 0))],
        out_specs=pl.BlockSpec((DEC_SEQ, LANES), lambda b, hp: (b, hp)),
        out_shape=jax.ShapeDtypeStruct((NL, D), F32),
        compiler_params=_cparams(("arbitrary", "arbitrary")),
        name="na_latent",
    )(qkv, qkv, qkv, cache_k2, cache_v2, tt2)


def _rpb_tables(rpb):
    col = np.arange(GRID_W)
    start = np.clip(col - WIN_C // 2, 0, GRID_W - WIN_C)
    inside = (col[None, :] >= start[:, None]) & (col[None, :] < start[:, None] + WIN_C)
    dc = col[None, :] - col[:, None] + (WIN_C - 1)
    t = jnp.full((NA_HEADS, 2 * WIN_R - 1, GRID_W, GRID_W), NEG, F32)
    for j in range(2 * WIN_C - 1):
        t = jnp.where((inside & (dc == j))[None, None], rpb[:, :, j, None, None], t)
    return jnp.concatenate([t[:, :-1], t[:, 1:]], axis=-1)


ROUTE_ROWS = 40


def _router_kernel(*refs, n_in):
    a_refs = refs[:2 * n_in]
    w_refs = refs[2 * n_in:3 * n_in]
    x_refs = refs[3 * n_in:-11]
    m_ref, g_ref, wr_ref, br_ref, tri_ref, xn_ref, xf_ref, ri_ref, rw_ref, cnt_ref, base_scr = refs[-11:]
    i = pl.program_id(0)

    @pl.when(i == 0)
    def _():
        base_scr[...] = jnp.zeros_like(base_scr)
        cnt_ref[...] = jnp.zeros_like(cnt_ref)

    acc = None
    for j, w_ref in enumerate(w_refs):
        part = _mm(_token_rows(a_refs[2 * j:2 * j + 2]), w_ref[...])
        acc = part if acc is None else acc + part
    x_new = _token_rows(x_refs) + m_ref[2:3, :] * acc
    xn_ref[...] = x_new
    h = _modulated_norm(x_new, m_ref, g_ref, 3, 4)
    xf_ref[...] = _pack_bf16_pairs(h[:, :D // 2], h[:, D // 2:])
    logits = lax.dot_general(wr_ref[...], h, (((1,), (1,)), ((), ())), preferred_element_type=F32,
                             precision=HIGHEST) + br_ref[:, 0:1]
    row = lax.broadcasted_iota(jnp.int32, logits.shape, 0)
    cmax = lambda x: jnp.max(x, axis=0, keepdims=True)
    cmin = lambda x: jnp.min(x, axis=0, keepdims=True)
    csum = lambda x: jnp.sum(x, axis=0, keepdims=True)

    gmask = row < N_GROUPS
    mg = cmax(jnp.where(gmask, logits, NEG))
    eg = jnp.where(gmask, jnp.exp(jnp.where(gmask, logits - mg, NEG)), 0.0)
    pg = eg / csum(eg)
    p_grp = cmax(pg)
    grp = cmin(jnp.where(jnp.logical_and(gmask, pg == p_grp), row, ROUTE_ROWS))
    lo = N_GROUPS + grp * EXP_PER_GROUP
    emask = jnp.logical_and(row >= lo, row < lo + EXP_PER_GROUP)
    me = cmax(jnp.where(emask, logits, NEG))
    ee = jnp.where(emask, jnp.exp(jnp.where(emask, logits - me, NEG)), 0.0)
    pe = ee / csum(ee)
    p1 = cmax(pe)
    i1 = cmin(jnp.where(jnp.logical_and(emask, pe == p1), row, ROUTE_ROWS))
    m2 = jnp.logical_and(emask, row != i1)
    p2 = cmax(jnp.where(m2, pe, -1.0))
    i2 = cmin(jnp.where(jnp.logical_and(m2, pe == p2), row, ROUTE_ROWS))
    den = p1 + p2
    w1 = p_grp * p1 / den
    w2 = p_grp * p2 / den

    sel1 = row == i1
    sel2 = row == i2
    oh = jnp.where(jnp.logical_or(sel1, sel2), 1.0, 0.0).astype(BF16)
    before = jnp.dot(oh, tri_ref[...], preferred_element_type=F32) + base_scr[:, 0:1]
    rank1 = csum(jnp.where(sel1, before, 0.0))
    rank2 = csum(jnp.where(sel2, before, 0.0))
    base_scr[...] = base_scr[...] + jnp.sum(oh.astype(F32), axis=1, keepdims=True)
    cnt_ref[...] = cnt_ref[...] + lax.dot_general(jnp.ones((8, TM), BF16), oh, (((1,), (1,)), ((), ())),
                                                  preferred_element_type=F32)
    sub = lax.broadcasted_iota(jnp.int32, (8, TM), 0)
    ri_ref[...] = jnp.where(sub == 0, i1 - N_GROUPS, jnp.where(sub == 1, i2 - N_GROUPS,
                  jnp.where(sub == 2, rank1.astype(jnp.int32), jnp.where(sub == 3, rank2.astype(jnp.int32), 0))))
    rw_ref[...] = jnp.where(sub == 0, w1, jnp.where(sub == 1, w2, 0.0))


def _router(a_pairs, w_list, x, m3, gain, wr_t, br_t, tri_upper):
    in_specs, args = [], []
    for pair in a_pairs:
        specs, ops = _token_specs(pair)
        in_specs += specs
        args += ops
    x_specs, x_args = _token_specs(x)
    in_specs += ([pl.BlockSpec(w.shape, lambda i: (0, 0)) for w in w_list] + x_specs
                 + [pl.BlockSpec((None, 6, D), lambda i: (_cond_row(i), 0, 0)),
                    pl.BlockSpec((1, D), lambda i: (0, 0)),
                    pl.BlockSpec((ROUTE_ROWS, D), lambda i: (0, 0)),
                    pl.BlockSpec((ROUTE_ROWS, LANES), lambda i: (0, 0)),
                    pl.BlockSpec((TM, TM), lambda i: (0, 0))])
    return pl.pallas_call(
        functools.partial(_router_kernel, n_in=len(a_pairs)),
        grid=(N // TM,),
        in_specs=in_specs,
        out_specs=[pl.BlockSpec((TM, D), lambda i: (i, 0)),
                   pl.BlockSpec((TM, D // 2), lambda i: (i, 0)),
                   pl.BlockSpec((8, TM), lambda i: (0, i)),
                   pl.BlockSpec((8, TM), lambda i: (0, i)),
                   pl.BlockSpec((8, ROUTE_ROWS), lambda i: (0, 0))],
        out_shape=[jax.ShapeDtypeStruct((N, D), F32),
                   jax.ShapeDtypeStruct((N, D // 2), jnp.int32),
                   jax.ShapeDtypeStruct((8, N), jnp.int32),
                   jax.ShapeDtypeStruct((8, N), F32),
                   jax.ShapeDtypeStruct((8, ROUTE_ROWS), F32)],
        scratch_shapes=[pltpu.VMEM((ROUTE_ROWS, LANES), F32)],
        compiler_params=_cparams(("arbitrary",)),
        name="router",
    )(*args, *w_list, *x_args, m3, gain, wr_t, br_t, tri_upper)


def _row_copy(src_hbm, row, dst, r, sem):
    return pltpu.make_async_copy(src_hbm.at[pl.ds(row, 1)], dst.at[pl.ds(r, 1)], sem)


def _expert_kernel(be_ref, nu_ref, d_ref, xp_hbm, wg_ref, wu_ref, wd_ref, ys_ref, xres, xbuf, st_ref, sem):
    del be_ref
    b = pl.program_id(0)
    half = D // 2

    def gather(blk, slot, part=0, parts=1):
        base = blk * MOE_ROWS
        for r in range(part * MOE_ROWS // parts, (part + 1) * MOE_ROWS // parts):
            xbuf[slot, pl.ds(r, 1), :] = xres[pl.ds(st_ref[base + r], 1), :]

    @pl.when(b == 0)
    def _():
        cp = pltpu.make_async_copy(xp_hbm, xres, sem.at[0])
        cp.start()

        def clear(j, c):
            for t in range(32):
                st_ref[32 * j + t] = 0
            return c

        lax.fori_loop(0, N_SLOT_BLOCKS * MOE_ROWS // 32, clear, 0)

        def invert(j, c):
            slots = [d_ref[16 * j + t] for t in range(16)]
            for t in range(16):
                st_ref[slots[t]] = 8 * j + t // 2
            return c

        lax.fori_loop(0, 2 * N // 16, invert, 0)
        cp.wait()
        gather(0, 0)

    @pl.when(b < nu_ref[0])
    def _():
        nxt = (jnp.minimum(b + 1, N_SLOT_BLOCKS - 1), (b + 1) % 2)
        x_lo, x_hi = _unpack_bf16_pairs(xbuf[b % 2])
        dot = functools.partial(jnp.dot, preferred_element_type=F32)
        gather(*nxt, 0, 8)
        g = dot(x_lo, wg_ref[:half, :].astype(BF16))
        gather(*nxt, 1, 8)
        g = g + dot(x_hi, wg_ref[half:, :].astype(BF16))
        gather(*nxt, 2, 8)
        u = dot(x_lo, wu_ref[:half, :].astype(BF16))
        gather(*nxt, 3, 8)
        u = u + dot(x_hi, wu_ref[half:, :].astype(BF16))
        hb = (_silu(g) * u).astype(BF16)
        quarter = D // 4
        for j in range(4):
            gather(*nxt, 4 + j, 8)
            cols = slice(j * quarter, (j + 1) * quarter)
            ys_ref[:, cols] = jnp.dot(hb, wd_ref[:, cols].astype(BF16), preferred_element_type=F32)

    @pl.when(b >= nu_ref[0])
    def _():
        ys_ref[...] = jnp.zeros_like(ys_ref)


def _experts(block_e, n_used, dest, xp, w_gate, w_up, w_down, layer):
    grid_spec = pltpu.PrefetchScalarGridSpec(
        num_scalar_prefetch=3,
        grid=(N_SLOT_BLOCKS,),
        in_specs=[pl.BlockSpec(memory_space=pl.ANY),
                  pl.BlockSpec((None, None, D, D_EXPERT), lambda b, be, nu, st: (layer, be[b], 0, 0)),
                  pl.BlockSpec((None, None, D, D_EXPERT), lambda b, be, nu, st: (layer, be[b], 0, 0)),
                  pl.BlockSpec((None, None, D_EXPERT, D), lambda b, be, nu, st: (layer, be[b], 0, 0))],
        out_specs=pl.BlockSpec((MOE_ROWS, D), lambda b, be, nu, st: (b, 0)),
        scratch_shapes=[pltpu.VMEM((N, D // 2), jnp.int32),
                        pltpu.VMEM((2, MOE_ROWS, D // 2), jnp.int32),
                        pltpu.SMEM((N_SLOT_BLOCKS * MOE_ROWS,), jnp.int32),
                        pltpu.SemaphoreType.DMA((1,))])
    return pl.pallas_call(
        _expert_kernel,
        grid_spec=grid_spec,
        out_shape=jax.ShapeDtypeStruct((N_SLOT_BLOCKS * MOE_ROWS, D), F32),
        compiler_params=pltpu.CompilerParams(dimension_semantics=("arbitrary",), vmem_limit_bytes=EXPERT_VMEM_LIMIT),
        name="experts",
    )(block_e, n_used, dest, xp, w_gate, w_up, w_down)


def _combine_kernel(d_ref, ys_hbm, x_ref, m_ref, rw_ref, fn_ref, o_ref, buf, sem, *, final, tile0, n_tiles):
    i = pl.program_id(0)

    def issue(tile, slot):
        base = (tile0 + tile) * (2 * TM)
        for r in range(TM):
            for kk in range(2):
                _row_copy(ys_hbm, d_ref[base + 2 * r + kk], buf.at[slot, kk], r, sem.at[slot]).start()

    @pl.when(i == 0)
    def _():
        issue(0, 0)

    @pl.when(i + 1 < n_tiles)
    def _():
        issue(i + 1, (i + 1) % 2)

    slot = i % 2
    for kk in range(2):
        pltpu.make_async_copy(ys_hbm.at[pl.ds(0, TM)], buf.at[slot, kk], sem.at[slot]).wait()
    w = rw_ref[...]
    y = w[:, 0:1] * buf[slot, 0] + w[:, 1:2] * buf[slot, 1]
    out = x_ref[...] + m_ref[5:6, :] * y
    if final:
        ms = jnp.mean(out * out, axis=-1, keepdims=True)
        out = out * lax.rsqrt(ms + EPS) * fn_ref[...]
    o_ref[...] = out


def _combine(dest_flat, ys, x, m3, rw, final_norm, final, tile0=0, n_tiles=N // TM):
    grid_spec = pltpu.PrefetchScalarGridSpec(
        num_scalar_prefetch=1,
        grid=(n_tiles,),
        in_specs=[pl.BlockSpec(memory_space=pl.ANY),
                  pl.BlockSpec((TM, D), lambda i, d: (tile0 + i, 0)),
                  pl.BlockSpec((None, 6, D), lambda i, d: (_cond_row(tile0 + i), 0, 0)),
                  pl.BlockSpec((TM, 2), lambda i, d: (tile0 + i, 0)),
                  pl.BlockSpec((1, D), lambda i, d: (0, 0))],
        out_specs=pl.BlockSpec((TM, D), lambda i, d: (i, 0)),
        scratch_shapes=[pltpu.VMEM((2, 2, TM, D), F32), pltpu.SemaphoreType.DMA((2,))])
    return pl.pallas_call(
        functools.partial(_combine_kernel, final=final, tile0=tile0, n_tiles=n_tiles),
        grid_spec=grid_spec,
        out_shape=jax.ShapeDtypeStruct((n_tiles * TM, D), F32),
        compiler_params=_cparams(("arbitrary",)),
        name="combine",
    )(dest_flat, ys, x, m3, rw, final_norm)


def _hier_moe(a_pairs, w_list, x, m3, gain, layer, w_rg, b_rg, w_re, b_re, w_gate, w_up, w_down, tri_tm,
              final_norm, final):
    pad = ROUTE_ROWS - N_GROUPS - N_EXPERTS
    wr_t = jnp.concatenate([w_rg.T, w_re.transpose(0, 2, 1).reshape(N_EXPERTS, D), jnp.zeros((pad, D), F32)], axis=0)
    br_t = jnp.broadcast_to(jnp.concatenate([b_rg, b_re.reshape(N_EXPERTS), jnp.zeros((pad,), F32)])[:, None],
                            (ROUTE_ROWS, LANES))
    x, xf, ri_t, rw_t, cnt = _router(a_pairs, w_list, x, m3, gain, wr_t, br_t, tri_tm)
    e_idx = ri_t[0:2].T
    rank = ri_t[2:4].T
    rw = rw_t[0:2].T
    counts = cnt[0, N_GROUPS:N_GROUPS + N_EXPERTS].astype(jnp.int32)
    padded = (counts + MOE_ROWS - 1) // MOE_ROWS * MOE_ROWS
    end_pad = jnp.cumsum(padded)
    start_pad = end_pad - padded
    experts = jnp.arange(N_EXPERTS, dtype=jnp.int32)
    start_of = jnp.sum(jnp.where(e_idx[:, :, None] == experts, start_pad, 0), axis=-1)
    dest = (start_of + rank).reshape(-1).astype(jnp.int32)
    block_start = jnp.arange(N_SLOT_BLOCKS, dtype=jnp.int32) * MOE_ROWS
    block_e = jnp.minimum(jnp.sum((end_pad[None, :] <= block_start[:, None]).astype(jnp.int32), axis=1),
                          N_EXPERTS - 1)
    n_used = (end_pad[-1:] // MOE_ROWS).astype(jnp.int32)
    ys = _experts(block_e, n_used, dest, xf, w_gate, w_up, w_down, layer)
    if not final:
        return _combine(dest, ys, x, m3, rw, final_norm, False)
    nct = NC // TM
    return (_combine(dest, ys, x, m3, rw, final_norm, True, 0, nct),
            _combine(dest, ys, x, m3, rw, final_norm, True, nct, N // TM - nct))


def kernel(x_prompt, x_sample, state_A_fwd, state_A_bwd, cache_k, cache_v, c, c_ctx, mod_w, mod_b, norm_mix, norm_ffn, ab_w_in, ab_conv, ab_a_log, ab_dt_bias, ab_o_gain, ab_w_out, na_w_qkv, na_rpb, na_w_out, moe_w_rg, moe_b_rg, moe_w_re, moe_b_re, moe_w_gate, moe_w_up, moe_w_down, final_norm):
    x = (x_prompt.reshape(NC, D), x_sample.reshape(NL, D))
    cond8 = jnp.concatenate([c_ctx[None, :], c, jnp.zeros((8 - 1 - DEC_BATCH, D), F32)], axis=0)
    mods = _ada_params(cond8, mod_w, mod_b).reshape(DEPTH, 8, 6, D)

    tri_tm = jnp.asarray(np.triu(np.ones((TM, TM)), 1), BF16)
    tri_c = jnp.asarray(np.tril(np.ones((CHUNK, CHUNK))), F32)
    half = np.arange(LANES) < A_DH
    jbd = jnp.asarray((half[:, None] == half[None, :]).astype(np.float32))
    fn = final_norm[None, :]

    m3 = mods[0]
    w_in = ab_w_in[0]
    w_in = jnp.concatenate([w_in[:, :2048], w_in[:, 2080:2592], w_in[:, 2048:2080],
                            jnp.zeros((D, AB_COLS - 2592), F32)], axis=1).astype(BF16)
    proj = _modproj(x, m3, norm_mix[0][None, :], w_in, 896)
    gate_p = jnp.broadcast_to(jnp.stack([ab_a_log[0].reshape(-1), ab_dt_bias[0].reshape(-1)])[:, :, None],
                              (2, 2 * A_HEADS, LANES))
    o_gain2 = jnp.tile(ab_o_gain[0], 2)[None, :]
    zeros_state = jnp.zeros((BATCH, A_HEADS, A_DH, A_DH), F32)
    mix_a_c, s_f, s_b = _deltanet(proj, ab_conv[0], gate_p, o_gain2, zeros_state, zeros_state, tri_c, jbd,
                                  seq_len=SEQ, n_seq=BATCH, n_sub=4, row_blk0=0)
    mix_a_l, _, _ = _deltanet(proj, ab_conv[0], gate_p, o_gain2, state_A_fwd[:, 0], state_A_bwd[:, 0], tri_c, jbd,
                              seq_len=DEC_SEQ, n_seq=DEC_BATCH, n_sub=1, row_blk0=NC // DEC_SEQ)

    cc, sc = _dft_mats(LANES)
    ct, st = _dft_mats(SEQ)
    c64, s64 = _dft_mats(64)
    cs = jnp.asarray(np.concatenate([cc, sc], axis=1), BF16)
    dft = jnp.asarray(np.concatenate([ct, -st], axis=1), BF16)
    ca = jnp.asarray(np.concatenate([cc, -sc, -sc, -cc], axis=1), BF16)
    m1 = jnp.asarray(np.concatenate([c64, s64], axis=1), BF16)
    tw_idx = np.arange(64)
    tw_ang = 2.0 * np.pi * (tw_idx[:, None] * tw_idx[None, :]) / DEC_SEQ
    twc = jnp.broadcast_to(jnp.asarray(np.cos(tw_ang), F32)[:, :, None], (64, 64, LANES))
    tws = jnp.broadcast_to(jnp.asarray(np.sin(tw_ang), F32)[:, :, None], (64, 64, LANES))
    mix_b_c = _fnet_ctx(proj, cs, dft)
    mix_b_l = _fnet_latent(proj, ca, m1, twc, tws)

    w_out = ab_w_out[0].astype(BF16)
    x = _hier_moe([(mix_a_c, mix_a_l), (mix_b_c, mix_b_l)], [w_out[:A_WIDTH], w_out[A_WIDTH:]], x, m3,
                  norm_ffn[0][None, :], 0, moe_w_rg[0], moe_b_rg[0], moe_w_re[0], moe_b_re[0],
                  moe_w_gate, moe_w_up, moe_w_down, tri_tm, fn, False)

    m3 = mods[1]
    qkv = _modproj(x, m3, norm_mix[1][None, :], na_w_qkv[0].astype(BF16), 512)
    attn_c = _na_ctx(qkv)
    attn_l = _na_latent(qkv, cache_k[:, 0].reshape(DEC_BATCH, 256, D), cache_v[:, 0].reshape(DEC_BATCH, 256, D),
                        _rpb_tables(na_rpb[0]))
    y_c, y_l = _hier_moe([(attn_c, attn_l)], [na_w_out[0].astype(BF16)], x, m3, norm_ffn[1][None, :], 1, moe_w_rg[1], moe_b_rg[1], moe_w_re[1], moe_b_re[1],
                         moe_w_gate, moe_w_up, moe_w_down, tri_tm, fn, True)

    new_k = qkv[:NC, D:2 * D].reshape(BATCH, 1, SEQ, NA_HEADS, NA_DH)
    new_v = qkv[:NC, 2 * D:].reshape(BATCH, 1, SEQ, NA_HEADS, NA_DH)
    return (y_c.reshape(BATCH, SEQ, D), y_l.reshape(DEC_BATCH, DEC_SEQ, D),
            s_f[:, None], s_b[:, None], new_k, new_v)
```

```python
import functools
import math

import numpy as np
import jax
import jax.numpy as jnp
from jax import lax
from jax.experimental import pallas as pl
from jax.experimental.pallas import tpu as pltpu

F32 = jnp.float32
BF16 = jnp.bfloat16
HIGHEST = lax.Precision.HIGHEST

D = 1024
BATCH, SEQ = 32, 256
DEC_BATCH, DEC_SEQ = 2, 4096
NC = BATCH * SEQ
NL = DEC_BATCH * DEC_SEQ
N = NC + NL
DEPTH = 2
GRID_W = 64
A_DH = 64
A_HEADS = 8
A_WIDTH = 512
CHUNK = 64
B_WIDTH = 512
B_GROUPS = 4
NA_DH = 64
NA_HEADS = 16
WIN_R, WIN_C = 8, 16
N_GROUPS, EXP_PER_GROUP, N_EXPERTS = 4, 8, 32
D_EXPERT = 512
EPS = 1e-6

LANES = 128
TM = 256
MOE_ROWS = 256
N_SLOT_BLOCKS = (2 * N) // MOE_ROWS + N_EXPERTS
AB_COLS = 2688
VMEM_LIMIT = 56 * 1024 * 1024
EXPERT_VMEM_LIMIT = 60 * 1024 * 1024
NEG = -1e30


def _cparams(sem):
    return pltpu.CompilerParams(dimension_semantics=sem, vmem_limit_bytes=VMEM_LIMIT)


def _mm(a, b):
    return jnp.dot(a.astype(BF16), b.astype(BF16), preferred_element_type=F32)


def _mm_nt(a, b):
    return lax.dot_general(a.astype(BF16), b.astype(BF16), (((1,), (1,)), ((), ())),
                           preferred_element_type=F32)


SOLVE_BLK = 16
NA_CTX_PAIRS = 2
NA_ROW_UNROLL = 8
TERM_UNROLL = 4


def _unit_lower_solve(a_mat, rhs, same_blk, eye):
    dg = jnp.where(same_blk, a_mat, 0.0)
    p = -dg
    dinv = eye + p
    for _ in range(int(math.log2(SOLVE_BLK)) - 1):
        p = _mm(p, p)
        dinv = dinv + _mm(p, dinv)
    mp = -_mm(dinv, a_mat - dg)
    y = _mm(dinv, rhs)
    y = y + _mm(mp, y)
    for _ in range(int(math.log2(a_mat.shape[0] // SOLVE_BLK)) - 1):
        mp = _mm(mp, mp)
        y = y + _mm(mp, y)
    return y


def _unit_lower_solve_many(a_mats, rhs, same_blk, eye):
    off = [jnp.where(same_blk, 0.0, a).astype(BF16) for a in a_mats]
    p = [jnp.where(same_blk, -a, 0.0).astype(BF16) for a in a_mats]
    dinv = [(eye + x.astype(F32)).astype(BF16) for x in p]
    for _ in range(int(math.log2(SOLVE_BLK)) - 1):
        p = [_mm(x, x).astype(BF16) for x in p]
        dinv = [(di.astype(F32) + _mm(x, di)).astype(BF16) for x, di in zip(p, dinv)]
    mp = [(-_mm(di, o)).astype(BF16) for di, o in zip(dinv, off)]
    y = [_mm(di, r) for di, r in zip(dinv, rhs)]
    y = [yi + _mm(m, yi) for m, yi in zip(mp, y)]
    for _ in range(int(math.log2(a_mats[0].shape[0] // SOLVE_BLK)) - 1):
        mp = [_mm(m, m).astype(BF16) for m in mp]
        y = [yi + _mm(m, yi) for m, yi in zip(mp, y)]
    return y


def _mm_split(a, b, parts, split_rhs=False):
    x = b if split_rhs else a
    acc = None
    for _ in range(parts):
        piece = x.astype(BF16)
        x = x - piece.astype(F32)
        term = (jnp.dot(a.astype(BF16), piece, preferred_element_type=F32) if split_rhs
                else jnp.dot(piece, b.astype(BF16), preferred_element_type=F32))
        acc = term if acc is None else acc + term
    return acc


def _mm_hi(a, b):
    return jnp.dot(a, b, preferred_element_type=F32, precision=HIGHEST)


def _silu(x):
    return x * jax.nn.sigmoid(x)


def _bf16_bits(x):
    b = lax.bitcast_convert_type(x, jnp.int32)
    return b + 0x7FFF + (lax.shift_right_logical(b, jnp.int32(16)) & 1)


_HIGH16 = -65536


def _pack_bf16_pairs(a, b):
    return lax.shift_right_logical(_bf16_bits(a), jnp.int32(16)) | (_bf16_bits(b) & _HIGH16)


def _unpack_bf16_pairs(p):
    a = lax.bitcast_convert_type(lax.shift_left(p, jnp.int32(16)), F32)
    b = lax.bitcast_convert_type(p & _HIGH16, F32)
    return a.astype(BF16), b.astype(BF16)


def _cond_row(i):
    return jnp.where(i < NC // TM, 0, 1 + (i - NC // TM) // (DEC_SEQ // TM))


def _modulated_norm(x, m_ref, g_ref, shift_idx, scale_idx):
    ms = jnp.mean(x * x, axis=-1, keepdims=True)
    y = x * lax.rsqrt(ms + EPS) * g_ref[...]
    return y * (1.0 + m_ref[scale_idx:scale_idx + 1, :]) + m_ref[shift_idx:shift_idx + 1, :]


def _ada_kernel(cond_ref, w_ref, b_ref, o_ref):
    o_ref[...] = _mm_hi(_silu(cond_ref[...]), w_ref[...]) + b_ref[...]


def _ada_params(cond8, mod_w, mod_b):
    tn = 1536
    return pl.pallas_call(
        _ada_kernel,
        grid=(DEPTH, 6 * D // tn),
        in_specs=[pl.BlockSpec((8, D), lambda l, j: (0, 0)),
                  pl.BlockSpec((None, D, tn), lambda l, j: (l, 0, j)),
                  pl.BlockSpec((None, 1, tn), lambda l, j: (l, 0, j))],
        out_specs=pl.BlockSpec((None, 8, tn), lambda l, j: (l, 0, j)),
        out_shape=jax.ShapeDtypeStruct((DEPTH, 8, 6 * D), F32),
        compiler_params=_cparams(("arbitrary", "arbitrary")),
        name="ada_params",
    )(cond8, mod_w, mod_b.reshape(DEPTH, 1, 6 * D))


def _token_specs(x):
    nct = NC // TM
    if isinstance(x, tuple):
        return ([pl.BlockSpec((TM, x[0].shape[1]), lambda i: (jnp.minimum(i, nct - 1), 0)),
                 pl.BlockSpec((TM, x[1].shape[1]), lambda i: (jnp.maximum(i - nct, 0), 0))], list(x))
    return [pl.BlockSpec((TM, x.shape[1]), lambda i: (i, 0))], [x]


def _token_rows(refs):
    if len(refs) == 1:
        return refs[0][...]
    return jnp.where(pl.program_id(0) < NC // TM, refs[0][...], refs[1][...])


def _modproj_kernel(*refs, n_chunk):
    m_ref, g_ref, w_ref, o_ref = refs[-4:]
    hb = _modulated_norm(_token_rows(refs[:-4]), m_ref, g_ref, 0, 1).astype(BF16)
    for j in range(o_ref.shape[1] // n_chunk):
        sl = slice(j * n_chunk, (j + 1) * n_chunk)
        o_ref[:, sl] = jnp.dot(hb, w_ref[:, sl], preferred_element_type=F32)


def _modproj(x, m3, gain, w_bf16, n_chunk):
    nout = w_bf16.shape[1]
    x_specs, x_args = _token_specs(x)
    return pl.pallas_call(
        functools.partial(_modproj_kernel, n_chunk=n_chunk),
        grid=(N // TM,),
        in_specs=x_specs + [pl.BlockSpec((None, 6, D), lambda i: (_cond_row(i), 0, 0)),
                            pl.BlockSpec((1, D), lambda i: (0, 0)),
                            pl.BlockSpec((D, nout), lambda i: (0, 0))],
        out_specs=pl.BlockSpec((TM, nout), lambda i: (i, 0)),
        out_shape=jax.ShapeDtypeStruct((N, nout), F32),
        compiler_params=_cparams(("arbitrary",)),
        name="modproj",
    )(*x_args, m3, gain, w_bf16)


def _deltanet_kernel(q_ref, k_ref, v_ref, z_ref, ab_ref, cq_ref, ck_ref, cv_ref, gp_ref, og_ref,
                     s0f_ref, s0b_ref, tri_ref, jbd_ref, o_ref, sf_ref, sb_ref,
                     u_s, w_s, qd_s, at_s, kt_s, ge_s, st_s, ob, *, seq_len, n_sub):
    hp = pl.program_id(1)
    C = CHUNK
    nc = seq_len // C
    nct = n_sub * nc
    P = LANES
    lane = lax.broadcasted_iota(jnp.int32, (C, P), 1)
    row = lax.broadcasted_iota(jnp.int32, (C, P), 0)
    first_head = lane < A_DH
    ri = lax.broadcasted_iota(jnp.int32, (P, P), 0)
    ci = lax.broadcasted_iota(jnp.int32, (P, P), 1)
    same_head = (ri < C) == (ci < C)
    same_blk = (ri // SOLVE_BLK) == (ci // SOLVE_BLK)
    eye = jnp.where(ri == ci, 1.0, 0.0)
    jbd = jbd_ref[...]
    lincl = tri_ref[...]
    cum_b = [lincl.T.astype(BF16), lincl.astype(BF16)]
    incl_m = [jnp.logical_and(same_head, ri >= ci), jnp.logical_and(same_head, ri <= ci)]
    strict_m = [jnp.logical_and(same_head, ri > ci), jnp.logical_and(same_head, ri < ci)]
    neg_a = -jnp.exp(gp_ref[0])
    dt_b = gp_ref[1]

    def conv_silu(ref, w_ref, c):
        base = pl.multiple_of(c * C, C)
        cs = c % nc
        xc = ref[pl.ds(base, C), :]
        pbase = pl.multiple_of(jnp.maximum(base - 8, 0), 8)
        nbase = pl.multiple_of(jnp.minimum(base + C, n_sub * seq_len - 8), 8)
        prev_row = ref[pl.ds(pbase, 8), :][7:8, :] * jnp.where(cs > 0, 1.0, 0.0)
        next_row = ref[pl.ds(nbase, 8), :][0:1, :] * jnp.where(cs < nc - 1, 1.0, 0.0)
        x_prev = jnp.where(row == 0, prev_row, pltpu.roll(xc, 1, 0))
        x_next = jnp.where(row == C - 1, next_row, pltpu.roll(xc, C - 1, 0))
        y = w_ref[0:1, :] * x_prev + w_ref[1:2, :] * xc + w_ref[2:3, :] * x_next
        return _silu(y)

    def stack(x):
        return jnp.concatenate([jnp.where(first_head, x, 0.0), jnp.where(first_head, 0.0, x)], axis=0)

    def chunk_inputs(c):
        base = pl.multiple_of(c * C, C)
        q = conv_silu(q_ref, cq_ref, c)
        k = conv_silu(k_ref, ck_ref, c)
        v = conv_silu(v_ref, cv_ref, c)
        return q, k, v, ab_ref[pl.ds(base, C), :].T

    sub8 = lax.broadcasted_iota(jnp.int32, (A_HEADS, C), 0)

    def pair_row(x8):
        r0 = jnp.sum(jnp.where(sub8 == 2 * hp, x8, 0.0), axis=0, keepdims=True)
        r1 = jnp.sum(jnp.where(sub8 == 2 * hp + 1, x8, 0.0), axis=0, keepdims=True)
        return jnp.concatenate([r0, r1], axis=1)

    def chain_gates(ab_t, d):
        a8 = ab_t[d * A_HEADS:(d + 1) * A_HEADS, :]
        b8 = ab_t[(2 + d) * A_HEADS:(3 + d) * A_HEADS, :]
        g8 = neg_a[d * A_HEADS:(d + 1) * A_HEADS, :C] * jax.nn.softplus(a8 + dt_b[d * A_HEADS:(d + 1) * A_HEADS, :C])
        gc8 = _mm_split(g8, cum_b[d], 3)
        tot8 = jnp.broadcast_to(jnp.sum(g8, axis=-1, keepdims=True), (A_HEADS, C))
        gc_row, beta_row, tot_row = pair_row(gc8), pair_row(jax.nn.sigmoid(b8)), pair_row(tot8)
        cols = jnp.concatenate([gc_row, beta_row, tot_row, jnp.zeros((5, P), F32)], axis=0).T
        return gc_row, tot_row, cols[:, 0:1], cols[:, 1:2], cols[:, 2:3]

    def terms_body(j, carry):
        cs = [TERM_UNROLL * j + t for t in range(TERM_UNROLL)]
        ins = [chunk_inputs(c) for c in cs]
        qsq = [_mm_split(x[0] * x[0], jbd, 2) for x in ins]
        ksq = [_mm_split(x[1] * x[1], jbd, 2) for x in ins]
        qs = [x[0] * lax.rsqrt(s + EPS) * (A_DH ** -0.5) for x, s in zip(ins, qsq)]
        ks = [x[1] * lax.rsqrt(s + EPS) for x, s in zip(ins, ksq)]
        qst = [stack(x) for x in qs]
        kst = [stack(x) for x in ks]
        vst = [stack(x[2]) for x in ins]
        kst_t = [x.T for x in kst]
        kk = [_mm_nt(x, x) for x in kst]
        qk = [_mm_nt(x, y) for x, y in zip(qst, kst)]
        chains = [(t, d) for t in range(TERM_UNROLL) for d in range(2)]
        gates = [chain_gates(ins[t][3], d) for t, d in chains]
        decay, e_gc = [], []
        for (t, d), (gc_row, tot_row, gc_col, beta_col, tot_col) in zip(chains, gates):
            diff = jnp.broadcast_to(gc_col, (P, P)) - jnp.broadcast_to(gc_row, (P, P))
            decay.append(jnp.where(incl_m[d], jnp.exp(jnp.where(incl_m[d], diff, 0.0)), 0.0))
            e_gc.append(jnp.exp(gc_col))
        a_mats = [jnp.where(strict_m[d], g[3] * kk[t] * dc, 0.0) for (t, d), g, dc in zip(chains, gates, decay)]
        rhs = [vst[t] * g[3] + pltpu.roll(kst[t] * (g[3] * e), A_DH, 1)
               for (t, d), g, e in zip(chains, gates, e_gc)]
        xs = _unit_lower_solve_many(a_mats, rhs, same_blk, eye)
        for (t, d), x, g, e, dc in zip(chains, xs, gates, e_gc, decay):
            c = cs[t]
            gc_row, tot_row = g[0], g[1]
            u_s[d, c] = jnp.where(same_head, x, 0.0).astype(BF16)
            w_s[d, c] = pltpu.roll(jnp.where(same_head, 0.0, x), A_DH, 1).astype(BF16)
            qd_s[d, c] = (qst[t] * e).astype(BF16)
            at_s[d, c] = jnp.where(incl_m[d], qk[t] * dc, 0.0).astype(BF16)
            kt_s[d, c] = (kst_t[t] * jnp.exp(tot_row - gc_row)).astype(BF16)
            ge_s[d, c] = jnp.broadcast_to(jnp.exp(tot_row), (8, P))
        return carry

    lax.fori_loop(0, nct // TERM_UNROLL, terms_body, 0)

    def block_diag(s2):
        z = jnp.zeros((A_DH, A_DH), F32)
        return jnp.concatenate([jnp.concatenate([s2[0], z], axis=1),
                                jnp.concatenate([z, s2[1]], axis=1)], axis=0)

    for s in range(n_sub):
        st_s[2 * s] = block_diag(s0f_ref[s])
        st_s[2 * s + 1] = block_diag(s0b_ref[s])

    def scan_body(i, carry):
        chains = [(s, d, s * nc + (i if d == 0 else nc - 1 - i)) for s in range(n_sub) for d in range(2)]
        dot = functools.partial(jnp.dot, preferred_element_type=F32)
        s_bd = [st_s[2 * s + d] for s, d, c in chains]
        sb16 = [x.astype(BF16) for x in s_bd]
        ws = [dot(w_s[d, c], sb) for (s, d, c), sb in zip(chains, sb16)]
        qs_ = [dot(qd_s[d, c], sb) for (s, d, c), sb in zip(chains, sb16)]
        vb = [(u_s[d, c].astype(F32) - x).astype(BF16) for (s, d, c), x in zip(chains, ws)]
        av = [dot(at_s[d, c], x) for (s, d, c), x in zip(chains, vb)]
        kv = [dot(kt_s[d, c], x) for (s, d, c), x in zip(chains, vb)]
        for (s, d, c), sb, q_, a_, k_ in zip(chains, s_bd, qs_, av, kv):
            st_s[2 * s + d] = sb * ge_s[d, c][0:1, :] + k_
            o_st = q_ + a_
            dst = o_ref if d == 0 else ob
            dst[pl.ds(pl.multiple_of(c * C, C), C), :] = o_st[:C] + o_st[C:]
        return carry

    lax.fori_loop(0, nc, scan_body, 0)

    for s in range(n_sub):
        for d, ref in ((0, sf_ref), (1, sb_ref)):
            s_bd = st_s[2 * s + d]
            ref[s, 0] = s_bd[:A_DH, :A_DH]
            ref[s, 1] = s_bd[A_DH:, A_DH:]

    def finish(j, carry):
        bases = [pl.multiple_of((TERM_UNROLL * j + t) * C, C) for t in range(TERM_UNROLL)]
        o = [o_ref[pl.ds(b, C), :] + ob[pl.ds(b, C), :] for b in bases]
        ms = [_mm_split(x * x, jbd, 2) * (1.0 / A_DH) for x in o]
        for b, x, m in zip(bases, o, ms):
            o_ref[pl.ds(b, C), :] = x * lax.rsqrt(m + EPS) * og_ref[...] * _silu(z_ref[pl.ds(b, C), :])
        return carry

    lax.fori_loop(0, nct // TERM_UNROLL, finish, 0)


def _deltanet(proj, conv_w, gate_p, o_gain2, s0f, s0b, tri, jbd, *, seq_len, n_seq, n_sub, row_blk0):
    rows = n_sub * seq_len
    nct = rows // CHUNK
    rb = lambda b: row_blk0 + b
    col = lambda off: (lambda b, hp: (rb(b), off + hp))
    st_spec = pl.BlockSpec((n_sub, 2, A_DH, A_DH), lambda b, hp: (b, hp, 0, 0))
    in_specs = [pl.BlockSpec((rows, LANES), col(0)),
                pl.BlockSpec((rows, LANES), col(4)),
                pl.BlockSpec((rows, LANES), col(8)),
                pl.BlockSpec((rows, LANES), col(12)),
                pl.BlockSpec((rows, LANES), lambda b, hp: (rb(b), 20)),
                pl.BlockSpec((3, LANES), lambda b, hp: (0, hp)),
                pl.BlockSpec((3, LANES), lambda b, hp: (0, 4 + hp)),
                pl.BlockSpec((3, LANES), lambda b, hp: (0, 8 + hp)),
                pl.BlockSpec((2, 2 * A_HEADS, LANES), lambda b, hp: (0, 0, 0)),
                pl.BlockSpec((1, LANES), lambda b, hp: (0, 0)),
                st_spec, st_spec,
                pl.BlockSpec((CHUNK, CHUNK), lambda b, hp: (0, 0)),
                pl.BlockSpec((LANES, LANES), lambda b, hp: (0, 0))]
    args = [proj, proj, proj, proj, proj, conv_w, conv_w, conv_w, gate_p, o_gain2, s0f, s0b, tri, jbd]
    st_shape = jax.ShapeDtypeStruct((n_seq, A_HEADS, A_DH, A_DH), F32)
    tile = lambda dt: pltpu.VMEM((2, nct, LANES, LANES), dt)
    return pl.pallas_call(
        functools.partial(_deltanet_kernel, seq_len=seq_len, n_sub=n_sub),
        grid=(n_seq // n_sub, A_HEADS // 2),
        in_specs=in_specs,
        out_specs=[pl.BlockSpec((rows, LANES), lambda b, hp: (b, hp)), st_spec, st_spec],
        out_shape=[jax.ShapeDtypeStruct((n_seq * seq_len, A_WIDTH), F32), st_shape, st_shape],
        scratch_shapes=[tile(BF16), tile(BF16), tile(BF16), tile(BF16), tile(BF16),
                        pltpu.VMEM((2, nct, 8, LANES), F32),
                        pltpu.VMEM((2 * n_sub, LANES, LANES), F32),
                        pltpu.VMEM((rows, LANES), F32)],
        compiler_params=_cparams(("arbitrary", "arbitrary")),
        name="deltanet",
    )(*args)


def _dft_mats(n):
    idx = np.arange(n)
    ang = 2.0 * np.pi * ((idx[:, None] * idx[None, :]) % n) / n
    return np.cos(ang), np.sin(ang)


def _fnet_ctx_kernel(u_ref, cs_ref, dft_ref, o_ref):
    norm = 1.0 / math.sqrt(SEQ * LANES)
    for g in range(B_GROUPS):
        sl = slice(g * LANES, (g + 1) * LANES)
        p = _mm(u_ref[:, sl], cs_ref[...])
        stack = jnp.concatenate([p[:, :LANES], p[:, LANES:]], axis=0)
        o_ref[:, sl] = _mm(dft_ref[...], stack) * norm


def _fnet_ctx(proj, cs, dft):
    return pl.pallas_call(
        _fnet_ctx_kernel,
        grid=(BATCH,),
        in_specs=[pl.BlockSpec((SEQ, B_WIDTH), lambda b: (b, 4)),
                  pl.BlockSpec(cs.shape, lambda b: (0, 0)),
                  pl.BlockSpec(dft.shape, lambda b: (0, 0))],
        out_specs=pl.BlockSpec((SEQ, B_WIDTH), lambda b: (b, 0)),
        out_shape=jax.ShapeDtypeStruct((NC, B_WIDTH), F32),
        compiler_params=_cparams(("arbitrary",)),
        name="fnet_ctx",
    )(proj, cs, dft)


FN_SUB = 4


def _fnet_lat1_kernel(u_ref, ca_ref, m1_ref, twc_ref, tws_ref, o_ref):
    r = 64
    for j in range(FN_SUB):
        c = twc_ref[j]
        s = tws_ref[j]
        rows = slice(j * r, (j + 1) * r)
        for g in range(B_GROUPS):
            pa = _mm(u_ref[rows, g * LANES:(g + 1) * LANES], ca_ref[...])
            rhs = jnp.concatenate([pa[:, :2 * LANES], pa[:, 2 * LANES:]], axis=0)
            zz = _mm(m1_ref[...], rhs)
            zr, zi = zz[:, :LANES], zz[:, LANES:]
            o_ref[rows, 2 * g * LANES:(2 * g + 1) * LANES] = (zr * c + zi * s).astype(BF16)
            o_ref[rows, (2 * g + 1) * LANES:(2 * g + 2) * LANES] = (zi * c - zr * s).astype(BF16)


def _fnet_lat2_kernel(z_ref, m1_ref, o_ref):
    r = 64
    norm = 1.0 / math.sqrt(DEC_SEQ * LANES)
    for j in range(FN_SUB):
        rows = slice(j * r, (j + 1) * r)
        for g in range(B_GROUPS):
            rhs = jnp.concatenate([z_ref[rows, 2 * g * LANES:(2 * g + 1) * LANES],
                                   z_ref[rows, (2 * g + 1) * LANES:(2 * g + 2) * LANES]], axis=0)
            o_ref[rows, g * LANES:(g + 1) * LANES] = _mm(m1_ref[...], rhs) * norm


def _fnet_latent(proj, ca, m1, twc, tws):
    r = 64
    u = proj[NC:, 2048:2560].astype(BF16).reshape(DEC_BATCH, r, r, B_WIDTH)
    u = u.transpose(0, 2, 1, 3).reshape(DEC_BATCH * r * r, B_WIDTH)
    steps = DEC_BATCH * r // FN_SUB
    blk = FN_SUB * r
    z = pl.pallas_call(
        _fnet_lat1_kernel,
        grid=(steps,),
        in_specs=[pl.BlockSpec((blk, B_WIDTH), lambda s: (s, 0)),
                  pl.BlockSpec(ca.shape, lambda s: (0, 0)),
                  pl.BlockSpec(m1.shape, lambda s: (0, 0)),
                  pl.BlockSpec((FN_SUB, r, LANES), lambda s: (s % (r // FN_SUB), 0, 0)),
                  pl.BlockSpec((FN_SUB, r, LANES), lambda s: (s % (r // FN_SUB), 0, 0))],
        out_specs=pl.BlockSpec((blk, 2 * B_WIDTH), lambda s: (s, 0)),
        out_shape=jax.ShapeDtypeStruct((NL, 2 * B_WIDTH), BF16),
        compiler_params=_cparams(("arbitrary",)),
        name="fnet_lat1",
    )(u, ca, m1, twc, tws)
    z = z.reshape(DEC_BATCH, r, r, 2 * B_WIDTH).transpose(0, 2, 1, 3).reshape(NL, 2 * B_WIDTH)
    y = pl.pallas_call(
        _fnet_lat2_kernel,
        grid=(steps,),
        in_specs=[pl.BlockSpec((blk, 2 * B_WIDTH), lambda s: (s, 0)),
                  pl.BlockSpec(m1.shape, lambda s: (0, 0))],
        out_specs=pl.BlockSpec((blk, B_WIDTH), lambda s: (s, 0)),
        out_shape=jax.ShapeDtypeStruct((NL, B_WIDTH), F32),
        compiler_params=_cparams(("arbitrary",)),
        name="fnet_lat2",
    )(z, m1)
    return y.reshape(DEC_BATCH, r, r, B_WIDTH).transpose(0, 2, 1, 3).reshape(NL, B_WIDTH)


def _head_masks():
    lane = lax.broadcasted_iota(jnp.int32, (1, LANES), 1)
    return lane < NA_DH


def _attend_many(chains):
    scale = NA_DH ** -0.5
    s = [[_mm_nt(q, k) * scale if b is None else _mm_nt(q, k) * scale + b for k, v, b in kv] for q, kv in chains]
    m = [functools.reduce(jnp.maximum, [jnp.max(x, axis=-1, keepdims=True) for x in xs]) for xs in s]
    p = [[jnp.exp(x - mi) for x in xs] for xs, mi in zip(s, m)]
    l = [sum(jnp.sum(x, axis=-1, keepdims=True) for x in xs) for xs in p]
    o = [sum(_mm(x, v) for x, (k, v, b) in zip(xs, kv)) for xs, (q, kv) in zip(p, chains)]
    return [oi / li for oi, li in zip(o, l)]


def _na_ctx_kernel(q_ref, k_ref, v_ref, o_ref):
    first = _head_masks()
    n_split = 2
    rows = SEQ // n_split
    chains = []
    for p in range(NA_CTX_PAIRS):
        lanes = slice(p * LANES, (p + 1) * LANES)
        k = k_ref[:, lanes].astype(BF16)
        v = v_ref[:, lanes].astype(BF16)
        for j in range(n_split):
            q = q_ref[j * rows:(j + 1) * rows, lanes]
            for a in range(2):
                chains.append((jnp.where(first if a == 0 else jnp.logical_not(first), q, 0.0), [(k, v, None)]))
    outs = _attend_many(chains)
    for p in range(NA_CTX_PAIRS):
        for j in range(n_split):
            i = 2 * (p * n_split + j)
            o_ref[j * rows:(j + 1) * rows, p * LANES:(p + 1) * LANES] = jnp.where(first, outs[i], outs[i + 1])


def _na_ctx(qkv):
    w = NA_CTX_PAIRS * LANES
    nblk = D // w
    return pl.pallas_call(
        _na_ctx_kernel,
        grid=(BATCH, nblk),
        in_specs=[pl.BlockSpec((SEQ, w), lambda b, hp: (b, hp)),
                  pl.BlockSpec((SEQ, w), lambda b, hp: (b, nblk + hp)),
                  pl.BlockSpec((SEQ, w), lambda b, hp: (b, 2 * nblk + hp))],
        out_specs=pl.BlockSpec((SEQ, w), lambda b, hp: (b, hp)),
        out_shape=jax.ShapeDtypeStruct((NC, D), F32),
        compiler_params=_cparams(("arbitrary", "arbitrary")),
        name="na_ctx",
    )(qkv, qkv, qkv)


def _na_lat_kernel(q_ref, k_ref, v_ref, kc_ref, vc_ref, tt_ref, o_ref):
    first = _head_masks()
    rows = DEC_SEQ // GRID_W
    kctx = kc_ref[...].astype(BF16)
    vctx = vc_ref[...].astype(BF16)
    nkeys = WIN_R * GRID_W

    def rows_body(j, carry):
        chains = []
        for t in range(NA_ROW_UNROLL):
            r = NA_ROW_UNROLL * j + t
            r0 = jnp.clip(r - WIN_R // 2, 0, rows - WIN_R)
            dr0 = r0 - r + (WIN_R - 1)
            q = q_ref[pl.ds(pl.multiple_of(r * GRID_W, GRID_W), GRID_W), :]
            kbase = pl.multiple_of(r0 * GRID_W, GRID_W)
            kl = k_ref[pl.ds(kbase, nkeys), :].astype(BF16)
            vl = v_ref[pl.ds(kbase, nkeys), :].astype(BF16)
            for a in range(2):
                qm = jnp.where(first if a == 0 else jnp.logical_not(first), q, 0.0)
                bias = jnp.concatenate([tt_ref[a, dr0 + 2 * i] for i in range(WIN_R // 2)], axis=1)
                chains.append((qm, [(kl, vl, bias), (kctx, vctx, None)]))
        outs = _attend_many(chains)
        for t in range(NA_ROW_UNROLL):
            r = NA_ROW_UNROLL * j + t
            o_ref[pl.ds(pl.multiple_of(r * GRID_W, GRID_W), GRID_W), :] = jnp.where(first, outs[2 * t], outs[2 * t + 1])
        return carry

    lax.fori_loop(0, rows // NA_ROW_UNROLL, rows_body, 0)


def _na_latent(qkv, cache_k2, cache_v2, tt2):
    rb0 = NC // DEC_SEQ
    return pl.pallas_call(
        _na_lat_kernel,
        grid=(DEC_BATCH, NA_HEADS // 2),
        in_specs=[pl.BlockSpec((DEC_SEQ, LANES), lambda b, hp: (rb0 + b, hp)),
                  pl.BlockSpec((DEC_SEQ, LANES), lambda b, hp: (rb0 + b, 8 + hp)),
                  pl.BlockSpec((DEC_SEQ, LANES), lambda b, hp: (rb0 + b, 16 + hp)),
                  pl.BlockSpec((None, 256, LANES), lambda b, hp: (b, 0, hp)),
                  pl.BlockSpec((None, 256, LANES), lambda b, hp: (b, 0, hp)),
                  pl.BlockSpec((2, 2 * WIN_R - 2, GRID_W, LANES), lambda b, hp: (hp, 0, 0, 0))],
        out_specs=pl.BlockSpec((DEC_SEQ, LANES), lambda b, hp: (b, hp)),
        out_shape=jax.ShapeDtypeStruct((NL, D), F32),
        compiler_params=_cparams(("arbitrary", "arbitrary")),
        name="na_latent",
    )(qkv, qkv, qkv, cache_k2, cache_v2, tt2)


def _rpb_tables(rpb):
    col = np.arange(GRID_W)
    start = np.clip(col - WIN_C // 2, 0, GRID_W - WIN_C)
    inside = (col[None, :] >= start[:, None]) & (col[None, :] < start[:, None] + WIN_C)
    w = GRID_W
    period = 2 * w - 1
    x = jnp.pad(rpb, ((0, 0), (0, 0), (w - WIN_C, w - WIN_C)))
    flat = jnp.tile(x, (1, 1, w))[:, :, w - 1:w - 1 + w * (period - 1)]
    t = flat.reshape(NA_HEADS, 2 * WIN_R - 1, w, period - 1)[..., :w]
    t = jnp.where(inside[None, None], t, NEG)
    return jnp.concatenate([t[:, :-1], t[:, 1:]], axis=-1)


ROUTE_ROWS = 40


def _router_kernel(*refs, n_in):
    a_refs = refs[:2 * n_in]
    w_refs = refs[2 * n_in:3 * n_in]
    x_refs = refs[3 * n_in:-11]
    m_ref, g_ref, wr_ref, br_ref, tri_ref, xn_ref, xf_ref, ri_ref, rw_ref, cnt_ref, base_scr = refs[-11:]
    i = pl.program_id(0)

    @pl.when(i == 0)
    def _():
        base_scr[...] = jnp.zeros_like(base_scr)
        cnt_ref[...] = jnp.zeros_like(cnt_ref)

    acc = None
    for j, w_ref in enumerate(w_refs):
        part = _mm(_token_rows(a_refs[2 * j:2 * j + 2]), w_ref[...])
        acc = part if acc is None else acc + part
    x_new = _token_rows(x_refs) + m_ref[2:3, :] * acc
    xn_ref[...] = x_new
    h = _modulated_norm(x_new, m_ref, g_ref, 3, 4)
    xf_ref[...] = _pack_bf16_pairs(h[:, :D // 2], h[:, D // 2:])
    logits = lax.dot_general(wr_ref[...], h, (((1,), (1,)), ((), ())), preferred_element_type=F32,
                             precision=HIGHEST) + br_ref[:, 0:1]
    row = lax.broadcasted_iota(jnp.int32, logits.shape, 0)
    cmax = lambda x: jnp.max(x, axis=0, keepdims=True)
    cmin = lambda x: jnp.min(x, axis=0, keepdims=True)
    csum = lambda x: jnp.sum(x, axis=0, keepdims=True)

    gmask = row < N_GROUPS
    mg = cmax(jnp.where(gmask, logits, NEG))
    eg = jnp.where(gmask, jnp.exp(jnp.where(gmask, logits - mg, NEG)), 0.0)
    pg = eg / csum(eg)
    p_grp = cmax(pg)
    grp = cmin(jnp.where(jnp.logical_and(gmask, pg == p_grp), row, ROUTE_ROWS))
    lo = N_GROUPS + grp * EXP_PER_GROUP
    emask = jnp.logical_and(row >= lo, row < lo + EXP_PER_GROUP)
    me = cmax(jnp.where(emask, logits, NEG))
    ee = jnp.where(emask, jnp.exp(jnp.where(emask, logits - me, NEG)), 0.0)
    pe = ee / csum(ee)
    p1 = cmax(pe)
    i1 = cmin(jnp.where(jnp.logical_and(emask, pe == p1), row, ROUTE_ROWS))
    m2 = jnp.logical_and(emask, row != i1)
    p2 = cmax(jnp.where(m2, pe, -1.0))
    i2 = cmin(jnp.where(jnp.logical_and(m2, pe == p2), row, ROUTE_ROWS))
    den = p1 + p2
    w1 = p_grp * p1 / den
    w2 = p_grp * p2 / den

    sel1 = row == i1
    sel2 = row == i2
    oh = jnp.where(jnp.logical_or(sel1, sel2), 1.0, 0.0).astype(BF16)
    before = jnp.dot(oh, tri_ref[...], preferred_element_type=F32) + base_scr[:, 0:1]
    rank1 = csum(jnp.where(sel1, before, 0.0))
    rank2 = csum(jnp.where(sel2, before, 0.0))
    base_scr[...] = base_scr[...] + jnp.sum(oh.astype(F32), axis=1, keepdims=True)
    cnt_ref[...] = cnt_ref[...] + lax.dot_general(jnp.ones((8, TM), BF16), oh, (((1,), (1,)), ((), ())),
                                                  preferred_element_type=F32)
    sub = lax.broadcasted_iota(jnp.int32, (8, TM), 0)
    ri_ref[...] = jnp.where(sub == 0, i1 - N_GROUPS, jnp.where(sub == 1, i2 - N_GROUPS,
                  jnp.where(sub == 2, rank1.astype(jnp.int32), jnp.where(sub == 3, rank2.astype(jnp.int32), 0))))
    rw_ref[...] = jnp.where(sub == 0, w1, jnp.where(sub == 1, w2, 0.0))


def _router(a_pairs, w_list, x, m3, gain, wr_t, br_t, tri_upper):
    in_specs, args = [], []
    for pair in a_pairs:
        specs, ops = _token_specs(pair)
        in_specs += specs
        args += ops
    x_specs, x_args = _token_specs(x)
    in_specs += ([pl.BlockSpec(w.shape, lambda i: (0, 0)) for w in w_list] + x_specs
                 + [pl.BlockSpec((None, 6, D), lambda i: (_cond_row(i), 0, 0)),
                    pl.BlockSpec((1, D), lambda i: (0, 0)),
                    pl.BlockSpec((ROUTE_ROWS, D), lambda i: (0, 0)),
                    pl.BlockSpec((ROUTE_ROWS, LANES), lambda i: (0, 0)),
                    pl.BlockSpec((TM, TM), lambda i: (0, 0))])
    return pl.pallas_call(
        functools.partial(_router_kernel, n_in=len(a_pairs)),
        grid=(N // TM,),
        in_specs=in_specs,
        out_specs=[pl.BlockSpec((TM, D), lambda i: (i, 0)),
                   pl.BlockSpec((TM, D // 2), lambda i: (i, 0)),
                   pl.BlockSpec((8, TM), lambda i: (0, i)),
                   pl.BlockSpec((8, TM), lambda i: (0, i)),
                   pl.BlockSpec((8, ROUTE_ROWS), lambda i: (0, 0))],
        out_shape=[jax.ShapeDtypeStruct((N, D), F32),
                   jax.ShapeDtypeStruct((N, D // 2), jnp.int32),
                   jax.ShapeDtypeStruct((8, N), jnp.int32),
                   jax.ShapeDtypeStruct((8, N), F32),
                   jax.ShapeDtypeStruct((8, ROUTE_ROWS), F32)],
        scratch_shapes=[pltpu.VMEM((ROUTE_ROWS, LANES), F32)],
        compiler_params=_cparams(("arbitrary",)),
        name="router",
    )(*args, *w_list, *x_args, m3, gain, wr_t, br_t, tri_upper)


def _row_copy(src_hbm, row, dst, r, sem):
    return pltpu.make_async_copy(src_hbm.at[pl.ds(row, 1)], dst.at[pl.ds(r, 1)], sem)


def _expert_kernel(be_ref, nu_ref, d_ref, xp_hbm, wg_ref, wu_ref, wd_ref, ys_ref, xres, xbuf, st_ref, sem):
    del be_ref
    b = pl.program_id(0)
    half = D // 2

    def gather(blk, slot, part=0, parts=1):
        base = blk * MOE_ROWS
        for r in range(part * MOE_ROWS // parts, (part + 1) * MOE_ROWS // parts):
            xbuf[slot, pl.ds(r, 1), :] = xres[pl.ds(st_ref[base + r], 1), :]

    @pl.when(b == 0)
    def _():
        cp = pltpu.make_async_copy(xp_hbm, xres, sem.at[0])
        cp.start()

        def clear(j, c):
            for t in range(32):
                st_ref[32 * j + t] = 0
            return c

        lax.fori_loop(0, N_SLOT_BLOCKS * MOE_ROWS // 32, clear, 0)

        def invert(j, c):
            slots = [d_ref[16 * j + t] for t in range(16)]
            for t in range(16):
                st_ref[slots[t]] = 8 * j + t // 2
            return c

        lax.fori_loop(0, 2 * N // 16, invert, 0)
        cp.wait()
        gather(0, 0)

    @pl.when(b < nu_ref[0])
    def _():
        nxt = (jnp.minimum(b + 1, N_SLOT_BLOCKS - 1), (b + 1) % 2)
        x_lo, x_hi = _unpack_bf16_pairs(xbuf[b % 2])
        dot = functools.partial(jnp.dot, preferred_element_type=F32)
        gather(*nxt, 0, 8)
        g = dot(x_lo, wg_ref[:half, :].astype(BF16))
        gather(*nxt, 1, 8)
        g = g + dot(x_hi, wg_ref[half:, :].astype(BF16))
        gather(*nxt, 2, 8)
        u = dot(x_lo, wu_ref[:half, :].astype(BF16))
        gather(*nxt, 3, 8)
        u = u + dot(x_hi, wu_ref[half:, :].astype(BF16))
        hb = (_silu(g) * u).astype(BF16)
        quarter = D // 4
        for j in range(4):
            gather(*nxt, 4 + j, 8)
            cols = slice(j * quarter, (j + 1) * quarter)
            ys_ref[:, cols] = jnp.dot(hb, wd_ref[:, cols].astype(BF16), preferred_element_type=F32)

    @pl.when(b >= nu_ref[0])
    def _():
        ys_ref[...] = jnp.zeros_like(ys_ref)


def _experts(block_e, n_used, dest, xp, w_gate, w_up, w_down, layer):
    grid_spec = pltpu.PrefetchScalarGridSpec(
        num_scalar_prefetch=3,
        grid=(N_SLOT_BLOCKS,),
        in_specs=[pl.BlockSpec(memory_space=pl.ANY),
                  pl.BlockSpec((None, None, D, D_EXPERT), lambda b, be, nu, st: (layer, be[b], 0, 0)),
                  pl.BlockSpec((None, None, D, D_EXPERT), lambda b, be, nu, st: (layer, be[b], 0, 0)),
                  pl.BlockSpec((None, None, D_EXPERT, D), lambda b, be, nu, st: (layer, be[b], 0, 0))],
        out_specs=pl.BlockSpec((MOE_ROWS, D), lambda b, be, nu, st: (b, 0)),
        scratch_shapes=[pltpu.VMEM((N, D // 2), jnp.int32),
                        pltpu.VMEM((2, MOE_ROWS, D // 2), jnp.int32),
                        pltpu.SMEM((N_SLOT_BLOCKS * MOE_ROWS,), jnp.int32),
                        pltpu.SemaphoreType.DMA((1,))])
    return pl.pallas_call(
        _expert_kernel,
        grid_spec=grid_spec,
        out_shape=jax.ShapeDtypeStruct((N_SLOT_BLOCKS * MOE_ROWS, D), F32),
        compiler_params=pltpu.CompilerParams(dimension_semantics=("arbitrary",), vmem_limit_bytes=EXPERT_VMEM_LIMIT),
        name="experts",
    )(block_e, n_used, dest, xp, w_gate, w_up, w_down)


def _combine_kernel(d_ref, ys_hbm, x_ref, m_ref, rw_ref, fn_ref, o_ref, buf, sem, *, final, tile0, n_tiles):
    i = pl.program_id(0)

    def issue(tile, slot):
        base = (tile0 + tile) * (2 * TM)
        for r in range(TM):
            for kk in range(2):
                _row_copy(ys_hbm, d_ref[base + 2 * r + kk], buf.at[slot, kk], r, sem.at[slot]).start()

    @pl.when(i == 0)
    def _():
        issue(0, 0)

    @pl.when(i + 1 < n_tiles)
    def _():
        issue(i + 1, (i + 1) % 2)

    slot = i % 2
    for kk in range(2):
        pltpu.make_async_copy(ys_hbm.at[pl.ds(0, TM)], buf.at[slot, kk], sem.at[slot]).wait()
    w = rw_ref[...]
    y = w[:, 0:1] * buf[slot, 0] + w[:, 1:2] * buf[slot, 1]
    out = x_ref[...] + m_ref[5:6, :] * y
    if final:
        ms = jnp.mean(out * out, axis=-1, keepdims=True)
        out = out * lax.rsqrt(ms + EPS) * fn_ref[...]
    o_ref[...] = out


def _combine(dest_flat, ys, x, m3, rw, final_norm, final, tile0=0, n_tiles=N // TM):
    grid_spec = pltpu.PrefetchScalarGridSpec(
        num_scalar_prefetch=1,
        grid=(n_tiles,),
        in_specs=[pl.BlockSpec(memory_space=pl.ANY),
                  pl.BlockSpec((TM, D), lambda i, d: (tile0 + i, 0)),
                  pl.BlockSpec((None, 6, D), lambda i, d: (_cond_row(tile0 + i), 0, 0)),
                  pl.BlockSpec((TM, 2), lambda i, d: (tile0 + i, 0)),
                  pl.BlockSpec((1, D), lambda i, d: (0, 0))],
        out_specs=pl.BlockSpec((TM, D), lambda i, d: (i, 0)),
        scratch_shapes=[pltpu.VMEM((2, 2, TM, D), F32), pltpu.SemaphoreType.DMA((2,))])
    return pl.pallas_call(
        functools.partial(_combine_kernel, final=final, tile0=tile0, n_tiles=n_tiles),
        grid_spec=grid_spec,
        out_shape=jax.ShapeDtypeStruct((n_tiles * TM, D), F32),
        compiler_params=_cparams(("arbitrary",)),
        name="combine",
    )(dest_flat, ys, x, m3, rw, final_norm)


def _hier_moe(a_pairs, w_list, x, m3, gain, layer, w_rg, b_rg, w_re, b_re, w_gate, w_up, w_down, tri_tm,
              final_norm, final):
    pad = ROUTE_ROWS - N_GROUPS - N_EXPERTS
    wr_t = jnp.concatenate([w_rg.T, w_re.transpose(0, 2, 1).reshape(N_EXPERTS, D), jnp.zeros((pad, D), F32)], axis=0)
    br_t = jnp.broadcast_to(jnp.concatenate([b_rg, b_re.reshape(N_EXPERTS), jnp.zeros((pad,), F32)])[:, None],
                            (ROUTE_ROWS, LANES))
    x, xf, ri_t, rw_t, cnt = _router(a_pairs, w_list, x, m3, gain, wr_t, br_t, tri_tm)
    e_idx = ri_t[0:2].T
    rank = ri_t[2:4].T
    rw = rw_t[0:2].T
    counts = cnt[0, N_GROUPS:N_GROUPS + N_EXPERTS].astype(jnp.int32)
    padded = (counts + MOE_ROWS - 1) // MOE_ROWS * MOE_ROWS
    end_pad = jnp.cumsum(padded)
    start_pad = end_pad - padded
    experts = jnp.arange(N_EXPERTS, dtype=jnp.int32)
    start_of = jnp.sum(jnp.where(e_idx[:, :, None] == experts, start_pad, 0), axis=-1)
    dest = (start_of + rank).reshape(-1).astype(jnp.int32)
    block_start = jnp.arange(N_SLOT_BLOCKS, dtype=jnp.int32) * MOE_ROWS
    block_e = jnp.minimum(jnp.sum((end_pad[None, :] <= block_start[:, None]).astype(jnp.int32), axis=1),
                          N_EXPERTS - 1)
    n_used = (end_pad[-1:] // MOE_ROWS).astype(jnp.int32)
    ys = _experts(block_e, n_used, dest, xf, w_gate, w_up, w_down, layer)
    if not final:
        return _combine(dest, ys, x, m3, rw, final_norm, False)
    nct = NC // TM
    return (_combine(dest, ys, x, m3, rw, final_norm, True, 0, nct),
            _combine(dest, ys, x, m3, rw, final_norm, True, nct, N // TM - nct))


def kernel(x_prompt, x_sample, state_A_fwd, state_A_bwd, cache_k, cache_v, c, c_ctx, mod_w, mod_b, norm_mix, norm_ffn, ab_w_in, ab_conv, ab_a_log, ab_dt_bias, ab_o_gain, ab_w_out, na_w_qkv, na_rpb, na_w_out, moe_w_rg, moe_b_rg, moe_w_re, moe_b_re, moe_w_gate, moe_w_up, moe_w_down, final_norm):
    x = (x_prompt.reshape(NC, D), x_sample.reshape(NL, D))
    cond8 = jnp.concatenate([c_ctx[None, :], c, jnp.zeros((8 - 1 - DEC_BATCH, D), F32)], axis=0)
    mods = _ada_params(cond8, mod_w, mod_b).reshape(DEPTH, 8, 6, D)

    tri_tm = jnp.asarray(np.triu(np.ones((TM, TM)), 1), BF16)
    tri_c = jnp.asarray(np.tril(np.ones((CHUNK, CHUNK))), F32)
    half = np.arange(LANES) < A_DH
    jbd = jnp.asarray((half[:, None] == half[None, :]).astype(np.float32))
    fn = final_norm[None, :]

    m3 = mods[0]
    w_in = ab_w_in[0]
    w_in = jnp.concatenate([w_in[:, :2048], w_in[:, 2080:2592], w_in[:, 2048:2080],
                            jnp.zeros((D, AB_COLS - 2592), F32)], axis=1).astype(BF16)
    proj = _modproj(x, m3, norm_mix[0][None, :], w_in, 896)
    gate_p = jnp.broadcast_to(jnp.stack([ab_a_log[0].reshape(-1), ab_dt_bias[0].reshape(-1)])[:, :, None],
                              (2, 2 * A_HEADS, LANES))
    o_gain2 = jnp.tile(ab_o_gain[0], 2)[None, :]
    zeros_state = jnp.zeros((BATCH, A_HEADS, A_DH, A_DH), F32)
    mix_a_c, s_f, s_b = _deltanet(proj, ab_conv[0], gate_p, o_gain2, zeros_state, zeros_state, tri_c, jbd,
                                  seq_len=SEQ, n_seq=BATCH, n_sub=4, row_blk0=0)
    mix_a_l, _, _ = _deltanet(proj, ab_conv[0], gate_p, o_gain2, state_A_fwd[:, 0], state_A_bwd[:, 0], tri_c, jbd,
                              seq_len=DEC_SEQ, n_seq=DEC_BATCH, n_sub=1, row_blk0=NC // DEC_SEQ)

    cc, sc = _dft_mats(LANES)
    ct, st = _dft_mats(SEQ)
    c64, s64 = _dft_mats(64)
    cs = jnp.asarray(np.concatenate([cc, sc], axis=1), BF16)
    dft = jnp.asarray(np.concatenate([ct, -st], axis=1), BF16)
    ca = jnp.asarray(np.concatenate([cc, -sc, -sc, -cc], axis=1), BF16)
    m1 = jnp.asarray(np.concatenate([c64, s64], axis=1), BF16)
    tw_idx = np.arange(64)
    tw_ang = 2.0 * np.pi * (tw_idx[:, None] * tw_idx[None, :]) / DEC_SEQ
    twc = jnp.broadcast_to(jnp.asarray(np.cos(tw_ang), F32)[:, :, None], (64, 64, LANES))
    tws = jnp.broadcast_to(jnp.asarray(np.sin(tw_ang), F32)[:, :, None], (64, 64, LANES))
    mix_b_c = _fnet_ctx(proj, cs, dft)
    mix_b_l = _fnet_latent(proj, ca, m1, twc, tws)

    w_out = ab_w_out[0].astype(BF16)
    x = _hier_moe([(mix_a_c, mix_a_l), (mix_b_c, mix_b_l)], [w_out[:A_WIDTH], w_out[A_WIDTH:]], x, m3,
                  norm_ffn[0][None, :], 0, moe_w_rg[0], moe_b_rg[0], moe_w_re[0], moe_b_re[0],
                  moe_w_gate, moe_w_up, moe_w_down, tri_tm, fn, False)

    m3 = mods[1]
    qkv = _modproj(x, m3, norm_mix[1][None, :], na_w_qkv[0].astype(BF16), 512)
    attn_c = _na_ctx(qkv)
    attn_l = _na_latent(qkv, cache_k[:, 0].reshape(DEC_BATCH, 256, D), cache_v[:, 0].reshape(DEC_BATCH, 256, D),
                        _rpb_tables(na_rpb[0]))
    y_c, y_l = _hier_moe([(attn_c, attn_l)], [na_w_out[0].astype(BF16)], x, m3, norm_ffn[1][None, :], 1, moe_w_rg[1], moe_b_rg[1], moe_w_re[1], moe_b_re[1],
                         moe_w_gate, moe_w_up, moe_w_down, tri_tm, fn, True)

    new_k = qkv[:NC, D:2 * D].reshape(BATCH, 1, SEQ, NA_HEADS, NA_DH)
    new_v = qkv[:NC, 2 * D:].reshape(BATCH, 1, SEQ, NA_HEADS, NA_DH)
    return (y_c.reshape(BATCH, SEQ, D), y_l.reshape(DEC_BATCH, DEC_SEQ, D),
            s_f[:, None], s_b[:, None], new_k, new_v)
```

```python
import functools
import math

import numpy as np
import jax
import jax.numpy as jnp
from jax import lax
from jax.experimental import pallas as pl
from jax.experimental.pallas import tpu as pltpu

F32 = jnp.float32
BF16 = jnp.bfloat16
HIGHEST = lax.Precision.HIGHEST

D = 1024
BATCH, SEQ = 32, 256
DEC_BATCH, DEC_SEQ = 2, 4096
NC = BATCH * SEQ
NL = DEC_BATCH * DEC_SEQ
N = NC + NL
DEPTH = 2
GRID_W = 64
A_DH = 64
A_HEADS = 8
A_WIDTH = 512
CHUNK = 64
B_WIDTH = 512
B_GROUPS = 4
NA_DH = 64
NA_HEADS = 16
WIN_R, WIN_C = 8, 16
N_GROUPS, EXP_PER_GROUP, N_EXPERTS = 4, 8, 32
D_EXPERT = 512
EPS = 1e-6

LANES = 128
TM = 256
MOE_ROWS = 256
N_SLOT_BLOCKS = (2 * N) // MOE_ROWS + N_EXPERTS
AB_COLS = 2688
VMEM_LIMIT = 56 * 1024 * 1024
EXPERT_VMEM_LIMIT = 60 * 1024 * 1024
NEG = -1e30


def _cparams(sem):
    return pltpu.CompilerParams(dimension_semantics=sem, vmem_limit_bytes=VMEM_LIMIT)


def _mm(a, b):
    return jnp.dot(a.astype(BF16), b.astype(BF16), preferred_element_type=F32)


def _mm_nt(a, b):
    return lax.dot_general(a.astype(BF16), b.astype(BF16), (((1,), (1,)), ((), ())),
                           preferred_element_type=F32)


SOLVE_BLK = 16
NA_CTX_PAIRS = 2
NA_ROW_UNROLL = 8
TERM_UNROLL = 4


def _unit_lower_solve(a_mat, rhs, same_blk, eye):
    dg = jnp.where(same_blk, a_mat, 0.0)
    p = -dg
    dinv = eye + p
    for _ in range(int(math.log2(SOLVE_BLK)) - 1):
        p = _mm(p, p)
        dinv = dinv + _mm(p, dinv)
    mp = -_mm(dinv, a_mat - dg)
    y = _mm(dinv, rhs)
    y = y + _mm(mp, y)
    for _ in range(int(math.log2(a_mat.shape[0] // SOLVE_BLK)) - 1):
        mp = _mm(mp, mp)
        y = y + _mm(mp, y)
    return y


def _unit_lower_solve_many(a_mats, rhs, same_blk, eye):
    off = [jnp.where(same_blk, 0.0, a).astype(BF16) for a in a_mats]
    p = [jnp.where(same_blk, -a, 0.0).astype(BF16) for a in a_mats]
    dinv = [(eye + x.astype(F32)).astype(BF16) for x in p]
    for _ in range(int(math.log2(SOLVE_BLK)) - 1):
        p = [_mm(x, x).astype(BF16) for x in p]
        dinv = [(di.astype(F32) + _mm(x, di)).astype(BF16) for x, di in zip(p, dinv)]
    mp = [(-_mm(di, o)).astype(BF16) for di, o in zip(dinv, off)]
    y = [_mm(di, r) for di, r in zip(dinv, rhs)]
    y = [yi + _mm(m, yi) for m, yi in zip(mp, y)]
    for _ in range(int(math.log2(a_mats[0].shape[0] // SOLVE_BLK)) - 1):
        mp = [_mm(m, m).astype(BF16) for m in mp]
        y = [yi + _mm(m, yi) for m, yi in zip(mp, y)]
    return y


def _mm_split(a, b, parts, split_rhs=False):
    x = b if split_rhs else a
    acc = None
    for _ in range(parts):
        piece = x.astype(BF16)
        x = x - piece.astype(F32)
        term = (jnp.dot(a.astype(BF16), piece, preferred_element_type=F32) if split_rhs
                else jnp.dot(piece, b.astype(BF16), preferred_element_type=F32))
        acc = term if acc is None else acc + term
    return acc


def _mm_hi(a, b):
    return jnp.dot(a, b, preferred_element_type=F32, precision=HIGHEST)


def _silu(x):
    return x * jax.nn.sigmoid(x)


def _bf16_bits(x):
    b = lax.bitcast_convert_type(x, jnp.int32)
    return b + 0x7FFF + (lax.shift_right_logical(b, jnp.int32(16)) & 1)


_HIGH16 = -65536


def _pack_bf16_pairs(a, b):
    return lax.shift_right_logical(_bf16_bits(a), jnp.int32(16)) | (_bf16_bits(b) & _HIGH16)


def _unpack_bf16_pairs(p):
    a = lax.bitcast_convert_type(lax.shift_left(p, jnp.int32(16)), F32)
    b = lax.bitcast_convert_type(p & _HIGH16, F32)
    return a.astype(BF16), b.astype(BF16)


def _cond_row(i):
    return jnp.where(i < NC // TM, 0, 1 + (i - NC // TM) // (DEC_SEQ // TM))


def _modulated_norm(x, m_ref, g_ref, shift_idx, scale_idx):
    ms = jnp.mean(x * x, axis=-1, keepdims=True)
    y = x * lax.rsqrt(ms + EPS) * g_ref[...]
    return y * (1.0 + m_ref[scale_idx:scale_idx + 1, :]) + m_ref[shift_idx:shift_idx + 1, :]


def _ada_kernel(cond_ref, w_ref, b_ref, o_ref):
    o_ref[...] = _mm_hi(_silu(cond_ref[...]), w_ref[...]) + b_ref[...]


def _ada_params(cond8, mod_w, mod_b):
    tn = 1536
    return pl.pallas_call(
        _ada_kernel,
        grid=(DEPTH, 6 * D // tn),
        in_specs=[pl.BlockSpec((8, D), lambda l, j: (0, 0)),
                  pl.BlockSpec((None, D, tn), lambda l, j: (l, 0, j)),
                  pl.BlockSpec((None, 1, tn), lambda l, j: (l, 0, j))],
        out_specs=pl.BlockSpec((None, 8, tn), lambda l, j: (l, 0, j)),
        out_shape=jax.ShapeDtypeStruct((DEPTH, 8, 6 * D), F32),
        compiler_params=_cparams(("arbitrary", "arbitrary")),
        name="ada_params",
    )(cond8, mod_w, mod_b.reshape(DEPTH, 1, 6 * D))


def _token_specs(x):
    nct = NC // TM
    if isinstance(x, tuple):
        return ([pl.BlockSpec((TM, x[0].shape[1]), lambda i: (jnp.minimum(i, nct - 1), 0)),
                 pl.BlockSpec((TM, x[1].shape[1]), lambda i: (jnp.maximum(i - nct, 0), 0))], list(x))
    return [pl.BlockSpec((TM, x.shape[1]), lambda i: (i, 0))], [x]


def _token_rows(refs):
    if len(refs) == 1:
        return refs[0][...]
    return jnp.where(pl.program_id(0) < NC // TM, refs[0][...], refs[1][...])


def _modproj_kernel(*refs, n_chunk):
    m_ref, g_ref, w_ref, o_ref = refs[-4:]
    hb = _modulated_norm(_token_rows(refs[:-4]), m_ref, g_ref, 0, 1).astype(BF16)
    for j in range(o_ref.shape[1] // n_chunk):
        sl = slice(j * n_chunk, (j + 1) * n_chunk)
        o_ref[:, sl] = jnp.dot(hb, w_ref[:, sl], preferred_element_type=F32)


def _modproj(x, m3, gain, w_bf16, n_chunk):
    nout = w_bf16.shape[1]
    x_specs, x_args = _token_specs(x)
    return pl.pallas_call(
        functools.partial(_modproj_kernel, n_chunk=n_chunk),
        grid=(N // TM,),
        in_specs=x_specs + [pl.BlockSpec((None, 6, D), lambda i: (_cond_row(i), 0, 0)),
                            pl.BlockSpec((1, D), lambda i: (0, 0)),
                            pl.BlockSpec((D, nout), lambda i: (0, 0))],
        out_specs=pl.BlockSpec((TM, nout), lambda i: (i, 0)),
        out_shape=jax.ShapeDtypeStruct((N, nout), F32),
        compiler_params=_cparams(("arbitrary",)),
        name="modproj",
    )(*x_args, m3, gain, w_bf16)


def _deltanet_kernel(q_ref, k_ref, v_ref, z_ref, ab_ref, cq_ref, ck_ref, cv_ref, gp_ref, og_ref,
                     s0f_ref, s0b_ref, tri_ref, jbd_ref, o_ref, sf_ref, sb_ref,
                     u_s, w_s, qd_s, at_s, kt_s, ge_s, st_s, ob, *, seq_len, n_sub):
    hp = pl.program_id(1)
    C = CHUNK
    nc = seq_len // C
    nct = n_sub * nc
    P = LANES
    lane = lax.broadcasted_iota(jnp.int32, (C, P), 1)
    row = lax.broadcasted_iota(jnp.int32, (C, P), 0)
    first_head = lane < A_DH
    ri = lax.broadcasted_iota(jnp.int32, (P, P), 0)
    ci = lax.broadcasted_iota(jnp.int32, (P, P), 1)
    same_head = (ri < C) == (ci < C)
    same_blk = (ri // SOLVE_BLK) == (ci // SOLVE_BLK)
    eye = jnp.where(ri == ci, 1.0, 0.0)
    jbd = jbd_ref[...]
    lincl = tri_ref[...]
    cum_b = [lincl.T.astype(BF16), lincl.astype(BF16)]
    incl_m = [jnp.logical_and(same_head, ri >= ci), jnp.logical_and(same_head, ri <= ci)]
    strict_m = [jnp.logical_and(same_head, ri > ci), jnp.logical_and(same_head, ri < ci)]
    neg_a = -jnp.exp(gp_ref[0])
    dt_b = gp_ref[1]

    def conv_silu(ref, w_ref, c):
        base = pl.multiple_of(c * C, C)
        cs = c % nc
        xc = ref[pl.ds(base, C), :]
        pbase = pl.multiple_of(jnp.maximum(base - 8, 0), 8)
        nbase = pl.multiple_of(jnp.minimum(base + C, n_sub * seq_len - 8), 8)
        prev_row = ref[pl.ds(pbase, 8), :][7:8, :] * jnp.where(cs > 0, 1.0, 0.0)
        next_row = ref[pl.ds(nbase, 8), :][0:1, :] * jnp.where(cs < nc - 1, 1.0, 0.0)
        x_prev = jnp.where(row == 0, prev_row, pltpu.roll(xc, 1, 0))
        x_next = jnp.where(row == C - 1, next_row, pltpu.roll(xc, C - 1, 0))
        y = w_ref[0:1, :] * x_prev + w_ref[1:2, :] * xc + w_ref[2:3, :] * x_next
        return _silu(y)

    def stack(x):
        return jnp.concatenate([jnp.where(first_head, x, 0.0), jnp.where(first_head, 0.0, x)], axis=0)

    def chunk_inputs(c):
        base = pl.multiple_of(c * C, C)
        q = conv_silu(q_ref, cq_ref, c)
        k = conv_silu(k_ref, ck_ref, c)
        v = conv_silu(v_ref, cv_ref, c)
        return q, k, v, ab_ref[pl.ds(base, C), :].T

    sub8 = lax.broadcasted_iota(jnp.int32, (A_HEADS, C), 0)

    def pair_row(x8):
        r0 = jnp.sum(jnp.where(sub8 == 2 * hp, x8, 0.0), axis=0, keepdims=True)
        r1 = jnp.sum(jnp.where(sub8 == 2 * hp + 1, x8, 0.0), axis=0, keepdims=True)
        return jnp.concatenate([r0, r1], axis=1)

    def chain_gates(ab_t, d):
        a8 = ab_t[d * A_HEADS:(d + 1) * A_HEADS, :]
        b8 = ab_t[(2 + d) * A_HEADS:(3 + d) * A_HEADS, :]
        g8 = neg_a[d * A_HEADS:(d + 1) * A_HEADS, :C] * jax.nn.softplus(a8 + dt_b[d * A_HEADS:(d + 1) * A_HEADS, :C])
        gc8 = _mm_split(g8, cum_b[d], 3)
        tot8 = jnp.broadcast_to(jnp.sum(g8, axis=-1, keepdims=True), (A_HEADS, C))
        gc_row, beta_row, tot_row = pair_row(gc8), pair_row(jax.nn.sigmoid(b8)), pair_row(tot8)
        cols = jnp.concatenate([gc_row, beta_row, tot_row, jnp.zeros((5, P), F32)], axis=0).T
        return gc_row, tot_row, cols[:, 0:1], cols[:, 1:2], cols[:, 2:3]

    def terms_body(j, carry):
        cs = [TERM_UNROLL * j + t for t in range(TERM_UNROLL)]
        ins = [chunk_inputs(c) for c in cs]
        qsq = [_mm_split(x[0] * x[0], jbd, 2) for x in ins]
        ksq = [_mm_split(x[1] * x[1], jbd, 2) for x in ins]
        qs = [x[0] * lax.rsqrt(s + EPS) * (A_DH ** -0.5) for x, s in zip(ins, qsq)]
        ks = [x[1] * lax.rsqrt(s + EPS) for x, s in zip(ins, ksq)]
        qst = [stack(x) for x in qs]
        kst = [stack(x) for x in ks]
        vst = [stack(x[2]) for x in ins]
        kst_t = [x.T for x in kst]
        kk = [_mm_nt(x, x) for x in kst]
        qk = [_mm_nt(x, y) for x, y in zip(qst, kst)]
        chains = [(t, d) for t in range(TERM_UNROLL) for d in range(2)]
        gates = [chain_gates(ins[t][3], d) for t, d in chains]
        decay, e_gc = [], []
        for (t, d), (gc_row, tot_row, gc_col, beta_col, tot_col) in zip(chains, gates):
            diff = jnp.broadcast_to(gc_col, (P, P)) - jnp.broadcast_to(gc_row, (P, P))
            decay.append(jnp.where(incl_m[d], jnp.exp(jnp.where(incl_m[d], diff, 0.0)), 0.0))
            e_gc.append(jnp.exp(gc_col))
        a_mats = [jnp.where(strict_m[d], g[3] * kk[t] * dc, 0.0) for (t, d), g, dc in zip(chains, gates, decay)]
        rhs = [vst[t] * g[3] + pltpu.roll(kst[t] * (g[3] * e), A_DH, 1)
               for (t, d), g, e in zip(chains, gates, e_gc)]
        xs = _unit_lower_solve_many(a_mats, rhs, same_blk, eye)
        for (t, d), x, g, e, dc in zip(chains, xs, gates, e_gc, decay):
            c = cs[t]
            gc_row, tot_row = g[0], g[1]
            u_s[d, c] = jnp.where(same_head, x, 0.0).astype(BF16)
            w_s[d, c] = pltpu.roll(jnp.where(same_head, 0.0, x), A_DH, 1).astype(BF16)
            qd_s[d, c] = (qst[t] * e).astype(BF16)
            at_s[d, c] = jnp.where(incl_m[d], qk[t] * dc, 0.0).astype(BF16)
            kt_s[d, c] = (kst_t[t] * jnp.exp(tot_row - gc_row)).astype(BF16)
            ge_s[d, c] = jnp.broadcast_to(jnp.exp(tot_row), (8, P))
        return carry

    lax.fori_loop(0, nct // TERM_UNROLL, terms_body, 0)

    def block_diag(s2):
        z = jnp.zeros((A_DH, A_DH), F32)
        return jnp.concatenate([jnp.concatenate([s2[0], z], axis=1),
                                jnp.concatenate([z, s2[1]], axis=1)], axis=0)

    for s in range(n_sub):
        st_s[2 * s] = block_diag(s0f_ref[s])
        st_s[2 * s + 1] = block_diag(s0b_ref[s])

    def scan_body(i, carry):
        chains = [(s, d, s * nc + (i if d == 0 else nc - 1 - i)) for s in range(n_sub) for d in range(2)]
        dot = functools.partial(jnp.dot, preferred_element_type=F32)
        s_bd = [st_s[2 * s + d] for s, d, c in chains]
        sb16 = [x.astype(BF16) for x in s_bd]
        ws = [dot(w_s[d, c], sb) for (s, d, c), sb in zip(chains, sb16)]
        qs_ = [dot(qd_s[d, c], sb) for (s, d, c), sb in zip(chains, sb16)]
        vb = [(u_s[d, c].astype(F32) - x).astype(BF16) for (s, d, c), x in zip(chains, ws)]
        av = [dot(at_s[d, c], x) for (s, d, c), x in zip(chains, vb)]
        kv = [dot(kt_s[d, c], x) for (s, d, c), x in zip(chains, vb)]
        for (s, d, c), sb, q_, a_, k_ in zip(chains, s_bd, qs_, av, kv):
            st_s[2 * s + d] = sb * ge_s[d, c][0:1, :] + k_
            o_st = q_ + a_
            dst = o_ref if d == 0 else ob
            dst[pl.ds(pl.multiple_of(c * C, C), C), :] = o_st[:C] + o_st[C:]
        return carry

    lax.fori_loop(0, nc, scan_body, 0)

    for s in range(n_sub):
        for d, ref in ((0, sf_ref), (1, sb_ref)):
            s_bd = st_s[2 * s + d]
            ref[s, 0] = s_bd[:A_DH, :A_DH]
            ref[s, 1] = s_bd[A_DH:, A_DH:]

    def finish(j, carry):
        bases = [pl.multiple_of((TERM_UNROLL * j + t) * C, C) for t in range(TERM_UNROLL)]
        o = [o_ref[pl.ds(b, C), :] + ob[pl.ds(b, C), :] for b in bases]
        ms = [_mm_split(x * x, jbd, 2) * (1.0 / A_DH) for x in o]
        for b, x, m in zip(bases, o, ms):
            o_ref[pl.ds(b, C), :] = x * lax.rsqrt(m + EPS) * og_ref[...] * _silu(z_ref[pl.ds(b, C), :])
        return carry

    lax.fori_loop(0, nct // TERM_UNROLL, finish, 0)


def _deltanet(proj, conv_w, gate_p, o_gain2, s0f, s0b, tri, jbd, *, seq_len, n_seq, n_sub, row_blk0):
    rows = n_sub * seq_len
    nct = rows // CHUNK
    rb = lambda b: row_blk0 + b
    col = lambda off: (lambda b, hp: (rb(b), off + hp))
    st_spec = pl.BlockSpec((n_sub, 2, A_DH, A_DH), lambda b, hp: (b, hp, 0, 0))
    in_specs = [pl.BlockSpec((rows, LANES), col(0)),
                pl.BlockSpec((rows, LANES), col(4)),
                pl.BlockSpec((rows, LANES), col(8)),
                pl.BlockSpec((rows, LANES), col(12)),
                pl.BlockSpec((rows, LANES), lambda b, hp: (rb(b), 20)),
                pl.BlockSpec((3, LANES), lambda b, hp: (0, hp)),
                pl.BlockSpec((3, LANES), lambda b, hp: (0, 4 + hp)),
                pl.BlockSpec((3, LANES), lambda b, hp: (0, 8 + hp)),
                pl.BlockSpec((2, 2 * A_HEADS, LANES), lambda b, hp: (0, 0, 0)),
                pl.BlockSpec((1, LANES), lambda b, hp: (0, 0)),
                st_spec, st_spec,
                pl.BlockSpec((CHUNK, CHUNK), lambda b, hp: (0, 0)),
                pl.BlockSpec((LANES, LANES), lambda b, hp: (0, 0))]
    args = [proj, proj, proj, proj, proj, conv_w, conv_w, conv_w, gate_p, o_gain2, s0f, s0b, tri, jbd]
    st_shape = jax.ShapeDtypeStruct((n_seq, A_HEADS, A_DH, A_DH), F32)
    tile = lambda dt: pltpu.VMEM((2, nct, LANES, LANES), dt)
    return pl.pallas_call(
        functools.partial(_deltanet_kernel, seq_len=seq_len, n_sub=n_sub),
        grid=(n_seq // n_sub, A_HEADS // 2),
        in_specs=in_specs,
        out_specs=[pl.BlockSpec((rows, LANES), lambda b, hp: (b, hp)), st_spec, st_spec],
        out_shape=[jax.ShapeDtypeStruct((n_seq * seq_len, A_WIDTH), F32), st_shape, st_shape],
        scratch_shapes=[tile(BF16), tile(BF16), tile(BF16), tile(BF16), tile(BF16),
                        pltpu.VMEM((2, nct, 8, LANES), F32),
                        pltpu.VMEM((2 * n_sub, LANES, LANES), F32),
                        pltpu.VMEM((rows, LANES), F32)],
        compiler_params=_cparams(("arbitrary", "arbitrary")),
        name="deltanet",
    )(*args)


def _dft_mats(n):
    idx = np.arange(n)
    ang = 2.0 * np.pi * ((idx[:, None] * idx[None, :]) % n) / n
    return np.cos(ang), np.sin(ang)


def _fnet_ctx_kernel(u_ref, cs_ref, dft_ref, o_ref):
    norm = 1.0 / math.sqrt(SEQ * LANES)
    for g in range(B_GROUPS):
        sl = slice(g * LANES, (g + 1) * LANES)
        p = _mm(u_ref[:, sl], cs_ref[...])
        stack = jnp.concatenate([p[:, :LANES], p[:, LANES:]], axis=0)
        o_ref[:, sl] = _mm(dft_ref[...], stack) * norm


def _fnet_ctx(proj, cs, dft):
    return pl.pallas_call(
        _fnet_ctx_kernel,
        grid=(BATCH,),
        in_specs=[pl.BlockSpec((SEQ, B_WIDTH), lambda b: (b, 4)),
                  pl.BlockSpec(cs.shape, lambda b: (0, 0)),
                  pl.BlockSpec(dft.shape, lambda b: (0, 0))],
        out_specs=pl.BlockSpec((SEQ, B_WIDTH), lambda b: (b, 0)),
        out_shape=jax.ShapeDtypeStruct((NC, B_WIDTH), F32),
        compiler_params=_cparams(("arbitrary",)),
        name="fnet_ctx",
    )(proj, cs, dft)


FN_SUB = 4


def _fnet_lat1_kernel(u_ref, ca_ref, m1_ref, twc_ref, tws_ref, o_ref):
    r = 64
    for j in range(FN_SUB):
        c = twc_ref[j]
        s = tws_ref[j]
        rows = slice(j * r, (j + 1) * r)
        for g in range(B_GROUPS):
            pa = _mm(u_ref[rows, g * LANES:(g + 1) * LANES], ca_ref[...])
            rhs = jnp.concatenate([pa[:, :2 * LANES], pa[:, 2 * LANES:]], axis=0)
            zz = _mm(m1_ref[...], rhs)
            zr, zi = zz[:, :LANES], zz[:, LANES:]
            o_ref[rows, 2 * g * LANES:(2 * g + 1) * LANES] = (zr * c + zi * s).astype(BF16)
            o_ref[rows, (2 * g + 1) * LANES:(2 * g + 2) * LANES] = (zi * c - zr * s).astype(BF16)


def _fnet_lat2_kernel(z_ref, m1_ref, o_ref):
    r = 64
    norm = 1.0 / math.sqrt(DEC_SEQ * LANES)
    for j in range(FN_SUB):
        rows = slice(j * r, (j + 1) * r)
        for g in range(B_GROUPS):
            rhs = jnp.concatenate([z_ref[rows, 2 * g * LANES:(2 * g + 1) * LANES],
                                   z_ref[rows, (2 * g + 1) * LANES:(2 * g + 2) * LANES]], axis=0)
            o_ref[rows, g * LANES:(g + 1) * LANES] = _mm(m1_ref[...], rhs) * norm


def _fnet_latent(proj, ca, m1, twc, tws):
    r = 64
    u = proj[NC:, 2048:2560].astype(BF16).reshape(DEC_BATCH, r, r, B_WIDTH)
    u = u.transpose(0, 2, 1, 3).reshape(DEC_BATCH * r * r, B_WIDTH)
    steps = DEC_BATCH * r // FN_SUB
    blk = FN_SUB * r
    z = pl.pallas_call(
        _fnet_lat1_kernel,
        grid=(steps,),
        in_specs=[pl.BlockSpec((blk, B_WIDTH), lambda s: (s, 0)),
                  pl.BlockSpec(ca.shape, lambda s: (0, 0)),
                  pl.BlockSpec(m1.shape, lambda s: (0, 0)),
                  pl.BlockSpec((FN_SUB, r, LANES), lambda s: (s % (r // FN_SUB), 0, 0)),
                  pl.BlockSpec((FN_SUB, r, LANES), lambda s: (s % (r // FN_SUB), 0, 0))],
        out_specs=pl.BlockSpec((blk, 2 * B_WIDTH), lambda s: (s, 0)),
        out_shape=jax.ShapeDtypeStruct((NL, 2 * B_WIDTH), BF16),
        compiler_params=_cparams(("arbitrary",)),
        name="fnet_lat1",
    )(u, ca, m1, twc, tws)
    z = z.reshape(DEC_BATCH, r, r, 2 * B_WIDTH).transpose(0, 2, 1, 3).reshape(NL, 2 * B_WIDTH)
    y = pl.pallas_call(
        _fnet_lat2_kernel,
        grid=(steps,),
        in_specs=[pl.BlockSpec((blk, 2 * B_WIDTH), lambda s: (s, 0)),
                  pl.BlockSpec(m1.shape, lambda s: (0, 0))],
        out_specs=pl.BlockSpec((blk, B_WIDTH), lambda s: (s, 0)),
        out_shape=jax.ShapeDtypeStruct((NL, B_WIDTH), F32),
        compiler_params=_cparams(("arbitrary",)),
        name="fnet_lat2",
    )(z, m1)
    return y.reshape(DEC_BATCH, r, r, B_WIDTH).transpose(0, 2, 1, 3).reshape(NL, B_WIDTH)


def _head_masks():
    lane = lax.broadcasted_iota(jnp.int32, (1, LANES), 1)
    return lane < NA_DH


def _attend_many(chains):
    scale = NA_DH ** -0.5
    s = [[_mm_nt(q, k) * scale if b is None else _mm_nt(q, k) * scale + b for k, v, b in kv] for q, kv in chains]
    m = [functools.reduce(jnp.maximum, [jnp.max(x, axis=-1, keepdims=True) for x in xs]) for xs in s]
    p = [[jnp.exp(x - mi) for x in xs] for xs, mi in zip(s, m)]
    l = [sum(jnp.sum(x, axis=-1, keepdims=True) for x in xs) for xs in p]
    o = [sum(_mm(x, v) for x, (k, v, b) in zip(xs, kv)) for xs, (q, kv) in zip(p, chains)]
    return [oi / li for oi, li in zip(o, l)]


def _na_ctx_kernel(q_ref, k_ref, v_ref, o_ref):
    first = _head_masks()
    n_split = 2
    rows = SEQ // n_split
    chains = []
    for p in range(NA_CTX_PAIRS):
        lanes = slice(p * LANES, (p + 1) * LANES)
        k = k_ref[:, lanes].astype(BF16)
        v = v_ref[:, lanes].astype(BF16)
        for j in range(n_split):
            q = q_ref[j * rows:(j + 1) * rows, lanes]
            for a in range(2):
                chains.append((jnp.where(first if a == 0 else jnp.logical_not(first), q, 0.0), [(k, v, None)]))
    outs = _attend_many(chains)
    for p in range(NA_CTX_PAIRS):
        for j in range(n_split):
            i = 2 * (p * n_split + j)
            o_ref[j * rows:(j + 1) * rows, p * LANES:(p + 1) * LANES] = jnp.where(first, outs[i], outs[i + 1])


def _na_ctx(qkv):
    w = NA_CTX_PAIRS * LANES
    nblk = D // w
    return pl.pallas_call(
        _na_ctx_kernel,
        grid=(BATCH, nblk),
        in_specs=[pl.BlockSpec((SEQ, w), lambda b, hp: (b, hp)),
                  pl.BlockSpec((SEQ, w), lambda b, hp: (b, nblk + hp)),
                  pl.BlockSpec((SEQ, w), lambda b, hp: (b, 2 * nblk + hp))],
        out_specs=pl.BlockSpec((SEQ, w), lambda b, hp: (b, hp)),
        out_shape=jax.ShapeDtypeStruct((NC, D), F32),
        compiler_params=_cparams(("arbitrary", "arbitrary")),
        name="na_ctx",
    )(qkv, qkv, qkv)


def _na_lat_kernel(q_ref, k_ref, v_ref, kc_ref, vc_ref, tt_ref, o_ref):
    first = _head_masks()
    rows = DEC_SEQ // GRID_W
    kctx = kc_ref[...].astype(BF16)
    vctx = vc_ref[...].astype(BF16)
    nkeys = WIN_R * GRID_W

    def rows_body(j, carry):
        chains = []
        for t in range(NA_ROW_UNROLL):
            r = NA_ROW_UNROLL * j + t
            r0 = jnp.clip(r - WIN_R // 2, 0, rows - WIN_R)
            dr0 = r0 - r + (WIN_R - 1)
            q = q_ref[pl.ds(pl.multiple_of(r * GRID_W, GRID_W), GRID_W), :]
            kbase = pl.multiple_of(r0 * GRID_W, GRID_W)
            kl = k_ref[pl.ds(kbase, nkeys), :].astype(BF16)
            vl = v_ref[pl.ds(kbase, nkeys), :].astype(BF16)
            for a in range(2):
                qm = jnp.where(first if a == 0 else jnp.logical_not(first), q, 0.0)
                bias = jnp.concatenate([tt_ref[a, dr0 + 2 * i] for i in range(WIN_R // 2)], axis=1)
                chains.append((qm, [(kl, vl, bias), (kctx, vctx, None)]))
        outs = _attend_many(chains)
        for t in range(NA_ROW_UNROLL):
            r = NA_ROW_UNROLL * j + t
            o_ref[pl.ds(pl.multiple_of(r * GRID_W, GRID_W), GRID_W), :] = jnp.where(first, outs[2 * t], outs[2 * t + 1])
        return carry

    lax.fori_loop(0, rows // NA_ROW_UNROLL, rows_body, 0)


def _na_latent(qkv, cache_k2, cache_v2, tt2):
    rb0 = NC // DEC_SEQ
    return pl.pallas_call(
        _na_lat_kernel,
        grid=(DEC_BATCH, NA_HEADS // 2),
        in_specs=[pl.BlockSpec((DEC_SEQ, LANES), lambda b, hp: (rb0 + b, hp)),
                  pl.BlockSpec((DEC_SEQ, LANES), lambda b, hp: (rb0 + b, 8 + hp)),
                  pl.BlockSpec((DEC_SEQ, LANES), lambda b, hp: (rb0 + b, 16 + hp)),
                  pl.BlockSpec((None, 256, LANES), lambda b, hp: (b, 0, hp)),
                  pl.BlockSpec((None, 256, LANES), lambda b, hp: (b, 0, hp)),
                  pl.BlockSpec((2, 2 * WIN_R - 2, GRID_W, LANES), lambda b, hp: (hp, 0, 0, 0))],
        out_specs=pl.BlockSpec((DEC_SEQ, LANES), lambda b, hp: (b, hp)),
        out_shape=jax.ShapeDtypeStruct((NL, D), F32),
        compiler_params=_cparams(("arbitrary", "arbitrary")),
        name="na_latent",
    )(qkv, qkv, qkv, cache_k2, cache_v2, tt2)


def _rpb_tables(rpb):
    col = np.arange(GRID_W)
    start = np.clip(col - WIN_C // 2, 0, GRID_W - WIN_C)
    inside = (col[None, :] >= start[:, None]) & (col[None, :] < start[:, None] + WIN_C)
    w = GRID_W
    period = 2 * w - 1
    x = jnp.pad(rpb, ((0, 0), (0, 0), (w - WIN_C, w - WIN_C)))
    flat = jnp.tile(x, (1, 1, w))[:, :, w - 1:w - 1 + w * (period - 1)]
    t = flat.reshape(NA_HEADS, 2 * WIN_R - 1, w, period - 1)[..., :w]
    t = jnp.where(inside[None, None], t, NEG)
    return jnp.concatenate([t[:, :-1], t[:, 1:]], axis=-1)


ROUTE_ROWS = 40


def _router_kernel(*refs, n_in):
    a_refs = refs[:2 * n_in]
    w_refs = refs[2 * n_in:3 * n_in]
    x_refs = refs[3 * n_in:-11]
    m_ref, g_ref, wr_ref, br_ref, tri_ref, xn_ref, xf_ref, ri_ref, rw_ref, cnt_ref, base_scr = refs[-11:]
    i = pl.program_id(0)

    @pl.when(i == 0)
    def _():
        base_scr[...] = jnp.zeros_like(base_scr)
        cnt_ref[...] = jnp.zeros_like(cnt_ref)

    acc = None
    for j, w_ref in enumerate(w_refs):
        part = _mm(_token_rows(a_refs[2 * j:2 * j + 2]), w_ref[...])
        acc = part if acc is None else acc + part
    x_new = _token_rows(x_refs) + m_ref[2:3, :] * acc
    xn_ref[...] = x_new
    h = _modulated_norm(x_new, m_ref, g_ref, 3, 4)
    xf_ref[...] = _pack_bf16_pairs(h[:, :D // 2], h[:, D // 2:])
    logits = lax.dot_general(wr_ref[...], h, (((1,), (1,)), ((), ())), preferred_element_type=F32,
                             precision=HIGHEST) + br_ref[:, 0:1]
    row = lax.broadcasted_iota(jnp.int32, logits.shape, 0)
    cmax = lambda x: jnp.max(x, axis=0, keepdims=True)
    cmin = lambda x: jnp.min(x, axis=0, keepdims=True)
    csum = lambda x: jnp.sum(x, axis=0, keepdims=True)

    gmask = row < N_GROUPS
    mg = cmax(jnp.where(gmask, logits, NEG))
    eg = jnp.where(gmask, jnp.exp(jnp.where(gmask, logits - mg, NEG)), 0.0)
    pg = eg / csum(eg)
    p_grp = cmax(pg)
    grp = cmin(jnp.where(jnp.logical_and(gmask, pg == p_grp), row, ROUTE_ROWS))
    lo = N_GROUPS + grp * EXP_PER_GROUP
    emask = jnp.logical_and(row >= lo, row < lo + EXP_PER_GROUP)
    me = cmax(jnp.where(emask, logits, NEG))
    ee = jnp.where(emask, jnp.exp(jnp.where(emask, logits - me, NEG)), 0.0)
    pe = ee / csum(ee)
    p1 = cmax(pe)
    i1 = cmin(jnp.where(jnp.logical_and(emask, pe == p1), row, ROUTE_ROWS))
    m2 = jnp.logical_and(emask, row != i1)
    p2 = cmax(jnp.where(m2, pe, -1.0))
    i2 = cmin(jnp.where(jnp.logical_and(m2, pe == p2), row, ROUTE_ROWS))
    den = p1 + p2
    w1 = p_grp * p1 / den
    w2 = p_grp * p2 / den

    sel1 = row == i1
    sel2 = row == i2
    oh = jnp.where(jnp.logical_or(sel1, sel2), 1.0, 0.0).astype(BF16)
    before = jnp.dot(oh, tri_ref[...], preferred_element_type=F32) + base_scr[:, 0:1]
    rank1 = csum(jnp.where(sel1, before, 0.0))
    rank2 = csum(jnp.where(sel2, before, 0.0))
    base_scr[...] = base_scr[...] + jnp.sum(oh.astype(F32), axis=1, keepdims=True)
    cnt_ref[...] = cnt_ref[...] + lax.dot_general(jnp.ones((8, TM), BF16), oh, (((1,), (1,)), ((), ())),
                                                  preferred_element_type=F32)
    sub = lax.broadcasted_iota(jnp.int32, (8, TM), 0)
    ri_ref[...] = jnp.where(sub == 0, i1 - N_GROUPS, jnp.where(sub == 1, i2 - N_GROUPS,
                  jnp.where(sub == 2, rank1.astype(jnp.int32), jnp.where(sub == 3, rank2.astype(jnp.int32), 0))))
    rw_ref[...] = jnp.where(sub == 0, w1, jnp.where(sub == 1, w2, 0.0))


def _router(a_pairs, w_list, x, m3, gain, wr_t, br_t, tri_upper):
    in_specs, args = [], []
    for pair in a_pairs:
        specs, ops = _token_specs(pair)
        in_specs += specs
        args += ops
    x_specs, x_args = _token_specs(x)
    in_specs += ([pl.BlockSpec(w.shape, lambda i: (0, 0)) for w in w_list] + x_specs
                 + [pl.BlockSpec((None, 6, D), lambda i: (_cond_row(i), 0, 0)),
                    pl.BlockSpec((1, D), lambda i: (0, 0)),
                    pl.BlockSpec((ROUTE_ROWS, D), lambda i: (0, 0)),
                    pl.BlockSpec((ROUTE_ROWS, LANES), lambda i: (0, 0)),
                    pl.BlockSpec((TM, TM), lambda i: (0, 0))])
    return pl.pallas_call(
        functools.partial(_router_kernel, n_in=len(a_pairs)),
        grid=(N // TM,),
        in_specs=in_specs,
        out_specs=[pl.BlockSpec((TM, D), lambda i: (i, 0)),
                   pl.BlockSpec((TM, D // 2), lambda i: (i, 0)),
                   pl.BlockSpec((8, TM), lambda i: (0, i)),
                   pl.BlockSpec((8, TM), lambda i: (0, i)),
                   pl.BlockSpec((8, ROUTE_ROWS), lambda i: (0, 0))],
        out_shape=[jax.ShapeDtypeStruct((N, D), F32),
                   jax.ShapeDtypeStruct((N, D // 2), jnp.int32),
                   jax.ShapeDtypeStruct((8, N), jnp.int32),
                   jax.ShapeDtypeStruct((8, N), F32),
                   jax.ShapeDtypeStruct((8, ROUTE_ROWS), F32)],
        scratch_shapes=[pltpu.VMEM((ROUTE_ROWS, LANES), F32)],
        compiler_params=_cparams(("arbitrary",)),
        name="router",
    )(*args, *w_list, *x_args, m3, gain, wr_t, br_t, tri_upper)


def _row_copy(src_hbm, row, dst, r, sem):
    return pltpu.make_async_copy(src_hbm.at[pl.ds(row, 1)], dst.at[pl.ds(r, 1)], sem)


def _expert_kernel(be_ref, nu_ref, d_ref, pad_ref, xp_hbm, wg_ref, wu_ref, wd_ref, ys_ref, xres, xbuf, st_ref, sem):
    del be_ref
    b = pl.program_id(0)
    half = D // 2

    def gather(blk, slot, part=0, parts=1):
        base = blk * MOE_ROWS
        for r in range(part * MOE_ROWS // parts, (part + 1) * MOE_ROWS // parts):
            xbuf[slot, pl.ds(r, 1), :] = xres[pl.ds(st_ref[base + r], 1), :]

    @pl.when(b == 0)
    def _():
        cp = pltpu.make_async_copy(xp_hbm, xres, sem.at[0])
        cp.start()

        def clear_tail(e, c):
            def clear(s_, c2):
                st_ref[s_] = 0
                return c2

            return lax.fori_loop(pad_ref[2 * e], pad_ref[2 * e + 1], clear, c)

        lax.fori_loop(0, N_EXPERTS + 1, clear_tail, 0)

        def invert(j, c):
            slots = [d_ref[16 * j + t] for t in range(16)]
            for t in range(16):
                st_ref[slots[t]] = 8 * j + t // 2
            return c

        lax.fori_loop(0, 2 * N // 16, invert, 0)
        cp.wait()
        gather(0, 0)

    @pl.when(b < nu_ref[0])
    def _():
        nxt = (jnp.minimum(b + 1, N_SLOT_BLOCKS - 1), (b + 1) % 2)
        x_lo, x_hi = _unpack_bf16_pairs(xbuf[b % 2])
        dot = functools.partial(jnp.dot, preferred_element_type=F32)
        gather(*nxt, 0, 8)
        g = dot(x_lo, wg_ref[:half, :].astype(BF16))
        gather(*nxt, 1, 8)
        g = g + dot(x_hi, wg_ref[half:, :].astype(BF16))
        gather(*nxt, 2, 8)
        u = dot(x_lo, wu_ref[:half, :].astype(BF16))
        gather(*nxt, 3, 8)
        u = u + dot(x_hi, wu_ref[half:, :].astype(BF16))
        hb = (_silu(g) * u).astype(BF16)
        quarter = D // 4
        for j in range(4):
            gather(*nxt, 4 + j, 8)
            cols = slice(j * quarter, (j + 1) * quarter)
            ys_ref[:, cols] = jnp.dot(hb, wd_ref[:, cols].astype(BF16), preferred_element_type=F32)

    @pl.when(b >= nu_ref[0])
    def _():
        ys_ref[...] = jnp.zeros_like(ys_ref)


def _experts(block_e, n_used, dest, pad_ranges, xp, w_gate, w_up, w_down, layer):
    grid_spec = pltpu.PrefetchScalarGridSpec(
        num_scalar_prefetch=4,
        grid=(N_SLOT_BLOCKS,),
        in_specs=[pl.BlockSpec(memory_space=pl.ANY),
                  pl.BlockSpec((None, None, D, D_EXPERT), lambda b, be, nu, st, pd: (layer, be[b], 0, 0)),
                  pl.BlockSpec((None, None, D, D_EXPERT), lambda b, be, nu, st, pd: (layer, be[b], 0, 0)),
                  pl.BlockSpec((None, None, D_EXPERT, D), lambda b, be, nu, st, pd: (layer, be[b], 0, 0))],
        out_specs=pl.BlockSpec((MOE_ROWS, D), lambda b, be, nu, st, pd: (b, 0)),
        scratch_shapes=[pltpu.VMEM((N, D // 2), jnp.int32),
                        pltpu.VMEM((2, MOE_ROWS, D // 2), jnp.int32),
                        pltpu.SMEM((N_SLOT_BLOCKS * MOE_ROWS,), jnp.int32),
                        pltpu.SemaphoreType.DMA((1,))])
    return pl.pallas_call(
        _expert_kernel,
        grid_spec=grid_spec,
        out_shape=jax.ShapeDtypeStruct((N_SLOT_BLOCKS * MOE_ROWS, D), F32),
        compiler_params=pltpu.CompilerParams(dimension_semantics=("arbitrary",), vmem_limit_bytes=EXPERT_VMEM_LIMIT),
        name="experts",
    )(block_e, n_used, dest, pad_ranges, xp, w_gate, w_up, w_down)


def _combine_kernel(d_ref, ys_hbm, x_ref, m_ref, rw_ref, fn_ref, o_ref, buf, sem, *, final, tile0, n_tiles):
    i = pl.program_id(0)

    def issue(tile, slot):
        base = (tile0 + tile) * (2 * TM)
        for r in range(TM):
            for kk in range(2):
                _row_copy(ys_hbm, d_ref[base + 2 * r + kk], buf.at[slot, kk], r, sem.at[slot]).start()

    @pl.when(i == 0)
    def _():
        issue(0, 0)

    @pl.when(i + 1 < n_tiles)
    def _():
        issue(i + 1, (i + 1) % 2)

    slot = i % 2
    for kk in range(2):
        pltpu.make_async_copy(ys_hbm.at[pl.ds(0, TM)], buf.at[slot, kk], sem.at[slot]).wait()
    w = rw_ref[...]
    y = w[:, 0:1] * buf[slot, 0] + w[:, 1:2] * buf[slot, 1]
    out = x_ref[...] + m_ref[5:6, :] * y
    if final:
        ms = jnp.mean(out * out, axis=-1, keepdims=True)
        out = out * lax.rsqrt(ms + EPS) * fn_ref[...]
    o_ref[...] = out


def _combine(dest_flat, ys, x, m3, rw, final_norm, final, tile0=0, n_tiles=N // TM):
    grid_spec = pltpu.PrefetchScalarGridSpec(
        num_scalar_prefetch=1,
        grid=(n_tiles,),
        in_specs=[pl.BlockSpec(memory_space=pl.ANY),
                  pl.BlockSpec((TM, D), lambda i, d: (tile0 + i, 0)),
                  pl.BlockSpec((None, 6, D), lambda i, d: (_cond_row(tile0 + i), 0, 0)),
                  pl.BlockSpec((TM, 2), lambda i, d: (tile0 + i, 0)),
                  pl.BlockSpec((1, D), lambda i, d: (0, 0))],
        out_specs=pl.BlockSpec((TM, D), lambda i, d: (i, 0)),
        scratch_shapes=[pltpu.VMEM((2, 2, TM, D), F32), pltpu.SemaphoreType.DMA((2,))])
    return pl.pallas_call(
        functools.partial(_combine_kernel, final=final, tile0=tile0, n_tiles=n_tiles),
        grid_spec=grid_spec,
        out_shape=jax.ShapeDtypeStruct((n_tiles * TM, D), F32),
        compiler_params=_cparams(("arbitrary",)),
        name="combine",
    )(dest_flat, ys, x, m3, rw, final_norm)


def _hier_moe(a_pairs, w_list, x, m3, gain, layer, w_rg, b_rg, w_re, b_re, w_gate, w_up, w_down, tri_tm,
              final_norm, final):
    pad = ROUTE_ROWS - N_GROUPS - N_EXPERTS
    wr_t = jnp.concatenate([w_rg.T, w_re.transpose(0, 2, 1).reshape(N_EXPERTS, D), jnp.zeros((pad, D), F32)], axis=0)
    br_t = jnp.broadcast_to(jnp.concatenate([b_rg, b_re.reshape(N_EXPERTS), jnp.zeros((pad,), F32)])[:, None],
                            (ROUTE_ROWS, LANES))
    x, xf, ri_t, rw_t, cnt = _router(a_pairs, w_list, x, m3, gain, wr_t, br_t, tri_tm)
    e_idx = ri_t[0:2].T
    rank = ri_t[2:4].T
    rw = rw_t[0:2].T
    counts = cnt[0, N_GROUPS:N_GROUPS + N_EXPERTS].astype(jnp.int32)
    padded = (counts + MOE_ROWS - 1) // MOE_ROWS * MOE_ROWS
    end_pad = jnp.cumsum(padded)
    start_pad = end_pad - padded
    experts = jnp.arange(N_EXPERTS, dtype=jnp.int32)
    start_of = jnp.sum(jnp.where(e_idx[:, :, None] == experts, start_pad, 0), axis=-1)
    dest = (start_of + rank).reshape(-1).astype(jnp.int32)
    block_start = jnp.arange(N_SLOT_BLOCKS, dtype=jnp.int32) * MOE_ROWS
    block_e = jnp.minimum(jnp.sum((end_pad[None, :] <= block_start[:, None]).astype(jnp.int32), axis=1),
                          N_EXPERTS - 1)
    n_used = (end_pad[-1:] // MOE_ROWS).astype(jnp.int32)
    tail = jnp.stack([end_pad[-1], jnp.minimum(end_pad[-1] + MOE_ROWS, N_SLOT_BLOCKS * MOE_ROWS)])
    pad_ranges = jnp.concatenate([jnp.stack([start_pad + counts, end_pad], axis=1).reshape(-1), tail]).astype(jnp.int32)
    ys = _experts(block_e, n_used, dest, pad_ranges, xf, w_gate, w_up, w_down, layer)
    if not final:
        return _combine(dest, ys, x, m3, rw, final_norm, False)
    nct = NC // TM
    return (_combine(dest, ys, x, m3, rw, final_norm, True, 0, nct),
            _combine(dest, ys, x, m3, rw, final_norm, True, nct, N // TM - nct))


def kernel(x_prompt, x_sample, state_A_fwd, state_A_bwd, cache_k, cache_v, c, c_ctx, mod_w, mod_b, norm_mix, norm_ffn, ab_w_in, ab_conv, ab_a_log, ab_dt_bias, ab_o_gain, ab_w_out, na_w_qkv, na_rpb, na_w_out, moe_w_rg, moe_b_rg, moe_w_re, moe_b_re, moe_w_gate, moe_w_up, moe_w_down, final_norm):
    x = (x_prompt.reshape(NC, D), x_sample.reshape(NL, D))
    cond8 = jnp.concatenate([c_ctx[None, :], c, jnp.zeros((8 - 1 - DEC_BATCH, D), F32)], axis=0)
    mods = _ada_params(cond8, mod_w, mod_b).reshape(DEPTH, 8, 6, D)

    tri_tm = jnp.asarray(np.triu(np.ones((TM, TM)), 1), BF16)
    tri_c = jnp.asarray(np.tril(np.ones((CHUNK, CHUNK))), F32)
    half = np.arange(LANES) < A_DH
    jbd = jnp.asarray((half[:, None] == half[None, :]).astype(np.float32))
    fn = final_norm[None, :]

    m3 = mods[0]
    w_in = ab_w_in[0]
    w_in = jnp.concatenate([w_in[:, :2048], w_in[:, 2080:2592], w_in[:, 2048:2080],
                            jnp.zeros((D, AB_COLS - 2592), F32)], axis=1).astype(BF16)
    proj = _modproj(x, m3, norm_mix[0][None, :], w_in, 896)
    gate_p = jnp.broadcast_to(jnp.stack([ab_a_log[0].reshape(-1), ab_dt_bias[0].reshape(-1)])[:, :, None],
                              (2, 2 * A_HEADS, LANES))
    o_gain2 = jnp.tile(ab_o_gain[0], 2)[None, :]
    zeros_state = jnp.zeros((BATCH, A_HEADS, A_DH, A_DH), F32)
    mix_a_c, s_f, s_b = _deltanet(proj, ab_conv[0], gate_p, o_gain2, zeros_state, zeros_state, tri_c, jbd,
                                  seq_len=SEQ, n_seq=BATCH, n_sub=4, row_blk0=0)
    mix_a_l, _, _ = _deltanet(proj, ab_conv[0], gate_p, o_gain2, state_A_fwd[:, 0], state_A_bwd[:, 0], tri_c, jbd,
                              seq_len=DEC_SEQ, n_seq=DEC_BATCH, n_sub=1, row_blk0=NC // DEC_SEQ)

    cc, sc = _dft_mats(LANES)
    ct, st = _dft_mats(SEQ)
    c64, s64 = _dft_mats(64)
    cs = jnp.asarray(np.concatenate([cc, sc], axis=1), BF16)
    dft = jnp.asarray(np.concatenate([ct, -st], axis=1), BF16)
    ca = jnp.asarray(np.concatenate([cc, -sc, -sc, -cc], axis=1), BF16)
    m1 = jnp.asarray(np.concatenate([c64, s64], axis=1), BF16)
    tw_idx = np.arange(64)
    tw_ang = 2.0 * np.pi * (tw_idx[:, None] * tw_idx[None, :]) / DEC_SEQ
    twc = jnp.broadcast_to(jnp.asarray(np.cos(tw_ang), F32)[:, :, None], (64, 64, LANES))
    tws = jnp.broadcast_to(jnp.asarray(np.sin(tw_ang), F32)[:, :, None], (64, 64, LANES))
    mix_b_c = _fnet_ctx(proj, cs, dft)
    mix_b_l = _fnet_latent(proj, ca, m1, twc, tws)

    w_out = ab_w_out[0].astype(BF16)
    x = _hier_moe([(mix_a_c, mix_a_l), (mix_b_c, mix_b_l)], [w_out[:A_WIDTH], w_out[A_WIDTH:]], x, m3,
                  norm_ffn[0][None, :], 0, moe_w_rg[0], moe_b_rg[0], moe_w_re[0], moe_b_re[0],
                  moe_w_gate, moe_w_up, moe_w_down, tri_tm, fn, False)

    m3 = mods[1]
    qkv = _modproj(x, m3, norm_mix[1][None, :], na_w_qkv[0].astype(BF16), 512)
    attn_c = _na_ctx(qkv)
    attn_l = _na_latent(qkv, cache_k[:, 0].reshape(DEC_BATCH, 256, D), cache_v[:, 0].reshape(DEC_BATCH, 256, D),
                        _rpb_tables(na_rpb[0]))
    y_c, y_l = _hier_moe([(attn_c, attn_l)], [na_w_out[0].astype(BF16)], x, m3, norm_ffn[1][None, :], 1, moe_w_rg[1], moe_b_rg[1], moe_w_re[1], moe_b_re[1],
                         moe_w_gate, moe_w_up, moe_w_down, tri_tm, fn, True)

    new_k = qkv[:NC, D:2 * D].reshape(BATCH, 1, SEQ, NA_HEADS, NA_DH)
    new_v = qkv[:NC, 2 * D:].reshape(BATCH, 1, SEQ, NA_HEADS, NA_DH)
    return (y_c.reshape(BATCH, SEQ, D), y_l.reshape(DEC_BATCH, DEC_SEQ, D),
            s_f[:, None], s_b[:, None], new_k, new_v)
```

```python
import functools
import math

import numpy as np
import jax
import jax.numpy as jnp
from jax import lax
from jax.experimental import pallas as pl
from jax.experimental.pallas import tpu as pltpu

F32 = jnp.float32
BF16 = jnp.bfloat16
HIGHEST = lax.Precision.HIGHEST

D = 1024
BATCH, SEQ = 32, 256
DEC_BATCH, DEC_SEQ = 2, 4096
NC = BATCH * SEQ
NL = DEC_BATCH * DEC_SEQ
N = NC + NL
DEPTH = 2
GRID_W = 64
A_DH = 64
A_HEADS = 8
A_WIDTH = 512
CHUNK = 64
B_WIDTH = 512
B_GROUPS = 4
NA_DH = 64
NA_HEADS = 16
WIN_R, WIN_C = 8, 16
N_GROUPS, EXP_PER_GROUP, N_EXPERTS = 4, 8, 32
D_EXPERT = 512
EPS = 1e-6

LANES = 128
TM = 256
MOE_ROWS = 256
N_SLOT_BLOCKS = (2 * N) // MOE_ROWS + N_EXPERTS
AB_COLS = 2688
VMEM_LIMIT = 56 * 1024 * 1024
EXPERT_VMEM_LIMIT = 60 * 1024 * 1024
NEG = -1e30


def _cparams(sem):
    return pltpu.CompilerParams(dimension_semantics=sem, vmem_limit_bytes=VMEM_LIMIT)


def _mm(a, b):
    return jnp.dot(a.astype(BF16), b.astype(BF16), preferred_element_type=F32)


def _mm_nt(a, b):
    return lax.dot_general(a.astype(BF16), b.astype(BF16), (((1,), (1,)), ((), ())),
                           preferred_element_type=F32)


SOLVE_BLK = 16
NA_SCALE = NA_DH ** -0.5
NA_CTX_PAIRS = 2
NA_ROW_UNROLL = 8
TERM_UNROLL = 4


def _unit_lower_solve(a_mat, rhs, same_blk, eye):
    dg = jnp.where(same_blk, a_mat, 0.0)
    p = -dg
    dinv = eye + p
    for _ in range(int(math.log2(SOLVE_BLK)) - 1):
        p = _mm(p, p)
        dinv = dinv + _mm(p, dinv)
    mp = -_mm(dinv, a_mat - dg)
    y = _mm(dinv, rhs)
    y = y + _mm(mp, y)
    for _ in range(int(math.log2(a_mat.shape[0] // SOLVE_BLK)) - 1):
        mp = _mm(mp, mp)
        y = y + _mm(mp, y)
    return y


def _unit_lower_solve_many(a_mats, rhs, same_blk, eye):
    off = [jnp.where(same_blk, 0.0, a).astype(BF16) for a in a_mats]
    p = [jnp.where(same_blk, -a, 0.0).astype(BF16) for a in a_mats]
    dinv = [(eye + x.astype(F32)).astype(BF16) for x in p]
    for _ in range(int(math.log2(SOLVE_BLK)) - 1):
        p = [_mm(x, x).astype(BF16) for x in p]
        dinv = [(di.astype(F32) + _mm(x, di)).astype(BF16) for x, di in zip(p, dinv)]
    mp = [(-_mm(di, o)).astype(BF16) for di, o in zip(dinv, off)]
    y = [_mm(di, r) for di, r in zip(dinv, rhs)]
    y = [yi + _mm(m, yi) for m, yi in zip(mp, y)]
    for _ in range(int(math.log2(a_mats[0].shape[0] // SOLVE_BLK)) - 1):
        mp = [_mm(m, m).astype(BF16) for m in mp]
        y = [yi + _mm(m, yi) for m, yi in zip(mp, y)]
    return y


def _mm_split(a, b, parts, split_rhs=False):
    x = b if split_rhs else a
    acc = None
    for _ in range(parts):
        piece = x.astype(BF16)
        x = x - piece.astype(F32)
        term = (jnp.dot(a.astype(BF16), piece, preferred_element_type=F32) if split_rhs
                else jnp.dot(piece, b.astype(BF16), preferred_element_type=F32))
        acc = term if acc is None else acc + term
    return acc


def _mm_hi(a, b):
    return jnp.dot(a, b, preferred_element_type=F32, precision=HIGHEST)


def _silu(x):
    return x * jax.nn.sigmoid(x)


def _bf16_bits(x):
    b = lax.bitcast_convert_type(x, jnp.int32)
    return b + 0x7FFF + (lax.shift_right_logical(b, jnp.int32(16)) & 1)


_HIGH16 = -65536


def _pack_bf16_pairs(a, b):
    return lax.shift_right_logical(_bf16_bits(a), jnp.int32(16)) | (_bf16_bits(b) & _HIGH16)


def _unpack_bf16_pairs(p):
    a = lax.bitcast_convert_type(lax.shift_left(p, jnp.int32(16)), F32)
    b = lax.bitcast_convert_type(p & _HIGH16, F32)
    return a.astype(BF16), b.astype(BF16)


def _cond_row(i):
    return jnp.where(i < NC // TM, 0, 1 + (i - NC // TM) // (DEC_SEQ // TM))


def _modulated_norm(x, m_ref, g_ref, shift_idx, scale_idx):
    ms = jnp.mean(x * x, axis=-1, keepdims=True)
    y = x * lax.rsqrt(ms + EPS) * g_ref[...]
    return y * (1.0 + m_ref[scale_idx:scale_idx + 1, :]) + m_ref[shift_idx:shift_idx + 1, :]


def _ada_kernel(cond_ref, w_ref, b_ref, o_ref):
    o_ref[...] = _mm_hi(_silu(cond_ref[...]), w_ref[...]) + b_ref[...]


def _ada_params(cond8, mod_w, mod_b):
    tn = 1536
    return pl.pallas_call(
        _ada_kernel,
        grid=(DEPTH, 6 * D // tn),
        in_specs=[pl.BlockSpec((8, D), lambda l, j: (0, 0)),
                  pl.BlockSpec((None, D, tn), lambda l, j: (l, 0, j)),
                  pl.BlockSpec((None, 1, tn), lambda l, j: (l, 0, j))],
        out_specs=pl.BlockSpec((None, 8, tn), lambda l, j: (l, 0, j)),
        out_shape=jax.ShapeDtypeStruct((DEPTH, 8, 6 * D), F32),
        compiler_params=_cparams(("arbitrary", "arbitrary")),
        name="ada_params",
    )(cond8, mod_w, mod_b.reshape(DEPTH, 1, 6 * D))


def _token_specs(x):
    nct = NC // TM
    if isinstance(x, tuple):
        return ([pl.BlockSpec((TM, x[0].shape[1]), lambda i: (jnp.minimum(i, nct - 1), 0)),
                 pl.BlockSpec((TM, x[1].shape[1]), lambda i: (jnp.maximum(i - nct, 0), 0))], list(x))
    return [pl.BlockSpec((TM, x.shape[1]), lambda i: (i, 0))], [x]


def _token_rows(refs):
    if len(refs) == 1:
        return refs[0][...]
    return jnp.where(pl.program_id(0) < NC // TM, refs[0][...], refs[1][...])


def _modproj_kernel(*refs, n_chunk):
    m_ref, g_ref, w_ref, o_ref = refs[-4:]
    hb = _modulated_norm(_token_rows(refs[:-4]), m_ref, g_ref, 0, 1).astype(BF16)
    for j in range(o_ref.shape[1] // n_chunk):
        sl = slice(j * n_chunk, (j + 1) * n_chunk)
        o_ref[:, sl] = jnp.dot(hb, w_ref[:, sl], preferred_element_type=F32)


def _modproj(x, m3, gain, w_bf16, n_chunk):
    nout = w_bf16.shape[1]
    x_specs, x_args = _token_specs(x)
    return pl.pallas_call(
        functools.partial(_modproj_kernel, n_chunk=n_chunk),
        grid=(N // TM,),
        in_specs=x_specs + [pl.BlockSpec((None, 6, D), lambda i: (_cond_row(i), 0, 0)),
                            pl.BlockSpec((1, D), lambda i: (0, 0)),
                            pl.BlockSpec((D, nout), lambda i: (0, 0))],
        out_specs=pl.BlockSpec((TM, nout), lambda i: (i, 0)),
        out_shape=jax.ShapeDtypeStruct((N, nout), F32),
        compiler_params=_cparams(("arbitrary",)),
        name="modproj",
    )(*x_args, m3, gain, w_bf16)


def _deltanet_kernel(q_ref, k_ref, v_ref, z_ref, ab_ref, cq_ref, ck_ref, cv_ref, gp_ref, og_ref,
                     s0f_ref, s0b_ref, tri_ref, jbd_ref, o_ref, sf_ref, sb_ref,
                     u_s, w_s, qd_s, at_s, kt_s, ge_s, st_s, ob, *, seq_len, n_sub):
    hp = pl.program_id(1)
    C = CHUNK
    nc = seq_len // C
    nct = n_sub * nc
    P = LANES
    lane = lax.broadcasted_iota(jnp.int32, (C, P), 1)
    row = lax.broadcasted_iota(jnp.int32, (C, P), 0)
    first_head = lane < A_DH
    ri = lax.broadcasted_iota(jnp.int32, (P, P), 0)
    ci = lax.broadcasted_iota(jnp.int32, (P, P), 1)
    same_head = (ri < C) == (ci < C)
    same_blk = (ri // SOLVE_BLK) == (ci // SOLVE_BLK)
    eye = jnp.where(ri == ci, 1.0, 0.0)
    jbd = jbd_ref[...]
    lincl = tri_ref[...]
    cum_b = [lincl.T.astype(BF16), lincl.astype(BF16)]
    incl_m = [jnp.logical_and(same_head, ri >= ci), jnp.logical_and(same_head, ri <= ci)]
    strict_m = [jnp.logical_and(same_head, ri > ci), jnp.logical_and(same_head, ri < ci)]
    neg_a = -jnp.exp(gp_ref[0])
    dt_b = gp_ref[1]

    def conv_silu(ref, w_ref, c):
        base = pl.multiple_of(c * C, C)
        cs = c % nc
        xc = ref[pl.ds(base, C), :]
        pbase = pl.multiple_of(jnp.maximum(base - 8, 0), 8)
        nbase = pl.multiple_of(jnp.minimum(base + C, n_sub * seq_len - 8), 8)
        prev_row = ref[pl.ds(pbase, 8), :][7:8, :] * jnp.where(cs > 0, 1.0, 0.0)
        next_row = ref[pl.ds(nbase, 8), :][0:1, :] * jnp.where(cs < nc - 1, 1.0, 0.0)
        x_prev = jnp.where(row == 0, prev_row, pltpu.roll(xc, 1, 0))
        x_next = jnp.where(row == C - 1, next_row, pltpu.roll(xc, C - 1, 0))
        y = w_ref[0:1, :] * x_prev + w_ref[1:2, :] * xc + w_ref[2:3, :] * x_next
        return _silu(y)

    def stack(x):
        return jnp.concatenate([jnp.where(first_head, x, 0.0), jnp.where(first_head, 0.0, x)], axis=0)

    def chunk_inputs(c):
        base = pl.multiple_of(c * C, C)
        q = conv_silu(q_ref, cq_ref, c)
        k = conv_silu(k_ref, ck_ref, c)
        v = conv_silu(v_ref, cv_ref, c)
        return q, k, v, ab_ref[pl.ds(base, C), :].T

    sub8 = lax.broadcasted_iota(jnp.int32, (A_HEADS, C), 0)

    def pair_row(x8):
        r0 = jnp.sum(jnp.where(sub8 == 2 * hp, x8, 0.0), axis=0, keepdims=True)
        r1 = jnp.sum(jnp.where(sub8 == 2 * hp + 1, x8, 0.0), axis=0, keepdims=True)
        return jnp.concatenate([r0, r1], axis=1)

    def chain_gates(ab_t, d):
        a8 = ab_t[d * A_HEADS:(d + 1) * A_HEADS, :]
        b8 = ab_t[(2 + d) * A_HEADS:(3 + d) * A_HEADS, :]
        g8 = neg_a[d * A_HEADS:(d + 1) * A_HEADS, :C] * jax.nn.softplus(a8 + dt_b[d * A_HEADS:(d + 1) * A_HEADS, :C])
        gc8 = _mm_split(g8, cum_b[d], 3)
        tot8 = jnp.broadcast_to(jnp.sum(g8, axis=-1, keepdims=True), (A_HEADS, C))
        gc_row, beta_row, tot_row = pair_row(gc8), pair_row(jax.nn.sigmoid(b8)), pair_row(tot8)
        cols = jnp.concatenate([gc_row, beta_row, tot_row, jnp.zeros((5, P), F32)], axis=0).T
        return gc_row, tot_row, cols[:, 0:1], cols[:, 1:2], cols[:, 2:3]

    def terms_body(j, carry):
        cs = [TERM_UNROLL * j + t for t in range(TERM_UNROLL)]
        ins = [chunk_inputs(c) for c in cs]
        qsq = [_mm_split(x[0] * x[0], jbd, 2) for x in ins]
        ksq = [_mm_split(x[1] * x[1], jbd, 2) for x in ins]
        qs = [x[0] * lax.rsqrt(s + EPS) * (A_DH ** -0.5) for x, s in zip(ins, qsq)]
        ks = [x[1] * lax.rsqrt(s + EPS) for x, s in zip(ins, ksq)]
        qst = [stack(x) for x in qs]
        kst = [stack(x) for x in ks]
        vst = [stack(x[2]) for x in ins]
        kst_t = [x.T for x in kst]
        kk = [_mm_nt(x, x) for x in kst]
        qk = [_mm_nt(x, y) for x, y in zip(qst, kst)]
        chains = [(t, d) for t in range(TERM_UNROLL) for d in range(2)]
        gates = [chain_gates(ins[t][3], d) for t, d in chains]
        decay, e_gc = [], []
        for (t, d), (gc_row, tot_row, gc_col, beta_col, tot_col) in zip(chains, gates):
            diff = jnp.broadcast_to(gc_col, (P, P)) - jnp.broadcast_to(gc_row, (P, P))
            decay.append(jnp.where(incl_m[d], jnp.exp(jnp.where(incl_m[d], diff, 0.0)), 0.0))
            e_gc.append(jnp.exp(gc_col))
        a_mats = [jnp.where(strict_m[d], g[3] * kk[t] * dc, 0.0) for (t, d), g, dc in zip(chains, gates, decay)]
        rhs = [vst[t] * g[3] + pltpu.roll(kst[t] * (g[3] * e), A_DH, 1)
               for (t, d), g, e in zip(chains, gates, e_gc)]
        xs = _unit_lower_solve_many(a_mats, rhs, same_blk, eye)
        for (t, d), x, g, e, dc in zip(chains, xs, gates, e_gc, decay):
            c = cs[t]
            gc_row, tot_row = g[0], g[1]
            u_s[d, c] = jnp.where(same_head, x, 0.0).astype(BF16)
            w_s[d, c] = pltpu.roll(jnp.where(same_head, 0.0, x), A_DH, 1).astype(BF16)
            qd_s[d, c] = (qst[t] * e).astype(BF16)
            at_s[d, c] = jnp.where(incl_m[d], qk[t] * dc, 0.0).astype(BF16)
            kt_s[d, c] = (kst_t[t] * jnp.exp(tot_row - gc_row)).astype(BF16)
            ge_s[d, c] = jnp.broadcast_to(jnp.exp(tot_row), (8, P))
        return carry

    lax.fori_loop(0, nct // TERM_UNROLL, terms_body, 0)

    def block_diag(s2):
        z = jnp.zeros((A_DH, A_DH), F32)
        return jnp.concatenate([jnp.concatenate([s2[0], z], axis=1),
                                jnp.concatenate([z, s2[1]], axis=1)], axis=0)

    for s in range(n_sub):
        st_s[2 * s] = block_diag(s0f_ref[s])
        st_s[2 * s + 1] = block_diag(s0b_ref[s])

    def scan_body(i, carry):
        chains = [(s, d, s * nc + (i if d == 0 else nc - 1 - i)) for s in range(n_sub) for d in range(2)]
        dot = functools.partial(jnp.dot, preferred_element_type=F32)
        s_bd = [st_s[2 * s + d] for s, d, c in chains]
        sb16 = [x.astype(BF16) for x in s_bd]
        ws = [dot(w_s[d, c], sb) for (s, d, c), sb in zip(chains, sb16)]
        qs_ = [dot(qd_s[d, c], sb) for (s, d, c), sb in zip(chains, sb16)]
        vb = [(u_s[d, c].astype(F32) - x).astype(BF16) for (s, d, c), x in zip(chains, ws)]
        av = [dot(at_s[d, c], x) for (s, d, c), x in zip(chains, vb)]
        kv = [dot(kt_s[d, c], x) for (s, d, c), x in zip(chains, vb)]
        for (s, d, c), sb, q_, a_, k_ in zip(chains, s_bd, qs_, av, kv):
            st_s[2 * s + d] = sb * ge_s[d, c][0:1, :] + k_
            o_st = q_ + a_
            dst = o_ref if d == 0 else ob
            dst[pl.ds(pl.multiple_of(c * C, C), C), :] = o_st[:C] + o_st[C:]
        return carry

    lax.fori_loop(0, nc, scan_body, 0)

    for s in range(n_sub):
        for d, ref in ((0, sf_ref), (1, sb_ref)):
            s_bd = st_s[2 * s + d]
            ref[s, 0] = s_bd[:A_DH, :A_DH]
            ref[s, 1] = s_bd[A_DH:, A_DH:]

    def finish(j, carry):
        bases = [pl.multiple_of((TERM_UNROLL * j + t) * C, C) for t in range(TERM_UNROLL)]
        o = [o_ref[pl.ds(b, C), :] + ob[pl.ds(b, C), :] for b in bases]
        ms = [_mm_split(x * x, jbd, 2) * (1.0 / A_DH) for x in o]
        for b, x, m in zip(bases, o, ms):
            o_ref[pl.ds(b, C), :] = x * lax.rsqrt(m + EPS) * og_ref[...] * _silu(z_ref[pl.ds(b, C), :])
        return carry

    lax.fori_loop(0, nct // TERM_UNROLL, finish, 0)


def _deltanet(proj, conv_w, gate_p, o_gain2, s0f, s0b, tri, jbd, *, seq_len, n_seq, n_sub, row_blk0):
    rows = n_sub * seq_len
    nct = rows // CHUNK
    rb = lambda b: row_blk0 + b
    col = lambda off: (lambda b, hp: (rb(b), off + hp))
    st_spec = pl.BlockSpec((n_sub, 2, A_DH, A_DH), lambda b, hp: (b, hp, 0, 0))
    in_specs = [pl.BlockSpec((rows, LANES), col(0)),
                pl.BlockSpec((rows, LANES), col(4)),
                pl.BlockSpec((rows, LANES), col(8)),
                pl.BlockSpec((rows, LANES), col(12)),
                pl.BlockSpec((rows, LANES), lambda b, hp: (rb(b), 20)),
                pl.BlockSpec((3, LANES), lambda b, hp: (0, hp)),
                pl.BlockSpec((3, LANES), lambda b, hp: (0, 4 + hp)),
                pl.BlockSpec((3, LANES), lambda b, hp: (0, 8 + hp)),
                pl.BlockSpec((2, 2 * A_HEADS, LANES), lambda b, hp: (0, 0, 0)),
                pl.BlockSpec((1, LANES), lambda b, hp: (0, 0)),
                st_spec, st_spec,
                pl.BlockSpec((CHUNK, CHUNK), lambda b, hp: (0, 0)),
                pl.BlockSpec((LANES, LANES), lambda b, hp: (0, 0))]
    args = [proj, proj, proj, proj, proj, conv_w, conv_w, conv_w, gate_p, o_gain2, s0f, s0b, tri, jbd]
    st_shape = jax.ShapeDtypeStruct((n_seq, A_HEADS, A_DH, A_DH), F32)
    tile = lambda dt: pltpu.VMEM((2, nct, LANES, LANES), dt)
    return pl.pallas_call(
        functools.partial(_deltanet_kernel, seq_len=seq_len, n_sub=n_sub),
        grid=(n_seq // n_sub, A_HEADS // 2),
        in_specs=in_specs,
        out_specs=[pl.BlockSpec((rows, LANES), lambda b, hp: (b, hp)), st_spec, st_spec],
        out_shape=[jax.ShapeDtypeStruct((n_seq * seq_len, A_WIDTH), F32), st_shape, st_shape],
        scratch_shapes=[tile(BF16), tile(BF16), tile(BF16), tile(BF16), tile(BF16),
                        pltpu.VMEM((2, nct, 8, LANES), F32),
                        pltpu.VMEM((2 * n_sub, LANES, LANES), F32),
                        pltpu.VMEM((rows, LANES), F32)],
        compiler_params=_cparams(("arbitrary", "arbitrary")),
        name="deltanet",
    )(*args)


def _dft_mats(n):
    idx = np.arange(n)
    ang = 2.0 * np.pi * ((idx[:, None] * idx[None, :]) % n) / n
    return np.cos(ang), np.sin(ang)


def _fnet_ctx_kernel(u_ref, cs_ref, dft_ref, o_ref):
    norm = 1.0 / math.sqrt(SEQ * LANES)
    for g in range(B_GROUPS):
        sl = slice(g * LANES, (g + 1) * LANES)
        p = _mm(u_ref[:, sl], cs_ref[...])
        stack = jnp.concatenate([p[:, :LANES], p[:, LANES:]], axis=0)
        o_ref[:, sl] = _mm(dft_ref[...], stack) * norm


def _fnet_ctx(proj, cs, dft):
    return pl.pallas_call(
        _fnet_ctx_kernel,
        grid=(BATCH,),
        in_specs=[pl.BlockSpec((SEQ, B_WIDTH), lambda b: (b, 4)),
                  pl.BlockSpec(cs.shape, lambda b: (0, 0)),
                  pl.BlockSpec(dft.shape, lambda b: (0, 0))],
        out_specs=pl.BlockSpec((SEQ, B_WIDTH), lambda b: (b, 0)),
        out_shape=jax.ShapeDtypeStruct((NC, B_WIDTH), F32),
        compiler_params=_cparams(("arbitrary",)),
        name="fnet_ctx",
    )(proj, cs, dft)


FN_SUB = 4


def _fnet_lat1_kernel(u_ref, ca_ref, m1_ref, twc_ref, tws_ref, o_ref):
    r = 64
    for j in range(FN_SUB):
        c = twc_ref[j]
        s = tws_ref[j]
        rows = slice(j * r, (j + 1) * r)
        for g in range(B_GROUPS):
            pa = _mm(u_ref[rows, g * LANES:(g + 1) * LANES], ca_ref[...])
            rhs = jnp.concatenate([pa[:, :2 * LANES], pa[:, 2 * LANES:]], axis=0)
            zz = _mm(m1_ref[...], rhs)
            zr, zi = zz[:, :LANES], zz[:, LANES:]
            o_ref[rows, 2 * g * LANES:(2 * g + 1) * LANES] = (zr * c + zi * s).astype(BF16)
            o_ref[rows, (2 * g + 1) * LANES:(2 * g + 2) * LANES] = (zi * c - zr * s).astype(BF16)


def _fnet_lat2_kernel(z_ref, m1_ref, o_ref):
    r = 64
    norm = 1.0 / math.sqrt(DEC_SEQ * LANES)
    for j in range(FN_SUB):
        rows = slice(j * r, (j + 1) * r)
        for g in range(B_GROUPS):
            rhs = jnp.concatenate([z_ref[rows, 2 * g * LANES:(2 * g + 1) * LANES],
                                   z_ref[rows, (2 * g + 1) * LANES:(2 * g + 2) * LANES]], axis=0)
            o_ref[rows, g * LANES:(g + 1) * LANES] = _mm(m1_ref[...], rhs) * norm


def _fnet_latent(proj, ca, m1, twc, tws):
    r = 64
    u = proj[NC:, 2048:2560].astype(BF16).reshape(DEC_BATCH, r, r, B_WIDTH)
    u = u.transpose(0, 2, 1, 3).reshape(DEC_BATCH * r * r, B_WIDTH)
    steps = DEC_BATCH * r // FN_SUB
    blk = FN_SUB * r
    z = pl.pallas_call(
        _fnet_lat1_kernel,
        grid=(steps,),
        in_specs=[pl.BlockSpec((blk, B_WIDTH), lambda s: (s, 0)),
                  pl.BlockSpec(ca.shape, lambda s: (0, 0)),
                  pl.BlockSpec(m1.shape, lambda s: (0, 0)),
                  pl.BlockSpec((FN_SUB, r, LANES), lambda s: (s % (r // FN_SUB), 0, 0)),
                  pl.BlockSpec((FN_SUB, r, LANES), lambda s: (s % (r // FN_SUB), 0, 0))],
        out_specs=pl.BlockSpec((blk, 2 * B_WIDTH), lambda s: (s, 0)),
        out_shape=jax.ShapeDtypeStruct((NL, 2 * B_WIDTH), BF16),
        compiler_params=_cparams(("arbitrary",)),
        name="fnet_lat1",
    )(u, ca, m1, twc, tws)
    z = z.reshape(DEC_BATCH, r, r, 2 * B_WIDTH).transpose(0, 2, 1, 3).reshape(NL, 2 * B_WIDTH)
    y = pl.pallas_call(
        _fnet_lat2_kernel,
        grid=(steps,),
        in_specs=[pl.BlockSpec((blk, 2 * B_WIDTH), lambda s: (s, 0)),
                  pl.BlockSpec(m1.shape, lambda s: (0, 0))],
        out_specs=pl.BlockSpec((blk, B_WIDTH), lambda s: (s, 0)),
        out_shape=jax.ShapeDtypeStruct((NL, B_WIDTH), F32),
        compiler_params=_cparams(("arbitrary",)),
        name="fnet_lat2",
    )(z, m1)
    return y.reshape(DEC_BATCH, r, r, B_WIDTH).transpose(0, 2, 1, 3).reshape(NL, B_WIDTH)


def _head_masks():
    lane = lax.broadcasted_iota(jnp.int32, (1, LANES), 1)
    return lane < NA_DH


def _attend_many(chains):
    s = [[_mm_nt(q, k) if b is None else _mm_nt(q, k) + b for k, v, b in kv] for q, kv in chains]
    m = [functools.reduce(jnp.maximum, [jnp.max(x, axis=-1, keepdims=True) for x in xs]) for xs in s]
    p = [[jnp.exp(x - mi) for x in xs] for xs, mi in zip(s, m)]
    l = [sum(jnp.sum(x, axis=-1, keepdims=True) for x in xs) for xs in p]
    o = [sum(_mm(x, v) for x, (k, v, b) in zip(xs, kv)) for xs, (q, kv) in zip(p, chains)]
    return [oi / li for oi, li in zip(o, l)]


def _na_ctx_kernel(q_ref, k_ref, v_ref, o_ref):
    first = _head_masks()
    n_split = 2
    rows = SEQ // n_split
    chains = []
    for p in range(NA_CTX_PAIRS):
        lanes = slice(p * LANES, (p + 1) * LANES)
        k = k_ref[:, lanes].astype(BF16)
        v = v_ref[:, lanes].astype(BF16)
        for j in range(n_split):
            q = q_ref[j * rows:(j + 1) * rows, lanes]
            for a in range(2):
                chains.append((jnp.where(first if a == 0 else jnp.logical_not(first), q * NA_SCALE, 0.0),
                               [(k, v, None)]))
    outs = _attend_many(chains)
    for p in range(NA_CTX_PAIRS):
        for j in range(n_split):
            i = 2 * (p * n_split + j)
            o_ref[j * rows:(j + 1) * rows, p * LANES:(p + 1) * LANES] = jnp.where(first, outs[i], outs[i + 1])


def _na_ctx(qkv):
    w = NA_CTX_PAIRS * LANES
    nblk = D // w
    return pl.pallas_call(
        _na_ctx_kernel,
        grid=(BATCH, nblk),
        in_specs=[pl.BlockSpec((SEQ, w), lambda b, hp: (b, hp)),
                  pl.BlockSpec((SEQ, w), lambda b, hp: (b, nblk + hp)),
                  pl.BlockSpec((SEQ, w), lambda b, hp: (b, 2 * nblk + hp))],
        out_specs=pl.BlockSpec((SEQ, w), lambda b, hp: (b, hp)),
        out_shape=jax.ShapeDtypeStruct((NC, D), F32),
        compiler_params=_cparams(("arbitrary", "arbitrary")),
        name="na_ctx",
    )(qkv, qkv, qkv)


def _na_lat_kernel(q_ref, k_ref, v_ref, kc_ref, vc_ref, tt_ref, o_ref):
    first = _head_masks()
    rows = DEC_SEQ // GRID_W
    kctx = kc_ref[...].astype(BF16)
    vctx = vc_ref[...].astype(BF16)
    nkeys = WIN_R * GRID_W

    def rows_body(j, carry):
        chains = []
        for t in range(NA_ROW_UNROLL):
            r = NA_ROW_UNROLL * j + t
            r0 = jnp.clip(r - WIN_R // 2, 0, rows - WIN_R)
            dr0 = r0 - r + (WIN_R - 1)
            q = q_ref[pl.ds(pl.multiple_of(r * GRID_W, GRID_W), GRID_W), :] * NA_SCALE
            kbase = pl.multiple_of(r0 * GRID_W, GRID_W)
            kl = k_ref[pl.ds(kbase, nkeys), :].astype(BF16)
            vl = v_ref[pl.ds(kbase, nkeys), :].astype(BF16)
            for a in range(2):
                qm = jnp.where(first if a == 0 else jnp.logical_not(first), q, 0.0)
                bias = jnp.concatenate([tt_ref[a, dr0 + 2 * i] for i in range(WIN_R // 2)], axis=1)
                chains.append((qm, [(kl, vl, bias), (kctx, vctx, None)]))
        outs = _attend_many(chains)
        for t in range(NA_ROW_UNROLL):
            r = NA_ROW_UNROLL * j + t
            o_ref[pl.ds(pl.multiple_of(r * GRID_W, GRID_W), GRID_W), :] = jnp.where(first, outs[2 * t], outs[2 * t + 1])
        return carry

    lax.fori_loop(0, rows // NA_ROW_UNROLL, rows_body, 0)


def _na_latent(qkv, cache_k2, cache_v2, tt2):
    rb0 = NC // DEC_SEQ
    return pl.pallas_call(
        _na_lat_kernel,
        grid=(DEC_BATCH, NA_HEADS // 2),
        in_specs=[pl.BlockSpec((DEC_SEQ, LANES), lambda b, hp: (rb0 + b, hp)),
                  pl.BlockSpec((DEC_SEQ, LANES), lambda b, hp: (rb0 + b, 8 + hp)),
                  pl.BlockSpec((DEC_SEQ, LANES), lambda b, hp: (rb0 + b, 16 + hp)),
                  pl.BlockSpec((None, 256, LANES), lambda b, hp: (b, 0, hp)),
                  pl.BlockSpec((None, 256, LANES), lambda b, hp: (b, 0, hp)),
                  pl.BlockSpec((2, 2 * WIN_R - 2, GRID_W, LANES), lambda b, hp: (hp, 0, 0, 0))],
        out_specs=pl.BlockSpec((DEC_SEQ, LANES), lambda b, hp: (b, hp)),
        out_shape=jax.ShapeDtypeStruct((NL, D), F32),
        compiler_params=_cparams(("arbitrary", "arbitrary")),
        name="na_latent",
    )(qkv, qkv, qkv, cache_k2, cache_v2, tt2)


def _rpb_tables(rpb):
    col = np.arange(GRID_W)
    start = np.clip(col - WIN_C // 2, 0, GRID_W - WIN_C)
    inside = (col[None, :] >= start[:, None]) & (col[None, :] < start[:, None] + WIN_C)
    w = GRID_W
    period = 2 * w - 1
    x = jnp.pad(rpb, ((0, 0), (0, 0), (w - WIN_C, w - WIN_C)))
    flat = jnp.tile(x, (1, 1, w))[:, :, w - 1:w - 1 + w * (period - 1)]
    t = flat.reshape(NA_HEADS, 2 * WIN_R - 1, w, period - 1)[..., :w]
    t = jnp.where(inside[None, None], t, NEG)
    return jnp.concatenate([t[:, :-1], t[:, 1:]], axis=-1)


ROUTE_ROWS = 40


def _router_kernel(*refs, n_in):
    a_refs = refs[:2 * n_in]
    w_refs = refs[2 * n_in:3 * n_in]
    x_refs = refs[3 * n_in:-11]
    m_ref, g_ref, wr_ref, br_ref, tri_ref, xn_ref, xf_ref, ri_ref, rw_ref, cnt_ref, base_scr = refs[-11:]
    i = pl.program_id(0)

    @pl.when(i == 0)
    def _():
        base_scr[...] = jnp.zeros_like(base_scr)
        cnt_ref[...] = jnp.zeros_like(cnt_ref)

    acc = None
    for j, w_ref in enumerate(w_refs):
        part = _mm(_token_rows(a_refs[2 * j:2 * j + 2]), w_ref[...])
        acc = part if acc is None else acc + part
    x_new = _token_rows(x_refs) + m_ref[2:3, :] * acc
    xn_ref[...] = x_new
    h = _modulated_norm(x_new, m_ref, g_ref, 3, 4)
    xf_ref[...] = _pack_bf16_pairs(h[:, :D // 2], h[:, D // 2:])
    logits = lax.dot_general(wr_ref[...], h, (((1,), (1,)), ((), ())), preferred_element_type=F32,
                             precision=HIGHEST) + br_ref[:, 0:1]
    row = lax.broadcasted_iota(jnp.int32, logits.shape, 0)
    cmax = lambda x: jnp.max(x, axis=0, keepdims=True)
    cmin = lambda x: jnp.min(x, axis=0, keepdims=True)
    csum = lambda x: jnp.sum(x, axis=0, keepdims=True)

    gmask = row < N_GROUPS
    mg = cmax(jnp.where(gmask, logits, NEG))
    eg = jnp.where(gmask, jnp.exp(jnp.where(gmask, logits - mg, NEG)), 0.0)
    pg = eg / csum(eg)
    p_grp = cmax(pg)
    grp = cmin(jnp.where(jnp.logical_and(gmask, pg == p_grp), row, ROUTE_ROWS))
    lo = N_GROUPS + grp * EXP_PER_GROUP
    emask = jnp.logical_and(row >= lo, row < lo + EXP_PER_GROUP)
    me = cmax(jnp.where(emask, logits, NEG))
    ee = jnp.where(emask, jnp.exp(jnp.where(emask, logits - me, NEG)), 0.0)
    pe = ee / csum(ee)
    p1 = cmax(pe)
    i1 = cmin(jnp.where(jnp.logical_and(emask, pe == p1), row, ROUTE_ROWS))
    m2 = jnp.logical_and(emask, row != i1)
    p2 = cmax(jnp.where(m2, pe, -1.0))
    i2 = cmin(jnp.where(jnp.logical_and(m2, pe == p2), row, ROUTE_ROWS))
    den = p1 + p2
    w1 = p_grp * p1 / den
    w2 = p_grp * p2 / den

    sel1 = row == i1
    sel2 = row == i2
    oh = jnp.where(jnp.logical_or(sel1, sel2), 1.0, 0.0).astype(BF16)
    before = jnp.dot(oh, tri_ref[...], preferred_element_type=F32) + base_scr[:, 0:1]
    rank1 = csum(jnp.where(sel1, before, 0.0))
    rank2 = csum(jnp.where(sel2, before, 0.0))
    base_scr[...] = base_scr[...] + jnp.sum(oh.astype(F32), axis=1, keepdims=True)
    cnt_ref[...] = cnt_ref[...] + lax.dot_general(jnp.ones((8, TM), BF16), oh, (((1,), (1,)), ((), ())),
                                                  preferred_element_type=F32)
    sub = lax.broadcasted_iota(jnp.int32, (8, TM), 0)
    ri_ref[...] = jnp.where(sub == 0, i1 - N_GROUPS, jnp.where(sub == 1, i2 - N_GROUPS,
                  jnp.where(sub == 2, rank1.astype(jnp.int32), jnp.where(sub == 3, rank2.astype(jnp.int32), 0))))
    rw_ref[...] = jnp.where(sub == 0, w1, jnp.where(sub == 1, w2, 0.0))


def _router(a_pairs, w_list, x, m3, gain, wr_t, br_t, tri_upper):
    in_specs, args = [], []
    for pair in a_pairs:
        specs, ops = _token_specs(pair)
        in_specs += specs
        args += ops
    x_specs, x_args = _token_specs(x)
    in_specs += ([pl.BlockSpec(w.shape, lambda i: (0, 0)) for w in w_list] + x_specs
                 + [pl.BlockSpec((None, 6, D), lambda i: (_cond_row(i), 0, 0)),
                    pl.BlockSpec((1, D), lambda i: (0, 0)),
                    pl.BlockSpec((ROUTE_ROWS, D), lambda i: (0, 0)),
                    pl.BlockSpec((ROUTE_ROWS, LANES), lambda i: (0, 0)),
                    pl.BlockSpec((TM, TM), lambda i: (0, 0))])
    return pl.pallas_call(
        functools.partial(_router_kernel, n_in=len(a_pairs)),
        grid=(N // TM,),
        in_specs=in_specs,
        out_specs=[pl.BlockSpec((TM, D), lambda i: (i, 0)),
                   pl.BlockSpec((TM, D // 2), lambda i: (i, 0)),
                   pl.BlockSpec((8, TM), lambda i: (0, i)),
                   pl.BlockSpec((8, TM), lambda i: (0, i)),
                   pl.BlockSpec((8, ROUTE_ROWS), lambda i: (0, 0))],
        out_shape=[jax.ShapeDtypeStruct((N, D), F32),
                   jax.ShapeDtypeStruct((N, D // 2), jnp.int32),
                   jax.ShapeDtypeStruct((8, N), jnp.int32),
                   jax.ShapeDtypeStruct((8, N), F32),
                   jax.ShapeDtypeStruct((8, ROUTE_ROWS), F32)],
        scratch_shapes=[pltpu.VMEM((ROUTE_ROWS, LANES), F32)],
        compiler_params=_cparams(("arbitrary",)),
        name="router",
    )(*args, *w_list, *x_args, m3, gain, wr_t, br_t, tri_upper)


def _row_copy(src_hbm, row, dst, r, sem):
    return pltpu.make_async_copy(src_hbm.at[pl.ds(row, 1)], dst.at[pl.ds(r, 1)], sem)


def _expert_kernel(be_ref, nu_ref, d_ref, pad_ref, xp_hbm, wg_ref, wu_ref, wd_ref, ys_ref, xres, xbuf, st_ref, sem):
    del be_ref
    b = pl.program_id(0)
    half = D // 2

    def gather(blk, slot, part=0, parts=1):
        base = blk * MOE_ROWS
        for r in range(part * MOE_ROWS // parts, (part + 1) * MOE_ROWS // parts):
            xbuf[slot, pl.ds(r, 1), :] = xres[pl.ds(st_ref[base + r], 1), :]

    @pl.when(b == 0)
    def _():
        cp = pltpu.make_async_copy(xp_hbm, xres, sem.at[0])
        cp.start()

        def clear_tail(e, c):
            def clear(s_, c2):
                st_ref[s_] = 0
                return c2

            return lax.fori_loop(pad_ref[2 * e], pad_ref[2 * e + 1], clear, c)

        lax.fori_loop(0, N_EXPERTS + 1, clear_tail, 0)

        def invert(j, c):
            slots = [d_ref[16 * j + t] for t in range(16)]
            for t in range(16):
                st_ref[slots[t]] = 8 * j + t // 2
            return c

        lax.fori_loop(0, 2 * N // 16, invert, 0)
        cp.wait()
        gather(0, 0)

    @pl.when(b < nu_ref[0])
    def _():
        nxt = (jnp.minimum(b + 1, N_SLOT_BLOCKS - 1), (b + 1) % 2)
        x_lo, x_hi = _unpack_bf16_pairs(xbuf[b % 2])
        dot = functools.partial(jnp.dot, preferred_element_type=F32)
        gather(*nxt, 0, 8)
        g = dot(x_lo, wg_ref[:half, :].astype(BF16))
        gather(*nxt, 1, 8)
        g = g + dot(x_hi, wg_ref[half:, :].astype(BF16))
        gather(*nxt, 2, 8)
        u = dot(x_lo, wu_ref[:half, :].astype(BF16))
        gather(*nxt, 3, 8)
        u = u + dot(x_hi, wu_ref[half:, :].astype(BF16))
        hb = (_silu(g) * u).astype(BF16)
        quarter = D // 4
        for j in range(4):
            gather(*nxt, 4 + j, 8)
            cols = slice(j * quarter, (j + 1) * quarter)
            ys_ref[:, cols] = jnp.dot(hb, wd_ref[:, cols].astype(BF16), preferred_element_type=F32)

    @pl.when(b >= nu_ref[0])
    def _():
        ys_ref[...] = jnp.zeros_like(ys_ref)


def _experts(block_e, n_used, dest, pad_ranges, xp, w_gate, w_up, w_down, layer):
    grid_spec = pltpu.PrefetchScalarGridSpec(
        num_scalar_prefetch=4,
        grid=(N_SLOT_BLOCKS,),
        in_specs=[pl.BlockSpec(memory_space=pl.ANY),
                  pl.BlockSpec((None, None, D, D_EXPERT), lambda b, be, nu, st, pd: (layer, be[b], 0, 0)),
                  pl.BlockSpec((None, None, D, D_EXPERT), lambda b, be, nu, st, pd: (layer, be[b], 0, 0)),
                  pl.BlockSpec((None, None, D_EXPERT, D), lambda b, be, nu, st, pd: (layer, be[b], 0, 0))],
        out_specs=pl.BlockSpec((MOE_ROWS, D), lambda b, be, nu, st, pd: (b, 0)),
        scratch_shapes=[pltpu.VMEM((N, D // 2), jnp.int32),
                        pltpu.VMEM((2, MOE_ROWS, D // 2), jnp.int32),
                        pltpu.SMEM((N_SLOT_BLOCKS * MOE_ROWS,), jnp.int32),
                        pltpu.SemaphoreType.DMA((1,))])
    return pl.pallas_call(
        _expert_kernel,
        grid_spec=grid_spec,
        out_shape=jax.ShapeDtypeStruct((N_SLOT_BLOCKS * MOE_ROWS, D), F32),
        compiler_params=pltpu.CompilerParams(dimension_semantics=("arbitrary",), vmem_limit_bytes=EXPERT_VMEM_LIMIT),
        name="experts",
    )(block_e, n_used, dest, pad_ranges, xp, w_gate, w_up, w_down)


def _combine_kernel(d_ref, ys_hbm, x_ref, m_ref, rw_ref, fn_ref, o_ref, buf, sem, *, final, tile0, n_tiles):
    i = pl.program_id(0)

    def issue(tile, slot):
        base = (tile0 + tile) * (2 * TM)
        for r in range(TM):
            for kk in range(2):
                _row_copy(ys_hbm, d_ref[base + 2 * r + kk], buf.at[slot, kk], r, sem.at[slot]).start()

    @pl.when(i == 0)
    def _():
        issue(0, 0)

    @pl.when(i + 1 < n_tiles)
    def _():
        issue(i + 1, (i + 1) % 2)

    slot = i % 2
    for kk in range(2):
        pltpu.make_async_copy(ys_hbm.at[pl.ds(0, TM)], buf.at[slot, kk], sem.at[slot]).wait()
    w = rw_ref[...]
    y = w[:, 0:1] * buf[slot, 0] + w[:, 1:2] * buf[slot, 1]
    out = x_ref[...] + m_ref[5:6, :] * y
    if final:
        ms = jnp.mean(out * out, axis=-1, keepdims=True)
        out = out * lax.rsqrt(ms + EPS) * fn_ref[...]
    o_ref[...] = out


def _combine(dest_flat, ys, x, m3, rw, final_norm, final, tile0=0, n_tiles=N // TM):
    grid_spec = pltpu.PrefetchScalarGridSpec(
        num_scalar_prefetch=1,
        grid=(n_tiles,),
        in_specs=[pl.BlockSpec(memory_space=pl.ANY),
                  pl.BlockSpec((TM, D), lambda i, d: (tile0 + i, 0)),
                  pl.BlockSpec((None, 6, D), lambda i, d: (_cond_row(tile0 + i), 0, 0)),
                  pl.BlockSpec((TM, 2), lambda i, d: (tile0 + i, 0)),
                  pl.BlockSpec((1, D), lambda i, d: (0, 0))],
        out_specs=pl.BlockSpec((TM, D), lambda i, d: (i, 0)),
        scratch_shapes=[pltpu.VMEM((2, 2, TM, D), F32), pltpu.SemaphoreType.DMA((2,))])
    return pl.pallas_call(
        functools.partial(_combine_kernel, final=final, tile0=tile0, n_tiles=n_tiles),
        grid_spec=grid_spec,
        out_shape=jax.ShapeDtypeStruct((n_tiles * TM, D), F32),
        compiler_params=_cparams(("arbitrary",)),
        name="combine",
    )(dest_flat, ys, x, m3, rw, final_norm)


def _hier_moe(a_pairs, w_list, x, m3, gain, layer, w_rg, b_rg, w_re, b_re, w_gate, w_up, w_down, tri_tm,
              final_norm, final):
    pad = ROUTE_ROWS - N_GROUPS - N_EXPERTS
    wr_t = jnp.concatenate([w_rg.T, w_re.transpose(0, 2, 1).reshape(N_EXPERTS, D), jnp.zeros((pad, D), F32)], axis=0)
    br_t = jnp.broadcast_to(jnp.concatenate([b_rg, b_re.reshape(N_EXPERTS), jnp.zeros((pad,), F32)])[:, None],
                            (ROUTE_ROWS, LANES))
    x, xf, ri_t, rw_t, cnt = _router(a_pairs, w_list, x, m3, gain, wr_t, br_t, tri_tm)
    e_idx = ri_t[0:2].T
    rank = ri_t[2:4].T
    rw = rw_t[0:2].T
    counts = cnt[0, N_GROUPS:N_GROUPS + N_EXPERTS].astype(jnp.int32)
    padded = (counts + MOE_ROWS - 1) // MOE_ROWS * MOE_ROWS
    end_pad = jnp.cumsum(padded)
    start_pad = end_pad - padded
    experts = jnp.arange(N_EXPERTS, dtype=jnp.int32)
    start_of = jnp.sum(jnp.where(e_idx[:, :, None] == experts, start_pad, 0), axis=-1)
    dest = (start_of + rank).reshape(-1).astype(jnp.int32)
    block_start = jnp.arange(N_SLOT_BLOCKS, dtype=jnp.int32) * MOE_ROWS
    block_e = jnp.minimum(jnp.sum((end_pad[None, :] <= block_start[:, None]).astype(jnp.int32), axis=1),
                          N_EXPERTS - 1)
    n_used = (end_pad[-1:] // MOE_ROWS).astype(jnp.int32)
    tail = jnp.stack([end_pad[-1], jnp.minimum(end_pad[-1] + MOE_ROWS, N_SLOT_BLOCKS * MOE_ROWS)])
    pad_ranges = jnp.concatenate([jnp.stack([start_pad + counts, end_pad], axis=1).reshape(-1), tail]).astype(jnp.int32)
    ys = _experts(block_e, n_used, dest, pad_ranges, xf, w_gate, w_up, w_down, layer)
    if not final:
        return _combine(dest, ys, x, m3, rw, final_norm, False)
    nct = NC // TM
    return (_combine(dest, ys, x, m3, rw, final_norm, True, 0, nct),
            _combine(dest, ys, x, m3, rw, final_norm, True, nct, N // TM - nct))


def kernel(x_prompt, x_sample, state_A_fwd, state_A_bwd, cache_k, cache_v, c, c_ctx, mod_w, mod_b, norm_mix, norm_ffn, ab_w_in, ab_conv, ab_a_log, ab_dt_bias, ab_o_gain, ab_w_out, na_w_qkv, na_rpb, na_w_out, moe_w_rg, moe_b_rg, moe_w_re, moe_b_re, moe_w_gate, moe_w_up, moe_w_down, final_norm):
    x = (x_prompt.reshape(NC, D), x_sample.reshape(NL, D))
    cond8 = jnp.concatenate([c_ctx[None, :], c, jnp.zeros((8 - 1 - DEC_BATCH, D), F32)], axis=0)
    mods = _ada_params(cond8, mod_w, mod_b).reshape(DEPTH, 8, 6, D)

    tri_tm = jnp.asarray(np.triu(np.ones((TM, TM)), 1), BF16)
    tri_c = jnp.asarray(np.tril(np.ones((CHUNK, CHUNK))), F32)
    half = np.arange(LANES) < A_DH
    jbd = jnp.asarray((half[:, None] == half[None, :]).astype(np.float32))
    fn = final_norm[None, :]

    m3 = mods[0]
    w_in = ab_w_in[0]
    w_in = jnp.concatenate([w_in[:, :2048], w_in[:, 2080:2592], w_in[:, 2048:2080],
                            jnp.zeros((D, AB_COLS - 2592), F32)], axis=1).astype(BF16)
    proj = _modproj(x, m3, norm_mix[0][None, :], w_in, 896)
    gate_p = jnp.broadcast_to(jnp.stack([ab_a_log[0].reshape(-1), ab_dt_bias[0].reshape(-1)])[:, :, None],
                              (2, 2 * A_HEADS, LANES))
    o_gain2 = jnp.tile(ab_o_gain[0], 2)[None, :]
    zeros_state = jnp.zeros((BATCH, A_HEADS, A_DH, A_DH), F32)
    mix_a_c, s_f, s_b = _deltanet(proj, ab_conv[0], gate_p, o_gain2, zeros_state, zeros_state, tri_c, jbd,
                                  seq_len=SEQ, n_seq=BATCH, n_sub=4, row_blk0=0)
    mix_a_l, _, _ = _deltanet(proj, ab_conv[0], gate_p, o_gain2, state_A_fwd[:, 0], state_A_bwd[:, 0], tri_c, jbd,
                              seq_len=DEC_SEQ, n_seq=DEC_BATCH, n_sub=1, row_blk0=NC // DEC_SEQ)

    cc, sc = _dft_mats(LANES)
    ct, st = _dft_mats(SEQ)
    c64, s64 = _dft_mats(64)
    cs = jnp.asarray(np.concatenate([cc, sc], axis=1), BF16)
    dft = jnp.asarray(np.concatenate([ct, -st], axis=1), BF16)
    ca = jnp.asarray(np.concatenate([cc, -sc, -sc, -cc], axis=1), BF16)
    m1 = jnp.asarray(np.concatenate([c64, s64], axis=1), BF16)
    tw_idx = np.arange(64)
    tw_ang = 2.0 * np.pi * (tw_idx[:, None] * tw_idx[None, :]) / DEC_SEQ
    twc = jnp.broadcast_to(jnp.asarray(np.cos(tw_ang), F32)[:, :, None], (64, 64, LANES))
    tws = jnp.broadcast_to(jnp.asarray(np.sin(tw_ang), F32)[:, :, None], (64, 64, LANES))
    mix_b_c = _fnet_ctx(proj, cs, dft)
    mix_b_l = _fnet_latent(proj, ca, m1, twc, tws)

    w_out = ab_w_out[0].astype(BF16)
    x = _hier_moe([(mix_a_c, mix_a_l), (mix_b_c, mix_b_l)], [w_out[:A_WIDTH], w_out[A_WIDTH:]], x, m3,
                  norm_ffn[0][None, :], 0, moe_w_rg[0], moe_b_rg[0], moe_w_re[0], moe_b_re[0],
                  moe_w_gate, moe_w_up, moe_w_down, tri_tm, fn, False)

    m3 = mods[1]
    qkv = _modproj(x, m3, norm_mix[1][None, :], na_w_qkv[0].astype(BF16), 512)
    attn_c = _na_ctx(qkv)
    attn_l = _na_latent(qkv, cache_k[:, 0].reshape(DEC_BATCH, 256, D), cache_v[:, 0].reshape(DEC_BATCH, 256, D),
                        _rpb_tables(na_rpb[0]))
    y_c, y_l = _hier_moe([(attn_c, attn_l)], [na_w_out[0].astype(BF16)], x, m3, norm_ffn[1][None, :], 1, moe_w_rg[1], moe_b_rg[1], moe_w_re[1], moe_b_re[1],
                         moe_w_gate, moe_w_up, moe_w_down, tri_tm, fn, True)

    new_k = qkv[:NC, D:2 * D].reshape(BATCH, 1, SEQ, NA_HEADS, NA_DH)
    new_v = qkv[:NC, 2 * D:].reshape(BATCH, 1, SEQ, NA_HEADS, NA_DH)
    return (y_c.reshape(BATCH, SEQ, D), y_l.reshape(DEC_BATCH, DEC_SEQ, D),
            s_f[:, None], s_b[:, None], new_k, new_v)
```

```python
import functools
import math

import numpy as np
import jax
import jax.numpy as jnp
from jax import lax
from jax.experimental import pallas as pl
from jax.experimental.pallas import tpu as pltpu

F32 = jnp.float32
BF16 = jnp.bfloat16
HIGHEST = lax.Precision.HIGHEST

D = 1024
BATCH, SEQ = 32, 256
DEC_BATCH, DEC_SEQ = 2, 4096
NC = BATCH * SEQ
NL = DEC_BATCH * DEC_SEQ
N = NC + NL
DEPTH = 2
GRID_W = 64
A_DH = 64
A_HEADS = 8
A_WIDTH = 512
CHUNK = 64
B_WIDTH = 512
B_GROUPS = 4
NA_DH = 64
NA_HEADS = 16
WIN_R, WIN_C = 8, 16
N_GROUPS, EXP_PER_GROUP, N_EXPERTS = 4, 8, 32
D_EXPERT = 512
EPS = 1e-6

LANES = 128
TM = 256
MOE_ROWS = 256
N_SLOT_BLOCKS = (2 * N) // MOE_ROWS + N_EXPERTS
AB_COLS = 2688
VMEM_LIMIT = 56 * 1024 * 1024
EXPERT_VMEM_LIMIT = 60 * 1024 * 1024
NEG = -1e30


def _cparams(sem):
    return pltpu.CompilerParams(dimension_semantics=sem, vmem_limit_bytes=VMEM_LIMIT)


def _mm(a, b):
    return jnp.dot(a.astype(BF16), b.astype(BF16), preferred_element_type=F32)


def _mm_nt(a, b):
    return lax.dot_general(a.astype(BF16), b.astype(BF16), (((1,), (1,)), ((), ())),
                           preferred_element_type=F32)


SOLVE_BLK = 16
NA_SCALE = NA_DH ** -0.5
NA_CTX_PAIRS = 2
NA_ROW_UNROLL = 8
TERM_UNROLL = 4


def _unit_lower_solve(a_mat, rhs, same_blk, eye):
    dg = jnp.where(same_blk, a_mat, 0.0)
    p = -dg
    dinv = eye + p
    for _ in range(int(math.log2(SOLVE_BLK)) - 1):
        p = _mm(p, p)
        dinv = dinv + _mm(p, dinv)
    mp = -_mm(dinv, a_mat - dg)
    y = _mm(dinv, rhs)
    y = y + _mm(mp, y)
    for _ in range(int(math.log2(a_mat.shape[0] // SOLVE_BLK)) - 1):
        mp = _mm(mp, mp)
        y = y + _mm(mp, y)
    return y


def _unit_lower_solve_many(a_mats, rhs, same_blk, eye):
    off = [jnp.where(same_blk, 0.0, a).astype(BF16) for a in a_mats]
    p = [jnp.where(same_blk, -a, 0.0).astype(BF16) for a in a_mats]
    dinv = [(eye + x.astype(F32)).astype(BF16) for x in p]
    for _ in range(int(math.log2(SOLVE_BLK)) - 1):
        p = [_mm(x, x).astype(BF16) for x in p]
        dinv = [(di.astype(F32) + _mm(x, di)).astype(BF16) for x, di in zip(p, dinv)]
    mp = [(-_mm(di, o)).astype(BF16) for di, o in zip(dinv, off)]
    y = [_mm(di, r) for di, r in zip(dinv, rhs)]
    y = [yi + _mm(m, yi) for m, yi in zip(mp, y)]
    for _ in range(int(math.log2(a_mats[0].shape[0] // SOLVE_BLK)) - 1):
        mp = [_mm(m, m).astype(BF16) for m in mp]
        y = [yi + _mm(m, yi) for m, yi in zip(mp, y)]
    return y


def _mm_split(a, b, parts, split_rhs=False):
    x = b if split_rhs else a
    acc = None
    for _ in range(parts):
        piece = x.astype(BF16)
        x = x - piece.astype(F32)
        term = (jnp.dot(a.astype(BF16), piece, preferred_element_type=F32) if split_rhs
                else jnp.dot(piece, b.astype(BF16), preferred_element_type=F32))
        acc = term if acc is None else acc + term
    return acc


def _mm_hi(a, b):
    return jnp.dot(a, b, preferred_element_type=F32, precision=HIGHEST)


def _silu(x):
    return x * jax.nn.sigmoid(x)


def _bf16_bits(x):
    b = lax.bitcast_convert_type(x, jnp.int32)
    return b + 0x7FFF + (lax.shift_right_logical(b, jnp.int32(16)) & 1)


_HIGH16 = -65536


def _pack_bf16_pairs(a, b):
    return lax.shift_right_logical(_bf16_bits(a), jnp.int32(16)) | (_bf16_bits(b) & _HIGH16)


def _unpack_bf16_pairs(p):
    a = lax.bitcast_convert_type(lax.shift_left(p, jnp.int32(16)), F32)
    b = lax.bitcast_convert_type(p & _HIGH16, F32)
    return a.astype(BF16), b.astype(BF16)


def _cond_row(i):
    return jnp.where(i < NC // TM, 0, 1 + (i - NC // TM) // (DEC_SEQ // TM))


def _modulated_norm(x, m_ref, g_ref, shift_idx, scale_idx):
    ms = jnp.mean(x * x, axis=-1, keepdims=True)
    y = x * lax.rsqrt(ms + EPS) * g_ref[...]
    return y * (1.0 + m_ref[scale_idx:scale_idx + 1, :]) + m_ref[shift_idx:shift_idx + 1, :]


def _ada_kernel(cond_ref, w_ref, b_ref, o_ref):
    o_ref[...] = _mm_hi(_silu(cond_ref[...]), w_ref[...]) + b_ref[...]


def _ada_params(cond8, mod_w, mod_b):
    tn = 1536
    return pl.pallas_call(
        _ada_kernel,
        grid=(DEPTH, 6 * D // tn),
        in_specs=[pl.BlockSpec((8, D), lambda l, j: (0, 0)),
                  pl.BlockSpec((None, D, tn), lambda l, j: (l, 0, j)),
                  pl.BlockSpec((None, 1, tn), lambda l, j: (l, 0, j))],
        out_specs=pl.BlockSpec((None, 8, tn), lambda l, j: (l, 0, j)),
        out_shape=jax.ShapeDtypeStruct((DEPTH, 8, 6 * D), F32),
        compiler_params=_cparams(("arbitrary", "arbitrary")),
        name="ada_params",
    )(cond8, mod_w, mod_b.reshape(DEPTH, 1, 6 * D))


def _token_specs(x):
    nct = NC // TM
    if isinstance(x, tuple):
        return ([pl.BlockSpec((TM, x[0].shape[1]), lambda i: (jnp.minimum(i, nct - 1), 0)),
                 pl.BlockSpec((TM, x[1].shape[1]), lambda i: (jnp.maximum(i - nct, 0), 0))], list(x))
    return [pl.BlockSpec((TM, x.shape[1]), lambda i: (i, 0))], [x]


def _token_rows(refs):
    if len(refs) == 1:
        return refs[0][...]
    return jnp.where(pl.program_id(0) < NC // TM, refs[0][...], refs[1][...])


def _modproj_kernel(*refs, n_chunk):
    m_ref, g_ref, w_ref, o_ref = refs[-4:]
    hb = _modulated_norm(_token_rows(refs[:-4]), m_ref, g_ref, 0, 1).astype(BF16)
    for j in range(o_ref.shape[1] // n_chunk):
        sl = slice(j * n_chunk, (j + 1) * n_chunk)
        o_ref[:, sl] = jnp.dot(hb, w_ref[:, sl], preferred_element_type=F32)


def _modproj(x, m3, gain, w_bf16, n_chunk):
    nout = w_bf16.shape[1]
    x_specs, x_args = _token_specs(x)
    return pl.pallas_call(
        functools.partial(_modproj_kernel, n_chunk=n_chunk),
        grid=(N // TM,),
        in_specs=x_specs + [pl.BlockSpec((None, 6, D), lambda i: (_cond_row(i), 0, 0)),
                            pl.BlockSpec((1, D), lambda i: (0, 0)),
                            pl.BlockSpec((D, nout), lambda i: (0, 0))],
        out_specs=pl.BlockSpec((TM, nout), lambda i: (i, 0)),
        out_shape=jax.ShapeDtypeStruct((N, nout), F32),
        compiler_params=_cparams(("arbitrary",)),
        name="modproj",
    )(*x_args, m3, gain, w_bf16)


def _deltanet_kernel(q_ref, k_ref, v_ref, z_ref, ab_ref, cq_ref, ck_ref, cv_ref, gp_ref, og_ref,
                     s0f_ref, s0b_ref, tri_ref, jbd_ref, o_ref, sf_ref, sb_ref,
                     u_s, w_s, qd_s, at_s, kt_s, ge_s, st_s, ob, *, seq_len, n_sub):
    hp = pl.program_id(1)
    C = CHUNK
    nc = seq_len // C
    nct = n_sub * nc
    P = LANES
    lane = lax.broadcasted_iota(jnp.int32, (C, P), 1)
    row = lax.broadcasted_iota(jnp.int32, (C, P), 0)
    first_head = lane < A_DH
    ri = lax.broadcasted_iota(jnp.int32, (P, P), 0)
    ci = lax.broadcasted_iota(jnp.int32, (P, P), 1)
    same_head = (ri < C) == (ci < C)
    same_blk = (ri // SOLVE_BLK) == (ci // SOLVE_BLK)
    eye = jnp.where(ri == ci, 1.0, 0.0)
    jbd = jbd_ref[...]
    lincl = tri_ref[...]
    cum_b = [lincl.T.astype(BF16), lincl.astype(BF16)]
    incl_m = [jnp.logical_and(same_head, ri >= ci), jnp.logical_and(same_head, ri <= ci)]
    strict_m = [jnp.logical_and(same_head, ri > ci), jnp.logical_and(same_head, ri < ci)]
    neg_a = -jnp.exp(gp_ref[0])
    dt_b = gp_ref[1]

    def conv_silu(ref, w_ref, c):
        base = pl.multiple_of(c * C, C)
        cs = c % nc
        xc = ref[pl.ds(base, C), :]
        pbase = pl.multiple_of(jnp.maximum(base - 8, 0), 8)
        nbase = pl.multiple_of(jnp.minimum(base + C, n_sub * seq_len - 8), 8)
        prev_row = ref[pl.ds(pbase, 8), :][7:8, :] * jnp.where(cs > 0, 1.0, 0.0)
        next_row = ref[pl.ds(nbase, 8), :][0:1, :] * jnp.where(cs < nc - 1, 1.0, 0.0)
        x_prev = jnp.where(row == 0, prev_row, pltpu.roll(xc, 1, 0))
        x_next = jnp.where(row == C - 1, next_row, pltpu.roll(xc, C - 1, 0))
        y = w_ref[0:1, :] * x_prev + w_ref[1:2, :] * xc + w_ref[2:3, :] * x_next
        return _silu(y)

    def stack(x):
        return jnp.concatenate([jnp.where(first_head, x, 0.0), jnp.where(first_head, 0.0, x)], axis=0)

    def chunk_inputs(c):
        base = pl.multiple_of(c * C, C)
        q = conv_silu(q_ref, cq_ref, c)
        k = conv_silu(k_ref, ck_ref, c)
        v = conv_silu(v_ref, cv_ref, c)
        return q, k, v, ab_ref[pl.ds(base, C), :].T

    sub8 = lax.broadcasted_iota(jnp.int32, (A_HEADS, C), 0)

    def pair_row(x8):
        r0 = jnp.sum(jnp.where(sub8 == 2 * hp, x8, 0.0), axis=0, keepdims=True)
        r1 = jnp.sum(jnp.where(sub8 == 2 * hp + 1, x8, 0.0), axis=0, keepdims=True)
        return jnp.concatenate([r0, r1], axis=1)

    def chain_gates(ab_t, d):
        a8 = ab_t[d * A_HEADS:(d + 1) * A_HEADS, :]
        b8 = ab_t[(2 + d) * A_HEADS:(3 + d) * A_HEADS, :]
        g8 = neg_a[d * A_HEADS:(d + 1) * A_HEADS, :C] * jax.nn.softplus(a8 + dt_b[d * A_HEADS:(d + 1) * A_HEADS, :C])
        gc8 = _mm_split(g8, cum_b[d], 3)
        tot8 = jnp.broadcast_to(jnp.sum(g8, axis=-1, keepdims=True), (A_HEADS, C))
        gc_row, beta_row, tot_row = pair_row(gc8), pair_row(jax.nn.sigmoid(b8)), pair_row(tot8)
        cols = jnp.concatenate([gc_row, beta_row, tot_row, jnp.zeros((5, P), F32)], axis=0).T
        return gc_row, tot_row, cols[:, 0:1], cols[:, 1:2], cols[:, 2:3]

    def terms_body(j, carry):
        cs = [TERM_UNROLL * j + t for t in range(TERM_UNROLL)]
        ins = [chunk_inputs(c) for c in cs]
        qsq = [_mm_split(x[0] * x[0], jbd, 2) for x in ins]
        ksq = [_mm_split(x[1] * x[1], jbd, 2) for x in ins]
        qs = [x[0] * lax.rsqrt(s + EPS) * (A_DH ** -0.5) for x, s in zip(ins, qsq)]
        ks = [x[1] * lax.rsqrt(s + EPS) for x, s in zip(ins, ksq)]
        qst = [stack(x) for x in qs]
        kst = [stack(x) for x in ks]
        vst = [stack(x[2]) for x in ins]
        kst_t = [x.T for x in kst]
        kk = [_mm_nt(x, x) for x in kst]
        qk = [_mm_nt(x, y) for x, y in zip(qst, kst)]
        chains = [(t, d) for t in range(TERM_UNROLL) for d in range(2)]
        gates = [chain_gates(ins[t][3], d) for t, d in chains]
        decay, e_gc = [], []
        for (t, d), (gc_row, tot_row, gc_col, beta_col, tot_col) in zip(chains, gates):
            diff = jnp.broadcast_to(gc_col, (P, P)) - jnp.broadcast_to(gc_row, (P, P))
            decay.append(jnp.where(incl_m[d], jnp.exp(jnp.where(incl_m[d], diff, 0.0)), 0.0))
            e_gc.append(jnp.exp(gc_col))
        a_mats = [jnp.where(strict_m[d], g[3] * kk[t] * dc, 0.0) for (t, d), g, dc in zip(chains, gates, decay)]
        rhs = [vst[t] * g[3] + pltpu.roll(kst[t] * (g[3] * e), A_DH, 1)
               for (t, d), g, e in zip(chains, gates, e_gc)]
        xs = _unit_lower_solve_many(a_mats, rhs, same_blk, eye)
        for (t, d), x, g, e, dc in zip(chains, xs, gates, e_gc, decay):
            c = cs[t]
            gc_row, tot_row = g[0], g[1]
            u_s[d, c] = jnp.where(same_head, x, 0.0).astype(BF16)
            w_s[d, c] = pltpu.roll(jnp.where(same_head, 0.0, x), A_DH, 1).astype(BF16)
            qd_s[d, c] = (qst[t] * e).astype(BF16)
            at_s[d, c] = jnp.where(incl_m[d], qk[t] * dc, 0.0).astype(BF16)
            kt_s[d, c] = (kst_t[t] * jnp.exp(tot_row - gc_row)).astype(BF16)
            ge_s[d, c] = jnp.broadcast_to(jnp.exp(tot_row), (8, P))
        return carry

    lax.fori_loop(0, nct // TERM_UNROLL, terms_body, 0)

    def block_diag(s2):
        z = jnp.zeros((A_DH, A_DH), F32)
        return jnp.concatenate([jnp.concatenate([s2[0], z], axis=1),
                                jnp.concatenate([z, s2[1]], axis=1)], axis=0)

    for s in range(n_sub):
        st_s[2 * s] = block_diag(s0f_ref[s])
        st_s[2 * s + 1] = block_diag(s0b_ref[s])

    def scan_body(i, carry):
        chains = [(s, d, s * nc + (i if d == 0 else nc - 1 - i)) for s in range(n_sub) for d in range(2)]
        dot = functools.partial(jnp.dot, preferred_element_type=F32)
        s_bd = [st_s[2 * s + d] for s, d, c in chains]
        sb16 = [x.astype(BF16) for x in s_bd]
        ws = [dot(w_s[d, c], sb) for (s, d, c), sb in zip(chains, sb16)]
        qs_ = [dot(qd_s[d, c], sb) for (s, d, c), sb in zip(chains, sb16)]
        vb = [(u_s[d, c].astype(F32) - x).astype(BF16) for (s, d, c), x in zip(chains, ws)]
        av = [dot(at_s[d, c], x) for (s, d, c), x in zip(chains, vb)]
        kv = [dot(kt_s[d, c], x) for (s, d, c), x in zip(chains, vb)]
        for (s, d, c), sb, q_, a_, k_ in zip(chains, s_bd, qs_, av, kv):
            st_s[2 * s + d] = sb * ge_s[d, c][0:1, :] + k_
            o_st = q_ + a_
            dst = o_ref if d == 0 else ob
            dst[pl.ds(pl.multiple_of(c * C, C), C), :] = o_st[:C] + o_st[C:]
        return carry

    lax.fori_loop(0, nc, scan_body, 0)

    for s in range(n_sub):
        for d, ref in ((0, sf_ref), (1, sb_ref)):
            s_bd = st_s[2 * s + d]
            ref[s, 0] = s_bd[:A_DH, :A_DH]
            ref[s, 1] = s_bd[A_DH:, A_DH:]

    def finish(j, carry):
        bases = [pl.multiple_of((TERM_UNROLL * j + t) * C, C) for t in range(TERM_UNROLL)]
        o = [o_ref[pl.ds(b, C), :] + ob[pl.ds(b, C), :] for b in bases]
        ms = [_mm_split(x * x, jbd, 2) * (1.0 / A_DH) for x in o]
        for b, x, m in zip(bases, o, ms):
            o_ref[pl.ds(b, C), :] = x * lax.rsqrt(m + EPS) * og_ref[...] * _silu(z_ref[pl.ds(b, C), :])
        return carry

    lax.fori_loop(0, nct // TERM_UNROLL, finish, 0)


def _deltanet(proj, conv_w, gate_p, o_gain2, s0f, s0b, tri, jbd, *, seq_len, n_seq, n_sub, row_blk0):
    rows = n_sub * seq_len
    nct = rows // CHUNK
    rb = lambda b: row_blk0 + b
    col = lambda off: (lambda b, hp: (rb(b), off + hp))
    st_spec = pl.BlockSpec((n_sub, 2, A_DH, A_DH), lambda b, hp: (b, hp, 0, 0))
    in_specs = [pl.BlockSpec((rows, LANES), col(0)),
                pl.BlockSpec((rows, LANES), col(4)),
                pl.BlockSpec((rows, LANES), col(8)),
                pl.BlockSpec((rows, LANES), col(12)),
                pl.BlockSpec((rows, LANES), lambda b, hp: (rb(b), 20)),
                pl.BlockSpec((3, LANES), lambda b, hp: (0, hp)),
                pl.BlockSpec((3, LANES), lambda b, hp: (0, 4 + hp)),
                pl.BlockSpec((3, LANES), lambda b, hp: (0, 8 + hp)),
                pl.BlockSpec((2, 2 * A_HEADS, LANES), lambda b, hp: (0, 0, 0)),
                pl.BlockSpec((1, LANES), lambda b, hp: (0, 0)),
                st_spec, st_spec,
                pl.BlockSpec((CHUNK, CHUNK), lambda b, hp: (0, 0)),
                pl.BlockSpec((LANES, LANES), lambda b, hp: (0, 0))]
    args = [proj, proj, proj, proj, proj, conv_w, conv_w, conv_w, gate_p, o_gain2, s0f, s0b, tri, jbd]
    st_shape = jax.ShapeDtypeStruct((n_seq, A_HEADS, A_DH, A_DH), F32)
    tile = lambda dt: pltpu.VMEM((2, nct, LANES, LANES), dt)
    return pl.pallas_call(
        functools.partial(_deltanet_kernel, seq_len=seq_len, n_sub=n_sub),
        grid=(n_seq // n_sub, A_HEADS // 2),
        in_specs=in_specs,
        out_specs=[pl.BlockSpec((rows, LANES), lambda b, hp: (b, hp)), st_spec, st_spec],
        out_shape=[jax.ShapeDtypeStruct((n_seq * seq_len, A_WIDTH), F32), st_shape, st_shape],
        scratch_shapes=[tile(BF16), tile(BF16), tile(BF16), tile(BF16), tile(BF16),
                        pltpu.VMEM((2, nct, 8, LANES), F32),
                        pltpu.VMEM((2 * n_sub, LANES, LANES), F32),
                        pltpu.VMEM((rows, LANES), F32)],
        compiler_params=_cparams(("arbitrary", "arbitrary")),
        name="deltanet",
    )(*args)


def _dft_mats(n):
    idx = np.arange(n)
    ang = 2.0 * np.pi * ((idx[:, None] * idx[None, :]) % n) / n
    return np.cos(ang), np.sin(ang)


def _fnet_ctx_kernel(u_ref, cs_ref, dft_ref, o_ref):
    norm = 1.0 / math.sqrt(SEQ * LANES)
    for g in range(B_GROUPS):
        sl = slice(g * LANES, (g + 1) * LANES)
        p = _mm(u_ref[:, sl], cs_ref[...])
        stack = jnp.concatenate([p[:, :LANES], p[:, LANES:]], axis=0)
        o_ref[:, sl] = _mm(dft_ref[...], stack) * norm


def _fnet_ctx(proj, cs, dft):
    return pl.pallas_call(
        _fnet_ctx_kernel,
        grid=(BATCH,),
        in_specs=[pl.BlockSpec((SEQ, B_WIDTH), lambda b: (b, 4)),
                  pl.BlockSpec(cs.shape, lambda b: (0, 0)),
                  pl.BlockSpec(dft.shape, lambda b: (0, 0))],
        out_specs=pl.BlockSpec((SEQ, B_WIDTH), lambda b: (b, 0)),
        out_shape=jax.ShapeDtypeStruct((NC, B_WIDTH), F32),
        compiler_params=_cparams(("arbitrary",)),
        name="fnet_ctx",
    )(proj, cs, dft)


FN_SUB = 4


def _fnet_lat1_kernel(u_ref, ca_ref, m1_ref, twc_ref, tws_ref, o_ref):
    r = 64
    for j in range(FN_SUB):
        c = twc_ref[j]
        s = tws_ref[j]
        rows = slice(j * r, (j + 1) * r)
        for g in range(B_GROUPS):
            pa = _mm(u_ref[rows, g * LANES:(g + 1) * LANES], ca_ref[...])
            rhs = jnp.concatenate([pa[:, :2 * LANES], pa[:, 2 * LANES:]], axis=0)
            zz = _mm(m1_ref[...], rhs)
            zr, zi = zz[:, :LANES], zz[:, LANES:]
            o_ref[rows, 2 * g * LANES:(2 * g + 1) * LANES] = (zr * c + zi * s).astype(BF16)
            o_ref[rows, (2 * g + 1) * LANES:(2 * g + 2) * LANES] = (zi * c - zr * s).astype(BF16)


def _fnet_lat2_kernel(z_ref, m1_ref, o_ref):
    r = 64
    norm = 1.0 / math.sqrt(DEC_SEQ * LANES)
    for j in range(FN_SUB):
        rows = slice(j * r, (j + 1) * r)
        for g in range(B_GROUPS):
            rhs = jnp.concatenate([z_ref[rows, 2 * g * LANES:(2 * g + 1) * LANES],
                                   z_ref[rows, (2 * g + 1) * LANES:(2 * g + 2) * LANES]], axis=0)
            o_ref[rows, g * LANES:(g + 1) * LANES] = _mm(m1_ref[...], rhs) * norm


def _fnet_latent(proj, ca, m1, twc, tws):
    r = 64
    u = proj[NC:, 2048:2560].astype(BF16).reshape(DEC_BATCH, r, r, B_WIDTH)
    u = u.transpose(0, 2, 1, 3).reshape(DEC_BATCH * r * r, B_WIDTH)
    steps = DEC_BATCH * r // FN_SUB
    blk = FN_SUB * r
    z = pl.pallas_call(
        _fnet_lat1_kernel,
        grid=(steps,),
        in_specs=[pl.BlockSpec((blk, B_WIDTH), lambda s: (s, 0)),
                  pl.BlockSpec(ca.shape, lambda s: (0, 0)),
                  pl.BlockSpec(m1.shape, lambda s: (0, 0)),
                  pl.BlockSpec((FN_SUB, r, LANES), lambda s: (s % (r // FN_SUB), 0, 0)),
                  pl.BlockSpec((FN_SUB, r, LANES), lambda s: (s % (r // FN_SUB), 0, 0))],
        out_specs=pl.BlockSpec((blk, 2 * B_WIDTH), lambda s: (s, 0)),
        out_shape=jax.ShapeDtypeStruct((NL, 2 * B_WIDTH), BF16),
        compiler_params=_cparams(("arbitrary",)),
        name="fnet_lat1",
    )(u, ca, m1, twc, tws)
    z = z.reshape(DEC_BATCH, r, r, 2 * B_WIDTH).transpose(0, 2, 1, 3).reshape(NL, 2 * B_WIDTH)
    y = pl.pallas_call(
        _fnet_lat2_kernel,
        grid=(steps,),
        in_specs=[pl.BlockSpec((blk, 2 * B_WIDTH), lambda s: (s, 0)),
                  pl.BlockSpec(m1.shape, lambda s: (0, 0))],
        out_specs=pl.BlockSpec((blk, B_WIDTH), lambda s: (s, 0)),
        out_shape=jax.ShapeDtypeStruct((NL, B_WIDTH), F32),
        compiler_params=_cparams(("arbitrary",)),
        name="fnet_lat2",
    )(z, m1)
    return y.reshape(DEC_BATCH, r, r, B_WIDTH).transpose(0, 2, 1, 3).reshape(NL, B_WIDTH)


def _head_masks():
    lane = lax.broadcasted_iota(jnp.int32, (1, LANES), 1)
    return lane < NA_DH


def _attend_many(chains):
    s = [[_mm_nt(q, k) if b is None else _mm_nt(q, k) + b for k, v, b in kv] for q, kv in chains]
    m = [functools.reduce(jnp.maximum, [jnp.max(x, axis=-1, keepdims=True) for x in xs]) for xs in s]
    p = [[jnp.exp(x - mi) for x in xs] for xs, mi in zip(s, m)]
    l = [sum(jnp.sum(x, axis=-1, keepdims=True) for x in xs) for xs in p]
    o = [sum(_mm(x, v) for x, (k, v, b) in zip(xs, kv)) for xs, (q, kv) in zip(p, chains)]
    return [oi / li for oi, li in zip(o, l)]


def _na_ctx_kernel(q_ref, k_ref, v_ref, o_ref):
    first = _head_masks()
    n_split = 2
    rows = SEQ // n_split
    chains = []
    for p in range(NA_CTX_PAIRS):
        lanes = slice(p * LANES, (p + 1) * LANES)
        k = k_ref[:, lanes].astype(BF16)
        v = v_ref[:, lanes].astype(BF16)
        for j in range(n_split):
            q = q_ref[j * rows:(j + 1) * rows, lanes]
            for a in range(2):
                chains.append((jnp.where(first if a == 0 else jnp.logical_not(first), q * NA_SCALE, 0.0),
                               [(k, v, None)]))
    outs = _attend_many(chains)
    for p in range(NA_CTX_PAIRS):
        for j in range(n_split):
            i = 2 * (p * n_split + j)
            o_ref[j * rows:(j + 1) * rows, p * LANES:(p + 1) * LANES] = jnp.where(first, outs[i], outs[i + 1])


def _na_ctx(qkv):
    w = NA_CTX_PAIRS * LANES
    nblk = D // w
    return pl.pallas_call(
        _na_ctx_kernel,
        grid=(BATCH, nblk),
        in_specs=[pl.BlockSpec((SEQ, w), lambda b, hp: (b, hp)),
                  pl.BlockSpec((SEQ, w), lambda b, hp: (b, nblk + hp)),
                  pl.BlockSpec((SEQ, w), lambda b, hp: (b, 2 * nblk + hp))],
        out_specs=pl.BlockSpec((SEQ, w), lambda b, hp: (b, hp)),
        out_shape=jax.ShapeDtypeStruct((NC, D), F32),
        compiler_params=_cparams(("arbitrary", "arbitrary")),
        name="na_ctx",
    )(qkv, qkv, qkv)


def _na_lat_kernel(q_ref, k_ref, v_ref, kc_ref, vc_ref, tt_ref, o_ref):
    first = _head_masks()
    rows = DEC_SEQ // GRID_W
    kctx = kc_ref[...].astype(BF16)
    vctx = vc_ref[...].astype(BF16)
    nkeys = WIN_R * GRID_W

    def rows_body(j, carry):
        chains = []
        for t in range(NA_ROW_UNROLL):
            r = NA_ROW_UNROLL * j + t
            r0 = jnp.clip(r - WIN_R // 2, 0, rows - WIN_R)
            dr0 = r0 - r + (WIN_R - 1)
            q = q_ref[pl.ds(pl.multiple_of(r * GRID_W, GRID_W), GRID_W), :] * NA_SCALE
            kbase = pl.multiple_of(r0 * GRID_W, GRID_W)
            kl = k_ref[pl.ds(kbase, nkeys), :].astype(BF16)
            vl = v_ref[pl.ds(kbase, nkeys), :].astype(BF16)
            for a in range(2):
                qm = jnp.where(first if a == 0 else jnp.logical_not(first), q, 0.0)
                bias = jnp.concatenate([tt_ref[a, dr0 + 2 * i] for i in range(WIN_R // 2)], axis=1)
                chains.append((qm, [(kl, vl, bias), (kctx, vctx, None)]))
        outs = _attend_many(chains)
        for t in range(NA_ROW_UNROLL):
            r = NA_ROW_UNROLL * j + t
            o_ref[pl.ds(pl.multiple_of(r * GRID_W, GRID_W), GRID_W), :] = jnp.where(first, outs[2 * t], outs[2 * t + 1])
        return carry

    lax.fori_loop(0, rows // NA_ROW_UNROLL, rows_body, 0)


def _na_latent(qkv, cache_k2, cache_v2, tt2):
    rb0 = NC // DEC_SEQ
    return pl.pallas_call(
        _na_lat_kernel,
        grid=(DEC_BATCH, NA_HEADS // 2),
        in_specs=[pl.BlockSpec((DEC_SEQ, LANES), lambda b, hp: (rb0 + b, hp)),
                  pl.BlockSpec((DEC_SEQ, LANES), lambda b, hp: (rb0 + b, 8 + hp)),
                  pl.BlockSpec((DEC_SEQ, LANES), lambda b, hp: (rb0 + b, 16 + hp)),
                  pl.BlockSpec((None, 256, LANES), lambda b, hp: (b, 0, hp)),
                  pl.BlockSpec((None, 256, LANES), lambda b, hp: (b, 0, hp)),
                  pl.BlockSpec((2, 2 * WIN_R - 2, GRID_W, LANES), lambda b, hp: (hp, 0, 0, 0))],
        out_specs=pl.BlockSpec((DEC_SEQ, LANES), lambda b, hp: (b, hp)),
        out_shape=jax.ShapeDtypeStruct((NL, D), F32),
        compiler_params=_cparams(("arbitrary", "arbitrary")),
        name="na_latent",
    )(qkv, qkv, qkv, cache_k2, cache_v2, tt2)


def _rpb_tables(rpb):
    col = np.arange(GRID_W)
    start = np.clip(col - WIN_C // 2, 0, GRID_W - WIN_C)
    inside = (col[None, :] >= start[:, None]) & (col[None, :] < start[:, None] + WIN_C)
    w = GRID_W
    period = 2 * w - 1
    x = jnp.pad(rpb, ((0, 0), (0, 0), (w - WIN_C, w - WIN_C)))
    flat = jnp.tile(x, (1, 1, w))[:, :, w - 1:w - 1 + w * (period - 1)]
    t = flat.reshape(NA_HEADS, 2 * WIN_R - 1, w, period - 1)[..., :w]
    t = jnp.where(inside[None, None], t, NEG)
    return jnp.concatenate([t[:, :-1], t[:, 1:]], axis=-1)


ROUTE_ROWS = 40


def _router_kernel(*refs, n_in):
    a_refs = refs[:2 * n_in]
    w_refs = refs[2 * n_in:3 * n_in]
    x_refs = refs[3 * n_in:-11]
    m_ref, g_ref, wr_ref, br_ref, tri_ref, xn_ref, xf_ref, ri_ref, rw_ref, cnt_ref, base_scr = refs[-11:]
    i = pl.program_id(0)

    @pl.when(i == 0)
    def _():
        base_scr[...] = jnp.zeros_like(base_scr)
        cnt_ref[...] = jnp.zeros_like(cnt_ref)

    acc = None
    for j, w_ref in enumerate(w_refs):
        part = _mm(_token_rows(a_refs[2 * j:2 * j + 2]), w_ref[...])
        acc = part if acc is None else acc + part
    x_new = _token_rows(x_refs) + m_ref[2:3, :] * acc
    xn_ref[...] = x_new
    h = _modulated_norm(x_new, m_ref, g_ref, 3, 4)
    xf_ref[...] = _pack_bf16_pairs(h[:, :D // 2], h[:, D // 2:])
    logits = lax.dot_general(wr_ref[...], h, (((1,), (1,)), ((), ())), preferred_element_type=F32,
                             precision=HIGHEST) + br_ref[:, 0:1]
    row = lax.broadcasted_iota(jnp.int32, logits.shape, 0)
    cmax = lambda x: jnp.max(x, axis=0, keepdims=True)
    cmin = lambda x: jnp.min(x, axis=0, keepdims=True)
    csum = lambda x: jnp.sum(x, axis=0, keepdims=True)

    gmask = row < N_GROUPS
    mg = cmax(jnp.where(gmask, logits, NEG))
    eg = jnp.where(gmask, jnp.exp(jnp.where(gmask, logits - mg, NEG)), 0.0)
    pg = eg / csum(eg)
    p_grp = cmax(pg)
    grp = cmin(jnp.where(jnp.logical_and(gmask, pg == p_grp), row, ROUTE_ROWS))
    lo = N_GROUPS + grp * EXP_PER_GROUP
    emask = jnp.logical_and(row >= lo, row < lo + EXP_PER_GROUP)
    me = cmax(jnp.where(emask, logits, NEG))
    ee = jnp.where(emask, jnp.exp(jnp.where(emask, logits - me, NEG)), 0.0)
    pe = ee / csum(ee)
    p1 = cmax(pe)
    i1 = cmin(jnp.where(jnp.logical_and(emask, pe == p1), row, ROUTE_ROWS))
    m2 = jnp.logical_and(emask, row != i1)
    p2 = cmax(jnp.where(m2, pe, -1.0))
    i2 = cmin(jnp.where(jnp.logical_and(m2, pe == p2), row, ROUTE_ROWS))
    den = p1 + p2
    w1 = p_grp * p1 / den
    w2 = p_grp * p2 / den

    sel1 = row == i1
    sel2 = row == i2
    oh = jnp.where(jnp.logical_or(sel1, sel2), 1.0, 0.0).astype(BF16)
    before = jnp.dot(oh, tri_ref[...], preferred_element_type=F32) + base_scr[:, 0:1]
    rank1 = csum(jnp.where(sel1, before, 0.0))
    rank2 = csum(jnp.where(sel2, before, 0.0))
    base_scr[...] = base_scr[...] + jnp.sum(oh.astype(F32), axis=1, keepdims=True)
    cnt_ref[...] = cnt_ref[...] + lax.dot_general(jnp.ones((8, TM), BF16), oh, (((1,), (1,)), ((), ())),
                                                  preferred_element_type=F32)
    sub = lax.broadcasted_iota(jnp.int32, (8, TM), 0)
    ri_ref[...] = jnp.where(sub == 0, i1 - N_GROUPS, jnp.where(sub == 1, i2 - N_GROUPS,
                  jnp.where(sub == 2, rank1.astype(jnp.int32), jnp.where(sub == 3, rank2.astype(jnp.int32), 0))))
    rw_ref[...] = jnp.where(sub == 0, w1, jnp.where(sub == 1, w2, 0.0))


def _router(a_pairs, w_list, x, m3, gain, wr_t, br_t, tri_upper):
    in_specs, args = [], []
    for pair in a_pairs:
        specs, ops = _token_specs(pair)
        in_specs += specs
        args += ops
    x_specs, x_args = _token_specs(x)
    in_specs += ([pl.BlockSpec(w.shape, lambda i: (0, 0)) for w in w_list] + x_specs
                 + [pl.BlockSpec((None, 6, D), lambda i: (_cond_row(i), 0, 0)),
                    pl.BlockSpec((1, D), lambda i: (0, 0)),
                    pl.BlockSpec((ROUTE_ROWS, D), lambda i: (0, 0)),
                    pl.BlockSpec((ROUTE_ROWS, LANES), lambda i: (0, 0)),
                    pl.BlockSpec((TM, TM), lambda i: (0, 0))])
    return pl.pallas_call(
        functools.partial(_router_kernel, n_in=len(a_pairs)),
        grid=(N // TM,),
        in_specs=in_specs,
        out_specs=[pl.BlockSpec((TM, D), lambda i: (i, 0)),
                   pl.BlockSpec((TM, D // 2), lambda i: (i, 0)),
                   pl.BlockSpec((8, TM), lambda i: (0, i)),
                   pl.BlockSpec((8, TM), lambda i: (0, i)),
                   pl.BlockSpec((8, ROUTE_ROWS), lambda i: (0, 0))],
        out_shape=[jax.ShapeDtypeStruct((N, D), F32),
                   jax.ShapeDtypeStruct((N, D // 2), jnp.int32),
                   jax.ShapeDtypeStruct((8, N), jnp.int32),
                   jax.ShapeDtypeStruct((8, N), F32),
                   jax.ShapeDtypeStruct((8, ROUTE_ROWS), F32)],
        scratch_shapes=[pltpu.VMEM((ROUTE_ROWS, LANES), F32)],
        compiler_params=_cparams(("arbitrary",)),
        name="router",
    )(*args, *w_list, *x_args, m3, gain, wr_t, br_t, tri_upper)


def _row_copy(src_hbm, row, dst, r, sem):
    return pltpu.make_async_copy(src_hbm.at[pl.ds(row, 1)], dst.at[pl.ds(r, 1)], sem)


def _expert_kernel(be_ref, nu_ref, d_ref, pad_ref, xp_hbm, wg_ref, wu_ref, wd_ref, ys_ref, xres, xbuf, st_ref, sem):
    del be_ref
    b = pl.program_id(0)
    half = D // 2

    def gather(blk, slot, part=0, parts=1):
        base = blk * MOE_ROWS
        for r in range(part * MOE_ROWS // parts, (part + 1) * MOE_ROWS // parts):
            xbuf[slot, pl.ds(r, 1), :] = xres[pl.ds(st_ref[base + r], 1), :]

    @pl.when(b == 0)
    def _():
        cp = pltpu.make_async_copy(xp_hbm, xres, sem.at[0])
        cp.start()

        def clear_tail(e, c):
            def clear(s_, c2):
                st_ref[s_] = 0
                return c2

            return lax.fori_loop(pad_ref[2 * e], pad_ref[2 * e + 1], clear, c)

        lax.fori_loop(0, N_EXPERTS + 1, clear_tail, 0)

        def invert(j, c):
            slots = [d_ref[16 * j + t] for t in range(16)]
            for t in range(16):
                st_ref[slots[t]] = 8 * j + t // 2
            return c

        lax.fori_loop(0, 2 * N // 16, invert, 0)
        cp.wait()
        gather(0, 0)

    @pl.when(b < nu_ref[0])
    def _():
        nxt = (jnp.minimum(b + 1, N_SLOT_BLOCKS - 1), (b + 1) % 2)
        x_lo, x_hi = _unpack_bf16_pairs(xbuf[b % 2])
        dot = functools.partial(jnp.dot, preferred_element_type=F32)
        gather(*nxt, 0, 8)
        g = dot(x_lo, wg_ref[:half, :].astype(BF16))
        gather(*nxt, 1, 8)
        g = g + dot(x_hi, wg_ref[half:, :].astype(BF16))
        gather(*nxt, 2, 8)
        u = dot(x_lo, wu_ref[:half, :].astype(BF16))
        gather(*nxt, 3, 8)
        u = u + dot(x_hi, wu_ref[half:, :].astype(BF16))
        hb = (_silu(g) * u).astype(BF16)
        quarter = D // 4
        for j in range(4):
            gather(*nxt, 4 + j, 8)
            cols = slice(j * quarter, (j + 1) * quarter)
            ys_ref[:, cols] = jnp.dot(hb, wd_ref[:, cols].astype(BF16), preferred_element_type=F32)

    @pl.when(b >= nu_ref[0])
    def _():
        ys_ref[...] = jnp.zeros_like(ys_ref)


def _experts(block_e, n_used, dest, pad_ranges, xp, w_gate, w_up, w_down, layer):
    grid_spec = pltpu.PrefetchScalarGridSpec(
        num_scalar_prefetch=4,
        grid=(N_SLOT_BLOCKS,),
        in_specs=[pl.BlockSpec(memory_space=pl.ANY),
                  pl.BlockSpec((None, None, D, D_EXPERT), lambda b, be, nu, st, pd: (layer, be[b], 0, 0)),
                  pl.BlockSpec((None, None, D, D_EXPERT), lambda b, be, nu, st, pd: (layer, be[b], 0, 0)),
                  pl.BlockSpec((None, None, D_EXPERT, D), lambda b, be, nu, st, pd: (layer, be[b], 0, 0))],
        out_specs=pl.BlockSpec((MOE_ROWS, D), lambda b, be, nu, st, pd: (b, 0)),
        scratch_shapes=[pltpu.VMEM((N, D // 2), jnp.int32),
                        pltpu.VMEM((2, MOE_ROWS, D // 2), jnp.int32),
                        pltpu.SMEM((N_SLOT_BLOCKS * MOE_ROWS,), jnp.int32),
                        pltpu.SemaphoreType.DMA((1,))])
    return pl.pallas_call(
        _expert_kernel,
        grid_spec=grid_spec,
        out_shape=jax.ShapeDtypeStruct((N_SLOT_BLOCKS * MOE_ROWS, D), F32),
        compiler_params=pltpu.CompilerParams(dimension_semantics=("arbitrary",), vmem_limit_bytes=EXPERT_VMEM_LIMIT),
        name="experts",
    )(block_e, n_used, dest, pad_ranges, xp, w_gate, w_up, w_down)


def _combine_kernel(d_ref, ys_hbm, x_ref, m_ref, rw_ref, fn_ref, o_ref, buf, sem, *, final, tile0, n_tiles):
    i = pl.program_id(0)

    def issue(tile, slot):
        base = (tile0 + tile) * (2 * TM)
        for r in range(TM):
            for kk in range(2):
                _row_copy(ys_hbm, d_ref[base + 2 * r + kk], buf.at[slot, kk], r, sem.at[slot]).start()

    @pl.when(i == 0)
    def _():
        issue(0, 0)

    @pl.when(i + 1 < n_tiles)
    def _():
        issue(i + 1, (i + 1) % 2)

    slot = i % 2
    for kk in range(2):
        pltpu.make_async_copy(ys_hbm.at[pl.ds(0, TM)], buf.at[slot, kk], sem.at[slot]).wait()
    w = rw_ref[...]
    y = w[:, 0:1] * buf[slot, 0] + w[:, 1:2] * buf[slot, 1]
    out = x_ref[...] + m_ref[5:6, :] * y
    if final:
        ms = jnp.mean(out * out, axis=-1, keepdims=True)
        out = out * lax.rsqrt(ms + EPS) * fn_ref[...]
    o_ref[...] = out


def _combine(dest_flat, ys, x, m3, rw, final_norm, final, tile0=0, n_tiles=N // TM):
    grid_spec = pltpu.PrefetchScalarGridSpec(
        num_scalar_prefetch=1,
        grid=(n_tiles,),
        in_specs=[pl.BlockSpec(memory_space=pl.ANY),
                  pl.BlockSpec((TM, D), lambda i, d: (tile0 + i, 0)),
                  pl.BlockSpec((None, 6, D), lambda i, d: (_cond_row(tile0 + i), 0, 0)),
                  pl.BlockSpec((TM, 2), lambda i, d: (tile0 + i, 0)),
                  pl.BlockSpec((1, D), lambda i, d: (0, 0))],
        out_specs=pl.BlockSpec((TM, D), lambda i, d: (i, 0)),
        scratch_shapes=[pltpu.VMEM((2, 2, TM, D), F32), pltpu.SemaphoreType.DMA((2,))])
    return pl.pallas_call(
        functools.partial(_combine_kernel, final=final, tile0=tile0, n_tiles=n_tiles),
        grid_spec=grid_spec,
        out_shape=jax.ShapeDtypeStruct((n_tiles * TM, D), F32),
        compiler_params=_cparams(("arbitrary",)),
        name="combine",
    )(dest_flat, ys, x, m3, rw, final_norm)


def _combine_proj_kernel(d_ref, ys_hbm, x_ref, m_ref, rw_ref, mn_ref, gn_ref, w_ref, xo_ref, o_ref, buf, sem, *,
                         n_chunk):
    i = pl.program_id(0)
    n_tiles = N // TM
    n_parts = o_ref.shape[1] // n_chunk

    def issue(tile, slot, part=0, parts=1):
        base = tile * (2 * TM)
        for r in range(part * TM // parts, (part + 1) * TM // parts):
            for kk in range(2):
                _row_copy(ys_hbm, d_ref[base + 2 * r + kk], buf.at[slot, kk], r, sem.at[slot]).start()

    def wait(slot):
        for kk in range(2):
            pltpu.make_async_copy(ys_hbm.at[pl.ds(0, TM)], buf.at[slot, kk], sem.at[slot]).wait()

    @pl.when(i == 0)
    def _():
        issue(0, 0)

    slot = i % 2
    wait(slot)
    w = rw_ref[...]
    x_new = x_ref[...] + m_ref[5:6, :] * (w[:, 0:1] * buf[slot, 0] + w[:, 1:2] * buf[slot, 1])
    xo_ref[...] = x_new
    hb = _modulated_norm(x_new, mn_ref, gn_ref, 0, 1).astype(BF16)
    nxt = jnp.minimum(i + 1, n_tiles - 1)
    for j in range(n_parts):
        issue(nxt, 1 - slot, j, n_parts)
        sl = slice(j * n_chunk, (j + 1) * n_chunk)
        o_ref[:, sl] = jnp.dot(hb, w_ref[:, sl], preferred_element_type=F32)

    @pl.when(i == n_tiles - 1)
    def _():
        wait(1 - slot)


def _combine_proj(dest_flat, ys, x, m3, rw, m3_next, gain_next, w_bf16, n_chunk):
    nout = w_bf16.shape[1]
    grid_spec = pltpu.PrefetchScalarGridSpec(
        num_scalar_prefetch=1,
        grid=(N // TM,),
        in_specs=[pl.BlockSpec(memory_space=pl.ANY),
                  pl.BlockSpec((TM, D), lambda i, d: (i, 0)),
                  pl.BlockSpec((None, 6, D), lambda i, d: (_cond_row(i), 0, 0)),
                  pl.BlockSpec((TM, 2), lambda i, d: (i, 0)),
                  pl.BlockSpec((None, 6, D), lambda i, d: (_cond_row(i), 0, 0)),
                  pl.BlockSpec((1, D), lambda i, d: (0, 0)),
                  pl.BlockSpec((D, nout), lambda i, d: (0, 0))],
        out_specs=[pl.BlockSpec((TM, D), lambda i, d: (i, 0)),
                   pl.BlockSpec((TM, nout), lambda i, d: (i, 0))],
        scratch_shapes=[pltpu.VMEM((2, 2, TM, D), F32), pltpu.SemaphoreType.DMA((2,))])
    return pl.pallas_call(
        functools.partial(_combine_proj_kernel, n_chunk=n_chunk),
        grid_spec=grid_spec,
        out_shape=[jax.ShapeDtypeStruct((N, D), F32), jax.ShapeDtypeStruct((N, nout), F32)],
        compiler_params=_cparams(("arbitrary",)),
        name="combine_proj",
    )(dest_flat, ys, x, m3, rw, m3_next, gain_next, w_bf16)


def _hier_moe(a_pairs, w_list, x, m3, gain, layer, w_rg, b_rg, w_re, b_re, w_gate, w_up, w_down, tri_tm,
              final_norm, final, next_proj=None):
    pad = ROUTE_ROWS - N_GROUPS - N_EXPERTS
    wr_t = jnp.concatenate([w_rg.T, w_re.transpose(0, 2, 1).reshape(N_EXPERTS, D), jnp.zeros((pad, D), F32)], axis=0)
    br_t = jnp.broadcast_to(jnp.concatenate([b_rg, b_re.reshape(N_EXPERTS), jnp.zeros((pad,), F32)])[:, None],
                            (ROUTE_ROWS, LANES))
    x, xf, ri_t, rw_t, cnt = _router(a_pairs, w_list, x, m3, gain, wr_t, br_t, tri_tm)
    e_idx = ri_t[0:2].T
    rank = ri_t[2:4].T
    rw = rw_t[0:2].T
    counts = cnt[0, N_GROUPS:N_GROUPS + N_EXPERTS].astype(jnp.int32)
    padded = (counts + MOE_ROWS - 1) // MOE_ROWS * MOE_ROWS
    end_pad = jnp.cumsum(padded)
    start_pad = end_pad - padded
    experts = jnp.arange(N_EXPERTS, dtype=jnp.int32)
    start_of = jnp.sum(jnp.where(e_idx[:, :, None] == experts, start_pad, 0), axis=-1)
    dest = (start_of + rank).reshape(-1).astype(jnp.int32)
    block_start = jnp.arange(N_SLOT_BLOCKS, dtype=jnp.int32) * MOE_ROWS
    block_e = jnp.minimum(jnp.sum((end_pad[None, :] <= block_start[:, None]).astype(jnp.int32), axis=1),
                          N_EXPERTS - 1)
    n_used = (end_pad[-1:] // MOE_ROWS).astype(jnp.int32)
    tail = jnp.stack([end_pad[-1], jnp.minimum(end_pad[-1] + MOE_ROWS, N_SLOT_BLOCKS * MOE_ROWS)])
    pad_ranges = jnp.concatenate([jnp.stack([start_pad + counts, end_pad], axis=1).reshape(-1), tail]).astype(jnp.int32)
    ys = _experts(block_e, n_used, dest, pad_ranges, xf, w_gate, w_up, w_down, layer)
    if not final:
        return _combine_proj(dest, ys, x, m3, rw, *next_proj)
    nct = NC // TM
    return (_combine(dest, ys, x, m3, rw, final_norm, True, 0, nct),
            _combine(dest, ys, x, m3, rw, final_norm, True, nct, N // TM - nct))


def kernel(x_prompt, x_sample, state_A_fwd, state_A_bwd, cache_k, cache_v, c, c_ctx, mod_w, mod_b, norm_mix, norm_ffn, ab_w_in, ab_conv, ab_a_log, ab_dt_bias, ab_o_gain, ab_w_out, na_w_qkv, na_rpb, na_w_out, moe_w_rg, moe_b_rg, moe_w_re, moe_b_re, moe_w_gate, moe_w_up, moe_w_down, final_norm):
    x = (x_prompt.reshape(NC, D), x_sample.reshape(NL, D))
    cond8 = jnp.concatenate([c_ctx[None, :], c, jnp.zeros((8 - 1 - DEC_BATCH, D), F32)], axis=0)
    mods = _ada_params(cond8, mod_w, mod_b).reshape(DEPTH, 8, 6, D)

    tri_tm = jnp.asarray(np.triu(np.ones((TM, TM)), 1), BF16)
    tri_c = jnp.asarray(np.tril(np.ones((CHUNK, CHUNK))), F32)
    half = np.arange(LANES) < A_DH
    jbd = jnp.asarray((half[:, None] == half[None, :]).astype(np.float32))
    fn = final_norm[None, :]

    m3 = mods[0]
    w_in = ab_w_in[0]
    w_in = jnp.concatenate([w_in[:, :2048], w_in[:, 2080:2592], w_in[:, 2048:2080],
                            jnp.zeros((D, AB_COLS - 2592), F32)], axis=1).astype(BF16)
    proj = _modproj(x, m3, norm_mix[0][None, :], w_in, 896)
    gate_p = jnp.broadcast_to(jnp.stack([ab_a_log[0].reshape(-1), ab_dt_bias[0].reshape(-1)])[:, :, None],
                              (2, 2 * A_HEADS, LANES))
    o_gain2 = jnp.tile(ab_o_gain[0], 2)[None, :]
    zeros_state = jnp.zeros((BATCH, A_HEADS, A_DH, A_DH), F32)
    mix_a_c, s_f, s_b = _deltanet(proj, ab_conv[0], gate_p, o_gain2, zeros_state, zeros_state, tri_c, jbd,
                                  seq_len=SEQ, n_seq=BATCH, n_sub=4, row_blk0=0)
    mix_a_l, _, _ = _deltanet(proj, ab_conv[0], gate_p, o_gain2, state_A_fwd[:, 0], state_A_bwd[:, 0], tri_c, jbd,
                              seq_len=DEC_SEQ, n_seq=DEC_BATCH, n_sub=1, row_blk0=NC // DEC_SEQ)

    cc, sc = _dft_mats(LANES)
    ct, st = _dft_mats(SEQ)
    c64, s64 = _dft_mats(64)
    cs = jnp.asarray(np.concatenate([cc, sc], axis=1), BF16)
    dft = jnp.asarray(np.concatenate([ct, -st], axis=1), BF16)
    ca = jnp.asarray(np.concatenate([cc, -sc, -sc, -cc], axis=1), BF16)
    m1 = jnp.asarray(np.concatenate([c64, s64], axis=1), BF16)
    tw_idx = np.arange(64)
    tw_ang = 2.0 * np.pi * (tw_idx[:, None] * tw_idx[None, :]) / DEC_SEQ
    twc = jnp.broadcast_to(jnp.asarray(np.cos(tw_ang), F32)[:, :, None], (64, 64, LANES))
    tws = jnp.broadcast_to(jnp.asarray(np.sin(tw_ang), F32)[:, :, None], (64, 64, LANES))
    mix_b_c = _fnet_ctx(proj, cs, dft)
    mix_b_l = _fnet_latent(proj, ca, m1, twc, tws)

    w_out = ab_w_out[0].astype(BF16)
    x, qkv = _hier_moe([(mix_a_c, mix_a_l), (mix_b_c, mix_b_l)], [w_out[:A_WIDTH], w_out[A_WIDTH:]], x, m3,
                       norm_ffn[0][None, :], 0, moe_w_rg[0], moe_b_rg[0], moe_w_re[0], moe_b_re[0],
                       moe_w_gate, moe_w_up, moe_w_down, tri_tm, fn, False,
                       next_proj=(mods[1], norm_mix[1][None, :], na_w_qkv[0].astype(BF16), 512))

    m3 = mods[1]
    attn_c = _na_ctx(qkv)
    attn_l = _na_latent(qkv, cache_k[:, 0].reshape(DEC_BATCH, 256, D), cache_v[:, 0].reshape(DEC_BATCH, 256, D),
                        _rpb_tables(na_rpb[0]))
    y_c, y_l = _hier_moe([(attn_c, attn_l)], [na_w_out[0].astype(BF16)], x, m3, norm_ffn[1][None, :], 1, moe_w_rg[1], moe_b_rg[1], moe_w_re[1], moe_b_re[1],
                         moe_w_gate, moe_w_up, moe_w_down, tri_tm, fn, True)

    new_k = qkv[:NC, D:2 * D].reshape(BATCH, 1, SEQ, NA_HEADS, NA_DH)
    new_v = qkv[:NC, 2 * D:].reshape(BATCH, 1, SEQ, NA_HEADS, NA_DH)
    return (y_c.reshape(BATCH, SEQ, D), y_l.reshape(DEC_BATCH, DEC_SEQ, D),
            s_f[:, None], s_b[:, None], new_k, new_v)
```

```python
import functools
import math

import numpy as np
import jax
import jax.numpy as jnp
from jax import lax
from jax.experimental import pallas as pl
from jax.experimental.pallas import tpu as pltpu

F32 = jnp.float32
BF16 = jnp.bfloat16
HIGHEST = lax.Precision.HIGHEST

D = 1024
BATCH, SEQ = 32, 256
DEC_BATCH, DEC_SEQ = 2, 4096
NC = BATCH * SEQ
NL = DEC_BATCH * DEC_SEQ
N = NC + NL
DEPTH = 2
GRID_W = 64
A_DH = 64
A_HEADS = 8
A_WIDTH = 512
CHUNK = 64
B_WIDTH = 512
B_GROUPS = 4
NA_DH = 64
NA_HEADS = 16
WIN_R, WIN_C = 8, 16
N_GROUPS, EXP_PER_GROUP, N_EXPERTS = 4, 8, 32
D_EXPERT = 512
EPS = 1e-6

LANES = 128
TM = 256
MOE_ROWS = 256
N_SLOT_BLOCKS = (2 * N) // MOE_ROWS + N_EXPERTS
AB_COLS = 2688
VMEM_LIMIT = 56 * 1024 * 1024
EXPERT_VMEM_LIMIT = 60 * 1024 * 1024
NEG = -1e30


def _cparams(sem):
    return pltpu.CompilerParams(dimension_semantics=sem, vmem_limit_bytes=VMEM_LIMIT)


def _mm(a, b):
    return jnp.dot(a.astype(BF16), b.astype(BF16), preferred_element_type=F32)


def _mm_nt(a, b):
    return lax.dot_general(a.astype(BF16), b.astype(BF16), (((1,), (1,)), ((), ())),
                           preferred_element_type=F32)


SOLVE_BLK = 16
NA_SCALE = NA_DH ** -0.5
NA_CTX_PAIRS = 2
NA_ROW_UNROLL = 8
TERM_UNROLL = 4


def _unit_lower_solve(a_mat, rhs, same_blk, eye):
    dg = jnp.where(same_blk, a_mat, 0.0)
    p = -dg
    dinv = eye + p
    for _ in range(int(math.log2(SOLVE_BLK)) - 1):
        p = _mm(p, p)
        dinv = dinv + _mm(p, dinv)
    mp = -_mm(dinv, a_mat - dg)
    y = _mm(dinv, rhs)
    y = y + _mm(mp, y)
    for _ in range(int(math.log2(a_mat.shape[0] // SOLVE_BLK)) - 1):
        mp = _mm(mp, mp)
        y = y + _mm(mp, y)
    return y


def _unit_lower_solve_many(a_mats, rhs, same_blk, eye):
    off = [jnp.where(same_blk, 0.0, a).astype(BF16) for a in a_mats]
    p = [jnp.where(same_blk, -a, 0.0).astype(BF16) for a in a_mats]
    dinv = [(eye + x.astype(F32)).astype(BF16) for x in p]
    for _ in range(int(math.log2(SOLVE_BLK)) - 1):
        p = [_mm(x, x).astype(BF16) for x in p]
        dinv = [(di.astype(F32) + _mm(x, di)).astype(BF16) for x, di in zip(p, dinv)]
    mp = [(-_mm(di, o)).astype(BF16) for di, o in zip(dinv, off)]
    y = [_mm(di, r) for di, r in zip(dinv, rhs)]
    y = [yi + _mm(m, yi) for m, yi in zip(mp, y)]
    for _ in range(int(math.log2(a_mats[0].shape[0] // SOLVE_BLK)) - 1):
        mp = [_mm(m, m).astype(BF16) for m in mp]
        y = [yi + _mm(m, yi) for m, yi in zip(mp, y)]
    return y


def _mm_split(a, b, parts, split_rhs=False):
    x = b if split_rhs else a
    acc = None
    for _ in range(parts):
        piece = x.astype(BF16)
        x = x - piece.astype(F32)
        term = (jnp.dot(a.astype(BF16), piece, preferred_element_type=F32) if split_rhs
                else jnp.dot(piece, b.astype(BF16), preferred_element_type=F32))
        acc = term if acc is None else acc + term
    return acc


def _mm_hi(a, b):
    return jnp.dot(a, b, preferred_element_type=F32, precision=HIGHEST)


def _silu(x):
    return x * jax.nn.sigmoid(x)


def _bf16_bits(x):
    b = lax.bitcast_convert_type(x, jnp.int32)
    return b + 0x7FFF + (lax.shift_right_logical(b, jnp.int32(16)) & 1)


_HIGH16 = -65536


def _pack_bf16_pairs(a, b):
    return lax.shift_right_logical(_bf16_bits(a), jnp.int32(16)) | (_bf16_bits(b) & _HIGH16)


def _unpack_bf16_pairs(p):
    a = lax.bitcast_convert_type(lax.shift_left(p, jnp.int32(16)), F32)
    b = lax.bitcast_convert_type(p & _HIGH16, F32)
    return a.astype(BF16), b.astype(BF16)


def _cond_row(i):
    return jnp.where(i < NC // TM, 0, 1 + (i - NC // TM) // (DEC_SEQ // TM))


def _modulated_norm(x, m_ref, g_ref, shift_idx, scale_idx):
    ms = jnp.mean(x * x, axis=-1, keepdims=True)
    y = x * lax.rsqrt(ms + EPS) * g_ref[...]
    return y * (1.0 + m_ref[scale_idx:scale_idx + 1, :]) + m_ref[shift_idx:shift_idx + 1, :]


def _ada_kernel(cond_ref, w_ref, b_ref, o_ref):
    o_ref[...] = _mm_hi(_silu(cond_ref[...]), w_ref[...]) + b_ref[...]


def _ada_params(cond8, mod_w, mod_b):
    tn = 1536
    return pl.pallas_call(
        _ada_kernel,
        grid=(DEPTH, 6 * D // tn),
        in_specs=[pl.BlockSpec((8, D), lambda l, j: (0, 0)),
                  pl.BlockSpec((None, D, tn), lambda l, j: (l, 0, j)),
                  pl.BlockSpec((None, 1, tn), lambda l, j: (l, 0, j))],
        out_specs=pl.BlockSpec((None, 8, tn), lambda l, j: (l, 0, j)),
        out_shape=jax.ShapeDtypeStruct((DEPTH, 8, 6 * D), F32),
        compiler_params=_cparams(("arbitrary", "arbitrary")),
        name="ada_params",
    )(cond8, mod_w, mod_b.reshape(DEPTH, 1, 6 * D))


def _token_specs(x):
    nct = NC // TM
    if isinstance(x, tuple):
        return ([pl.BlockSpec((TM, x[0].shape[1]), lambda i: (jnp.minimum(i, nct - 1), 0)),
                 pl.BlockSpec((TM, x[1].shape[1]), lambda i: (jnp.maximum(i - nct, 0), 0))], list(x))
    return [pl.BlockSpec((TM, x.shape[1]), lambda i: (i, 0))], [x]


def _token_rows(refs):
    if len(refs) == 1:
        return refs[0][...]
    return jnp.where(pl.program_id(0) < NC // TM, refs[0][...], refs[1][...])


def _modproj_kernel(*refs, n_chunk):
    m_ref, g_ref, w_ref, o_ref = refs[-4:]
    hb = _modulated_norm(_token_rows(refs[:-4]), m_ref, g_ref, 0, 1).astype(BF16)
    for j in range(o_ref.shape[1] // n_chunk):
        sl = slice(j * n_chunk, (j + 1) * n_chunk)
        o_ref[:, sl] = jnp.dot(hb, w_ref[:, sl], preferred_element_type=F32)


def _modproj(x, m3, gain, w_bf16, n_chunk):
    nout = w_bf16.shape[1]
    x_specs, x_args = _token_specs(x)
    return pl.pallas_call(
        functools.partial(_modproj_kernel, n_chunk=n_chunk),
        grid=(N // TM,),
        in_specs=x_specs + [pl.BlockSpec((None, 6, D), lambda i: (_cond_row(i), 0, 0)),
                            pl.BlockSpec((1, D), lambda i: (0, 0)),
                            pl.BlockSpec((D, nout), lambda i: (0, 0))],
        out_specs=pl.BlockSpec((TM, nout), lambda i: (i, 0)),
        out_shape=jax.ShapeDtypeStruct((N, nout), F32),
        compiler_params=_cparams(("arbitrary",)),
        name="modproj",
    )(*x_args, m3, gain, w_bf16)


def _deltanet_kernel(q_ref, k_ref, v_ref, z_ref, ab_ref, cq_ref, ck_ref, cv_ref, gp_ref, og_ref,
                     s0f_ref, s0b_ref, tri_ref, jbd_ref, o_ref, sf_ref, sb_ref,
                     u_s, w_s, qd_s, at_s, kt_s, ge_s, st_s, ob, *, seq_len, n_sub):
    hp = pl.program_id(1)
    C = CHUNK
    nc = seq_len // C
    nct = n_sub * nc
    P = LANES
    lane = lax.broadcasted_iota(jnp.int32, (C, P), 1)
    row = lax.broadcasted_iota(jnp.int32, (C, P), 0)
    first_head = lane < A_DH
    ri = lax.broadcasted_iota(jnp.int32, (P, P), 0)
    ci = lax.broadcasted_iota(jnp.int32, (P, P), 1)
    same_head = (ri < C) == (ci < C)
    same_blk = (ri // SOLVE_BLK) == (ci // SOLVE_BLK)
    eye = jnp.where(ri == ci, 1.0, 0.0)
    jbd = jbd_ref[...]
    lincl = tri_ref[...]
    cum_b = [lincl.T.astype(BF16), lincl.astype(BF16)]
    incl_m = [jnp.logical_and(same_head, ri >= ci), jnp.logical_and(same_head, ri <= ci)]
    strict_m = [jnp.logical_and(same_head, ri > ci), jnp.logical_and(same_head, ri < ci)]
    neg_a = -jnp.exp(gp_ref[0])
    dt_b = gp_ref[1]

    def conv_silu(ref, w_ref, c):
        base = pl.multiple_of(c * C, C)
        cs = c % nc
        xc = ref[pl.ds(base, C), :]
        pbase = pl.multiple_of(jnp.maximum(base - 8, 0), 8)
        nbase = pl.multiple_of(jnp.minimum(base + C, n_sub * seq_len - 8), 8)
        prev_row = ref[pl.ds(pbase, 8), :][7:8, :] * jnp.where(cs > 0, 1.0, 0.0)
        next_row = ref[pl.ds(nbase, 8), :][0:1, :] * jnp.where(cs < nc - 1, 1.0, 0.0)
        x_prev = jnp.where(row == 0, prev_row, pltpu.roll(xc, 1, 0))
        x_next = jnp.where(row == C - 1, next_row, pltpu.roll(xc, C - 1, 0))
        y = w_ref[0:1, :] * x_prev + w_ref[1:2, :] * xc + w_ref[2:3, :] * x_next
        return _silu(y)

    def stack(x):
        return jnp.concatenate([jnp.where(first_head, x, 0.0), jnp.where(first_head, 0.0, x)], axis=0)

    def chunk_inputs(c):
        base = pl.multiple_of(c * C, C)
        q = conv_silu(q_ref, cq_ref, c)
        k = conv_silu(k_ref, ck_ref, c)
        v = conv_silu(v_ref, cv_ref, c)
        return q, k, v, ab_ref[pl.ds(base, C), :].T

    sub8 = lax.broadcasted_iota(jnp.int32, (A_HEADS, C), 0)

    def pair_row(x8):
        r0 = jnp.sum(jnp.where(sub8 == 2 * hp, x8, 0.0), axis=0, keepdims=True)
        r1 = jnp.sum(jnp.where(sub8 == 2 * hp + 1, x8, 0.0), axis=0, keepdims=True)
        return jnp.concatenate([r0, r1], axis=1)

    def chain_gates(ab_t, d):
        a8 = ab_t[d * A_HEADS:(d + 1) * A_HEADS, :]
        b8 = ab_t[(2 + d) * A_HEADS:(3 + d) * A_HEADS, :]
        g8 = neg_a[d * A_HEADS:(d + 1) * A_HEADS, :C] * jax.nn.softplus(a8 + dt_b[d * A_HEADS:(d + 1) * A_HEADS, :C])
        gc8 = _mm_split(g8, cum_b[d], 3)
        tot8 = jnp.broadcast_to(jnp.sum(g8, axis=-1, keepdims=True), (A_HEADS, C))
        gc_row, beta_row, tot_row = pair_row(gc8), pair_row(jax.nn.sigmoid(b8)), pair_row(tot8)
        cols = jnp.concatenate([gc_row, beta_row, tot_row, jnp.zeros((5, P), F32)], axis=0).T
        return gc_row, tot_row, cols[:, 0:1], cols[:, 1:2], cols[:, 2:3]

    def terms_body(j, carry):
        cs = [TERM_UNROLL * j + t for t in range(TERM_UNROLL)]
        ins = [chunk_inputs(c) for c in cs]
        qsq = [_mm_split(x[0] * x[0], jbd, 2) for x in ins]
        ksq = [_mm_split(x[1] * x[1], jbd, 2) for x in ins]
        qs = [x[0] * lax.rsqrt(s + EPS) * (A_DH ** -0.5) for x, s in zip(ins, qsq)]
        ks = [x[1] * lax.rsqrt(s + EPS) for x, s in zip(ins, ksq)]
        qst = [stack(x) for x in qs]
        kst = [stack(x) for x in ks]
        vst = [stack(x[2]) for x in ins]
        kst_t = [x.T for x in kst]
        kk = [_mm_nt(x, x) for x in kst]
        qk = [_mm_nt(x, y) for x, y in zip(qst, kst)]
        chains = [(t, d) for t in range(TERM_UNROLL) for d in range(2)]
        gates = [chain_gates(ins[t][3], d) for t, d in chains]
        decay, e_gc = [], []
        for (t, d), (gc_row, tot_row, gc_col, beta_col, tot_col) in zip(chains, gates):
            diff = jnp.broadcast_to(gc_col, (P, P)) - jnp.broadcast_to(gc_row, (P, P))
            decay.append(jnp.where(incl_m[d], jnp.exp(jnp.where(incl_m[d], diff, 0.0)), 0.0))
            e_gc.append(jnp.exp(gc_col))
        a_mats = [jnp.where(strict_m[d], g[3] * kk[t] * dc, 0.0) for (t, d), g, dc in zip(chains, gates, decay)]
        rhs = [vst[t] * g[3] + pltpu.roll(kst[t] * (g[3] * e), A_DH, 1)
               for (t, d), g, e in zip(chains, gates, e_gc)]
        xs = _unit_lower_solve_many(a_mats, rhs, same_blk, eye)
        for (t, d), x, g, e, dc in zip(chains, xs, gates, e_gc, decay):
            c = cs[t]
            gc_row, tot_row = g[0], g[1]
            u_s[d, c] = jnp.where(same_head, x, 0.0).astype(BF16)
            w_s[d, c] = pltpu.roll(jnp.where(same_head, 0.0, x), A_DH, 1).astype(BF16)
            qd_s[d, c] = (qst[t] * e).astype(BF16)
            at_s[d, c] = jnp.where(incl_m[d], qk[t] * dc, 0.0).astype(BF16)
            kt_s[d, c] = (kst_t[t] * jnp.exp(tot_row - gc_row)).astype(BF16)
            ge_s[d, c] = jnp.broadcast_to(jnp.exp(tot_row), (8, P))
        return carry

    lax.fori_loop(0, nct // TERM_UNROLL, terms_body, 0)

    def block_diag(s2):
        z = jnp.zeros((A_DH, A_DH), F32)
        return jnp.concatenate([jnp.concatenate([s2[0], z], axis=1),
                                jnp.concatenate([z, s2[1]], axis=1)], axis=0)

    for s in range(n_sub):
        st_s[2 * s] = block_diag(s0f_ref[s])
        st_s[2 * s + 1] = block_diag(s0b_ref[s])

    def scan_body(i, carry):
        chains = [(s, d, s * nc + (i if d == 0 else nc - 1 - i)) for s in range(n_sub) for d in range(2)]
        dot = functools.partial(jnp.dot, preferred_element_type=F32)
        s_bd = [st_s[2 * s + d] for s, d, c in chains]
        sb16 = [x.astype(BF16) for x in s_bd]
        ws = [dot(w_s[d, c], sb) for (s, d, c), sb in zip(chains, sb16)]
        qs_ = [dot(qd_s[d, c], sb) for (s, d, c), sb in zip(chains, sb16)]
        vb = [(u_s[d, c].astype(F32) - x).astype(BF16) for (s, d, c), x in zip(chains, ws)]
        av = [dot(at_s[d, c], x) for (s, d, c), x in zip(chains, vb)]
        kv = [dot(kt_s[d, c], x) for (s, d, c), x in zip(chains, vb)]
        for (s, d, c), sb, q_, a_, k_ in zip(chains, s_bd, qs_, av, kv):
            st_s[2 * s + d] = sb * ge_s[d, c][0:1, :] + k_
            o_st = q_ + a_
            dst = o_ref if d == 0 else ob
            dst[pl.ds(pl.multiple_of(c * C, C), C), :] = o_st[:C] + o_st[C:]
        return carry

    lax.fori_loop(0, nc, scan_body, 0)

    for s in range(n_sub):
        for d, ref in ((0, sf_ref), (1, sb_ref)):
            s_bd = st_s[2 * s + d]
            ref[s, 0] = s_bd[:A_DH, :A_DH]
            ref[s, 1] = s_bd[A_DH:, A_DH:]

    def finish(j, carry):
        bases = [pl.multiple_of((TERM_UNROLL * j + t) * C, C) for t in range(TERM_UNROLL)]
        o = [o_ref[pl.ds(b, C), :] + ob[pl.ds(b, C), :] for b in bases]
        ms = [_mm_split(x * x, jbd, 2) * (1.0 / A_DH) for x in o]
        for b, x, m in zip(bases, o, ms):
            o_ref[pl.ds(b, C), :] = x * lax.rsqrt(m + EPS) * og_ref[...] * _silu(z_ref[pl.ds(b, C), :])
        return carry

    lax.fori_loop(0, nct // TERM_UNROLL, finish, 0)


def _deltanet(proj, conv_w, gate_p, o_gain2, s0f, s0b, tri, jbd, *, seq_len, n_seq, n_sub, row_blk0):
    rows = n_sub * seq_len
    nct = rows // CHUNK
    rb = lambda b: row_blk0 + b
    col = lambda off: (lambda b, hp: (rb(b), off + hp))
    st_spec = pl.BlockSpec((n_sub, 2, A_DH, A_DH), lambda b, hp: (b, hp, 0, 0))
    in_specs = [pl.BlockSpec((rows, LANES), col(0)),
                pl.BlockSpec((rows, LANES), col(4)),
                pl.BlockSpec((rows, LANES), col(8)),
                pl.BlockSpec((rows, LANES), col(12)),
                pl.BlockSpec((rows, LANES), lambda b, hp: (rb(b), 20)),
                pl.BlockSpec((3, LANES), lambda b, hp: (0, hp)),
                pl.BlockSpec((3, LANES), lambda b, hp: (0, 4 + hp)),
                pl.BlockSpec((3, LANES), lambda b, hp: (0, 8 + hp)),
                pl.BlockSpec((2, 2 * A_HEADS, LANES), lambda b, hp: (0, 0, 0)),
                pl.BlockSpec((1, LANES), lambda b, hp: (0, 0)),
                st_spec, st_spec,
                pl.BlockSpec((CHUNK, CHUNK), lambda b, hp: (0, 0)),
                pl.BlockSpec((LANES, LANES), lambda b, hp: (0, 0))]
    args = [proj, proj, proj, proj, proj, conv_w, conv_w, conv_w, gate_p, o_gain2, s0f, s0b, tri, jbd]
    st_shape = jax.ShapeDtypeStruct((n_seq, A_HEADS, A_DH, A_DH), F32)
    tile = lambda dt: pltpu.VMEM((2, nct, LANES, LANES), dt)
    return pl.pallas_call(
        functools.partial(_deltanet_kernel, seq_len=seq_len, n_sub=n_sub),
        grid=(n_seq // n_sub, A_HEADS // 2),
        in_specs=in_specs,
        out_specs=[pl.BlockSpec((rows, LANES), lambda b, hp: (b, hp)), st_spec, st_spec],
        out_shape=[jax.ShapeDtypeStruct((n_seq * seq_len, A_WIDTH), F32), st_shape, st_shape],
        scratch_shapes=[tile(BF16), tile(BF16), tile(BF16), tile(BF16), tile(BF16),
                        pltpu.VMEM((2, nct, 8, LANES), F32),
                        pltpu.VMEM((2 * n_sub, LANES, LANES), F32),
                        pltpu.VMEM((rows, LANES), F32)],
        compiler_params=_cparams(("arbitrary", "arbitrary")),
        name="deltanet",
    )(*args)


def _dft_mats(n):
    idx = np.arange(n)
    ang = 2.0 * np.pi * ((idx[:, None] * idx[None, :]) % n) / n
    return np.cos(ang), np.sin(ang)


def _fnet_ctx_kernel(u_ref, cs_ref, dft_ref, o_ref):
    norm = 1.0 / math.sqrt(SEQ * LANES)
    for g in range(B_GROUPS):
        sl = slice(g * LANES, (g + 1) * LANES)
        p = _mm(u_ref[:, sl], cs_ref[...])
        stack = jnp.concatenate([p[:, :LANES], p[:, LANES:]], axis=0)
        o_ref[:, sl] = _mm(dft_ref[...], stack) * norm


def _fnet_ctx(proj, cs, dft):
    return pl.pallas_call(
        _fnet_ctx_kernel,
        grid=(BATCH,),
        in_specs=[pl.BlockSpec((SEQ, B_WIDTH), lambda b: (b, 4)),
                  pl.BlockSpec(cs.shape, lambda b: (0, 0)),
                  pl.BlockSpec(dft.shape, lambda b: (0, 0))],
        out_specs=pl.BlockSpec((SEQ, B_WIDTH), lambda b: (b, 0)),
        out_shape=jax.ShapeDtypeStruct((NC, B_WIDTH), F32),
        compiler_params=_cparams(("arbitrary",)),
        name="fnet_ctx",
    )(proj, cs, dft)


FN_SUB = 4


def _fnet_lat1_kernel(u_ref, ca_ref, m1_ref, twc_ref, tws_ref, o_ref):
    r = 64
    for j in range(FN_SUB):
        c = twc_ref[j]
        s = tws_ref[j]
        rows = slice(j * r, (j + 1) * r)
        for g in range(B_GROUPS):
            pa = _mm(u_ref[rows, g * LANES:(g + 1) * LANES], ca_ref[...])
            rhs = jnp.concatenate([pa[:, :2 * LANES], pa[:, 2 * LANES:]], axis=0)
            zz = _mm(m1_ref[...], rhs)
            zr, zi = zz[:, :LANES], zz[:, LANES:]
            o_ref[rows, 2 * g * LANES:(2 * g + 1) * LANES] = (zr * c + zi * s).astype(BF16)
            o_ref[rows, (2 * g + 1) * LANES:(2 * g + 2) * LANES] = (zi * c - zr * s).astype(BF16)


def _fnet_lat2_kernel(z_ref, m1_ref, o_ref):
    r = 64
    norm = 1.0 / math.sqrt(DEC_SEQ * LANES)
    for j in range(FN_SUB):
        rows = slice(j * r, (j + 1) * r)
        for g in range(B_GROUPS):
            rhs = jnp.concatenate([z_ref[rows, 2 * g * LANES:(2 * g + 1) * LANES],
                                   z_ref[rows, (2 * g + 1) * LANES:(2 * g + 2) * LANES]], axis=0)
            o_ref[rows, g * LANES:(g + 1) * LANES] = _mm(m1_ref[...], rhs) * norm


def _fnet_latent(proj, ca, m1, twc, tws):
    r = 64
    u = proj[NC:, 2048:2560].astype(BF16).reshape(DEC_BATCH, r, r, B_WIDTH)
    u = u.transpose(0, 2, 1, 3).reshape(DEC_BATCH * r * r, B_WIDTH)
    steps = DEC_BATCH * r // FN_SUB
    blk = FN_SUB * r
    z = pl.pallas_call(
        _fnet_lat1_kernel,
        grid=(steps,),
        in_specs=[pl.BlockSpec((blk, B_WIDTH), lambda s: (s, 0)),
                  pl.BlockSpec(ca.shape, lambda s: (0, 0)),
                  pl.BlockSpec(m1.shape, lambda s: (0, 0)),
                  pl.BlockSpec((FN_SUB, r, LANES), lambda s: (s % (r // FN_SUB), 0, 0)),
                  pl.BlockSpec((FN_SUB, r, LANES), lambda s: (s % (r // FN_SUB), 0, 0))],
        out_specs=pl.BlockSpec((blk, 2 * B_WIDTH), lambda s: (s, 0)),
        out_shape=jax.ShapeDtypeStruct((NL, 2 * B_WIDTH), BF16),
        compiler_params=_cparams(("arbitrary",)),
        name="fnet_lat1",
    )(u, ca, m1, twc, tws)
    z = z.reshape(DEC_BATCH, r, r, 2 * B_WIDTH).transpose(0, 2, 1, 3).reshape(NL, 2 * B_WIDTH)
    y = pl.pallas_call(
        _fnet_lat2_kernel,
        grid=(steps,),
        in_specs=[pl.BlockSpec((blk, 2 * B_WIDTH), lambda s: (s, 0)),
                  pl.BlockSpec(m1.shape, lambda s: (0, 0))],
        out_specs=pl.BlockSpec((blk, B_WIDTH), lambda s: (s, 0)),
        out_shape=jax.ShapeDtypeStruct((NL, B_WIDTH), F32),
        compiler_params=_cparams(("arbitrary",)),
        name="fnet_lat2",
    )(z, m1)
    return y.reshape(DEC_BATCH, r, r, B_WIDTH).transpose(0, 2, 1, 3).reshape(NL, B_WIDTH)


def _head_masks():
    lane = lax.broadcasted_iota(jnp.int32, (1, LANES), 1)
    return lane < NA_DH


def _attend_many(chains):
    s = [[_mm_nt(q, k) if b is None else _mm_nt(q, k) + b for k, v, b in kv] for q, kv in chains]
    m = [functools.reduce(jnp.maximum, [jnp.max(x, axis=-1, keepdims=True) for x in xs]) for xs in s]
    p = [[jnp.exp(x - mi) for x in xs] for xs, mi in zip(s, m)]
    l = [sum(jnp.sum(x, axis=-1, keepdims=True) for x in xs) for xs in p]
    o = [sum(_mm(x, v) for x, (k, v, b) in zip(xs, kv)) for xs, (q, kv) in zip(p, chains)]
    return [oi / li for oi, li in zip(o, l)]


def _na_ctx_kernel(q_ref, k_ref, v_ref, o_ref):
    first = _head_masks()
    n_split = 2
    rows = SEQ // n_split
    chains = []
    for p in range(NA_CTX_PAIRS):
        lanes = slice(p * LANES, (p + 1) * LANES)
        k = k_ref[:, lanes].astype(BF16)
        v = v_ref[:, lanes].astype(BF16)
        for j in range(n_split):
            q = q_ref[j * rows:(j + 1) * rows, lanes]
            for a in range(2):
                chains.append((jnp.where(first if a == 0 else jnp.logical_not(first), q * NA_SCALE, 0.0),
                               [(k, v, None)]))
    outs = _attend_many(chains)
    for p in range(NA_CTX_PAIRS):
        for j in range(n_split):
            i = 2 * (p * n_split + j)
            o_ref[j * rows:(j + 1) * rows, p * LANES:(p + 1) * LANES] = jnp.where(first, outs[i], outs[i + 1])


def _na_ctx(qkv):
    w = NA_CTX_PAIRS * LANES
    nblk = D // w
    return pl.pallas_call(
        _na_ctx_kernel,
        grid=(BATCH, nblk),
        in_specs=[pl.BlockSpec((SEQ, w), lambda b, hp: (b, hp)),
                  pl.BlockSpec((SEQ, w), lambda b, hp: (b, nblk + hp)),
                  pl.BlockSpec((SEQ, w), lambda b, hp: (b, 2 * nblk + hp))],
        out_specs=pl.BlockSpec((SEQ, w), lambda b, hp: (b, hp)),
        out_shape=jax.ShapeDtypeStruct((NC, D), F32),
        compiler_params=_cparams(("arbitrary", "arbitrary")),
        name="na_ctx",
    )(qkv, qkv, qkv)


def _na_lat_kernel(q_ref, k_ref, v_ref, kc_ref, vc_ref, tt_ref, o_ref):
    first = _head_masks()
    rows = DEC_SEQ // GRID_W
    kctx = kc_ref[...].astype(BF16)
    vctx = vc_ref[...].astype(BF16)
    nkeys = WIN_R * GRID_W

    def rows_body(j, carry):
        chains = []
        for t in range(NA_ROW_UNROLL):
            r = NA_ROW_UNROLL * j + t
            r0 = jnp.clip(r - WIN_R // 2, 0, rows - WIN_R)
            dr0 = r0 - r + (WIN_R - 1)
            q = q_ref[pl.ds(pl.multiple_of(r * GRID_W, GRID_W), GRID_W), :] * NA_SCALE
            kbase = pl.multiple_of(r0 * GRID_W, GRID_W)
            kl = k_ref[pl.ds(kbase, nkeys), :].astype(BF16)
            vl = v_ref[pl.ds(kbase, nkeys), :].astype(BF16)
            for a in range(2):
                qm = jnp.where(first if a == 0 else jnp.logical_not(first), q, 0.0)
                bias = jnp.concatenate([tt_ref[a, dr0 + 2 * i] for i in range(WIN_R // 2)], axis=1)
                chains.append((qm, [(kl, vl, bias), (kctx, vctx, None)]))
        outs = _attend_many(chains)
        for t in range(NA_ROW_UNROLL):
            r = NA_ROW_UNROLL * j + t
            o_ref[pl.ds(pl.multiple_of(r * GRID_W, GRID_W), GRID_W), :] = jnp.where(first, outs[2 * t], outs[2 * t + 1])
        return carry

    lax.fori_loop(0, rows // NA_ROW_UNROLL, rows_body, 0)


def _na_latent(qkv, cache_k2, cache_v2, tt2):
    rb0 = NC // DEC_SEQ
    return pl.pallas_call(
        _na_lat_kernel,
        grid=(DEC_BATCH, NA_HEADS // 2),
        in_specs=[pl.BlockSpec((DEC_SEQ, LANES), lambda b, hp: (rb0 + b, hp)),
                  pl.BlockSpec((DEC_SEQ, LANES), lambda b, hp: (rb0 + b, 8 + hp)),
                  pl.BlockSpec((DEC_SEQ, LANES), lambda b, hp: (rb0 + b, 16 + hp)),
                  pl.BlockSpec((None, 256, LANES), lambda b, hp: (b, 0, hp)),
                  pl.BlockSpec((None, 256, LANES), lambda b, hp: (b, 0, hp)),
                  pl.BlockSpec((2, 2 * WIN_R - 2, GRID_W, LANES), lambda b, hp: (hp, 0, 0, 0))],
        out_specs=pl.BlockSpec((DEC_SEQ, LANES), lambda b, hp: (b, hp)),
        out_shape=jax.ShapeDtypeStruct((NL, D), F32),
        compiler_params=_cparams(("arbitrary", "arbitrary")),
        name="na_latent",
    )(qkv, qkv, qkv, cache_k2, cache_v2, tt2)


def _rpb_tables(rpb):
    col = np.arange(GRID_W)
    start = np.clip(col - WIN_C // 2, 0, GRID_W - WIN_C)
    inside = (col[None, :] >= start[:, None]) & (col[None, :] < start[:, None] + WIN_C)
    w = GRID_W
    period = 2 * w - 1
    x = jnp.pad(rpb, ((0, 0), (0, 0), (w - WIN_C, w - WIN_C)))
    flat = jnp.tile(x, (1, 1, w))[:, :, w - 1:w - 1 + w * (period - 1)]
    t = flat.reshape(NA_HEADS, 2 * WIN_R - 1, w, period - 1)[..., :w]
    t = jnp.where(inside[None, None], t, NEG)
    return jnp.concatenate([t[:, :-1], t[:, 1:]], axis=-1)


ROUTE_SPLIT = 2
ROUTE_ROWS = 40


def _router_kernel(*refs, n_in):
    a_refs = refs[:2 * n_in]
    w_refs = refs[2 * n_in:3 * n_in]
    x_refs = refs[3 * n_in:-11]
    m_ref, g_ref, wr_ref, br_ref, tri_ref, xn_ref, xf_ref, ri_ref, rw_ref, cnt_ref, base_scr = refs[-11:]
    i = pl.program_id(0)

    @pl.when(i == 0)
    def _():
        base_scr[...] = jnp.zeros_like(base_scr)
        cnt_ref[...] = jnp.zeros_like(cnt_ref)

    rows = [slice(t * TM // ROUTE_SPLIT, (t + 1) * TM // ROUTE_SPLIT) for t in range(ROUTE_SPLIT)]
    is_ctx = pl.program_id(0) < NC // TM
    pick = lambda pair, r: jnp.where(is_ctx, pair[0][r, :], pair[1][r, :]) if len(pair) == 2 else pair[0][r, :]
    acc = [sum(_mm(pick(a_refs[2 * j:2 * j + 2], r), w_ref[...]) for j, w_ref in enumerate(w_refs)) for r in rows]
    x_new = [pick(x_refs, r) + m_ref[2:3, :] * a for r, a in zip(rows, acc)]
    for r, xn in zip(rows, x_new):
        xn_ref[r, :] = xn
    h = [_modulated_norm(xn, m_ref, g_ref, 3, 4) for xn in x_new]
    for r, hh in zip(rows, h):
        xf_ref[r, :] = _pack_bf16_pairs(hh[:, :D // 2], hh[:, D // 2:])
    logits = jnp.concatenate(
        [lax.dot_general(wr_ref[...], hh, (((1,), (1,)), ((), ())), preferred_element_type=F32, precision=HIGHEST)
         for hh in h], axis=1) + br_ref[:, 0:1]
    row = lax.broadcasted_iota(jnp.int32, logits.shape, 0)
    cmax = lambda x: jnp.max(x, axis=0, keepdims=True)
    cmin = lambda x: jnp.min(x, axis=0, keepdims=True)
    csum = lambda x: jnp.sum(x, axis=0, keepdims=True)

    gmask = row < N_GROUPS
    mg = cmax(jnp.where(gmask, logits, NEG))
    eg = jnp.where(gmask, jnp.exp(jnp.where(gmask, logits - mg, NEG)), 0.0)
    pg = eg / csum(eg)
    p_grp = cmax(pg)
    grp = cmin(jnp.where(jnp.logical_and(gmask, pg == p_grp), row, ROUTE_ROWS))
    lo = N_GROUPS + grp * EXP_PER_GROUP
    emask = jnp.logical_and(row >= lo, row < lo + EXP_PER_GROUP)
    me = cmax(jnp.where(emask, logits, NEG))
    ee = jnp.where(emask, jnp.exp(jnp.where(emask, logits - me, NEG)), 0.0)
    pe = ee / csum(ee)
    p1 = cmax(pe)
    i1 = cmin(jnp.where(jnp.logical_and(emask, pe == p1), row, ROUTE_ROWS))
    m2 = jnp.logical_and(emask, row != i1)
    p2 = cmax(jnp.where(m2, pe, -1.0))
    i2 = cmin(jnp.where(jnp.logical_and(m2, pe == p2), row, ROUTE_ROWS))
    den = p1 + p2
    w1 = p_grp * p1 / den
    w2 = p_grp * p2 / den

    sel1 = row == i1
    sel2 = row == i2
    oh = jnp.where(jnp.logical_or(sel1, sel2), 1.0, 0.0).astype(BF16)
    before = jnp.dot(oh, tri_ref[...], preferred_element_type=F32) + base_scr[:, 0:1]
    rank1 = csum(jnp.where(sel1, before, 0.0))
    rank2 = csum(jnp.where(sel2, before, 0.0))
    base_scr[...] = base_scr[...] + jnp.sum(oh.astype(F32), axis=1, keepdims=True)
    cnt_ref[...] = cnt_ref[...] + lax.dot_general(jnp.ones((8, TM), BF16), oh, (((1,), (1,)), ((), ())),
                                                  preferred_element_type=F32)
    sub = lax.broadcasted_iota(jnp.int32, (8, TM), 0)
    ri_ref[...] = jnp.where(sub == 0, i1 - N_GROUPS, jnp.where(sub == 1, i2 - N_GROUPS,
                  jnp.where(sub == 2, rank1.astype(jnp.int32), jnp.where(sub == 3, rank2.astype(jnp.int32), 0))))
    rw_ref[...] = jnp.where(sub == 0, w1, jnp.where(sub == 1, w2, 0.0))


def _router(a_pairs, w_list, x, m3, gain, wr_t, br_t, tri_upper):
    in_specs, args = [], []
    for pair in a_pairs:
        specs, ops = _token_specs(pair)
        in_specs += specs
        args += ops
    x_specs, x_args = _token_specs(x)
    in_specs += ([pl.BlockSpec(w.shape, lambda i: (0, 0)) for w in w_list] + x_specs
                 + [pl.BlockSpec((None, 6, D), lambda i: (_cond_row(i), 0, 0)),
                    pl.BlockSpec((1, D), lambda i: (0, 0)),
                    pl.BlockSpec((ROUTE_ROWS, D), lambda i: (0, 0)),
                    pl.BlockSpec((ROUTE_ROWS, LANES), lambda i: (0, 0)),
                    pl.BlockSpec((TM, TM), lambda i: (0, 0))])
    return pl.pallas_call(
        functools.partial(_router_kernel, n_in=len(a_pairs)),
        grid=(N // TM,),
        in_specs=in_specs,
        out_specs=[pl.BlockSpec((TM, D), lambda i: (i, 0)),
                   pl.BlockSpec((TM, D // 2), lambda i: (i, 0)),
                   pl.BlockSpec((8, TM), lambda i: (0, i)),
                   pl.BlockSpec((8, TM), lambda i: (0, i)),
                   pl.BlockSpec((8, ROUTE_ROWS), lambda i: (0, 0))],
        out_shape=[jax.ShapeDtypeStruct((N, D), F32),
                   jax.ShapeDtypeStruct((N, D // 2), jnp.int32),
                   jax.ShapeDtypeStruct((8, N), jnp.int32),
                   jax.ShapeDtypeStruct((8, N), F32),
                   jax.ShapeDtypeStruct((8, ROUTE_ROWS), F32)],
        scratch_shapes=[pltpu.VMEM((ROUTE_ROWS, LANES), F32)],
        compiler_params=_cparams(("arbitrary",)),
        name="router",
    )(*args, *w_list, *x_args, m3, gain, wr_t, br_t, tri_upper)


def _row_copy(src_hbm, row, dst, r, sem):
    return pltpu.make_async_copy(src_hbm.at[pl.ds(row, 1)], dst.at[pl.ds(r, 1)], sem)


def _expert_kernel(be_ref, nu_ref, d_ref, pad_ref, xp_hbm, wg_ref, wu_ref, wd_ref, ys_ref, xres, xbuf, st_ref, sem):
    del be_ref
    b = pl.program_id(0)
    half = D // 2

    def gather(blk, slot, part=0, parts=1):
        base = blk * MOE_ROWS
        for r in range(part * MOE_ROWS // parts, (part + 1) * MOE_ROWS // parts):
            xbuf[slot, pl.ds(r, 1), :] = xres[pl.ds(st_ref[base + r], 1), :]

    @pl.when(b == 0)
    def _():
        cp = pltpu.make_async_copy(xp_hbm, xres, sem.at[0])
        cp.start()

        def clear_tail(e, c):
            def clear(s_, c2):
                st_ref[s_] = 0
                return c2

            return lax.fori_loop(pad_ref[2 * e], pad_ref[2 * e + 1], clear, c)

        lax.fori_loop(0, N_EXPERTS + 1, clear_tail, 0)

        def invert(j, c):
            slots = [d_ref[16 * j + t] for t in range(16)]
            for t in range(16):
                st_ref[slots[t]] = 8 * j + t // 2
            return c

        lax.fori_loop(0, 2 * N // 16, invert, 0)
        cp.wait()
        gather(0, 0)

    @pl.when(b < nu_ref[0])
    def _():
        nxt = (jnp.minimum(b + 1, N_SLOT_BLOCKS - 1), (b + 1) % 2)
        x_lo, x_hi = _unpack_bf16_pairs(xbuf[b % 2])
        dot = functools.partial(jnp.dot, preferred_element_type=F32)
        gather(*nxt, 0, 8)
        g = dot(x_lo, wg_ref[:half, :].astype(BF16))
        gather(*nxt, 1, 8)
        g = g + dot(x_hi, wg_ref[half:, :].astype(BF16))
        gather(*nxt, 2, 8)
        u = dot(x_lo, wu_ref[:half, :].astype(BF16))
        gather(*nxt, 3, 8)
        u = u + dot(x_hi, wu_ref[half:, :].astype(BF16))
        hb = (_silu(g) * u).astype(BF16)
        quarter = D // 4
        for j in range(4):
            gather(*nxt, 4 + j, 8)
            cols = slice(j * quarter, (j + 1) * quarter)
            ys_ref[:, cols] = jnp.dot(hb, wd_ref[:, cols].astype(BF16), preferred_element_type=F32)

    @pl.when(b >= nu_ref[0])
    def _():
        ys_ref[...] = jnp.zeros_like(ys_ref)


def _experts(block_e, n_used, dest, pad_ranges, xp, w_gate, w_up, w_down, layer):
    grid_spec = pltpu.PrefetchScalarGridSpec(
        num_scalar_prefetch=4,
        grid=(N_SLOT_BLOCKS,),
        in_specs=[pl.BlockSpec(memory_space=pl.ANY),
                  pl.BlockSpec((None, None, D, D_EXPERT), lambda b, be, nu, st, pd: (layer, be[b], 0, 0)),
                  pl.BlockSpec((None, None, D, D_EXPERT), lambda b, be, nu, st, pd: (layer, be[b], 0, 0)),
                  pl.BlockSpec((None, None, D_EXPERT, D), lambda b, be, nu, st, pd: (layer, be[b], 0, 0))],
        out_specs=pl.BlockSpec((MOE_ROWS, D), lambda b, be, nu, st, pd: (b, 0)),
        scratch_shapes=[pltpu.VMEM((N, D // 2), jnp.int32),
                        pltpu.VMEM((2, MOE_ROWS, D // 2), jnp.int32),
                        pltpu.SMEM((N_SLOT_BLOCKS * MOE_ROWS,), jnp.int32),
                        pltpu.SemaphoreType.DMA((1,))])
    return pl.pallas_call(
        _expert_kernel,
        grid_spec=grid_spec,
        out_shape=jax.ShapeDtypeStruct((N_SLOT_BLOCKS * MOE_ROWS, D), F32),
        compiler_params=pltpu.CompilerParams(dimension_semantics=("arbitrary",), vmem_limit_bytes=EXPERT_VMEM_LIMIT),
        name="experts",
    )(block_e, n_used, dest, pad_ranges, xp, w_gate, w_up, w_down)


def _combine_kernel(d_ref, ys_hbm, x_ref, m_ref, rw_ref, fn_ref, o_ref, buf, sem, *, final, tile0, n_tiles):
    i = pl.program_id(0)

    def issue(tile, slot):
        base = (tile0 + tile) * (2 * TM)
        for r in range(TM):
            for kk in range(2):
                _row_copy(ys_hbm, d_ref[base + 2 * r + kk], buf.at[slot, kk], r, sem.at[slot]).start()

    @pl.when(i == 0)
    def _():
        issue(0, 0)

    @pl.when(i + 1 < n_tiles)
    def _():
        issue(i + 1, (i + 1) % 2)

    slot = i % 2
    for kk in range(2):
        pltpu.make_async_copy(ys_hbm.at[pl.ds(0, TM)], buf.at[slot, kk], sem.at[slot]).wait()
    w = rw_ref[...]
    y = w[:, 0:1] * buf[slot, 0] + w[:, 1:2] * buf[slot, 1]
    out = x_ref[...] + m_ref[5:6, :] * y
    if final:
        ms = jnp.mean(out * out, axis=-1, keepdims=True)
        out = out * lax.rsqrt(ms + EPS) * fn_ref[...]
    o_ref[...] = out


def _combine(dest_flat, ys, x, m3, rw, final_norm, final, tile0=0, n_tiles=N // TM):
    grid_spec = pltpu.PrefetchScalarGridSpec(
        num_scalar_prefetch=1,
        grid=(n_tiles,),
        in_specs=[pl.BlockSpec(memory_space=pl.ANY),
                  pl.BlockSpec((TM, D), lambda i, d: (tile0 + i, 0)),
                  pl.BlockSpec((None, 6, D), lambda i, d: (_cond_row(tile0 + i), 0, 0)),
                  pl.BlockSpec((TM, 2), lambda i, d: (tile0 + i, 0)),
                  pl.BlockSpec((1, D), lambda i, d: (0, 0))],
        out_specs=pl.BlockSpec((TM, D), lambda i, d: (i, 0)),
        scratch_shapes=[pltpu.VMEM((2, 2, TM, D), F32), pltpu.SemaphoreType.DMA((2,))])
    return pl.pallas_call(
        functools.partial(_combine_kernel, final=final, tile0=tile0, n_tiles=n_tiles),
        grid_spec=grid_spec,
        out_shape=jax.ShapeDtypeStruct((n_tiles * TM, D), F32),
        compiler_params=_cparams(("arbitrary",)),
        name="combine",
    )(dest_flat, ys, x, m3, rw, final_norm)


def _combine_proj_kernel(d_ref, ys_hbm, x_ref, m_ref, rw_ref, mn_ref, gn_ref, w_ref, xo_ref, o_ref, buf, sem, *,
                         n_chunk):
    i = pl.program_id(0)
    n_tiles = N // TM
    n_parts = o_ref.shape[1] // n_chunk

    def issue(tile, slot, part=0, parts=1):
        base = tile * (2 * TM)
        for r in range(part * TM // parts, (part + 1) * TM // parts):
            for kk in range(2):
                _row_copy(ys_hbm, d_ref[base + 2 * r + kk], buf.at[slot, kk], r, sem.at[slot]).start()

    def wait(slot):
        for kk in range(2):
            pltpu.make_async_copy(ys_hbm.at[pl.ds(0, TM)], buf.at[slot, kk], sem.at[slot]).wait()

    @pl.when(i == 0)
    def _():
        issue(0, 0)

    slot = i % 2
    wait(slot)
    w = rw_ref[...]
    x_new = x_ref[...] + m_ref[5:6, :] * (w[:, 0:1] * buf[slot, 0] + w[:, 1:2] * buf[slot, 1])
    xo_ref[...] = x_new
    hb = _modulated_norm(x_new, mn_ref, gn_ref, 0, 1).astype(BF16)
    nxt = jnp.minimum(i + 1, n_tiles - 1)
    for j in range(n_parts):
        issue(nxt, 1 - slot, j, n_parts)
        sl = slice(j * n_chunk, (j + 1) * n_chunk)
        o_ref[:, sl] = jnp.dot(hb, w_ref[:, sl], preferred_element_type=F32)

    @pl.when(i == n_tiles - 1)
    def _():
        wait(1 - slot)


def _combine_proj(dest_flat, ys, x, m3, rw, m3_next, gain_next, w_bf16, n_chunk):
    nout = w_bf16.shape[1]
    grid_spec = pltpu.PrefetchScalarGridSpec(
        num_scalar_prefetch=1,
        grid=(N // TM,),
        in_specs=[pl.BlockSpec(memory_space=pl.ANY),
                  pl.BlockSpec((TM, D), lambda i, d: (i, 0)),
                  pl.BlockSpec((None, 6, D), lambda i, d: (_cond_row(i), 0, 0)),
                  pl.BlockSpec((TM, 2), lambda i, d: (i, 0)),
                  pl.BlockSpec((None, 6, D), lambda i, d: (_cond_row(i), 0, 0)),
                  pl.BlockSpec((1, D), lambda i, d: (0, 0)),
                  pl.BlockSpec((D, nout), lambda i, d: (0, 0))],
        out_specs=[pl.BlockSpec((TM, D), lambda i, d: (i, 0)),
                   pl.BlockSpec((TM, nout), lambda i, d: (i, 0))],
        scratch_shapes=[pltpu.VMEM((2, 2, TM, D), F32), pltpu.SemaphoreType.DMA((2,))])
    return pl.pallas_call(
        functools.partial(_combine_proj_kernel, n_chunk=n_chunk),
        grid_spec=grid_spec,
        out_shape=[jax.ShapeDtypeStruct((N, D), F32), jax.ShapeDtypeStruct((N, nout), F32)],
        compiler_params=_cparams(("arbitrary",)),
        name="combine_proj",
    )(dest_flat, ys, x, m3, rw, m3_next, gain_next, w_bf16)


def _hier_moe(a_pairs, w_list, x, m3, gain, layer, w_rg, b_rg, w_re, b_re, w_gate, w_up, w_down, tri_tm,
              final_norm, final, next_proj=None):
    pad = ROUTE_ROWS - N_GROUPS - N_EXPERTS
    wr_t = jnp.concatenate([w_rg.T, w_re.transpose(0, 2, 1).reshape(N_EXPERTS, D), jnp.zeros((pad, D), F32)], axis=0)
    br_t = jnp.broadcast_to(jnp.concatenate([b_rg, b_re.reshape(N_EXPERTS), jnp.zeros((pad,), F32)])[:, None],
                            (ROUTE_ROWS, LANES))
    x, xf, ri_t, rw_t, cnt = _router(a_pairs, w_list, x, m3, gain, wr_t, br_t, tri_tm)
    e_idx = ri_t[0:2].T
    rank = ri_t[2:4].T
    rw = rw_t[0:2].T
    counts = cnt[0, N_GROUPS:N_GROUPS + N_EXPERTS].astype(jnp.int32)
    padded = (counts + MOE_ROWS - 1) // MOE_ROWS * MOE_ROWS
    end_pad = jnp.cumsum(padded)
    start_pad = end_pad - padded
    experts = jnp.arange(N_EXPERTS, dtype=jnp.int32)
    start_of = jnp.sum(jnp.where(e_idx[:, :, None] == experts, start_pad, 0), axis=-1)
    dest = (start_of + rank).reshape(-1).astype(jnp.int32)
    block_start = jnp.arange(N_SLOT_BLOCKS, dtype=jnp.int32) * MOE_ROWS
    block_e = jnp.minimum(jnp.sum((end_pad[None, :] <= block_start[:, None]).astype(jnp.int32), axis=1),
                          N_EXPERTS - 1)
    n_used = (end_pad[-1:] // MOE_ROWS).astype(jnp.int32)
    tail = jnp.stack([end_pad[-1], jnp.minimum(end_pad[-1] + MOE_ROWS, N_SLOT_BLOCKS * MOE_ROWS)])
    pad_ranges = jnp.concatenate([jnp.stack([start_pad + counts, end_pad], axis=1).reshape(-1), tail]).astype(jnp.int32)
    ys = _experts(block_e, n_used, dest, pad_ranges, xf, w_gate, w_up, w_down, layer)
    if not final:
        return _combine_proj(dest, ys, x, m3, rw, *next_proj)
    nct = NC // TM
    return (_combine(dest, ys, x, m3, rw, final_norm, True, 0, nct),
            _combine(dest, ys, x, m3, rw, final_norm, True, nct, N // TM - nct))


def kernel(x_prompt, x_sample, state_A_fwd, state_A_bwd, cache_k, cache_v, c, c_ctx, mod_w, mod_b, norm_mix, norm_ffn, ab_w_in, ab_conv, ab_a_log, ab_dt_bias, ab_o_gain, ab_w_out, na_w_qkv, na_rpb, na_w_out, moe_w_rg, moe_b_rg, moe_w_re, moe_b_re, moe_w_gate, moe_w_up, moe_w_down, final_norm):
    x = (x_prompt.reshape(NC, D), x_sample.reshape(NL, D))
    cond8 = jnp.concatenate([c_ctx[None, :], c, jnp.zeros((8 - 1 - DEC_BATCH, D), F32)], axis=0)
    mods = _ada_params(cond8, mod_w, mod_b).reshape(DEPTH, 8, 6, D)

    tri_tm = jnp.asarray(np.triu(np.ones((TM, TM)), 1), BF16)
    tri_c = jnp.asarray(np.tril(np.ones((CHUNK, CHUNK))), F32)
    half = np.arange(LANES) < A_DH
    jbd = jnp.asarray((half[:, None] == half[None, :]).astype(np.float32))
    fn = final_norm[None, :]

    m3 = mods[0]
    w_in = ab_w_in[0]
    w_in = jnp.concatenate([w_in[:, :2048], w_in[:, 2080:2592], w_in[:, 2048:2080],
                            jnp.zeros((D, AB_COLS - 2592), F32)], axis=1).astype(BF16)
    proj = _modproj(x, m3, norm_mix[0][None, :], w_in, 896)
    gate_p = jnp.broadcast_to(jnp.stack([ab_a_log[0].reshape(-1), ab_dt_bias[0].reshape(-1)])[:, :, None],
                              (2, 2 * A_HEADS, LANES))
    o_gain2 = jnp.tile(ab_o_gain[0], 2)[None, :]
    zeros_state = jnp.zeros((BATCH, A_HEADS, A_DH, A_DH), F32)
    mix_a_c, s_f, s_b = _deltanet(proj, ab_conv[0], gate_p, o_gain2, zeros_state, zeros_state, tri_c, jbd,
                                  seq_len=SEQ, n_seq=BATCH, n_sub=4, row_blk0=0)
    mix_a_l, _, _ = _deltanet(proj, ab_conv[0], gate_p, o_gain2, state_A_fwd[:, 0], state_A_bwd[:, 0], tri_c, jbd,
                              seq_len=DEC_SEQ, n_seq=DEC_BATCH, n_sub=1, row_blk0=NC // DEC_SEQ)

    cc, sc = _dft_mats(LANES)
    ct, st = _dft_mats(SEQ)
    c64, s64 = _dft_mats(64)
    cs = jnp.asarray(np.concatenate([cc, sc], axis=1), BF16)
    dft = jnp.asarray(np.concatenate([ct, -st], axis=1), BF16)
    ca = jnp.asarray(np.concatenate([cc, -sc, -sc, -cc], axis=1), BF16)
    m1 = jnp.asarray(np.concatenate([c64, s64], axis=1), BF16)
    tw_idx = np.arange(64)
    tw_ang = 2.0 * np.pi * (tw_idx[:, None] * tw_idx[None, :]) / DEC_SEQ
    twc = jnp.broadcast_to(jnp.asarray(np.cos(tw_ang), F32)[:, :, None], (64, 64, LANES))
    tws = jnp.broadcast_to(jnp.asarray(np.sin(tw_ang), F32)[:, :, None], (64, 64, LANES))
    mix_b_c = _fnet_ctx(proj, cs, dft)
    mix_b_l = _fnet_latent(proj, ca, m1, twc, tws)

    w_out = ab_w_out[0].astype(BF16)
    x, qkv = _hier_moe([(mix_a_c, mix_a_l), (mix_b_c, mix_b_l)], [w_out[:A_WIDTH], w_out[A_WIDTH:]], x, m3,
                       norm_ffn[0][None, :], 0, moe_w_rg[0], moe_b_rg[0], moe_w_re[0], moe_b_re[0],
                       moe_w_gate, moe_w_up, moe_w_down, tri_tm, fn, False,
                       next_proj=(mods[1], norm_mix[1][None, :], na_w_qkv[0].astype(BF16), 512))

    m3 = mods[1]
    attn_c = _na_ctx(qkv)
    attn_l = _na_latent(qkv, cache_k[:, 0].reshape(DEC_BATCH, 256, D), cache_v[:, 0].reshape(DEC_BATCH, 256, D),
                        _rpb_tables(na_rpb[0]))
    y_c, y_l = _hier_moe([(attn_c, attn_l)], [na_w_out[0].astype(BF16)], x, m3, norm_ffn[1][None, :], 1, moe_w_rg[1], moe_b_rg[1], moe_w_re[1], moe_b_re[1],
                         moe_w_gate, moe_w_up, moe_w_down, tri_tm, fn, True)

    new_k = qkv[:NC, D:2 * D].reshape(BATCH, 1, SEQ, NA_HEADS, NA_DH)
    new_v = qkv[:NC, 2 * D:].reshape(BATCH, 1, SEQ, NA_HEADS, NA_DH)
    return (y_c.reshape(BATCH, SEQ, D), y_l.reshape(DEC_BATCH, DEC_SEQ, D),
            s_f[:, None], s_b[:, None], new_k, new_v)
```

```python
import functools
import math

import numpy as np
import jax
import jax.numpy as jnp
from jax import lax
from jax.experimental import pallas as pl
from jax.experimental.pallas import tpu as pltpu

F32 = jnp.float32
BF16 = jnp.bfloat16
HIGHEST = lax.Precision.HIGHEST

D = 1024
BATCH, SEQ = 32, 256
DEC_BATCH, DEC_SEQ = 2, 4096
NC = BATCH * SEQ
NL = DEC_BATCH * DEC_SEQ
N = NC + NL
DEPTH = 2
GRID_W = 64
A_DH = 64
A_HEADS = 8
A_WIDTH = 512
CHUNK = 64
B_WIDTH = 512
B_GROUPS = 4
NA_DH = 64
NA_HEADS = 16
WIN_R, WIN_C = 8, 16
N_GROUPS, EXP_PER_GROUP, N_EXPERTS = 4, 8, 32
D_EXPERT = 512
EPS = 1e-6

LANES = 128
TM = 256
MOE_ROWS = 256
N_SLOT_BLOCKS = (2 * N) // MOE_ROWS + N_EXPERTS
AB_COLS = 2688
VMEM_LIMIT = 56 * 1024 * 1024
EXPERT_VMEM_LIMIT = 60 * 1024 * 1024
NEG = -1e30


def _cparams(sem):
    return pltpu.CompilerParams(dimension_semantics=sem, vmem_limit_bytes=VMEM_LIMIT)


def _mm(a, b):
    return jnp.dot(a.astype(BF16), b.astype(BF16), preferred_element_type=F32)


def _mm_nt(a, b):
    return lax.dot_general(a.astype(BF16), b.astype(BF16), (((1,), (1,)), ((), ())),
                           preferred_element_type=F32)


SOLVE_BLK = 16
NA_SCALE = NA_DH ** -0.5
NA_CTX_PAIRS = 2
NA_ROW_UNROLL = 8
TERM_UNROLL = 4


def _unit_lower_solve(a_mat, rhs, same_blk, eye):
    dg = jnp.where(same_blk, a_mat, 0.0)
    p = -dg
    dinv = eye + p
    for _ in range(int(math.log2(SOLVE_BLK)) - 1):
        p = _mm(p, p)
        dinv = dinv + _mm(p, dinv)
    mp = -_mm(dinv, a_mat - dg)
    y = _mm(dinv, rhs)
    y = y + _mm(mp, y)
    for _ in range(int(math.log2(a_mat.shape[0] // SOLVE_BLK)) - 1):
        mp = _mm(mp, mp)
        y = y + _mm(mp, y)
    return y


def _unit_lower_solve_many(a_mats, rhs, same_blk, eye):
    off = [jnp.where(same_blk, 0.0, a).astype(BF16) for a in a_mats]
    p = [jnp.where(same_blk, -a, 0.0).astype(BF16) for a in a_mats]
    dinv = [(eye + x.astype(F32)).astype(BF16) for x in p]
    for _ in range(int(math.log2(SOLVE_BLK)) - 1):
        p = [_mm(x, x).astype(BF16) for x in p]
        dinv = [(di.astype(F32) + _mm(x, di)).astype(BF16) for x, di in zip(p, dinv)]
    mp = [(-_mm(di, o)).astype(BF16) for di, o in zip(dinv, off)]
    y = [_mm(di, r) for di, r in zip(dinv, rhs)]
    y = [yi + _mm(m, yi) for m, yi in zip(mp, y)]
    for _ in range(int(math.log2(a_mats[0].shape[0] // SOLVE_BLK)) - 1):
        mp = [_mm(m, m).astype(BF16) for m in mp]
        y = [yi + _mm(m, yi) for m, yi in zip(mp, y)]
    return y


def _mm_split(a, b, parts, split_rhs=False):
    x = b if split_rhs else a
    acc = None
    for _ in range(parts):
        piece = x.astype(BF16)
        x = x - piece.astype(F32)
        term = (jnp.dot(a.astype(BF16), piece, preferred_element_type=F32) if split_rhs
                else jnp.dot(piece, b.astype(BF16), preferred_element_type=F32))
        acc = term if acc is None else acc + term
    return acc


def _mm_hi(a, b):
    return jnp.dot(a, b, preferred_element_type=F32, precision=HIGHEST)


def _silu(x):
    return x * jax.nn.sigmoid(x)


def _bf16_bits(x):
    b = lax.bitcast_convert_type(x, jnp.int32)
    return b + 0x7FFF + (lax.shift_right_logical(b, jnp.int32(16)) & 1)


_HIGH16 = -65536


def _pack_bf16_pairs(a, b):
    return lax.shift_right_logical(_bf16_bits(a), jnp.int32(16)) | (_bf16_bits(b) & _HIGH16)


def _unpack_bf16_pairs(p):
    a = lax.bitcast_convert_type(lax.shift_left(p, jnp.int32(16)), F32)
    b = lax.bitcast_convert_type(p & _HIGH16, F32)
    return a.astype(BF16), b.astype(BF16)


def _cond_row(i):
    return jnp.where(i < NC // TM, 0, 1 + (i - NC // TM) // (DEC_SEQ // TM))


def _modulated_norm(x, m_ref, g_ref, shift_idx, scale_idx):
    ms = jnp.mean(x * x, axis=-1, keepdims=True)
    y = x * lax.rsqrt(ms + EPS) * g_ref[...]
    return y * (1.0 + m_ref[scale_idx:scale_idx + 1, :]) + m_ref[shift_idx:shift_idx + 1, :]


def _ada_kernel(cond_ref, w_ref, b_ref, o_ref):
    o_ref[...] = _mm_hi(_silu(cond_ref[...]), w_ref[...]) + b_ref[...]


def _ada_params(cond8, mod_w, mod_b):
    tn = 1536
    return pl.pallas_call(
        _ada_kernel,
        grid=(DEPTH, 6 * D // tn),
        in_specs=[pl.BlockSpec((8, D), lambda l, j: (0, 0)),
                  pl.BlockSpec((None, D, tn), lambda l, j: (l, 0, j)),
                  pl.BlockSpec((None, 1, tn), lambda l, j: (l, 0, j))],
        out_specs=pl.BlockSpec((None, 8, tn), lambda l, j: (l, 0, j)),
        out_shape=jax.ShapeDtypeStruct((DEPTH, 8, 6 * D), F32),
        compiler_params=_cparams(("arbitrary", "arbitrary")),
        name="ada_params",
    )(cond8, mod_w, mod_b.reshape(DEPTH, 1, 6 * D))


def _token_specs(x):
    nct = NC // TM
    if isinstance(x, tuple):
        return ([pl.BlockSpec((TM, x[0].shape[1]), lambda i: (jnp.minimum(i, nct - 1), 0)),
                 pl.BlockSpec((TM, x[1].shape[1]), lambda i: (jnp.maximum(i - nct, 0), 0))], list(x))
    return [pl.BlockSpec((TM, x.shape[1]), lambda i: (i, 0))], [x]


def _token_rows(refs):
    if len(refs) == 1:
        return refs[0][...]
    return jnp.where(pl.program_id(0) < NC // TM, refs[0][...], refs[1][...])


def _modproj_kernel(*refs, n_chunk):
    m_ref, g_ref, w_ref, o_ref = refs[-4:]
    hb = _modulated_norm(_token_rows(refs[:-4]), m_ref, g_ref, 0, 1).astype(BF16)
    for j in range(o_ref.shape[1] // n_chunk):
        sl = slice(j * n_chunk, (j + 1) * n_chunk)
        o_ref[:, sl] = jnp.dot(hb, w_ref[:, sl], preferred_element_type=F32)


def _modproj(x, m3, gain, w_bf16, n_chunk):
    nout = w_bf16.shape[1]
    x_specs, x_args = _token_specs(x)
    return pl.pallas_call(
        functools.partial(_modproj_kernel, n_chunk=n_chunk),
        grid=(N // TM,),
        in_specs=x_specs + [pl.BlockSpec((None, 6, D), lambda i: (_cond_row(i), 0, 0)),
                            pl.BlockSpec((1, D), lambda i: (0, 0)),
                            pl.BlockSpec((D, nout), lambda i: (0, 0))],
        out_specs=pl.BlockSpec((TM, nout), lambda i: (i, 0)),
        out_shape=jax.ShapeDtypeStruct((N, nout), F32),
        compiler_params=_cparams(("arbitrary",)),
        name="modproj",
    )(*x_args, m3, gain, w_bf16)


def _deltanet_kernel(q_ref, k_ref, v_ref, z_ref, ab_ref, cq_ref, ck_ref, cv_ref, gp_ref, og_ref,
                     s0f_ref, s0b_ref, tri_ref, jbd_ref, o_ref, sf_ref, sb_ref,
                     u_s, w_s, qd_s, at_s, kt_s, ge_s, st_s, ob, *, seq_len, n_sub):
    hp = pl.program_id(1)
    C = CHUNK
    nc = seq_len // C
    nct = n_sub * nc
    P = LANES
    lane = lax.broadcasted_iota(jnp.int32, (C, P), 1)
    row = lax.broadcasted_iota(jnp.int32, (C, P), 0)
    first_head = lane < A_DH
    ri = lax.broadcasted_iota(jnp.int32, (P, P), 0)
    ci = lax.broadcasted_iota(jnp.int32, (P, P), 1)
    same_head = (ri < C) == (ci < C)
    same_blk = (ri // SOLVE_BLK) == (ci // SOLVE_BLK)
    eye = jnp.where(ri == ci, 1.0, 0.0)
    jbd = jbd_ref[...]
    lincl = tri_ref[...]
    cum_b = [lincl.T.astype(BF16), lincl.astype(BF16)]
    incl_m = [jnp.logical_and(same_head, ri >= ci), jnp.logical_and(same_head, ri <= ci)]
    strict_m = [jnp.logical_and(same_head, ri > ci), jnp.logical_and(same_head, ri < ci)]
    neg_a = -jnp.exp(gp_ref[0])
    dt_b = gp_ref[1]

    def conv_silu(ref, w_ref, c):
        base = pl.multiple_of(c * C, C)
        cs = c % nc
        xc = ref[pl.ds(base, C), :]
        pbase = pl.multiple_of(jnp.maximum(base - 8, 0), 8)
        nbase = pl.multiple_of(jnp.minimum(base + C, n_sub * seq_len - 8), 8)
        prev_row = ref[pl.ds(pbase, 8), :][7:8, :] * jnp.where(cs > 0, 1.0, 0.0)
        next_row = ref[pl.ds(nbase, 8), :][0:1, :] * jnp.where(cs < nc - 1, 1.0, 0.0)
        x_prev = jnp.where(row == 0, prev_row, pltpu.roll(xc, 1, 0))
        x_next = jnp.where(row == C - 1, next_row, pltpu.roll(xc, C - 1, 0))
        y = w_ref[0:1, :] * x_prev + w_ref[1:2, :] * xc + w_ref[2:3, :] * x_next
        return _silu(y)

    def stack(x):
        return jnp.concatenate([jnp.where(first_head, x, 0.0), jnp.where(first_head, 0.0, x)], axis=0)

    def chunk_inputs(c):
        base = pl.multiple_of(c * C, C)
        q = conv_silu(q_ref, cq_ref, c)
        k = conv_silu(k_ref, ck_ref, c)
        v = conv_silu(v_ref, cv_ref, c)
        return q, k, v, ab_ref[pl.ds(base, C), :].T

    sub8 = lax.broadcasted_iota(jnp.int32, (A_HEADS, C), 0)

    def pair_row(x8):
        r0 = jnp.sum(jnp.where(sub8 == 2 * hp, x8, 0.0), axis=0, keepdims=True)
        r1 = jnp.sum(jnp.where(sub8 == 2 * hp + 1, x8, 0.0), axis=0, keepdims=True)
        return jnp.concatenate([r0, r1], axis=1)

    def chain_gates(ab_t, d):
        a8 = ab_t[d * A_HEADS:(d + 1) * A_HEADS, :]
        b8 = ab_t[(2 + d) * A_HEADS:(3 + d) * A_HEADS, :]
        g8 = neg_a[d * A_HEADS:(d + 1) * A_HEADS, :C] * jax.nn.softplus(a8 + dt_b[d * A_HEADS:(d + 1) * A_HEADS, :C])
        gc8 = _mm_split(g8, cum_b[d], 3)
        tot8 = jnp.broadcast_to(jnp.sum(g8, axis=-1, keepdims=True), (A_HEADS, C))
        gc_row, beta_row, tot_row = pair_row(gc8), pair_row(jax.nn.sigmoid(b8)), pair_row(tot8)
        cols = jnp.concatenate([gc_row, beta_row, tot_row, jnp.zeros((5, P), F32)], axis=0).T
        return gc_row, tot_row, cols[:, 0:1], cols[:, 1:2], cols[:, 2:3]

    def terms_body(j, carry):
        cs = [TERM_UNROLL * j + t for t in range(TERM_UNROLL)]
        ins = [chunk_inputs(c) for c in cs]
        qsq = [_mm_split(x[0] * x[0], jbd, 2) for x in ins]
        ksq = [_mm_split(x[1] * x[1], jbd, 2) for x in ins]
        qs = [x[0] * lax.rsqrt(s + EPS) * (A_DH ** -0.5) for x, s in zip(ins, qsq)]
        ks = [x[1] * lax.rsqrt(s + EPS) for x, s in zip(ins, ksq)]
        qst = [stack(x) for x in qs]
        kst = [stack(x) for x in ks]
        vst = [stack(x[2]) for x in ins]
        kst_t = [x.T for x in kst]
        kk = [_mm_nt(x, x) for x in kst]
        qk = [_mm_nt(x, y) for x, y in zip(qst, kst)]
        chains = [(t, d) for t in range(TERM_UNROLL) for d in range(2)]
        gates = [chain_gates(ins[t][3], d) for t, d in chains]
        decay, e_gc = [], []
        for (t, d), (gc_row, tot_row, gc_col, beta_col, tot_col) in zip(chains, gates):
            diff = jnp.broadcast_to(gc_col, (P, P)) - jnp.broadcast_to(gc_row, (P, P))
            decay.append(jnp.where(incl_m[d], jnp.exp(jnp.where(incl_m[d], diff, 0.0)), 0.0))
            e_gc.append(jnp.exp(gc_col))
        a_mats = [jnp.where(strict_m[d], g[3] * kk[t] * dc, 0.0) for (t, d), g, dc in zip(chains, gates, decay)]
        rhs = [vst[t] * g[3] + pltpu.roll(kst[t] * (g[3] * e), A_DH, 1)
               for (t, d), g, e in zip(chains, gates, e_gc)]
        xs = _unit_lower_solve_many(a_mats, rhs, same_blk, eye)
        for (t, d), x, g, e, dc in zip(chains, xs, gates, e_gc, decay):
            c = cs[t]
            gc_row, tot_row = g[0], g[1]
            u_s[d, c] = jnp.where(same_head, x, 0.0).astype(BF16)
            w_s[d, c] = pltpu.roll(jnp.where(same_head, 0.0, x), A_DH, 1).astype(BF16)
            qd_s[d, c] = (qst[t] * e).astype(BF16)
            at_s[d, c] = jnp.where(incl_m[d], qk[t] * dc, 0.0).astype(BF16)
            kt_s[d, c] = (kst_t[t] * jnp.exp(tot_row - gc_row)).astype(BF16)
            ge_s[d, c] = jnp.broadcast_to(jnp.exp(tot_row), (8, P))
        return carry

    lax.fori_loop(0, nct // TERM_UNROLL, terms_body, 0)

    def block_diag(s2):
        z = jnp.zeros((A_DH, A_DH), F32)
        return jnp.concatenate([jnp.concatenate([s2[0], z], axis=1),
                                jnp.concatenate([z, s2[1]], axis=1)], axis=0)

    for s in range(n_sub):
        st_s[2 * s] = block_diag(s0f_ref[s])
        st_s[2 * s + 1] = block_diag(s0b_ref[s])

    def scan_body(i, carry):
        chains = [(s, d, s * nc + (i if d == 0 else nc - 1 - i)) for s in range(n_sub) for d in range(2)]
        dot = functools.partial(jnp.dot, preferred_element_type=F32)
        s_bd = [st_s[2 * s + d] for s, d, c in chains]
        sb16 = [x.astype(BF16) for x in s_bd]
        ws = [dot(w_s[d, c], sb) for (s, d, c), sb in zip(chains, sb16)]
        qs_ = [dot(qd_s[d, c], sb) for (s, d, c), sb in zip(chains, sb16)]
        vb = [(u_s[d, c].astype(F32) - x).astype(BF16) for (s, d, c), x in zip(chains, ws)]
        av = [dot(at_s[d, c], x) for (s, d, c), x in zip(chains, vb)]
        kv = [dot(kt_s[d, c], x) for (s, d, c), x in zip(chains, vb)]
        for (s, d, c), sb, q_, a_, k_ in zip(chains, s_bd, qs_, av, kv):
            st_s[2 * s + d] = sb * ge_s[d, c][0:1, :] + k_
            o_st = q_ + a_
            dst = o_ref if d == 0 else ob
            dst[pl.ds(pl.multiple_of(c * C, C), C), :] = o_st[:C] + o_st[C:]
        return carry

    lax.fori_loop(0, nc, scan_body, 0)

    for s in range(n_sub):
        for d, ref in ((0, sf_ref), (1, sb_ref)):
            s_bd = st_s[2 * s + d]
            ref[s, 0] = s_bd[:A_DH, :A_DH]
            ref[s, 1] = s_bd[A_DH:, A_DH:]

    def finish(j, carry):
        bases = [pl.multiple_of((TERM_UNROLL * j + t) * C, C) for t in range(TERM_UNROLL)]
        o = [o_ref[pl.ds(b, C), :] + ob[pl.ds(b, C), :] for b in bases]
        ms = [_mm_split(x * x, jbd, 2) * (1.0 / A_DH) for x in o]
        for b, x, m in zip(bases, o, ms):
            o_ref[pl.ds(b, C), :] = x * lax.rsqrt(m + EPS) * og_ref[...] * _silu(z_ref[pl.ds(b, C), :])
        return carry

    lax.fori_loop(0, nct // TERM_UNROLL, finish, 0)


def _deltanet(proj, conv_w, gate_p, o_gain2, s0f, s0b, tri, jbd, *, seq_len, n_seq, n_sub, row_blk0):
    rows = n_sub * seq_len
    nct = rows // CHUNK
    rb = lambda b: row_blk0 + b
    col = lambda off: (lambda b, hp: (rb(b), off + hp))
    st_spec = pl.BlockSpec((n_sub, 2, A_DH, A_DH), lambda b, hp: (b, hp, 0, 0))
    in_specs = [pl.BlockSpec((rows, LANES), col(0)),
                pl.BlockSpec((rows, LANES), col(4)),
                pl.BlockSpec((rows, LANES), col(8)),
                pl.BlockSpec((rows, LANES), col(12)),
                pl.BlockSpec((rows, LANES), lambda b, hp: (rb(b), 20)),
                pl.BlockSpec((3, LANES), lambda b, hp: (0, hp)),
                pl.BlockSpec((3, LANES), lambda b, hp: (0, 4 + hp)),
                pl.BlockSpec((3, LANES), lambda b, hp: (0, 8 + hp)),
                pl.BlockSpec((2, 2 * A_HEADS, LANES), lambda b, hp: (0, 0, 0)),
                pl.BlockSpec((1, LANES), lambda b, hp: (0, 0)),
                st_spec, st_spec,
                pl.BlockSpec((CHUNK, CHUNK), lambda b, hp: (0, 0)),
                pl.BlockSpec((LANES, LANES), lambda b, hp: (0, 0))]
    args = [proj, proj, proj, proj, proj, conv_w, conv_w, conv_w, gate_p, o_gain2, s0f, s0b, tri, jbd]
    st_shape = jax.ShapeDtypeStruct((n_seq, A_HEADS, A_DH, A_DH), F32)
    tile = lambda dt: pltpu.VMEM((2, nct, LANES, LANES), dt)
    return pl.pallas_call(
        functools.partial(_deltanet_kernel, seq_len=seq_len, n_sub=n_sub),
        grid=(n_seq // n_sub, A_HEADS // 2),
        in_specs=in_specs,
        out_specs=[pl.BlockSpec((rows, LANES), lambda b, hp: (b, hp)), st_spec, st_spec],
        out_shape=[jax.ShapeDtypeStruct((n_seq * seq_len, A_WIDTH), F32), st_shape, st_shape],
        scratch_shapes=[tile(BF16), tile(BF16), tile(BF16), tile(BF16), tile(BF16),
                        pltpu.VMEM((2, nct, 8, LANES), F32),
                        pltpu.VMEM((2 * n_sub, LANES, LANES), F32),
                        pltpu.VMEM((rows, LANES), F32)],
        compiler_params=_cparams(("arbitrary", "arbitrary")),
        name="deltanet",
    )(*args)


def _dft_mats(n):
    idx = np.arange(n)
    ang = 2.0 * np.pi * ((idx[:, None] * idx[None, :]) % n) / n
    return np.cos(ang), np.sin(ang)


def _fnet_ctx_kernel(u_ref, cs_ref, dft_ref, o_ref):
    norm = 1.0 / math.sqrt(SEQ * LANES)
    sls = [slice(g * LANES, (g + 1) * LANES) for g in range(B_GROUPS)]
    p = [_mm(u_ref[:, sl], cs_ref[...]) for sl in sls]
    stack = [jnp.concatenate([x[:, :LANES], x[:, LANES:]], axis=0) for x in p]
    y = [_mm(dft_ref[...], x) for x in stack]
    for sl, x in zip(sls, y):
        o_ref[:, sl] = x * norm


def _fnet_ctx(proj, cs, dft):
    return pl.pallas_call(
        _fnet_ctx_kernel,
        grid=(BATCH,),
        in_specs=[pl.BlockSpec((SEQ, B_WIDTH), lambda b: (b, 4)),
                  pl.BlockSpec(cs.shape, lambda b: (0, 0)),
                  pl.BlockSpec(dft.shape, lambda b: (0, 0))],
        out_specs=pl.BlockSpec((SEQ, B_WIDTH), lambda b: (b, 0)),
        out_shape=jax.ShapeDtypeStruct((NC, B_WIDTH), F32),
        compiler_params=_cparams(("arbitrary",)),
        name="fnet_ctx",
    )(proj, cs, dft)


FN_SUB = 4


def _fnet_lat1_kernel(u_ref, ca_ref, m1_ref, twc_ref, tws_ref, o_ref):
    r = 64
    units = [(j, g) for j in range(FN_SUB) for g in range(B_GROUPS)]
    rows = lambda j: slice(j * r, (j + 1) * r)
    pa = [_mm(u_ref[rows(j), g * LANES:(g + 1) * LANES], ca_ref[...]) for j, g in units]
    rhs = [jnp.concatenate([x[:, :2 * LANES], x[:, 2 * LANES:]], axis=0) for x in pa]
    zz = [_mm(m1_ref[...], x) for x in rhs]
    for (j, g), z in zip(units, zz):
        c = twc_ref[j]
        s = tws_ref[j]
        zr, zi = z[:, :LANES], z[:, LANES:]
        o_ref[rows(j), 2 * g * LANES:(2 * g + 1) * LANES] = (zr * c + zi * s).astype(BF16)
        o_ref[rows(j), (2 * g + 1) * LANES:(2 * g + 2) * LANES] = (zi * c - zr * s).astype(BF16)


def _fnet_lat2_kernel(z_ref, m1_ref, o_ref):
    r = 64
    norm = 1.0 / math.sqrt(DEC_SEQ * LANES)
    units = [(j, g) for j in range(FN_SUB) for g in range(B_GROUPS)]
    rows = lambda j: slice(j * r, (j + 1) * r)
    rhs = [jnp.concatenate([z_ref[rows(j), 2 * g * LANES:(2 * g + 1) * LANES],
                            z_ref[rows(j), (2 * g + 1) * LANES:(2 * g + 2) * LANES]], axis=0) for j, g in units]
    y = [_mm(m1_ref[...], x) for x in rhs]
    for (j, g), x in zip(units, y):
        o_ref[rows(j), g * LANES:(g + 1) * LANES] = x * norm


def _fnet_latent(proj, ca, m1, twc, tws):
    r = 64
    u = proj[NC:, 2048:2560].astype(BF16).reshape(DEC_BATCH, r, r, B_WIDTH)
    u = u.transpose(0, 2, 1, 3).reshape(DEC_BATCH * r * r, B_WIDTH)
    steps = DEC_BATCH * r // FN_SUB
    blk = FN_SUB * r
    z = pl.pallas_call(
        _fnet_lat1_kernel,
        grid=(steps,),
        in_specs=[pl.BlockSpec((blk, B_WIDTH), lambda s: (s, 0)),
                  pl.BlockSpec(ca.shape, lambda s: (0, 0)),
                  pl.BlockSpec(m1.shape, lambda s: (0, 0)),
                  pl.BlockSpec((FN_SUB, r, LANES), lambda s: (s % (r // FN_SUB), 0, 0)),
                  pl.BlockSpec((FN_SUB, r, LANES), lambda s: (s % (r // FN_SUB), 0, 0))],
        out_specs=pl.BlockSpec((blk, 2 * B_WIDTH), lambda s: (s, 0)),
        out_shape=jax.ShapeDtypeStruct((NL, 2 * B_WIDTH), BF16),
        compiler_params=_cparams(("arbitrary",)),
        name="fnet_lat1",
    )(u, ca, m1, twc, tws)
    z = z.reshape(DEC_BATCH, r, r, 2 * B_WIDTH).transpose(0, 2, 1, 3).reshape(NL, 2 * B_WIDTH)
    y = pl.pallas_call(
        _fnet_lat2_kernel,
        grid=(steps,),
        in_specs=[pl.BlockSpec((blk, 2 * B_WIDTH), lambda s: (s, 0)),
                  pl.BlockSpec(m1.shape, lambda s: (0, 0))],
        out_specs=pl.BlockSpec((blk, B_WIDTH), lambda s: (s, 0)),
        out_shape=jax.ShapeDtypeStruct((NL, B_WIDTH), F32),
        compiler_params=_cparams(("arbitrary",)),
        name="fnet_lat2",
    )(z, m1)
    return y.reshape(DEC_BATCH, r, r, B_WIDTH).transpose(0, 2, 1, 3).reshape(NL, B_WIDTH)


def _head_masks():
    lane = lax.broadcasted_iota(jnp.int32, (1, LANES), 1)
    return lane < NA_DH


def _attend_many(chains):
    s = [[_mm_nt(q, k) if b is None else _mm_nt(q, k) + b for k, v, b in kv] for q, kv in chains]
    m = [functools.reduce(jnp.maximum, [jnp.max(x, axis=-1, keepdims=True) for x in xs]) for xs in s]
    p = [[jnp.exp(x - mi) for x in xs] for xs, mi in zip(s, m)]
    l = [sum(jnp.sum(x, axis=-1, keepdims=True) for x in xs) for xs in p]
    o = [sum(_mm(x, v) for x, (k, v, b) in zip(xs, kv)) for xs, (q, kv) in zip(p, chains)]
    return [oi / li for oi, li in zip(o, l)]


def _na_ctx_kernel(q_ref, k_ref, v_ref, o_ref):
    first = _head_masks()
    n_split = 2
    rows = SEQ // n_split
    chains = []
    for p in range(NA_CTX_PAIRS):
        lanes = slice(p * LANES, (p + 1) * LANES)
        k = k_ref[:, lanes].astype(BF16)
        v = v_ref[:, lanes].astype(BF16)
        for j in range(n_split):
            q = q_ref[j * rows:(j + 1) * rows, lanes]
            for a in range(2):
                chains.append((jnp.where(first if a == 0 else jnp.logical_not(first), q * NA_SCALE, 0.0),
                               [(k, v, None)]))
    outs = _attend_many(chains)
    for p in range(NA_CTX_PAIRS):
        for j in range(n_split):
            i = 2 * (p * n_split + j)
            o_ref[j * rows:(j + 1) * rows, p * LANES:(p + 1) * LANES] = jnp.where(first, outs[i], outs[i + 1])


def _na_ctx(qkv):
    w = NA_CTX_PAIRS * LANES
    nblk = D // w
    return pl.pallas_call(
        _na_ctx_kernel,
        grid=(BATCH, nblk),
        in_specs=[pl.BlockSpec((SEQ, w), lambda b, hp: (b, hp)),
                  pl.BlockSpec((SEQ, w), lambda b, hp: (b, nblk + hp)),
                  pl.BlockSpec((SEQ, w), lambda b, hp: (b, 2 * nblk + hp))],
        out_specs=pl.BlockSpec((SEQ, w), lambda b, hp: (b, hp)),
        out_shape=jax.ShapeDtypeStruct((NC, D), F32),
        compiler_params=_cparams(("arbitrary", "arbitrary")),
        name="na_ctx",
    )(qkv, qkv, qkv)


def _na_lat_kernel(q_ref, k_ref, v_ref, kc_ref, vc_ref, tt_ref, o_ref):
    first = _head_masks()
    rows = DEC_SEQ // GRID_W
    kctx = kc_ref[...].astype(BF16)
    vctx = vc_ref[...].astype(BF16)
    nkeys = WIN_R * GRID_W

    def rows_body(j, carry):
        chains = []
        for t in range(NA_ROW_UNROLL):
            r = NA_ROW_UNROLL * j + t
            r0 = jnp.clip(r - WIN_R // 2, 0, rows - WIN_R)
            dr0 = r0 - r + (WIN_R - 1)
            q = q_ref[pl.ds(pl.multiple_of(r * GRID_W, GRID_W), GRID_W), :] * NA_SCALE
            kbase = pl.multiple_of(r0 * GRID_W, GRID_W)
            kl = k_ref[pl.ds(kbase, nkeys), :].astype(BF16)
            vl = v_ref[pl.ds(kbase, nkeys), :].astype(BF16)
            for a in range(2):
                qm = jnp.where(first if a == 0 else jnp.logical_not(first), q, 0.0)
                bias = jnp.concatenate([tt_ref[a, dr0 + 2 * i] for i in range(WIN_R // 2)], axis=1)
                chains.append((qm, [(kl, vl, bias), (kctx, vctx, None)]))
        outs = _attend_many(chains)
        for t in range(NA_ROW_UNROLL):
            r = NA_ROW_UNROLL * j + t
            o_ref[pl.ds(pl.multiple_of(r * GRID_W, GRID_W), GRID_W), :] = jnp.where(first, outs[2 * t], outs[2 * t + 1])
        return carry

    lax.fori_loop(0, rows // NA_ROW_UNROLL, rows_body, 0)


def _na_latent(qkv, cache_k2, cache_v2, tt2):
    rb0 = NC // DEC_SEQ
    return pl.pallas_call(
        _na_lat_kernel,
        grid=(DEC_BATCH, NA_HEADS // 2),
        in_specs=[pl.BlockSpec((DEC_SEQ, LANES), lambda b, hp: (rb0 + b, hp)),
                  pl.BlockSpec((DEC_SEQ, LANES), lambda b, hp: (rb0 + b, 8 + hp)),
                  pl.BlockSpec((DEC_SEQ, LANES), lambda b, hp: (rb0 + b, 16 + hp)),
                  pl.BlockSpec((None, 256, LANES), lambda b, hp: (b, 0, hp)),
                  pl.BlockSpec((None, 256, LANES), lambda b, hp: (b, 0, hp)),
                  pl.BlockSpec((2, 2 * WIN_R - 2, GRID_W, LANES), lambda b, hp: (hp, 0, 0, 0))],
        out_specs=pl.BlockSpec((DEC_SEQ, LANES), lambda b, hp: (b, hp)),
        out_shape=jax.ShapeDtypeStruct((NL, D), F32),
        compiler_params=_cparams(("arbitrary", "arbitrary")),
        name="na_latent",
    )(qkv, qkv, qkv, cache_k2, cache_v2, tt2)


def _rpb_tables(rpb):
    col = np.arange(GRID_W)
    start = np.clip(col - WIN_C // 2, 0, GRID_W - WIN_C)
    inside = (col[None, :] >= start[:, None]) & (col[None, :] < start[:, None] + WIN_C)
    w = GRID_W
    period = 2 * w - 1
    x = jnp.pad(rpb, ((0, 0), (0, 0), (w - WIN_C, w - WIN_C)))
    flat = jnp.tile(x, (1, 1, w))[:, :, w - 1:w - 1 + w * (period - 1)]
    t = flat.reshape(NA_HEADS, 2 * WIN_R - 1, w, period - 1)[..., :w]
    t = jnp.where(inside[None, None], t, NEG)
    return jnp.concatenate([t[:, :-1], t[:, 1:]], axis=-1)


ROUTE_SPLIT = 2
ROUTE_ROWS = 40


def _router_kernel(*refs, n_in):
    a_refs = refs[:2 * n_in]
    w_refs = refs[2 * n_in:3 * n_in]
    x_refs = refs[3 * n_in:-11]
    m_ref, g_ref, wr_ref, br_ref, tri_ref, xn_ref, xf_ref, ri_ref, rw_ref, cnt_ref, base_scr = refs[-11:]
    i = pl.program_id(0)

    @pl.when(i == 0)
    def _():
        base_scr[...] = jnp.zeros_like(base_scr)
        cnt_ref[...] = jnp.zeros_like(cnt_ref)

    rows = [slice(t * TM // ROUTE_SPLIT, (t + 1) * TM // ROUTE_SPLIT) for t in range(ROUTE_SPLIT)]
    is_ctx = pl.program_id(0) < NC // TM
    pick = lambda pair, r: jnp.where(is_ctx, pair[0][r, :], pair[1][r, :]) if len(pair) == 2 else pair[0][r, :]
    acc = [sum(_mm(pick(a_refs[2 * j:2 * j + 2], r), w_ref[...]) for j, w_ref in enumerate(w_refs)) for r in rows]
    x_new = [pick(x_refs, r) + m_ref[2:3, :] * a for r, a in zip(rows, acc)]
    for r, xn in zip(rows, x_new):
        xn_ref[r, :] = xn
    h = [_modulated_norm(xn, m_ref, g_ref, 3, 4) for xn in x_new]
    for r, hh in zip(rows, h):
        xf_ref[r, :] = _pack_bf16_pairs(hh[:, :D // 2], hh[:, D // 2:])
    logits = jnp.concatenate(
        [lax.dot_general(wr_ref[...], hh, (((1,), (1,)), ((), ())), preferred_element_type=F32, precision=HIGHEST)
         for hh in h], axis=1) + br_ref[:, 0:1]
    row = lax.broadcasted_iota(jnp.int32, logits.shape, 0)
    cmax = lambda x: jnp.max(x, axis=0, keepdims=True)
    cmin = lambda x: jnp.min(x, axis=0, keepdims=True)
    csum = lambda x: jnp.sum(x, axis=0, keepdims=True)

    gmask = row < N_GROUPS
    mg = cmax(jnp.where(gmask, logits, NEG))
    eg = jnp.where(gmask, jnp.exp(jnp.where(gmask, logits - mg, NEG)), 0.0)
    pg = eg / csum(eg)
    p_grp = cmax(pg)
    grp = cmin(jnp.where(jnp.logical_and(gmask, pg == p_grp), row, ROUTE_ROWS))
    lo = N_GROUPS + grp * EXP_PER_GROUP
    emask = jnp.logical_and(row >= lo, row < lo + EXP_PER_GROUP)
    me = cmax(jnp.where(emask, logits, NEG))
    ee = jnp.where(emask, jnp.exp(jnp.where(emask, logits - me, NEG)), 0.0)
    pe = ee / csum(ee)
    p1 = cmax(pe)
    i1 = cmin(jnp.where(jnp.logical_and(emask, pe == p1), row, ROUTE_ROWS))
    m2 = jnp.logical_and(emask, row != i1)
    p2 = cmax(jnp.where(m2, pe, -1.0))
    i2 = cmin(jnp.where(jnp.logical_and(m2, pe == p2), row, ROUTE_ROWS))
    den = p1 + p2
    w1 = p_grp * p1 / den
    w2 = p_grp * p2 / den

    sel1 = row == i1
    sel2 = row == i2
    oh = jnp.where(jnp.logical_or(sel1, sel2), 1.0, 0.0).astype(BF16)
    before = jnp.dot(oh, tri_ref[...], preferred_element_type=F32) + base_scr[:, 0:1]
    rank1 = csum(jnp.where(sel1, before, 0.0))
    rank2 = csum(jnp.where(sel2, before, 0.0))
    base_scr[...] = base_scr[...] + jnp.sum(oh.astype(F32), axis=1, keepdims=True)
    cnt_ref[...] = cnt_ref[...] + lax.dot_general(jnp.ones((8, TM), BF16), oh, (((1,), (1,)), ((), ())),
                                                  preferred_element_type=F32)
    sub = lax.broadcasted_iota(jnp.int32, (8, TM), 0)
    ri_ref[...] = jnp.where(sub == 0, i1 - N_GROUPS, jnp.where(sub == 1, i2 - N_GROUPS,
                  jnp.where(sub == 2, rank1.astype(jnp.int32), jnp.where(sub == 3, rank2.astype(jnp.int32), 0))))
    rw_ref[...] = jnp.where(sub == 0, w1, jnp.where(sub == 1, w2, 0.0))


def _router(a_pairs, w_list, x, m3, gain, wr_t, br_t, tri_upper):
    in_specs, args = [], []
    for pair in a_pairs:
        specs, ops = _token_specs(pair)
        in_specs += specs
        args += ops
    x_specs, x_args = _token_specs(x)
    in_specs += ([pl.BlockSpec(w.shape, lambda i: (0, 0)) for w in w_list] + x_specs
                 + [pl.BlockSpec((None, 6, D), lambda i: (_cond_row(i), 0, 0)),
                    pl.BlockSpec((1, D), lambda i: (0, 0)),
                    pl.BlockSpec((ROUTE_ROWS, D), lambda i: (0, 0)),
                    pl.BlockSpec((ROUTE_ROWS, LANES), lambda i: (0, 0)),
                    pl.BlockSpec((TM, TM), lambda i: (0, 0))])
    return pl.pallas_call(
        functools.partial(_router_kernel, n_in=len(a_pairs)),
        grid=(N // TM,),
        in_specs=in_specs,
        out_specs=[pl.BlockSpec((TM, D), lambda i: (i, 0)),
                   pl.BlockSpec((TM, D // 2), lambda i: (i, 0)),
                   pl.BlockSpec((8, TM), lambda i: (0, i)),
                   pl.BlockSpec((8, TM), lambda i: (0, i)),
                   pl.BlockSpec((8, ROUTE_ROWS), lambda i: (0, 0))],
        out_shape=[jax.ShapeDtypeStruct((N, D), F32),
                   jax.ShapeDtypeStruct((N, D // 2), jnp.int32),
                   jax.ShapeDtypeStruct((8, N), jnp.int32),
                   jax.ShapeDtypeStruct((8, N), F32),
                   jax.ShapeDtypeStruct((8, ROUTE_ROWS), F32)],
        scratch_shapes=[pltpu.VMEM((ROUTE_ROWS, LANES), F32)],
        compiler_params=_cparams(("arbitrary",)),
        name="router",
    )(*args, *w_list, *x_args, m3, gain, wr_t, br_t, tri_upper)


def _row_copy(src_hbm, row, dst, r, sem):
    return pltpu.make_async_copy(src_hbm.at[pl.ds(row, 1)], dst.at[pl.ds(r, 1)], sem)


def _expert_kernel(be_ref, nu_ref, d_ref, pad_ref, xp_hbm, wg_ref, wu_ref, wd_ref, ys_ref, xres, xbuf, st_ref, sem):
    del be_ref
    b = pl.program_id(0)
    half = D // 2

    def gather(blk, slot, part=0, parts=1):
        base = blk * MOE_ROWS
        for r in range(part * MOE_ROWS // parts, (part + 1) * MOE_ROWS // parts):
            xbuf[slot, pl.ds(r, 1), :] = xres[pl.ds(st_ref[base + r], 1), :]

    @pl.when(b == 0)
    def _():
        cp = pltpu.make_async_copy(xp_hbm, xres, sem.at[0])
        cp.start()

        def clear_tail(e, c):
            def clear(s_, c2):
                st_ref[s_] = 0
                return c2

            return lax.fori_loop(pad_ref[2 * e], pad_ref[2 * e + 1], clear, c)

        lax.fori_loop(0, N_EXPERTS + 1, clear_tail, 0)

        def invert(j, c):
            slots = [d_ref[16 * j + t] for t in range(16)]
            for t in range(16):
                st_ref[slots[t]] = 8 * j + t // 2
            return c

        lax.fori_loop(0, 2 * N // 16, invert, 0)
        cp.wait()
        gather(0, 0)

    @pl.when(b < nu_ref[0])
    def _():
        nxt = (jnp.minimum(b + 1, N_SLOT_BLOCKS - 1), (b + 1) % 2)
        x_lo, x_hi = _unpack_bf16_pairs(xbuf[b % 2])
        dot = functools.partial(jnp.dot, preferred_element_type=F32)
        gather(*nxt, 0, 8)
        g = dot(x_lo, wg_ref[:half, :].astype(BF16))
        gather(*nxt, 1, 8)
        g = g + dot(x_hi, wg_ref[half:, :].astype(BF16))
        gather(*nxt, 2, 8)
        u = dot(x_lo, wu_ref[:half, :].astype(BF16))
        gather(*nxt, 3, 8)
        u = u + dot(x_hi, wu_ref[half:, :].astype(BF16))
        hb = (_silu(g) * u).astype(BF16)
        quarter = D // 4
        for j in range(4):
            gather(*nxt, 4 + j, 8)
            cols = slice(j * quarter, (j + 1) * quarter)
            ys_ref[:, cols] = jnp.dot(hb, wd_ref[:, cols].astype(BF16), preferred_element_type=F32)

    @pl.when(b >= nu_ref[0])
    def _():
        ys_ref[...] = jnp.zeros_like(ys_ref)


def _experts(block_e, n_used, dest, pad_ranges, xp, w_gate, w_up, w_down, layer):
    grid_spec = pltpu.PrefetchScalarGridSpec(
        num_scalar_prefetch=4,
        grid=(N_SLOT_BLOCKS,),
        in_specs=[pl.BlockSpec(memory_space=pl.ANY),
                  pl.BlockSpec((None, None, D, D_EXPERT), lambda b, be, nu, st, pd: (layer, be[b], 0, 0)),
                  pl.BlockSpec((None, None, D, D_EXPERT), lambda b, be, nu, st, pd: (layer, be[b], 0, 0)),
                  pl.BlockSpec((None, None, D_EXPERT, D), lambda b, be, nu, st, pd: (layer, be[b], 0, 0))],
        out_specs=pl.BlockSpec((MOE_ROWS, D), lambda b, be, nu, st, pd: (b, 0)),
        scratch_shapes=[pltpu.VMEM((N, D // 2), jnp.int32),
                        pltpu.VMEM((2, MOE_ROWS, D // 2), jnp.int32),
                        pltpu.SMEM((N_SLOT_BLOCKS * MOE_ROWS,), jnp.int32),
                        pltpu.SemaphoreType.DMA((1,))])
    return pl.pallas_call(
        _expert_kernel,
        grid_spec=grid_spec,
        out_shape=jax.ShapeDtypeStruct((N_SLOT_BLOCKS * MOE_ROWS, D), F32),
        compiler_params=pltpu.CompilerParams(dimension_semantics=("arbitrary",), vmem_limit_bytes=EXPERT_VMEM_LIMIT),
        name="experts",
    )(block_e, n_used, dest, pad_ranges, xp, w_gate, w_up, w_down)


def _combine_kernel(d_ref, ys_hbm, x_ref, m_ref, rw_ref, fn_ref, o_ref, buf, sem, *, final, tile0, n_tiles):
    i = pl.program_id(0)

    def issue(tile, slot):
        base = (tile0 + tile) * (2 * TM)
        for r in range(TM):
            for kk in range(2):
                _row_copy(ys_hbm, d_ref[base + 2 * r + kk], buf.at[slot, kk], r, sem.at[slot]).start()

    @pl.when(i == 0)
    def _():
        issue(0, 0)

    @pl.when(i + 1 < n_tiles)
    def _():
        issue(i + 1, (i + 1) % 2)

    slot = i % 2
    for kk in range(2):
        pltpu.make_async_copy(ys_hbm.at[pl.ds(0, TM)], buf.at[slot, kk], sem.at[slot]).wait()
    w = rw_ref[...]
    y = w[:, 0:1] * buf[slot, 0] + w[:, 1:2] * buf[slot, 1]
    out = x_ref[...] + m_ref[5:6, :] * y
    if final:
        ms = jnp.mean(out * out, axis=-1, keepdims=True)
        out = out * lax.rsqrt(ms + EPS) * fn_ref[...]
    o_ref[...] = out


def _combine(dest_flat, ys, x, m3, rw, final_norm, final, tile0=0, n_tiles=N // TM):
    grid_spec = pltpu.PrefetchScalarGridSpec(
        num_scalar_prefetch=1,
        grid=(n_tiles,),
        in_specs=[pl.BlockSpec(memory_space=pl.ANY),
                  pl.BlockSpec((TM, D), lambda i, d: (tile0 + i, 0)),
                  pl.BlockSpec((None, 6, D), lambda i, d: (_cond_row(tile0 + i), 0, 0)),
                  pl.BlockSpec((TM, 2), lambda i, d: (tile0 + i, 0)),
                  pl.BlockSpec((1, D), lambda i, d: (0, 0))],
        out_specs=pl.BlockSpec((TM, D), lambda i, d: (i, 0)),
        scratch_shapes=[pltpu.VMEM((2, 2, TM, D), F32), pltpu.SemaphoreType.DMA((2,))])
    return pl.pallas_call(
        functools.partial(_combine_kernel, final=final, tile0=tile0, n_tiles=n_tiles),
        grid_spec=grid_spec,
        out_shape=jax.ShapeDtypeStruct((n_tiles * TM, D), F32),
        compiler_params=_cparams(("arbitrary",)),
        name="combine",
    )(dest_flat, ys, x, m3, rw, final_norm)


def _combine_proj_kernel(d_ref, ys_hbm, x_ref, m_ref, rw_ref, mn_ref, gn_ref, w_ref, xo_ref, o_ref, buf, sem, *,
                         n_chunk):
    i = pl.program_id(0)
    n_tiles = N // TM
    n_parts = o_ref.shape[1] // n_chunk

    def issue(tile, slot, part=0, parts=1):
        base = tile * (2 * TM)
        for r in range(part * TM // parts, (part + 1) * TM // parts):
            for kk in range(2):
                _row_copy(ys_hbm, d_ref[base + 2 * r + kk], buf.at[slot, kk], r, sem.at[slot]).start()

    def wait(slot):
        for kk in range(2):
            pltpu.make_async_copy(ys_hbm.at[pl.ds(0, TM)], buf.at[slot, kk], sem.at[slot]).wait()

    @pl.when(i == 0)
    def _():
        issue(0, 0)

    slot = i % 2
    wait(slot)
    w = rw_ref[...]
    x_new = x_ref[...] + m_ref[5:6, :] * (w[:, 0:1] * buf[slot, 0] + w[:, 1:2] * buf[slot, 1])
    xo_ref[...] = x_new
    hb = _modulated_norm(x_new, mn_ref, gn_ref, 0, 1).astype(BF16)
    nxt = jnp.minimum(i + 1, n_tiles - 1)
    for j in range(n_parts):
        issue(nxt, 1 - slot, j, n_parts)
        sl = slice(j * n_chunk, (j + 1) * n_chunk)
        o_ref[:, sl] = jnp.dot(hb, w_ref[:, sl], preferred_element_type=F32)

    @pl.when(i == n_tiles - 1)
    def _():
        wait(1 - slot)


def _combine_proj(dest_flat, ys, x, m3, rw, m3_next, gain_next, w_bf16, n_chunk):
    nout = w_bf16.shape[1]
    grid_spec = pltpu.PrefetchScalarGridSpec(
        num_scalar_prefetch=1,
        grid=(N // TM,),
        in_specs=[pl.BlockSpec(memory_space=pl.ANY),
                  pl.BlockSpec((TM, D), lambda i, d: (i, 0)),
                  pl.BlockSpec((None, 6, D), lambda i, d: (_cond_row(i), 0, 0)),
                  pl.BlockSpec((TM, 2), lambda i, d: (i, 0)),
                  pl.BlockSpec((None, 6, D), lambda i, d: (_cond_row(i), 0, 0)),
                  pl.BlockSpec((1, D), lambda i, d: (0, 0)),
                  pl.BlockSpec((D, nout), lambda i, d: (0, 0))],
        out_specs=[pl.BlockSpec((TM, D), lambda i, d: (i, 0)),
                   pl.BlockSpec((TM, nout), lambda i, d: (i, 0))],
        scratch_shapes=[pltpu.VMEM((2, 2, TM, D), F32), pltpu.SemaphoreType.DMA((2,))])
    return pl.pallas_call(
        functools.partial(_combine_proj_kernel, n_chunk=n_chunk),
        grid_spec=grid_spec,
        out_shape=[jax.ShapeDtypeStruct((N, D), F32), jax.ShapeDtypeStruct((N, nout), F32)],
        compiler_params=_cparams(("arbitrary",)),
        name="combine_proj",
    )(dest_flat, ys, x, m3, rw, m3_next, gain_next, w_bf16)


def _hier_moe(a_pairs, w_list, x, m3, gain, layer, w_rg, b_rg, w_re, b_re, w_gate, w_up, w_down, tri_tm,
              final_norm, final, next_proj=None):
    pad = ROUTE_ROWS - N_GROUPS - N_EXPERTS
    wr_t = jnp.concatenate([w_rg.T, w_re.transpose(0, 2, 1).reshape(N_EXPERTS, D), jnp.zeros((pad, D), F32)], axis=0)
    br_t = jnp.broadcast_to(jnp.concatenate([b_rg, b_re.reshape(N_EXPERTS), jnp.zeros((pad,), F32)])[:, None],
                            (ROUTE_ROWS, LANES))
    x, xf, ri_t, rw_t, cnt = _router(a_pairs, w_list, x, m3, gain, wr_t, br_t, tri_tm)
    e_idx = ri_t[0:2].T
    rank = ri_t[2:4].T
    rw = rw_t[0:2].T
    counts = cnt[0, N_GROUPS:N_GROUPS + N_EXPERTS].astype(jnp.int32)
    padded = (counts + MOE_ROWS - 1) // MOE_ROWS * MOE_ROWS
    end_pad = jnp.cumsum(padded)
    start_pad = end_pad - padded
    experts = jnp.arange(N_EXPERTS, dtype=jnp.int32)
    start_of = jnp.sum(jnp.where(e_idx[:, :, None] == experts, start_pad, 0), axis=-1)
    dest = (start_of + rank).reshape(-1).astype(jnp.int32)
    block_start = jnp.arange(N_SLOT_BLOCKS, dtype=jnp.int32) * MOE_ROWS
    block_e = jnp.minimum(jnp.sum((end_pad[None, :] <= block_start[:, None]).astype(jnp.int32), axis=1),
                          N_EXPERTS - 1)
    n_used = (end_pad[-1:] // MOE_ROWS).astype(jnp.int32)
    tail = jnp.stack([end_pad[-1], jnp.minimum(end_pad[-1] + MOE_ROWS, N_SLOT_BLOCKS * MOE_ROWS)])
    pad_ranges = jnp.concatenate([jnp.stack([start_pad + counts, end_pad], axis=1).reshape(-1), tail]).astype(jnp.int32)
    ys = _experts(block_e, n_used, dest, pad_ranges, xf, w_gate, w_up, w_down, layer)
    if not final:
        return _combine_proj(dest, ys, x, m3, rw, *next_proj)
    nct = NC // TM
    return (_combine(dest, ys, x, m3, rw, final_norm, True, 0, nct),
            _combine(dest, ys, x, m3, rw, final_norm, True, nct, N // TM - nct))


def kernel(x_prompt, x_sample, state_A_fwd, state_A_bwd, cache_k, cache_v, c, c_ctx, mod_w, mod_b, norm_mix, norm_ffn, ab_w_in, ab_conv, ab_a_log, ab_dt_bias, ab_o_gain, ab_w_out, na_w_qkv, na_rpb, na_w_out, moe_w_rg, moe_b_rg, moe_w_re, moe_b_re, moe_w_gate, moe_w_up, moe_w_down, final_norm):
    x = (x_prompt.reshape(NC, D), x_sample.reshape(NL, D))
    cond8 = jnp.concatenate([c_ctx[None, :], c, jnp.zeros((8 - 1 - DEC_BATCH, D), F32)], axis=0)
    mods = _ada_params(cond8, mod_w, mod_b).reshape(DEPTH, 8, 6, D)

    tri_tm = jnp.asarray(np.triu(np.ones((TM, TM)), 1), BF16)
    tri_c = jnp.asarray(np.tril(np.ones((CHUNK, CHUNK))), F32)
    half = np.arange(LANES) < A_DH
    jbd = jnp.asarray((half[:, None] == half[None, :]).astype(np.float32))
    fn = final_norm[None, :]

    m3 = mods[0]
    w_in = ab_w_in[0]
    w_in = jnp.concatenate([w_in[:, :2048], w_in[:, 2080:2592], w_in[:, 2048:2080],
                            jnp.zeros((D, AB_COLS - 2592), F32)], axis=1).astype(BF16)
    proj = _modproj(x, m3, norm_mix[0][None, :], w_in, 896)
    gate_p = jnp.broadcast_to(jnp.stack([ab_a_log[0].reshape(-1), ab_dt_bias[0].reshape(-1)])[:, :, None],
                              (2, 2 * A_HEADS, LANES))
    o_gain2 = jnp.tile(ab_o_gain[0], 2)[None, :]
    zeros_state = jnp.zeros((BATCH, A_HEADS, A_DH, A_DH), F32)
    mix_a_c, s_f, s_b = _deltanet(proj, ab_conv[0], gate_p, o_gain2, zeros_state, zeros_state, tri_c, jbd,
                                  seq_len=SEQ, n_seq=BATCH, n_sub=4, row_blk0=0)
    mix_a_l, _, _ = _deltanet(proj, ab_conv[0], gate_p, o_gain2, state_A_fwd[:, 0], state_A_bwd[:, 0], tri_c, jbd,
                              seq_len=DEC_SEQ, n_seq=DEC_BATCH, n_sub=1, row_blk0=NC // DEC_SEQ)

    cc, sc = _dft_mats(LANES)
    ct, st = _dft_mats(SEQ)
    c64, s64 = _dft_mats(64)
    cs = jnp.asarray(np.concatenate([cc, sc], axis=1), BF16)
    dft = jnp.asarray(np.concatenate([ct, -st], axis=1), BF16)
    ca = jnp.asarray(np.concatenate([cc, -sc, -sc, -cc], axis=1), BF16)
    m1 = jnp.asarray(np.concatenate([c64, s64], axis=1), BF16)
    tw_idx = np.arange(64)
    tw_ang = 2.0 * np.pi * (tw_idx[:, None] * tw_idx[None, :]) / DEC_SEQ
    twc = jnp.broadcast_to(jnp.asarray(np.cos(tw_ang), F32)[:, :, None], (64, 64, LANES))
    tws = jnp.broadcast_to(jnp.asarray(np.sin(tw_ang), F32)[:, :, None], (64, 64, LANES))
    mix_b_c = _fnet_ctx(proj, cs, dft)
    mix_b_l = _fnet_latent(proj, ca, m1, twc, tws)

    w_out = ab_w_out[0].astype(BF16)
    x, qkv = _hier_moe([(mix_a_c, mix_a_l), (mix_b_c, mix_b_l)], [w_out[:A_WIDTH], w_out[A_WIDTH:]], x, m3,
                       norm_ffn[0][None, :], 0, moe_w_rg[0], moe_b_rg[0], moe_w_re[0], moe_b_re[0],
                       moe_w_gate, moe_w_up, moe_w_down, tri_tm, fn, False,
                       next_proj=(mods[1], norm_mix[1][None, :], na_w_qkv[0].astype(BF16), 512))

    m3 = mods[1]
    attn_c = _na_ctx(qkv)
    attn_l = _na_latent(qkv, cache_k[:, 0].reshape(DEC_BATCH, 256, D), cache_v[:, 0].reshape(DEC_BATCH, 256, D),
                        _rpb_tables(na_rpb[0]))
    y_c, y_l = _hier_moe([(attn_c, attn_l)], [na_w_out[0].astype(BF16)], x, m3, norm_ffn[1][None, :], 1, moe_w_rg[1], moe_b_rg[1], moe_w_re[1], moe_b_re[1],
                         moe_w_gate, moe_w_up, moe_w_down, tri_tm, fn, True)

    new_k = qkv[:NC, D:2 * D].reshape(BATCH, 1, SEQ, NA_HEADS, NA_DH)
    new_v = qkv[:NC, 2 * D:].reshape(BATCH, 1, SEQ, NA_HEADS, NA_DH)
    return (y_c.reshape(BATCH, SEQ, D), y_l.reshape(DEC_BATCH, DEC_SEQ, D),
            s_f[:, None], s_b[:, None], new_k, new_v)
```

```python
import functools
import math

import numpy as np
import jax
import jax.numpy as jnp
from jax import lax
from jax.experimental import pallas as pl
from jax.experimental.pallas import tpu as pltpu

F32 = jnp.float32
BF16 = jnp.bfloat16
HIGHEST = lax.Precision.HIGHEST

D = 1024
BATCH, SEQ = 32, 256
DEC_BATCH, DEC_SEQ = 2, 4096
NC = BATCH * SEQ
NL = DEC_BATCH * DEC_SEQ
N = NC + NL
DEPTH = 2
GRID_W = 64
A_DH = 64
A_HEADS = 8
A_WIDTH = 512
CHUNK = 64
B_WIDTH = 512
B_GROUPS = 4
NA_DH = 64
NA_HEADS = 16
WIN_R, WIN_C = 8, 16
N_GROUPS, EXP_PER_GROUP, N_EXPERTS = 4, 8, 32
D_EXPERT = 512
EPS = 1e-6

LANES = 128
TM = 256
MOE_ROWS = 256
N_SLOT_BLOCKS = (2 * N) // MOE_ROWS + N_EXPERTS
AB_COLS = 2688
VMEM_LIMIT = 56 * 1024 * 1024
EXPERT_VMEM_LIMIT = 60 * 1024 * 1024
NEG = -1e30


def _cparams(sem):
    return pltpu.CompilerParams(dimension_semantics=sem, vmem_limit_bytes=VMEM_LIMIT)


def _mm(a, b):
    return jnp.dot(a.astype(BF16), b.astype(BF16), preferred_element_type=F32)


def _mm_nt(a, b):
    return lax.dot_general(a.astype(BF16), b.astype(BF16), (((1,), (1,)), ((), ())),
                           preferred_element_type=F32)


SOLVE_BLK = 16
NA_SCALE = NA_DH ** -0.5
NA_CTX_PAIRS = 4
NA_ROW_UNROLL = 8
TERM_UNROLL = 4


def _unit_lower_solve(a_mat, rhs, same_blk, eye):
    dg = jnp.where(same_blk, a_mat, 0.0)
    p = -dg
    dinv = eye + p
    for _ in range(int(math.log2(SOLVE_BLK)) - 1):
        p = _mm(p, p)
        dinv = dinv + _mm(p, dinv)
    mp = -_mm(dinv, a_mat - dg)
    y = _mm(dinv, rhs)
    y = y + _mm(mp, y)
    for _ in range(int(math.log2(a_mat.shape[0] // SOLVE_BLK)) - 1):
        mp = _mm(mp, mp)
        y = y + _mm(mp, y)
    return y


def _unit_lower_solve_many(a_mats, rhs, same_blk, eye):
    off = [jnp.where(same_blk, 0.0, a).astype(BF16) for a in a_mats]
    p = [jnp.where(same_blk, -a, 0.0).astype(BF16) for a in a_mats]
    dinv = [(eye + x.astype(F32)).astype(BF16) for x in p]
    for _ in range(int(math.log2(SOLVE_BLK)) - 1):
        p = [_mm(x, x).astype(BF16) for x in p]
        dinv = [(di.astype(F32) + _mm(x, di)).astype(BF16) for x, di in zip(p, dinv)]
    mp = [(-_mm(di, o)).astype(BF16) for di, o in zip(dinv, off)]
    y = [_mm(di, r) for di, r in zip(dinv, rhs)]
    y = [yi + _mm(m, yi) for m, yi in zip(mp, y)]
    for _ in range(int(math.log2(a_mats[0].shape[0] // SOLVE_BLK)) - 1):
        mp = [_mm(m, m).astype(BF16) for m in mp]
        y = [yi + _mm(m, yi) for m, yi in zip(mp, y)]
    return y


def _mm_split(a, b, parts, split_rhs=False):
    x = b if split_rhs else a
    acc = None
    for _ in range(parts):
        piece = x.astype(BF16)
        x = x - piece.astype(F32)
        term = (jnp.dot(a.astype(BF16), piece, preferred_element_type=F32) if split_rhs
                else jnp.dot(piece, b.astype(BF16), preferred_element_type=F32))
        acc = term if acc is None else acc + term
    return acc


def _mm_hi(a, b):
    return jnp.dot(a, b, preferred_element_type=F32, precision=HIGHEST)


def _silu(x):
    return x * jax.nn.sigmoid(x)


def _bf16_bits(x):
    b = lax.bitcast_convert_type(x, jnp.int32)
    return b + 0x7FFF + (lax.shift_right_logical(b, jnp.int32(16)) & 1)


_HIGH16 = -65536


def _pack_bf16_pairs(a, b):
    return lax.shift_right_logical(_bf16_bits(a), jnp.int32(16)) | (_bf16_bits(b) & _HIGH16)


def _unpack_bf16_pairs(p):
    a = lax.bitcast_convert_type(lax.shift_left(p, jnp.int32(16)), F32)
    b = lax.bitcast_convert_type(p & _HIGH16, F32)
    return a.astype(BF16), b.astype(BF16)


def _cond_row(i):
    return jnp.where(i < NC // TM, 0, 1 + (i - NC // TM) // (DEC_SEQ // TM))


def _modulated_norm(x, m_ref, g_ref, shift_idx, scale_idx):
    ms = jnp.mean(x * x, axis=-1, keepdims=True)
    y = x * lax.rsqrt(ms + EPS) * g_ref[...]
    return y * (1.0 + m_ref[scale_idx:scale_idx + 1, :]) + m_ref[shift_idx:shift_idx + 1, :]


def _ada_kernel(cond_ref, w_ref, b_ref, o_ref):
    o_ref[...] = _mm_hi(_silu(cond_ref[...]), w_ref[...]) + b_ref[...]


def _ada_params(cond8, mod_w, mod_b):
    tn = 1536
    return pl.pallas_call(
        _ada_kernel,
        grid=(DEPTH, 6 * D // tn),
        in_specs=[pl.BlockSpec((8, D), lambda l, j: (0, 0)),
                  pl.BlockSpec((None, D, tn), lambda l, j: (l, 0, j)),
                  pl.BlockSpec((None, 1, tn), lambda l, j: (l, 0, j))],
        out_specs=pl.BlockSpec((None, 8, tn), lambda l, j: (l, 0, j)),
        out_shape=jax.ShapeDtypeStruct((DEPTH, 8, 6 * D), F32),
        compiler_params=_cparams(("arbitrary", "arbitrary")),
        name="ada_params",
    )(cond8, mod_w, mod_b.reshape(DEPTH, 1, 6 * D))


def _token_specs(x):
    nct = NC // TM
    if isinstance(x, tuple):
        return ([pl.BlockSpec((TM, x[0].shape[1]), lambda i: (jnp.minimum(i, nct - 1), 0)),
                 pl.BlockSpec((TM, x[1].shape[1]), lambda i: (jnp.maximum(i - nct, 0), 0))], list(x))
    return [pl.BlockSpec((TM, x.shape[1]), lambda i: (i, 0))], [x]


def _token_rows(refs):
    if len(refs) == 1:
        return refs[0][...]
    return jnp.where(pl.program_id(0) < NC // TM, refs[0][...], refs[1][...])


def _modproj_kernel(*refs, n_chunk):
    m_ref, g_ref, w_ref, o_ref = refs[-4:]
    hb = _modulated_norm(_token_rows(refs[:-4]), m_ref, g_ref, 0, 1).astype(BF16)
    for j in range(o_ref.shape[1] // n_chunk):
        sl = slice(j * n_chunk, (j + 1) * n_chunk)
        o_ref[:, sl] = jnp.dot(hb, w_ref[:, sl], preferred_element_type=F32)


def _modproj(x, m3, gain, w_bf16, n_chunk):
    nout = w_bf16.shape[1]
    x_specs, x_args = _token_specs(x)
    return pl.pallas_call(
        functools.partial(_modproj_kernel, n_chunk=n_chunk),
        grid=(N // TM,),
        in_specs=x_specs + [pl.BlockSpec((None, 6, D), lambda i: (_cond_row(i), 0, 0)),
                            pl.BlockSpec((1, D), lambda i: (0, 0)),
                            pl.BlockSpec((D, nout), lambda i: (0, 0))],
        out_specs=pl.BlockSpec((TM, nout), lambda i: (i, 0)),
        out_shape=jax.ShapeDtypeStruct((N, nout), F32),
        compiler_params=_cparams(("arbitrary",)),
        name="modproj",
    )(*x_args, m3, gain, w_bf16)


def _deltanet_kernel(q_ref, k_ref, v_ref, z_ref, ab_ref, cq_ref, ck_ref, cv_ref, gp_ref, og_ref,
                     s0f_ref, s0b_ref, tri_ref, jbd_ref, o_ref, sf_ref, sb_ref,
                     u_s, w_s, qd_s, at_s, kt_s, ge_s, st_s, ob, *, seq_len, n_sub):
    hp = pl.program_id(1)
    C = CHUNK
    nc = seq_len // C
    nct = n_sub * nc
    P = LANES
    lane = lax.broadcasted_iota(jnp.int32, (C, P), 1)
    row = lax.broadcasted_iota(jnp.int32, (C, P), 0)
    first_head = lane < A_DH
    ri = lax.broadcasted_iota(jnp.int32, (P, P), 0)
    ci = lax.broadcasted_iota(jnp.int32, (P, P), 1)
    same_head = (ri < C) == (ci < C)
    same_blk = (ri // SOLVE_BLK) == (ci // SOLVE_BLK)
    eye = jnp.where(ri == ci, 1.0, 0.0)
    jbd = jbd_ref[...]
    lincl = tri_ref[...]
    cum_b = [lincl.T.astype(BF16), lincl.astype(BF16)]
    incl_m = [jnp.logical_and(same_head, ri >= ci), jnp.logical_and(same_head, ri <= ci)]
    strict_m = [jnp.logical_and(same_head, ri > ci), jnp.logical_and(same_head, ri < ci)]
    neg_a = -jnp.exp(gp_ref[0])
    dt_b = gp_ref[1]

    def conv_silu(ref, w_ref, c):
        base = pl.multiple_of(c * C, C)
        cs = c % nc
        xc = ref[pl.ds(base, C), :]
        pbase = pl.multiple_of(jnp.maximum(base - 8, 0), 8)
        nbase = pl.multiple_of(jnp.minimum(base + C, n_sub * seq_len - 8), 8)
        prev_row = ref[pl.ds(pbase, 8), :][7:8, :] * jnp.where(cs > 0, 1.0, 0.0)
        next_row = ref[pl.ds(nbase, 8), :][0:1, :] * jnp.where(cs < nc - 1, 1.0, 0.0)
        x_prev = jnp.where(row == 0, prev_row, pltpu.roll(xc, 1, 0))
        x_next = jnp.where(row == C - 1, next_row, pltpu.roll(xc, C - 1, 0))
        y = w_ref[0:1, :] * x_prev + w_ref[1:2, :] * xc + w_ref[2:3, :] * x_next
        return _silu(y)

    def stack(x):
        return jnp.concatenate([jnp.where(first_head, x, 0.0), jnp.where(first_head, 0.0, x)], axis=0)

    def chunk_inputs(c):
        base = pl.multiple_of(c * C, C)
        q = conv_silu(q_ref, cq_ref, c)
        k = conv_silu(k_ref, ck_ref, c)
        v = conv_silu(v_ref, cv_ref, c)
        return q, k, v, ab_ref[pl.ds(base, C), :].T

    sub8 = lax.broadcasted_iota(jnp.int32, (A_HEADS, C), 0)

    def pair_row(x8):
        r0 = jnp.sum(jnp.where(sub8 == 2 * hp, x8, 0.0), axis=0, keepdims=True)
        r1 = jnp.sum(jnp.where(sub8 == 2 * hp + 1, x8, 0.0), axis=0, keepdims=True)
        return jnp.concatenate([r0, r1], axis=1)

    def chain_gates(ab_t, d):
        a8 = ab_t[d * A_HEADS:(d + 1) * A_HEADS, :]
        b8 = ab_t[(2 + d) * A_HEADS:(3 + d) * A_HEADS, :]
        g8 = neg_a[d * A_HEADS:(d + 1) * A_HEADS, :C] * jax.nn.softplus(a8 + dt_b[d * A_HEADS:(d + 1) * A_HEADS, :C])
        gc8 = _mm_split(g8, cum_b[d], 3)
        tot8 = jnp.broadcast_to(jnp.sum(g8, axis=-1, keepdims=True), (A_HEADS, C))
        gc_row, beta_row, tot_row = pair_row(gc8), pair_row(jax.nn.sigmoid(b8)), pair_row(tot8)
        cols = jnp.concatenate([gc_row, beta_row, tot_row, jnp.zeros((5, P), F32)], axis=0).T
        return gc_row, tot_row, cols[:, 0:1], cols[:, 1:2], cols[:, 2:3]

    def terms_body(j, carry):
        cs = [TERM_UNROLL * j + t for t in range(TERM_UNROLL)]
        ins = [chunk_inputs(c) for c in cs]
        qsq = [_mm_split(x[0] * x[0], jbd, 2) for x in ins]
        ksq = [_mm_split(x[1] * x[1], jbd, 2) for x in ins]
        qs = [x[0] * lax.rsqrt(s + EPS) * (A_DH ** -0.5) for x, s in zip(ins, qsq)]
        ks = [x[1] * lax.rsqrt(s + EPS) for x, s in zip(ins, ksq)]
        qst = [stack(x) for x in qs]
        kst = [stack(x) for x in ks]
        vst = [stack(x[2]) for x in ins]
        kst_t = [x.T for x in kst]
        kk = [_mm_nt(x, x) for x in kst]
        qk = [_mm_nt(x, y) for x, y in zip(qst, kst)]
        chains = [(t, d) for t in range(TERM_UNROLL) for d in range(2)]
        gates = [chain_gates(ins[t][3], d) for t, d in chains]
        decay, e_gc = [], []
        for (t, d), (gc_row, tot_row, gc_col, beta_col, tot_col) in zip(chains, gates):
            diff = jnp.broadcast_to(gc_col, (P, P)) - jnp.broadcast_to(gc_row, (P, P))
            decay.append(jnp.where(incl_m[d], jnp.exp(jnp.where(incl_m[d], diff, 0.0)), 0.0))
            e_gc.append(jnp.exp(gc_col))
        a_mats = [jnp.where(strict_m[d], g[3] * kk[t] * dc, 0.0) for (t, d), g, dc in zip(chains, gates, decay)]
        rhs = [vst[t] * g[3] + pltpu.roll(kst[t] * (g[3] * e), A_DH, 1)
               for (t, d), g, e in zip(chains, gates, e_gc)]
        xs = _unit_lower_solve_many(a_mats, rhs, same_blk, eye)
        for (t, d), x, g, e, dc in zip(chains, xs, gates, e_gc, decay):
            c = cs[t]
            gc_row, tot_row = g[0], g[1]
            u_s[d, c] = jnp.where(same_head, x, 0.0).astype(BF16)
            w_s[d, c] = pltpu.roll(jnp.where(same_head, 0.0, x), A_DH, 1).astype(BF16)
            qd_s[d, c] = (qst[t] * e).astype(BF16)
            at_s[d, c] = jnp.where(incl_m[d], qk[t] * dc, 0.0).astype(BF16)
            kt_s[d, c] = (kst_t[t] * jnp.exp(tot_row - gc_row)).astype(BF16)
            ge_s[d, c] = jnp.broadcast_to(jnp.exp(tot_row), (8, P))
        return carry

    lax.fori_loop(0, nct // TERM_UNROLL, terms_body, 0)

    def block_diag(s2):
        z = jnp.zeros((A_DH, A_DH), F32)
        return jnp.concatenate([jnp.concatenate([s2[0], z], axis=1),
                                jnp.concatenate([z, s2[1]], axis=1)], axis=0)

    for s in range(n_sub):
        st_s[2 * s] = block_diag(s0f_ref[s])
        st_s[2 * s + 1] = block_diag(s0b_ref[s])

    def scan_body(i, carry):
        chains = [(s, d, s * nc + (i if d == 0 else nc - 1 - i)) for s in range(n_sub) for d in range(2)]
        dot = functools.partial(jnp.dot, preferred_element_type=F32)
        s_bd = [st_s[2 * s + d] for s, d, c in chains]
        sb16 = [x.astype(BF16) for x in s_bd]
        ws = [dot(w_s[d, c], sb) for (s, d, c), sb in zip(chains, sb16)]
        qs_ = [dot(qd_s[d, c], sb) for (s, d, c), sb in zip(chains, sb16)]
        vb = [(u_s[d, c].astype(F32) - x).astype(BF16) for (s, d, c), x in zip(chains, ws)]
        av = [dot(at_s[d, c], x) for (s, d, c), x in zip(chains, vb)]
        kv = [dot(kt_s[d, c], x) for (s, d, c), x in zip(chains, vb)]
        for (s, d, c), sb, q_, a_, k_ in zip(chains, s_bd, qs_, av, kv):
            st_s[2 * s + d] = sb * ge_s[d, c][0:1, :] + k_
            o_st = q_ + a_
            dst = o_ref if d == 0 else ob
            dst[pl.ds(pl.multiple_of(c * C, C), C), :] = o_st[:C] + o_st[C:]
        return carry

    lax.fori_loop(0, nc, scan_body, 0)

    for s in range(n_sub):
        for d, ref in ((0, sf_ref), (1, sb_ref)):
            s_bd = st_s[2 * s + d]
            ref[s, 0] = s_bd[:A_DH, :A_DH]
            ref[s, 1] = s_bd[A_DH:, A_DH:]

    def finish(j, carry):
        bases = [pl.multiple_of((TERM_UNROLL * j + t) * C, C) for t in range(TERM_UNROLL)]
        o = [o_ref[pl.ds(b, C), :] + ob[pl.ds(b, C), :] for b in bases]
        ms = [_mm_split(x * x, jbd, 2) * (1.0 / A_DH) for x in o]
        for b, x, m in zip(bases, o, ms):
            o_ref[pl.ds(b, C), :] = x * lax.rsqrt(m + EPS) * og_ref[...] * _silu(z_ref[pl.ds(b, C), :])
        return carry

    lax.fori_loop(0, nct // TERM_UNROLL, finish, 0)


def _deltanet(proj, conv_w, gate_p, o_gain2, s0f, s0b, tri, jbd, *, seq_len, n_seq, n_sub, row_blk0):
    rows = n_sub * seq_len
    nct = rows // CHUNK
    rb = lambda b: row_blk0 + b
    col = lambda off: (lambda b, hp: (rb(b), off + hp))
    st_spec = pl.BlockSpec((n_sub, 2, A_DH, A_DH), lambda b, hp: (b, hp, 0, 0))
    in_specs = [pl.BlockSpec((rows, LANES), col(0)),
                pl.BlockSpec((rows, LANES), col(4)),
                pl.BlockSpec((rows, LANES), col(8)),
                pl.BlockSpec((rows, LANES), col(12)),
                pl.BlockSpec((rows, LANES), lambda b, hp: (rb(b), 20)),
                pl.BlockSpec((3, LANES), lambda b, hp: (0, hp)),
                pl.BlockSpec((3, LANES), lambda b, hp: (0, 4 + hp)),
                pl.BlockSpec((3, LANES), lambda b, hp: (0, 8 + hp)),
                pl.BlockSpec((2, 2 * A_HEADS, LANES), lambda b, hp: (0, 0, 0)),
                pl.BlockSpec((1, LANES), lambda b, hp: (0, 0)),
                st_spec, st_spec,
                pl.BlockSpec((CHUNK, CHUNK), lambda b, hp: (0, 0)),
                pl.BlockSpec((LANES, LANES), lambda b, hp: (0, 0))]
    args = [proj, proj, proj, proj, proj, conv_w, conv_w, conv_w, gate_p, o_gain2, s0f, s0b, tri, jbd]
    st_shape = jax.ShapeDtypeStruct((n_seq, A_HEADS, A_DH, A_DH), F32)
    tile = lambda dt: pltpu.VMEM((2, nct, LANES, LANES), dt)
    return pl.pallas_call(
        functools.partial(_deltanet_kernel, seq_len=seq_len, n_sub=n_sub),
        grid=(n_seq // n_sub, A_HEADS // 2),
        in_specs=in_specs,
        out_specs=[pl.BlockSpec((rows, LANES), lambda b, hp: (b, hp)), st_spec, st_spec],
        out_shape=[jax.ShapeDtypeStruct((n_seq * seq_len, A_WIDTH), F32), st_shape, st_shape],
        scratch_shapes=[tile(BF16), tile(BF16), tile(BF16), tile(BF16), tile(BF16),
                        pltpu.VMEM((2, nct, 8, LANES), F32),
                        pltpu.VMEM((2 * n_sub, LANES, LANES), F32),
                        pltpu.VMEM((rows, LANES), F32)],
        compiler_params=_cparams(("arbitrary", "arbitrary")),
        name="deltanet",
    )(*args)


def _dft_mats(n):
    idx = np.arange(n)
    ang = 2.0 * np.pi * ((idx[:, None] * idx[None, :]) % n) / n
    return np.cos(ang), np.sin(ang)


def _fnet_ctx_kernel(u_ref, cs_ref, dft_ref, o_ref):
    norm = 1.0 / math.sqrt(SEQ * LANES)
    sls = [slice(g * LANES, (g + 1) * LANES) for g in range(B_GROUPS)]
    p = [_mm(u_ref[:, sl], cs_ref[...]) for sl in sls]
    stack = [jnp.concatenate([x[:, :LANES], x[:, LANES:]], axis=0) for x in p]
    y = [_mm(dft_ref[...], x) for x in stack]
    for sl, x in zip(sls, y):
        o_ref[:, sl] = x * norm


def _fnet_ctx(proj, cs, dft):
    return pl.pallas_call(
        _fnet_ctx_kernel,
        grid=(BATCH,),
        in_specs=[pl.BlockSpec((SEQ, B_WIDTH), lambda b: (b, 4)),
                  pl.BlockSpec(cs.shape, lambda b: (0, 0)),
                  pl.BlockSpec(dft.shape, lambda b: (0, 0))],
        out_specs=pl.BlockSpec((SEQ, B_WIDTH), lambda b: (b, 0)),
        out_shape=jax.ShapeDtypeStruct((NC, B_WIDTH), F32),
        compiler_params=_cparams(("arbitrary",)),
        name="fnet_ctx",
    )(proj, cs, dft)


FN_SUB = 4


def _fnet_lat1_kernel(u_ref, ca_ref, m1_ref, twc_ref, tws_ref, o_ref):
    r = 64
    units = [(j, g) for j in range(FN_SUB) for g in range(B_GROUPS)]
    rows = lambda j: slice(j * r, (j + 1) * r)
    pa = [_mm(u_ref[rows(j), g * LANES:(g + 1) * LANES], ca_ref[...]) for j, g in units]
    rhs = [jnp.concatenate([x[:, :2 * LANES], x[:, 2 * LANES:]], axis=0) for x in pa]
    zz = [_mm(m1_ref[...], x) for x in rhs]
    for (j, g), z in zip(units, zz):
        c = twc_ref[j]
        s = tws_ref[j]
        zr, zi = z[:, :LANES], z[:, LANES:]
        o_ref[rows(j), 2 * g * LANES:(2 * g + 1) * LANES] = (zr * c + zi * s).astype(BF16)
        o_ref[rows(j), (2 * g + 1) * LANES:(2 * g + 2) * LANES] = (zi * c - zr * s).astype(BF16)


def _fnet_lat2_kernel(z_ref, m1_ref, o_ref):
    r = 64
    norm = 1.0 / math.sqrt(DEC_SEQ * LANES)
    units = [(j, g) for j in range(FN_SUB) for g in range(B_GROUPS)]
    rows = lambda j: slice(j * r, (j + 1) * r)
    rhs = [jnp.concatenate([z_ref[rows(j), 2 * g * LANES:(2 * g + 1) * LANES],
                            z_ref[rows(j), (2 * g + 1) * LANES:(2 * g + 2) * LANES]], axis=0) for j, g in units]
    y = [_mm(m1_ref[...], x) for x in rhs]
    for (j, g), x in zip(units, y):
        o_ref[rows(j), g * LANES:(g + 1) * LANES] = x * norm


def _fnet_latent(proj, ca, m1, twc, tws):
    r = 64
    u = proj[NC:, 2048:2560].astype(BF16).reshape(DEC_BATCH, r, r, B_WIDTH)
    u = u.transpose(0, 2, 1, 3).reshape(DEC_BATCH * r * r, B_WIDTH)
    steps = DEC_BATCH * r // FN_SUB
    blk = FN_SUB * r
    z = pl.pallas_call(
        _fnet_lat1_kernel,
        grid=(steps,),
        in_specs=[pl.BlockSpec((blk, B_WIDTH), lambda s: (s, 0)),
                  pl.BlockSpec(ca.shape, lambda s: (0, 0)),
                  pl.BlockSpec(m1.shape, lambda s: (0, 0)),
                  pl.BlockSpec((FN_SUB, r, LANES), lambda s: (s % (r // FN_SUB), 0, 0)),
                  pl.BlockSpec((FN_SUB, r, LANES), lambda s: (s % (r // FN_SUB), 0, 0))],
        out_specs=pl.BlockSpec((blk, 2 * B_WIDTH), lambda s: (s, 0)),
        out_shape=jax.ShapeDtypeStruct((NL, 2 * B_WIDTH), BF16),
        compiler_params=_cparams(("arbitrary",)),
        name="fnet_lat1",
    )(u, ca, m1, twc, tws)
    z = z.reshape(DEC_BATCH, r, r, 2 * B_WIDTH).transpose(0, 2, 1, 3).reshape(NL, 2 * B_WIDTH)
    y = pl.pallas_call(
        _fnet_lat2_kernel,
        grid=(steps,),
        in_specs=[pl.BlockSpec((blk, 2 * B_WIDTH), lambda s: (s, 0)),
                  pl.BlockSpec(m1.shape, lambda s: (0, 0))],
        out_specs=pl.BlockSpec((blk, B_WIDTH), lambda s: (s, 0)),
        out_shape=jax.ShapeDtypeStruct((NL, B_WIDTH), F32),
        compiler_params=_cparams(("arbitrary",)),
        name="fnet_lat2",
    )(z, m1)
    return y.reshape(DEC_BATCH, r, r, B_WIDTH).transpose(0, 2, 1, 3).reshape(NL, B_WIDTH)


def _head_masks():
    lane = lax.broadcasted_iota(jnp.int32, (1, LANES), 1)
    return lane < NA_DH


def _attend_many(chains):
    s = [[_mm_nt(q, k) if b is None else _mm_nt(q, k) + b for k, v, b in kv] for q, kv in chains]
    m = [functools.reduce(jnp.maximum, [jnp.max(x, axis=-1, keepdims=True) for x in xs]) for xs in s]
    p = [[jnp.exp(x - mi) for x in xs] for xs, mi in zip(s, m)]
    l = [sum(jnp.sum(x, axis=-1, keepdims=True) for x in xs) for xs in p]
    o = [sum(_mm(x, v) for x, (k, v, b) in zip(xs, kv)) for xs, (q, kv) in zip(p, chains)]
    return [oi / li for oi, li in zip(o, l)]


def _na_ctx_kernel(q_ref, k_ref, v_ref, o_ref):
    first = _head_masks()
    n_split = 2
    rows = SEQ // n_split
    chains = []
    for p in range(NA_CTX_PAIRS):
        lanes = slice(p * LANES, (p + 1) * LANES)
        k = k_ref[:, lanes].astype(BF16)
        v = v_ref[:, lanes].astype(BF16)
        for j in range(n_split):
            q = q_ref[j * rows:(j + 1) * rows, lanes]
            for a in range(2):
                chains.append((jnp.where(first if a == 0 else jnp.logical_not(first), q * NA_SCALE, 0.0),
                               [(k, v, None)]))
    outs = _attend_many(chains)
    for p in range(NA_CTX_PAIRS):
        for j in range(n_split):
            i = 2 * (p * n_split + j)
            o_ref[j * rows:(j + 1) * rows, p * LANES:(p + 1) * LANES] = jnp.where(first, outs[i], outs[i + 1])


def _na_ctx(qkv):
    w = NA_CTX_PAIRS * LANES
    nblk = D // w
    return pl.pallas_call(
        _na_ctx_kernel,
        grid=(BATCH, nblk),
        in_specs=[pl.BlockSpec((SEQ, w), lambda b, hp: (b, hp)),
                  pl.BlockSpec((SEQ, w), lambda b, hp: (b, nblk + hp)),
                  pl.BlockSpec((SEQ, w), lambda b, hp: (b, 2 * nblk + hp))],
        out_specs=pl.BlockSpec((SEQ, w), lambda b, hp: (b, hp)),
        out_shape=jax.ShapeDtypeStruct((NC, D), F32),
        compiler_params=_cparams(("arbitrary", "arbitrary")),
        name="na_ctx",
    )(qkv, qkv, qkv)


def _na_lat_kernel(q_ref, k_ref, v_ref, kc_ref, vc_ref, tt_ref, o_ref):
    first = _head_masks()
    rows = DEC_SEQ // GRID_W
    kctx = kc_ref[...].astype(BF16)
    vctx = vc_ref[...].astype(BF16)
    nkeys = WIN_R * GRID_W

    def rows_body(j, carry):
        chains = []
        for t in range(NA_ROW_UNROLL):
            r = NA_ROW_UNROLL * j + t
            r0 = jnp.clip(r - WIN_R // 2, 0, rows - WIN_R)
            dr0 = r0 - r + (WIN_R - 1)
            q = q_ref[pl.ds(pl.multiple_of(r * GRID_W, GRID_W), GRID_W), :] * NA_SCALE
            kbase = pl.multiple_of(r0 * GRID_W, GRID_W)
            kl = k_ref[pl.ds(kbase, nkeys), :].astype(BF16)
            vl = v_ref[pl.ds(kbase, nkeys), :].astype(BF16)
            for a in range(2):
                qm = jnp.where(first if a == 0 else jnp.logical_not(first), q, 0.0)
                bias = jnp.concatenate([tt_ref[a, dr0 + 2 * i] for i in range(WIN_R // 2)], axis=1)
                chains.append((qm, [(kl, vl, bias), (kctx, vctx, None)]))
        outs = _attend_many(chains)
        for t in range(NA_ROW_UNROLL):
            r = NA_ROW_UNROLL * j + t
            o_ref[pl.ds(pl.multiple_of(r * GRID_W, GRID_W), GRID_W), :] = jnp.where(first, outs[2 * t], outs[2 * t + 1])
        return carry

    lax.fori_loop(0, rows // NA_ROW_UNROLL, rows_body, 0)


def _na_latent(qkv, cache_k2, cache_v2, tt2):
    rb0 = NC // DEC_SEQ
    return pl.pallas_call(
        _na_lat_kernel,
        grid=(DEC_BATCH, NA_HEADS // 2),
        in_specs=[pl.BlockSpec((DEC_SEQ, LANES), lambda b, hp: (rb0 + b, hp)),
                  pl.BlockSpec((DEC_SEQ, LANES), lambda b, hp: (rb0 + b, 8 + hp)),
                  pl.BlockSpec((DEC_SEQ, LANES), lambda b, hp: (rb0 + b, 16 + hp)),
                  pl.BlockSpec((None, 256, LANES), lambda b, hp: (b, 0, hp)),
                  pl.BlockSpec((None, 256, LANES), lambda b, hp: (b, 0, hp)),
                  pl.BlockSpec((2, 2 * WIN_R - 2, GRID_W, LANES), lambda b, hp: (hp, 0, 0, 0))],
        out_specs=pl.BlockSpec((DEC_SEQ, LANES), lambda b, hp: (b, hp)),
        out_shape=jax.ShapeDtypeStruct((NL, D), F32),
        compiler_params=_cparams(("arbitrary", "arbitrary")),
        name="na_latent",
    )(qkv, qkv, qkv, cache_k2, cache_v2, tt2)


def _rpb_tables(rpb):
    col = np.arange(GRID_W)
    start = np.clip(col - WIN_C // 2, 0, GRID_W - WIN_C)
    inside = (col[None, :] >= start[:, None]) & (col[None, :] < start[:, None] + WIN_C)
    w = GRID_W
    period = 2 * w - 1
    x = jnp.pad(rpb, ((0, 0), (0, 0), (w - WIN_C, w - WIN_C)))
    flat = jnp.tile(x, (1, 1, w))[:, :, w - 1:w - 1 + w * (period - 1)]
    t = flat.reshape(NA_HEADS, 2 * WIN_R - 1, w, period - 1)[..., :w]
    t = jnp.where(inside[None, None], t, NEG)
    return jnp.concatenate([t[:, :-1], t[:, 1:]], axis=-1)


ROUTE_SPLIT = 2
ROUTE_ROWS = 40


def _router_kernel(*refs, n_in):
    a_refs = refs[:2 * n_in]
    w_refs = refs[2 * n_in:3 * n_in]
    x_refs = refs[3 * n_in:-11]
    m_ref, g_ref, wr_ref, br_ref, tri_ref, xn_ref, xf_ref, ri_ref, rw_ref, cnt_ref, base_scr = refs[-11:]
    i = pl.program_id(0)

    @pl.when(i == 0)
    def _():
        base_scr[...] = jnp.zeros_like(base_scr)
        cnt_ref[...] = jnp.zeros_like(cnt_ref)

    rows = [slice(t * TM // ROUTE_SPLIT, (t + 1) * TM // ROUTE_SPLIT) for t in range(ROUTE_SPLIT)]
    is_ctx = pl.program_id(0) < NC // TM
    pick = lambda pair, r: jnp.where(is_ctx, pair[0][r, :], pair[1][r, :]) if len(pair) == 2 else pair[0][r, :]
    acc = [sum(_mm(pick(a_refs[2 * j:2 * j + 2], r), w_ref[...]) for j, w_ref in enumerate(w_refs)) for r in rows]
    x_new = [pick(x_refs, r) + m_ref[2:3, :] * a for r, a in zip(rows, acc)]
    for r, xn in zip(rows, x_new):
        xn_ref[r, :] = xn
    h = [_modulated_norm(xn, m_ref, g_ref, 3, 4) for xn in x_new]
    for r, hh in zip(rows, h):
        xf_ref[r, :] = _pack_bf16_pairs(hh[:, :D // 2], hh[:, D // 2:])
    logits = jnp.concatenate(
        [lax.dot_general(wr_ref[...], hh, (((1,), (1,)), ((), ())), preferred_element_type=F32, precision=HIGHEST)
         for hh in h], axis=1) + br_ref[:, 0:1]
    row = lax.broadcasted_iota(jnp.int32, logits.shape, 0)
    cmax = lambda x: jnp.max(x, axis=0, keepdims=True)
    cmin = lambda x: jnp.min(x, axis=0, keepdims=True)
    csum = lambda x: jnp.sum(x, axis=0, keepdims=True)

    gmask = row < N_GROUPS
    mg = cmax(jnp.where(gmask, logits, NEG))
    eg = jnp.where(gmask, jnp.exp(jnp.where(gmask, logits - mg, NEG)), 0.0)
    pg = eg / csum(eg)
    p_grp = cmax(pg)
    grp = cmin(jnp.where(jnp.logical_and(gmask, pg == p_grp), row, ROUTE_ROWS))
    lo = N_GROUPS + grp * EXP_PER_GROUP
    emask = jnp.logical_and(row >= lo, row < lo + EXP_PER_GROUP)
    me = cmax(jnp.where(emask, logits, NEG))
    ee = jnp.where(emask, jnp.exp(jnp.where(emask, logits - me, NEG)), 0.0)
    pe = ee / csum(ee)
    p1 = cmax(pe)
    i1 = cmin(jnp.where(jnp.logical_and(emask, pe == p1), row, ROUTE_ROWS))
    m2 = jnp.logical_and(emask, row != i1)
    p2 = cmax(jnp.where(m2, pe, -1.0))
    i2 = cmin(jnp.where(jnp.logical_and(m2, pe == p2), row, ROUTE_ROWS))
    den = p1 + p2
    w1 = p_grp * p1 / den
    w2 = p_grp * p2 / den

    sel1 = row == i1
    sel2 = row == i2
    oh = jnp.where(jnp.logical_or(sel1, sel2), 1.0, 0.0).astype(BF16)
    before = jnp.dot(oh, tri_ref[...], preferred_element_type=F32) + base_scr[:, 0:1]
    rank1 = csum(jnp.where(sel1, before, 0.0))
    rank2 = csum(jnp.where(sel2, before, 0.0))
    base_scr[...] = base_scr[...] + jnp.sum(oh.astype(F32), axis=1, keepdims=True)
    cnt_ref[...] = cnt_ref[...] + lax.dot_general(jnp.ones((8, TM), BF16), oh, (((1,), (1,)), ((), ())),
                                                  preferred_element_type=F32)
    sub = lax.broadcasted_iota(jnp.int32, (8, TM), 0)
    ri_ref[...] = jnp.where(sub == 0, i1 - N_GROUPS, jnp.where(sub == 1, i2 - N_GROUPS,
                  jnp.where(sub == 2, rank1.astype(jnp.int32), jnp.where(sub == 3, rank2.astype(jnp.int32), 0))))
    rw_ref[...] = jnp.where(sub == 0, w1, jnp.where(sub == 1, w2, 0.0))


def _router(a_pairs, w_list, x, m3, gain, wr_t, br_t, tri_upper):
    in_specs, args = [], []
    for pair in a_pairs:
        specs, ops = _token_specs(pair)
        in_specs += specs
        args += ops
    x_specs, x_args = _token_specs(x)
    in_specs += ([pl.BlockSpec(w.shape, lambda i: (0, 0)) for w in w_list] + x_specs
                 + [pl.BlockSpec((None, 6, D), lambda i: (_cond_row(i), 0, 0)),
                    pl.BlockSpec((1, D), lambda i: (0, 0)),
                    pl.BlockSpec((ROUTE_ROWS, D), lambda i: (0, 0)),
                    pl.BlockSpec((ROUTE_ROWS, LANES), lambda i: (0, 0)),
                    pl.BlockSpec((TM, TM), lambda i: (0, 0))])
    return pl.pallas_call(
        functools.partial(_router_kernel, n_in=len(a_pairs)),
        grid=(N // TM,),
        in_specs=in_specs,
        out_specs=[pl.BlockSpec((TM, D), lambda i: (i, 0)),
                   pl.BlockSpec((TM, D // 2), lambda i: (i, 0)),
                   pl.BlockSpec((8, TM), lambda i: (0, i)),
                   pl.BlockSpec((8, TM), lambda i: (0, i)),
                   pl.BlockSpec((8, ROUTE_ROWS), lambda i: (0, 0))],
        out_shape=[jax.ShapeDtypeStruct((N, D), F32),
                   jax.ShapeDtypeStruct((N, D // 2), jnp.int32),
                   jax.ShapeDtypeStruct((8, N), jnp.int32),
                   jax.ShapeDtypeStruct((8, N), F32),
                   jax.ShapeDtypeStruct((8, ROUTE_ROWS), F32)],
        scratch_shapes=[pltpu.VMEM((ROUTE_ROWS, LANES), F32)],
        compiler_params=_cparams(("arbitrary",)),
        name="router",
    )(*args, *w_list, *x_args, m3, gain, wr_t, br_t, tri_upper)


def _row_copy(src_hbm, row, dst, r, sem):
    return pltpu.make_async_copy(src_hbm.at[pl.ds(row, 1)], dst.at[pl.ds(r, 1)], sem)


def _expert_kernel(be_ref, nu_ref, d_ref, pad_ref, xp_hbm, wg_ref, wu_ref, wd_ref, ys_ref, xres, xbuf, st_ref, sem):
    del be_ref
    b = pl.program_id(0)
    half = D // 2

    def gather(blk, slot, part=0, parts=1):
        base = blk * MOE_ROWS
        for r in range(part * MOE_ROWS // parts, (part + 1) * MOE_ROWS // parts):
            xbuf[slot, pl.ds(r, 1), :] = xres[pl.ds(st_ref[base + r], 1), :]

    @pl.when(b == 0)
    def _():
        cp = pltpu.make_async_copy(xp_hbm, xres, sem.at[0])
        cp.start()

        def clear_tail(e, c):
            def clear(s_, c2):
                st_ref[s_] = 0
                return c2

            return lax.fori_loop(pad_ref[2 * e], pad_ref[2 * e + 1], clear, c)

        lax.fori_loop(0, N_EXPERTS + 1, clear_tail, 0)

        def invert(j, c):
            slots = [d_ref[16 * j + t] for t in range(16)]
            for t in range(16):
                st_ref[slots[t]] = 8 * j + t // 2
            return c

        lax.fori_loop(0, 2 * N // 16, invert, 0)
        cp.wait()
        gather(0, 0)

    @pl.when(b < nu_ref[0])
    def _():
        nxt = (jnp.minimum(b + 1, N_SLOT_BLOCKS - 1), (b + 1) % 2)
        x_lo, x_hi = _unpack_bf16_pairs(xbuf[b % 2])
        dot = functools.partial(jnp.dot, preferred_element_type=F32)
        gather(*nxt, 0, 8)
        g = dot(x_lo, wg_ref[:half, :].astype(BF16))
        gather(*nxt, 1, 8)
        g = g + dot(x_hi, wg_ref[half:, :].astype(BF16))
        gather(*nxt, 2, 8)
        u = dot(x_lo, wu_ref[:half, :].astype(BF16))
        gather(*nxt, 3, 8)
        u = u + dot(x_hi, wu_ref[half:, :].astype(BF16))
        hb = (_silu(g) * u).astype(BF16)
        quarter = D // 4
        for j in range(4):
            gather(*nxt, 4 + j, 8)
            cols = slice(j * quarter, (j + 1) * quarter)
            ys_ref[:, cols] = jnp.dot(hb, wd_ref[:, cols].astype(BF16), preferred_element_type=F32)

    @pl.when(b >= nu_ref[0])
    def _():
        ys_ref[...] = jnp.zeros_like(ys_ref)


def _experts(block_e, n_used, dest, pad_ranges, xp, w_gate, w_up, w_down, layer):
    grid_spec = pltpu.PrefetchScalarGridSpec(
        num_scalar_prefetch=4,
        grid=(N_SLOT_BLOCKS,),
        in_specs=[pl.BlockSpec(memory_space=pl.ANY),
                  pl.BlockSpec((None, None, D, D_EXPERT), lambda b, be, nu, st, pd: (layer, be[b], 0, 0)),
                  pl.BlockSpec((None, None, D, D_EXPERT), lambda b, be, nu, st, pd: (layer, be[b], 0, 0)),
                  pl.BlockSpec((None, None, D_EXPERT, D), lambda b, be, nu, st, pd: (layer, be[b], 0, 0))],
        out_specs=pl.BlockSpec((MOE_ROWS, D), lambda b, be, nu, st, pd: (b, 0)),
        scratch_shapes=[pltpu.VMEM((N, D // 2), jnp.int32),
                        pltpu.VMEM((2, MOE_ROWS, D // 2), jnp.int32),
                        pltpu.SMEM((N_SLOT_BLOCKS * MOE_ROWS,), jnp.int32),
                        pltpu.SemaphoreType.DMA((1,))])
    return pl.pallas_call(
        _expert_kernel,
        grid_spec=grid_spec,
        out_shape=jax.ShapeDtypeStruct((N_SLOT_BLOCKS * MOE_ROWS, D), F32),
        compiler_params=pltpu.CompilerParams(dimension_semantics=("arbitrary",), vmem_limit_bytes=EXPERT_VMEM_LIMIT),
        name="experts",
    )(block_e, n_used, dest, pad_ranges, xp, w_gate, w_up, w_down)


def _combine_kernel(d_ref, ys_hbm, x_ref, m_ref, rw_ref, fn_ref, o_ref, buf, sem, *, final, tile0, n_tiles):
    i = pl.program_id(0)

    def issue(tile, slot):
        base = (tile0 + tile) * (2 * TM)
        for r in range(TM):
            for kk in range(2):
                _row_copy(ys_hbm, d_ref[base + 2 * r + kk], buf.at[slot, kk], r, sem.at[slot]).start()

    @pl.when(i == 0)
    def _():
        issue(0, 0)

    @pl.when(i + 1 < n_tiles)
    def _():
        issue(i + 1, (i + 1) % 2)

    slot = i % 2
    for kk in range(2):
        pltpu.make_async_copy(ys_hbm.at[pl.ds(0, TM)], buf.at[slot, kk], sem.at[slot]).wait()
    w = rw_ref[...]
    y = w[:, 0:1] * buf[slot, 0] + w[:, 1:2] * buf[slot, 1]
    out = x_ref[...] + m_ref[5:6, :] * y
    if final:
        ms = jnp.mean(out * out, axis=-1, keepdims=True)
        out = out * lax.rsqrt(ms + EPS) * fn_ref[...]
    o_ref[...] = out


def _combine(dest_flat, ys, x, m3, rw, final_norm, final, tile0=0, n_tiles=N // TM):
    grid_spec = pltpu.PrefetchScalarGridSpec(
        num_scalar_prefetch=1,
        grid=(n_tiles,),
        in_specs=[pl.BlockSpec(memory_space=pl.ANY),
                  pl.BlockSpec((TM, D), lambda i, d: (tile0 + i, 0)),
                  pl.BlockSpec((None, 6, D), lambda i, d: (_cond_row(tile0 + i), 0, 0)),
                  pl.BlockSpec((TM, 2), lambda i, d: (tile0 + i, 0)),
                  pl.BlockSpec((1, D), lambda i, d: (0, 0))],
        out_specs=pl.BlockSpec((TM, D), lambda i, d: (i, 0)),
        scratch_shapes=[pltpu.VMEM((2, 2, TM, D), F32), pltpu.SemaphoreType.DMA((2,))])
    return pl.pallas_call(
        functools.partial(_combine_kernel, final=final, tile0=tile0, n_tiles=n_tiles),
        grid_spec=grid_spec,
        out_shape=jax.ShapeDtypeStruct((n_tiles * TM, D), F32),
        compiler_params=_cparams(("arbitrary",)),
        name="combine",
    )(dest_flat, ys, x, m3, rw, final_norm)


def _combine_proj_kernel(d_ref, ys_hbm, x_ref, m_ref, rw_ref, mn_ref, gn_ref, w_ref, xo_ref, o_ref, buf, sem, *,
                         n_chunk):
    i = pl.program_id(0)
    n_tiles = N // TM
    n_parts = o_ref.shape[1] // n_chunk

    def issue(tile, slot, part=0, parts=1):
        base = tile * (2 * TM)
        for r in range(part * TM // parts, (part + 1) * TM // parts):
            for kk in range(2):
                _row_copy(ys_hbm, d_ref[base + 2 * r + kk], buf.at[slot, kk], r, sem.at[slot]).start()

    def wait(slot):
        for kk in range(2):
            pltpu.make_async_copy(ys_hbm.at[pl.ds(0, TM)], buf.at[slot, kk], sem.at[slot]).wait()

    @pl.when(i == 0)
    def _():
        issue(0, 0)

    slot = i % 2
    wait(slot)
    w = rw_ref[...]
    x_new = x_ref[...] + m_ref[5:6, :] * (w[:, 0:1] * buf[slot, 0] + w[:, 1:2] * buf[slot, 1])
    xo_ref[...] = x_new
    hb = _modulated_norm(x_new, mn_ref, gn_ref, 0, 1).astype(BF16)
    nxt = jnp.minimum(i + 1, n_tiles - 1)
    for j in range(n_parts):
        issue(nxt, 1 - slot, j, n_parts)
        sl = slice(j * n_chunk, (j + 1) * n_chunk)
        o_ref[:, sl] = jnp.dot(hb, w_ref[:, sl], preferred_element_type=F32)

    @pl.when(i == n_tiles - 1)
    def _():
        wait(1 - slot)


def _combine_proj(dest_flat, ys, x, m3, rw, m3_next, gain_next, w_bf16, n_chunk):
    nout = w_bf16.shape[1]
    grid_spec = pltpu.PrefetchScalarGridSpec(
        num_scalar_prefetch=1,
        grid=(N // TM,),
        in_specs=[pl.BlockSpec(memory_space=pl.ANY),
                  pl.BlockSpec((TM, D), lambda i, d: (i, 0)),
                  pl.BlockSpec((None, 6, D), lambda i, d: (_cond_row(i), 0, 0)),
                  pl.BlockSpec((TM, 2), lambda i, d: (i, 0)),
                  pl.BlockSpec((None, 6, D), lambda i, d: (_cond_row(i), 0, 0)),
                  pl.BlockSpec((1, D), lambda i, d: (0, 0)),
                  pl.BlockSpec((D, nout), lambda i, d: (0, 0))],
        out_specs=[pl.BlockSpec((TM, D), lambda i, d: (i, 0)),
                   pl.BlockSpec((TM, nout), lambda i, d: (i, 0))],
        scratch_shapes=[pltpu.VMEM((2, 2, TM, D), F32), pltpu.SemaphoreType.DMA((2,))])
    return pl.pallas_call(
        functools.partial(_combine_proj_kernel, n_chunk=n_chunk),
        grid_spec=grid_spec,
        out_shape=[jax.ShapeDtypeStruct((N, D), F32), jax.ShapeDtypeStruct((N, nout), F32)],
        compiler_params=_cparams(("arbitrary",)),
        name="combine_proj",
    )(dest_flat, ys, x, m3, rw, m3_next, gain_next, w_bf16)


def _hier_moe(a_pairs, w_list, x, m3, gain, layer, w_rg, b_rg, w_re, b_re, w_gate, w_up, w_down, tri_tm,
              final_norm, final, next_proj=None):
    pad = ROUTE_ROWS - N_GROUPS - N_EXPERTS
    wr_t = jnp.concatenate([w_rg.T, w_re.transpose(0, 2, 1).reshape(N_EXPERTS, D), jnp.zeros((pad, D), F32)], axis=0)
    br_t = jnp.broadcast_to(jnp.concatenate([b_rg, b_re.reshape(N_EXPERTS), jnp.zeros((pad,), F32)])[:, None],
                            (ROUTE_ROWS, LANES))
    x, xf, ri_t, rw_t, cnt = _router(a_pairs, w_list, x, m3, gain, wr_t, br_t, tri_tm)
    e_idx = ri_t[0:2].T
    rank = ri_t[2:4].T
    rw = rw_t[0:2].T
    counts = cnt[0, N_GROUPS:N_GROUPS + N_EXPERTS].astype(jnp.int32)
    padded = (counts + MOE_ROWS - 1) // MOE_ROWS * MOE_ROWS
    end_pad = jnp.cumsum(padded)
    start_pad = end_pad - padded
    experts = jnp.arange(N_EXPERTS, dtype=jnp.int32)
    start_of = jnp.sum(jnp.where(e_idx[:, :, None] == experts, start_pad, 0), axis=-1)
    dest = (start_of + rank).reshape(-1).astype(jnp.int32)
    block_start = jnp.arange(N_SLOT_BLOCKS, dtype=jnp.int32) * MOE_ROWS
    block_e = jnp.minimum(jnp.sum((end_pad[None, :] <= block_start[:, None]).astype(jnp.int32), axis=1),
                          N_EXPERTS - 1)
    n_used = (end_pad[-1:] // MOE_ROWS).astype(jnp.int32)
    tail = jnp.stack([end_pad[-1], jnp.minimum(end_pad[-1] + MOE_ROWS, N_SLOT_BLOCKS * MOE_ROWS)])
    pad_ranges = jnp.concatenate([jnp.stack([start_pad + counts, end_pad], axis=1).reshape(-1), tail]).astype(jnp.int32)
    ys = _experts(block_e, n_used, dest, pad_ranges, xf, w_gate, w_up, w_down, layer)
    if not final:
        return _combine_proj(dest, ys, x, m3, rw, *next_proj)
    nct = NC // TM
    return (_combine(dest, ys, x, m3, rw, final_norm, True, 0, nct),
            _combine(dest, ys, x, m3, rw, final_norm, True, nct, N // TM - nct))


def kernel(x_prompt, x_sample, state_A_fwd, state_A_bwd, cache_k, cache_v, c, c_ctx, mod_w, mod_b, norm_mix, norm_ffn, ab_w_in, ab_conv, ab_a_log, ab_dt_bias, ab_o_gain, ab_w_out, na_w_qkv, na_rpb, na_w_out, moe_w_rg, moe_b_rg, moe_w_re, moe_b_re, moe_w_gate, moe_w_up, moe_w_down, final_norm):
    x = (x_prompt.reshape(NC, D), x_sample.reshape(NL, D))
    cond8 = jnp.concatenate([c_ctx[None, :], c, jnp.zeros((8 - 1 - DEC_BATCH, D), F32)], axis=0)
    mods = _ada_params(cond8, mod_w, mod_b).reshape(DEPTH, 8, 6, D)

    tri_tm = jnp.asarray(np.triu(np.ones((TM, TM)), 1), BF16)
    tri_c = jnp.asarray(np.tril(np.ones((CHUNK, CHUNK))), F32)
    half = np.arange(LANES) < A_DH
    jbd = jnp.asarray((half[:, None] == half[None, :]).astype(np.float32))
    fn = final_norm[None, :]

    m3 = mods[0]
    w_in = ab_w_in[0]
    w_in = jnp.concatenate([w_in[:, :2048], w_in[:, 2080:2592], w_in[:, 2048:2080],
                            jnp.zeros((D, AB_COLS - 2592), F32)], axis=1).astype(BF16)
    proj = _modproj(x, m3, norm_mix[0][None, :], w_in, 896)
    gate_p = jnp.broadcast_to(jnp.stack([ab_a_log[0].reshape(-1), ab_dt_bias[0].reshape(-1)])[:, :, None],
                              (2, 2 * A_HEADS, LANES))
    o_gain2 = jnp.tile(ab_o_gain[0], 2)[None, :]
    zeros_state = jnp.zeros((BATCH, A_HEADS, A_DH, A_DH), F32)
    mix_a_c, s_f, s_b = _deltanet(proj, ab_conv[0], gate_p, o_gain2, zeros_state, zeros_state, tri_c, jbd,
                                  seq_len=SEQ, n_seq=BATCH, n_sub=4, row_blk0=0)
    mix_a_l, _, _ = _deltanet(proj, ab_conv[0], gate_p, o_gain2, state_A_fwd[:, 0], state_A_bwd[:, 0], tri_c, jbd,
                              seq_len=DEC_SEQ, n_seq=DEC_BATCH, n_sub=1, row_blk0=NC // DEC_SEQ)

    cc, sc = _dft_mats(LANES)
    ct, st = _dft_mats(SEQ)
    c64, s64 = _dft_mats(64)
    cs = jnp.asarray(np.concatenate([cc, sc], axis=1), BF16)
    dft = jnp.asarray(np.concatenate([ct, -st], axis=1), BF16)
    ca = jnp.asarray(np.concatenate([cc, -sc, -sc, -cc], axis=1), BF16)
    m1 = jnp.asarray(np.concatenate([c64, s64], axis=1), BF16)
    tw_idx = np.arange(64)
    tw_ang = 2.0 * np.pi * (tw_idx[:, None] * tw_idx[None, :]) / DEC_SEQ
    twc = jnp.broadcast_to(jnp.asarray(np.cos(tw_ang), F32)[:, :, None], (64, 64, LANES))
    tws = jnp.broadcast_to(jnp.asarray(np.sin(tw_ang), F32)[:, :, None], (64, 64, LANES))
    mix_b_c = _fnet_ctx(proj, cs, dft)
    mix_b_l = _fnet_latent(proj, ca, m1, twc, tws)

    w_out = ab_w_out[0].astype(BF16)
    x, qkv = _hier_moe([(mix_a_c, mix_a_l), (mix_b_c, mix_b_l)], [w_out[:A_WIDTH], w_out[A_WIDTH:]], x, m3,
                       norm_ffn[0][None, :], 0, moe_w_rg[0], moe_b_rg[0], moe_w_re[0], moe_b_re[0],
                       moe_w_gate, moe_w_up, moe_w_down, tri_tm, fn, False,
                       next_proj=(mods[1], norm_mix[1][None, :], na_w_qkv[0].astype(BF16), 512))

    m3 = mods[1]
    attn_c = _na_ctx(qkv)
    attn_l = _na_latent(qkv, cache_k[:, 0].reshape(DEC_BATCH, 256, D), cache_v[:, 0].reshape(DEC_BATCH, 256, D),
                        _rpb_tables(na_rpb[0]))
    y_c, y_l = _hier_moe([(attn_c, attn_l)], [na_w_out[0].astype(BF16)], x, m3, norm_ffn[1][None, :], 1, moe_w_rg[1], moe_b_rg[1], moe_w_re[1], moe_b_re[1],
                         moe_w_gate, moe_w_up, moe_w_down, tri_tm, fn, True)

    new_k = qkv[:NC, D:2 * D].reshape(BATCH, 1, SEQ, NA_HEADS, NA_DH)
    new_v = qkv[:NC, 2 * D:].reshape(BATCH, 1, SEQ, NA_HEADS, NA_DH)
    return (y_c.reshape(BATCH, SEQ, D), y_l.reshape(DEC_BATCH, DEC_SEQ, D),
            s_f[:, None], s_b[:, None], new_k, new_v)
```

```python
import functools
import math

import numpy as np
import jax
import jax.numpy as jnp
from jax import lax
from jax.experimental import pallas as pl
from jax.experimental.pallas import tpu as pltpu

F32 = jnp.float32
BF16 = jnp.bfloat16
HIGHEST = lax.Precision.HIGHEST

D = 1024
BATCH, SEQ = 32, 256
DEC_BATCH, DEC_SEQ = 2, 4096
NC = BATCH * SEQ
NL = DEC_BATCH * DEC_SEQ
N = NC + NL
DEPTH = 2
GRID_W = 64
A_DH = 64
A_HEADS = 8
A_WIDTH = 512
CHUNK = 64
B_WIDTH = 512
B_GROUPS = 4
NA_DH = 64
NA_HEADS = 16
WIN_R, WIN_C = 8, 16
N_GROUPS, EXP_PER_GROUP, N_EXPERTS = 4, 8, 32
D_EXPERT = 512
EPS = 1e-6

LANES = 128
TM = 256
MOE_ROWS = 256
N_SLOT_BLOCKS = (2 * N) // MOE_ROWS + N_EXPERTS
AB_COLS = 2688
VMEM_LIMIT = 56 * 1024 * 1024
EXPERT_VMEM_LIMIT = 60 * 1024 * 1024
NEG = -1e30


def _cparams(sem):
    return pltpu.CompilerParams(dimension_semantics=sem, vmem_limit_bytes=VMEM_LIMIT)


def _mm(a, b):
    return jnp.dot(a.astype(BF16), b.astype(BF16), preferred_element_type=F32)


def _mm_nt(a, b):
    return lax.dot_general(a.astype(BF16), b.astype(BF16), (((1,), (1,)), ((), ())),
                           preferred_element_type=F32)


SOLVE_BLK = 16
NA_SCALE = NA_DH ** -0.5
NA_CTX_PAIRS = 8
NA_ROW_UNROLL = 16
TERM_UNROLL = 4


def _unit_lower_solve(a_mat, rhs, same_blk, eye):
    dg = jnp.where(same_blk, a_mat, 0.0)
    p = -dg
    dinv = eye + p
    for _ in range(int(math.log2(SOLVE_BLK)) - 1):
        p = _mm(p, p)
        dinv = dinv + _mm(p, dinv)
    mp = -_mm(dinv, a_mat - dg)
    y = _mm(dinv, rhs)
    y = y + _mm(mp, y)
    for _ in range(int(math.log2(a_mat.shape[0] // SOLVE_BLK)) - 1):
        mp = _mm(mp, mp)
        y = y + _mm(mp, y)
    return y


def _unit_lower_solve_many(a_mats, rhs, same_blk, eye):
    off = [jnp.where(same_blk, 0.0, a).astype(BF16) for a in a_mats]
    p = [jnp.where(same_blk, -a, 0.0).astype(BF16) for a in a_mats]
    dinv = [(eye + x.astype(F32)).astype(BF16) for x in p]
    for _ in range(int(math.log2(SOLVE_BLK)) - 1):
        p = [_mm(x, x).astype(BF16) for x in p]
        dinv = [(di.astype(F32) + _mm(x, di)).astype(BF16) for x, di in zip(p, dinv)]
    mp = [(-_mm(di, o)).astype(BF16) for di, o in zip(dinv, off)]
    y = [_mm(di, r) for di, r in zip(dinv, rhs)]
    y = [yi + _mm(m, yi) for m, yi in zip(mp, y)]
    for _ in range(int(math.log2(a_mats[0].shape[0] // SOLVE_BLK)) - 1):
        mp = [_mm(m, m).astype(BF16) for m in mp]
        y = [yi + _mm(m, yi) for m, yi in zip(mp, y)]
    return y


def _mm_split(a, b, parts, split_rhs=False):
    x = b if split_rhs else a
    acc = None
    for _ in range(parts):
        piece = x.astype(BF16)
        x = x - piece.astype(F32)
        term = (jnp.dot(a.astype(BF16), piece, preferred_element_type=F32) if split_rhs
                else jnp.dot(piece, b.astype(BF16), preferred_element_type=F32))
        acc = term if acc is None else acc + term
    return acc


def _mm_hi(a, b):
    return jnp.dot(a, b, preferred_element_type=F32, precision=HIGHEST)


def _silu(x):
    return x * jax.nn.sigmoid(x)


def _bf16_bits(x):
    b = lax.bitcast_convert_type(x, jnp.int32)
    return b + 0x7FFF + (lax.shift_right_logical(b, jnp.int32(16)) & 1)


_HIGH16 = -65536


def _pack_bf16_pairs(a, b):
    return lax.shift_right_logical(_bf16_bits(a), jnp.int32(16)) | (_bf16_bits(b) & _HIGH16)


def _unpack_bf16_pairs(p):
    a = lax.bitcast_convert_type(lax.shift_left(p, jnp.int32(16)), F32)
    b = lax.bitcast_convert_type(p & _HIGH16, F32)
    return a.astype(BF16), b.astype(BF16)


def _cond_row(i):
    return jnp.where(i < NC // TM, 0, 1 + (i - NC // TM) // (DEC_SEQ // TM))


def _modulated_norm(x, m_ref, g_ref, shift_idx, scale_idx):
    ms = jnp.mean(x * x, axis=-1, keepdims=True)
    y = x * lax.rsqrt(ms + EPS) * g_ref[...]
    return y * (1.0 + m_ref[scale_idx:scale_idx + 1, :]) + m_ref[shift_idx:shift_idx + 1, :]


def _ada_kernel(cond_ref, w_ref, b_ref, o_ref):
    o_ref[...] = _mm_hi(_silu(cond_ref[...]), w_ref[...]) + b_ref[...]


def _ada_params(cond8, mod_w, mod_b):
    tn = 1536
    return pl.pallas_call(
        _ada_kernel,
        grid=(DEPTH, 6 * D // tn),
        in_specs=[pl.BlockSpec((8, D), lambda l, j: (0, 0)),
                  pl.BlockSpec((None, D, tn), lambda l, j: (l, 0, j)),
                  pl.BlockSpec((None, 1, tn), lambda l, j: (l, 0, j))],
        out_specs=pl.BlockSpec((None, 8, tn), lambda l, j: (l, 0, j)),
        out_shape=jax.ShapeDtypeStruct((DEPTH, 8, 6 * D), F32),
        compiler_params=_cparams(("arbitrary", "arbitrary")),
        name="ada_params",
    )(cond8, mod_w, mod_b.reshape(DEPTH, 1, 6 * D))


def _token_specs(x):
    nct = NC // TM
    if isinstance(x, tuple):
        return ([pl.BlockSpec((TM, x[0].shape[1]), lambda i: (jnp.minimum(i, nct - 1), 0)),
                 pl.BlockSpec((TM, x[1].shape[1]), lambda i: (jnp.maximum(i - nct, 0), 0))], list(x))
    return [pl.BlockSpec((TM, x.shape[1]), lambda i: (i, 0))], [x]


def _token_rows(refs):
    if len(refs) == 1:
        return refs[0][...]
    return jnp.where(pl.program_id(0) < NC // TM, refs[0][...], refs[1][...])


def _modproj_kernel(*refs, n_chunk):
    m_ref, g_ref, w_ref, o_ref = refs[-4:]
    hb = _modulated_norm(_token_rows(refs[:-4]), m_ref, g_ref, 0, 1).astype(BF16)
    for j in range(o_ref.shape[1] // n_chunk):
        sl = slice(j * n_chunk, (j + 1) * n_chunk)
        o_ref[:, sl] = jnp.dot(hb, w_ref[:, sl], preferred_element_type=F32)


def _modproj(x, m3, gain, w_bf16, n_chunk):
    nout = w_bf16.shape[1]
    x_specs, x_args = _token_specs(x)
    return pl.pallas_call(
        functools.partial(_modproj_kernel, n_chunk=n_chunk),
        grid=(N // TM,),
        in_specs=x_specs + [pl.BlockSpec((None, 6, D), lambda i: (_cond_row(i), 0, 0)),
                            pl.BlockSpec((1, D), lambda i: (0, 0)),
                            pl.BlockSpec((D, nout), lambda i: (0, 0))],
        out_specs=pl.BlockSpec((TM, nout), lambda i: (i, 0)),
        out_shape=jax.ShapeDtypeStruct((N, nout), F32),
        compiler_params=_cparams(("arbitrary",)),
        name="modproj",
    )(*x_args, m3, gain, w_bf16)


def _deltanet_kernel(q_ref, k_ref, v_ref, z_ref, ab_ref, cq_ref, ck_ref, cv_ref, gp_ref, og_ref,
                     s0f_ref, s0b_ref, tri_ref, jbd_ref, o_ref, sf_ref, sb_ref,
                     u_s, w_s, qd_s, at_s, kt_s, ge_s, st_s, ob, *, seq_len, n_sub):
    hp = pl.program_id(1)
    C = CHUNK
    nc = seq_len // C
    nct = n_sub * nc
    P = LANES
    lane = lax.broadcasted_iota(jnp.int32, (C, P), 1)
    row = lax.broadcasted_iota(jnp.int32, (C, P), 0)
    first_head = lane < A_DH
    ri = lax.broadcasted_iota(jnp.int32, (P, P), 0)
    ci = lax.broadcasted_iota(jnp.int32, (P, P), 1)
    same_head = (ri < C) == (ci < C)
    same_blk = (ri // SOLVE_BLK) == (ci // SOLVE_BLK)
    eye = jnp.where(ri == ci, 1.0, 0.0)
    jbd = jbd_ref[...]
    lincl = tri_ref[...]
    cum_b = [lincl.T.astype(BF16), lincl.astype(BF16)]
    incl_m = [jnp.logical_and(same_head, ri >= ci), jnp.logical_and(same_head, ri <= ci)]
    strict_m = [jnp.logical_and(same_head, ri > ci), jnp.logical_and(same_head, ri < ci)]
    neg_a = -jnp.exp(gp_ref[0])
    dt_b = gp_ref[1]

    def conv_silu(ref, w_ref, c):
        base = pl.multiple_of(c * C, C)
        cs = c % nc
        xc = ref[pl.ds(base, C), :]
        pbase = pl.multiple_of(jnp.maximum(base - 8, 0), 8)
        nbase = pl.multiple_of(jnp.minimum(base + C, n_sub * seq_len - 8), 8)
        prev_row = ref[pl.ds(pbase, 8), :][7:8, :] * jnp.where(cs > 0, 1.0, 0.0)
        next_row = ref[pl.ds(nbase, 8), :][0:1, :] * jnp.where(cs < nc - 1, 1.0, 0.0)
        x_prev = jnp.where(row == 0, prev_row, pltpu.roll(xc, 1, 0))
        x_next = jnp.where(row == C - 1, next_row, pltpu.roll(xc, C - 1, 0))
        y = w_ref[0:1, :] * x_prev + w_ref[1:2, :] * xc + w_ref[2:3, :] * x_next
        return _silu(y)

    def stack(x):
        return jnp.concatenate([jnp.where(first_head, x, 0.0), jnp.where(first_head, 0.0, x)], axis=0)

    def chunk_inputs(c):
        base = pl.multiple_of(c * C, C)
        q = conv_silu(q_ref, cq_ref, c)
        k = conv_silu(k_ref, ck_ref, c)
        v = conv_silu(v_ref, cv_ref, c)
        return q, k, v, ab_ref[pl.ds(base, C), :].T

    sub8 = lax.broadcasted_iota(jnp.int32, (A_HEADS, C), 0)

    def pair_row(x8):
        r0 = jnp.sum(jnp.where(sub8 == 2 * hp, x8, 0.0), axis=0, keepdims=True)
        r1 = jnp.sum(jnp.where(sub8 == 2 * hp + 1, x8, 0.0), axis=0, keepdims=True)
        return jnp.concatenate([r0, r1], axis=1)

    def chain_gates(ab_t, d):
        a8 = ab_t[d * A_HEADS:(d + 1) * A_HEADS, :]
        b8 = ab_t[(2 + d) * A_HEADS:(3 + d) * A_HEADS, :]
        g8 = neg_a[d * A_HEADS:(d + 1) * A_HEADS, :C] * jax.nn.softplus(a8 + dt_b[d * A_HEADS:(d + 1) * A_HEADS, :C])
        gc8 = _mm_split(g8, cum_b[d], 3)
        tot8 = jnp.broadcast_to(jnp.sum(g8, axis=-1, keepdims=True), (A_HEADS, C))
        gc_row, beta_row, tot_row = pair_row(gc8), pair_row(jax.nn.sigmoid(b8)), pair_row(tot8)
        cols = jnp.concatenate([gc_row, beta_row, tot_row, jnp.zeros((5, P), F32)], axis=0).T
        return gc_row, tot_row, cols[:, 0:1], cols[:, 1:2], cols[:, 2:3]

    def terms_body(j, carry):
        cs = [TERM_UNROLL * j + t for t in range(TERM_UNROLL)]
        ins = [chunk_inputs(c) for c in cs]
        qsq = [_mm_split(x[0] * x[0], jbd, 2) for x in ins]
        ksq = [_mm_split(x[1] * x[1], jbd, 2) for x in ins]
        qs = [x[0] * lax.rsqrt(s + EPS) * (A_DH ** -0.5) for x, s in zip(ins, qsq)]
        ks = [x[1] * lax.rsqrt(s + EPS) for x, s in zip(ins, ksq)]
        qst = [stack(x) for x in qs]
        kst = [stack(x) for x in ks]
        vst = [stack(x[2]) for x in ins]
        kst_t = [x.T for x in kst]
        kk = [_mm_nt(x, x) for x in kst]
        qk = [_mm_nt(x, y) for x, y in zip(qst, kst)]
        chains = [(t, d) for t in range(TERM_UNROLL) for d in range(2)]
        gates = [chain_gates(ins[t][3], d) for t, d in chains]
        decay, e_gc = [], []
        for (t, d), (gc_row, tot_row, gc_col, beta_col, tot_col) in zip(chains, gates):
            diff = jnp.broadcast_to(gc_col, (P, P)) - jnp.broadcast_to(gc_row, (P, P))
            decay.append(jnp.where(incl_m[d], jnp.exp(jnp.where(incl_m[d], diff, 0.0)), 0.0))
            e_gc.append(jnp.exp(gc_col))
        a_mats = [jnp.where(strict_m[d], g[3] * kk[t] * dc, 0.0) for (t, d), g, dc in zip(chains, gates, decay)]
        rhs = [vst[t] * g[3] + pltpu.roll(kst[t] * (g[3] * e), A_DH, 1)
               for (t, d), g, e in zip(chains, gates, e_gc)]
        xs = _unit_lower_solve_many(a_mats, rhs, same_blk, eye)
        for (t, d), x, g, e, dc in zip(chains, xs, gates, e_gc, decay):
            c = cs[t]
            gc_row, tot_row = g[0], g[1]
            u_s[d, c] = jnp.where(same_head, x, 0.0).astype(BF16)
            w_s[d, c] = pltpu.roll(jnp.where(same_head, 0.0, x), A_DH, 1).astype(BF16)
            qd_s[d, c] = (qst[t] * e).astype(BF16)
            at_s[d, c] = jnp.where(incl_m[d], qk[t] * dc, 0.0).astype(BF16)
            kt_s[d, c] = (kst_t[t] * jnp.exp(tot_row - gc_row)).astype(BF16)
            ge_s[d, c] = jnp.broadcast_to(jnp.exp(tot_row), (8, P))
        return carry

    lax.fori_loop(0, nct // TERM_UNROLL, terms_body, 0)

    def block_diag(s2):
        z = jnp.zeros((A_DH, A_DH), F32)
        return jnp.concatenate([jnp.concatenate([s2[0], z], axis=1),
                                jnp.concatenate([z, s2[1]], axis=1)], axis=0)

    for s in range(n_sub):
        st_s[2 * s] = block_diag(s0f_ref[s])
        st_s[2 * s + 1] = block_diag(s0b_ref[s])

    def scan_body(i, carry):
        chains = [(s, d, s * nc + (i if d == 0 else nc - 1 - i)) for s in range(n_sub) for d in range(2)]
        dot = functools.partial(jnp.dot, preferred_element_type=F32)
        s_bd = [st_s[2 * s + d] for s, d, c in chains]
        sb16 = [x.astype(BF16) for x in s_bd]
        ws = [dot(w_s[d, c], sb) for (s, d, c), sb in zip(chains, sb16)]
        qs_ = [dot(qd_s[d, c], sb) for (s, d, c), sb in zip(chains, sb16)]
        vb = [(u_s[d, c].astype(F32) - x).astype(BF16) for (s, d, c), x in zip(chains, ws)]
        av = [dot(at_s[d, c], x) for (s, d, c), x in zip(chains, vb)]
        kv = [dot(kt_s[d, c], x) for (s, d, c), x in zip(chains, vb)]
        for (s, d, c), sb, q_, a_, k_ in zip(chains, s_bd, qs_, av, kv):
            st_s[2 * s + d] = sb * ge_s[d, c][0:1, :] + k_
            o_st = q_ + a_
            dst = o_ref if d == 0 else ob
            dst[pl.ds(pl.multiple_of(c * C, C), C), :] = o_st[:C] + o_st[C:]
        return carry

    lax.fori_loop(0, nc, scan_body, 0)

    for s in range(n_sub):
        for d, ref in ((0, sf_ref), (1, sb_ref)):
            s_bd = st_s[2 * s + d]
            ref[s, 0] = s_bd[:A_DH, :A_DH]
            ref[s, 1] = s_bd[A_DH:, A_DH:]

    def finish(j, carry):
        bases = [pl.multiple_of((TERM_UNROLL * j + t) * C, C) for t in range(TERM_UNROLL)]
        o = [o_ref[pl.ds(b, C), :] + ob[pl.ds(b, C), :] for b in bases]
        ms = [_mm_split(x * x, jbd, 2) * (1.0 / A_DH) for x in o]
        for b, x, m in zip(bases, o, ms):
            o_ref[pl.ds(b, C), :] = x * lax.rsqrt(m + EPS) * og_ref[...] * _silu(z_ref[pl.ds(b, C), :])
        return carry

    lax.fori_loop(0, nct // TERM_UNROLL, finish, 0)


def _deltanet(proj, conv_w, gate_p, o_gain2, s0f, s0b, tri, jbd, *, seq_len, n_seq, n_sub, row_blk0):
    rows = n_sub * seq_len
    nct = rows // CHUNK
    rb = lambda b: row_blk0 + b
    col = lambda off: (lambda b, hp: (rb(b), off + hp))
    st_spec = pl.BlockSpec((n_sub, 2, A_DH, A_DH), lambda b, hp: (b, hp, 0, 0))
    in_specs = [pl.BlockSpec((rows, LANES), col(0)),
                pl.BlockSpec((rows, LANES), col(4)),
                pl.BlockSpec((rows, LANES), col(8)),
                pl.BlockSpec((rows, LANES), col(12)),
                pl.BlockSpec((rows, LANES), lambda b, hp: (rb(b), 20)),
                pl.BlockSpec((3, LANES), lambda b, hp: (0, hp)),
                pl.BlockSpec((3, LANES), lambda b, hp: (0, 4 + hp)),
                pl.BlockSpec((3, LANES), lambda b, hp: (0, 8 + hp)),
                pl.BlockSpec((2, 2 * A_HEADS, LANES), lambda b, hp: (0, 0, 0)),
                pl.BlockSpec((1, LANES), lambda b, hp: (0, 0)),
                st_spec, st_spec,
                pl.BlockSpec((CHUNK, CHUNK), lambda b, hp: (0, 0)),
                pl.BlockSpec((LANES, LANES), lambda b, hp: (0, 0))]
    args = [proj, proj, proj, proj, proj, conv_w, conv_w, conv_w, gate_p, o_gain2, s0f, s0b, tri, jbd]
    st_shape = jax.ShapeDtypeStruct((n_seq, A_HEADS, A_DH, A_DH), F32)
    tile = lambda dt: pltpu.VMEM((2, nct, LANES, LANES), dt)
    return pl.pallas_call(
        functools.partial(_deltanet_kernel, seq_len=seq_len, n_sub=n_sub),
        grid=(n_seq // n_sub, A_HEADS // 2),
        in_specs=in_specs,
        out_specs=[pl.BlockSpec((rows, LANES), lambda b, hp: (b, hp)), st_spec, st_spec],
        out_shape=[jax.ShapeDtypeStruct((n_seq * seq_len, A_WIDTH), F32), st_shape, st_shape],
        scratch_shapes=[tile(BF16), tile(BF16), tile(BF16), tile(BF16), tile(BF16),
                        pltpu.VMEM((2, nct, 8, LANES), F32),
                        pltpu.VMEM((2 * n_sub, LANES, LANES), F32),
                        pltpu.VMEM((rows, LANES), F32)],
        compiler_params=_cparams(("arbitrary", "arbitrary")),
        name="deltanet",
    )(*args)


def _dft_mats(n):
    idx = np.arange(n)
    ang = 2.0 * np.pi * ((idx[:, None] * idx[None, :]) % n) / n
    return np.cos(ang), np.sin(ang)


def _fnet_ctx_kernel(u_ref, cs_ref, dft_ref, o_ref):
    norm = 1.0 / math.sqrt(SEQ * LANES)
    sls = [slice(g * LANES, (g + 1) * LANES) for g in range(B_GROUPS)]
    p = [_mm(u_ref[:, sl], cs_ref[...]) for sl in sls]
    stack = [jnp.concatenate([x[:, :LANES], x[:, LANES:]], axis=0) for x in p]
    y = [_mm(dft_ref[...], x) for x in stack]
    for sl, x in zip(sls, y):
        o_ref[:, sl] = x * norm


def _fnet_ctx(proj, cs, dft):
    return pl.pallas_call(
        _fnet_ctx_kernel,
        grid=(BATCH,),
        in_specs=[pl.BlockSpec((SEQ, B_WIDTH), lambda b: (b, 4)),
                  pl.BlockSpec(cs.shape, lambda b: (0, 0)),
                  pl.BlockSpec(dft.shape, lambda b: (0, 0))],
        out_specs=pl.BlockSpec((SEQ, B_WIDTH), lambda b: (b, 0)),
        out_shape=jax.ShapeDtypeStruct((NC, B_WIDTH), F32),
        compiler_params=_cparams(("arbitrary",)),
        name="fnet_ctx",
    )(proj, cs, dft)


FN_SUB = 4


def _fnet_lat1_kernel(u_ref, ca_ref, m1_ref, twc_ref, tws_ref, o_ref):
    r = 64
    units = [(j, g) for j in range(FN_SUB) for g in range(B_GROUPS)]
    rows = lambda j: slice(j * r, (j + 1) * r)
    pa = [_mm(u_ref[rows(j), g * LANES:(g + 1) * LANES], ca_ref[...]) for j, g in units]
    rhs = [jnp.concatenate([x[:, :2 * LANES], x[:, 2 * LANES:]], axis=0) for x in pa]
    zz = [_mm(m1_ref[...], x) for x in rhs]
    for (j, g), z in zip(units, zz):
        c = twc_ref[j]
        s = tws_ref[j]
        zr, zi = z[:, :LANES], z[:, LANES:]
        o_ref[rows(j), 2 * g * LANES:(2 * g + 1) * LANES] = (zr * c + zi * s).astype(BF16)
        o_ref[rows(j), (2 * g + 1) * LANES:(2 * g + 2) * LANES] = (zi * c - zr * s).astype(BF16)


def _fnet_lat2_kernel(z_ref, m1_ref, o_ref):
    r = 64
    norm = 1.0 / math.sqrt(DEC_SEQ * LANES)
    units = [(j, g) for j in range(FN_SUB) for g in range(B_GROUPS)]
    rows = lambda j: slice(j * r, (j + 1) * r)
    rhs = [jnp.concatenate([z_ref[rows(j), 2 * g * LANES:(2 * g + 1) * LANES],
                            z_ref[rows(j), (2 * g + 1) * LANES:(2 * g + 2) * LANES]], axis=0) for j, g in units]
    y = [_mm(m1_ref[...], x) for x in rhs]
    for (j, g), x in zip(units, y):
        o_ref[rows(j), g * LANES:(g + 1) * LANES] = x * norm


def _fnet_latent(proj, ca, m1, twc, tws):
    r = 64
    u = proj[NC:, 2048:2560].astype(BF16).reshape(DEC_BATCH, r, r, B_WIDTH)
    u = u.transpose(0, 2, 1, 3).reshape(DEC_BATCH * r * r, B_WIDTH)
    steps = DEC_BATCH * r // FN_SUB
    blk = FN_SUB * r
    z = pl.pallas_call(
        _fnet_lat1_kernel,
        grid=(steps,),
        in_specs=[pl.BlockSpec((blk, B_WIDTH), lambda s: (s, 0)),
                  pl.BlockSpec(ca.shape, lambda s: (0, 0)),
                  pl.BlockSpec(m1.shape, lambda s: (0, 0)),
                  pl.BlockSpec((FN_SUB, r, LANES), lambda s: (s % (r // FN_SUB), 0, 0)),
                  pl.BlockSpec((FN_SUB, r, LANES), lambda s: (s % (r // FN_SUB), 0, 0))],
        out_specs=pl.BlockSpec((blk, 2 * B_WIDTH), lambda s: (s, 0)),
        out_shape=jax.ShapeDtypeStruct((NL, 2 * B_WIDTH), BF16),
        compiler_params=_cparams(("arbitrary",)),
        name="fnet_lat1",
    )(u, ca, m1, twc, tws)
    z = z.reshape(DEC_BATCH, r, r, 2 * B_WIDTH).transpose(0, 2, 1, 3).reshape(NL, 2 * B_WIDTH)
    y = pl.pallas_call(
        _fnet_lat2_kernel,
        grid=(steps,),
        in_specs=[pl.BlockSpec((blk, 2 * B_WIDTH), lambda s: (s, 0)),
                  pl.BlockSpec(m1.shape, lambda s: (0, 0))],
        out_specs=pl.BlockSpec((blk, B_WIDTH), lambda s: (s, 0)),
        out_shape=jax.ShapeDtypeStruct((NL, B_WIDTH), F32),
        compiler_params=_cparams(("arbitrary",)),
        name="fnet_lat2",
    )(z, m1)
    return y.reshape(DEC_BATCH, r, r, B_WIDTH).transpose(0, 2, 1, 3).reshape(NL, B_WIDTH)


def _head_masks():
    lane = lax.broadcasted_iota(jnp.int32, (1, LANES), 1)
    return lane < NA_DH


def _attend_many(chains):
    s = [[_mm_nt(q, k) if b is None else _mm_nt(q, k) + b for k, v, b in kv] for q, kv in chains]
    m = [functools.reduce(jnp.maximum, [jnp.max(x, axis=-1, keepdims=True) for x in xs]) for xs in s]
    p = [[jnp.exp(x - mi) for x in xs] for xs, mi in zip(s, m)]
    l = [sum(jnp.sum(x, axis=-1, keepdims=True) for x in xs) for xs in p]
    o = [sum(_mm(x, v) for x, (k, v, b) in zip(xs, kv)) for xs, (q, kv) in zip(p, chains)]
    return [oi / li for oi, li in zip(o, l)]


def _na_ctx_kernel(q_ref, k_ref, v_ref, o_ref):
    first = _head_masks()
    n_split = 2
    rows = SEQ // n_split
    chains = []
    for p in range(NA_CTX_PAIRS):
        lanes = slice(p * LANES, (p + 1) * LANES)
        k = k_ref[:, lanes].astype(BF16)
        v = v_ref[:, lanes].astype(BF16)
        for j in range(n_split):
            q = q_ref[j * rows:(j + 1) * rows, lanes]
            for a in range(2):
                chains.append((jnp.where(first if a == 0 else jnp.logical_not(first), q * NA_SCALE, 0.0),
                               [(k, v, None)]))
    outs = _attend_many(chains)
    for p in range(NA_CTX_PAIRS):
        for j in range(n_split):
            i = 2 * (p * n_split + j)
            o_ref[j * rows:(j + 1) * rows, p * LANES:(p + 1) * LANES] = jnp.where(first, outs[i], outs[i + 1])


def _na_ctx(qkv):
    w = NA_CTX_PAIRS * LANES
    nblk = D // w
    return pl.pallas_call(
        _na_ctx_kernel,
        grid=(BATCH, nblk),
        in_specs=[pl.BlockSpec((SEQ, w), lambda b, hp: (b, hp)),
                  pl.BlockSpec((SEQ, w), lambda b, hp: (b, nblk + hp)),
                  pl.BlockSpec((SEQ, w), lambda b, hp: (b, 2 * nblk + hp))],
        out_specs=pl.BlockSpec((SEQ, w), lambda b, hp: (b, hp)),
        out_shape=jax.ShapeDtypeStruct((NC, D), F32),
        compiler_params=_cparams(("arbitrary", "arbitrary")),
        name="na_ctx",
    )(qkv, qkv, qkv)


def _na_lat_kernel(q_ref, k_ref, v_ref, kc_ref, vc_ref, tt_ref, o_ref):
    first = _head_masks()
    rows = DEC_SEQ // GRID_W
    kctx = kc_ref[...].astype(BF16)
    vctx = vc_ref[...].astype(BF16)
    nkeys = WIN_R * GRID_W

    def rows_body(j, carry):
        chains = []
        for t in range(NA_ROW_UNROLL):
            r = NA_ROW_UNROLL * j + t
            r0 = jnp.clip(r - WIN_R // 2, 0, rows - WIN_R)
            dr0 = r0 - r + (WIN_R - 1)
            q = q_ref[pl.ds(pl.multiple_of(r * GRID_W, GRID_W), GRID_W), :] * NA_SCALE
            kbase = pl.multiple_of(r0 * GRID_W, GRID_W)
            kl = k_ref[pl.ds(kbase, nkeys), :].astype(BF16)
            vl = v_ref[pl.ds(kbase, nkeys), :].astype(BF16)
            for a in range(2):
                qm = jnp.where(first if a == 0 else jnp.logical_not(first), q, 0.0)
                bias = jnp.concatenate([tt_ref[a, dr0 + 2 * i] for i in range(WIN_R // 2)], axis=1)
                chains.append((qm, [(kl, vl, bias), (kctx, vctx, None)]))
        outs = _attend_many(chains)
        for t in range(NA_ROW_UNROLL):
            r = NA_ROW_UNROLL * j + t
            o_ref[pl.ds(pl.multiple_of(r * GRID_W, GRID_W), GRID_W), :] = jnp.where(first, outs[2 * t], outs[2 * t + 1])
        return carry

    lax.fori_loop(0, rows // NA_ROW_UNROLL, rows_body, 0)


def _na_latent(qkv, cache_k2, cache_v2, tt2):
    rb0 = NC // DEC_SEQ
    return pl.pallas_call(
        _na_lat_kernel,
        grid=(DEC_BATCH, NA_HEADS // 2),
        in_specs=[pl.BlockSpec((DEC_SEQ, LANES), lambda b, hp: (rb0 + b, hp)),
                  pl.BlockSpec((DEC_SEQ, LANES), lambda b, hp: (rb0 + b, 8 + hp)),
                  pl.BlockSpec((DEC_SEQ, LANES), lambda b, hp: (rb0 + b, 16 + hp)),
                  pl.BlockSpec((None, 256, LANES), lambda b, hp: (b, 0, hp)),
                  pl.BlockSpec((None, 256, LANES), lambda b, hp: (b, 0, hp)),
                  pl.BlockSpec((2, 2 * WIN_R - 2, GRID_W, LANES), lambda b, hp: (hp, 0, 0, 0))],
        out_specs=pl.BlockSpec((DEC_SEQ, LANES), lambda b, hp: (b, hp)),
        out_shape=jax.ShapeDtypeStruct((NL, D), F32),
        compiler_params=_cparams(("arbitrary", "arbitrary")),
        name="na_latent",
    )(qkv, qkv, qkv, cache_k2, cache_v2, tt2)


def _rpb_tables(rpb):
    col = np.arange(GRID_W)
    start = np.clip(col - WIN_C // 2, 0, GRID_W - WIN_C)
    inside = (col[None, :] >= start[:, None]) & (col[None, :] < start[:, None] + WIN_C)
    w = GRID_W
    period = 2 * w - 1
    x = jnp.pad(rpb, ((0, 0), (0, 0), (w - WIN_C, w - WIN_C)))
    flat = jnp.tile(x, (1, 1, w))[:, :, w - 1:w - 1 + w * (period - 1)]
    t = flat.reshape(NA_HEADS, 2 * WIN_R - 1, w, period - 1)[..., :w]
    t = jnp.where(inside[None, None], t, NEG)
    return jnp.concatenate([t[:, :-1], t[:, 1:]], axis=-1)


ROUTE_SPLIT = 2
ROUTE_ROWS = 40


def _router_kernel(*refs, n_in):
    a_refs = refs[:2 * n_in]
    w_refs = refs[2 * n_in:3 * n_in]
    x_refs = refs[3 * n_in:-11]
    m_ref, g_ref, wr_ref, br_ref, tri_ref, xn_ref, xf_ref, ri_ref, rw_ref, cnt_ref, base_scr = refs[-11:]
    i = pl.program_id(0)

    @pl.when(i == 0)
    def _():
        base_scr[...] = jnp.zeros_like(base_scr)
        cnt_ref[...] = jnp.zeros_like(cnt_ref)

    rows = [slice(t * TM // ROUTE_SPLIT, (t + 1) * TM // ROUTE_SPLIT) for t in range(ROUTE_SPLIT)]
    is_ctx = pl.program_id(0) < NC // TM
    pick = lambda pair, r: jnp.where(is_ctx, pair[0][r, :], pair[1][r, :]) if len(pair) == 2 else pair[0][r, :]
    acc = [sum(_mm(pick(a_refs[2 * j:2 * j + 2], r), w_ref[...]) for j, w_ref in enumerate(w_refs)) for r in rows]
    x_new = [pick(x_refs, r) + m_ref[2:3, :] * a for r, a in zip(rows, acc)]
    for r, xn in zip(rows, x_new):
        xn_ref[r, :] = xn
    h = [_modulated_norm(xn, m_ref, g_ref, 3, 4) for xn in x_new]
    for r, hh in zip(rows, h):
        xf_ref[r, :] = _pack_bf16_pairs(hh[:, :D // 2], hh[:, D // 2:])
    logits = jnp.concatenate(
        [lax.dot_general(wr_ref[...], hh, (((1,), (1,)), ((), ())), preferred_element_type=F32, precision=HIGHEST)
         for hh in h], axis=1) + br_ref[:, 0:1]
    row = lax.broadcasted_iota(jnp.int32, logits.shape, 0)
    cmax = lambda x: jnp.max(x, axis=0, keepdims=True)
    cmin = lambda x: jnp.min(x, axis=0, keepdims=True)
    csum = lambda x: jnp.sum(x, axis=0, keepdims=True)

    gmask = row < N_GROUPS
    mg = cmax(jnp.where(gmask, logits, NEG))
    eg = jnp.where(gmask, jnp.exp(jnp.where(gmask, logits - mg, NEG)), 0.0)
    pg = eg / csum(eg)
    p_grp = cmax(pg)
    grp = cmin(jnp.where(jnp.logical_and(gmask, pg == p_grp), row, ROUTE_ROWS))
    lo = N_GROUPS + grp * EXP_PER_GROUP
    emask = jnp.logical_and(row >= lo, row < lo + EXP_PER_GROUP)
    me = cmax(jnp.where(emask, logits, NEG))
    ee = jnp.where(emask, jnp.exp(jnp.where(emask, logits - me, NEG)), 0.0)
    pe = ee / csum(ee)
    p1 = cmax(pe)
    i1 = cmin(jnp.where(jnp.logical_and(emask, pe == p1), row, ROUTE_ROWS))
    m2 = jnp.logical_and(emask, row != i1)
    p2 = cmax(jnp.where(m2, pe, -1.0))
    i2 = cmin(jnp.where(jnp.logical_and(m2, pe == p2), row, ROUTE_ROWS))
    den = p1 + p2
    w1 = p_grp * p1 / den
    w2 = p_grp * p2 / den

    sel1 = row == i1
    sel2 = row == i2
    oh = jnp.where(jnp.logical_or(sel1, sel2), 1.0, 0.0).astype(BF16)
    before = jnp.dot(oh, tri_ref[...], preferred_element_type=F32) + base_scr[:, 0:1]
    rank1 = csum(jnp.where(sel1, before, 0.0))
    rank2 = csum(jnp.where(sel2, before, 0.0))
    base_scr[...] = base_scr[...] + jnp.sum(oh.astype(F32), axis=1, keepdims=True)
    cnt_ref[...] = cnt_ref[...] + lax.dot_general(jnp.ones((8, TM), BF16), oh, (((1,), (1,)), ((), ())),
                                                  preferred_element_type=F32)
    sub = lax.broadcasted_iota(jnp.int32, (8, TM), 0)
    ri_ref[...] = jnp.where(sub == 0, i1 - N_GROUPS, jnp.where(sub == 1, i2 - N_GROUPS,
                  jnp.where(sub == 2, rank1.astype(jnp.int32), jnp.where(sub == 3, rank2.astype(jnp.int32), 0))))
    rw_ref[...] = jnp.where(sub == 0, w1, jnp.where(sub == 1, w2, 0.0))


def _router(a_pairs, w_list, x, m3, gain, wr_t, br_t, tri_upper):
    in_specs, args = [], []
    for pair in a_pairs:
        specs, ops = _token_specs(pair)
        in_specs += specs
        args += ops
    x_specs, x_args = _token_specs(x)
    in_specs += ([pl.BlockSpec(w.shape, lambda i: (0, 0)) for w in w_list] + x_specs
                 + [pl.BlockSpec((None, 6, D), lambda i: (_cond_row(i), 0, 0)),
                    pl.BlockSpec((1, D), lambda i: (0, 0)),
                    pl.BlockSpec((ROUTE_ROWS, D), lambda i: (0, 0)),
                    pl.BlockSpec((ROUTE_ROWS, LANES), lambda i: (0, 0)),
                    pl.BlockSpec((TM, TM), lambda i: (0, 0))])
    return pl.pallas_call(
        functools.partial(_router_kernel, n_in=len(a_pairs)),
        grid=(N // TM,),
        in_specs=in_specs,
        out_specs=[pl.BlockSpec((TM, D), lambda i: (i, 0)),
                   pl.BlockSpec((TM, D // 2), lambda i: (i, 0)),
                   pl.BlockSpec((8, TM), lambda i: (0, i)),
                   pl.BlockSpec((8, TM), lambda i: (0, i)),
                   pl.BlockSpec((8, ROUTE_ROWS), lambda i: (0, 0))],
        out_shape=[jax.ShapeDtypeStruct((N, D), F32),
                   jax.ShapeDtypeStruct((N, D // 2), jnp.int32),
                   jax.ShapeDtypeStruct((8, N), jnp.int32),
                   jax.ShapeDtypeStruct((8, N), F32),
                   jax.ShapeDtypeStruct((8, ROUTE_ROWS), F32)],
        scratch_shapes=[pltpu.VMEM((ROUTE_ROWS, LANES), F32)],
        compiler_params=_cparams(("arbitrary",)),
        name="router",
    )(*args, *w_list, *x_args, m3, gain, wr_t, br_t, tri_upper)


def _row_copy(src_hbm, row, dst, r, sem):
    return pltpu.make_async_copy(src_hbm.at[pl.ds(row, 1)], dst.at[pl.ds(r, 1)], sem)


def _expert_kernel(be_ref, nu_ref, d_ref, pad_ref, xp_hbm, wg_ref, wu_ref, wd_ref, ys_ref, xres, xbuf, st_ref, sem):
    del be_ref
    b = pl.program_id(0)
    half = D // 2

    def gather(blk, slot, part=0, parts=1):
        base = blk * MOE_ROWS
        for r in range(part * MOE_ROWS // parts, (part + 1) * MOE_ROWS // parts):
            xbuf[slot, pl.ds(r, 1), :] = xres[pl.ds(st_ref[base + r], 1), :]

    @pl.when(b == 0)
    def _():
        cp = pltpu.make_async_copy(xp_hbm, xres, sem.at[0])
        cp.start()

        def clear_tail(e, c):
            def clear(s_, c2):
                st_ref[s_] = 0
                return c2

            return lax.fori_loop(pad_ref[2 * e], pad_ref[2 * e + 1], clear, c)

        lax.fori_loop(0, N_EXPERTS + 1, clear_tail, 0)

        def invert(j, c):
            slots = [d_ref[16 * j + t] for t in range(16)]
            for t in range(16):
                st_ref[slots[t]] = 8 * j + t // 2
            return c

        lax.fori_loop(0, 2 * N // 16, invert, 0)
        cp.wait()
        gather(0, 0)

    @pl.when(b < nu_ref[0])
    def _():
        nxt = (jnp.minimum(b + 1, N_SLOT_BLOCKS - 1), (b + 1) % 2)
        x_lo, x_hi = _unpack_bf16_pairs(xbuf[b % 2])
        dot = functools.partial(jnp.dot, preferred_element_type=F32)
        gather(*nxt, 0, 8)
        g = dot(x_lo, wg_ref[:half, :].astype(BF16))
        gather(*nxt, 1, 8)
        g = g + dot(x_hi, wg_ref[half:, :].astype(BF16))
        gather(*nxt, 2, 8)
        u = dot(x_lo, wu_ref[:half, :].astype(BF16))
        gather(*nxt, 3, 8)
        u = u + dot(x_hi, wu_ref[half:, :].astype(BF16))
        hb = (_silu(g) * u).astype(BF16)
        quarter = D // 4
        for j in range(4):
            gather(*nxt, 4 + j, 8)
            cols = slice(j * quarter, (j + 1) * quarter)
            ys_ref[:, cols] = jnp.dot(hb, wd_ref[:, cols].astype(BF16), preferred_element_type=F32)

    @pl.when(b >= nu_ref[0])
    def _():
        ys_ref[...] = jnp.zeros_like(ys_ref)


def _experts(block_e, n_used, dest, pad_ranges, xp, w_gate, w_up, w_down, layer):
    grid_spec = pltpu.PrefetchScalarGridSpec(
        num_scalar_prefetch=4,
        grid=(N_SLOT_BLOCKS,),
        in_specs=[pl.BlockSpec(memory_space=pl.ANY),
                  pl.BlockSpec((None, None, D, D_EXPERT), lambda b, be, nu, st, pd: (layer, be[b], 0, 0)),
                  pl.BlockSpec((None, None, D, D_EXPERT), lambda b, be, nu, st, pd: (layer, be[b], 0, 0)),
                  pl.BlockSpec((None, None, D_EXPERT, D), lambda b, be, nu, st, pd: (layer, be[b], 0, 0))],
        out_specs=pl.BlockSpec((MOE_ROWS, D), lambda b, be, nu, st, pd: (b, 0)),
        scratch_shapes=[pltpu.VMEM((N, D // 2), jnp.int32),
                        pltpu.VMEM((2, MOE_ROWS, D // 2), jnp.int32),
                        pltpu.SMEM((N_SLOT_BLOCKS * MOE_ROWS,), jnp.int32),
                        pltpu.SemaphoreType.DMA((1,))])
    return pl.pallas_call(
        _expert_kernel,
        grid_spec=grid_spec,
        out_shape=jax.ShapeDtypeStruct((N_SLOT_BLOCKS * MOE_ROWS, D), F32),
        compiler_params=pltpu.CompilerParams(dimension_semantics=("arbitrary",), vmem_limit_bytes=EXPERT_VMEM_LIMIT),
        name="experts",
    )(block_e, n_used, dest, pad_ranges, xp, w_gate, w_up, w_down)


def _combine_kernel(d_ref, ys_hbm, x_ref, m_ref, rw_ref, fn_ref, o_ref, buf, sem, *, final, tile0, n_tiles):
    i = pl.program_id(0)

    def issue(tile, slot):
        base = (tile0 + tile) * (2 * TM)
        for r in range(TM):
            for kk in range(2):
                _row_copy(ys_hbm, d_ref[base + 2 * r + kk], buf.at[slot, kk], r, sem.at[slot]).start()

    @pl.when(i == 0)
    def _():
        issue(0, 0)

    @pl.when(i + 1 < n_tiles)
    def _():
        issue(i + 1, (i + 1) % 2)

    slot = i % 2
    for kk in range(2):
        pltpu.make_async_copy(ys_hbm.at[pl.ds(0, TM)], buf.at[slot, kk], sem.at[slot]).wait()
    w = rw_ref[...]
    y = w[:, 0:1] * buf[slot, 0] + w[:, 1:2] * buf[slot, 1]
    out = x_ref[...] + m_ref[5:6, :] * y
    if final:
        ms = jnp.mean(out * out, axis=-1, keepdims=True)
        out = out * lax.rsqrt(ms + EPS) * fn_ref[...]
    o_ref[...] = out


def _combine(dest_flat, ys, x, m3, rw, final_norm, final, tile0=0, n_tiles=N // TM):
    grid_spec = pltpu.PrefetchScalarGridSpec(
        num_scalar_prefetch=1,
        grid=(n_tiles,),
        in_specs=[pl.BlockSpec(memory_space=pl.ANY),
                  pl.BlockSpec((TM, D), lambda i, d: (tile0 + i, 0)),
                  pl.BlockSpec((None, 6, D), lambda i, d: (_cond_row(tile0 + i), 0, 0)),
                  pl.BlockSpec((TM, 2), lambda i, d: (tile0 + i, 0)),
                  pl.BlockSpec((1, D), lambda i, d: (0, 0))],
        out_specs=pl.BlockSpec((TM, D), lambda i, d: (i, 0)),
        scratch_shapes=[pltpu.VMEM((2, 2, TM, D), F32), pltpu.SemaphoreType.DMA((2,))])
    return pl.pallas_call(
        functools.partial(_combine_kernel, final=final, tile0=tile0, n_tiles=n_tiles),
        grid_spec=grid_spec,
        out_shape=jax.ShapeDtypeStruct((n_tiles * TM, D), F32),
        compiler_params=_cparams(("arbitrary",)),
        name="combine",
    )(dest_flat, ys, x, m3, rw, final_norm)


def _combine_proj_kernel(d_ref, ys_hbm, x_ref, m_ref, rw_ref, mn_ref, gn_ref, w_ref, xo_ref, o_ref, buf, sem, *,
                         n_chunk):
    i = pl.program_id(0)
    n_tiles = N // TM
    n_parts = o_ref.shape[1] // n_chunk

    def issue(tile, slot, part=0, parts=1):
        base = tile * (2 * TM)
        for r in range(part * TM // parts, (part + 1) * TM // parts):
            for kk in range(2):
                _row_copy(ys_hbm, d_ref[base + 2 * r + kk], buf.at[slot, kk], r, sem.at[slot]).start()

    def wait(slot):
        for kk in range(2):
            pltpu.make_async_copy(ys_hbm.at[pl.ds(0, TM)], buf.at[slot, kk], sem.at[slot]).wait()

    @pl.when(i == 0)
    def _():
        issue(0, 0)

    slot = i % 2
    wait(slot)
    w = rw_ref[...]
    x_new = x_ref[...] + m_ref[5:6, :] * (w[:, 0:1] * buf[slot, 0] + w[:, 1:2] * buf[slot, 1])
    xo_ref[...] = x_new
    hb = _modulated_norm(x_new, mn_ref, gn_ref, 0, 1).astype(BF16)
    nxt = jnp.minimum(i + 1, n_tiles - 1)
    for j in range(n_parts):
        issue(nxt, 1 - slot, j, n_parts)
        sl = slice(j * n_chunk, (j + 1) * n_chunk)
        o_ref[:, sl] = jnp.dot(hb, w_ref[:, sl], preferred_element_type=F32)

    @pl.when(i == n_tiles - 1)
    def _():
        wait(1 - slot)


def _combine_proj(dest_flat, ys, x, m3, rw, m3_next, gain_next, w_bf16, n_chunk):
    nout = w_bf16.shape[1]
    grid_spec = pltpu.PrefetchScalarGridSpec(
        num_scalar_prefetch=1,
        grid=(N // TM,),
        in_specs=[pl.BlockSpec(memory_space=pl.ANY),
                  pl.BlockSpec((TM, D), lambda i, d: (i, 0)),
                  pl.BlockSpec((None, 6, D), lambda i, d: (_cond_row(i), 0, 0)),
                  pl.BlockSpec((TM, 2), lambda i, d: (i, 0)),
                  pl.BlockSpec((None, 6, D), lambda i, d: (_cond_row(i), 0, 0)),
                  pl.BlockSpec((1, D), lambda i, d: (0, 0)),
                  pl.BlockSpec((D, nout), lambda i, d: (0, 0))],
        out_specs=[pl.BlockSpec((TM, D), lambda i, d: (i, 0)),
                   pl.BlockSpec((TM, nout), lambda i, d: (i, 0))],
        scratch_shapes=[pltpu.VMEM((2, 2, TM, D), F32), pltpu.SemaphoreType.DMA((2,))])
    return pl.pallas_call(
        functools.partial(_combine_proj_kernel, n_chunk=n_chunk),
        grid_spec=grid_spec,
        out_shape=[jax.ShapeDtypeStruct((N, D), F32), jax.ShapeDtypeStruct((N, nout), F32)],
        compiler_params=_cparams(("arbitrary",)),
        name="combine_proj",
    )(dest_flat, ys, x, m3, rw, m3_next, gain_next, w_bf16)


def _hier_moe(a_pairs, w_list, x, m3, gain, layer, w_rg, b_rg, w_re, b_re, w_gate, w_up, w_down, tri_tm,
              final_norm, final, next_proj=None):
    pad = ROUTE_ROWS - N_GROUPS - N_EXPERTS
    wr_t = jnp.concatenate([w_rg.T, w_re.transpose(0, 2, 1).reshape(N_EXPERTS, D), jnp.zeros((pad, D), F32)], axis=0)
    br_t = jnp.broadcast_to(jnp.concatenate([b_rg, b_re.reshape(N_EXPERTS), jnp.zeros((pad,), F32)])[:, None],
                            (ROUTE_ROWS, LANES))
    x, xf, ri_t, rw_t, cnt = _router(a_pairs, w_list, x, m3, gain, wr_t, br_t, tri_tm)
    e_idx = ri_t[0:2].T
    rank = ri_t[2:4].T
    rw = rw_t[0:2].T
    counts = cnt[0, N_GROUPS:N_GROUPS + N_EXPERTS].astype(jnp.int32)
    padded = (counts + MOE_ROWS - 1) // MOE_ROWS * MOE_ROWS
    end_pad = jnp.cumsum(padded)
    start_pad = end_pad - padded
    experts = jnp.arange(N_EXPERTS, dtype=jnp.int32)
    start_of = jnp.sum(jnp.where(e_idx[:, :, None] == experts, start_pad, 0), axis=-1)
    dest = (start_of + rank).reshape(-1).astype(jnp.int32)
    block_start = jnp.arange(N_SLOT_BLOCKS, dtype=jnp.int32) * MOE_ROWS
    block_e = jnp.minimum(jnp.sum((end_pad[None, :] <= block_start[:, None]).astype(jnp.int32), axis=1),
                          N_EXPERTS - 1)
    n_used = (end_pad[-1:] // MOE_ROWS).astype(jnp.int32)
    tail = jnp.stack([end_pad[-1], jnp.minimum(end_pad[-1] + MOE_ROWS, N_SLOT_BLOCKS * MOE_ROWS)])
    pad_ranges = jnp.concatenate([jnp.stack([start_pad + counts, end_pad], axis=1).reshape(-1), tail]).astype(jnp.int32)
    ys = _experts(block_e, n_used, dest, pad_ranges, xf, w_gate, w_up, w_down, layer)
    if not final:
        return _combine_proj(dest, ys, x, m3, rw, *next_proj)
    nct = NC // TM
    return (_combine(dest, ys, x, m3, rw, final_norm, True, 0, nct),
            _combine(dest, ys, x, m3, rw, final_norm, True, nct, N // TM - nct))


def kernel(x_prompt, x_sample, state_A_fwd, state_A_bwd, cache_k, cache_v, c, c_ctx, mod_w, mod_b, norm_mix, norm_ffn, ab_w_in, ab_conv, ab_a_log, ab_dt_bias, ab_o_gain, ab_w_out, na_w_qkv, na_rpb, na_w_out, moe_w_rg, moe_b_rg, moe_w_re, moe_b_re, moe_w_gate, moe_w_up, moe_w_down, final_norm):
    x = (x_prompt.reshape(NC, D), x_sample.reshape(NL, D))
    cond8 = jnp.concatenate([c_ctx[None, :], c, jnp.zeros((8 - 1 - DEC_BATCH, D), F32)], axis=0)
    mods = _ada_params(cond8, mod_w, mod_b).reshape(DEPTH, 8, 6, D)

    tri_tm = jnp.asarray(np.triu(np.ones((TM, TM)), 1), BF16)
    tri_c = jnp.asarray(np.tril(np.ones((CHUNK, CHUNK))), F32)
    half = np.arange(LANES) < A_DH
    jbd = jnp.asarray((half[:, None] == half[None, :]).astype(np.float32))
    fn = final_norm[None, :]

    m3 = mods[0]
    w_in = ab_w_in[0]
    w_in = jnp.concatenate([w_in[:, :2048], w_in[:, 2080:2592], w_in[:, 2048:2080],
                            jnp.zeros((D, AB_COLS - 2592), F32)], axis=1).astype(BF16)
    proj = _modproj(x, m3, norm_mix[0][None, :], w_in, 896)
    gate_p = jnp.broadcast_to(jnp.stack([ab_a_log[0].reshape(-1), ab_dt_bias[0].reshape(-1)])[:, :, None],
                              (2, 2 * A_HEADS, LANES))
    o_gain2 = jnp.tile(ab_o_gain[0], 2)[None, :]
    zeros_state = jnp.zeros((BATCH, A_HEADS, A_DH, A_DH), F32)
    mix_a_c, s_f, s_b = _deltanet(proj, ab_conv[0], gate_p, o_gain2, zeros_state, zeros_state, tri_c, jbd,
                                  seq_len=SEQ, n_seq=BATCH, n_sub=4, row_blk0=0)
    mix_a_l, _, _ = _deltanet(proj, ab_conv[0], gate_p, o_gain2, state_A_fwd[:, 0], state_A_bwd[:, 0], tri_c, jbd,
                              seq_len=DEC_SEQ, n_seq=DEC_BATCH, n_sub=1, row_blk0=NC // DEC_SEQ)

    cc, sc = _dft_mats(LANES)
    ct, st = _dft_mats(SEQ)
    c64, s64 = _dft_mats(64)
    cs = jnp.asarray(np.concatenate([cc, sc], axis=1), BF16)
    dft = jnp.asarray(np.concatenate([ct, -st], axis=1), BF16)
    ca = jnp.asarray(np.concatenate([cc, -sc, -sc, -cc], axis=1), BF16)
    m1 = jnp.asarray(np.concatenate([c64, s64], axis=1), BF16)
    tw_idx = np.arange(64)
    tw_ang = 2.0 * np.pi * (tw_idx[:, None] * tw_idx[None, :]) / DEC_SEQ
    twc = jnp.broadcast_to(jnp.asarray(np.cos(tw_ang), F32)[:, :, None], (64, 64, LANES))
    tws = jnp.broadcast_to(jnp.asarray(np.sin(tw_ang), F32)[:, :, None], (64, 64, LANES))
    mix_b_c = _fnet_ctx(proj, cs, dft)
    mix_b_l = _fnet_latent(proj, ca, m1, twc, tws)

    w_out = ab_w_out[0].astype(BF16)
    x, qkv = _hier_moe([(mix_a_c, mix_a_l), (mix_b_c, mix_b_l)], [w_out[:A_WIDTH], w_out[A_WIDTH:]], x, m3,
                       norm_ffn[0][None, :], 0, moe_w_rg[0], moe_b_rg[0], moe_w_re[0], moe_b_re[0],
                       moe_w_gate, moe_w_up, moe_w_down, tri_tm, fn, False,
                       next_proj=(mods[1], norm_mix[1][None, :], na_w_qkv[0].astype(BF16), 512))

    m3 = mods[1]
    attn_c = _na_ctx(qkv)
    attn_l = _na_latent(qkv, cache_k[:, 0].reshape(DEC_BATCH, 256, D), cache_v[:, 0].reshape(DEC_BATCH, 256, D),
                        _rpb_tables(na_rpb[0]))
    y_c, y_l = _hier_moe([(attn_c, attn_l)], [na_w_out[0].astype(BF16)], x, m3, norm_ffn[1][None, :], 1, moe_w_rg[1], moe_b_rg[1], moe_w_re[1], moe_b_re[1],
                         moe_w_gate, moe_w_up, moe_w_down, tri_tm, fn, True)

    new_k = qkv[:NC, D:2 * D].reshape(BATCH, 1, SEQ, NA_HEADS, NA_DH)
    new_v = qkv[:NC, 2 * D:].reshape(BATCH, 1, SEQ, NA_HEADS, NA_DH)
    return (y_c.reshape(BATCH, SEQ, D), y_l.reshape(DEC_BATCH, DEC_SEQ, D),
            s_f[:, None], s_b[:, None], new_k, new_v)
```

```python
import functools
import math

import numpy as np
import jax
import jax.numpy as jnp
from jax import lax
from jax.experimental import pallas as pl
from jax.experimental.pallas import tpu as pltpu

F32 = jnp.float32
BF16 = jnp.bfloat16
HIGHEST = lax.Precision.HIGHEST

D = 1024
BATCH, SEQ = 32, 256
DEC_BATCH, DEC_SEQ = 2, 4096
NC = BATCH * SEQ
NL = DEC_BATCH * DEC_SEQ
N = NC + NL
DEPTH = 2
GRID_W = 64
A_DH = 64
A_HEADS = 8
A_WIDTH = 512
CHUNK = 64
B_WIDTH = 512
B_GROUPS = 4
NA_DH = 64
NA_HEADS = 16
WIN_R, WIN_C = 8, 16
N_GROUPS, EXP_PER_GROUP, N_EXPERTS = 4, 8, 32
D_EXPERT = 512
EPS = 1e-6

LANES = 128
TM = 256
MOE_ROWS = 256
N_SLOT_BLOCKS = (2 * N) // MOE_ROWS + N_EXPERTS
AB_COLS = 2688
VMEM_LIMIT = 56 * 1024 * 1024
EXPERT_VMEM_LIMIT = 60 * 1024 * 1024
NEG = -1e30


def _cparams(sem):
    return pltpu.CompilerParams(dimension_semantics=sem, vmem_limit_bytes=VMEM_LIMIT)


def _mm(a, b):
    return jnp.dot(a.astype(BF16), b.astype(BF16), preferred_element_type=F32)


def _mm_nt(a, b):
    return lax.dot_general(a.astype(BF16), b.astype(BF16), (((1,), (1,)), ((), ())),
                           preferred_element_type=F32)


SOLVE_BLK = 16
NA_SCALE = NA_DH ** -0.5
NA_CTX_PAIRS = 8
NA_ROW_UNROLL = 16
TERM_UNROLL = 8


def _unit_lower_solve(a_mat, rhs, same_blk, eye):
    dg = jnp.where(same_blk, a_mat, 0.0)
    p = -dg
    dinv = eye + p
    for _ in range(int(math.log2(SOLVE_BLK)) - 1):
        p = _mm(p, p)
        dinv = dinv + _mm(p, dinv)
    mp = -_mm(dinv, a_mat - dg)
    y = _mm(dinv, rhs)
    y = y + _mm(mp, y)
    for _ in range(int(math.log2(a_mat.shape[0] // SOLVE_BLK)) - 1):
        mp = _mm(mp, mp)
        y = y + _mm(mp, y)
    return y


def _unit_lower_solve_many(a_mats, rhs, same_blk, eye):
    off = [jnp.where(same_blk, 0.0, a).astype(BF16) for a in a_mats]
    p = [jnp.where(same_blk, -a, 0.0).astype(BF16) for a in a_mats]
    dinv = [(eye + x.astype(F32)).astype(BF16) for x in p]
    for _ in range(int(math.log2(SOLVE_BLK)) - 1):
        p = [_mm(x, x).astype(BF16) for x in p]
        dinv = [(di.astype(F32) + _mm(x, di)).astype(BF16) for x, di in zip(p, dinv)]
    mp = [(-_mm(di, o)).astype(BF16) for di, o in zip(dinv, off)]
    y = [_mm(di, r) for di, r in zip(dinv, rhs)]
    y = [yi + _mm(m, yi) for m, yi in zip(mp, y)]
    for _ in range(int(math.log2(a_mats[0].shape[0] // SOLVE_BLK)) - 1):
        mp = [_mm(m, m).astype(BF16) for m in mp]
        y = [yi + _mm(m, yi) for m, yi in zip(mp, y)]
    return y


def _mm_split(a, b, parts, split_rhs=False):
    x = b if split_rhs else a
    acc = None
    for _ in range(parts):
        piece = x.astype(BF16)
        x = x - piece.astype(F32)
        term = (jnp.dot(a.astype(BF16), piece, preferred_element_type=F32) if split_rhs
                else jnp.dot(piece, b.astype(BF16), preferred_element_type=F32))
        acc = term if acc is None else acc + term
    return acc


def _mm_hi(a, b):
    return jnp.dot(a, b, preferred_element_type=F32, precision=HIGHEST)


def _silu(x):
    return x * jax.nn.sigmoid(x)


def _bf16_bits(x):
    b = lax.bitcast_convert_type(x, jnp.int32)
    return b + 0x7FFF + (lax.shift_right_logical(b, jnp.int32(16)) & 1)


_HIGH16 = -65536


def _pack_bf16_pairs(a, b):
    return lax.shift_right_logical(_bf16_bits(a), jnp.int32(16)) | (_bf16_bits(b) & _HIGH16)


def _unpack_bf16_pairs(p):
    a = lax.bitcast_convert_type(lax.shift_left(p, jnp.int32(16)), F32)
    b = lax.bitcast_convert_type(p & _HIGH16, F32)
    return a.astype(BF16), b.astype(BF16)


def _cond_row(i):
    return jnp.where(i < NC // TM, 0, 1 + (i - NC // TM) // (DEC_SEQ // TM))


def _modulated_norm(x, m_ref, g_ref, shift_idx, scale_idx):
    ms = jnp.mean(x * x, axis=-1, keepdims=True)
    y = x * lax.rsqrt(ms + EPS) * g_ref[...]
    return y * (1.0 + m_ref[scale_idx:scale_idx + 1, :]) + m_ref[shift_idx:shift_idx + 1, :]


def _ada_kernel(cond_ref, w_ref, b_ref, o_ref):
    o_ref[...] = _mm_hi(_silu(cond_ref[...]), w_ref[...]) + b_ref[...]


def _ada_params(cond8, mod_w, mod_b):
    tn = 1536
    return pl.pallas_call(
        _ada_kernel,
        grid=(DEPTH, 6 * D // tn),
        in_specs=[pl.BlockSpec((8, D), lambda l, j: (0, 0)),
                  pl.BlockSpec((None, D, tn), lambda l, j: (l, 0, j)),
                  pl.BlockSpec((None, 1, tn), lambda l, j: (l, 0, j))],
        out_specs=pl.BlockSpec((None, 8, tn), lambda l, j: (l, 0, j)),
        out_shape=jax.ShapeDtypeStruct((DEPTH, 8, 6 * D), F32),
        compiler_params=_cparams(("arbitrary", "arbitrary")),
        name="ada_params",
    )(cond8, mod_w, mod_b.reshape(DEPTH, 1, 6 * D))


def _token_specs(x):
    nct = NC // TM
    if isinstance(x, tuple):
        return ([pl.BlockSpec((TM, x[0].shape[1]), lambda i: (jnp.minimum(i, nct - 1), 0)),
                 pl.BlockSpec((TM, x[1].shape[1]), lambda i: (jnp.maximum(i - nct, 0), 0))], list(x))
    return [pl.BlockSpec((TM, x.shape[1]), lambda i: (i, 0))], [x]


def _token_rows(refs):
    if len(refs) == 1:
        return refs[0][...]
    return jnp.where(pl.program_id(0) < NC // TM, refs[0][...], refs[1][...])


def _modproj_kernel(*refs, n_chunk):
    m_ref, g_ref, w_ref, o_ref = refs[-4:]
    hb = _modulated_norm(_token_rows(refs[:-4]), m_ref, g_ref, 0, 1).astype(BF16)
    for j in range(o_ref.shape[1] // n_chunk):
        sl = slice(j * n_chunk, (j + 1) * n_chunk)
        o_ref[:, sl] = jnp.dot(hb, w_ref[:, sl], preferred_element_type=F32)


def _modproj(x, m3, gain, w_bf16, n_chunk):
    nout = w_bf16.shape[1]
    x_specs, x_args = _token_specs(x)
    return pl.pallas_call(
        functools.partial(_modproj_kernel, n_chunk=n_chunk),
        grid=(N // TM,),
        in_specs=x_specs + [pl.BlockSpec((None, 6, D), lambda i: (_cond_row(i), 0, 0)),
                            pl.BlockSpec((1, D), lambda i: (0, 0)),
                            pl.BlockSpec((D, nout), lambda i: (0, 0))],
        out_specs=pl.BlockSpec((TM, nout), lambda i: (i, 0)),
        out_shape=jax.ShapeDtypeStruct((N, nout), F32),
        compiler_params=_cparams(("arbitrary",)),
        name="modproj",
    )(*x_args, m3, gain, w_bf16)


def _deltanet_kernel(q_ref, k_ref, v_ref, z_ref, ab_ref, cq_ref, ck_ref, cv_ref, gp_ref, og_ref,
                     s0f_ref, s0b_ref, tri_ref, jbd_ref, o_ref, sf_ref, sb_ref,
                     u_s, w_s, qd_s, at_s, kt_s, ge_s, st_s, ob, *, seq_len, n_sub):
    hp = pl.program_id(1)
    C = CHUNK
    nc = seq_len // C
    nct = n_sub * nc
    P = LANES
    lane = lax.broadcasted_iota(jnp.int32, (C, P), 1)
    row = lax.broadcasted_iota(jnp.int32, (C, P), 0)
    first_head = lane < A_DH
    ri = lax.broadcasted_iota(jnp.int32, (P, P), 0)
    ci = lax.broadcasted_iota(jnp.int32, (P, P), 1)
    same_head = (ri < C) == (ci < C)
    same_blk = (ri // SOLVE_BLK) == (ci // SOLVE_BLK)
    eye = jnp.where(ri == ci, 1.0, 0.0)
    jbd = jbd_ref[...]
    lincl = tri_ref[...]
    cum_b = [lincl.T.astype(BF16), lincl.astype(BF16)]
    incl_m = [jnp.logical_and(same_head, ri >= ci), jnp.logical_and(same_head, ri <= ci)]
    strict_m = [jnp.logical_and(same_head, ri > ci), jnp.logical_and(same_head, ri < ci)]
    neg_a = -jnp.exp(gp_ref[0])
    dt_b = gp_ref[1]

    def conv_silu(ref, w_ref, c):
        base = pl.multiple_of(c * C, C)
        cs = c % nc
        xc = ref[pl.ds(base, C), :]
        pbase = pl.multiple_of(jnp.maximum(base - 8, 0), 8)
        nbase = pl.multiple_of(jnp.minimum(base + C, n_sub * seq_len - 8), 8)
        prev_row = ref[pl.ds(pbase, 8), :][7:8, :] * jnp.where(cs > 0, 1.0, 0.0)
        next_row = ref[pl.ds(nbase, 8), :][0:1, :] * jnp.where(cs < nc - 1, 1.0, 0.0)
        x_prev = jnp.where(row == 0, prev_row, pltpu.roll(xc, 1, 0))
        x_next = jnp.where(row == C - 1, next_row, pltpu.roll(xc, C - 1, 0))
        y = w_ref[0:1, :] * x_prev + w_ref[1:2, :] * xc + w_ref[2:3, :] * x_next
        return _silu(y)

    def stack(x):
        return jnp.concatenate([jnp.where(first_head, x, 0.0), jnp.where(first_head, 0.0, x)], axis=0)

    def chunk_inputs(c):
        base = pl.multiple_of(c * C, C)
        q = conv_silu(q_ref, cq_ref, c)
        k = conv_silu(k_ref, ck_ref, c)
        v = conv_silu(v_ref, cv_ref, c)
        return q, k, v, ab_ref[pl.ds(base, C), :].T

    sub8 = lax.broadcasted_iota(jnp.int32, (A_HEADS, C), 0)

    def pair_row(x8):
        r0 = jnp.sum(jnp.where(sub8 == 2 * hp, x8, 0.0), axis=0, keepdims=True)
        r1 = jnp.sum(jnp.where(sub8 == 2 * hp + 1, x8, 0.0), axis=0, keepdims=True)
        return jnp.concatenate([r0, r1], axis=1)

    def chain_gates(ab_t, d):
        a8 = ab_t[d * A_HEADS:(d + 1) * A_HEADS, :]
        b8 = ab_t[(2 + d) * A_HEADS:(3 + d) * A_HEADS, :]
        g8 = neg_a[d * A_HEADS:(d + 1) * A_HEADS, :C] * jax.nn.softplus(a8 + dt_b[d * A_HEADS:(d + 1) * A_HEADS, :C])
        gc8 = _mm_split(g8, cum_b[d], 3)
        tot8 = jnp.broadcast_to(jnp.sum(g8, axis=-1, keepdims=True), (A_HEADS, C))
        gc_row, beta_row, tot_row = pair_row(gc8), pair_row(jax.nn.sigmoid(b8)), pair_row(tot8)
        cols = jnp.concatenate([gc_row, beta_row, tot_row, jnp.zeros((5, P), F32)], axis=0).T
        return gc_row, tot_row, cols[:, 0:1], cols[:, 1:2], cols[:, 2:3]

    def terms_body(j, carry):
        cs = [TERM_UNROLL * j + t for t in range(TERM_UNROLL)]
        ins = [chunk_inputs(c) for c in cs]
        qsq = [_mm_split(x[0] * x[0], jbd, 2) for x in ins]
        ksq = [_mm_split(x[1] * x[1], jbd, 2) for x in ins]
        qs = [x[0] * lax.rsqrt(s + EPS) * (A_DH ** -0.5) for x, s in zip(ins, qsq)]
        ks = [x[1] * lax.rsqrt(s + EPS) for x, s in zip(ins, ksq)]
        qst = [stack(x) for x in qs]
        kst = [stack(x) for x in ks]
        vst = [stack(x[2]) for x in ins]
        kst_t = [x.T for x in kst]
        kk = [_mm_nt(x, x) for x in kst]
        qk = [_mm_nt(x, y) for x, y in zip(qst, kst)]
        chains = [(t, d) for t in range(TERM_UNROLL) for d in range(2)]
        gates = [chain_gates(ins[t][3], d) for t, d in chains]
        decay, e_gc = [], []
        for (t, d), (gc_row, tot_row, gc_col, beta_col, tot_col) in zip(chains, gates):
            diff = jnp.broadcast_to(gc_col, (P, P)) - jnp.broadcast_to(gc_row, (P, P))
            decay.append(jnp.where(incl_m[d], jnp.exp(jnp.where(incl_m[d], diff, 0.0)), 0.0))
            e_gc.append(jnp.exp(gc_col))
        a_mats = [jnp.where(strict_m[d], g[3] * kk[t] * dc, 0.0) for (t, d), g, dc in zip(chains, gates, decay)]
        rhs = [vst[t] * g[3] + pltpu.roll(kst[t] * (g[3] * e), A_DH, 1)
               for (t, d), g, e in zip(chains, gates, e_gc)]
        xs = _unit_lower_solve_many(a_mats, rhs, same_blk, eye)
        for (t, d), x, g, e, dc in zip(chains, xs, gates, e_gc, decay):
            c = cs[t]
            gc_row, tot_row = g[0], g[1]
            u_s[d, c] = jnp.where(same_head, x, 0.0).astype(BF16)
            w_s[d, c] = pltpu.roll(jnp.where(same_head, 0.0, x), A_DH, 1).astype(BF16)
            qd_s[d, c] = (qst[t] * e).astype(BF16)
            at_s[d, c] = jnp.where(incl_m[d], qk[t] * dc, 0.0).astype(BF16)
            kt_s[d, c] = (kst_t[t] * jnp.exp(tot_row - gc_row)).astype(BF16)
            ge_s[d, c] = jnp.broadcast_to(jnp.exp(tot_row), (8, P))
        return carry

    lax.fori_loop(0, nct // TERM_UNROLL, terms_body, 0)

    def block_diag(s2):
        z = jnp.zeros((A_DH, A_DH), F32)
        return jnp.concatenate([jnp.concatenate([s2[0], z], axis=1),
                                jnp.concatenate([z, s2[1]], axis=1)], axis=0)

    for s in range(n_sub):
        st_s[2 * s] = block_diag(s0f_ref[s])
        st_s[2 * s + 1] = block_diag(s0b_ref[s])

    def scan_body(i, carry):
        chains = [(s, d, s * nc + (i if d == 0 else nc - 1 - i)) for s in range(n_sub) for d in range(2)]
        dot = functools.partial(jnp.dot, preferred_element_type=F32)
        s_bd = [st_s[2 * s + d] for s, d, c in chains]
        sb16 = [x.astype(BF16) for x in s_bd]
        ws = [dot(w_s[d, c], sb) for (s, d, c), sb in zip(chains, sb16)]
        qs_ = [dot(qd_s[d, c], sb) for (s, d, c), sb in zip(chains, sb16)]
        vb = [(u_s[d, c].astype(F32) - x).astype(BF16) for (s, d, c), x in zip(chains, ws)]
        av = [dot(at_s[d, c], x) for (s, d, c), x in zip(chains, vb)]
        kv = [dot(kt_s[d, c], x) for (s, d, c), x in zip(chains, vb)]
        for (s, d, c), sb, q_, a_, k_ in zip(chains, s_bd, qs_, av, kv):
            st_s[2 * s + d] = sb * ge_s[d, c][0:1, :] + k_
            o_st = q_ + a_
            dst = o_ref if d == 0 else ob
            dst[pl.ds(pl.multiple_of(c * C, C), C), :] = o_st[:C] + o_st[C:]
        return carry

    lax.fori_loop(0, nc, scan_body, 0)

    for s in range(n_sub):
        for d, ref in ((0, sf_ref), (1, sb_ref)):
            s_bd = st_s[2 * s + d]
            ref[s, 0] = s_bd[:A_DH, :A_DH]
            ref[s, 1] = s_bd[A_DH:, A_DH:]

    def finish(j, carry):
        bases = [pl.multiple_of((TERM_UNROLL * j + t) * C, C) for t in range(TERM_UNROLL)]
        o = [o_ref[pl.ds(b, C), :] + ob[pl.ds(b, C), :] for b in bases]
        ms = [_mm_split(x * x, jbd, 2) * (1.0 / A_DH) for x in o]
        for b, x, m in zip(bases, o, ms):
            o_ref[pl.ds(b, C), :] = x * lax.rsqrt(m + EPS) * og_ref[...] * _silu(z_ref[pl.ds(b, C), :])
        return carry

    lax.fori_loop(0, nct // TERM_UNROLL, finish, 0)


def _deltanet(proj, conv_w, gate_p, o_gain2, s0f, s0b, tri, jbd, *, seq_len, n_seq, n_sub, row_blk0):
    rows = n_sub * seq_len
    nct = rows // CHUNK
    rb = lambda b: row_blk0 + b
    col = lambda off: (lambda b, hp: (rb(b), off + hp))
    st_spec = pl.BlockSpec((n_sub, 2, A_DH, A_DH), lambda b, hp: (b, hp, 0, 0))
    in_specs = [pl.BlockSpec((rows, LANES), col(0)),
                pl.BlockSpec((rows, LANES), col(4)),
                pl.BlockSpec((rows, LANES), col(8)),
                pl.BlockSpec((rows, LANES), col(12)),
                pl.BlockSpec((rows, LANES), lambda b, hp: (rb(b), 20)),
                pl.BlockSpec((3, LANES), lambda b, hp: (0, hp)),
                pl.BlockSpec((3, LANES), lambda b, hp: (0, 4 + hp)),
                pl.BlockSpec((3, LANES), lambda b, hp: (0, 8 + hp)),
                pl.BlockSpec((2, 2 * A_HEADS, LANES), lambda b, hp: (0, 0, 0)),
                pl.BlockSpec((1, LANES), lambda b, hp: (0, 0)),
                st_spec, st_spec,
                pl.BlockSpec((CHUNK, CHUNK), lambda b, hp: (0, 0)),
                pl.BlockSpec((LANES, LANES), lambda b, hp: (0, 0))]
    args = [proj, proj, proj, proj, proj, conv_w, conv_w, conv_w, gate_p, o_gain2, s0f, s0b, tri, jbd]
    st_shape = jax.ShapeDtypeStruct((n_seq, A_HEADS, A_DH, A_DH), F32)
    tile = lambda dt: pltpu.VMEM((2, nct, LANES, LANES), dt)
    return pl.pallas_call(
        functools.partial(_deltanet_kernel, seq_len=seq_len, n_sub=n_sub),
        grid=(n_seq // n_sub, A_HEADS // 2),
        in_specs=in_specs,
        out_specs=[pl.BlockSpec((rows, LANES), lambda b, hp: (b, hp)), st_spec, st_spec],
        out_shape=[jax.ShapeDtypeStruct((n_seq * seq_len, A_WIDTH), F32), st_shape, st_shape],
        scratch_shapes=[tile(BF16), tile(BF16), tile(BF16), tile(BF16), tile(BF16),
                        pltpu.VMEM((2, nct, 8, LANES), F32),
                        pltpu.VMEM((2 * n_sub, LANES, LANES), F32),
                        pltpu.VMEM((rows, LANES), F32)],
        compiler_params=_cparams(("arbitrary", "arbitrary")),
        name="deltanet",
    )(*args)


def _dft_mats(n):
    idx = np.arange(n)
    ang = 2.0 * np.pi * ((idx[:, None] * idx[None, :]) % n) / n
    return np.cos(ang), np.sin(ang)


def _fnet_ctx_kernel(u_ref, cs_ref, dft_ref, o_ref):
    norm = 1.0 / math.sqrt(SEQ * LANES)
    sls = [slice(g * LANES, (g + 1) * LANES) for g in range(B_GROUPS)]
    p = [_mm(u_ref[:, sl], cs_ref[...]) for sl in sls]
    stack = [jnp.concatenate([x[:, :LANES], x[:, LANES:]], axis=0) for x in p]
    y = [_mm(dft_ref[...], x) for x in stack]
    for sl, x in zip(sls, y):
        o_ref[:, sl] = x * norm


def _fnet_ctx(proj, cs, dft):
    return pl.pallas_call(
        _fnet_ctx_kernel,
        grid=(BATCH,),
        in_specs=[pl.BlockSpec((SEQ, B_WIDTH), lambda b: (b, 4)),
                  pl.BlockSpec(cs.shape, lambda b: (0, 0)),
                  pl.BlockSpec(dft.shape, lambda b: (0, 0))],
        out_specs=pl.BlockSpec((SEQ, B_WIDTH), lambda b: (b, 0)),
        out_shape=jax.ShapeDtypeStruct((NC, B_WIDTH), F32),
        compiler_params=_cparams(("arbitrary",)),
        name="fnet_ctx",
    )(proj, cs, dft)


FN_SUB = 4


def _fnet_lat1_kernel(u_ref, ca_ref, m1_ref, twc_ref, tws_ref, o_ref):
    r = 64
    units = [(j, g) for j in range(FN_SUB) for g in range(B_GROUPS)]
    rows = lambda j: slice(j * r, (j + 1) * r)
    pa = [_mm(u_ref[rows(j), g * LANES:(g + 1) * LANES], ca_ref[...]) for j, g in units]
    rhs = [jnp.concatenate([x[:, :2 * LANES], x[:, 2 * LANES:]], axis=0) for x in pa]
    zz = [_mm(m1_ref[...], x) for x in rhs]
    for (j, g), z in zip(units, zz):
        c = twc_ref[j]
        s = tws_ref[j]
        zr, zi = z[:, :LANES], z[:, LANES:]
        o_ref[rows(j), 2 * g * LANES:(2 * g + 1) * LANES] = (zr * c + zi * s).astype(BF16)
        o_ref[rows(j), (2 * g + 1) * LANES:(2 * g + 2) * LANES] = (zi * c - zr * s).astype(BF16)


def _fnet_lat2_kernel(z_ref, m1_ref, o_ref):
    r = 64
    norm = 1.0 / math.sqrt(DEC_SEQ * LANES)
    units = [(j, g) for j in range(FN_SUB) for g in range(B_GROUPS)]
    rows = lambda j: slice(j * r, (j + 1) * r)
    rhs = [jnp.concatenate([z_ref[rows(j), 2 * g * LANES:(2 * g + 1) * LANES],
                            z_ref[rows(j), (2 * g + 1) * LANES:(2 * g + 2) * LANES]], axis=0) for j, g in units]
    y = [_mm(m1_ref[...], x) for x in rhs]
    for (j, g), x in zip(units, y):
        o_ref[rows(j), g * LANES:(g + 1) * LANES] = x * norm


def _fnet_latent(proj, ca, m1, twc, tws):
    r = 64
    u = proj[NC:, 2048:2560].astype(BF16).reshape(DEC_BATCH, r, r, B_WIDTH)
    u = u.transpose(0, 2, 1, 3).reshape(DEC_BATCH * r * r, B_WIDTH)
    steps = DEC_BATCH * r // FN_SUB
    blk = FN_SUB * r
    z = pl.pallas_call(
        _fnet_lat1_kernel,
        grid=(steps,),
        in_specs=[pl.BlockSpec((blk, B_WIDTH), lambda s: (s, 0)),
                  pl.BlockSpec(ca.shape, lambda s: (0, 0)),
                  pl.BlockSpec(m1.shape, lambda s: (0, 0)),
                  pl.BlockSpec((FN_SUB, r, LANES), lambda s: (s % (r // FN_SUB), 0, 0)),
                  pl.BlockSpec((FN_SUB, r, LANES), lambda s: (s % (r // FN_SUB), 0, 0))],
        out_specs=pl.BlockSpec((blk, 2 * B_WIDTH), lambda s: (s, 0)),
        out_shape=jax.ShapeDtypeStruct((NL, 2 * B_WIDTH), BF16),
        compiler_params=_cparams(("arbitrary",)),
        name="fnet_lat1",
    )(u, ca, m1, twc, tws)
    z = z.reshape(DEC_BATCH, r, r, 2 * B_WIDTH).transpose(0, 2, 1, 3).reshape(NL, 2 * B_WIDTH)
    y = pl.pallas_call(
        _fnet_lat2_kernel,
        grid=(steps,),
        in_specs=[pl.BlockSpec((blk, 2 * B_WIDTH), lambda s: (s, 0)),
                  pl.BlockSpec(m1.shape, lambda s: (0, 0))],
        out_specs=pl.BlockSpec((blk, B_WIDTH), lambda s: (s, 0)),
        out_shape=jax.ShapeDtypeStruct((NL, B_WIDTH), F32),
        compiler_params=_cparams(("arbitrary",)),
        name="fnet_lat2",
    )(z, m1)
    return y.reshape(DEC_BATCH, r, r, B_WIDTH).transpose(0, 2, 1, 3).reshape(NL, B_WIDTH)


def _head_masks():
    lane = lax.broadcasted_iota(jnp.int32, (1, LANES), 1)
    return lane < NA_DH


def _attend_many(chains):
    s = [[_mm_nt(q, k) if b is None else _mm_nt(q, k) + b for k, v, b in kv] for q, kv in chains]
    m = [functools.reduce(jnp.maximum, [jnp.max(x, axis=-1, keepdims=True) for x in xs]) for xs in s]
    p = [[jnp.exp(x - mi) for x in xs] for xs, mi in zip(s, m)]
    l = [sum(jnp.sum(x, axis=-1, keepdims=True) for x in xs) for xs in p]
    o = [sum(_mm(x, v) for x, (k, v, b) in zip(xs, kv)) for xs, (q, kv) in zip(p, chains)]
    return [oi / li for oi, li in zip(o, l)]


def _na_ctx_kernel(q_ref, k_ref, v_ref, o_ref):
    first = _head_masks()
    n_split = 2
    rows = SEQ // n_split
    chains = []
    for p in range(NA_CTX_PAIRS):
        lanes = slice(p * LANES, (p + 1) * LANES)
        k = k_ref[:, lanes].astype(BF16)
        v = v_ref[:, lanes].astype(BF16)
        for j in range(n_split):
            q = q_ref[j * rows:(j + 1) * rows, lanes]
            for a in range(2):
                chains.append((jnp.where(first if a == 0 else jnp.logical_not(first), q * NA_SCALE, 0.0),
                               [(k, v, None)]))
    outs = _attend_many(chains)
    for p in range(NA_CTX_PAIRS):
        for j in range(n_split):
            i = 2 * (p * n_split + j)
            o_ref[j * rows:(j + 1) * rows, p * LANES:(p + 1) * LANES] = jnp.where(first, outs[i], outs[i + 1])


def _na_ctx(qkv):
    w = NA_CTX_PAIRS * LANES
    nblk = D // w
    return pl.pallas_call(
        _na_ctx_kernel,
        grid=(BATCH, nblk),
        in_specs=[pl.BlockSpec((SEQ, w), lambda b, hp: (b, hp)),
                  pl.BlockSpec((SEQ, w), lambda b, hp: (b, nblk + hp)),
                  pl.BlockSpec((SEQ, w), lambda b, hp: (b, 2 * nblk + hp))],
        out_specs=pl.BlockSpec((SEQ, w), lambda b, hp: (b, hp)),
        out_shape=jax.ShapeDtypeStruct((NC, D), F32),
        compiler_params=_cparams(("arbitrary", "arbitrary")),
        name="na_ctx",
    )(qkv, qkv, qkv)


def _na_lat_kernel(q_ref, k_ref, v_ref, kc_ref, vc_ref, tt_ref, o_ref):
    first = _head_masks()
    rows = DEC_SEQ // GRID_W
    kctx = kc_ref[...].astype(BF16)
    vctx = vc_ref[...].astype(BF16)
    nkeys = WIN_R * GRID_W

    def rows_body(j, carry):
        chains = []
        for t in range(NA_ROW_UNROLL):
            r = NA_ROW_UNROLL * j + t
            r0 = jnp.clip(r - WIN_R // 2, 0, rows - WIN_R)
            dr0 = r0 - r + (WIN_R - 1)
            q = q_ref[pl.ds(pl.multiple_of(r * GRID_W, GRID_W), GRID_W), :] * NA_SCALE
            kbase = pl.multiple_of(r0 * GRID_W, GRID_W)
            kl = k_ref[pl.ds(kbase, nkeys), :].astype(BF16)
            vl = v_ref[pl.ds(kbase, nkeys), :].astype(BF16)
            for a in range(2):
                qm = jnp.where(first if a == 0 else jnp.logical_not(first), q, 0.0)
                bias = jnp.concatenate([tt_ref[a, dr0 + 2 * i] for i in range(WIN_R // 2)], axis=1)
                chains.append((qm, [(kl, vl, bias), (kctx, vctx, None)]))
        outs = _attend_many(chains)
        for t in range(NA_ROW_UNROLL):
            r = NA_ROW_UNROLL * j + t
            o_ref[pl.ds(pl.multiple_of(r * GRID_W, GRID_W), GRID_W), :] = jnp.where(first, outs[2 * t], outs[2 * t + 1])
        return carry

    lax.fori_loop(0, rows // NA_ROW_UNROLL, rows_body, 0)


def _na_latent(qkv, cache_k2, cache_v2, tt2):
    rb0 = NC // DEC_SEQ
    return pl.pallas_call(
        _na_lat_kernel,
        grid=(DEC_BATCH, NA_HEADS // 2),
        in_specs=[pl.BlockSpec((DEC_SEQ, LANES), lambda b, hp: (rb0 + b, hp)),
                  pl.BlockSpec((DEC_SEQ, LANES), lambda b, hp: (rb0 + b, 8 + hp)),
                  pl.BlockSpec((DEC_SEQ, LANES), lambda b, hp: (rb0 + b, 16 + hp)),
                  pl.BlockSpec((None, 256, LANES), lambda b, hp: (b, 0, hp)),
                  pl.BlockSpec((None, 256, LANES), lambda b, hp: (b, 0, hp)),
                  pl.BlockSpec((2, 2 * WIN_R - 2, GRID_W, LANES), lambda b, hp: (hp, 0, 0, 0))],
        out_specs=pl.BlockSpec((DEC_SEQ, LANES), lambda b, hp: (b, hp)),
        out_shape=jax.ShapeDtypeStruct((NL, D), F32),
        compiler_params=_cparams(("arbitrary", "arbitrary")),
        name="na_latent",
    )(qkv, qkv, qkv, cache_k2, cache_v2, tt2)


def _rpb_tables(rpb):
    col = np.arange(GRID_W)
    start = np.clip(col - WIN_C // 2, 0, GRID_W - WIN_C)
    inside = (col[None, :] >= start[:, None]) & (col[None, :] < start[:, None] + WIN_C)
    w = GRID_W
    period = 2 * w - 1
    x = jnp.pad(rpb, ((0, 0), (0, 0), (w - WIN_C, w - WIN_C)))
    flat = jnp.tile(x, (1, 1, w))[:, :, w - 1:w - 1 + w * (period - 1)]
    t = flat.reshape(NA_HEADS, 2 * WIN_R - 1, w, period - 1)[..., :w]
    t = jnp.where(inside[None, None], t, NEG)
    return jnp.concatenate([t[:, :-1], t[:, 1:]], axis=-1)


ROUTE_SPLIT = 2
ROUTE_ROWS = 40


def _router_kernel(*refs, n_in):
    a_refs = refs[:2 * n_in]
    w_refs = refs[2 * n_in:3 * n_in]
    x_refs = refs[3 * n_in:-11]
    m_ref, g_ref, wr_ref, br_ref, tri_ref, xn_ref, xf_ref, ri_ref, rw_ref, cnt_ref, base_scr = refs[-11:]
    i = pl.program_id(0)

    @pl.when(i == 0)
    def _():
        base_scr[...] = jnp.zeros_like(base_scr)
        cnt_ref[...] = jnp.zeros_like(cnt_ref)

    rows = [slice(t * TM // ROUTE_SPLIT, (t + 1) * TM // ROUTE_SPLIT) for t in range(ROUTE_SPLIT)]
    is_ctx = pl.program_id(0) < NC // TM
    pick = lambda pair, r: jnp.where(is_ctx, pair[0][r, :], pair[1][r, :]) if len(pair) == 2 else pair[0][r, :]
    acc = [sum(_mm(pick(a_refs[2 * j:2 * j + 2], r), w_ref[...]) for j, w_ref in enumerate(w_refs)) for r in rows]
    x_new = [pick(x_refs, r) + m_ref[2:3, :] * a for r, a in zip(rows, acc)]
    for r, xn in zip(rows, x_new):
        xn_ref[r, :] = xn
    h = [_modulated_norm(xn, m_ref, g_ref, 3, 4) for xn in x_new]
    for r, hh in zip(rows, h):
        xf_ref[r, :] = _pack_bf16_pairs(hh[:, :D // 2], hh[:, D // 2:])
    logits = jnp.concatenate(
        [lax.dot_general(wr_ref[...], hh, (((1,), (1,)), ((), ())), preferred_element_type=F32, precision=HIGHEST)
         for hh in h], axis=1) + br_ref[:, 0:1]
    row = lax.broadcasted_iota(jnp.int32, logits.shape, 0)
    cmax = lambda x: jnp.max(x, axis=0, keepdims=True)
    cmin = lambda x: jnp.min(x, axis=0, keepdims=True)
    csum = lambda x: jnp.sum(x, axis=0, keepdims=True)

    gmask = row < N_GROUPS
    mg = cmax(jnp.where(gmask, logits, NEG))
    eg = jnp.where(gmask, jnp.exp(jnp.where(gmask, logits - mg, NEG)), 0.0)
    pg = eg / csum(eg)
    p_grp = cmax(pg)
    grp = cmin(jnp.where(jnp.logical_and(gmask, pg == p_grp), row, ROUTE_ROWS))
    lo = N_GROUPS + grp * EXP_PER_GROUP
    emask = jnp.logical_and(row >= lo, row < lo + EXP_PER_GROUP)
    me = cmax(jnp.where(emask, logits, NEG))
    ee = jnp.where(emask, jnp.exp(jnp.where(emask, logits - me, NEG)), 0.0)
    pe = ee / csum(ee)
    p1 = cmax(pe)
    i1 = cmin(jnp.where(jnp.logical_and(emask, pe == p1), row, ROUTE_ROWS))
    m2 = jnp.logical_and(emask, row != i1)
    p2 = cmax(jnp.where(m2, pe, -1.0))
    i2 = cmin(jnp.where(jnp.logical_and(m2, pe == p2), row, ROUTE_ROWS))
    den = p1 + p2
    w1 = p_grp * p1 / den
    w2 = p_grp * p2 / den

    sel1 = row == i1
    sel2 = row == i2
    oh = jnp.where(jnp.logical_or(sel1, sel2), 1.0, 0.0).astype(BF16)
    before = jnp.dot(oh, tri_ref[...], preferred_element_type=F32) + base_scr[:, 0:1]
    rank1 = csum(jnp.where(sel1, before, 0.0))
    rank2 = csum(jnp.where(sel2, before, 0.0))
    base_scr[...] = base_scr[...] + jnp.sum(oh.astype(F32), axis=1, keepdims=True)
    cnt_ref[...] = cnt_ref[...] + lax.dot_general(jnp.ones((8, TM), BF16), oh, (((1,), (1,)), ((), ())),
                                                  preferred_element_type=F32)
    sub = lax.broadcasted_iota(jnp.int32, (8, TM), 0)
    ri_ref[...] = jnp.where(sub == 0, i1 - N_GROUPS, jnp.where(sub == 1, i2 - N_GROUPS,
                  jnp.where(sub == 2, rank1.astype(jnp.int32), jnp.where(sub == 3, rank2.astype(jnp.int32), 0))))
    rw_ref[...] = jnp.where(sub == 0, w1, jnp.where(sub == 1, w2, 0.0))


def _router(a_pairs, w_list, x, m3, gain, wr_t, br_t, tri_upper):
    in_specs, args = [], []
    for pair in a_pairs:
        specs, ops = _token_specs(pair)
        in_specs += specs
        args += ops
    x_specs, x_args = _token_specs(x)
    in_specs += ([pl.BlockSpec(w.shape, lambda i: (0, 0)) for w in w_list] + x_specs
                 + [pl.BlockSpec((None, 6, D), lambda i: (_cond_row(i), 0, 0)),
                    pl.BlockSpec((1, D), lambda i: (0, 0)),
                    pl.BlockSpec((ROUTE_ROWS, D), lambda i: (0, 0)),
                    pl.BlockSpec((ROUTE_ROWS, LANES), lambda i: (0, 0)),
                    pl.BlockSpec((TM, TM), lambda i: (0, 0))])
    return pl.pallas_call(
        functools.partial(_router_kernel, n_in=len(a_pairs)),
        grid=(N // TM,),
        in_specs=in_specs,
        out_specs=[pl.BlockSpec((TM, D), lambda i: (i, 0)),
                   pl.BlockSpec((TM, D // 2), lambda i: (i, 0)),
                   pl.BlockSpec((8, TM), lambda i: (0, i)),
                   pl.BlockSpec((8, TM), lambda i: (0, i)),
                   pl.BlockSpec((8, ROUTE_ROWS), lambda i: (0, 0))],
        out_shape=[jax.ShapeDtypeStruct((N, D), F32),
                   jax.ShapeDtypeStruct((N, D // 2), jnp.int32),
                   jax.ShapeDtypeStruct((8, N), jnp.int32),
                   jax.ShapeDtypeStruct((8, N), F32),
                   jax.ShapeDtypeStruct((8, ROUTE_ROWS), F32)],
        scratch_shapes=[pltpu.VMEM((ROUTE_ROWS, LANES), F32)],
        compiler_params=_cparams(("arbitrary",)),
        name="router",
    )(*args, *w_list, *x_args, m3, gain, wr_t, br_t, tri_upper)


def _row_copy(src_hbm, row, dst, r, sem):
    return pltpu.make_async_copy(src_hbm.at[pl.ds(row, 1)], dst.at[pl.ds(r, 1)], sem)


def _expert_kernel(be_ref, nu_ref, d_ref, pad_ref, xp_hbm, wg_ref, wu_ref, wd_ref, ys_ref, xres, xbuf, st_ref, sem):
    del be_ref
    b = pl.program_id(0)
    half = D // 2

    def gather(blk, slot, part=0, parts=1):
        base = blk * MOE_ROWS
        for r in range(part * MOE_ROWS // parts, (part + 1) * MOE_ROWS // parts):
            xbuf[slot, pl.ds(r, 1), :] = xres[pl.ds(st_ref[base + r], 1), :]

    @pl.when(b == 0)
    def _():
        cp = pltpu.make_async_copy(xp_hbm, xres, sem.at[0])
        cp.start()

        def clear_tail(e, c):
            def clear(s_, c2):
                st_ref[s_] = 0
                return c2

            return lax.fori_loop(pad_ref[2 * e], pad_ref[2 * e + 1], clear, c)

        lax.fori_loop(0, N_EXPERTS + 1, clear_tail, 0)

        def invert(j, c):
            slots = [d_ref[16 * j + t] for t in range(16)]
            for t in range(16):
                st_ref[slots[t]] = 8 * j + t // 2
            return c

        lax.fori_loop(0, 2 * N // 16, invert, 0)
        cp.wait()
        gather(0, 0)

    @pl.when(b < nu_ref[0])
    def _():
        nxt = (jnp.minimum(b + 1, N_SLOT_BLOCKS - 1), (b + 1) % 2)
        x_lo, x_hi = _unpack_bf16_pairs(xbuf[b % 2])
        dot = functools.partial(jnp.dot, preferred_element_type=F32)
        gather(*nxt, 0, 8)
        g = dot(x_lo, wg_ref[:half, :].astype(BF16))
        gather(*nxt, 1, 8)
        g = g + dot(x_hi, wg_ref[half:, :].astype(BF16))
        gather(*nxt, 2, 8)
        u = dot(x_lo, wu_ref[:half, :].astype(BF16))
        gather(*nxt, 3, 8)
        u = u + dot(x_hi, wu_ref[half:, :].astype(BF16))
        hb = (_silu(g) * u).astype(BF16)
        quarter = D // 4
        for j in range(4):
            gather(*nxt, 4 + j, 8)
            cols = slice(j * quarter, (j + 1) * quarter)
            ys_ref[:, cols] = jnp.dot(hb, wd_ref[:, cols].astype(BF16), preferred_element_type=F32)

    @pl.when(b >= nu_ref[0])
    def _():
        ys_ref[...] = jnp.zeros_like(ys_ref)


def _experts(block_e, n_used, dest, pad_ranges, xp, w_gate, w_up, w_down, layer):
    grid_spec = pltpu.PrefetchScalarGridSpec(
        num_scalar_prefetch=4,
        grid=(N_SLOT_BLOCKS,),
        in_specs=[pl.BlockSpec(memory_space=pl.ANY),
                  pl.BlockSpec((None, None, D, D_EXPERT), lambda b, be, nu, st, pd: (layer, be[b], 0, 0)),
                  pl.BlockSpec((None, None, D, D_EXPERT), lambda b, be, nu, st, pd: (layer, be[b], 0, 0)),
                  pl.BlockSpec((None, None, D_EXPERT, D), lambda b, be, nu, st, pd: (layer, be[b], 0, 0))],
        out_specs=pl.BlockSpec((MOE_ROWS, D), lambda b, be, nu, st, pd: (b, 0)),
        scratch_shapes=[pltpu.VMEM((N, D // 2), jnp.int32),
                        pltpu.VMEM((2, MOE_ROWS, D // 2), jnp.int32),
                        pltpu.SMEM((N_SLOT_BLOCKS * MOE_ROWS,), jnp.int32),
                        pltpu.SemaphoreType.DMA((1,))])
    return pl.pallas_call(
        _expert_kernel,
        grid_spec=grid_spec,
        out_shape=jax.ShapeDtypeStruct((N_SLOT_BLOCKS * MOE_ROWS, D), F32),
        compiler_params=pltpu.CompilerParams(dimension_semantics=("arbitrary",), vmem_limit_bytes=EXPERT_VMEM_LIMIT),
        name="experts",
    )(block_e, n_used, dest, pad_ranges, xp, w_gate, w_up, w_down)


def _combine_kernel(d_ref, ys_hbm, x_ref, m_ref, rw_ref, fn_ref, o_ref, buf, sem, *, final, tile0, n_tiles):
    i = pl.program_id(0)

    def issue(tile, slot):
        base = (tile0 + tile) * (2 * TM)
        for r in range(TM):
            for kk in range(2):
                _row_copy(ys_hbm, d_ref[base + 2 * r + kk], buf.at[slot, kk], r, sem.at[slot]).start()

    @pl.when(i == 0)
    def _():
        issue(0, 0)

    @pl.when(i + 1 < n_tiles)
    def _():
        issue(i + 1, (i + 1) % 2)

    slot = i % 2
    for kk in range(2):
        pltpu.make_async_copy(ys_hbm.at[pl.ds(0, TM)], buf.at[slot, kk], sem.at[slot]).wait()
    w = rw_ref[...]
    y = w[:, 0:1] * buf[slot, 0] + w[:, 1:2] * buf[slot, 1]
    out = x_ref[...] + m_ref[5:6, :] * y
    if final:
        ms = jnp.mean(out * out, axis=-1, keepdims=True)
        out = out * lax.rsqrt(ms + EPS) * fn_ref[...]
    o_ref[...] = out


def _combine(dest_flat, ys, x, m3, rw, final_norm, final, tile0=0, n_tiles=N // TM):
    grid_spec = pltpu.PrefetchScalarGridSpec(
        num_scalar_prefetch=1,
        grid=(n_tiles,),
        in_specs=[pl.BlockSpec(memory_space=pl.ANY),
                  pl.BlockSpec((TM, D), lambda i, d: (tile0 + i, 0)),
                  pl.BlockSpec((None, 6, D), lambda i, d: (_cond_row(tile0 + i), 0, 0)),
                  pl.BlockSpec((TM, 2), lambda i, d: (tile0 + i, 0)),
                  pl.BlockSpec((1, D), lambda i, d: (0, 0))],
        out_specs=pl.BlockSpec((TM, D), lambda i, d: (i, 0)),
        scratch_shapes=[pltpu.VMEM((2, 2, TM, D), F32), pltpu.SemaphoreType.DMA((2,))])
    return pl.pallas_call(
        functools.partial(_combine_kernel, final=final, tile0=tile0, n_tiles=n_tiles),
        grid_spec=grid_spec,
        out_shape=jax.ShapeDtypeStruct((n_tiles * TM, D), F32),
        compiler_params=_cparams(("arbitrary",)),
        name="combine",
    )(dest_flat, ys, x, m3, rw, final_norm)


def _combine_proj_kernel(d_ref, ys_hbm, x_ref, m_ref, rw_ref, mn_ref, gn_ref, w_ref, xo_ref, o_ref, buf, sem, *,
                         n_chunk):
    i = pl.program_id(0)
    n_tiles = N // TM
    n_parts = o_ref.shape[1] // n_chunk

    def issue(tile, slot, part=0, parts=1):
        base = tile * (2 * TM)
        for r in range(part * TM // parts, (part + 1) * TM // parts):
            for kk in range(2):
                _row_copy(ys_hbm, d_ref[base + 2 * r + kk], buf.at[slot, kk], r, sem.at[slot]).start()

    def wait(slot):
        for kk in range(2):
            pltpu.make_async_copy(ys_hbm.at[pl.ds(0, TM)], buf.at[slot, kk], sem.at[slot]).wait()

    @pl.when(i == 0)
    def _():
        issue(0, 0)

    slot = i % 2
    wait(slot)
    w = rw_ref[...]
    x_new = x_ref[...] + m_ref[5:6, :] * (w[:, 0:1] * buf[slot, 0] + w[:, 1:2] * buf[slot, 1])
    xo_ref[...] = x_new
    hb = _modulated_norm(x_new, mn_ref, gn_ref, 0, 1).astype(BF16)
    nxt = jnp.minimum(i + 1, n_tiles - 1)
    for j in range(n_parts):
        issue(nxt, 1 - slot, j, n_parts)
        sl = slice(j * n_chunk, (j + 1) * n_chunk)
        o_ref[:, sl] = jnp.dot(hb, w_ref[:, sl], preferred_element_type=F32)

    @pl.when(i == n_tiles - 1)
    def _():
        wait(1 - slot)


def _combine_proj(dest_flat, ys, x, m3, rw, m3_next, gain_next, w_bf16, n_chunk):
    nout = w_bf16.shape[1]
    grid_spec = pltpu.PrefetchScalarGridSpec(
        num_scalar_prefetch=1,
        grid=(N // TM,),
        in_specs=[pl.BlockSpec(memory_space=pl.ANY),
                  pl.BlockSpec((TM, D), lambda i, d: (i, 0)),
                  pl.BlockSpec((None, 6, D), lambda i, d: (_cond_row(i), 0, 0)),
                  pl.BlockSpec((TM, 2), lambda i, d: (i, 0)),
                  pl.BlockSpec((None, 6, D), lambda i, d: (_cond_row(i), 0, 0)),
                  pl.BlockSpec((1, D), lambda i, d: (0, 0)),
                  pl.BlockSpec((D, nout), lambda i, d: (0, 0))],
        out_specs=[pl.BlockSpec((TM, D), lambda i, d: (i, 0)),
                   pl.BlockSpec((TM, nout), lambda i, d: (i, 0))],
        scratch_shapes=[pltpu.VMEM((2, 2, TM, D), F32), pltpu.SemaphoreType.DMA((2,))])
    return pl.pallas_call(
        functools.partial(_combine_proj_kernel, n_chunk=n_chunk),
        grid_spec=grid_spec,
        out_shape=[jax.ShapeDtypeStruct((N, D), F32), jax.ShapeDtypeStruct((N, nout), F32)],
        compiler_params=_cparams(("arbitrary",)),
        name="combine_proj",
    )(dest_flat, ys, x, m3, rw, m3_next, gain_next, w_bf16)


def _hier_moe(a_pairs, w_list, x, m3, gain, layer, w_rg, b_rg, w_re, b_re, w_gate, w_up, w_down, tri_tm,
              final_norm, final, next_proj=None):
    pad = ROUTE_ROWS - N_GROUPS - N_EXPERTS
    wr_t = jnp.concatenate([w_rg.T, w_re.transpose(0, 2, 1).reshape(N_EXPERTS, D), jnp.zeros((pad, D), F32)], axis=0)
    br_t = jnp.broadcast_to(jnp.concatenate([b_rg, b_re.reshape(N_EXPERTS), jnp.zeros((pad,), F32)])[:, None],
                            (ROUTE_ROWS, LANES))
    x, xf, ri_t, rw_t, cnt = _router(a_pairs, w_list, x, m3, gain, wr_t, br_t, tri_tm)
    e_idx = ri_t[0:2].T
    rank = ri_t[2:4].T
    rw = rw_t[0:2].T
    counts = cnt[0, N_GROUPS:N_GROUPS + N_EXPERTS].astype(jnp.int32)
    padded = (counts + MOE_ROWS - 1) // MOE_ROWS * MOE_ROWS
    end_pad = jnp.cumsum(padded)
    start_pad = end_pad - padded
    experts = jnp.arange(N_EXPERTS, dtype=jnp.int32)
    start_of = jnp.sum(jnp.where(e_idx[:, :, None] == experts, start_pad, 0), axis=-1)
    dest = (start_of + rank).reshape(-1).astype(jnp.int32)
    block_start = jnp.arange(N_SLOT_BLOCKS, dtype=jnp.int32) * MOE_ROWS
    block_e = jnp.minimum(jnp.sum((end_pad[None, :] <= block_start[:, None]).astype(jnp.int32), axis=1),
                          N_EXPERTS - 1)
    n_used = (end_pad[-1:] // MOE_ROWS).astype(jnp.int32)
    tail = jnp.stack([end_pad[-1], jnp.minimum(end_pad[-1] + MOE_ROWS, N_SLOT_BLOCKS * MOE_ROWS)])
    pad_ranges = jnp.concatenate([jnp.stack([start_pad + counts, end_pad], axis=1).reshape(-1), tail]).astype(jnp.int32)
    ys = _experts(block_e, n_used, dest, pad_ranges, xf, w_gate, w_up, w_down, layer)
    if not final:
        return _combine_proj(dest, ys, x, m3, rw, *next_proj)
    nct = NC // TM
    return (_combine(dest, ys, x, m3, rw, final_norm, True, 0, nct),
            _combine(dest, ys, x, m3, rw, final_norm, True, nct, N // TM - nct))


def kernel(x_prompt, x_sample, state_A_fwd, state_A_bwd, cache_k, cache_v, c, c_ctx, mod_w, mod_b, norm_mix, norm_ffn, ab_w_in, ab_conv, ab_a_log, ab_dt_bias, ab_o_gain, ab_w_out, na_w_qkv, na_rpb, na_w_out, moe_w_rg, moe_b_rg, moe_w_re, moe_b_re, moe_w_gate, moe_w_up, moe_w_down, final_norm):
    x = (x_prompt.reshape(NC, D), x_sample.reshape(NL, D))
    cond8 = jnp.concatenate([c_ctx[None, :], c, jnp.zeros((8 - 1 - DEC_BATCH, D), F32)], axis=0)
    mods = _ada_params(cond8, mod_w, mod_b).reshape(DEPTH, 8, 6, D)

    tri_tm = jnp.asarray(np.triu(np.ones((TM, TM)), 1), BF16)
    tri_c = jnp.asarray(np.tril(np.ones((CHUNK, CHUNK))), F32)
    half = np.arange(LANES) < A_DH
    jbd = jnp.asarray((half[:, None] == half[None, :]).astype(np.float32))
    fn = final_norm[None, :]

    m3 = mods[0]
    w_in = ab_w_in[0]
    w_in = jnp.concatenate([w_in[:, :2048], w_in[:, 2080:2592], w_in[:, 2048:2080],
                            jnp.zeros((D, AB_COLS - 2592), F32)], axis=1).astype(BF16)
    proj = _modproj(x, m3, norm_mix[0][None, :], w_in, 896)
    gate_p = jnp.broadcast_to(jnp.stack([ab_a_log[0].reshape(-1), ab_dt_bias[0].reshape(-1)])[:, :, None],
                              (2, 2 * A_HEADS, LANES))
    o_gain2 = jnp.tile(ab_o_gain[0], 2)[None, :]
    zeros_state = jnp.zeros((BATCH, A_HEADS, A_DH, A_DH), F32)
    mix_a_c, s_f, s_b = _deltanet(proj, ab_conv[0], gate_p, o_gain2, zeros_state, zeros_state, tri_c, jbd,
                                  seq_len=SEQ, n_seq=BATCH, n_sub=4, row_blk0=0)
    mix_a_l, _, _ = _deltanet(proj, ab_conv[0], gate_p, o_gain2, state_A_fwd[:, 0], state_A_bwd[:, 0], tri_c, jbd,
                              seq_len=DEC_SEQ, n_seq=DEC_BATCH, n_sub=1, row_blk0=NC // DEC_SEQ)

    cc, sc = _dft_mats(LANES)
    ct, st = _dft_mats(SEQ)
    c64, s64 = _dft_mats(64)
    cs = jnp.asarray(np.concatenate([cc, sc], axis=1), BF16)
    dft = jnp.asarray(np.concatenate([ct, -st], axis=1), BF16)
    ca = jnp.asarray(np.concatenate([cc, -sc, -sc, -cc], axis=1), BF16)
    m1 = jnp.asarray(np.concatenate([c64, s64], axis=1), BF16)
    tw_idx = np.arange(64)
    tw_ang = 2.0 * np.pi * (tw_idx[:, None] * tw_idx[None, :]) / DEC_SEQ
    twc = jnp.broadcast_to(jnp.asarray(np.cos(tw_ang), F32)[:, :, None], (64, 64, LANES))
    tws = jnp.broadcast_to(jnp.asarray(np.sin(tw_ang), F32)[:, :, None], (64, 64, LANES))
    mix_b_c = _fnet_ctx(proj, cs, dft)
    mix_b_l = _fnet_latent(proj, ca, m1, twc, tws)

    w_out = ab_w_out[0].astype(BF16)
    x, qkv = _hier_moe([(mix_a_c, mix_a_l), (mix_b_c, mix_b_l)], [w_out[:A_WIDTH], w_out[A_WIDTH:]], x, m3,
                       norm_ffn[0][None, :], 0, moe_w_rg[0], moe_b_rg[0], moe_w_re[0], moe_b_re[0],
                       moe_w_gate, moe_w_up, moe_w_down, tri_tm, fn, False,
                       next_proj=(mods[1], norm_mix[1][None, :], na_w_qkv[0].astype(BF16), 512))

    m3 = mods[1]
    attn_c = _na_ctx(qkv)
    attn_l = _na_latent(qkv, cache_k[:, 0].reshape(DEC_BATCH, 256, D), cache_v[:, 0].reshape(DEC_BATCH, 256, D),
                        _rpb_tables(na_rpb[0]))
    y_c, y_l = _hier_moe([(attn_c, attn_l)], [na_w_out[0].astype(BF16)], x, m3, norm_ffn[1][None, :], 1, moe_w_rg[1], moe_b_rg[1], moe_w_re[1], moe_b_re[1],
                         moe_w_gate, moe_w_up, moe_w_down, tri_tm, fn, True)

    new_k = qkv[:NC, D:2 * D].reshape(BATCH, 1, SEQ, NA_HEADS, NA_DH)
    new_v = qkv[:NC, 2 * D:].reshape(BATCH, 1, SEQ, NA_HEADS, NA_DH)
    return (y_c.reshape(BATCH, SEQ, D), y_l.reshape(DEC_BATCH, DEC_SEQ, D),
            s_f[:, None], s_b[:, None], new_k, new_v)
```

```python
import functools
import math

import numpy as np
import jax
import jax.numpy as jnp
from jax import lax
from jax.experimental import pallas as pl
from jax.experimental.pallas import tpu as pltpu

F32 = jnp.float32
BF16 = jnp.bfloat16
HIGHEST = lax.Precision.HIGHEST

D = 1024
BATCH, SEQ = 32, 256
DEC_BATCH, DEC_SEQ = 2, 4096
NC = BATCH * SEQ
NL = DEC_BATCH * DEC_SEQ
N = NC + NL
DEPTH = 2
GRID_W = 64
A_DH = 64
A_HEADS = 8
A_WIDTH = 512
CHUNK = 64
B_WIDTH = 512
B_GROUPS = 4
NA_DH = 64
NA_HEADS = 16
WIN_R, WIN_C = 8, 16
N_GROUPS, EXP_PER_GROUP, N_EXPERTS = 4, 8, 32
D_EXPERT = 512
EPS = 1e-6

LANES = 128
TM = 256
MOE_ROWS = 256
N_SLOT_BLOCKS = (2 * N) // MOE_ROWS + N_EXPERTS
AB_COLS = 2688
VMEM_LIMIT = 56 * 1024 * 1024
EXPERT_VMEM_LIMIT = 60 * 1024 * 1024
NEG = -1e30


def _cparams(sem):
    return pltpu.CompilerParams(dimension_semantics=sem, vmem_limit_bytes=VMEM_LIMIT)


def _mm(a, b):
    return jnp.dot(a.astype(BF16), b.astype(BF16), preferred_element_type=F32)


def _mm_nt(a, b):
    return lax.dot_general(a.astype(BF16), b.astype(BF16), (((1,), (1,)), ((), ())),
                           preferred_element_type=F32)


SOLVE_BLK = 16
NA_SCALE = NA_DH ** -0.5
NA_CTX_PAIRS = 8
NA_ROW_UNROLL = 16
TERM_UNROLL = 8


def _unit_lower_solve_many(a_mats, rhs, same_blk, eye):
    off = [jnp.where(same_blk, 0.0, a).astype(BF16) for a in a_mats]
    p = [jnp.where(same_blk, -a, 0.0).astype(BF16) for a in a_mats]
    dinv = [(eye + x.astype(F32)).astype(BF16) for x in p]
    for _ in range(int(math.log2(SOLVE_BLK)) - 1):
        p = [_mm(x, x).astype(BF16) for x in p]
        dinv = [(di.astype(F32) + _mm(x, di)).astype(BF16) for x, di in zip(p, dinv)]
    mp = [(-_mm(di, o)).astype(BF16) for di, o in zip(dinv, off)]
    y = [_mm(di, r) for di, r in zip(dinv, rhs)]
    y = [yi + _mm(m, yi) for m, yi in zip(mp, y)]
    for _ in range(int(math.log2(a_mats[0].shape[0] // SOLVE_BLK)) - 1):
        mp = [_mm(m, m).astype(BF16) for m in mp]
        y = [yi + _mm(m, yi) for m, yi in zip(mp, y)]
    return y


def _mm_split(a, b, parts, split_rhs=False):
    x = b if split_rhs else a
    acc = None
    for _ in range(parts):
        piece = x.astype(BF16)
        x = x - piece.astype(F32)
        term = (jnp.dot(a.astype(BF16), piece, preferred_element_type=F32) if split_rhs
                else jnp.dot(piece, b.astype(BF16), preferred_element_type=F32))
        acc = term if acc is None else acc + term
    return acc


def _mm_hi(a, b):
    return jnp.dot(a, b, preferred_element_type=F32, precision=HIGHEST)


def _silu(x):
    return x * jax.nn.sigmoid(x)


def _bf16_bits(x):
    b = lax.bitcast_convert_type(x, jnp.int32)
    return b + 0x7FFF + (lax.shift_right_logical(b, jnp.int32(16)) & 1)


_HIGH16 = -65536


def _pack_bf16_pairs(a, b):
    return lax.shift_right_logical(_bf16_bits(a), jnp.int32(16)) | (_bf16_bits(b) & _HIGH16)


def _unpack_bf16_pairs(p):
    a = lax.bitcast_convert_type(lax.shift_left(p, jnp.int32(16)), F32)
    b = lax.bitcast_convert_type(p & _HIGH16, F32)
    return a.astype(BF16), b.astype(BF16)


def _cond_row(i):
    return jnp.where(i < NC // TM, 0, 1 + (i - NC // TM) // (DEC_SEQ // TM))


def _modulated_norm(x, m_ref, g_ref, shift_idx, scale_idx):
    ms = jnp.mean(x * x, axis=-1, keepdims=True)
    y = x * lax.rsqrt(ms + EPS) * g_ref[...]
    return y * (1.0 + m_ref[scale_idx:scale_idx + 1, :]) + m_ref[shift_idx:shift_idx + 1, :]


def _ada_kernel(cond_ref, w_ref, b_ref, o_ref):
    o_ref[...] = _mm_hi(_silu(cond_ref[...]), w_ref[...]) + b_ref[...]


def _ada_params(cond8, mod_w, mod_b):
    tn = 1536
    return pl.pallas_call(
        _ada_kernel,
        grid=(DEPTH, 6 * D // tn),
        in_specs=[pl.BlockSpec((8, D), lambda l, j: (0, 0)),
                  pl.BlockSpec((None, D, tn), lambda l, j: (l, 0, j)),
                  pl.BlockSpec((None, 1, tn), lambda l, j: (l, 0, j))],
        out_specs=pl.BlockSpec((None, 8, tn), lambda l, j: (l, 0, j)),
        out_shape=jax.ShapeDtypeStruct((DEPTH, 8, 6 * D), F32),
        compiler_params=_cparams(("arbitrary", "arbitrary")),
        name="ada_params",
    )(cond8, mod_w, mod_b.reshape(DEPTH, 1, 6 * D))


def _token_specs(x):
    nct = NC // TM
    if isinstance(x, tuple):
        return ([pl.BlockSpec((TM, x[0].shape[1]), lambda i: (jnp.minimum(i, nct - 1), 0)),
                 pl.BlockSpec((TM, x[1].shape[1]), lambda i: (jnp.maximum(i - nct, 0), 0))], list(x))
    return [pl.BlockSpec((TM, x.shape[1]), lambda i: (i, 0))], [x]


def _token_rows(refs):
    if len(refs) == 1:
        return refs[0][...]
    return jnp.where(pl.program_id(0) < NC // TM, refs[0][...], refs[1][...])


def _modproj_kernel(*refs, n_chunk):
    m_ref, g_ref, w_ref, o_ref = refs[-4:]
    hb = _modulated_norm(_token_rows(refs[:-4]), m_ref, g_ref, 0, 1).astype(BF16)
    for j in range(o_ref.shape[1] // n_chunk):
        sl = slice(j * n_chunk, (j + 1) * n_chunk)
        o_ref[:, sl] = jnp.dot(hb, w_ref[:, sl], preferred_element_type=F32)


def _modproj(x, m3, gain, w_bf16, n_chunk):
    nout = w_bf16.shape[1]
    x_specs, x_args = _token_specs(x)
    return pl.pallas_call(
        functools.partial(_modproj_kernel, n_chunk=n_chunk),
        grid=(N // TM,),
        in_specs=x_specs + [pl.BlockSpec((None, 6, D), lambda i: (_cond_row(i), 0, 0)),
                            pl.BlockSpec((1, D), lambda i: (0, 0)),
                            pl.BlockSpec((D, nout), lambda i: (0, 0))],
        out_specs=pl.BlockSpec((TM, nout), lambda i: (i, 0)),
        out_shape=jax.ShapeDtypeStruct((N, nout), F32),
        compiler_params=_cparams(("arbitrary",)),
        name="modproj",
    )(*x_args, m3, gain, w_bf16)


def _deltanet_kernel(q_ref, k_ref, v_ref, z_ref, ab_ref, cq_ref, ck_ref, cv_ref, gp_ref, og_ref,
                     s0f_ref, s0b_ref, tri_ref, jbd_ref, o_ref, sf_ref, sb_ref,
                     u_s, w_s, qd_s, at_s, kt_s, ge_s, st_s, ob, *, seq_len, n_sub):
    hp = pl.program_id(1)
    C = CHUNK
    nc = seq_len // C
    nct = n_sub * nc
    P = LANES
    lane = lax.broadcasted_iota(jnp.int32, (C, P), 1)
    row = lax.broadcasted_iota(jnp.int32, (C, P), 0)
    first_head = lane < A_DH
    ri = lax.broadcasted_iota(jnp.int32, (P, P), 0)
    ci = lax.broadcasted_iota(jnp.int32, (P, P), 1)
    same_head = (ri < C) == (ci < C)
    same_blk = (ri // SOLVE_BLK) == (ci // SOLVE_BLK)
    eye = jnp.where(ri == ci, 1.0, 0.0)
    jbd = jbd_ref[...]
    lincl = tri_ref[...]
    cum_b = [lincl.T.astype(BF16), lincl.astype(BF16)]
    incl_m = [jnp.logical_and(same_head, ri >= ci), jnp.logical_and(same_head, ri <= ci)]
    strict_m = [jnp.logical_and(same_head, ri > ci), jnp.logical_and(same_head, ri < ci)]
    neg_a = -jnp.exp(gp_ref[0])
    dt_b = gp_ref[1]

    def conv_silu(ref, w_ref, c):
        base = pl.multiple_of(c * C, C)
        cs = c % nc
        xc = ref[pl.ds(base, C), :]
        pbase = pl.multiple_of(jnp.maximum(base - 8, 0), 8)
        nbase = pl.multiple_of(jnp.minimum(base + C, n_sub * seq_len - 8), 8)
        prev_row = ref[pl.ds(pbase, 8), :][7:8, :] * jnp.where(cs > 0, 1.0, 0.0)
        next_row = ref[pl.ds(nbase, 8), :][0:1, :] * jnp.where(cs < nc - 1, 1.0, 0.0)
        x_prev = jnp.where(row == 0, prev_row, pltpu.roll(xc, 1, 0))
        x_next = jnp.where(row == C - 1, next_row, pltpu.roll(xc, C - 1, 0))
        y = w_ref[0:1, :] * x_prev + w_ref[1:2, :] * xc + w_ref[2:3, :] * x_next
        return _silu(y)

    def stack(x):
        return jnp.concatenate([jnp.where(first_head, x, 0.0), jnp.where(first_head, 0.0, x)], axis=0)

    def chunk_inputs(c):
        base = pl.multiple_of(c * C, C)
        q = conv_silu(q_ref, cq_ref, c)
        k = conv_silu(k_ref, ck_ref, c)
        v = conv_silu(v_ref, cv_ref, c)
        return q, k, v, ab_ref[pl.ds(base, C), :].T

    sub8 = lax.broadcasted_iota(jnp.int32, (A_HEADS, C), 0)

    def pair_row(x8):
        r0 = jnp.sum(jnp.where(sub8 == 2 * hp, x8, 0.0), axis=0, keepdims=True)
        r1 = jnp.sum(jnp.where(sub8 == 2 * hp + 1, x8, 0.0), axis=0, keepdims=True)
        return jnp.concatenate([r0, r1], axis=1)

    def chain_gates(ab_t, d):
        a8 = ab_t[d * A_HEADS:(d + 1) * A_HEADS, :]
        b8 = ab_t[(2 + d) * A_HEADS:(3 + d) * A_HEADS, :]
        g8 = neg_a[d * A_HEADS:(d + 1) * A_HEADS, :C] * jax.nn.softplus(a8 + dt_b[d * A_HEADS:(d + 1) * A_HEADS, :C])
        gc8 = _mm_split(g8, cum_b[d], 3)
        tot8 = jnp.broadcast_to(jnp.sum(g8, axis=-1, keepdims=True), (A_HEADS, C))
        gc_row, beta_row, tot_row = pair_row(gc8), pair_row(jax.nn.sigmoid(b8)), pair_row(tot8)
        cols = jnp.concatenate([gc_row, beta_row, tot_row, jnp.zeros((5, P), F32)], axis=0).T
        return gc_row, tot_row, cols[:, 0:1], cols[:, 1:2], cols[:, 2:3]

    def terms_body(j, carry):
        cs = [TERM_UNROLL * j + t for t in range(TERM_UNROLL)]
        ins = [chunk_inputs(c) for c in cs]
        qsq = [_mm_split(x[0] * x[0], jbd, 2) for x in ins]
        ksq = [_mm_split(x[1] * x[1], jbd, 2) for x in ins]
        qs = [x[0] * lax.rsqrt(s + EPS) * (A_DH ** -0.5) for x, s in zip(ins, qsq)]
        ks = [x[1] * lax.rsqrt(s + EPS) for x, s in zip(ins, ksq)]
        qst = [stack(x) for x in qs]
        kst = [stack(x) for x in ks]
        vst = [stack(x[2]) for x in ins]
        kst_t = [x.T for x in kst]
        kk = [_mm_nt(x, x) for x in kst]
        qk = [_mm_nt(x, y) for x, y in zip(qst, kst)]
        chains = [(t, d) for t in range(TERM_UNROLL) for d in range(2)]
        gates = [chain_gates(ins[t][3], d) for t, d in chains]
        decay, e_gc = [], []
        for (t, d), (gc_row, tot_row, gc_col, beta_col, tot_col) in zip(chains, gates):
            diff = jnp.broadcast_to(gc_col, (P, P)) - jnp.broadcast_to(gc_row, (P, P))
            decay.append(jnp.where(incl_m[d], jnp.exp(jnp.where(incl_m[d], diff, 0.0)), 0.0))
            e_gc.append(jnp.exp(gc_col))
        a_mats = [jnp.where(strict_m[d], g[3] * kk[t] * dc, 0.0) for (t, d), g, dc in zip(chains, gates, decay)]
        rhs = [vst[t] * g[3] + pltpu.roll(kst[t] * (g[3] * e), A_DH, 1)
               for (t, d), g, e in zip(chains, gates, e_gc)]
        xs = _unit_lower_solve_many(a_mats, rhs, same_blk, eye)
        for (t, d), x, g, e, dc in zip(chains, xs, gates, e_gc, decay):
            c = cs[t]
            gc_row, tot_row = g[0], g[1]
            u_s[d, c] = jnp.where(same_head, x, 0.0).astype(BF16)
            w_s[d, c] = pltpu.roll(jnp.where(same_head, 0.0, x), A_DH, 1).astype(BF16)
            qd_s[d, c] = (qst[t] * e).astype(BF16)
            at_s[d, c] = jnp.where(incl_m[d], qk[t] * dc, 0.0).astype(BF16)
            kt_s[d, c] = (kst_t[t] * jnp.exp(tot_row - gc_row)).astype(BF16)
            ge_s[d, c] = jnp.broadcast_to(jnp.exp(tot_row), (8, P))
        return carry

    lax.fori_loop(0, nct // TERM_UNROLL, terms_body, 0)

    def block_diag(s2):
        z = jnp.zeros((A_DH, A_DH), F32)
        return jnp.concatenate([jnp.concatenate([s2[0], z], axis=1),
                                jnp.concatenate([z, s2[1]], axis=1)], axis=0)

    for s in range(n_sub):
        st_s[2 * s] = block_diag(s0f_ref[s])
        st_s[2 * s + 1] = block_diag(s0b_ref[s])

    def scan_body(i, carry):
        chains = [(s, d, s * nc + (i if d == 0 else nc - 1 - i)) for s in range(n_sub) for d in range(2)]
        dot = functools.partial(jnp.dot, preferred_element_type=F32)
        s_bd = [st_s[2 * s + d] for s, d, c in chains]
        sb16 = [x.astype(BF16) for x in s_bd]
        ws = [dot(w_s[d, c], sb) for (s, d, c), sb in zip(chains, sb16)]
        qs_ = [dot(qd_s[d, c], sb) for (s, d, c), sb in zip(chains, sb16)]
        vb = [(u_s[d, c].astype(F32) - x).astype(BF16) for (s, d, c), x in zip(chains, ws)]
        av = [dot(at_s[d, c], x) for (s, d, c), x in zip(chains, vb)]
        kv = [dot(kt_s[d, c], x) for (s, d, c), x in zip(chains, vb)]
        for (s, d, c), sb, q_, a_, k_ in zip(chains, s_bd, qs_, av, kv):
            st_s[2 * s + d] = sb * ge_s[d, c][0:1, :] + k_
            o_st = q_ + a_
            dst = o_ref if d == 0 else ob
            dst[pl.ds(pl.multiple_of(c * C, C), C), :] = o_st[:C] + o_st[C:]
        return carry

    lax.fori_loop(0, nc, scan_body, 0)

    for s in range(n_sub):
        for d, ref in ((0, sf_ref), (1, sb_ref)):
            s_bd = st_s[2 * s + d]
            ref[s, 0] = s_bd[:A_DH, :A_DH]
            ref[s, 1] = s_bd[A_DH:, A_DH:]

    def finish(j, carry):
        bases = [pl.multiple_of((TERM_UNROLL * j + t) * C, C) for t in range(TERM_UNROLL)]
        o = [o_ref[pl.ds(b, C), :] + ob[pl.ds(b, C), :] for b in bases]
        ms = [_mm_split(x * x, jbd, 2) * (1.0 / A_DH) for x in o]
        for b, x, m in zip(bases, o, ms):
            o_ref[pl.ds(b, C), :] = x * lax.rsqrt(m + EPS) * og_ref[...] * _silu(z_ref[pl.ds(b, C), :])
        return carry

    lax.fori_loop(0, nct // TERM_UNROLL, finish, 0)


def _deltanet(proj, conv_w, gate_p, o_gain2, s0f, s0b, tri, jbd, *, seq_len, n_seq, n_sub, row_blk0):
    rows = n_sub * seq_len
    nct = rows // CHUNK
    rb = lambda b: row_blk0 + b
    col = lambda off: (lambda b, hp: (rb(b), off + hp))
    st_spec = pl.BlockSpec((n_sub, 2, A_DH, A_DH), lambda b, hp: (b, hp, 0, 0))
    in_specs = [pl.BlockSpec((rows, LANES), col(0)),
                pl.BlockSpec((rows, LANES), col(4)),
                pl.BlockSpec((rows, LANES), col(8)),
                pl.BlockSpec((rows, LANES), col(12)),
                pl.BlockSpec((rows, LANES), lambda b, hp: (rb(b), 20)),
                pl.BlockSpec((3, LANES), lambda b, hp: (0, hp)),
                pl.BlockSpec((3, LANES), lambda b, hp: (0, 4 + hp)),
                pl.BlockSpec((3, LANES), lambda b, hp: (0, 8 + hp)),
                pl.BlockSpec((2, 2 * A_HEADS, LANES), lambda b, hp: (0, 0, 0)),
                pl.BlockSpec((1, LANES), lambda b, hp: (0, 0)),
                st_spec, st_spec,
                pl.BlockSpec((CHUNK, CHUNK), lambda b, hp: (0, 0)),
                pl.BlockSpec((LANES, LANES), lambda b, hp: (0, 0))]
    args = [proj, proj, proj, proj, proj, conv_w, conv_w, conv_w, gate_p, o_gain2, s0f, s0b, tri, jbd]
    st_shape = jax.ShapeDtypeStruct((n_seq, A_HEADS, A_DH, A_DH), F32)
    tile = lambda dt: pltpu.VMEM((2, nct, LANES, LANES), dt)
    return pl.pallas_call(
        functools.partial(_deltanet_kernel, seq_len=seq_len, n_sub=n_sub),
        grid=(n_seq // n_sub, A_HEADS // 2),
        in_specs=in_specs,
        out_specs=[pl.BlockSpec((rows, LANES), lambda b, hp: (b, hp)), st_spec, st_spec],
        out_shape=[jax.ShapeDtypeStruct((n_seq * seq_len, A_WIDTH), F32), st_shape, st_shape],
        scratch_shapes=[tile(BF16), tile(BF16), tile(BF16), tile(BF16), tile(BF16),
                        pltpu.VMEM((2, nct, 8, LANES), F32),
                        pltpu.VMEM((2 * n_sub, LANES, LANES), F32),
                        pltpu.VMEM((rows, LANES), F32)],
        compiler_params=_cparams(("arbitrary", "arbitrary")),
        name="deltanet",
    )(*args)


def _dft_mats(n):
    idx = np.arange(n)
    ang = 2.0 * np.pi * ((idx[:, None] * idx[None, :]) % n) / n
    return np.cos(ang), np.sin(ang)


def _fnet_ctx_kernel(u_ref, cs_ref, dft_ref, o_ref):
    norm = 1.0 / math.sqrt(SEQ * LANES)
    sls = [slice(g * LANES, (g + 1) * LANES) for g in range(B_GROUPS)]
    p = [_mm(u_ref[:, sl], cs_ref[...]) for sl in sls]
    stack = [jnp.concatenate([x[:, :LANES], x[:, LANES:]], axis=0) for x in p]
    y = [_mm(dft_ref[...], x) for x in stack]
    for sl, x in zip(sls, y):
        o_ref[:, sl] = x * norm


def _fnet_ctx(proj, cs, dft):
    return pl.pallas_call(
        _fnet_ctx_kernel,
        grid=(BATCH,),
        in_specs=[pl.BlockSpec((SEQ, B_WIDTH), lambda b: (b, 4)),
                  pl.BlockSpec(cs.shape, lambda b: (0, 0)),
                  pl.BlockSpec(dft.shape, lambda b: (0, 0))],
        out_specs=pl.BlockSpec((SEQ, B_WIDTH), lambda b: (b, 0)),
        out_shape=jax.ShapeDtypeStruct((NC, B_WIDTH), F32),
        compiler_params=_cparams(("arbitrary",)),
        name="fnet_ctx",
    )(proj, cs, dft)


FN_SUB = 4


def _fnet_lat1_kernel(u_ref, ca_ref, m1_ref, twc_ref, tws_ref, o_ref):
    r = 64
    units = [(j, g) for j in range(FN_SUB) for g in range(B_GROUPS)]
    rows = lambda j: slice(j * r, (j + 1) * r)
    pa = [_mm(u_ref[rows(j), g * LANES:(g + 1) * LANES], ca_ref[...]) for j, g in units]
    rhs = [jnp.concatenate([x[:, :2 * LANES], x[:, 2 * LANES:]], axis=0) for x in pa]
    zz = [_mm(m1_ref[...], x) for x in rhs]
    for (j, g), z in zip(units, zz):
        c = twc_ref[j]
        s = tws_ref[j]
        zr, zi = z[:, :LANES], z[:, LANES:]
        o_ref[rows(j), 2 * g * LANES:(2 * g + 1) * LANES] = (zr * c + zi * s).astype(BF16)
        o_ref[rows(j), (2 * g + 1) * LANES:(2 * g + 2) * LANES] = (zi * c - zr * s).astype(BF16)


def _fnet_lat2_kernel(z_ref, m1_ref, o_ref):
    r = 64
    norm = 1.0 / math.sqrt(DEC_SEQ * LANES)
    units = [(j, g) for j in range(FN_SUB) for g in range(B_GROUPS)]
    rows = lambda j: slice(j * r, (j + 1) * r)
    rhs = [jnp.concatenate([z_ref[rows(j), 2 * g * LANES:(2 * g + 1) * LANES],
                            z_ref[rows(j), (2 * g + 1) * LANES:(2 * g + 2) * LANES]], axis=0) for j, g in units]
    y = [_mm(m1_ref[...], x) for x in rhs]
    for (j, g), x in zip(units, y):
        o_ref[rows(j), g * LANES:(g + 1) * LANES] = x * norm


def _fnet_latent(proj, ca, m1, twc, tws):
    r = 64
    u = proj[NC:, 2048:2560].astype(BF16).reshape(DEC_BATCH, r, r, B_WIDTH)
    u = u.transpose(0, 2, 1, 3).reshape(DEC_BATCH * r * r, B_WIDTH)
    steps = DEC_BATCH * r // FN_SUB
    blk = FN_SUB * r
    z = pl.pallas_call(
        _fnet_lat1_kernel,
        grid=(steps,),
        in_specs=[pl.BlockSpec((blk, B_WIDTH), lambda s: (s, 0)),
                  pl.BlockSpec(ca.shape, lambda s: (0, 0)),
                  pl.BlockSpec(m1.shape, lambda s: (0, 0)),
                  pl.BlockSpec((FN_SUB, r, LANES), lambda s: (s % (r // FN_SUB), 0, 0)),
                  pl.BlockSpec((FN_SUB, r, LANES), lambda s: (s % (r // FN_SUB), 0, 0))],
        out_specs=pl.BlockSpec((blk, 2 * B_WIDTH), lambda s: (s, 0)),
        out_shape=jax.ShapeDtypeStruct((NL, 2 * B_WIDTH), BF16),
        compiler_params=_cparams(("arbitrary",)),
        name="fnet_lat1",
    )(u, ca, m1, twc, tws)
    z = z.reshape(DEC_BATCH, r, r, 2 * B_WIDTH).transpose(0, 2, 1, 3).reshape(NL, 2 * B_WIDTH)
    y = pl.pallas_call(
        _fnet_lat2_kernel,
        grid=(steps,),
        in_specs=[pl.BlockSpec((blk, 2 * B_WIDTH), lambda s: (s, 0)),
                  pl.BlockSpec(m1.shape, lambda s: (0, 0))],
        out_specs=pl.BlockSpec((blk, B_WIDTH), lambda s: (s, 0)),
        out_shape=jax.ShapeDtypeStruct((NL, B_WIDTH), F32),
        compiler_params=_cparams(("arbitrary",)),
        name="fnet_lat2",
    )(z, m1)
    return y.reshape(DEC_BATCH, r, r, B_WIDTH).transpose(0, 2, 1, 3).reshape(NL, B_WIDTH)


def _head_masks():
    lane = lax.broadcasted_iota(jnp.int32, (1, LANES), 1)
    return lane < NA_DH


def _attend_many(chains):
    s = [[_mm_nt(q, k) if b is None else _mm_nt(q, k) + b for k, v, b in kv] for q, kv in chains]
    m = [functools.reduce(jnp.maximum, [jnp.max(x, axis=-1, keepdims=True) for x in xs]) for xs in s]
    p = [[jnp.exp(x - mi) for x in xs] for xs, mi in zip(s, m)]
    l = [sum(jnp.sum(x, axis=-1, keepdims=True) for x in xs) for xs in p]
    o = [sum(_mm(x, v) for x, (k, v, b) in zip(xs, kv)) for xs, (q, kv) in zip(p, chains)]
    return [oi / li for oi, li in zip(o, l)]


def _na_ctx_kernel(q_ref, k_ref, v_ref, o_ref):
    first = _head_masks()
    n_split = 2
    rows = SEQ // n_split
    chains = []
    for p in range(NA_CTX_PAIRS):
        lanes = slice(p * LANES, (p + 1) * LANES)
        k = k_ref[:, lanes].astype(BF16)
        v = v_ref[:, lanes].astype(BF16)
        for j in range(n_split):
            q = q_ref[j * rows:(j + 1) * rows, lanes]
            for a in range(2):
                chains.append((jnp.where(first if a == 0 else jnp.logical_not(first), q * NA_SCALE, 0.0),
                               [(k, v, None)]))
    outs = _attend_many(chains)
    for p in range(NA_CTX_PAIRS):
        for j in range(n_split):
            i = 2 * (p * n_split + j)
            o_ref[j * rows:(j + 1) * rows, p * LANES:(p + 1) * LANES] = jnp.where(first, outs[i], outs[i + 1])


def _na_ctx(qkv):
    w = NA_CTX_PAIRS * LANES
    nblk = D // w
    return pl.pallas_call(
        _na_ctx_kernel,
        grid=(BATCH, nblk),
        in_specs=[pl.BlockSpec((SEQ, w), lambda b, hp: (b, hp)),
                  pl.BlockSpec((SEQ, w), lambda b, hp: (b, nblk + hp)),
                  pl.BlockSpec((SEQ, w), lambda b, hp: (b, 2 * nblk + hp))],
        out_specs=pl.BlockSpec((SEQ, w), lambda b, hp: (b, hp)),
        out_shape=jax.ShapeDtypeStruct((NC, D), F32),
        compiler_params=_cparams(("arbitrary", "arbitrary")),
        name="na_ctx",
    )(qkv, qkv, qkv)


def _na_lat_kernel(q_ref, k_ref, v_ref, kc_ref, vc_ref, tt_ref, o_ref):
    first = _head_masks()
    rows = DEC_SEQ // GRID_W
    kctx = kc_ref[...].astype(BF16)
    vctx = vc_ref[...].astype(BF16)
    nkeys = WIN_R * GRID_W

    def rows_body(j, carry):
        chains = []
        for t in range(NA_ROW_UNROLL):
            r = NA_ROW_UNROLL * j + t
            r0 = jnp.clip(r - WIN_R // 2, 0, rows - WIN_R)
            dr0 = r0 - r + (WIN_R - 1)
            q = q_ref[pl.ds(pl.multiple_of(r * GRID_W, GRID_W), GRID_W), :] * NA_SCALE
            kbase = pl.multiple_of(r0 * GRID_W, GRID_W)
            kl = k_ref[pl.ds(kbase, nkeys), :].astype(BF16)
            vl = v_ref[pl.ds(kbase, nkeys), :].astype(BF16)
            for a in range(2):
                qm = jnp.where(first if a == 0 else jnp.logical_not(first), q, 0.0)
                bias = jnp.concatenate([tt_ref[a, dr0 + 2 * i] for i in range(WIN_R // 2)], axis=1)
                chains.append((qm, [(kl, vl, bias), (kctx, vctx, None)]))
        outs = _attend_many(chains)
        for t in range(NA_ROW_UNROLL):
            r = NA_ROW_UNROLL * j + t
            o_ref[pl.ds(pl.multiple_of(r * GRID_W, GRID_W), GRID_W), :] = jnp.where(first, outs[2 * t], outs[2 * t + 1])
        return carry

    lax.fori_loop(0, rows // NA_ROW_UNROLL, rows_body, 0)


def _na_latent(qkv, cache_k2, cache_v2, tt2):
    rb0 = NC // DEC_SEQ
    return pl.pallas_call(
        _na_lat_kernel,
        grid=(DEC_BATCH, NA_HEADS // 2),
        in_specs=[pl.BlockSpec((DEC_SEQ, LANES), lambda b, hp: (rb0 + b, hp)),
                  pl.BlockSpec((DEC_SEQ, LANES), lambda b, hp: (rb0 + b, 8 + hp)),
                  pl.BlockSpec((DEC_SEQ, LANES), lambda b, hp: (rb0 + b, 16 + hp)),
                  pl.BlockSpec((None, 256, LANES), lambda b, hp: (b, 0, hp)),
                  pl.BlockSpec((None, 256, LANES), lambda b, hp: (b, 0, hp)),
                  pl.BlockSpec((2, 2 * WIN_R - 2, GRID_W, LANES), lambda b, hp: (hp, 0, 0, 0))],
        out_specs=pl.BlockSpec((DEC_SEQ, LANES), lambda b, hp: (b, hp)),
        out_shape=jax.ShapeDtypeStruct((NL, D), F32),
        compiler_params=_cparams(("arbitrary", "arbitrary")),
        name="na_latent",
    )(qkv, qkv, qkv, cache_k2, cache_v2, tt2)


def _rpb_tables(rpb):
    col = np.arange(GRID_W)
    start = np.clip(col - WIN_C // 2, 0, GRID_W - WIN_C)
    inside = (col[None, :] >= start[:, None]) & (col[None, :] < start[:, None] + WIN_C)
    w = GRID_W
    period = 2 * w - 1
    x = jnp.pad(rpb, ((0, 0), (0, 0), (w - WIN_C, w - WIN_C)))
    flat = jnp.tile(x, (1, 1, w))[:, :, w - 1:w - 1 + w * (period - 1)]
    t = flat.reshape(NA_HEADS, 2 * WIN_R - 1, w, period - 1)[..., :w]
    t = jnp.where(inside[None, None], t, NEG)
    return jnp.concatenate([t[:, :-1], t[:, 1:]], axis=-1)


ROUTE_SPLIT = 2
ROUTE_ROWS = 40


def _router_kernel(*refs, n_in):
    a_refs = refs[:2 * n_in]
    w_refs = refs[2 * n_in:3 * n_in]
    x_refs = refs[3 * n_in:-11]
    m_ref, g_ref, wr_ref, br_ref, tri_ref, xn_ref, xf_ref, ri_ref, rw_ref, cnt_ref, base_scr = refs[-11:]
    i = pl.program_id(0)

    @pl.when(i == 0)
    def _():
        base_scr[...] = jnp.zeros_like(base_scr)
        cnt_ref[...] = jnp.zeros_like(cnt_ref)

    rows = [slice(t * TM // ROUTE_SPLIT, (t + 1) * TM // ROUTE_SPLIT) for t in range(ROUTE_SPLIT)]
    is_ctx = pl.program_id(0) < NC // TM
    pick = lambda pair, r: jnp.where(is_ctx, pair[0][r, :], pair[1][r, :]) if len(pair) == 2 else pair[0][r, :]
    acc = [sum(_mm(pick(a_refs[2 * j:2 * j + 2], r), w_ref[...]) for j, w_ref in enumerate(w_refs)) for r in rows]
    x_new = [pick(x_refs, r) + m_ref[2:3, :] * a for r, a in zip(rows, acc)]
    for r, xn in zip(rows, x_new):
        xn_ref[r, :] = xn
    h = [_modulated_norm(xn, m_ref, g_ref, 3, 4) for xn in x_new]
    for r, hh in zip(rows, h):
        xf_ref[r, :] = _pack_bf16_pairs(hh[:, :D // 2], hh[:, D // 2:])
    logits = jnp.concatenate(
        [lax.dot_general(wr_ref[...], hh, (((1,), (1,)), ((), ())), preferred_element_type=F32, precision=HIGHEST)
         for hh in h], axis=1) + br_ref[:, 0:1]
    row = lax.broadcasted_iota(jnp.int32, logits.shape, 0)
    cmax = lambda x: jnp.max(x, axis=0, keepdims=True)
    cmin = lambda x: jnp.min(x, axis=0, keepdims=True)
    csum = lambda x: jnp.sum(x, axis=0, keepdims=True)

    gmask = row < N_GROUPS
    mg = cmax(jnp.where(gmask, logits, NEG))
    eg = jnp.where(gmask, jnp.exp(jnp.where(gmask, logits - mg, NEG)), 0.0)
    pg = eg / csum(eg)
    p_grp = cmax(pg)
    grp = cmin(jnp.where(jnp.logical_and(gmask, pg == p_grp), row, ROUTE_ROWS))
    lo = N_GROUPS + grp * EXP_PER_GROUP
    emask = jnp.logical_and(row >= lo, row < lo + EXP_PER_GROUP)
    me = cmax(jnp.where(emask, logits, NEG))
    ee = jnp.where(emask, jnp.exp(jnp.where(emask, logits - me, NEG)), 0.0)
    pe = ee / csum(ee)
    p1 = cmax(pe)
    i1 = cmin(jnp.where(jnp.logical_and(emask, pe == p1), row, ROUTE_ROWS))
    m2 = jnp.logical_and(emask, row != i1)
    p2 = cmax(jnp.where(m2, pe, -1.0))
    i2 = cmin(jnp.where(jnp.logical_and(m2, pe == p2), row, ROUTE_ROWS))
    den = p1 + p2
    w1 = p_grp * p1 / den
    w2 = p_grp * p2 / den

    sel1 = row == i1
    sel2 = row == i2
    oh = jnp.where(jnp.logical_or(sel1, sel2), 1.0, 0.0).astype(BF16)
    before = jnp.dot(oh, tri_ref[...], preferred_element_type=F32) + base_scr[:, 0:1]
    rank1 = csum(jnp.where(sel1, before, 0.0))
    rank2 = csum(jnp.where(sel2, before, 0.0))
    base_scr[...] = base_scr[...] + jnp.sum(oh.astype(F32), axis=1, keepdims=True)
    cnt_ref[...] = cnt_ref[...] + lax.dot_general(jnp.ones((8, TM), BF16), oh, (((1,), (1,)), ((), ())),
                                                  preferred_element_type=F32)
    sub = lax.broadcasted_iota(jnp.int32, (8, TM), 0)
    ri_ref[...] = jnp.where(sub == 0, i1 - N_GROUPS, jnp.where(sub == 1, i2 - N_GROUPS,
                  jnp.where(sub == 2, rank1.astype(jnp.int32), jnp.where(sub == 3, rank2.astype(jnp.int32), 0))))
    rw_ref[...] = jnp.where(sub == 0, w1, jnp.where(sub == 1, w2, 0.0))


def _router(a_pairs, w_list, x, m3, gain, wr_t, br_t, tri_upper):
    in_specs, args = [], []
    for pair in a_pairs:
        specs, ops = _token_specs(pair)
        in_specs += specs
        args += ops
    x_specs, x_args = _token_specs(x)
    in_specs += ([pl.BlockSpec(w.shape, lambda i: (0, 0)) for w in w_list] + x_specs
                 + [pl.BlockSpec((None, 6, D), lambda i: (_cond_row(i), 0, 0)),
                    pl.BlockSpec((1, D), lambda i: (0, 0)),
                    pl.BlockSpec((ROUTE_ROWS, D), lambda i: (0, 0)),
                    pl.BlockSpec((ROUTE_ROWS, LANES), lambda i: (0, 0)),
                    pl.BlockSpec((TM, TM), lambda i: (0, 0))])
    return pl.pallas_call(
        functools.partial(_router_kernel, n_in=len(a_pairs)),
        grid=(N // TM,),
        in_specs=in_specs,
        out_specs=[pl.BlockSpec((TM, D), lambda i: (i, 0)),
                   pl.BlockSpec((TM, D // 2), lambda i: (i, 0)),
                   pl.BlockSpec((8, TM), lambda i: (0, i)),
                   pl.BlockSpec((8, TM), lambda i: (0, i)),
                   pl.BlockSpec((8, ROUTE_ROWS), lambda i: (0, 0))],
        out_shape=[jax.ShapeDtypeStruct((N, D), F32),
                   jax.ShapeDtypeStruct((N, D // 2), jnp.int32),
                   jax.ShapeDtypeStruct((8, N), jnp.int32),
                   jax.ShapeDtypeStruct((8, N), F32),
                   jax.ShapeDtypeStruct((8, ROUTE_ROWS), F32)],
        scratch_shapes=[pltpu.VMEM((ROUTE_ROWS, LANES), F32)],
        compiler_params=_cparams(("arbitrary",)),
        name="router",
    )(*args, *w_list, *x_args, m3, gain, wr_t, br_t, tri_upper)


def _row_copy(src_hbm, row, dst, r, sem):
    return pltpu.make_async_copy(src_hbm.at[pl.ds(row, 1)], dst.at[pl.ds(r, 1)], sem)


def _expert_kernel(be_ref, nu_ref, d_ref, pad_ref, xp_hbm, wg_ref, wu_ref, wd_ref, ys_ref, xres, xbuf, st_ref, sem):
    del be_ref
    b = pl.program_id(0)
    half = D // 2

    def gather(blk, slot, part=0, parts=1):
        base = blk * MOE_ROWS
        for r in range(part * MOE_ROWS // parts, (part + 1) * MOE_ROWS // parts):
            xbuf[slot, pl.ds(r, 1), :] = xres[pl.ds(st_ref[base + r], 1), :]

    @pl.when(b == 0)
    def _():
        cp = pltpu.make_async_copy(xp_hbm, xres, sem.at[0])
        cp.start()

        def clear_tail(e, c):
            def clear(s_, c2):
                st_ref[s_] = 0
                return c2

            return lax.fori_loop(pad_ref[2 * e], pad_ref[2 * e + 1], clear, c)

        lax.fori_loop(0, N_EXPERTS + 1, clear_tail, 0)

        def invert(j, c):
            slots = [d_ref[16 * j + t] for t in range(16)]
            for t in range(16):
                st_ref[slots[t]] = 8 * j + t // 2
            return c

        lax.fori_loop(0, 2 * N // 16, invert, 0)
        cp.wait()
        gather(0, 0)

    @pl.when(b < nu_ref[0])
    def _():
        nxt = (jnp.minimum(b + 1, N_SLOT_BLOCKS - 1), (b + 1) % 2)
        x_lo, x_hi = _unpack_bf16_pairs(xbuf[b % 2])
        dot = functools.partial(jnp.dot, preferred_element_type=F32)
        gather(*nxt, 0, 8)
        g = dot(x_lo, wg_ref[:half, :].astype(BF16))
        gather(*nxt, 1, 8)
        g = g + dot(x_hi, wg_ref[half:, :].astype(BF16))
        gather(*nxt, 2, 8)
        u = dot(x_lo, wu_ref[:half, :].astype(BF16))
        gather(*nxt, 3, 8)
        u = u + dot(x_hi, wu_ref[half:, :].astype(BF16))
        hb = (_silu(g) * u).astype(BF16)
        quarter = D // 4
        for j in range(4):
            gather(*nxt, 4 + j, 8)
            cols = slice(j * quarter, (j + 1) * quarter)
            ys_ref[:, cols] = jnp.dot(hb, wd_ref[:, cols].astype(BF16), preferred_element_type=F32)

    @pl.when(b >= nu_ref[0])
    def _():
        ys_ref[...] = jnp.zeros_like(ys_ref)


def _experts(block_e, n_used, dest, pad_ranges, xp, w_gate, w_up, w_down, layer):
    grid_spec = pltpu.PrefetchScalarGridSpec(
        num_scalar_prefetch=4,
        grid=(N_SLOT_BLOCKS,),
        in_specs=[pl.BlockSpec(memory_space=pl.ANY),
                  pl.BlockSpec((None, None, D, D_EXPERT), lambda b, be, nu, st, pd: (layer, be[b], 0, 0)),
                  pl.BlockSpec((None, None, D, D_EXPERT), lambda b, be, nu, st, pd: (layer, be[b], 0, 0)),
                  pl.BlockSpec((None, None, D_EXPERT, D), lambda b, be, nu, st, pd: (layer, be[b], 0, 0))],
        out_specs=pl.BlockSpec((MOE_ROWS, D), lambda b, be, nu, st, pd: (b, 0)),
        scratch_shapes=[pltpu.VMEM((N, D // 2), jnp.int32),
                        pltpu.VMEM((2, MOE_ROWS, D // 2), jnp.int32),
                        pltpu.SMEM((N_SLOT_BLOCKS * MOE_ROWS,), jnp.int32),
                        pltpu.SemaphoreType.DMA((1,))])
    return pl.pallas_call(
        _expert_kernel,
        grid_spec=grid_spec,
        out_shape=jax.ShapeDtypeStruct((N_SLOT_BLOCKS * MOE_ROWS, D), F32),
        compiler_params=pltpu.CompilerParams(dimension_semantics=("arbitrary",), vmem_limit_bytes=EXPERT_VMEM_LIMIT),
        name="experts",
    )(block_e, n_used, dest, pad_ranges, xp, w_gate, w_up, w_down)


def _combine_kernel(d_ref, ys_hbm, x_ref, m_ref, rw_ref, fn_ref, o_ref, buf, sem, *, final, tile0, n_tiles):
    i = pl.program_id(0)

    def issue(tile, slot):
        base = (tile0 + tile) * (2 * TM)
        for r in range(TM):
            for kk in range(2):
                _row_copy(ys_hbm, d_ref[base + 2 * r + kk], buf.at[slot, kk], r, sem.at[slot]).start()

    @pl.when(i == 0)
    def _():
        issue(0, 0)

    @pl.when(i + 1 < n_tiles)
    def _():
        issue(i + 1, (i + 1) % 2)

    slot = i % 2
    for kk in range(2):
        pltpu.make_async_copy(ys_hbm.at[pl.ds(0, TM)], buf.at[slot, kk], sem.at[slot]).wait()
    w = rw_ref[...]
    y = w[:, 0:1] * buf[slot, 0] + w[:, 1:2] * buf[slot, 1]
    out = x_ref[...] + m_ref[5:6, :] * y
    if final:
        ms = jnp.mean(out * out, axis=-1, keepdims=True)
        out = out * lax.rsqrt(ms + EPS) * fn_ref[...]
    o_ref[...] = out


def _combine(dest_flat, ys, x, m3, rw, final_norm, final, tile0=0, n_tiles=N // TM):
    grid_spec = pltpu.PrefetchScalarGridSpec(
        num_scalar_prefetch=1,
        grid=(n_tiles,),
        in_specs=[pl.BlockSpec(memory_space=pl.ANY),
                  pl.BlockSpec((TM, D), lambda i, d: (tile0 + i, 0)),
                  pl.BlockSpec((None, 6, D), lambda i, d: (_cond_row(tile0 + i), 0, 0)),
                  pl.BlockSpec((TM, 2), lambda i, d: (tile0 + i, 0)),
                  pl.BlockSpec((1, D), lambda i, d: (0, 0))],
        out_specs=pl.BlockSpec((TM, D), lambda i, d: (i, 0)),
        scratch_shapes=[pltpu.VMEM((2, 2, TM, D), F32), pltpu.SemaphoreType.DMA((2,))])
    return pl.pallas_call(
        functools.partial(_combine_kernel, final=final, tile0=tile0, n_tiles=n_tiles),
        grid_spec=grid_spec,
        out_shape=jax.ShapeDtypeStruct((n_tiles * TM, D), F32),
        compiler_params=_cparams(("arbitrary",)),
        name="combine",
    )(dest_flat, ys, x, m3, rw, final_norm)


def _combine_proj_kernel(d_ref, ys_hbm, x_ref, m_ref, rw_ref, mn_ref, gn_ref, w_ref, xo_ref, o_ref, buf, sem, *,
                         n_chunk):
    i = pl.program_id(0)
    n_tiles = N // TM
    n_parts = o_ref.shape[1] // n_chunk

    def issue(tile, slot, part=0, parts=1):
        base = tile * (2 * TM)
        for r in range(part * TM // parts, (part + 1) * TM // parts):
            for kk in range(2):
                _row_copy(ys_hbm, d_ref[base + 2 * r + kk], buf.at[slot, kk], r, sem.at[slot]).start()

    def wait(slot):
        for kk in range(2):
            pltpu.make_async_copy(ys_hbm.at[pl.ds(0, TM)], buf.at[slot, kk], sem.at[slot]).wait()

    @pl.when(i == 0)
    def _():
        issue(0, 0)

    slot = i % 2
    wait(slot)
    w = rw_ref[...]
    x_new = x_ref[...] + m_ref[5:6, :] * (w[:, 0:1] * buf[slot, 0] + w[:, 1:2] * buf[slot, 1])
    xo_ref[...] = x_new
    hb = _modulated_norm(x_new, mn_ref, gn_ref, 0, 1).astype(BF16)
    nxt = jnp.minimum(i + 1, n_tiles - 1)
    for j in range(n_parts):
        issue(nxt, 1 - slot, j, n_parts)
        sl = slice(j * n_chunk, (j + 1) * n_chunk)
        o_ref[:, sl] = jnp.dot(hb, w_ref[:, sl], preferred_element_type=F32)

    @pl.when(i == n_tiles - 1)
    def _():
        wait(1 - slot)


def _combine_proj(dest_flat, ys, x, m3, rw, m3_next, gain_next, w_bf16, n_chunk):
    nout = w_bf16.shape[1]
    grid_spec = pltpu.PrefetchScalarGridSpec(
        num_scalar_prefetch=1,
        grid=(N // TM,),
        in_specs=[pl.BlockSpec(memory_space=pl.ANY),
                  pl.BlockSpec((TM, D), lambda i, d: (i, 0)),
                  pl.BlockSpec((None, 6, D), lambda i, d: (_cond_row(i), 0, 0)),
                  pl.BlockSpec((TM, 2), lambda i, d: (i, 0)),
                  pl.BlockSpec((None, 6, D), lambda i, d: (_cond_row(i), 0, 0)),
                  pl.BlockSpec((1, D), lambda i, d: (0, 0)),
                  pl.BlockSpec((D, nout), lambda i, d: (0, 0))],
        out_specs=[pl.BlockSpec((TM, D), lambda i, d: (i, 0)),
                   pl.BlockSpec((TM, nout), lambda i, d: (i, 0))],
        scratch_shapes=[pltpu.VMEM((2, 2, TM, D), F32), pltpu.SemaphoreType.DMA((2,))])
    return pl.pallas_call(
        functools.partial(_combine_proj_kernel, n_chunk=n_chunk),
        grid_spec=grid_spec,
        out_shape=[jax.ShapeDtypeStruct((N, D), F32), jax.ShapeDtypeStruct((N, nout), F32)],
        compiler_params=_cparams(("arbitrary",)),
        name="combine_proj",
    )(dest_flat, ys, x, m3, rw, m3_next, gain_next, w_bf16)


def _hier_moe(a_pairs, w_list, x, m3, gain, layer, w_rg, b_rg, w_re, b_re, w_gate, w_up, w_down, tri_tm,
              final_norm, final, next_proj=None):
    pad = ROUTE_ROWS - N_GROUPS - N_EXPERTS
    wr_t = jnp.concatenate([w_rg.T, w_re.transpose(0, 2, 1).reshape(N_EXPERTS, D), jnp.zeros((pad, D), F32)], axis=0)
    br_t = jnp.broadcast_to(jnp.concatenate([b_rg, b_re.reshape(N_EXPERTS), jnp.zeros((pad,), F32)])[:, None],
                            (ROUTE_ROWS, LANES))
    x, xf, ri_t, rw_t, cnt = _router(a_pairs, w_list, x, m3, gain, wr_t, br_t, tri_tm)
    e_idx = ri_t[0:2].T
    rank = ri_t[2:4].T
    rw = rw_t[0:2].T
    counts = cnt[0, N_GROUPS:N_GROUPS + N_EXPERTS].astype(jnp.int32)
    padded = (counts + MOE_ROWS - 1) // MOE_ROWS * MOE_ROWS
    end_pad = jnp.cumsum(padded)
    start_pad = end_pad - padded
    experts = jnp.arange(N_EXPERTS, dtype=jnp.int32)
    start_of = jnp.sum(jnp.where(e_idx[:, :, None] == experts, start_pad, 0), axis=-1)
    dest = (start_of + rank).reshape(-1).astype(jnp.int32)
    block_start = jnp.arange(N_SLOT_BLOCKS, dtype=jnp.int32) * MOE_ROWS
    block_e = jnp.minimum(jnp.sum((end_pad[None, :] <= block_start[:, None]).astype(jnp.int32), axis=1),
                          N_EXPERTS - 1)
    n_used = (end_pad[-1:] // MOE_ROWS).astype(jnp.int32)
    tail = jnp.stack([end_pad[-1], jnp.minimum(end_pad[-1] + MOE_ROWS, N_SLOT_BLOCKS * MOE_ROWS)])
    pad_ranges = jnp.concatenate([jnp.stack([start_pad + counts, end_pad], axis=1).reshape(-1), tail]).astype(jnp.int32)
    ys = _experts(block_e, n_used, dest, pad_ranges, xf, w_gate, w_up, w_down, layer)
    if not final:
        return _combine_proj(dest, ys, x, m3, rw, *next_proj)
    nct = NC // TM
    return (_combine(dest, ys, x, m3, rw, final_norm, True, 0, nct),
            _combine(dest, ys, x, m3, rw, final_norm, True, nct, N // TM - nct))


def kernel(x_prompt, x_sample, state_A_fwd, state_A_bwd, cache_k, cache_v, c, c_ctx, mod_w, mod_b, norm_mix, norm_ffn, ab_w_in, ab_conv, ab_a_log, ab_dt_bias, ab_o_gain, ab_w_out, na_w_qkv, na_rpb, na_w_out, moe_w_rg, moe_b_rg, moe_w_re, moe_b_re, moe_w_gate, moe_w_up, moe_w_down, final_norm):
    x = (x_prompt.reshape(NC, D), x_sample.reshape(NL, D))
    cond8 = jnp.concatenate([c_ctx[None, :], c, jnp.zeros((8 - 1 - DEC_BATCH, D), F32)], axis=0)
    mods = _ada_params(cond8, mod_w, mod_b).reshape(DEPTH, 8, 6, D)

    tri_tm = jnp.asarray(np.triu(np.ones((TM, TM)), 1), BF16)
    tri_c = jnp.asarray(np.tril(np.ones((CHUNK, CHUNK))), F32)
    half = np.arange(LANES) < A_DH
    jbd = jnp.asarray((half[:, None] == half[None, :]).astype(np.float32))
    fn = final_norm[None, :]

    m3 = mods[0]
    w_in = ab_w_in[0]
    w_in = jnp.concatenate([w_in[:, :2048], w_in[:, 2080:2592], w_in[:, 2048:2080],
                            jnp.zeros((D, AB_COLS - 2592), F32)], axis=1).astype(BF16)
    proj = _modproj(x, m3, norm_mix[0][None, :], w_in, 896)
    gate_p = jnp.broadcast_to(jnp.stack([ab_a_log[0].reshape(-1), ab_dt_bias[0].reshape(-1)])[:, :, None],
                              (2, 2 * A_HEADS, LANES))
    o_gain2 = jnp.tile(ab_o_gain[0], 2)[None, :]
    zeros_state = jnp.zeros((BATCH, A_HEADS, A_DH, A_DH), F32)
    mix_a_c, s_f, s_b = _deltanet(proj, ab_conv[0], gate_p, o_gain2, zeros_state, zeros_state, tri_c, jbd,
                                  seq_len=SEQ, n_seq=BATCH, n_sub=4, row_blk0=0)
    mix_a_l, _, _ = _deltanet(proj, ab_conv[0], gate_p, o_gain2, state_A_fwd[:, 0], state_A_bwd[:, 0], tri_c, jbd,
                              seq_len=DEC_SEQ, n_seq=DEC_BATCH, n_sub=1, row_blk0=NC // DEC_SEQ)

    cc, sc = _dft_mats(LANES)
    ct, st = _dft_mats(SEQ)
    c64, s64 = _dft_mats(64)
    cs = jnp.asarray(np.concatenate([cc, sc], axis=1), BF16)
    dft = jnp.asarray(np.concatenate([ct, -st], axis=1), BF16)
    ca = jnp.asarray(np.concatenate([cc, -sc, -sc, -cc], axis=1), BF16)
    m1 = jnp.asarray(np.concatenate([c64, s64], axis=1), BF16)
    tw_idx = np.arange(64)
    tw_ang = 2.0 * np.pi * (tw_idx[:, None] * tw_idx[None, :]) / DEC_SEQ
    twc = jnp.broadcast_to(jnp.asarray(np.cos(tw_ang), F32)[:, :, None], (64, 64, LANES))
    tws = jnp.broadcast_to(jnp.asarray(np.sin(tw_ang), F32)[:, :, None], (64, 64, LANES))
    mix_b_c = _fnet_ctx(proj, cs, dft)
    mix_b_l = _fnet_latent(proj, ca, m1, twc, tws)

    w_out = ab_w_out[0].astype(BF16)
    x, qkv = _hier_moe([(mix_a_c, mix_a_l), (mix_b_c, mix_b_l)], [w_out[:A_WIDTH], w_out[A_WIDTH:]], x, m3,
                       norm_ffn[0][None, :], 0, moe_w_rg[0], moe_b_rg[0], moe_w_re[0], moe_b_re[0],
                       moe_w_gate, moe_w_up, moe_w_down, tri_tm, fn, False,
                       next_proj=(mods[1], norm_mix[1][None, :], na_w_qkv[0].astype(BF16), 512))

    m3 = mods[1]
    attn_c = _na_ctx(qkv)
    attn_l = _na_latent(qkv, cache_k[:, 0].reshape(DEC_BATCH, 256, D), cache_v[:, 0].reshape(DEC_BATCH, 256, D),
                        _rpb_tables(na_rpb[0]))
    y_c, y_l = _hier_moe([(attn_c, attn_l)], [na_w_out[0].astype(BF16)], x, m3, norm_ffn[1][None, :], 1, moe_w_rg[1], moe_b_rg[1], moe_w_re[1], moe_b_re[1],
                         moe_w_gate, moe_w_up, moe_w_down, tri_tm, fn, True)

    new_k = qkv[:NC, D:2 * D].reshape(BATCH, 1, SEQ, NA_HEADS, NA_DH)
    new_v = qkv[:NC, 2 * D:].reshape(BATCH, 1, SEQ, NA_HEADS, NA_DH)
    return (y_c.reshape(BATCH, SEQ, D), y_l.reshape(DEC_BATCH, DEC_SEQ, D),
            s_f[:, None], s_b[:, None], new_k, new_v)
```

```python
import functools
import math

import numpy as np
import jax
import jax.numpy as jnp
from jax import lax
from jax.experimental import pallas as pl
from jax.experimental.pallas import tpu as pltpu

F32 = jnp.float32
BF16 = jnp.bfloat16
HIGHEST = lax.Precision.HIGHEST

D = 1024
BATCH, SEQ = 32, 256
DEC_BATCH, DEC_SEQ = 2, 4096
NC = BATCH * SEQ
NL = DEC_BATCH * DEC_SEQ
N = NC + NL
DEPTH = 2
GRID_W = 64
A_DH = 64
A_HEADS = 8
A_WIDTH = 512
CHUNK = 64
B_WIDTH = 512
B_GROUPS = 4
NA_DH = 64
NA_HEADS = 16
WIN_R, WIN_C = 8, 16
N_GROUPS, EXP_PER_GROUP, N_EXPERTS = 4, 8, 32
D_EXPERT = 512
EPS = 1e-6

LANES = 128
TM = 256
MOE_ROWS = 256
N_SLOT_BLOCKS = (2 * N) // MOE_ROWS + N_EXPERTS
AB_COLS = 2688
VMEM_LIMIT = 56 * 1024 * 1024
EXPERT_VMEM_LIMIT = 60 * 1024 * 1024
NEG = -1e30


def _cparams(sem):
    return pltpu.CompilerParams(dimension_semantics=sem, vmem_limit_bytes=VMEM_LIMIT)


def _mm(a, b):
    return jnp.dot(a.astype(BF16), b.astype(BF16), preferred_element_type=F32)


def _mm_nt(a, b):
    return lax.dot_general(a.astype(BF16), b.astype(BF16), (((1,), (1,)), ((), ())),
                           preferred_element_type=F32)


SOLVE_BLK = 16
NA_SCALE = NA_DH ** -0.5
NA_CTX_PAIRS = 8
NA_ROW_UNROLL = 16
TERM_UNROLL = 8


def _unit_lower_solve_many(a_mats, rhs, same_blk, eye):
    off = [jnp.where(same_blk, 0.0, a).astype(BF16) for a in a_mats]
    p = [jnp.where(same_blk, -a, 0.0).astype(BF16) for a in a_mats]
    dinv = [(eye + x.astype(F32)).astype(BF16) for x in p]
    for _ in range(int(math.log2(SOLVE_BLK)) - 1):
        p = [_mm(x, x).astype(BF16) for x in p]
        dinv = [(di.astype(F32) + _mm(x, di)).astype(BF16) for x, di in zip(p, dinv)]
    mp = [(-_mm(di, o)).astype(BF16) for di, o in zip(dinv, off)]
    y = [_mm(di, r) for di, r in zip(dinv, rhs)]
    y = [yi + _mm(m, yi) for m, yi in zip(mp, y)]
    for _ in range(int(math.log2(a_mats[0].shape[0] // SOLVE_BLK)) - 1):
        mp = [_mm(m, m).astype(BF16) for m in mp]
        y = [yi + _mm(m, yi) for m, yi in zip(mp, y)]
    return y


def _mm_split(a, b, parts, split_rhs=False):
    x = b if split_rhs else a
    acc = None
    for _ in range(parts):
        piece = x.astype(BF16)
        x = x - piece.astype(F32)
        term = (jnp.dot(a.astype(BF16), piece, preferred_element_type=F32) if split_rhs
                else jnp.dot(piece, b.astype(BF16), preferred_element_type=F32))
        acc = term if acc is None else acc + term
    return acc


def _mm_hi(a, b):
    return jnp.dot(a, b, preferred_element_type=F32, precision=HIGHEST)


def _silu(x):
    return x * jax.nn.sigmoid(x)


def _bf16_bits(x):
    b = lax.bitcast_convert_type(x, jnp.int32)
    return b + 0x7FFF + (lax.shift_right_logical(b, jnp.int32(16)) & 1)


_HIGH16 = -65536


def _pack_bf16_pairs(a, b):
    return lax.shift_right_logical(_bf16_bits(a), jnp.int32(16)) | (_bf16_bits(b) & _HIGH16)


def _unpack_bf16_pairs(p):
    a = lax.bitcast_convert_type(lax.shift_left(p, jnp.int32(16)), F32)
    b = lax.bitcast_convert_type(p & _HIGH16, F32)
    return a.astype(BF16), b.astype(BF16)


def _cond_row(i):
    return jnp.where(i < NC // TM, 0, 1 + (i - NC // TM) // (DEC_SEQ // TM))


def _modulated_norm(x, m_ref, g_ref, shift_idx, scale_idx):
    ms = jnp.mean(x * x, axis=-1, keepdims=True)
    y = x * lax.rsqrt(ms + EPS) * g_ref[...]
    return y * (1.0 + m_ref[scale_idx:scale_idx + 1, :]) + m_ref[shift_idx:shift_idx + 1, :]


def _ada_kernel(cond_ref, w_ref, b_ref, o_ref):
    o_ref[...] = _mm_hi(_silu(cond_ref[...]), w_ref[...]) + b_ref[...]


def _ada_params(cond8, mod_w, mod_b):
    tn = 1536
    return pl.pallas_call(
        _ada_kernel,
        grid=(DEPTH, 6 * D // tn),
        in_specs=[pl.BlockSpec((8, D), lambda l, j: (0, 0)),
                  pl.BlockSpec((None, D, tn), lambda l, j: (l, 0, j)),
                  pl.BlockSpec((None, 1, tn), lambda l, j: (l, 0, j))],
        out_specs=pl.BlockSpec((None, 8, tn), lambda l, j: (l, 0, j)),
        out_shape=jax.ShapeDtypeStruct((DEPTH, 8, 6 * D), F32),
        compiler_params=_cparams(("arbitrary", "arbitrary")),
        name="ada_params",
    )(cond8, mod_w, mod_b.reshape(DEPTH, 1, 6 * D))


def _token_specs(x):
    nct = NC // TM
    if isinstance(x, tuple):
        return ([pl.BlockSpec((TM, x[0].shape[1]), lambda i: (jnp.minimum(i, nct - 1), 0)),
                 pl.BlockSpec((TM, x[1].shape[1]), lambda i: (jnp.maximum(i - nct, 0), 0))], list(x))
    return [pl.BlockSpec((TM, x.shape[1]), lambda i: (i, 0))], [x]


def _token_rows(refs):
    if len(refs) == 1:
        return refs[0][...]
    return jnp.where(pl.program_id(0) < NC // TM, refs[0][...], refs[1][...])


def _modproj_kernel(*refs, n_chunk):
    m_ref, g_ref, w_ref, o_ref = refs[-4:]
    hb = _modulated_norm(_token_rows(refs[:-4]), m_ref, g_ref, 0, 1).astype(BF16)
    for j in range(o_ref.shape[1] // n_chunk):
        sl = slice(j * n_chunk, (j + 1) * n_chunk)
        o_ref[:, sl] = jnp.dot(hb, w_ref[:, sl], preferred_element_type=F32)


def _modproj(x, m3, gain, w_bf16, n_chunk):
    nout = w_bf16.shape[1]
    x_specs, x_args = _token_specs(x)
    return pl.pallas_call(
        functools.partial(_modproj_kernel, n_chunk=n_chunk),
        grid=(N // TM,),
        in_specs=x_specs + [pl.BlockSpec((None, 6, D), lambda i: (_cond_row(i), 0, 0)),
                            pl.BlockSpec((1, D), lambda i: (0, 0)),
                            pl.BlockSpec((D, nout), lambda i: (0, 0))],
        out_specs=pl.BlockSpec((TM, nout), lambda i: (i, 0)),
        out_shape=jax.ShapeDtypeStruct((N, nout), F32),
        compiler_params=_cparams(("arbitrary",)),
        name="modproj",
    )(*x_args, m3, gain, w_bf16)


def _deltanet_kernel(q_ref, k_ref, v_ref, z_ref, ab_ref, cq_ref, ck_ref, cv_ref, gp_ref, og_ref,
                     s0f_ref, s0b_ref, tri_ref, jbd_ref, o_ref, sf_ref, sb_ref,
                     u_s, w_s, qd_s, at_s, kt_s, ge_s, st_s, ob, *, seq_len, n_sub):
    hp = pl.program_id(1)
    C = CHUNK
    nc = seq_len // C
    nct = n_sub * nc
    P = LANES
    lane = lax.broadcasted_iota(jnp.int32, (C, P), 1)
    row = lax.broadcasted_iota(jnp.int32, (C, P), 0)
    first_head = lane < A_DH
    ri = lax.broadcasted_iota(jnp.int32, (P, P), 0)
    ci = lax.broadcasted_iota(jnp.int32, (P, P), 1)
    same_head = (ri < C) == (ci < C)
    same_blk = (ri // SOLVE_BLK) == (ci // SOLVE_BLK)
    eye = jnp.where(ri == ci, 1.0, 0.0)
    jbd = jbd_ref[...]
    lincl = tri_ref[...]
    cum_b = [lincl.T.astype(BF16), lincl.astype(BF16)]
    incl_m = [jnp.logical_and(same_head, ri >= ci), jnp.logical_and(same_head, ri <= ci)]
    strict_m = [jnp.logical_and(same_head, ri > ci), jnp.logical_and(same_head, ri < ci)]
    neg_a = -jnp.exp(gp_ref[0])
    dt_b = gp_ref[1]

    def conv_silu(ref, w_ref, c):
        base = pl.multiple_of(c * C, C)
        cs = c % nc
        xc = ref[pl.ds(base, C), :]
        pbase = pl.multiple_of(jnp.maximum(base - 8, 0), 8)
        nbase = pl.multiple_of(jnp.minimum(base + C, n_sub * seq_len - 8), 8)
        prev_row = ref[pl.ds(pbase, 8), :][7:8, :] * jnp.where(cs > 0, 1.0, 0.0)
        next_row = ref[pl.ds(nbase, 8), :][0:1, :] * jnp.where(cs < nc - 1, 1.0, 0.0)
        x_prev = jnp.where(row == 0, prev_row, pltpu.roll(xc, 1, 0))
        x_next = jnp.where(row == C - 1, next_row, pltpu.roll(xc, C - 1, 0))
        y = w_ref[0:1, :] * x_prev + w_ref[1:2, :] * xc + w_ref[2:3, :] * x_next
        return _silu(y)

    def stack(x):
        return jnp.concatenate([jnp.where(first_head, x, 0.0), jnp.where(first_head, 0.0, x)], axis=0)

    def chunk_inputs(c):
        base = pl.multiple_of(c * C, C)
        q = conv_silu(q_ref, cq_ref, c)
        k = conv_silu(k_ref, ck_ref, c)
        v = conv_silu(v_ref, cv_ref, c)
        return q, k, v, ab_ref[pl.ds(base, C), :].T

    sub8 = lax.broadcasted_iota(jnp.int32, (A_HEADS, C), 0)

    def pair_row(x8):
        r0 = jnp.sum(jnp.where(sub8 == 2 * hp, x8, 0.0), axis=0, keepdims=True)
        r1 = jnp.sum(jnp.where(sub8 == 2 * hp + 1, x8, 0.0), axis=0, keepdims=True)
        return jnp.concatenate([r0, r1], axis=1)

    def chain_gates(ab_t, d):
        a8 = ab_t[d * A_HEADS:(d + 1) * A_HEADS, :]
        b8 = ab_t[(2 + d) * A_HEADS:(3 + d) * A_HEADS, :]
        g8 = neg_a[d * A_HEADS:(d + 1) * A_HEADS, :C] * jax.nn.softplus(a8 + dt_b[d * A_HEADS:(d + 1) * A_HEADS, :C])
        gc8 = _mm_split(g8, cum_b[d], 3)
        tot8 = jnp.broadcast_to(jnp.sum(g8, axis=-1, keepdims=True), (A_HEADS, C))
        gc_row, beta_row, tot_row = pair_row(gc8), pair_row(jax.nn.sigmoid(b8)), pair_row(tot8)
        cols = jnp.concatenate([gc_row, beta_row, tot_row, jnp.zeros((5, P), F32)], axis=0).T
        return gc_row, tot_row, cols[:, 0:1], cols[:, 1:2], cols[:, 2:3]

    def terms_body(j, carry):
        cs = [TERM_UNROLL * j + t for t in range(TERM_UNROLL)]
        ins = [chunk_inputs(c) for c in cs]
        qsq = [_mm_split(x[0] * x[0], jbd, 2) for x in ins]
        ksq = [_mm_split(x[1] * x[1], jbd, 2) for x in ins]
        qs = [x[0] * lax.rsqrt(s + EPS) * (A_DH ** -0.5) for x, s in zip(ins, qsq)]
        ks = [x[1] * lax.rsqrt(s + EPS) for x, s in zip(ins, ksq)]
        qst = [stack(x) for x in qs]
        kst = [stack(x) for x in ks]
        vst = [stack(x[2]) for x in ins]
        kst_t = [x.T for x in kst]
        kk = [_mm_nt(x, x) for x in kst]
        qk = [_mm_nt(x, y) for x, y in zip(qst, kst)]
        chains = [(t, d) for t in range(TERM_UNROLL) for d in range(2)]
        gates = [chain_gates(ins[t][3], d) for t, d in chains]
        decay, e_gc = [], []
        for (t, d), (gc_row, tot_row, gc_col, beta_col, tot_col) in zip(chains, gates):
            diff = jnp.broadcast_to(gc_col, (P, P)) - jnp.broadcast_to(gc_row, (P, P))
            decay.append(jnp.where(incl_m[d], jnp.exp(jnp.where(incl_m[d], diff, 0.0)), 0.0))
            e_gc.append(jnp.exp(gc_col))
        a_mats = [jnp.where(strict_m[d], g[3] * kk[t] * dc, 0.0) for (t, d), g, dc in zip(chains, gates, decay)]
        rhs = [vst[t] * g[3] + pltpu.roll(kst[t] * (g[3] * e), A_DH, 1)
               for (t, d), g, e in zip(chains, gates, e_gc)]
        xs = _unit_lower_solve_many(a_mats, rhs, same_blk, eye)
        for (t, d), x, g, e, dc in zip(chains, xs, gates, e_gc, decay):
            c = cs[t]
            gc_row, tot_row = g[0], g[1]
            u_s[d, c] = jnp.where(same_head, x, 0.0).astype(BF16)
            w_s[d, c] = pltpu.roll(jnp.where(same_head, 0.0, x), A_DH, 1).astype(BF16)
            qd_s[d, c] = (qst[t] * e).astype(BF16)
            at_s[d, c] = jnp.where(incl_m[d], qk[t] * dc, 0.0).astype(BF16)
            kt_s[d, c] = (kst_t[t] * jnp.exp(tot_row - gc_row)).astype(BF16)
            ge_s[d, c] = jnp.broadcast_to(jnp.exp(tot_row), (8, P))
        return carry

    lax.fori_loop(0, nct // TERM_UNROLL, terms_body, 0)

    def block_diag(s2):
        z = jnp.zeros((A_DH, A_DH), F32)
        return jnp.concatenate([jnp.concatenate([s2[0], z], axis=1),
                                jnp.concatenate([z, s2[1]], axis=1)], axis=0)

    for s in range(n_sub):
        st_s[2 * s] = block_diag(s0f_ref[s])
        st_s[2 * s + 1] = block_diag(s0b_ref[s])

    def scan_body(i, carry):
        chains = [(s, d, s * nc + (i if d == 0 else nc - 1 - i)) for s in range(n_sub) for d in range(2)]
        dot = functools.partial(jnp.dot, preferred_element_type=F32)
        s_bd = [st_s[2 * s + d] for s, d, c in chains]
        sb16 = [x.astype(BF16) for x in s_bd]
        ws = [dot(w_s[d, c], sb) for (s, d, c), sb in zip(chains, sb16)]
        qs_ = [dot(qd_s[d, c], sb) for (s, d, c), sb in zip(chains, sb16)]
        vb = [(u_s[d, c].astype(F32) - x).astype(BF16) for (s, d, c), x in zip(chains, ws)]
        av = [dot(at_s[d, c], x) for (s, d, c), x in zip(chains, vb)]
        kv = [dot(kt_s[d, c], x) for (s, d, c), x in zip(chains, vb)]
        for (s, d, c), sb, q_, a_, k_ in zip(chains, s_bd, qs_, av, kv):
            st_s[2 * s + d] = sb * ge_s[d, c][0:1, :] + k_
            o_st = q_ + a_
            dst = o_ref if d == 0 else ob
            dst[pl.ds(pl.multiple_of(c * C, C), C), :] = o_st[:C] + o_st[C:]
        return carry

    lax.fori_loop(0, nc, scan_body, 0)

    for s in range(n_sub):
        for d, ref in ((0, sf_ref), (1, sb_ref)):
            s_bd = st_s[2 * s + d]
            ref[s, 0] = s_bd[:A_DH, :A_DH]
            ref[s, 1] = s_bd[A_DH:, A_DH:]

    def finish(j, carry):
        bases = [pl.multiple_of((TERM_UNROLL * j + t) * C, C) for t in range(TERM_UNROLL)]
        o = [o_ref[pl.ds(b, C), :] + ob[pl.ds(b, C), :] for b in bases]
        ms = [_mm_split(x * x, jbd, 2) * (1.0 / A_DH) for x in o]
        for b, x, m in zip(bases, o, ms):
            o_ref[pl.ds(b, C), :] = x * lax.rsqrt(m + EPS) * og_ref[...] * _silu(z_ref[pl.ds(b, C), :])
        return carry

    lax.fori_loop(0, nct // TERM_UNROLL, finish, 0)


def _deltanet(proj, conv_w, gate_p, o_gain2, s0f, s0b, tri, jbd, *, seq_len, n_seq, n_sub, row_blk0):
    rows = n_sub * seq_len
    nct = rows // CHUNK
    rb = lambda b: row_blk0 + b
    col = lambda off: (lambda b, hp: (rb(b), off + hp))
    st_spec = pl.BlockSpec((n_sub, 2, A_DH, A_DH), lambda b, hp: (b, hp, 0, 0))
    in_specs = [pl.BlockSpec((rows, LANES), col(0)),
                pl.BlockSpec((rows, LANES), col(4)),
                pl.BlockSpec((rows, LANES), col(8)),
                pl.BlockSpec((rows, LANES), col(12)),
                pl.BlockSpec((rows, LANES), lambda b, hp: (rb(b), 20)),
                pl.BlockSpec((3, LANES), lambda b, hp: (0, hp)),
                pl.BlockSpec((3, LANES), lambda b, hp: (0, 4 + hp)),
                pl.BlockSpec((3, LANES), lambda b, hp: (0, 8 + hp)),
                pl.BlockSpec((2, 2 * A_HEADS, LANES), lambda b, hp: (0, 0, 0)),
                pl.BlockSpec((1, LANES), lambda b, hp: (0, 0)),
                st_spec, st_spec,
                pl.BlockSpec((CHUNK, CHUNK), lambda b, hp: (0, 0)),
                pl.BlockSpec((LANES, LANES), lambda b, hp: (0, 0))]
    args = [proj, proj, proj, proj, proj, conv_w, conv_w, conv_w, gate_p, o_gain2, s0f, s0b, tri, jbd]
    st_shape = jax.ShapeDtypeStruct((n_seq, A_HEADS, A_DH, A_DH), F32)
    tile = lambda dt: pltpu.VMEM((2, nct, LANES, LANES), dt)
    return pl.pallas_call(
        functools.partial(_deltanet_kernel, seq_len=seq_len, n_sub=n_sub),
        grid=(n_seq // n_sub, A_HEADS // 2),
        in_specs=in_specs,
        out_specs=[pl.BlockSpec((rows, LANES), lambda b, hp: (b, hp)), st_spec, st_spec],
        out_shape=[jax.ShapeDtypeStruct((n_seq * seq_len, A_WIDTH), F32), st_shape, st_shape],
        scratch_shapes=[tile(BF16), tile(BF16), tile(BF16), tile(BF16), tile(BF16),
                        pltpu.VMEM((2, nct, 8, LANES), F32),
                        pltpu.VMEM((2 * n_sub, LANES, LANES), F32),
                        pltpu.VMEM((rows, LANES), F32)],
        compiler_params=_cparams(("arbitrary", "arbitrary")),
        name="deltanet",
    )(*args)


def _dft_mats(n):
    idx = np.arange(n)
    ang = 2.0 * np.pi * ((idx[:, None] * idx[None, :]) % n) / n
    return np.cos(ang), np.sin(ang)


def _fnet_ctx_kernel(u_ref, cs_ref, dft_ref, o_ref):
    norm = 1.0 / math.sqrt(SEQ * LANES)
    sls = [slice(g * LANES, (g + 1) * LANES) for g in range(B_GROUPS)]
    p = [_mm(u_ref[:, sl], cs_ref[...]) for sl in sls]
    stack = [jnp.concatenate([x[:, :LANES], x[:, LANES:]], axis=0) for x in p]
    y = [_mm(dft_ref[...], x) for x in stack]
    for sl, x in zip(sls, y):
        o_ref[:, sl] = x * norm


def _fnet_ctx(proj, cs, dft):
    return pl.pallas_call(
        _fnet_ctx_kernel,
        grid=(BATCH,),
        in_specs=[pl.BlockSpec((SEQ, B_WIDTH), lambda b: (b, 4)),
                  pl.BlockSpec(cs.shape, lambda b: (0, 0)),
                  pl.BlockSpec(dft.shape, lambda b: (0, 0))],
        out_specs=pl.BlockSpec((SEQ, B_WIDTH), lambda b: (b, 0)),
        out_shape=jax.ShapeDtypeStruct((NC, B_WIDTH), F32),
        compiler_params=_cparams(("arbitrary",)),
        name="fnet_ctx",
    )(proj, cs, dft)


FN_SUB = 4


def _fnet_lat1_kernel(u_ref, ca_ref, m1_ref, twc_ref, tws_ref, o_ref):
    r = 64
    units = [(j, g) for j in range(FN_SUB) for g in range(B_GROUPS)]
    rows = lambda j: slice(j * r, (j + 1) * r)
    pa = [_mm(u_ref[rows(j), g * LANES:(g + 1) * LANES], ca_ref[...]) for j, g in units]
    rhs = [jnp.concatenate([x[:, :2 * LANES], x[:, 2 * LANES:]], axis=0) for x in pa]
    zz = [_mm(m1_ref[...], x) for x in rhs]
    for (j, g), z in zip(units, zz):
        c = twc_ref[j]
        s = tws_ref[j]
        zr, zi = z[:, :LANES], z[:, LANES:]
        o_ref[rows(j), 2 * g * LANES:(2 * g + 1) * LANES] = (zr * c + zi * s).astype(BF16)
        o_ref[rows(j), (2 * g + 1) * LANES:(2 * g + 2) * LANES] = (zi * c - zr * s).astype(BF16)


def _fnet_lat2_kernel(z_ref, m1_ref, o_ref):
    r = 64
    norm = 1.0 / math.sqrt(DEC_SEQ * LANES)
    units = [(j, g) for j in range(FN_SUB) for g in range(B_GROUPS)]
    rows = lambda j: slice(j * r, (j + 1) * r)
    rhs = [jnp.concatenate([z_ref[rows(j), 2 * g * LANES:(2 * g + 1) * LANES],
                            z_ref[rows(j), (2 * g + 1) * LANES:(2 * g + 2) * LANES]], axis=0) for j, g in units]
    y = [_mm(m1_ref[...], x) for x in rhs]
    for (j, g), x in zip(units, y):
        o_ref[rows(j), g * LANES:(g + 1) * LANES] = x * norm


def _fnet_latent(proj, ca, m1, twc, tws):
    r = 64
    u = proj[NC:, 2048:2560].astype(BF16).reshape(DEC_BATCH, r, r, B_WIDTH)
    u = u.transpose(0, 2, 1, 3).reshape(DEC_BATCH * r * r, B_WIDTH)
    steps = DEC_BATCH * r // FN_SUB
    blk = FN_SUB * r
    z = pl.pallas_call(
        _fnet_lat1_kernel,
        grid=(steps,),
        in_specs=[pl.BlockSpec((blk, B_WIDTH), lambda s: (s, 0)),
                  pl.BlockSpec(ca.shape, lambda s: (0, 0)),
                  pl.BlockSpec(m1.shape, lambda s: (0, 0)),
                  pl.BlockSpec((FN_SUB, r, LANES), lambda s: (s % (r // FN_SUB), 0, 0)),
                  pl.BlockSpec((FN_SUB, r, LANES), lambda s: (s % (r // FN_SUB), 0, 0))],
        out_specs=pl.BlockSpec((blk, 2 * B_WIDTH), lambda s: (s, 0)),
        out_shape=jax.ShapeDtypeStruct((NL, 2 * B_WIDTH), BF16),
        compiler_params=_cparams(("arbitrary",)),
        name="fnet_lat1",
    )(u, ca, m1, twc, tws)
    z = z.reshape(DEC_BATCH, r, r, 2 * B_WIDTH).transpose(0, 2, 1, 3).reshape(NL, 2 * B_WIDTH)
    y = pl.pallas_call(
        _fnet_lat2_kernel,
        grid=(steps,),
        in_specs=[pl.BlockSpec((blk, 2 * B_WIDTH), lambda s: (s, 0)),
                  pl.BlockSpec(m1.shape, lambda s: (0, 0))],
        out_specs=pl.BlockSpec((blk, B_WIDTH), lambda s: (s, 0)),
        out_shape=jax.ShapeDtypeStruct((NL, B_WIDTH), F32),
        compiler_params=_cparams(("arbitrary",)),
        name="fnet_lat2",
    )(z, m1)
    return y.reshape(DEC_BATCH, r, r, B_WIDTH).transpose(0, 2, 1, 3).reshape(NL, B_WIDTH)


def _head_masks():
    lane = lax.broadcasted_iota(jnp.int32, (1, LANES), 1)
    return lane < NA_DH


def _attend_many(chains):
    s = [[_mm_nt(q, k) if b is None else _mm_nt(q, k) + b for k, v, b in kv] for q, kv in chains]
    m = [functools.reduce(jnp.maximum, [jnp.max(x, axis=-1, keepdims=True) for x in xs]) for xs in s]
    p = [[jnp.exp(x - mi) for x in xs] for xs, mi in zip(s, m)]
    l = [sum(jnp.sum(x, axis=-1, keepdims=True) for x in xs) for xs in p]
    o = [sum(_mm(x, v) for x, (k, v, b) in zip(xs, kv)) for xs, (q, kv) in zip(p, chains)]
    return [oi / li for oi, li in zip(o, l)]


def _na_ctx_kernel(q_ref, k_ref, v_ref, o_ref):
    first = _head_masks()
    n_split = 2
    rows = SEQ // n_split
    chains = []
    for p in range(NA_CTX_PAIRS):
        lanes = slice(p * LANES, (p + 1) * LANES)
        k = k_ref[:, lanes].astype(BF16)
        v = v_ref[:, lanes].astype(BF16)
        for j in range(n_split):
            q = q_ref[j * rows:(j + 1) * rows, lanes]
            for a in range(2):
                chains.append((jnp.where(first if a == 0 else jnp.logical_not(first), q * NA_SCALE, 0.0),
                               [(k, v, None)]))
    outs = _attend_many(chains)
    for p in range(NA_CTX_PAIRS):
        for j in range(n_split):
            i = 2 * (p * n_split + j)
            o_ref[j * rows:(j + 1) * rows, p * LANES:(p + 1) * LANES] = jnp.where(first, outs[i], outs[i + 1])


def _na_ctx(qkv):
    w = NA_CTX_PAIRS * LANES
    nblk = D // w
    return pl.pallas_call(
        _na_ctx_kernel,
        grid=(BATCH, nblk),
        in_specs=[pl.BlockSpec((SEQ, w), lambda b, hp: (b, hp)),
                  pl.BlockSpec((SEQ, w), lambda b, hp: (b, nblk + hp)),
                  pl.BlockSpec((SEQ, w), lambda b, hp: (b, 2 * nblk + hp))],
        out_specs=pl.BlockSpec((SEQ, w), lambda b, hp: (b, hp)),
        out_shape=jax.ShapeDtypeStruct((NC, D), F32),
        compiler_params=_cparams(("arbitrary", "arbitrary")),
        name="na_ctx",
    )(qkv, qkv, qkv)


def _na_lat_kernel(q_ref, k_ref, v_ref, kc_ref, vc_ref, tt_ref, o_ref):
    first = _head_masks()
    rows = DEC_SEQ // GRID_W
    kctx = kc_ref[...].astype(BF16)
    vctx = vc_ref[...].astype(BF16)
    nkeys = WIN_R * GRID_W

    def rows_body(j, carry):
        chains = []
        for t in range(NA_ROW_UNROLL):
            r = NA_ROW_UNROLL * j + t
            r0 = jnp.clip(r - WIN_R // 2, 0, rows - WIN_R)
            dr0 = r0 - r + (WIN_R - 1)
            q = q_ref[pl.ds(pl.multiple_of(r * GRID_W, GRID_W), GRID_W), :] * NA_SCALE
            kbase = pl.multiple_of(r0 * GRID_W, GRID_W)
            kl = k_ref[pl.ds(kbase, nkeys), :].astype(BF16)
            vl = v_ref[pl.ds(kbase, nkeys), :].astype(BF16)
            for a in range(2):
                qm = jnp.where(first if a == 0 else jnp.logical_not(first), q, 0.0)
                bias = jnp.concatenate([tt_ref[a, dr0 + 2 * i] for i in range(WIN_R // 2)], axis=1)
                chains.append((qm, [(kl, vl, bias), (kctx, vctx, None)]))
        outs = _attend_many(chains)
        for t in range(NA_ROW_UNROLL):
            r = NA_ROW_UNROLL * j + t
            o_ref[pl.ds(pl.multiple_of(r * GRID_W, GRID_W), GRID_W), :] = jnp.where(first, outs[2 * t], outs[2 * t + 1])
        return carry

    lax.fori_loop(0, rows // NA_ROW_UNROLL, rows_body, 0)


def _na_latent(qkv, cache_k2, cache_v2, tt2):
    rb0 = NC // DEC_SEQ
    return pl.pallas_call(
        _na_lat_kernel,
        grid=(DEC_BATCH, NA_HEADS // 2),
        in_specs=[pl.BlockSpec((DEC_SEQ, LANES), lambda b, hp: (rb0 + b, hp)),
                  pl.BlockSpec((DEC_SEQ, LANES), lambda b, hp: (rb0 + b, 8 + hp)),
                  pl.BlockSpec((DEC_SEQ, LANES), lambda b, hp: (rb0 + b, 16 + hp)),
                  pl.BlockSpec((None, 256, LANES), lambda b, hp: (b, 0, hp)),
                  pl.BlockSpec((None, 256, LANES), lambda b, hp: (b, 0, hp)),
                  pl.BlockSpec((2, 2 * WIN_R - 2, GRID_W, LANES), lambda b, hp: (hp, 0, 0, 0))],
        out_specs=pl.BlockSpec((DEC_SEQ, LANES), lambda b, hp: (b, hp)),
        out_shape=jax.ShapeDtypeStruct((NL, D), F32),
        compiler_params=_cparams(("arbitrary", "arbitrary")),
        name="na_latent",
    )(qkv, qkv, qkv, cache_k2, cache_v2, tt2)


def _rpb_tables(rpb):
    col = np.arange(GRID_W)
    start = np.clip(col - WIN_C // 2, 0, GRID_W - WIN_C)
    inside = (col[None, :] >= start[:, None]) & (col[None, :] < start[:, None] + WIN_C)
    w = GRID_W
    period = 2 * w - 1
    x = jnp.pad(rpb, ((0, 0), (0, 0), (w - WIN_C, w - WIN_C)))
    flat = jnp.tile(x, (1, 1, w))[:, :, w - 1:w - 1 + w * (period - 1)]
    t = flat.reshape(NA_HEADS, 2 * WIN_R - 1, w, period - 1)[..., :w]
    t = jnp.where(inside[None, None], t, NEG)
    return jnp.concatenate([t[:, :-1], t[:, 1:]], axis=-1)


ROUTE_SPLIT = 2
ROUTE_ROWS = 40


def _router_kernel(*refs, n_in):
    a_refs = refs[:2 * n_in]
    w_refs = refs[2 * n_in:3 * n_in]
    x_refs = refs[3 * n_in:-11]
    m_ref, g_ref, wr_ref, br_ref, tri_ref, xn_ref, xf_ref, ri_ref, rw_ref, cnt_ref, base_scr = refs[-11:]
    i = pl.program_id(0)

    @pl.when(i == 0)
    def _():
        base_scr[...] = jnp.zeros_like(base_scr)
        cnt_ref[...] = jnp.zeros_like(cnt_ref)

    rows = [slice(t * TM // ROUTE_SPLIT, (t + 1) * TM // ROUTE_SPLIT) for t in range(ROUTE_SPLIT)]
    is_ctx = pl.program_id(0) < NC // TM
    pick = lambda pair, r: jnp.where(is_ctx, pair[0][r, :], pair[1][r, :]) if len(pair) == 2 else pair[0][r, :]
    acc = [sum(_mm(pick(a_refs[2 * j:2 * j + 2], r), w_ref[...]) for j, w_ref in enumerate(w_refs)) for r in rows]
    x_new = [pick(x_refs, r) + m_ref[2:3, :] * a for r, a in zip(rows, acc)]
    for r, xn in zip(rows, x_new):
        xn_ref[r, :] = xn
    h = [_modulated_norm(xn, m_ref, g_ref, 3, 4) for xn in x_new]
    for r, hh in zip(rows, h):
        xf_ref[r, :] = _pack_bf16_pairs(hh[:, :D // 2], hh[:, D // 2:])
    logits = jnp.concatenate(
        [lax.dot_general(wr_ref[...], hh, (((1,), (1,)), ((), ())), preferred_element_type=F32, precision=HIGHEST)
         for hh in h], axis=1) + br_ref[:, 0:1]
    row = lax.broadcasted_iota(jnp.int32, logits.shape, 0)
    cmax = lambda x: jnp.max(x, axis=0, keepdims=True)
    cmin = lambda x: jnp.min(x, axis=0, keepdims=True)
    csum = lambda x: jnp.sum(x, axis=0, keepdims=True)

    gmask = row < N_GROUPS
    mg = cmax(jnp.where(gmask, logits, NEG))
    eg = jnp.where(gmask, jnp.exp(jnp.where(gmask, logits - mg, NEG)), 0.0)
    pg = eg / csum(eg)
    p_grp = cmax(pg)
    grp = cmin(jnp.where(jnp.logical_and(gmask, pg == p_grp), row, ROUTE_ROWS))
    lo = N_GROUPS + grp * EXP_PER_GROUP
    emask = jnp.logical_and(row >= lo, row < lo + EXP_PER_GROUP)
    me = cmax(jnp.where(emask, logits, NEG))
    ee = jnp.where(emask, jnp.exp(jnp.where(emask, logits - me, NEG)), 0.0)
    pe = ee / csum(ee)
    p1 = cmax(pe)
    i1 = cmin(jnp.where(jnp.logical_and(emask, pe == p1), row, ROUTE_ROWS))
    m2 = jnp.logical_and(emask, row != i1)
    p2 = cmax(jnp.where(m2, pe, -1.0))
    i2 = cmin(jnp.where(jnp.logical_and(m2, pe == p2), row, ROUTE_ROWS))
    den = p1 + p2
    w1 = p_grp * p1 / den
    w2 = p_grp * p2 / den

    sel1 = row == i1
    sel2 = row == i2
    oh = jnp.where(jnp.logical_or(sel1, sel2), 1.0, 0.0).astype(BF16)
    before = jnp.dot(oh, tri_ref[...], preferred_element_type=F32) + base_scr[:, 0:1]
    rank1 = csum(jnp.where(sel1, before, 0.0))
    rank2 = csum(jnp.where(sel2, before, 0.0))
    base_scr[...] = base_scr[...] + jnp.sum(oh.astype(F32), axis=1, keepdims=True)
    cnt_ref[...] = cnt_ref[...] + lax.dot_general(jnp.ones((8, TM), BF16), oh, (((1,), (1,)), ((), ())),
                                                  preferred_element_type=F32)
    sub = lax.broadcasted_iota(jnp.int32, (8, TM), 0)
    ri_ref[...] = jnp.where(sub == 0, i1 - N_GROUPS, jnp.where(sub == 1, i2 - N_GROUPS,
                  jnp.where(sub == 2, rank1.astype(jnp.int32), jnp.where(sub == 3, rank2.astype(jnp.int32), 0))))
    rw_ref[...] = jnp.where(sub == 0, w1, jnp.where(sub == 1, w2, 0.0))


def _router(a_pairs, w_list, x, m3, gain, wr_t, br_t, tri_upper):
    in_specs, args = [], []
    for pair in a_pairs:
        specs, ops = _token_specs(pair)
        in_specs += specs
        args += ops
    x_specs, x_args = _token_specs(x)
    in_specs += ([pl.BlockSpec(w.shape, lambda i: (0, 0)) for w in w_list] + x_specs
                 + [pl.BlockSpec((None, 6, D), lambda i: (_cond_row(i), 0, 0)),
                    pl.BlockSpec((1, D), lambda i: (0, 0)),
                    pl.BlockSpec((ROUTE_ROWS, D), lambda i: (0, 0)),
                    pl.BlockSpec((ROUTE_ROWS, LANES), lambda i: (0, 0)),
                    pl.BlockSpec((TM, TM), lambda i: (0, 0))])
    return pl.pallas_call(
        functools.partial(_router_kernel, n_in=len(a_pairs)),
        grid=(N // TM,),
        in_specs=in_specs,
        out_specs=[pl.BlockSpec((TM, D), lambda i: (i, 0)),
                   pl.BlockSpec((TM, D // 2), lambda i: (i, 0)),
                   pl.BlockSpec((8, TM), lambda i: (0, i)),
                   pl.BlockSpec((8, TM), lambda i: (0, i)),
                   pl.BlockSpec((8, ROUTE_ROWS), lambda i: (0, 0))],
        out_shape=[jax.ShapeDtypeStruct((N, D), F32),
                   jax.ShapeDtypeStruct((N, D // 2), jnp.int32),
                   jax.ShapeDtypeStruct((8, N), jnp.int32),
                   jax.ShapeDtypeStruct((8, N), F32),
                   jax.ShapeDtypeStruct((8, ROUTE_ROWS), F32)],
        scratch_shapes=[pltpu.VMEM((ROUTE_ROWS, LANES), F32)],
        compiler_params=_cparams(("arbitrary",)),
        name="router",
    )(*args, *w_list, *x_args, m3, gain, wr_t, br_t, tri_upper)


def _row_copy(src_hbm, row, dst, r, sem):
    return pltpu.make_async_copy(src_hbm.at[pl.ds(row, 1)], dst.at[pl.ds(r, 1)], sem)


def _expert_kernel(be_ref, nu_ref, d_ref, pad_ref, xp_hbm, wg_ref, wu_ref, wd_ref, ys_ref, xres, xbuf, st_ref, sem):
    del be_ref
    b = pl.program_id(0)
    half = D // 2

    def gather(blk, slot, part=0, parts=1):
        base = blk * MOE_ROWS
        for r in range(part * MOE_ROWS // parts, (part + 1) * MOE_ROWS // parts):
            xbuf[slot, pl.ds(r, 1), :] = xres[pl.ds(st_ref[base + r], 1), :]

    @pl.when(b == 0)
    def _():
        cp = pltpu.make_async_copy(xp_hbm, xres, sem.at[0])
        cp.start()

        def clear_tail(e, c):
            def clear(s_, c2):
                st_ref[s_] = 0
                return c2

            return lax.fori_loop(pad_ref[2 * e], pad_ref[2 * e + 1], clear, c)

        lax.fori_loop(0, N_EXPERTS + 1, clear_tail, 0)

        def invert(j, c):
            slots = [d_ref[16 * j + t] for t in range(16)]
            for t in range(16):
                st_ref[slots[t]] = 8 * j + t // 2
            return c

        lax.fori_loop(0, 2 * N // 16, invert, 0)
        cp.wait()
        gather(0, 0)

    @pl.when(b < nu_ref[0])
    def _():
        nxt = (jnp.minimum(b + 1, N_SLOT_BLOCKS - 1), (b + 1) % 2)
        x_lo, x_hi = _unpack_bf16_pairs(xbuf[b % 2])
        dot = functools.partial(jnp.dot, preferred_element_type=F32)
        gather(*nxt, 0, 8)
        g = dot(x_lo, wg_ref[:half, :].astype(BF16))
        gather(*nxt, 1, 8)
        g = g + dot(x_hi, wg_ref[half:, :].astype(BF16))
        gather(*nxt, 2, 8)
        u = dot(x_lo, wu_ref[:half, :].astype(BF16))
        gather(*nxt, 3, 8)
        u = u + dot(x_hi, wu_ref[half:, :].astype(BF16))
        hb = (_silu(g) * u).astype(BF16)
        quarter = D // 4
        for j in range(4):
            gather(*nxt, 4 + j, 8)
            cols = slice(j * quarter, (j + 1) * quarter)
            ys_ref[:, cols] = jnp.dot(hb, wd_ref[:, cols].astype(BF16), preferred_element_type=F32)

    @pl.when(b >= nu_ref[0])
    def _():
        ys_ref[...] = jnp.zeros_like(ys_ref)


def _experts(block_e, n_used, dest, pad_ranges, xp, w_gate, w_up, w_down, layer):
    grid_spec = pltpu.PrefetchScalarGridSpec(
        num_scalar_prefetch=4,
        grid=(N_SLOT_BLOCKS,),
        in_specs=[pl.BlockSpec(memory_space=pl.ANY),
                  pl.BlockSpec((None, None, D, D_EXPERT), lambda b, be, nu, st, pd: (layer, be[b], 0, 0)),
                  pl.BlockSpec((None, None, D, D_EXPERT), lambda b, be, nu, st, pd: (layer, be[b], 0, 0)),
                  pl.BlockSpec((None, None, D_EXPERT, D), lambda b, be, nu, st, pd: (layer, be[b], 0, 0))],
        out_specs=pl.BlockSpec((MOE_ROWS, D), lambda b, be, nu, st, pd: (b, 0)),
        scratch_shapes=[pltpu.VMEM((N, D // 2), jnp.int32),
                        pltpu.VMEM((2, MOE_ROWS, D // 2), jnp.int32),
                        pltpu.SMEM((N_SLOT_BLOCKS * MOE_ROWS,), jnp.int32),
                        pltpu.SemaphoreType.DMA((1,))])
    return pl.pallas_call(
        _expert_kernel,
        grid_spec=grid_spec,
        out_shape=jax.ShapeDtypeStruct((N_SLOT_BLOCKS * MOE_ROWS, D), F32),
        compiler_params=pltpu.CompilerParams(dimension_semantics=("arbitrary",), vmem_limit_bytes=EXPERT_VMEM_LIMIT),
        name="experts",
    )(block_e, n_used, dest, pad_ranges, xp, w_gate, w_up, w_down)


def _combine_kernel(d_ref, ys_hbm, x_ref, m_ref, rw_ref, fn_ref, o_ref, buf, sem, *, final, tile0, n_tiles):
    i = pl.program_id(0)

    def issue(tile, slot):
        base = (tile0 + tile) * (2 * TM)
        for r in range(TM):
            for kk in range(2):
                _row_copy(ys_hbm, d_ref[base + 2 * r + kk], buf.at[slot, kk], r, sem.at[slot]).start(priority=kk)

    @pl.when(i == 0)
    def _():
        issue(0, 0)

    @pl.when(i + 1 < n_tiles)
    def _():
        issue(i + 1, (i + 1) % 2)

    slot = i % 2
    for kk in range(2):
        pltpu.make_async_copy(ys_hbm.at[pl.ds(0, TM)], buf.at[slot, kk], sem.at[slot]).wait()
    w = rw_ref[...]
    y = w[:, 0:1] * buf[slot, 0] + w[:, 1:2] * buf[slot, 1]
    out = x_ref[...] + m_ref[5:6, :] * y
    if final:
        ms = jnp.mean(out * out, axis=-1, keepdims=True)
        out = out * lax.rsqrt(ms + EPS) * fn_ref[...]
    o_ref[...] = out


def _combine(dest_flat, ys, x, m3, rw, final_norm, final, tile0=0, n_tiles=N // TM):
    grid_spec = pltpu.PrefetchScalarGridSpec(
        num_scalar_prefetch=1,
        grid=(n_tiles,),
        in_specs=[pl.BlockSpec(memory_space=pl.ANY),
                  pl.BlockSpec((TM, D), lambda i, d: (tile0 + i, 0)),
                  pl.BlockSpec((None, 6, D), lambda i, d: (_cond_row(tile0 + i), 0, 0)),
                  pl.BlockSpec((TM, 2), lambda i, d: (tile0 + i, 0)),
                  pl.BlockSpec((1, D), lambda i, d: (0, 0))],
        out_specs=pl.BlockSpec((TM, D), lambda i, d: (i, 0)),
        scratch_shapes=[pltpu.VMEM((2, 2, TM, D), F32), pltpu.SemaphoreType.DMA((2,))])
    return pl.pallas_call(
        functools.partial(_combine_kernel, final=final, tile0=tile0, n_tiles=n_tiles),
        grid_spec=grid_spec,
        out_shape=jax.ShapeDtypeStruct((n_tiles * TM, D), F32),
        compiler_params=_cparams(("arbitrary",)),
        name="combine",
    )(dest_flat, ys, x, m3, rw, final_norm)


def _combine_proj_kernel(d_ref, ys_hbm, x_ref, m_ref, rw_ref, mn_ref, gn_ref, w_ref, xo_ref, o_ref, buf, sem, *,
                         n_chunk):
    i = pl.program_id(0)
    n_tiles = N // TM
    n_parts = o_ref.shape[1] // n_chunk

    def issue(tile, slot, part=0, parts=1):
        base = tile * (2 * TM)
        for r in range(part * TM // parts, (part + 1) * TM // parts):
            for kk in range(2):
                _row_copy(ys_hbm, d_ref[base + 2 * r + kk], buf.at[slot, kk], r, sem.at[slot]).start(priority=kk)

    def wait(slot):
        for kk in range(2):
            pltpu.make_async_copy(ys_hbm.at[pl.ds(0, TM)], buf.at[slot, kk], sem.at[slot]).wait()

    @pl.when(i == 0)
    def _():
        issue(0, 0)

    slot = i % 2
    wait(slot)
    w = rw_ref[...]
    x_new = x_ref[...] + m_ref[5:6, :] * (w[:, 0:1] * buf[slot, 0] + w[:, 1:2] * buf[slot, 1])
    xo_ref[...] = x_new
    hb = _modulated_norm(x_new, mn_ref, gn_ref, 0, 1).astype(BF16)
    nxt = jnp.minimum(i + 1, n_tiles - 1)
    for j in range(n_parts):
        issue(nxt, 1 - slot, j, n_parts)
        sl = slice(j * n_chunk, (j + 1) * n_chunk)
        o_ref[:, sl] = jnp.dot(hb, w_ref[:, sl], preferred_element_type=F32)

    @pl.when(i == n_tiles - 1)
    def _():
        wait(1 - slot)


def _combine_proj(dest_flat, ys, x, m3, rw, m3_next, gain_next, w_bf16, n_chunk):
    nout = w_bf16.shape[1]
    grid_spec = pltpu.PrefetchScalarGridSpec(
        num_scalar_prefetch=1,
        grid=(N // TM,),
        in_specs=[pl.BlockSpec(memory_space=pl.ANY),
                  pl.BlockSpec((TM, D), lambda i, d: (i, 0)),
                  pl.BlockSpec((None, 6, D), lambda i, d: (_cond_row(i), 0, 0)),
                  pl.BlockSpec((TM, 2), lambda i, d: (i, 0)),
                  pl.BlockSpec((None, 6, D), lambda i, d: (_cond_row(i), 0, 0)),
                  pl.BlockSpec((1, D), lambda i, d: (0, 0)),
                  pl.BlockSpec((D, nout), lambda i, d: (0, 0))],
        out_specs=[pl.BlockSpec((TM, D), lambda i, d: (i, 0)),
                   pl.BlockSpec((TM, nout), lambda i, d: (i, 0))],
        scratch_shapes=[pltpu.VMEM((2, 2, TM, D), F32), pltpu.SemaphoreType.DMA((2,))])
    return pl.pallas_call(
        functools.partial(_combine_proj_kernel, n_chunk=n_chunk),
        grid_spec=grid_spec,
        out_shape=[jax.ShapeDtypeStruct((N, D), F32), jax.ShapeDtypeStruct((N, nout), F32)],
        compiler_params=_cparams(("arbitrary",)),
        name="combine_proj",
    )(dest_flat, ys, x, m3, rw, m3_next, gain_next, w_bf16)


def _hier_moe(a_pairs, w_list, x, m3, gain, layer, w_rg, b_rg, w_re, b_re, w_gate, w_up, w_down, tri_tm,
              final_norm, final, next_proj=None):
    pad = ROUTE_ROWS - N_GROUPS - N_EXPERTS
    wr_t = jnp.concatenate([w_rg.T, w_re.transpose(0, 2, 1).reshape(N_EXPERTS, D), jnp.zeros((pad, D), F32)], axis=0)
    br_t = jnp.broadcast_to(jnp.concatenate([b_rg, b_re.reshape(N_EXPERTS), jnp.zeros((pad,), F32)])[:, None],
                            (ROUTE_ROWS, LANES))
    x, xf, ri_t, rw_t, cnt = _router(a_pairs, w_list, x, m3, gain, wr_t, br_t, tri_tm)
    e_idx = ri_t[0:2].T
    rank = ri_t[2:4].T
    rw = rw_t[0:2].T
    counts = cnt[0, N_GROUPS:N_GROUPS + N_EXPERTS].astype(jnp.int32)
    padded = (counts + MOE_ROWS - 1) // MOE_ROWS * MOE_ROWS
    end_pad = jnp.cumsum(padded)
    start_pad = end_pad - padded
    experts = jnp.arange(N_EXPERTS, dtype=jnp.int32)
    start_of = jnp.sum(jnp.where(e_idx[:, :, None] == experts, start_pad, 0), axis=-1)
    dest = (start_of + rank).reshape(-1).astype(jnp.int32)
    block_start = jnp.arange(N_SLOT_BLOCKS, dtype=jnp.int32) * MOE_ROWS
    block_e = jnp.minimum(jnp.sum((end_pad[None, :] <= block_start[:, None]).astype(jnp.int32), axis=1),
                          N_EXPERTS - 1)
    n_used = (end_pad[-1:] // MOE_ROWS).astype(jnp.int32)
    tail = jnp.stack([end_pad[-1], jnp.minimum(end_pad[-1] + MOE_ROWS, N_SLOT_BLOCKS * MOE_ROWS)])
    pad_ranges = jnp.concatenate([jnp.stack([start_pad + counts, end_pad], axis=1).reshape(-1), tail]).astype(jnp.int32)
    ys = _experts(block_e, n_used, dest, pad_ranges, xf, w_gate, w_up, w_down, layer)
    if not final:
        return _combine_proj(dest, ys, x, m3, rw, *next_proj)
    nct = NC // TM
    return (_combine(dest, ys, x, m3, rw, final_norm, True, 0, nct),
            _combine(dest, ys, x, m3, rw, final_norm, True, nct, N // TM - nct))


def kernel(x_prompt, x_sample, state_A_fwd, state_A_bwd, cache_k, cache_v, c, c_ctx, mod_w, mod_b, norm_mix, norm_ffn, ab_w_in, ab_conv, ab_a_log, ab_dt_bias, ab_o_gain, ab_w_out, na_w_qkv, na_rpb, na_w_out, moe_w_rg, moe_b_rg, moe_w_re, moe_b_re, moe_w_gate, moe_w_up, moe_w_down, final_norm):
    x = (x_prompt.reshape(NC, D), x_sample.reshape(NL, D))
    cond8 = jnp.concatenate([c_ctx[None, :], c, jnp.zeros((8 - 1 - DEC_BATCH, D), F32)], axis=0)
    mods = _ada_params(cond8, mod_w, mod_b).reshape(DEPTH, 8, 6, D)

    tri_tm = jnp.asarray(np.triu(np.ones((TM, TM)), 1), BF16)
    tri_c = jnp.asarray(np.tril(np.ones((CHUNK, CHUNK))), F32)
    half = np.arange(LANES) < A_DH
    jbd = jnp.asarray((half[:, None] == half[None, :]).astype(np.float32))
    fn = final_norm[None, :]

    m3 = mods[0]
    w_in = ab_w_in[0]
    w_in = jnp.concatenate([w_in[:, :2048], w_in[:, 2080:2592], w_in[:, 2048:2080],
                            jnp.zeros((D, AB_COLS - 2592), F32)], axis=1).astype(BF16)
    proj = _modproj(x, m3, norm_mix[0][None, :], w_in, 896)
    gate_p = jnp.broadcast_to(jnp.stack([ab_a_log[0].reshape(-1), ab_dt_bias[0].reshape(-1)])[:, :, None],
                              (2, 2 * A_HEADS, LANES))
    o_gain2 = jnp.tile(ab_o_gain[0], 2)[None, :]
    zeros_state = jnp.zeros((BATCH, A_HEADS, A_DH, A_DH), F32)
    mix_a_c, s_f, s_b = _deltanet(proj, ab_conv[0], gate_p, o_gain2, zeros_state, zeros_state, tri_c, jbd,
                                  seq_len=SEQ, n_seq=BATCH, n_sub=4, row_blk0=0)
    mix_a_l, _, _ = _deltanet(proj, ab_conv[0], gate_p, o_gain2, state_A_fwd[:, 0], state_A_bwd[:, 0], tri_c, jbd,
                              seq_len=DEC_SEQ, n_seq=DEC_BATCH, n_sub=1, row_blk0=NC // DEC_SEQ)

    cc, sc = _dft_mats(LANES)
    ct, st = _dft_mats(SEQ)
    c64, s64 = _dft_mats(64)
    cs = jnp.asarray(np.concatenate([cc, sc], axis=1), BF16)
    dft = jnp.asarray(np.concatenate([ct, -st], axis=1), BF16)
    ca = jnp.asarray(np.concatenate([cc, -sc, -sc, -cc], axis=1), BF16)
    m1 = jnp.asarray(np.concatenate([c64, s64], axis=1), BF16)
    tw_idx = np.arange(64)
    tw_ang = 2.0 * np.pi * (tw_idx[:, None] * tw_idx[None, :]) / DEC_SEQ
    twc = jnp.broadcast_to(jnp.asarray(np.cos(tw_ang), F32)[:, :, None], (64, 64, LANES))
    tws = jnp.broadcast_to(jnp.asarray(np.sin(tw_ang), F32)[:, :, None], (64, 64, LANES))
    mix_b_c = _fnet_ctx(proj, cs, dft)
    mix_b_l = _fnet_latent(proj, ca, m1, twc, tws)

    w_out = ab_w_out[0].astype(BF16)
    x, qkv = _hier_moe([(mix_a_c, mix_a_l), (mix_b_c, mix_b_l)], [w_out[:A_WIDTH], w_out[A_WIDTH:]], x, m3,
                       norm_ffn[0][None, :], 0, moe_w_rg[0], moe_b_rg[0], moe_w_re[0], moe_b_re[0],
                       moe_w_gate, moe_w_up, moe_w_down, tri_tm, fn, False,
                       next_proj=(mods[1], norm_mix[1][None, :], na_w_qkv[0].astype(BF16), 512))

    m3 = mods[1]
    attn_c = _na_ctx(qkv)
    attn_l = _na_latent(qkv, cache_k[:, 0].reshape(DEC_BATCH, 256, D), cache_v[:, 0].reshape(DEC_BATCH, 256, D),
                        _rpb_tables(na_rpb[0]))
    y_c, y_l = _hier_moe([(attn_c, attn_l)], [na_w_out[0].astype(BF16)], x, m3, norm_ffn[1][None, :], 1, moe_w_rg[1], moe_b_rg[1], moe_w_re[1], moe_b_re[1],
                         moe_w_gate, moe_w_up, moe_w_down, tri_tm, fn, True)

    new_k = qkv[:NC, D:2 * D].reshape(BATCH, 1, SEQ, NA_HEADS, NA_DH)
    new_v = qkv[:NC, 2 * D:].reshape(BATCH, 1, SEQ, NA_HEADS, NA_DH)
    return (y_c.reshape(BATCH, SEQ, D), y_l.reshape(DEC_BATCH, DEC_SEQ, D),
            s_f[:, None], s_b[:, None], new_k, new_v)
```
